```python
import jax
import jax.numpy as jnp
from jax import lax
import numpy as np

D_MODEL = 1024
BATCH = 8
SEQ = 8192
DEPTH = 2

CHUNK = 64
D_FF = 2816
D_CONV = 512
CONV_WIDTH = 31
GLA_HEADS = 4
GLA_DK = 64
GLA_DV = 128
D_GLA = GLA_HEADS * GLA_DV
D_MIX = D_CONV + D_GLA
GATE_RANK = 16
GATE_TAU = 16.0
N_MOD = 9
EPS = 1e-6
SPLITS = (D_CONV, 2 * D_CONV, 2 * D_CONV + GLA_HEADS * GLA_DK, 2 * D_CONV + 2 * GLA_HEADS * GLA_DK, 2 * D_CONV + 2 * GLA_HEADS * GLA_DK + D_GLA, 2 * D_CONV + 2 * GLA_HEADS * GLA_DK + 2 * D_GLA)
D_IN = SPLITS[-1] + GATE_RANK

kernel_name = 'hybrid_conformer_gla_macaron'


def rmsnorm(x, g):
    xf = x.astype(jnp.float32)
    y = xf * lax.rsqrt(jnp.mean(xf * xf, axis=-1, keepdims=True) + EPS)
    return (y * g.astype(jnp.float32)).astype(x.dtype)


def layernorm(x, g, b):
    xf = x.astype(jnp.float32)
    xc = xf - jnp.mean(xf, axis=-1, keepdims=True)
    y = xc * lax.rsqrt(jnp.mean(xc * xc, axis=-1, keepdims=True) + EPS)
    return (y * g.astype(jnp.float32) + b.astype(jnp.float32)).astype(x.dtype)


def modulate(h, shift, scale):
    return h * (1 + scale) + shift


def swiglu(h, w_in, w_out):
    gate, up = jnp.split(h @ w_in, 2, axis=-1)
    return (jax.nn.silu(gate) * up) @ w_out


def conformer_conv(a, b, w_dw, b_dw, g_ln, b_ln):
    u = a * jax.nn.sigmoid(b)
    u = jnp.pad(u, ((0, 0), (CONV_WIDTH - 1, 0), (0, 0)))
    y = lax.conv_general_dilated(u, w_dw.astype(u.dtype)[:, None, :], window_strides=(1,), padding='VALID', dimension_numbers=('NWC', 'WIO', 'NWC'), feature_group_count=D_CONV)
    return jax.nn.silu(layernorm(y + b_dw, g_ln, b_ln))


def gla(q, k, v, r, glr, w_gate_up, b_gate, g_norm):
    bsz, seq, _ = q.shape
    n = seq // CHUNK
    f32 = jnp.float32

    def heads(t, d):
        return t.astype(f32).reshape(bsz, n, CHUNK, GLA_HEADS, d).transpose(0, 3, 1, 2, 4)

    log_a = jax.nn.log_sigmoid((glr @ w_gate_up + b_gate).astype(f32)) / GATE_TAU
    qh = heads(q, GLA_DK) * (GLA_DK ** -0.5)
    kh = heads(k, GLA_DK)
    vh = heads(v, GLA_DV)
    bc = jnp.cumsum(heads(log_a, GLA_DK), axis=3)
    b_end = bc[:, :, :, -1:, :]
    q_fwd = qh * jnp.exp(bc)
    att_fwd = jnp.einsum('bhnik,bhnjk->bhnij', q_fwd, kh * jnp.exp(-bc))
    att_bwd = jnp.einsum('bhnik,bhnjk->bhnij', qh * jnp.exp(-bc), kh * jnp.exp(bc))
    tri = jnp.tril(jnp.ones((CHUNK, CHUNK), dtype=bool))
    o = jnp.einsum('bhnij,bhnjv->bhniv', jnp.where(tri, att_fwd, att_bwd), vh)
    u = jnp.einsum('bhnjk,bhnjv->bhnkv', kh * jnp.exp(b_end - bc), vh)
    g = jnp.exp(b_end[:, :, :, 0, :])

    def step(state, inp):
        g_c, u_c = inp
        return g_c[..., None] * state + u_c, state

    s0 = jnp.zeros((bsz, GLA_HEADS, GLA_DK, GLA_DV), f32)
    _, s_prev = lax.scan(step, s0, (jnp.moveaxis(g, 2, 0), jnp.moveaxis(u, 2, 0)))
    o = o + jnp.einsum('bhnik,nbhkv->bhniv', q_fwd, s_prev)
    o = o * lax.rsqrt(jnp.mean(o * o, axis=-1, keepdims=True) + EPS) * g_norm.astype(f32)[:, None, None, :]
    o = o.transpose(0, 2, 3, 1, 4).reshape(bsz, seq, D_GLA)
    return (o * jax.nn.silu(r.astype(f32))).astype(q.dtype)


def _fwd_setup_inputs(seed: int = 0) -> dict:
    key = jax.random.key(seed)
    ks = jax.random.split(key, 26)
    L = DEPTH

    def nrm(k, shape, scale):
        return jax.random.normal(k, shape, jnp.float32) * scale

    return {
        'x': nrm(ks[0], (BATCH, SEQ, D_MODEL), 1.0),
        'c': nrm(ks[1], (BATCH, D_MODEL), 1.0),
        'w_ada': nrm(ks[2], (L, D_MODEL, N_MOD * D_MODEL), 0.5 * D_MODEL ** -0.5),
        'b_ada': nrm(ks[3], (L, N_MOD * D_MODEL), 0.02),
        'g_norm_ffn1': 1.0 + nrm(ks[4], (L, D_MODEL), 0.02),
        'w_ffn1_in': nrm(ks[5], (L, D_MODEL, 2 * D_FF), D_MODEL ** -0.5),
        'w_ffn1_out': nrm(ks[6], (L, D_FF, D_MODEL), D_FF ** -0.5),
        'g_norm_mix': 1.0 + nrm(ks[7], (L, D_MODEL), 0.02),
        'w_in': nrm(ks[8], (L, D_MODEL, D_IN), D_MODEL ** -0.5),
        'w_dw': nrm(ks[9], (L, CONV_WIDTH, D_CONV), CONV_WIDTH ** -0.5),
        'b_dw': nrm(ks[10], (L, D_CONV), 0.02),
        'g_conv_ln': 1.0 + nrm(ks[11], (L, D_CONV), 0.02),
        'b_conv_ln': nrm(ks[12], (L, D_CONV), 0.02),
        'w_gate_up': nrm(ks[13], (L, GATE_RANK, GLA_HEADS * GLA_DK), GATE_RANK ** -0.5),
        'b_gate': nrm(ks[14], (L, GLA_HEADS * GLA_DK), 0.02),
        'g_gla_norm': 1.0 + nrm(ks[15], (L, GLA_HEADS, GLA_DV), 0.02),
        'w_out': nrm(ks[16], (L, D_MIX, D_MODEL), D_MIX ** -0.5),
        'g_norm_ffn2': 1.0 + nrm(ks[17], (L, D_MODEL), 0.02),
        'w_ffn2_in': nrm(ks[18], (L, D_MODEL, 2 * D_FF), D_MODEL ** -0.5),
        'w_ffn2_out': nrm(ks[19], (L, D_FF, D_MODEL), D_FF ** -0.5),
        'g_norm_final': 1.0 + nrm(ks[20], (D_MODEL,), 0.02),
        'w_ada_final': nrm(ks[21], (D_MODEL, 2 * D_MODEL), 0.5 * D_MODEL ** -0.5),
        'b_ada_final': nrm(ks[22], (2 * D_MODEL,), 0.02),
    }


def _fwd_reference(x, c, w_ada, b_ada, g_norm_ffn1, w_ffn1_in, w_ffn1_out, g_norm_mix, w_in, w_dw, b_dw, g_conv_ln, b_conv_ln, w_gate_up, b_gate, g_gla_norm, w_out, g_norm_ffn2, w_ffn2_in, w_ffn2_out, g_norm_final, w_ada_final, b_ada_final):
    bsz = x.shape[0]
    c_act = jax.nn.silu(c)
    for l in range(DEPTH):
        mod = (c_act @ w_ada[l] + b_ada[l]).reshape(bsz, N_MOD, 1, D_MODEL)
        h = modulate(rmsnorm(x, g_norm_ffn1[l]), mod[:, 0], mod[:, 1])
        x = x + 0.5 * mod[:, 2] * swiglu(h, w_ffn1_in[l], w_ffn1_out[l])
        h = modulate(rmsnorm(x, g_norm_mix[l]), mod[:, 3], mod[:, 4])
        a, b, q, k, v, r, glr = jnp.split(h @ w_in[l], SPLITS, axis=-1)
        y_conv = conformer_conv(a, b, w_dw[l], b_dw[l], g_conv_ln[l], b_conv_ln[l])
        y_gla = gla(q, k, v, r, glr, w_gate_up[l], b_gate[l], g_gla_norm[l])
        x = x + mod[:, 5] * (jnp.concatenate([y_conv, y_gla], axis=-1) @ w_out[l])
        h = modulate(rmsnorm(x, g_norm_ffn2[l]), mod[:, 6], mod[:, 7])
        x = x + 0.5 * mod[:, 8] * swiglu(h, w_ffn2_in[l], w_ffn2_out[l])
    fmod = (c_act @ w_ada_final + b_ada_final).reshape(bsz, 2, 1, D_MODEL)
    return modulate(rmsnorm(x, g_norm_final), fmod[:, 0], fmod[:, 1])


import jax as _jax
import jax.numpy as _jnp

TWIN_FORMAT = 'train_step'
FWD_PARAMS = ['x', 'c', 'w_ada', 'b_ada', 'g_norm_ffn1', 'w_ffn1_in', 'w_ffn1_out', 'g_norm_mix', 'w_in', 'w_dw', 'b_dw', 'g_conv_ln', 'b_conv_ln', 'w_gate_up', 'b_gate', 'g_gla_norm', 'w_out', 'g_norm_ffn2', 'w_ffn2_in', 'w_ffn2_out', 'g_norm_final', 'w_ada_final', 'b_ada_final']
TWIN_WEIGHTS = ['w_ada', 'b_ada', 'g_norm_ffn1', 'w_ffn1_in', 'w_ffn1_out', 'g_norm_mix', 'w_in', 'w_dw', 'b_dw', 'g_conv_ln', 'b_conv_ln', 'w_gate_up', 'b_gate', 'g_gla_norm', 'w_out', 'g_norm_ffn2', 'w_ffn2_in', 'w_ffn2_out', 'g_norm_final', 'w_ada_final', 'b_ada_final']
TWIN_DIFF_INPUT = 'x'
TWIN_INPUTS = ['x', 'c', 'w_ada', 'b_ada', 'g_norm_ffn1', 'w_ffn1_in', 'w_ffn1_out', 'g_norm_mix', 'w_in', 'w_dw', 'b_dw', 'g_conv_ln', 'b_conv_ln', 'w_gate_up', 'b_gate', 'g_gla_norm', 'w_out', 'g_norm_ffn2', 'w_ffn2_in', 'w_ffn2_out', 'g_norm_final', 'w_ada_final', 'b_ada_final', 'loss_target', 'm_w_ada', 'm_b_ada', 'm_g_norm_ffn1', 'm_w_ffn1_in', 'm_w_ffn1_out', 'm_g_norm_mix', 'm_w_in', 'm_w_dw', 'm_b_dw', 'm_g_conv_ln', 'm_b_conv_ln', 'm_w_gate_up', 'm_b_gate', 'm_g_gla_norm', 'm_w_out', 'm_g_norm_ffn2', 'm_w_ffn2_in', 'm_w_ffn2_out', 'm_g_norm_final', 'm_w_ada_final', 'm_b_ada_final', 'v_w_ada', 'v_b_ada', 'v_g_norm_ffn1', 'v_w_ffn1_in', 'v_w_ffn1_out', 'v_g_norm_mix', 'v_w_in', 'v_w_dw', 'v_b_dw', 'v_g_conv_ln', 'v_b_conv_ln', 'v_w_gate_up', 'v_b_gate', 'v_g_gla_norm', 'v_w_out', 'v_g_norm_ffn2', 'v_w_ffn2_in', 'v_w_ffn2_out', 'v_g_norm_final', 'v_w_ada_final', 'v_b_ada_final']
TWIN_OUTPUTS = ['loss', 'grad_x', 'grad_w_ada', 'grad_b_ada', 'grad_g_norm_ffn1', 'grad_w_ffn1_in', 'grad_w_ffn1_out', 'grad_g_norm_mix', 'grad_w_in', 'grad_w_dw', 'grad_b_dw', 'grad_g_conv_ln', 'grad_b_conv_ln', 'grad_w_gate_up', 'grad_b_gate', 'grad_g_gla_norm', 'grad_w_out', 'grad_g_norm_ffn2', 'grad_w_ffn2_in', 'grad_w_ffn2_out', 'grad_g_norm_final', 'grad_w_ada_final', 'grad_b_ada_final', 'delta_w_ada', 'delta_b_ada', 'delta_g_norm_ffn1', 'delta_w_ffn1_in', 'delta_w_ffn1_out', 'delta_g_norm_mix', 'delta_w_in', 'delta_w_dw', 'delta_b_dw', 'delta_g_conv_ln', 'delta_b_conv_ln', 'delta_w_gate_up', 'delta_b_gate', 'delta_g_gla_norm', 'delta_w_out', 'delta_g_norm_ffn2', 'delta_w_ffn2_in', 'delta_w_ffn2_out', 'delta_g_norm_final', 'delta_w_ada_final', 'delta_b_ada_final', 'new_m_w_ada', 'new_m_b_ada', 'new_m_g_norm_ffn1', 'new_m_w_ffn1_in', 'new_m_w_ffn1_out', 'new_m_g_norm_mix', 'new_m_w_in', 'new_m_w_dw', 'new_m_b_dw', 'new_m_g_conv_ln', 'new_m_b_conv_ln', 'new_m_w_gate_up', 'new_m_b_gate', 'new_m_g_gla_norm', 'new_m_w_out', 'new_m_g_norm_ffn2', 'new_m_w_ffn2_in', 'new_m_w_ffn2_out', 'new_m_g_norm_final', 'new_m_w_ada_final', 'new_m_b_ada_final', 'new_v_w_ada', 'new_v_b_ada', 'new_v_g_norm_ffn1', 'new_v_w_ffn1_in', 'new_v_w_ffn1_out', 'new_v_g_norm_mix', 'new_v_w_in', 'new_v_w_dw', 'new_v_b_dw', 'new_v_g_conv_ln', 'new_v_b_conv_ln', 'new_v_w_gate_up', 'new_v_b_gate', 'new_v_g_gla_norm', 'new_v_w_out', 'new_v_g_norm_ffn2', 'new_v_w_ffn2_in', 'new_v_w_ffn2_out', 'new_v_g_norm_final', 'new_v_w_ada_final', 'new_v_b_ada_final']
TWIN_LEAF_KINDS = {'loss': 'loss', 'grad_x': 'grad_x', 'grad_w_ada': 'grad_w', 'grad_b_ada': 'grad_w', 'grad_g_norm_ffn1': 'grad_w', 'grad_w_ffn1_in': 'grad_w', 'grad_w_ffn1_out': 'grad_w', 'grad_g_norm_mix': 'grad_w', 'grad_w_in': 'grad_w', 'grad_w_dw': 'grad_w', 'grad_b_dw': 'grad_w', 'grad_g_conv_ln': 'grad_w', 'grad_b_conv_ln': 'grad_w', 'grad_w_gate_up': 'grad_w', 'grad_b_gate': 'grad_w', 'grad_g_gla_norm': 'grad_w', 'grad_w_out': 'grad_w', 'grad_g_norm_ffn2': 'grad_w', 'grad_w_ffn2_in': 'grad_w', 'grad_w_ffn2_out': 'grad_w', 'grad_g_norm_final': 'grad_w', 'grad_w_ada_final': 'grad_w', 'grad_b_ada_final': 'grad_w', 'delta_w_ada': 'delta_w', 'delta_b_ada': 'delta_w', 'delta_g_norm_ffn1': 'delta_w', 'delta_w_ffn1_in': 'delta_w', 'delta_w_ffn1_out': 'delta_w', 'delta_g_norm_mix': 'delta_w', 'delta_w_in': 'delta_w', 'delta_w_dw': 'delta_w', 'delta_b_dw': 'delta_w', 'delta_g_conv_ln': 'delta_w', 'delta_b_conv_ln': 'delta_w', 'delta_w_gate_up': 'delta_w', 'delta_b_gate': 'delta_w', 'delta_g_gla_norm': 'delta_w', 'delta_w_out': 'delta_w', 'delta_g_norm_ffn2': 'delta_w', 'delta_w_ffn2_in': 'delta_w', 'delta_w_ffn2_out': 'delta_w', 'delta_g_norm_final': 'delta_w', 'delta_w_ada_final': 'delta_w', 'delta_b_ada_final': 'delta_w', 'new_m_w_ada': 'new_m', 'new_m_b_ada': 'new_m', 'new_m_g_norm_ffn1': 'new_m', 'new_m_w_ffn1_in': 'new_m', 'new_m_w_ffn1_out': 'new_m', 'new_m_g_norm_mix': 'new_m', 'new_m_w_in': 'new_m', 'new_m_w_dw': 'new_m', 'new_m_b_dw': 'new_m', 'new_m_g_conv_ln': 'new_m', 'new_m_b_conv_ln': 'new_m', 'new_m_w_gate_up': 'new_m', 'new_m_b_gate': 'new_m', 'new_m_g_gla_norm': 'new_m', 'new_m_w_out': 'new_m', 'new_m_g_norm_ffn2': 'new_m', 'new_m_w_ffn2_in': 'new_m', 'new_m_w_ffn2_out': 'new_m', 'new_m_g_norm_final': 'new_m', 'new_m_w_ada_final': 'new_m', 'new_m_b_ada_final': 'new_m', 'new_v_w_ada': 'new_v', 'new_v_b_ada': 'new_v', 'new_v_g_norm_ffn1': 'new_v', 'new_v_w_ffn1_in': 'new_v', 'new_v_w_ffn1_out': 'new_v', 'new_v_g_norm_mix': 'new_v', 'new_v_w_in': 'new_v', 'new_v_w_dw': 'new_v', 'new_v_b_dw': 'new_v', 'new_v_g_conv_ln': 'new_v', 'new_v_b_conv_ln': 'new_v', 'new_v_w_gate_up': 'new_v', 'new_v_b_gate': 'new_v', 'new_v_g_gla_norm': 'new_v', 'new_v_w_out': 'new_v', 'new_v_g_norm_ffn2': 'new_v', 'new_v_w_ffn2_in': 'new_v', 'new_v_w_ffn2_out': 'new_v', 'new_v_g_norm_final': 'new_v', 'new_v_w_ada_final': 'new_v', 'new_v_b_ada_final': 'new_v'}


def _forward(args):
    return _fwd_reference(*[args[k] for k in FWD_PARAMS])


def _output_shape():
    def fwd():
        inp = _fwd_setup_inputs(0)
        return _fwd_reference(*[inp[k] for k in FWD_PARAMS])
    out = _jax.eval_shape(fwd)
    return out.shape, out.dtype

N_MICROBATCH = 1
ADAM_LR = 0.001
ADAM_B1 = 0.9
ADAM_B2 = 0.999
ADAM_EPS = 1e-08
ADAM_WD = 0.01
ADAM_STEP = 10
PER_EXAMPLE_BATCH_AXIS = {'x': 0, 'c': 0, 'loss_target': 0}
SHARED_INPUTS = []
_WEIGHT_DTYPES = {'w_ada': _jnp.float32, 'b_ada': _jnp.float32, 'g_norm_ffn1': _jnp.float32, 'w_ffn1_in': _jnp.float32, 'w_ffn1_out': _jnp.float32, 'g_norm_mix': _jnp.float32, 'w_in': _jnp.float32, 'w_dw': _jnp.float32, 'b_dw': _jnp.float32, 'g_conv_ln': _jnp.float32, 'b_conv_ln': _jnp.float32, 'w_gate_up': _jnp.float32, 'b_gate': _jnp.float32, 'g_gla_norm': _jnp.float32, 'w_out': _jnp.float32, 'g_norm_ffn2': _jnp.float32, 'w_ffn2_in': _jnp.float32, 'w_ffn2_out': _jnp.float32, 'g_norm_final': _jnp.float32, 'w_ada_final': _jnp.float32, 'b_ada_final': _jnp.float32}
MOMENT_SCALE = {'w_ada': 3.950073e-01, 'b_ada': 7.345243e-01, 'g_norm_ffn1': 5.395344e-02, 'w_ffn1_in': 3.442009e-02, 'w_ffn1_out': 6.733454e-02, 'g_norm_mix': 1.013861e-01, 'w_in': 1.073000e-01, 'w_dw': 2.419297e-01, 'b_dw': 1.646790e+00, 'g_conv_ln': 7.387715e-01, 'b_conv_ln': 1.053689e+00, 'w_gate_up': 1.927924e-02, 'b_gate': 5.738758e-02, 'g_gla_norm': 9.470297e-02, 'w_out': 3.194727e-01, 'g_norm_ffn2': 5.854366e-02, 'w_ffn2_in': 3.583246e-02, 'w_ffn2_out': 7.001032e-02, 'g_norm_final': 7.225805e+01, 'w_ada_final': 1.376519e+01, 'b_ada_final': 4.628402e+01}


def _to_microbatches(a, axis):
    t = _jnp.moveaxis(a, axis, 0)
    t = t.reshape((N_MICROBATCH, t.shape[0] // N_MICROBATCH) + t.shape[1:])
    return _jnp.moveaxis(t, 1, axis + 1)


def setup_inputs(seed: int = 0) -> dict:
    inp = _fwd_setup_inputs(seed)
    key = _jax.random.fold_in(_jax.random.key(seed), 7919)
    shape, _ = _output_shape()
    out = dict(inp)
    out["loss_target"] = _jax.random.normal(_jax.random.fold_in(key, 0), shape, _jnp.float32)
    for i, name in enumerate(TWIN_WEIGHTS):
        w = inp[name].astype(_jnp.float32)
        if MOMENT_SCALE is None:
            s = _jnp.sqrt(_jnp.mean(_jnp.square(w)) + 1e-30)
        else:
            s = MOMENT_SCALE[name]
        km, kv = _jax.random.split(_jax.random.fold_in(key, i + 1))
        out[name] = w
        out["m_" + name] = s * _jax.random.normal(km, w.shape, _jnp.float32)
        out["v_" + name] = (s * s) * _jax.random.uniform(kv, w.shape, _jnp.float32, 0.5, 1.5)
    if N_MICROBATCH > 1:
        for name, axis in PER_EXAMPLE_BATCH_AXIS.items():
            out[name] = _to_microbatches(out[name], axis)
    return {'x': out['x'], 'c': out['c'], 'w_ada': out['w_ada'], 'b_ada': out['b_ada'], 'g_norm_ffn1': out['g_norm_ffn1'], 'w_ffn1_in': out['w_ffn1_in'], 'w_ffn1_out': out['w_ffn1_out'], 'g_norm_mix': out['g_norm_mix'], 'w_in': out['w_in'], 'w_dw': out['w_dw'], 'b_dw': out['b_dw'], 'g_conv_ln': out['g_conv_ln'], 'b_conv_ln': out['b_conv_ln'], 'w_gate_up': out['w_gate_up'], 'b_gate': out['b_gate'], 'g_gla_norm': out['g_gla_norm'], 'w_out': out['w_out'], 'g_norm_ffn2': out['g_norm_ffn2'], 'w_ffn2_in': out['w_ffn2_in'], 'w_ffn2_out': out['w_ffn2_out'], 'g_norm_final': out['g_norm_final'], 'w_ada_final': out['w_ada_final'], 'b_ada_final': out['b_ada_final'], 'loss_target': out['loss_target'], 'm_w_ada': out['m_w_ada'], 'm_b_ada': out['m_b_ada'], 'm_g_norm_ffn1': out['m_g_norm_ffn1'], 'm_w_ffn1_in': out['m_w_ffn1_in'], 'm_w_ffn1_out': out['m_w_ffn1_out'], 'm_g_norm_mix': out['m_g_norm_mix'], 'm_w_in': out['m_w_in'], 'm_w_dw': out['m_w_dw'], 'm_b_dw': out['m_b_dw'], 'm_g_conv_ln': out['m_g_conv_ln'], 'm_b_conv_ln': out['m_b_conv_ln'], 'm_w_gate_up': out['m_w_gate_up'], 'm_b_gate': out['m_b_gate'], 'm_g_gla_norm': out['m_g_gla_norm'], 'm_w_out': out['m_w_out'], 'm_g_norm_ffn2': out['m_g_norm_ffn2'], 'm_w_ffn2_in': out['m_w_ffn2_in'], 'm_w_ffn2_out': out['m_w_ffn2_out'], 'm_g_norm_final': out['m_g_norm_final'], 'm_w_ada_final': out['m_w_ada_final'], 'm_b_ada_final': out['m_b_ada_final'], 'v_w_ada': out['v_w_ada'], 'v_b_ada': out['v_b_ada'], 'v_g_norm_ffn1': out['v_g_norm_ffn1'], 'v_w_ffn1_in': out['v_w_ffn1_in'], 'v_w_ffn1_out': out['v_w_ffn1_out'], 'v_g_norm_mix': out['v_g_norm_mix'], 'v_w_in': out['v_w_in'], 'v_w_dw': out['v_w_dw'], 'v_b_dw': out['v_b_dw'], 'v_g_conv_ln': out['v_g_conv_ln'], 'v_b_conv_ln': out['v_b_conv_ln'], 'v_w_gate_up': out['v_w_gate_up'], 'v_b_gate': out['v_b_gate'], 'v_g_gla_norm': out['v_g_gla_norm'], 'v_w_out': out['v_w_out'], 'v_g_norm_ffn2': out['v_g_norm_ffn2'], 'v_w_ffn2_in': out['v_w_ffn2_in'], 'v_w_ffn2_out': out['v_w_ffn2_out'], 'v_g_norm_final': out['v_g_norm_final'], 'v_w_ada_final': out['v_w_ada_final'], 'v_b_ada_final': out['v_b_ada_final']}


def _loss(weights, diff, rest, loss_target):
    with _jax.named_scope("forward"):
        args = {**rest, TWIN_DIFF_INPUT: diff, **{k: w.astype(_WEIGHT_DTYPES[k]) for k, w in weights.items()}}
        y = _forward(args)
    with _jax.named_scope("loss_head"):
        err = _jnp.square(y.astype(_jnp.float32) - loss_target)
        return 0.5 * _jnp.sum(_jnp.mean(err, axis=-1)) if err.ndim else 0.5 * err


def _adamw(w, g, m, v):
    m = ADAM_B1 * m + (1.0 - ADAM_B1) * g
    v = ADAM_B2 * v + (1.0 - ADAM_B2) * _jnp.square(g)
    m_hat = m / (1.0 - ADAM_B1 ** ADAM_STEP)
    v_hat = v / (1.0 - ADAM_B2 ** ADAM_STEP)
    delta = -ADAM_LR * (m_hat / (_jnp.sqrt(v_hat) + ADAM_EPS) + ADAM_WD * w)
    return delta, m, v


def reference(x, c, w_ada, b_ada, g_norm_ffn1, w_ffn1_in, w_ffn1_out, g_norm_mix, w_in, w_dw, b_dw, g_conv_ln, b_conv_ln, w_gate_up, b_gate, g_gla_norm, w_out, g_norm_ffn2, w_ffn2_in, w_ffn2_out, g_norm_final, w_ada_final, b_ada_final, loss_target, m_w_ada, m_b_ada, m_g_norm_ffn1, m_w_ffn1_in, m_w_ffn1_out, m_g_norm_mix, m_w_in, m_w_dw, m_b_dw, m_g_conv_ln, m_b_conv_ln, m_w_gate_up, m_b_gate, m_g_gla_norm, m_w_out, m_g_norm_ffn2, m_w_ffn2_in, m_w_ffn2_out, m_g_norm_final, m_w_ada_final, m_b_ada_final, v_w_ada, v_b_ada, v_g_norm_ffn1, v_w_ffn1_in, v_w_ffn1_out, v_g_norm_mix, v_w_in, v_w_dw, v_b_dw, v_g_conv_ln, v_b_conv_ln, v_w_gate_up, v_b_gate, v_g_gla_norm, v_w_out, v_g_norm_ffn2, v_w_ffn2_in, v_w_ffn2_out, v_g_norm_final, v_w_ada_final, v_b_ada_final):
    given = dict(x=x, c=c, w_ada=w_ada, b_ada=b_ada, g_norm_ffn1=g_norm_ffn1, w_ffn1_in=w_ffn1_in, w_ffn1_out=w_ffn1_out, g_norm_mix=g_norm_mix, w_in=w_in, w_dw=w_dw, b_dw=b_dw, g_conv_ln=g_conv_ln, b_conv_ln=b_conv_ln, w_gate_up=w_gate_up, b_gate=b_gate, g_gla_norm=g_gla_norm, w_out=w_out, g_norm_ffn2=g_norm_ffn2, w_ffn2_in=w_ffn2_in, w_ffn2_out=w_ffn2_out, g_norm_final=g_norm_final, w_ada_final=w_ada_final, b_ada_final=b_ada_final, loss_target=loss_target, m_w_ada=m_w_ada, m_b_ada=m_b_ada, m_g_norm_ffn1=m_g_norm_ffn1, m_w_ffn1_in=m_w_ffn1_in, m_w_ffn1_out=m_w_ffn1_out, m_g_norm_mix=m_g_norm_mix, m_w_in=m_w_in, m_w_dw=m_w_dw, m_b_dw=m_b_dw, m_g_conv_ln=m_g_conv_ln, m_b_conv_ln=m_b_conv_ln, m_w_gate_up=m_w_gate_up, m_b_gate=m_b_gate, m_g_gla_norm=m_g_gla_norm, m_w_out=m_w_out, m_g_norm_ffn2=m_g_norm_ffn2, m_w_ffn2_in=m_w_ffn2_in, m_w_ffn2_out=m_w_ffn2_out, m_g_norm_final=m_g_norm_final, m_w_ada_final=m_w_ada_final, m_b_ada_final=m_b_ada_final, v_w_ada=v_w_ada, v_b_ada=v_b_ada, v_g_norm_ffn1=v_g_norm_ffn1, v_w_ffn1_in=v_w_ffn1_in, v_w_ffn1_out=v_w_ffn1_out, v_g_norm_mix=v_g_norm_mix, v_w_in=v_w_in, v_w_dw=v_w_dw, v_b_dw=v_b_dw, v_g_conv_ln=v_g_conv_ln, v_b_conv_ln=v_b_conv_ln, v_w_gate_up=v_w_gate_up, v_b_gate=v_b_gate, v_g_gla_norm=v_g_gla_norm, v_w_out=v_w_out, v_g_norm_ffn2=v_g_norm_ffn2, v_w_ffn2_in=v_w_ffn2_in, v_w_ffn2_out=v_w_ffn2_out, v_g_norm_final=v_g_norm_final, v_w_ada_final=v_w_ada_final, v_b_ada_final=v_b_ada_final)
    weights = {n: given[n] for n in TWIN_WEIGHTS}
    shared = {n: given[n] for n in SHARED_INPUTS}
    per_example = {n: given[n] for n in ['x', 'c']}
    grad_fn = _jax.value_and_grad(_loss, argnums=(0, 1))

    def one_microbatch(ex, loss_target):
        ex = dict(ex)
        diff = ex.pop(TWIN_DIFF_INPUT)
        return grad_fn(weights, diff, {**shared, **ex}, loss_target)

    if N_MICROBATCH == 1:
        loss, (grad_w, grad_x) = one_microbatch(per_example, given["loss_target"])
    else:
        def body(carry, xs):
            loss_sum, grad_sum = carry
            l_k, (gw_k, gx_k) = one_microbatch(xs[0], xs[1])
            with _jax.named_scope("update"):
                return (loss_sum + l_k, _jax.tree.map(_jnp.add, grad_sum, gw_k)), gx_k

        init = (_jnp.zeros((), _jnp.float32), _jax.tree.map(_jnp.zeros_like, weights))
        (loss, grad_w), grad_x = _jax.lax.scan(body, init, (per_example, given["loss_target"]))
    with _jax.named_scope("update"):
        delta_w, new_m, new_v = {}, {}, {}
        for n in TWIN_WEIGHTS:
            delta_w[n], new_m[n], new_v[n] = _adamw(weights[n], grad_w[n], given["m_" + n], given["v_" + n])
    return (loss, grad_x, *[grad_w[n] for n in TWIN_WEIGHTS], *[delta_w[n] for n in TWIN_WEIGHTS],
            *[new_m[n] for n in TWIN_WEIGHTS], *[new_v[n] for n in TWIN_WEIGHTS])
```

```python
import functools

import jax
import jax.numpy as jnp
from jax import lax
from jax.experimental import pallas as pl
from jax.experimental.pallas import tpu as pltpu

f32 = jnp.float32
bf16 = jnp.bfloat16

N_DEV = 8
DEPTH = 2
D = 1024
F = 2816
DC = 512
NH = 4
DK = 64
DV = 128
DQK = NH * DK
DG = NH * DV
CH = 64
CW = 31
GR = 16
TAU = 16.0
N_MOD = 9
DIN = 2 * DC + 2 * DQK + 2 * DG + GR
DINP = 2688
EPS = 1e-6
HALO = 32

ADAM_LR = 0.001
ADAM_B1 = 0.9
ADAM_B2 = 0.999
ADAM_EPS = 1e-08
ADAM_WD = 0.01
ADAM_STEP = 10

V7X_VMEM_LIMIT = 56 * 1024 * 1024
MESH = pl.DeviceIdType.MESH
HIGHEST = lax.Precision.HIGHEST

NT = (((1,), (1,)), ((), ()))
TN = (((0,), (0,)), ((), ()))


def _cp(n_axes):
    return pltpu.CompilerParams(dimension_semantics=("arbitrary",) * n_axes, vmem_limit_bytes=V7X_VMEM_LIMIT)


def _full(shape):
    nd = len(shape)
    return pl.BlockSpec(shape, lambda *_: (0,) * nd)


def _dot(a, b):
    return jnp.dot(a, b, preferred_element_type=f32)


def _dg(a, b, dims):
    return lax.dot_general(a, b, dims, preferred_element_type=f32)


def _sigmoid(x):
    return jax.nn.sigmoid(x)


def _rowsum(x):
    return jnp.sum(x, axis=0, keepdims=True)


def _rms_parts(xv):
    rstd = lax.rsqrt(jnp.mean(xv * xv, axis=-1, keepdims=True) + EPS)
    return xv * rstd, rstd


def _rms_bwd(dxh, xh, rstd):
    return rstd * (dxh - xh * jnp.mean(dxh * xh, axis=-1, keepdims=True))


def ffn_fwd(x, mod, g, wg, wu, wo, rows, tm, tf, name):
    S = x.shape[0]
    nj = F // tf
    r_shift, r_scale, r_gate = rows

    def body(x_ref, mod_ref, g_ref, wg_ref, wu_ref, wo_ref, xo_ref, zg_ref, zu_ref, f_ref, h_s, acc_s):
        j = pl.program_id(1)

        @pl.when(j == 0)
        def _():
            xh, _ = _rms_parts(x_ref[...])
            hv = xh * g_ref[...] * (1.0 + mod_ref[r_scale:r_scale + 1, :]) + mod_ref[r_shift:r_shift + 1, :]
            h_s[...] = hv.astype(bf16)
            acc_s[...] = jnp.zeros_like(acc_s)

        h = h_s[...]
        zg = _dot(h, wg_ref[...])
        zu = _dot(h, wu_ref[...])
        zg_ref[...] = zg.astype(bf16)
        zu_ref[...] = zu.astype(bf16)
        a = zg * _sigmoid(zg) * zu
        acc_s[...] += _dot(a.astype(bf16), wo_ref[...])

        @pl.when(j == nj - 1)
        def _():
            fv = acc_s[...]
            f_ref[...] = fv.astype(bf16)
            xo_ref[...] = x_ref[...] + 0.5 * mod_ref[r_gate:r_gate + 1, :] * fv

    return pl.pallas_call(
        body, name=name,
        grid=(S // tm, nj),
        in_specs=[
            pl.BlockSpec((tm, D), lambda i, j: (i, 0)),
            _full(mod.shape), _full(g.shape),
            pl.BlockSpec((D, tf), lambda i, j: (0, j)),
            pl.BlockSpec((D, tf), lambda i, j: (0, j)),
            pl.BlockSpec((tf, D), lambda i, j: (j, 0)),
        ],
        out_specs=[
            pl.BlockSpec((tm, D), lambda i, j: (i, 0)),
            pl.BlockSpec((tm, tf), lambda i, j: (i, j)),
            pl.BlockSpec((tm, tf), lambda i, j: (i, j)),
            pl.BlockSpec((tm, D), lambda i, j: (i, 0)),
        ],
        out_shape=[
            jax.ShapeDtypeStruct((S, D), f32),
            jax.ShapeDtypeStruct((S, F), bf16),
            jax.ShapeDtypeStruct((S, F), bf16),
            jax.ShapeDtypeStruct((S, D), bf16),
        ],
        scratch_shapes=[pltpu.VMEM((tm, D), bf16), pltpu.VMEM((tm, D), f32)],
        compiler_params=_cp(2),
    )(x, mod, g, wg, wu, wo)


def ffn_bwd(x, dy, zg, zu, fo, mod, g, wg, wu, wo, rows, tm, tf, name):
    S = x.shape[0]
    nj = F // tf
    r_shift, r_scale, r_gate = rows

    def body(x_ref, dy_ref, zg_ref, zu_ref, f_ref, mod_ref, g_ref, wg_ref, wu_ref, wo_ref,
             dx_ref, h_ref, df_ref, a_ref, dzg_ref, dzu_ref, red_ref, acc_s, df_s):
        i = pl.program_id(0)
        j = pl.program_id(1)

        @pl.when((i == 0) & (j == 0))
        def _():
            red_ref[...] = jnp.zeros_like(red_ref)

        @pl.when(j == 0)
        def _():
            xh, _ = _rms_parts(x_ref[...])
            hv = xh * g_ref[...] * (1.0 + mod_ref[r_scale:r_scale + 1, :]) + mod_ref[r_shift:r_shift + 1, :]
            h_ref[...] = hv.astype(bf16)
            df = (0.5 * mod_ref[r_gate:r_gate + 1, :] * dy_ref[...]).astype(bf16)
            df_s[...] = df
            df_ref[...] = df
            acc_s[...] = jnp.zeros_like(acc_s)

        zgv = zg_ref[...].astype(f32)
        zuv = zu_ref[...].astype(f32)
        s = _sigmoid(zgv)
        sil = zgv * s
        a_ref[...] = (sil * zuv).astype(bf16)
        da = _dg(df_s[...], wo_ref[...], NT)
        dzu = (da * sil).astype(bf16)
        dzg = (da * zuv * (s * (1.0 + zgv * (1.0 - s)))).astype(bf16)
        dzg_ref[...] = dzg
        dzu_ref[...] = dzu
        acc_s[...] += _dg(dzg, wg_ref[...], NT) + _dg(dzu, wu_ref[...], NT)

        @pl.when(j == nj - 1)
        def _():
            dh = acc_s[...]
            dyv = dy_ref[...]
            xh, rstd = _rms_parts(x_ref[...])
            gv = g_ref[...]
            n = xh * gv
            dn = dh * (1.0 + mod_ref[r_scale:r_scale + 1, :])
            red_ref[0:1, :] += _rowsum(dh)
            red_ref[1:2, :] += _rowsum(dh * n)
            red_ref[2:3, :] += _rowsum(0.5 * f_ref[...].astype(f32) * dyv)
            red_ref[3:4, :] += _rowsum(dn * xh)
            dx_ref[...] = dyv + _rms_bwd(dn * gv, xh, rstd)

    row = lambda i, j: (i, 0)
    tile = lambda i, j: (i, j)
    return pl.pallas_call(
        body, name=name,
        grid=(S // tm, nj),
        in_specs=[
            pl.BlockSpec((tm, D), row), pl.BlockSpec((tm, D), row),
            pl.BlockSpec((tm, tf), tile), pl.BlockSpec((tm, tf), tile),
            pl.BlockSpec((tm, D), row),
            _full(mod.shape), _full(g.shape),
            pl.BlockSpec((D, tf), lambda i, j: (0, j)),
            pl.BlockSpec((D, tf), lambda i, j: (0, j)),
            pl.BlockSpec((tf, D), lambda i, j: (j, 0)),
        ],
        out_specs=[
            pl.BlockSpec((tm, D), row), pl.BlockSpec((tm, D), row), pl.BlockSpec((tm, D), row),
            pl.BlockSpec((tm, tf), tile), pl.BlockSpec((tm, tf), tile), pl.BlockSpec((tm, tf), tile),
            _full((8, D)),
        ],
        out_shape=[
            jax.ShapeDtypeStruct((S, D), f32), jax.ShapeDtypeStruct((S, D), bf16), jax.ShapeDtypeStruct((S, D), bf16),
            jax.ShapeDtypeStruct((S, F), bf16), jax.ShapeDtypeStruct((S, F), bf16), jax.ShapeDtypeStruct((S, F), bf16),
            jax.ShapeDtypeStruct((8, D), f32),
        ],
        scratch_shapes=[pltpu.VMEM((tm, D), f32), pltpu.VMEM((tm, D), bf16)],
        compiler_params=_cp(2),
    )(x, dy, zg, zu, fo, mod, g, wg, wu, wo)


def matmul_tn(a, b, M, N, bm, bn, bk, name, a_col_block=0):
    S = b.shape[0]

    def body(a_ref, b_ref, o_ref):
        @pl.when(pl.program_id(2) == 0)
        def _():
            o_ref[...] = jnp.zeros_like(o_ref)

        o_ref[...] += _dg(a_ref[...].astype(bf16), b_ref[...].astype(bf16), TN)

    return pl.pallas_call(
        body, name=name,
        grid=(M // bm, N // bn, S // bk),
        in_specs=[
            pl.BlockSpec((bk, bm), lambda i, j, k: (k, i + a_col_block)),
            pl.BlockSpec((bk, bn), lambda i, j, k: (k, j)),
        ],
        out_specs=pl.BlockSpec((bm, bn), lambda i, j, k: (i, j)),
        out_shape=jax.ShapeDtypeStruct((M, N), f32),
        compiler_params=_cp(3),
    )(a, b)


def mixin_fwd(x1, mod, g, win, wgu, bgate, tm, name):
    S = x1.shape[0]

    def body(x_ref, mod_ref, g_ref, win_ref, wgu_ref, bg_ref, z_ref, la_ref):
        xh, _ = _rms_parts(x_ref[...])
        hv = xh * g_ref[...] * (1.0 + mod_ref[4:5, :]) + mod_ref[3:4, :]
        z = _dot(hv.astype(bf16), win_ref[...])
        z_ref[...] = z
        glr = z[:, DINP - 128:]
        pre = _dot(glr.astype(bf16), wgu_ref[...]) + bg_ref[...]
        la_ref[...] = (jnp.minimum(pre, 0.0) - jnp.log(1.0 + jnp.exp(-jnp.abs(pre)))) * (1.0 / TAU)

    return pl.pallas_call(
        body, name=name,
        grid=(S // tm,),
        in_specs=[pl.BlockSpec((tm, D), lambda i: (i, 0)), _full(mod.shape), _full(g.shape),
                  _full(win.shape), _full(wgu.shape), _full(bgate.shape)],
        out_specs=[pl.BlockSpec((tm, DINP), lambda i: (i, 0)), pl.BlockSpec((tm, DQK), lambda i: (i, 0))],
        out_shape=[jax.ShapeDtypeStruct((S, DINP), f32), jax.ShapeDtypeStruct((S, DQK), f32)],
        compiler_params=_cp(1),
    )(x1, mod, g, win, wgu, bgate)


def mixin_bwd(x1, dres, dzab, dq, dk, dv, dr, dpre, mod, g, win, wgu, tm, name):
    S = x1.shape[0]

    def body(x_ref, dres_ref, dzab_ref, dq_ref, dk_ref, dv_ref, dr_ref, dpre_ref, mod_ref, g_ref, win_ref, wgu_ref,
             dx_ref, h_ref, dz_ref, red_ref):
        @pl.when(pl.program_id(0) == 0)
        def _():
            red_ref[...] = jnp.zeros_like(red_ref)

        dglr = _dg(dpre_ref[...].astype(bf16), wgu_ref[...], NT)
        dz = jnp.concatenate([dzab_ref[...], dq_ref[...], dk_ref[...], dv_ref[...], dr_ref[...], dglr], axis=1).astype(bf16)
        dz_ref[...] = dz
        dh = _dg(dz, win_ref[...], NT)
        xh, rstd = _rms_parts(x_ref[...])
        gv = g_ref[...]
        n = xh * gv
        sc = 1.0 + mod_ref[4:5, :]
        h_ref[...] = (n * sc + mod_ref[3:4, :]).astype(bf16)
        dn = dh * sc
        red_ref[0:1, :] += _rowsum(dh)
        red_ref[1:2, :] += _rowsum(dh * n)
        red_ref[2:3, :] += _rowsum(dn * xh)
        dx_ref[...] = dres_ref[...] + _rms_bwd(dn * gv, xh, rstd)

    row = lambda i: (i, 0)
    return pl.pallas_call(
        body, name=name,
        grid=(S // tm,),
        in_specs=[pl.BlockSpec((tm, D), row), pl.BlockSpec((tm, D), row),
                  pl.BlockSpec((tm, 2 * DC), row), pl.BlockSpec((tm, DQK), row), pl.BlockSpec((tm, DQK), row),
                  pl.BlockSpec((tm, DG), row), pl.BlockSpec((tm, DG), row), pl.BlockSpec((tm, DQK), row),
                  _full(mod.shape), _full(g.shape), _full(win.shape), _full(wgu.shape)],
        out_specs=[pl.BlockSpec((tm, D), row), pl.BlockSpec((tm, D), row), pl.BlockSpec((tm, DINP), row), _full((8, D))],
        out_shape=[jax.ShapeDtypeStruct((S, D), f32), jax.ShapeDtypeStruct((S, D), bf16),
                   jax.ShapeDtypeStruct((S, DINP), bf16), jax.ShapeDtypeStruct((8, D), f32)],
        compiler_params=_cp(1),
    )(x1, dres, dzab, dq, dk, dv, dr, dpre, mod, g, win, wgu)


def _glu(zab):
    return zab[:, :DC] * _sigmoid(zab[:, DC:])


def conv_fwd(z, wdw, cpar, tc, name):
    S = z.shape[0]
    nb = tc // HALO

    def body(zc_ref, zp_ref, w_ref, cp_ref, y_ref, yc_ref, u_s):
        i = pl.program_id(0)
        up = _glu(zp_ref[...])
        u_s[0:HALO, :] = jnp.where(i > 0, up, 0.0)
        u_s[HALO:HALO + tc, :] = _glu(zc_ref[...])
        acc = jnp.zeros((tc, DC), f32)
        for w in range(CW):
            o = HALO - (CW - 1) + w
            acc = acc + u_s[o:o + tc, :] * w_ref[w:w + 1, :]
        y = acc + cp_ref[0:1, :]
        y_ref[...] = y
        yc = y - jnp.mean(y, axis=-1, keepdims=True)
        yl = yc * lax.rsqrt(jnp.mean(yc * yc, axis=-1, keepdims=True) + EPS) * cp_ref[1:2, :] + cp_ref[2:3, :]
        yc_ref[...] = (yl * _sigmoid(yl)).astype(bf16)

    return pl.pallas_call(
        body, name=name,
        grid=(S // tc,),
        in_specs=[pl.BlockSpec((tc, 2 * DC), lambda i: (i, 0)),
                  pl.BlockSpec((HALO, 2 * DC), lambda i: (jnp.maximum(i * nb - 1, 0), 0)),
                  _full(wdw.shape), _full(cpar.shape)],
        out_specs=[pl.BlockSpec((tc, DC), lambda i: (i, 0)), pl.BlockSpec((tc, DC), lambda i: (i, 0))],
        out_shape=[jax.ShapeDtypeStruct((S, DC), f32), jax.ShapeDtypeStruct((S, DC), bf16)],
        scratch_shapes=[pltpu.VMEM((HALO + tc, DC), f32)],
        compiler_params=_cp(1),
    )(z, z, wdw, cpar)


def conv_bwd(z, y, dyc, wdw, cpar, tc, name):
    S = z.shape[0]
    nb = tc // HALO
    nt = S // tc
    last_halo = S // HALO - 1

    def body(zc_ref, zp_ref, y_ref, yn_ref, d_ref, dn_ref, w_ref, cp_ref, dz_ref, red_ref, u_s, dy_s):
        i = pl.program_id(0)

        @pl.when(i == 0)
        def _():
            red_ref[...] = jnp.zeros_like(red_ref)

        gl = cp_ref[1:2, :]
        bl = cp_ref[2:3, :]

        def ln_bwd(yv, dv):
            yc = yv - jnp.mean(yv, axis=-1, keepdims=True)
            rstd = lax.rsqrt(jnp.mean(yc * yc, axis=-1, keepdims=True) + EPS)
            yh = yc * rstd
            yl = yh * gl + bl
            s = _sigmoid(yl)
            dyl = dv * (s * (1.0 + yl * (1.0 - s)))
            dyh = dyl * gl
            dyv = rstd * (dyh - jnp.mean(dyh, axis=-1, keepdims=True) - yh * jnp.mean(dyh * yh, axis=-1, keepdims=True))
            return dyv, dyl, yh

        dy_c, dyl_c, yh_c = ln_bwd(y_ref[...], d_ref[...])
        dy_n, _, _ = ln_bwd(yn_ref[...], dn_ref[...])
        dy_s[0:tc, :] = dy_c
        dy_s[tc:tc + HALO, :] = jnp.where(i < nt - 1, dy_n, 0.0)
        zc = zc_ref[...]
        av = zc[:, :DC]
        sb = _sigmoid(zc[:, DC:])
        u_s[0:HALO, :] = jnp.where(i > 0, _glu(zp_ref[...]), 0.0)
        u_s[HALO:HALO + tc, :] = av * sb
        du = jnp.zeros((tc, DC), f32)
        for w in range(CW):
            o = HALO - (CW - 1) + w
            red_ref[w:w + 1, :] += _rowsum(u_s[o:o + tc, :] * dy_c)
            du = du + dy_s[CW - 1 - w:CW - 1 - w + tc, :] * w_ref[w:w + 1, :]
        red_ref[32:33, :] += _rowsum(dy_c)
        red_ref[33:34, :] += _rowsum(dyl_c * yh_c)
        red_ref[34:35, :] += _rowsum(dyl_c)
        dz_ref[...] = jnp.concatenate([du * sb, du * av * sb * (1.0 - sb)], axis=1)

    cur = lambda i: (i, 0)
    nxt = lambda i: (jnp.minimum((i + 1) * nb, last_halo), 0)
    return pl.pallas_call(
        body, name=name,
        grid=(nt,),
        in_specs=[pl.BlockSpec((tc, 2 * DC), cur),
                  pl.BlockSpec((HALO, 2 * DC), lambda i: (jnp.maximum(i * nb - 1, 0), 0)),
                  pl.BlockSpec((tc, DC), cur), pl.BlockSpec((HALO, DC), nxt),
                  pl.BlockSpec((tc, DC), cur), pl.BlockSpec((HALO, DC), nxt),
                  _full(wdw.shape), _full(cpar.shape)],
        out_specs=[pl.BlockSpec((tc, 2 * DC), cur), _full((40, DC))],
        out_shape=[jax.ShapeDtypeStruct((S, 2 * DC), f32), jax.ShapeDtypeStruct((40, DC), f32)],
        scratch_shapes=[pltpu.VMEM((HALO + tc, DC), f32), pltpu.VMEM((tc + HALO, DC), f32)],
        compiler_params=_cp(1),
    )(z, z, y, y, dyc, dyc, wdw, cpar)


def _gla_consts():
    r = lax.broadcasted_iota(jnp.int32, (CH, CH), 0)
    c = lax.broadcasted_iota(jnp.int32, (CH, CH), 1)
    tril = r >= c
    lane = lax.broadcasted_iota(jnp.int32, (CH, DQK), 1)
    masks = [(lane >= h * DK) & (lane < (h + 1) * DK) for h in range(NH)]
    r4 = lax.broadcasted_iota(jnp.int32, (DQK, DQK), 0)
    c4 = lax.broadcasted_iota(jnp.int32, (DQK, DQK), 1)
    eye4 = (r4 == c4).astype(f32)
    rs = lax.broadcasted_iota(jnp.int32, (DQK, CH), 0) & (CH - 1)
    tril4 = rs >= lax.broadcasted_iota(jnp.int32, (DQK, CH), 1)
    return tril, tril4, masks, eye4


def _stack(xv, masks):
    return jnp.concatenate([jnp.where(m, xv, 0.0) for m in masks], axis=0)


def _unstack(rv, masks):
    out = jnp.where(masks[0], rv[0:CH, :], 0.0)
    for h in range(1, NH):
        out = out + jnp.where(masks[h], rv[h * CH:(h + 1) * CH, :], 0.0)
    return out


def _vstack(xv):
    return jnp.concatenate([xv[:, h * DV:(h + 1) * DV] for h in range(NH)], axis=0)


def _vunstack(xv):
    return jnp.concatenate([xv[h * CH:(h + 1) * CH, :] for h in range(NH)], axis=1)


def _gla_chunk_fwd(lac, qc, kc, vc, s_all, tril, masks, tril4):
    lmat = tril.astype(f32)
    bc = jnp.dot(lmat, lac, preferred_element_type=f32, precision=HIGHEST)
    bend = bc[CH - 1:CH, :]
    eb = jnp.exp(bc)
    enb = jnp.exp(-bc)
    ed = jnp.exp(bend - bc)
    qh = qc * (DK ** -0.5)
    qf = qh * eb
    qn = qh * enb
    kn = kc * enb
    kp = kc * eb
    kd = kc * ed
    qf_s = _stack(qf, masks).astype(bf16)
    qn_s = _stack(qn, masks).astype(bf16)
    kn_b = kn.astype(bf16)
    kp_b = kp.astype(bf16)
    attf = _dg(qf_s, kn_b, NT)
    attb = _dg(qn_s, kp_b, NT)
    a_s = jnp.where(tril4, attf, attb)
    a_b = a_s.astype(bf16)
    v_b = vc.astype(bf16)
    intra = jnp.concatenate(
        [_dot(a_b[h * CH:(h + 1) * CH, :], v_b[:, h * DV:(h + 1) * DV]) for h in range(NH)], axis=0)
    o_s = intra + _dot(qf_s, s_all.astype(bf16))
    return dict(bc=bc, bend=bend, eb=eb, enb=enb, ed=ed, qf=qf, qn=qn, kn=kn, kp=kp, kd=kd,
                qf_s=qf_s, qn_s=qn_s, kn_b=kn_b, kp_b=kp_b, a_b=a_b, v_b=v_b, o_s=o_s)


def _col_from_row(row, eye4):
    return jnp.sum(eye4 * row, axis=1, keepdims=True)


def _row_from_col(col, eye4):
    return jnp.sum(eye4 * col, axis=0, keepdims=True)


def gla_fwd(z, la, gn_s, tg, name):
    S = z.shape[0]
    nc = tg // CH

    def body(q_ref, k_ref, v_ref, r_ref, la_ref, gn_ref, yg_ref, sp_ref, st):
        @pl.when(pl.program_id(0) == 0)
        def _():
            st[...] = jnp.zeros_like(st)

        tril, tril4, masks, eye4 = _gla_consts()

        def chunk(c, carry):
            r0 = pl.multiple_of(c * CH, CH)
            s0 = pl.multiple_of(c * DQK, DQK)
            s_all = st[...]
            sp_ref[pl.ds(s0, DQK), :] = s_all
            vc = v_ref[pl.ds(r0, CH), :]
            t = _gla_chunk_fwd(la_ref[pl.ds(r0, CH), :], q_ref[pl.ds(r0, CH), :], k_ref[pl.ds(r0, CH), :], vc,
                               s_all, tril, masks, tril4)
            u_all = _dg(_stack(t["kd"], masks).astype(bf16), _vstack(vc).astype(bf16), TN)
            st[...] = _col_from_row(jnp.exp(t["bend"]), eye4) * s_all + u_all
            o_s = t["o_s"]
            on = o_s * lax.rsqrt(jnp.mean(o_s * o_s, axis=-1, keepdims=True) + EPS) * gn_ref[...]
            rc = r_ref[pl.ds(r0, CH), :]
            yg_ref[pl.ds(r0, CH), :] = (_vunstack(on) * (rc * _sigmoid(rc))).astype(bf16)
            return carry

        lax.fori_loop(0, nc, chunk, 0)

    return pl.pallas_call(
        body, name=name,
        grid=(S // tg,),
        in_specs=[pl.BlockSpec((tg, DQK), lambda i: (i, 4)), pl.BlockSpec((tg, DQK), lambda i: (i, 5)),
                  pl.BlockSpec((tg, DG), lambda i: (i, 3)), pl.BlockSpec((tg, DG), lambda i: (i, 4)),
                  pl.BlockSpec((tg, DQK), lambda i: (i, 0)), _full(gn_s.shape)],
        out_specs=[pl.BlockSpec((tg, DG), lambda i: (i, 0)), pl.BlockSpec((nc * DQK, DV), lambda i: (i, 0))],
        out_shape=[jax.ShapeDtypeStruct((S, DG), bf16), jax.ShapeDtypeStruct((S // CH * DQK, DV), f32)],
        scratch_shapes=[pltpu.VMEM((DQK, DV), f32)],
        compiler_params=_cp(1),
    )(z, z, z, z, la, gn_s)


def gla_bwd(z, la, sprev, dyg, gn_s, tg, name):
    S = z.shape[0]
    nc = tg // CH
    nt = S // tg

    def body(q_ref, k_ref, v_ref, r_ref, la_ref, sp_ref, dy_ref, gn_ref,
             dq_ref, dk_ref, dv_ref, dr_ref, dpre_ref, redg_ref, redb_ref, gs):
        @pl.when(pl.program_id(0) == 0)
        def _():
            gs[...] = jnp.zeros_like(gs)
            redg_ref[...] = jnp.zeros_like(redg_ref)
            redb_ref[...] = jnp.zeros_like(redb_ref)

        tril, tril4, masks, eye4 = _gla_consts()
        umat = (lax.broadcasted_iota(jnp.int32, (CH, CH), 0) <= lax.broadcasted_iota(jnp.int32, (CH, CH), 1)).astype(f32)
        last_row = lax.broadcasted_iota(jnp.int32, (CH, DQK), 0) == CH - 1

        def chunk(tt, carry):
            c = nc - 1 - tt
            r0 = pl.multiple_of(c * CH, CH)
            s0 = pl.multiple_of(c * DQK, DQK)
            s_all = sp_ref[pl.ds(s0, DQK), :]
            lac = la_ref[pl.ds(r0, CH), :]
            vc = v_ref[pl.ds(r0, CH), :]
            rc = r_ref[pl.ds(r0, CH), :]
            t = _gla_chunk_fwd(lac, q_ref[pl.ds(r0, CH), :], k_ref[pl.ds(r0, CH), :], vc, s_all, tril, masks, tril4)
            g_all = gs[...]
            g_b = g_all.astype(bf16)
            s_b = s_all.astype(bf16)
            o_s = t["o_s"]
            rstd = lax.rsqrt(jnp.mean(o_s * o_s, axis=-1, keepdims=True) + EPS)
            oh = o_s * rstd
            gnv = gn_ref[...]
            sr = _sigmoid(rc)
            dyv = dy_ref[pl.ds(r0, CH), :]
            dr_ref[pl.ds(r0, CH), :] = dyv * _vunstack(oh * gnv) * (sr * (1.0 + rc * (1.0 - sr)))
            don = _vstack(dyv * (rc * sr))
            redg_ref[...] += don * oh
            doh = don * gnv
            do_s = rstd * (doh - oh * jnp.mean(doh * oh, axis=-1, keepdims=True))
            do_b = do_s.astype(bf16)
            v_b = t["v_b"]
            vst_b = _vstack(vc).astype(bf16)
            kd_s = _stack(t["kd"], masks).astype(bf16)
            da_s = jnp.concatenate(
                [_dg(do_b[h * CH:(h + 1) * CH, :], v_b[:, h * DV:(h + 1) * DV], NT) for h in range(NH)], axis=0)
            a_b = t["a_b"]
            dv_s = jnp.concatenate(
                [_dg(a_b[h * CH:(h + 1) * CH, :], do_b[h * CH:(h + 1) * CH, :], TN) for h in range(NH)], axis=0)
            dv_s = dv_s + _dot(kd_s, g_b)
            dv_ref[pl.ds(r0, CH), :] = _vunstack(dv_s)
            gend = jnp.exp(t["bend"])
            gcol = _col_from_row(gend, eye4)
            gs[...] = gcol * g_all + _dg(t["qf_s"], do_b, TN)
            dgcol = jnp.sum(g_all * s_all, axis=1, keepdims=True)
            dbend = _row_from_col(dgcol * gcol, eye4)
            dkd = _unstack(_dg(vst_b, g_b, NT), masks)
            daf = jnp.where(tril4, da_s, 0.0).astype(bf16)
            dab = jnp.where(tril4, 0.0, da_s).astype(bf16)
            dqf = _unstack(_dot(daf, t["kn_b"]) + _dg(do_b, s_b, NT), masks)
            dqn = _unstack(_dot(dab, t["kp_b"]), masks)
            dkn = _dg(daf, t["qf_s"], TN)
            dkp = _dg(dab, t["qn_s"], TN)
            dq_ref[pl.ds(r0, CH), :] = (dqf * t["eb"] + dqn * t["enb"]) * (DK ** -0.5)
            dk_ref[pl.ds(r0, CH), :] = dkn * t["enb"] + dkp * t["eb"] + dkd * t["ed"]
            dkd_kd = dkd * t["kd"]
            dbc = dqf * t["qf"] - dqn * t["qn"] - dkn * t["kn"] + dkp * t["kp"] - dkd_kd
            dbc = dbc + jnp.where(last_row, _rowsum(dkd_kd) + dbend, 0.0)
            dla = jnp.dot(umat, dbc, preferred_element_type=f32, precision=HIGHEST)
            dpre = dla * (1.0 / TAU) * (1.0 - jnp.exp(TAU * lac))
            dpre_ref[pl.ds(r0, CH), :] = dpre
            redb_ref[...] += dpre
            return carry

        lax.fori_loop(0, nc, chunk, 0)

    rev = lambda col: (lambda i: (nt - 1 - i, col))
    return pl.pallas_call(
        body, name=name,
        grid=(nt,),
        in_specs=[pl.BlockSpec((tg, DQK), rev(4)), pl.BlockSpec((tg, DQK), rev(5)),
                  pl.BlockSpec((tg, DG), rev(3)), pl.BlockSpec((tg, DG), rev(4)),
                  pl.BlockSpec((tg, DQK), rev(0)), pl.BlockSpec((nc * DQK, DV), rev(0)),
                  pl.BlockSpec((tg, DG), rev(0)), _full(gn_s.shape)],
        out_specs=[pl.BlockSpec((tg, DQK), rev(0)), pl.BlockSpec((tg, DQK), rev(0)),
                   pl.BlockSpec((tg, DG), rev(0)), pl.BlockSpec((tg, DG), rev(0)), pl.BlockSpec((tg, DQK), rev(0)),
                   _full((DQK, DV)), _full((CH, DQK))],
        out_shape=[jax.ShapeDtypeStruct((S, DQK), f32), jax.ShapeDtypeStruct((S, DQK), f32),
                   jax.ShapeDtypeStruct((S, DG), f32), jax.ShapeDtypeStruct((S, DG), f32), jax.ShapeDtypeStruct((S, DQK), f32),
                   jax.ShapeDtypeStruct((DQK, DV), f32), jax.ShapeDtypeStruct((CH, DQK), f32)],
        scratch_shapes=[pltpu.VMEM((DQK, DV), f32)],
        compiler_params=_cp(1),
    )(z, z, z, z, la, sprev, dyg, gn_s)


def mixout_fwd(x1, yc, yg, mod, wout, tm, name):
    S = x1.shape[0]

    def body(x_ref, yc_ref, yg_ref, mod_ref, w_ref, xo_ref):
        mixo = _dot(yc_ref[...], w_ref[0:DC, :]) + _dot(yg_ref[...], w_ref[DC:DC + DG, :])
        xo_ref[...] = x_ref[...] + mod_ref[5:6, :] * mixo

    row = lambda i: (i, 0)
    return pl.pallas_call(
        body, name=name,
        grid=(S // tm,),
        in_specs=[pl.BlockSpec((tm, D), row), pl.BlockSpec((tm, DC), row), pl.BlockSpec((tm, DG), row),
                  _full(mod.shape), _full(wout.shape)],
        out_specs=pl.BlockSpec((tm, D), row),
        out_shape=jax.ShapeDtypeStruct((S, D), f32),
        compiler_params=_cp(1),
    )(x1, yc, yg, mod, wout)


def mixout_bwd(dx2, yc, yg, mod, wout, tm, name):
    S = dx2.shape[0]

    def body(dx_ref, yc_ref, yg_ref, mod_ref, w_ref, dm_ref, dyc_ref, dyg_ref, red_ref):
        @pl.when(pl.program_id(0) == 0)
        def _():
            red_ref[...] = jnp.zeros_like(red_ref)

        dxv = dx_ref[...]
        mixo = _dot(yc_ref[...], w_ref[0:DC, :]) + _dot(yg_ref[...], w_ref[DC:DC + DG, :])
        red_ref[0:1, :] += _rowsum(dxv * mixo)
        dm = (mod_ref[5:6, :] * dxv).astype(bf16)
        dm_ref[...] = dm
        dycat = _dg(dm, w_ref[...], NT)
        dyc_ref[...] = dycat[:, :DC]
        dyg_ref[...] = dycat[:, DC:]

    row = lambda i: (i, 0)
    return pl.pallas_call(
        body, name=name,
        grid=(S // tm,),
        in_specs=[pl.BlockSpec((tm, D), row), pl.BlockSpec((tm, DC), row), pl.BlockSpec((tm, DG), row),
                  _full(mod.shape), _full(wout.shape)],
        out_specs=[pl.BlockSpec((tm, D), row), pl.BlockSpec((tm, DC), row), pl.BlockSpec((tm, DG), row), _full((8, D))],
        out_shape=[jax.ShapeDtypeStruct((S, D), bf16), jax.ShapeDtypeStruct((S, DC), f32),
                   jax.ShapeDtypeStruct((S, DG), f32), jax.ShapeDtypeStruct((8, D), f32)],
        compiler_params=_cp(1),
    )(dx2, yc, yg, mod, wout)


def final_fwd_bwd(x, tgt, fmod, g, tm, name):
    S = x.shape[0]

    def body(x_ref, t_ref, fm_ref, g_ref, dx_ref, red_ref):
        @pl.when(pl.program_id(0) == 0)
        def _():
            red_ref[...] = jnp.zeros_like(red_ref)

        xh, rstd = _rms_parts(x_ref[...])
        gv = g_ref[...]
        n = xh * gv
        sc = 1.0 + fm_ref[1:2, :]
        e = n * sc + fm_ref[0:1, :] - t_ref[...]
        red_ref[0:1, :] += _rowsum(e * e) * (0.5 / D)
        dy = e * (1.0 / D)
        dn = dy * sc
        red_ref[1:2, :] += _rowsum(dy)
        red_ref[2:3, :] += _rowsum(dy * n)
        red_ref[3:4, :] += _rowsum(dn * xh)
        dx_ref[...] = _rms_bwd(dn * gv, xh, rstd)

    row = lambda i: (i, 0)
    return pl.pallas_call(
        body, name=name,
        grid=(S // tm,),
        in_specs=[pl.BlockSpec((tm, D), row), pl.BlockSpec((tm, D), row), _full(fmod.shape), _full(g.shape)],
        out_specs=[pl.BlockSpec((tm, D), row), _full((8, D))],
        out_shape=[jax.ShapeDtypeStruct((S, D), f32), jax.ShapeDtypeStruct((8, D), f32)],
        compiler_params=_cp(1),
    )(x, tgt, fmod, g)


def ada_fwd(c_all, w, b, name):
    n = w.shape[1]

    def body(c_ref, w_ref, b_ref, o_ref):
        cv = c_ref[...]
        o_ref[...] = jnp.dot(cv * _sigmoid(cv), w_ref[...], preferred_element_type=f32, precision=HIGHEST) + b_ref[...]

    return pl.pallas_call(
        body, name=name,
        in_specs=[_full(c_all.shape), _full(w.shape), _full(b.shape)],
        out_specs=_full((N_DEV, n)),
        out_shape=jax.ShapeDtypeStruct((N_DEV, n), f32),
        grid=(1,),
        compiler_params=_cp(1),
    )(c_all, w, b)


def ada_wgrad(c_all_t, dm, name):
    n = dm.shape[1]

    def body(c_ref, d_ref, o_ref):
        cv = c_ref[...]
        o_ref[...] = jnp.dot(cv * _sigmoid(cv), d_ref[...], preferred_element_type=f32, precision=HIGHEST)

    return pl.pallas_call(
        body, name=name,
        in_specs=[_full(c_all_t.shape), _full(dm.shape)],
        out_specs=_full((D, n)),
        out_shape=jax.ShapeDtypeStruct((D, n), f32),
        grid=(1,),
        compiler_params=_cp(1),
    )(c_all_t, dm)


def _adam_math(gv, wv, mv, vv):
    m = ADAM_B1 * mv + (1.0 - ADAM_B1) * gv
    v = ADAM_B2 * vv + (1.0 - ADAM_B2) * (gv * gv)
    m_hat = m / (1.0 - ADAM_B1 ** ADAM_STEP)
    v_hat = v / (1.0 - ADAM_B2 ** ADAM_STEP)
    delta = -ADAM_LR * (m_hat / (jnp.sqrt(v_hat) + ADAM_EPS) + ADAM_WD * wv)
    return delta, m, v


def adam_parts(parts, w, m, v, tr, name):
    R, C = w.shape

    def body(p_ref, w_ref, m_ref, v_ref, g_ref, d_ref, mo_ref, vo_ref):
        gv = p_ref[0].astype(f32)
        for k in range(1, N_DEV):
            gv = gv + p_ref[k].astype(f32)
        g_ref[...] = gv
        d_ref[...], mo_ref[...], vo_ref[...] = _adam_math(gv, w_ref[...], m_ref[...], v_ref[...])

    row = lambda i: (i, 0)
    spec = pl.BlockSpec((tr, C), row)
    shp = jax.ShapeDtypeStruct((R, C), f32)
    return pl.pallas_call(
        body, name=name,
        grid=(R // tr,),
        in_specs=[pl.BlockSpec((N_DEV, tr, C), lambda i: (0, i, 0)), spec, spec, spec],
        out_specs=[spec, spec, spec, spec],
        out_shape=[shp, shp, shp, shp],
        compiler_params=_cp(1),
    )(parts, w, m, v)


def adam_plain(gr, w, m, v, tr, name):
    R, C = w.shape

    def body(g_ref, w_ref, m_ref, v_ref, d_ref, mo_ref, vo_ref):
        d_ref[...], mo_ref[...], vo_ref[...] = _adam_math(g_ref[...], w_ref[...], m_ref[...], v_ref[...])

    spec = pl.BlockSpec((tr, C), lambda i: (i, 0))
    shp = jax.ShapeDtypeStruct((R, C), f32)
    return pl.pallas_call(
        body, name=name,
        grid=(R // tr,),
        in_specs=[spec, spec, spec, spec],
        out_specs=[spec, spec, spec],
        out_shape=[shp, shp, shp],
        compiler_params=_cp(1),
    )(gr, w, m, v)


def sum8(parts, name):
    _, R, C = parts.shape

    def body(p_ref, o_ref):
        acc = p_ref[0]
        for k in range(1, N_DEV):
            acc = acc + p_ref[k]
        o_ref[...] = acc

    return pl.pallas_call(
        body, name=name,
        grid=(1,),
        in_specs=[_full(parts.shape)],
        out_specs=_full((R, C)),
        out_shape=jax.ShapeDtypeStruct((R, C), f32),
        compiler_params=_cp(1),
    )(parts)


def _place():
    return lax.axis_index("x"), lax.axis_index("y"), lax.axis_index("c")


def all_gather(arrs, name):
    n = len(arrs)

    def body(*refs):
        ins, outs = refs[:n], refs[n:2 * n]
        send_sems, recv_sems, local_sems = refs[2 * n:]
        x, y, c = _place()
        me, sibling = (x, y, c), (x, y, 1 - c)
        chips = [(1 - x, y), (x, 1 - y), (1 - x, 1 - y)]

        def slot(a, p):
            return outs[a].at[4 * p[0] + 2 * p[1] + p[2]]

        def copy(a, k, block, to, src=None):
            return pltpu.make_async_remote_copy(
                src_ref=slot(a, block) if src is None else src, dst_ref=slot(a, block),
                send_sem=send_sems.at[a * 7 + k], recv_sem=recv_sems.at[a * 7 + k],
                device_id=to, device_id_type=MESH)

        mine = [pltpu.make_async_copy(ins[a], slot(a, me), local_sems.at[a]) for a in range(n)]
        for cp in mine:
            cp.start()
        first = []
        for a in range(n):
            first.append(copy(a, 0, me, sibling, src=ins[a]))
            first += [copy(a, 1 + j, me, (*chip, c), src=ins[a]) for j, chip in enumerate(chips)]
        for cp in first:
            cp.start()
        passed = []
        for j, chip in enumerate(chips):
            for a in range(n):
                copy(a, 1 + j, (*chip, c), me).wait_recv()
                fw = copy(a, 4 + j, (*chip, c), sibling)
                fw.start()
                passed.append(fw)
        for a in range(n):
            copy(a, 0, sibling, me).wait_recv()
            for j, chip in enumerate(chips):
                copy(a, 4 + j, (*chip, 1 - c), me).wait_recv()
        for cp in first + passed:
            cp.wait_send()
        for cp in mine:
            cp.wait()

    any_spec = pl.BlockSpec(memory_space=pl.ANY)
    return pl.pallas_call(
        body, name=name,
        in_specs=[any_spec] * n,
        out_specs=[any_spec] * n,
        out_shape=[jax.ShapeDtypeStruct((N_DEV,) + a.shape, a.dtype) for a in arrs],
        scratch_shapes=[pltpu.SemaphoreType.DMA((7 * n,)), pltpu.SemaphoreType.DMA((7 * n,)),
                        pltpu.SemaphoreType.DMA((n,))],
    )(*arrs)


def all_to_all(arrs, name):
    n = len(arrs)

    def body(*refs):
        ins, outs = refs[:n], refs[n:2 * n]
        send_sems, recv_sems, local_sems = refs[2 * n:]
        x, y, c = _place()
        me_i = 4 * x + 2 * y + c
        mine = [pltpu.make_async_copy(ins[a].at[me_i], outs[a].at[me_i], local_sems.at[a]) for a in range(n)]
        for cp in mine:
            cp.start()
        sends, recvs = [], []
        for k in range(1, N_DEV):
            px = 1 - x if (k >> 2) & 1 else x
            py = 1 - y if (k >> 1) & 1 else y
            pc = 1 - c if k & 1 else c
            p_i = 4 * px + 2 * py + pc
            for a in range(n):
                sem = a * 7 + k - 1
                sends.append(pltpu.make_async_remote_copy(
                    src_ref=ins[a].at[p_i], dst_ref=outs[a].at[me_i],
                    send_sem=send_sems.at[sem], recv_sem=recv_sems.at[sem],
                    device_id=(px, py, pc), device_id_type=MESH))
                recvs.append(pltpu.make_async_remote_copy(
                    src_ref=ins[a].at[p_i], dst_ref=outs[a].at[p_i],
                    send_sem=send_sems.at[sem], recv_sem=recv_sems.at[sem],
                    device_id=(px, py, pc), device_id_type=MESH))
        for cp in sends:
            cp.start()
        for cp in recvs:
            cp.wait_recv()
        for cp in sends:
            cp.wait_send()
        for cp in mine:
            cp.wait()

    any_spec = pl.BlockSpec(memory_space=pl.ANY)
    return pl.pallas_call(
        body, name=name,
        in_specs=[any_spec] * n,
        out_specs=[any_spec] * n,
        out_shape=[jax.ShapeDtypeStruct(a.shape, a.dtype) for a in arrs],
        scratch_shapes=[pltpu.SemaphoreType.DMA((7 * n,)), pltpu.SemaphoreType.DMA((7 * n,)),
                        pltpu.SemaphoreType.DMA((n,))],
    )(*arrs)


def _tiles(S):
    t = min(512, S)
    return dict(ffn_fwd=t, ffn_bwd=min(256, S), row=t, conv=t, gla=t, tf=F // 2, bk=min(1024, S))


def local_step(x, tgt, mods, fmod, wts):
    S = x.shape[0]
    T = _tiles(S)
    tf, bk = T["tf"], T["bk"]
    saved = []
    xc = x
    for l in range(DEPTH):
        w = wts[f"L{l}"]
        x0 = xc
        x1, zg1, zu1, f1 = ffn_fwd(x0, mods[l], w["g1"], w["wg1"], w["wu1"], w["wo1"], (0, 1, 2), T["ffn_fwd"], tf, f"ffn1_fwd_{l}")
        z, la = mixin_fwd(x1, mods[l], w["g2"], w["win"], w["wgu"], w["bgate"], T["row"], f"mixin_fwd_{l}")
        y, yc = conv_fwd(z, w["wdw"], w["cpar"], T["conv"], f"conv_fwd_{l}")
        yg, sprev = gla_fwd(z, la, w["gn_s"], T["gla"], f"gla_fwd_{l}")
        x2 = mixout_fwd(x1, yc, yg, mods[l], w["wout"], T["row"], f"mixout_fwd_{l}")
        x3, zg2, zu2, f2 = ffn_fwd(x2, mods[l], w["g3"], w["wg2"], w["wu2"], w["wo2"], (6, 7, 8), T["ffn_fwd"], tf, f"ffn2_fwd_{l}")
        saved.append(dict(x0=x0, x1=x1, x2=x2, zg1=zg1, zu1=zu1, f1=f1, zg2=zg2, zu2=zu2, f2=f2,
                          z=z, la=la, y=y, yc=yc, yg=yg, sprev=sprev))
        xc = x3

    dx, redf = final_fwd_bwd(xc, tgt, fmod, wts["gf"], T["row"], "loss_head")
    loss_lanes = redf[0]
    dfmod = redf[1:3]
    grads = {"gf": redf[3]}
    dmods = [None] * DEPTH

    def wgrad(a, b, M, N, name, **kw):
        bn = N // 2 if N % 256 == 0 and N > 1408 else N
        return matmul_tn(a, b, M, N, M, bn, bk, name, **kw)

    for l in reversed(range(DEPTH)):
        w, sv = wts[f"L{l}"], saved[l]
        g = {}
        dx2, h, df, a, dzg, dzu, red3 = ffn_bwd(sv["x2"], dx, sv["zg2"], sv["zu2"], sv["f2"], mods[l], w["g3"],
                                                w["wg2"], w["wu2"], w["wo2"], (6, 7, 8), T["ffn_bwd"], tf, f"ffn2_bwd_{l}")
        g["wg2"] = wgrad(h, dzg, D, F, f"dwg2_{l}")
        g["wu2"] = wgrad(h, dzu, D, F, f"dwu2_{l}")
        g["wo2"] = wgrad(a, df, F, D, f"dwo2_{l}")
        dmix, dyc, dyg, red_o = mixout_bwd(dx2, sv["yc"], sv["yg"], mods[l], w["wout"], T["row"], f"mixout_bwd_{l}")
        g["wout"] = jnp.concatenate([wgrad(sv["yc"], dmix, DC, D, f"dwout_c_{l}"),
                                     wgrad(sv["yg"], dmix, DG, D, f"dwout_g_{l}")], axis=0)
        dq, dk, dv, dr, dpre, redg, redb = gla_bwd(sv["z"], sv["la"], sv["sprev"], dyg, w["gn_s"], T["gla"], f"gla_bwd_{l}")
        dzab, redc = conv_bwd(sv["z"], sv["y"], dyc, w["wdw"], w["cpar"], T["conv"], f"conv_bwd_{l}")
        dx1, h2, dz, red2 = mixin_bwd(sv["x1"], dx2, dzab, dq, dk, dv, dr, dpre, mods[l], w["g2"], w["win"], w["wgu"],
                                      T["row"], f"mixin_bwd_{l}")
        g["win"] = wgrad(h2, dz, D, DINP, f"dwin_{l}")[:, :DIN]
        g["wgu"] = matmul_tn(sv["z"], dpre, 128, DQK, 128, DQK, bk, f"dwgu_{l}", a_col_block=(DINP - 128) // 128)[:GR]
        g["bgate"] = jnp.sum(redb, axis=0)
        g["gn"] = jnp.sum(redg.reshape(NH, CH, DV), axis=1)
        g["wdw"] = redc[:CW]
        g["bdw"], g["gln"], g["bln"] = redc[32], redc[33], redc[34]
        dx0, h, df, a, dzg, dzu, red1 = ffn_bwd(sv["x0"], dx1, sv["zg1"], sv["zu1"], sv["f1"], mods[l], w["g1"],
                                                w["wg1"], w["wu1"], w["wo1"], (0, 1, 2), T["ffn_bwd"], tf, f"ffn1_bwd_{l}")
        g["wg1"] = wgrad(h, dzg, D, F, f"dwg1_{l}")
        g["wu1"] = wgrad(h, dzu, D, F, f"dwu1_{l}")
        g["wo1"] = wgrad(a, df, F, D, f"dwo1_{l}")
        g["g1"], g["g2"], g["g3"] = red1[3], red2[2], red3[3]
        dmods[l] = jnp.stack([red1[0], red1[1], red1[2], red2[0], red2[1], red_o[0], red3[0], red3[1], red3[2]], axis=0)
        grads[f"L{l}"] = g
        dx = dx0
    return loss_lanes, dx, grads, dmods, dfmod


def _col_shards_to_full(gathered):
    n, r, c = gathered.shape
    return jnp.transpose(gathered, (1, 0, 2)).reshape(r, n * c)


def _full_to_col_shards(full):
    r, nc = full.shape
    return jnp.transpose(full.reshape(r, N_DEV, nc // N_DEV), (1, 0, 2))


def _pad_rows(a, rows):
    return jnp.pad(a, ((0, rows - a.shape[0]), (0, 0)))


def kernel(x, c, w_ada, b_ada, g_norm_ffn1, w_ffn1_in, w_ffn1_out, g_norm_mix, w_in, w_dw, b_dw, g_conv_ln, b_conv_ln, w_gate_up, b_gate, g_gla_norm, w_out, g_norm_ffn2, w_ffn2_in, w_ffn2_out, g_norm_final, w_ada_final, b_ada_final, loss_target, m_w_ada, m_b_ada, m_g_norm_ffn1, m_w_ffn1_in, m_w_ffn1_out, m_g_norm_mix, m_w_in, m_w_dw, m_b_dw, m_g_conv_ln, m_b_conv_ln, m_w_gate_up, m_b_gate, m_g_gla_norm, m_w_out, m_g_norm_ffn2, m_w_ffn2_in, m_w_ffn2_out, m_g_norm_final, m_w_ada_final, m_b_ada_final, v_w_ada, v_b_ada, v_g_norm_ffn1, v_w_ffn1_in, v_w_ffn1_out, v_g_norm_mix, v_w_in, v_w_dw, v_b_dw, v_g_conv_ln, v_b_conv_ln, v_w_gate_up, v_b_gate, v_g_gla_norm, v_w_out, v_g_norm_ffn2, v_w_ffn2_in, v_w_ffn2_out, v_g_norm_final, v_w_ada_final, v_b_ada_final):
    me = 4 * lax.axis_index("x") + 2 * lax.axis_index("y") + lax.axis_index("c")
    L = DEPTH
    n_ada = N_MOD * D // N_DEV
    n_fin = 2 * D // N_DEV

    small = jnp.concatenate([c.reshape(-1), w_dw.reshape(-1), w_gate_up.reshape(-1)])
    n_small = small.shape[0]
    small = jnp.pad(small, (0, 8 * D - n_small)).reshape(8, D)
    big = [w_ffn1_in, w_ffn1_out, w_in, w_out, w_ffn2_in, w_ffn2_out]
    gathered = all_gather([a.astype(bf16) for a in big] + [small], "gather_weights")
    wi1_a, wo1_a, win_a, wout_a, wi2_a, wo2_a, small_a = gathered
    small_a = small_a.reshape(N_DEV, 8 * D)
    c_all = small_a[:, :D]
    o1 = D + L * CW * (DC // N_DEV)
    wdw_full = _col_shards_to_full(small_a[:, D:o1].reshape(N_DEV, L * CW, DC // N_DEV)).reshape(L, CW, DC)
    wgu_full = _col_shards_to_full(small_a[:, o1:o1 + L * GR * (DQK // N_DEV)].reshape(N_DEV, L * GR, DQK // N_DEV)).reshape(L, GR, DQK)

    b_ada_mine = lax.dynamic_slice(b_ada, (0, me * n_ada), (L, n_ada))
    b_fin_mine = lax.dynamic_slice(b_ada_final, (me * n_fin,), (n_fin,))
    parts = [ada_fwd(c_all, w_ada[l], b_ada_mine[l:l + 1], f"ada_fwd_{l}") for l in range(L)]
    parts.append(ada_fwd(c_all, w_ada_final, b_fin_mine.reshape(1, n_fin), "ada_fwd_final"))
    modsrc = jnp.concatenate(parts, axis=1)
    n_row = modsrc.shape[1]
    modsrc = jnp.pad(modsrc, ((0, 0), (0, 24 * 128 - n_row))).reshape(N_DEV, 24, 128)
    (modrecv,) = all_to_all([modsrc], "exchange_mod")
    modrecv = modrecv.reshape(N_DEV, 24 * 128)
    mods = []
    for l in range(L):
        mvec = modrecv[:, l * n_ada:(l + 1) * n_ada].reshape(N_MOD, D)
        mods.append(_pad_rows(mvec, 16))
    fmod = _pad_rows(modrecv[:, L * n_ada:L * n_ada + n_fin].reshape(2, D), 8)

    wts = {"gf": g_norm_final.reshape(1, D)}
    for l in range(L):
        wi1 = _col_shards_to_full(wi1_a[:, l])
        wi2 = _col_shards_to_full(wi2_a[:, l])
        win_full = _col_shards_to_full(win_a[:, l])
        wts[f"L{l}"] = dict(
            g1=g_norm_ffn1[l].reshape(1, D), g2=g_norm_mix[l].reshape(1, D), g3=g_norm_ffn2[l].reshape(1, D),
            wg1=wi1[:, :F], wu1=wi1[:, F:], wo1=wo1_a[:, l].reshape(F, D),
            wg2=wi2[:, :F], wu2=wi2[:, F:], wo2=wo2_a[:, l].reshape(F, D),
            win=jnp.pad(win_full, ((0, 0), (0, DINP - DIN))),
            wout=wout_a[:, l].reshape(D, D),
            wgu=_pad_rows(wgu_full[l], 128).astype(bf16),
            bgate=b_gate[l].reshape(1, DQK),
            wdw=_pad_rows(wdw_full[l], 32),
            cpar=_pad_rows(jnp.stack([b_dw[l], g_conv_ln[l], b_conv_ln[l]]), 8),
            gn_s=jnp.repeat(g_gla_norm[l], CH, axis=0),
        )

    loss_lanes, grad_x, gr, dmods, dfmod = local_step(x[0], loss_target[0], mods, fmod, wts)

    def col_pieces(name_g, name_u=None):
        per_layer = []
        for l in range(L):
            full = gr[f"L{l}"][name_g] if name_u is None else jnp.concatenate([gr[f"L{l}"][name_g], gr[f"L{l}"][name_u]], axis=1)
            per_layer.append(_full_to_col_shards(full.astype(bf16)))
        return jnp.stack(per_layer, axis=1)

    def row_pieces(name):
        per_layer = [gr[f"L{l}"][name].astype(bf16).reshape(N_DEV, -1, D) for l in range(L)]
        return jnp.stack(per_layer, axis=1)

    pieces = [col_pieces("wg1", "wu1"), row_pieces("wo1"), col_pieces("win"), row_pieces("wout"),
              col_pieces("wg2", "wu2"), row_pieces("wo2")]
    recv = all_to_all(pieces, "exchange_grads")

    def adam_big(rv, w, m, v, name):
        shp = w.shape
        R, C = shp[0] * shp[1], shp[2]
        tr = 256 if R % 256 == 0 else R // 2
        outs = adam_parts(rv.reshape(N_DEV, R, C), w.reshape(R, C), m.reshape(R, C), v.reshape(R, C), tr, name)
        return [o.reshape(shp) for o in outs]

    res = {}
    res["w_ffn1_in"] = adam_big(recv[0], w_ffn1_in, m_w_ffn1_in, v_w_ffn1_in, "adam_ffn1_in")
    res["w_ffn1_out"] = adam_big(recv[1], w_ffn1_out, m_w_ffn1_out, v_w_ffn1_out, "adam_ffn1_out")
    res["w_in"] = adam_big(recv[2], w_in, m_w_in, v_w_in, "adam_w_in")
    res["w_out"] = adam_big(recv[3], w_out, m_w_out, v_w_out, "adam_w_out")
    res["w_ffn2_in"] = adam_big(recv[4], w_ffn2_in, m_w_ffn2_in, v_w_ffn2_in, "adam_ffn2_in")
    res["w_ffn2_out"] = adam_big(recv[5], w_ffn2_out, m_w_ffn2_out, v_w_ffn2_out, "adam_ffn2_out")

    flat = lambda name: jnp.stack([gr[f"L{l}"][name] for l in range(L)]).reshape(-1)
    sections = [
        ("b_ada", jnp.stack(dmods).reshape(-1)), ("b_ada_final", dfmod.reshape(-1)),
        ("g_norm_ffn1", flat("g1")), ("g_norm_mix", flat("g2")), ("g_norm_ffn2", flat("g3")), ("g_norm_final", gr["gf"]),
        ("b_dw", flat("bdw")), ("g_conv_ln", flat("gln")), ("b_conv_ln", flat("bln")), ("b_gate", flat("bgate")),
        ("g_gla_norm", flat("gn")),
    ]
    n_rep = sum(s[1].shape[0] for s in sections)
    rep_rows = -(-n_rep // D)
    extra = [("loss", loss_lanes), ("w_dw", flat("wdw")), ("w_gate_up", flat("wgu"))]
    pack = jnp.concatenate([s[1] for s in sections] + [jnp.zeros((rep_rows * D - n_rep,), f32)] + [s[1] for s in extra])
    n_pack = pack.shape[0]
    pack_rows = -(-n_pack // (8 * D)) * 8
    pack = jnp.pad(pack, (0, pack_rows * D - n_pack)).reshape(pack_rows, D)
    (pack_all,) = all_gather([pack], "gather_small_grads")
    tot = sum8(pack_all, "sum_small_grads")
    tot_flat = tot.reshape(-1)
    loss = jnp.sum(tot_flat[rep_rows * D:rep_rows * D + D])
    o_dw = rep_rows * D + D
    g_wdw_full = tot_flat[o_dw:o_dw + L * CW * DC].reshape(L, CW, DC)
    o_gu = o_dw + L * CW * DC
    g_wgu_full = tot_flat[o_gu:o_gu + L * GR * DQK].reshape(L, GR, DQK)

    small_params = dict(b_ada=(b_ada, m_b_ada, v_b_ada), b_ada_final=(b_ada_final, m_b_ada_final, v_b_ada_final),
                        g_norm_ffn1=(g_norm_ffn1, m_g_norm_ffn1, v_g_norm_ffn1), g_norm_mix=(g_norm_mix, m_g_norm_mix, v_g_norm_mix),
                        g_norm_ffn2=(g_norm_ffn2, m_g_norm_ffn2, v_g_norm_ffn2), g_norm_final=(g_norm_final, m_g_norm_final, v_g_norm_final),
                        b_dw=(b_dw, m_b_dw, v_b_dw), g_conv_ln=(g_conv_ln, m_g_conv_ln, v_g_conv_ln),
                        b_conv_ln=(b_conv_ln, m_b_conv_ln, v_b_conv_ln), b_gate=(b_gate, m_b_gate, v_b_gate),
                        g_gla_norm=(g_gla_norm, m_g_gla_norm, v_g_gla_norm))

    def rep_pack(idx):
        p = jnp.concatenate([small_params[s[0]][idx].reshape(-1) for s in sections])
        return jnp.pad(p, (0, rep_rows * D - n_rep)).reshape(rep_rows, D)

    g_rep = tot[:rep_rows]
    d_rep, m_rep, v_rep = adam_plain(g_rep, rep_pack(0), rep_pack(1), rep_pack(2), rep_rows, "adam_small")
    off = 0
    for sname, sval in sections:
        shp = small_params[sname][0].shape
        nel = sval.shape[0]
        res[sname] = [a.reshape(-1)[off:off + nel].reshape(shp) for a in (g_rep, d_rep, m_rep, v_rep)]
        off += nel

    def adam_cols(g_full, w, m, v, name):
        shp = w.shape
        g_mine = lax.dynamic_slice(g_full, (0, 0, me * shp[2]), shp)
        R, C = shp[0] * shp[1], shp[2]
        outs = adam_plain(g_mine.reshape(R, C), w.reshape(R, C), m.reshape(R, C), v.reshape(R, C), R, name)
        return [g_mine] + [o.reshape(shp) for o in outs]

    res["w_dw"] = adam_cols(g_wdw_full, w_dw, m_w_dw, v_w_dw, "adam_w_dw")
    res["w_gate_up"] = adam_cols(g_wgu_full, w_gate_up, m_w_gate_up, v_w_gate_up, "adam_w_gate_up")

    c_all_t = c_all.T
    dmod_all = pack_all.reshape(N_DEV, -1)[:, :L * N_MOD * D].reshape(N_DEV, L, N_MOD * D)
    dfm_all = pack_all.reshape(N_DEV, -1)[:, L * N_MOD * D:L * N_MOD * D + 2 * D]
    dm_mine = lax.dynamic_slice(dmod_all, (0, 0, me * n_ada), (N_DEV, L, n_ada))
    dfm_mine = lax.dynamic_slice(dfm_all, (0, me * n_fin), (N_DEV, n_fin))
    g_w_ada = jnp.stack([ada_wgrad(c_all_t, dm_mine[:, l], f"ada_wgrad_{l}") for l in range(L)])
    g_w_fin = ada_wgrad(c_all_t, dfm_mine, "ada_wgrad_final")
    outs = adam_plain(g_w_ada.reshape(L * D, n_ada), w_ada.reshape(L * D, n_ada), m_w_ada.reshape(L * D, n_ada),
                      v_w_ada.reshape(L * D, n_ada), 256, "adam_w_ada")
    res["w_ada"] = [g_w_ada] + [o.reshape(w_ada.shape) for o in outs]
    res["w_ada_final"] = [g_w_fin] + list(adam_plain(g_w_fin, w_ada_final, m_w_ada_final, v_w_ada_final, 256, "adam_w_ada_final"))

    order = ["w_ada", "b_ada", "g_norm_ffn1", "w_ffn1_in", "w_ffn1_out", "g_norm_mix", "w_in", "w_dw", "b_dw", "g_conv_ln",
             "b_conv_ln", "w_gate_up", "b_gate", "g_gla_norm", "w_out", "g_norm_ffn2", "w_ffn2_in", "w_ffn2_out",
             "g_norm_final", "w_ada_final", "b_ada_final"]
    out = [loss, grad_x[None]]
    for k in range(4):
        out += [res[name][k] for name in order]
    return tuple(out)
```

```python
import functools

import jax
import jax.numpy as jnp
from jax import lax
from jax.experimental import pallas as pl
from jax.experimental.pallas import tpu as pltpu

f32 = jnp.float32
bf16 = jnp.bfloat16

N_DEV = 8
DEPTH = 2
D = 1024
F = 2816
DC = 512
NH = 4
DK = 64
DV = 128
DQK = NH * DK
DG = NH * DV
CH = 64
CW = 31
GR = 16
TAU = 16.0
N_MOD = 9
DIN = 2 * DC + 2 * DQK + 2 * DG + GR
DINP = 2688
EPS = 1e-6
HALO = 32
SUBLANES = 8
NFS = 4
FS = F // NFS

ADAM_LR = 0.001
ADAM_B1 = 0.9
ADAM_B2 = 0.999
ADAM_EPS = 1e-08
ADAM_WD = 0.01
ADAM_STEP = 10

V7X_VMEM_LIMIT = 56 * 1024 * 1024
MESH = pl.DeviceIdType.MESH
HIGHEST = lax.Precision.HIGHEST

NT = (((1,), (1,)), ((), ()))
TN = (((0,), (0,)), ((), ()))


def _cp(n_axes):
    return pltpu.CompilerParams(dimension_semantics=("arbitrary",) * n_axes, vmem_limit_bytes=V7X_VMEM_LIMIT)


def _full(shape):
    nd = len(shape)
    return pl.BlockSpec(shape, lambda *_: (0,) * nd)


def _dot(a, b):
    return jnp.dot(a, b, preferred_element_type=f32)


def _dg(a, b, dims):
    return lax.dot_general(a, b, dims, preferred_element_type=f32)


def _sigmoid(x):
    return jax.nn.sigmoid(x)


def _rowsum(x):
    return jnp.sum(x, axis=0, keepdims=True)


def _rms_parts(xv):
    rstd = lax.rsqrt(jnp.mean(xv * xv, axis=-1, keepdims=True) + EPS)
    return xv * rstd, rstd


def _rms_bwd(dxh, xh, rstd):
    return rstd * (dxh - xh * jnp.mean(dxh * xh, axis=-1, keepdims=True))


def ffn_fwd(x, mod, g, wi, wo, layer, rows, tm, name):
    S = x.shape[0]
    nj = NFS
    tf = FS
    r_shift, r_scale, r_gate = rows

    def body(x_ref, mod_ref, g_ref, wg_ref, wu_ref, wo_ref, xo_ref, zg_ref, zu_ref, f_ref, h_s, acc_s):
        j = pl.program_id(1)

        @pl.when(j == 0)
        def _():
            xh, _ = _rms_parts(x_ref[...])
            hv = xh * g_ref[...] * (1.0 + mod_ref[r_scale:r_scale + 1, :]) + mod_ref[r_shift:r_shift + 1, :]
            h_s[...] = hv.astype(bf16)
            acc_s[...] = jnp.zeros_like(acc_s)

        h = h_s[...]
        zg = _dot(h, wg_ref[...])
        zu = _dot(h, wu_ref[...])
        zg_ref[...] = zg.astype(bf16)
        zu_ref[...] = zu.astype(bf16)
        a = zg * _sigmoid(zg) * zu
        acc_s[...] += _dot(a.astype(bf16), wo_ref[...])

        @pl.when(j == nj - 1)
        def _():
            fv = acc_s[...]
            f_ref[...] = fv.astype(bf16)
            xo_ref[...] = x_ref[...] + 0.5 * mod_ref[r_gate:r_gate + 1, :] * fv

    return pl.pallas_call(
        body, name=name,
        grid=(S // tm, nj),
        in_specs=[
            pl.BlockSpec((tm, D), lambda i, j: (i, 0)),
            _full(mod.shape), _full(g.shape),
            pl.BlockSpec((None, None, D, tf), lambda i, j: (j, layer, 0, 0)),
            pl.BlockSpec((None, None, D, tf), lambda i, j: (j + NFS, layer, 0, 0)),
            pl.BlockSpec((tf, D), lambda i, j: (j, 0)),
        ],
        out_specs=[
            pl.BlockSpec((tm, D), lambda i, j: (i, 0)),
            pl.BlockSpec((None, tm, tf), lambda i, j: (j, i, 0)),
            pl.BlockSpec((None, tm, tf), lambda i, j: (j, i, 0)),
            pl.BlockSpec((tm, D), lambda i, j: (i, 0)),
        ],
        out_shape=[
            jax.ShapeDtypeStruct((S, D), f32),
            jax.ShapeDtypeStruct((NFS, S, tf), bf16),
            jax.ShapeDtypeStruct((NFS, S, tf), bf16),
            jax.ShapeDtypeStruct((S, D), bf16),
        ],
        scratch_shapes=[pltpu.VMEM((tm, D), bf16), pltpu.VMEM((tm, D), f32)],
        compiler_params=_cp(2),
    )(x, mod, g, wi, wi, wo)


def ffn_bwd(x, dy, zg, zu, fo, mod, g, wi, wo, layer, rows, tm, name):
    S = x.shape[0]
    nj = NFS
    tf = FS
    r_shift, r_scale, r_gate = rows

    def body(x_ref, dy_ref, zg_ref, zu_ref, f_ref, mod_ref, g_ref, wg_ref, wu_ref, wo_ref,
             dx_ref, h_ref, df_ref, a_ref, dzg_ref, dzu_ref, red_ref, acc_s, df_s):
        i = pl.program_id(0)
        j = pl.program_id(1)

        @pl.when((i == 0) & (j == 0))
        def _():
            red_ref[...] = jnp.zeros_like(red_ref)

        @pl.when(j == 0)
        def _():
            xh, _ = _rms_parts(x_ref[...])
            hv = xh * g_ref[...] * (1.0 + mod_ref[r_scale:r_scale + 1, :]) + mod_ref[r_shift:r_shift + 1, :]
            h_ref[...] = hv.astype(bf16)
            df = (0.5 * mod_ref[r_gate:r_gate + 1, :] * dy_ref[...]).astype(bf16)
            df_s[...] = df
            df_ref[...] = df
            acc_s[...] = jnp.zeros_like(acc_s)

        zgv = zg_ref[...].astype(f32)
        zuv = zu_ref[...].astype(f32)
        s = _sigmoid(zgv)
        sil = zgv * s
        a_ref[...] = (sil * zuv).astype(bf16)
        da = _dg(df_s[...], wo_ref[...], NT)
        dzu = (da * sil).astype(bf16)
        dzg = (da * zuv * (s * (1.0 + zgv * (1.0 - s)))).astype(bf16)
        dzg_ref[...] = dzg
        dzu_ref[...] = dzu
        acc_s[...] += _dg(dzg, wg_ref[...], NT) + _dg(dzu, wu_ref[...], NT)

        @pl.when(j == nj - 1)
        def _():
            dh = acc_s[...]
            dyv = dy_ref[...]
            xh, rstd = _rms_parts(x_ref[...])
            gv = g_ref[...]
            n = xh * gv
            dn = dh * (1.0 + mod_ref[r_scale:r_scale + 1, :])
            red_ref[0:1, :] += _rowsum(dh)
            red_ref[1:2, :] += _rowsum(dh * n)
            red_ref[2:3, :] += _rowsum(0.5 * f_ref[...].astype(f32) * dyv)
            red_ref[3:4, :] += _rowsum(dn * xh)
            dx_ref[...] = dyv + _rms_bwd(dn * gv, xh, rstd)

    row = lambda i, j: (i, 0)
    tile = lambda i, j: (j, i, 0)
    shard = jax.ShapeDtypeStruct((NFS, S, tf), bf16)
    return pl.pallas_call(
        body, name=name,
        grid=(S // tm, nj),
        in_specs=[
            pl.BlockSpec((tm, D), row), pl.BlockSpec((tm, D), row),
            pl.BlockSpec((None, tm, tf), tile), pl.BlockSpec((None, tm, tf), tile),
            pl.BlockSpec((tm, D), row),
            _full(mod.shape), _full(g.shape),
            pl.BlockSpec((None, None, D, tf), lambda i, j: (j, layer, 0, 0)),
            pl.BlockSpec((None, None, D, tf), lambda i, j: (j + NFS, layer, 0, 0)),
            pl.BlockSpec((tf, D), lambda i, j: (j, 0)),
        ],
        out_specs=[
            pl.BlockSpec((tm, D), row), pl.BlockSpec((tm, D), row), pl.BlockSpec((tm, D), row),
            pl.BlockSpec((None, tm, tf), tile), pl.BlockSpec((None, tm, tf), tile), pl.BlockSpec((None, tm, tf), tile),
            _full((8, D)),
        ],
        out_shape=[
            jax.ShapeDtypeStruct((S, D), f32), jax.ShapeDtypeStruct((S, D), bf16), jax.ShapeDtypeStruct((S, D), bf16),
            shard, shard, shard,
            jax.ShapeDtypeStruct((8, D), f32),
        ],
        scratch_shapes=[pltpu.VMEM((tm, D), f32), pltpu.VMEM((tm, D), bf16)],
        compiler_params=_cp(2),
    )(x, dy, zg, zu, fo, mod, g, wi, wi, wo)


def matmul_tn(a, b, M, N, bm, bn, bk, name, a_col_block=0, out_dtype=f32):
    S = b.shape[0]
    nk = S // bk

    def body(a_ref, b_ref, o_ref, acc_s):
        k = pl.program_id(2)

        @pl.when(k == 0)
        def _():
            acc_s[...] = jnp.zeros_like(acc_s)

        acc_s[...] += _dg(a_ref[...].astype(bf16), b_ref[...].astype(bf16), TN)

        @pl.when(k == nk - 1)
        def _():
            o_ref[...] = acc_s[...].astype(out_dtype)

    return pl.pallas_call(
        body, name=name,
        grid=(M // bm, N // bn, nk),
        in_specs=[
            pl.BlockSpec((bk, bm), lambda i, j, k: (k, i + a_col_block)),
            pl.BlockSpec((bk, bn), lambda i, j, k: (k, j)),
        ],
        out_specs=pl.BlockSpec((bm, bn), lambda i, j, k: (i, j)),
        out_shape=jax.ShapeDtypeStruct((M, N), out_dtype),
        scratch_shapes=[pltpu.VMEM((bm, bn), f32)],
        compiler_params=_cp(3),
    )(a, b)


def dwi_pieces(h, dzg, dzu, bk, name):
    S = h.shape[0]
    nk = S // bk

    def body(h_ref, g_ref, u_ref, o_ref, acc_s):
        j = pl.program_id(0)
        k = pl.program_id(1)

        @pl.when(k == 0)
        def _():
            acc_s[...] = jnp.zeros_like(acc_s)

        @pl.when(j < NFS)
        def _():
            acc_s[...] += _dg(h_ref[...], g_ref[...], TN)

        @pl.when(j >= NFS)
        def _():
            acc_s[...] += _dg(h_ref[...], u_ref[...], TN)

        @pl.when(k == nk - 1)
        def _():
            o_ref[...] = acc_s[...].astype(bf16)

    return pl.pallas_call(
        body, name=name,
        grid=(2 * NFS, nk),
        in_specs=[
            pl.BlockSpec((bk, D), lambda j, k: (k, 0)),
            pl.BlockSpec((None, bk, FS), lambda j, k: (jnp.minimum(j, NFS - 1), jnp.where(j < NFS, k, nk - 1), 0)),
            pl.BlockSpec((None, bk, FS), lambda j, k: (jnp.maximum(j - NFS, 0), jnp.where(j >= NFS, k, 0), 0)),
        ],
        out_specs=pl.BlockSpec((None, D, FS), lambda j, k: (j, 0, 0)),
        out_shape=jax.ShapeDtypeStruct((2 * NFS, D, FS), bf16),
        scratch_shapes=[pltpu.VMEM((D, FS), f32)],
        compiler_params=_cp(2),
    )(h, dzg, dzu)


def dwo_pieces(a, df, bk, name):
    S = df.shape[0]
    nk = S // bk

    def body(a_ref, d_ref, o_ref, acc_s):
        k = pl.program_id(1)

        @pl.when(k == 0)
        def _():
            acc_s[...] = jnp.zeros_like(acc_s)

        acc_s[...] += _dg(a_ref[...], d_ref[...], TN)

        @pl.when(k == nk - 1)
        def _():
            o_ref[...] = acc_s[...].astype(bf16)

    return pl.pallas_call(
        body, name=name,
        grid=(NFS, nk),
        in_specs=[pl.BlockSpec((None, bk, FS), lambda j, k: (j, k, 0)), pl.BlockSpec((bk, D), lambda j, k: (k, 0))],
        out_specs=pl.BlockSpec((FS, D), lambda j, k: (j, 0)),
        out_shape=jax.ShapeDtypeStruct((F, D), bf16),
        scratch_shapes=[pltpu.VMEM((FS, D), f32)],
        compiler_params=_cp(2),
    )(a, df)


def mixin_fwd(x1, mod, g, win, wgu, bgate, tm, name):
    S = x1.shape[0]

    def body(x_ref, mod_ref, g_ref, win_ref, wgu_ref, bg_ref, z_ref, la_ref):
        xh, _ = _rms_parts(x_ref[...])
        hv = xh * g_ref[...] * (1.0 + mod_ref[4:5, :]) + mod_ref[3:4, :]
        z = _dot(hv.astype(bf16), win_ref[...])
        z_ref[...] = z
        glr = z[:, DINP - 128:]
        pre = _dot(glr.astype(bf16), wgu_ref[...]) + bg_ref[...]
        la_ref[...] = (jnp.minimum(pre, 0.0) - jnp.log(1.0 + jnp.exp(-jnp.abs(pre)))) * (1.0 / TAU)

    return pl.pallas_call(
        body, name=name,
        grid=(S // tm,),
        in_specs=[pl.BlockSpec((tm, D), lambda i: (i, 0)), _full(mod.shape), _full(g.shape),
                  _full(win.shape), _full(wgu.shape), _full(bgate.shape)],
        out_specs=[pl.BlockSpec((tm, DINP), lambda i: (i, 0)), pl.BlockSpec((tm, DQK), lambda i: (i, 0))],
        out_shape=[jax.ShapeDtypeStruct((S, DINP), f32), jax.ShapeDtypeStruct((S, DQK), f32)],
        compiler_params=_cp(1),
    )(x1, mod, g, win, wgu, bgate)


def mixin_bwd(x1, dres, dzab, dq, dk, dv, dr, dpre, mod, g, win, wgu, tm, name):
    S = x1.shape[0]

    def body(x_ref, dres_ref, dzab_ref, dq_ref, dk_ref, dv_ref, dr_ref, dpre_ref, mod_ref, g_ref, win_ref, wgu_ref,
             dx_ref, h_ref, dz_ref, red_ref):
        @pl.when(pl.program_id(0) == 0)
        def _():
            red_ref[...] = jnp.zeros_like(red_ref)

        dglr = _dg(dpre_ref[...].astype(bf16), wgu_ref[...], NT)
        dz = jnp.concatenate([dzab_ref[...], dq_ref[...], dk_ref[...], dv_ref[...], dr_ref[...], dglr], axis=1).astype(bf16)
        dz_ref[...] = dz
        dh = _dg(dz, win_ref[...], NT)
        xh, rstd = _rms_parts(x_ref[...])
        gv = g_ref[...]
        n = xh * gv
        sc = 1.0 + mod_ref[4:5, :]
        h_ref[...] = (n * sc + mod_ref[3:4, :]).astype(bf16)
        dn = dh * sc
        red_ref[0:1, :] += _rowsum(dh)
        red_ref[1:2, :] += _rowsum(dh * n)
        red_ref[2:3, :] += _rowsum(dn * xh)
        dx_ref[...] = dres_ref[...] + _rms_bwd(dn * gv, xh, rstd)

    row = lambda i: (i, 0)
    return pl.pallas_call(
        body, name=name,
        grid=(S // tm,),
        in_specs=[pl.BlockSpec((tm, D), row), pl.BlockSpec((tm, D), row),
                  pl.BlockSpec((tm, 2 * DC), row), pl.BlockSpec((tm, DQK), row), pl.BlockSpec((tm, DQK), row),
                  pl.BlockSpec((tm, DG), row), pl.BlockSpec((tm, DG), row), pl.BlockSpec((tm, DQK), row),
                  _full(mod.shape), _full(g.shape), _full(win.shape), _full(wgu.shape)],
        out_specs=[pl.BlockSpec((tm, D), row), pl.BlockSpec((tm, D), row), pl.BlockSpec((tm, DINP), row), _full((8, D))],
        out_shape=[jax.ShapeDtypeStruct((S, D), f32), jax.ShapeDtypeStruct((S, D), bf16),
                   jax.ShapeDtypeStruct((S, DINP), bf16), jax.ShapeDtypeStruct((8, D), f32)],
        compiler_params=_cp(1),
    )(x1, dres, dzab, dq, dk, dv, dr, dpre, mod, g, win, wgu)


def _glu(zab):
    return zab[:, :DC] * _sigmoid(zab[:, DC:])


def _shift_copies(src_s, dst_s, tc):
    n = tc + HALO - SUBLANES
    for b in range(1, SUBLANES):
        dst_s[b, 0:n, :] = src_s[b:b + n, :]


def _shifted(src_s, dst_s, o, tc):
    b = o % SUBLANES
    a = o - b
    return src_s[a:a + tc, :] if b == 0 else dst_s[b, a:a + tc, :]


def conv_fwd(z, wdw, cpar, tc, name):
    S = z.shape[0]
    nb = tc // HALO

    def body(zc_ref, zp_ref, w_ref, cp_ref, y_ref, yc_ref, u_s, us_s):
        i = pl.program_id(0)
        up = _glu(zp_ref[...])
        u_s[0:HALO, :] = jnp.where(i > 0, up, 0.0)
        u_s[HALO:HALO + tc, :] = _glu(zc_ref[...])
        _shift_copies(u_s, us_s, tc)
        acc = jnp.zeros((tc, DC), f32)
        for w in range(CW):
            acc = acc + _shifted(u_s, us_s, HALO - (CW - 1) + w, tc) * w_ref[w:w + 1, :]
        y = acc + cp_ref[0:1, :]
        y_ref[...] = y
        yc = y - jnp.mean(y, axis=-1, keepdims=True)
        yl = yc * lax.rsqrt(jnp.mean(yc * yc, axis=-1, keepdims=True) + EPS) * cp_ref[1:2, :] + cp_ref[2:3, :]
        yc_ref[...] = (yl * _sigmoid(yl)).astype(bf16)

    return pl.pallas_call(
        body, name=name,
        grid=(S // tc,),
        in_specs=[pl.BlockSpec((tc, 2 * DC), lambda i: (i, 0)),
                  pl.BlockSpec((HALO, 2 * DC), lambda i: (jnp.maximum(i * nb - 1, 0), 0)),
                  _full(wdw.shape), _full(cpar.shape)],
        out_specs=[pl.BlockSpec((tc, DC), lambda i: (i, 0)), pl.BlockSpec((tc, DC), lambda i: (i, 0))],
        out_shape=[jax.ShapeDtypeStruct((S, DC), f32), jax.ShapeDtypeStruct((S, DC), bf16)],
        scratch_shapes=[pltpu.VMEM((HALO + tc, DC), f32), pltpu.VMEM((SUBLANES, HALO + tc, DC), f32)],
        compiler_params=_cp(1),
    )(z, z, wdw, cpar)


def conv_bwd(z, y, dyc, wdw, cpar, tc, name):
    S = z.shape[0]
    nb = tc // HALO
    nt = S // tc
    last_halo = S // HALO - 1

    def body(zc_ref, zp_ref, y_ref, yn_ref, d_ref, dn_ref, w_ref, cp_ref, dz_ref, red_ref, u_s, dy_s, us_s, dys_s):
        i = pl.program_id(0)

        @pl.when(i == 0)
        def _():
            red_ref[...] = jnp.zeros_like(red_ref)

        gl = cp_ref[1:2, :]
        bl = cp_ref[2:3, :]

        def ln_bwd(yv, dv):
            yc = yv - jnp.mean(yv, axis=-1, keepdims=True)
            rstd = lax.rsqrt(jnp.mean(yc * yc, axis=-1, keepdims=True) + EPS)
            yh = yc * rstd
            yl = yh * gl + bl
            s = _sigmoid(yl)
            dyl = dv * (s * (1.0 + yl * (1.0 - s)))
            dyh = dyl * gl
            dyv = rstd * (dyh - jnp.mean(dyh, axis=-1, keepdims=True) - yh * jnp.mean(dyh * yh, axis=-1, keepdims=True))
            return dyv, dyl, yh

        dy_c, dyl_c, yh_c = ln_bwd(y_ref[...], d_ref[...])
        dy_n, _, _ = ln_bwd(yn_ref[...], dn_ref[...])
        dy_s[0:tc, :] = dy_c
        dy_s[tc:tc + HALO, :] = jnp.where(i < nt - 1, dy_n, 0.0)
        zc = zc_ref[...]
        av = zc[:, :DC]
        sb = _sigmoid(zc[:, DC:])
        u_s[0:HALO, :] = jnp.where(i > 0, _glu(zp_ref[...]), 0.0)
        u_s[HALO:HALO + tc, :] = av * sb
        _shift_copies(u_s, us_s, tc)
        _shift_copies(dy_s, dys_s, tc)
        du = jnp.zeros((tc, DC), f32)
        for w in range(CW):
            red_ref[w:w + 1, :] += _rowsum(_shifted(u_s, us_s, HALO - (CW - 1) + w, tc) * dy_c)
            du = du + _shifted(dy_s, dys_s, CW - 1 - w, tc) * w_ref[w:w + 1, :]
        red_ref[32:33, :] += _rowsum(dy_c)
        red_ref[33:34, :] += _rowsum(dyl_c * yh_c)
        red_ref[34:35, :] += _rowsum(dyl_c)
        dz_ref[...] = jnp.concatenate([du * sb, du * av * sb * (1.0 - sb)], axis=1)

    cur = lambda i: (i, 0)
    nxt = lambda i: (jnp.minimum((i + 1) * nb, last_halo), 0)
    return pl.pallas_call(
        body, name=name,
        grid=(nt,),
        in_specs=[pl.BlockSpec((tc, 2 * DC), cur),
                  pl.BlockSpec((HALO, 2 * DC), lambda i: (jnp.maximum(i * nb - 1, 0), 0)),
                  pl.BlockSpec((tc, DC), cur), pl.BlockSpec((HALO, DC), nxt),
                  pl.BlockSpec((tc, DC), cur), pl.BlockSpec((HALO, DC), nxt),
                  _full(wdw.shape), _full(cpar.shape)],
        out_specs=[pl.BlockSpec((tc, 2 * DC), cur), _full((40, DC))],
        out_shape=[jax.ShapeDtypeStruct((S, 2 * DC), f32), jax.ShapeDtypeStruct((40, DC), f32)],
        scratch_shapes=[pltpu.VMEM((HALO + tc, DC), f32), pltpu.VMEM((tc + HALO, DC), f32),
                        pltpu.VMEM((SUBLANES, HALO + tc, DC), f32), pltpu.VMEM((SUBLANES, HALO + tc, DC), f32)],
        compiler_params=_cp(1),
    )(z, z, y, y, dyc, dyc, wdw, cpar)


def _gla_consts():
    r = lax.broadcasted_iota(jnp.int32, (CH, CH), 0)
    c = lax.broadcasted_iota(jnp.int32, (CH, CH), 1)
    tril = r >= c
    lane = lax.broadcasted_iota(jnp.int32, (CH, DQK), 1)
    masks = [(lane >= h * DK) & (lane < (h + 1) * DK) for h in range(NH)]
    r4 = lax.broadcasted_iota(jnp.int32, (DQK, DQK), 0)
    c4 = lax.broadcasted_iota(jnp.int32, (DQK, DQK), 1)
    eye4 = (r4 == c4).astype(f32)
    rs = lax.broadcasted_iota(jnp.int32, (DQK, CH), 0) & (CH - 1)
    tril4 = rs >= lax.broadcasted_iota(jnp.int32, (DQK, CH), 1)
    return tril, tril4, masks, eye4


def _stack(xv, masks):
    return jnp.concatenate([jnp.where(m, xv, 0.0) for m in masks], axis=0)


def _unstack(rv, masks):
    out = jnp.where(masks[0], rv[0:CH, :], 0.0)
    for h in range(1, NH):
        out = out + jnp.where(masks[h], rv[h * CH:(h + 1) * CH, :], 0.0)
    return out


def _vstack(xv):
    return jnp.concatenate([xv[:, h * DV:(h + 1) * DV] for h in range(NH)], axis=0)


def _vunstack(xv):
    return jnp.concatenate([xv[h * CH:(h + 1) * CH, :] for h in range(NH)], axis=1)


def _gla_chunk_fwd(lac, qc, kc, vc, s_all, tril, masks, tril4):
    lmat = tril.astype(f32)
    bc = jnp.dot(lmat, lac, preferred_element_type=f32, precision=HIGHEST)
    bend = bc[CH - 1:CH, :]
    eb = jnp.exp(bc)
    enb = jnp.exp(-bc)
    ed = jnp.exp(bend - bc)
    qh = qc * (DK ** -0.5)
    qf = qh * eb
    qn = qh * enb
    kn = kc * enb
    kp = kc * eb
    kd = kc * ed
    qf_s = _stack(qf, masks).astype(bf16)
    qn_s = _stack(qn, masks).astype(bf16)
    kn_b = kn.astype(bf16)
    kp_b = kp.astype(bf16)
    attf = _dg(qf_s, kn_b, NT)
    attb = _dg(qn_s, kp_b, NT)
    a_s = jnp.where(tril4, attf, attb)
    a_b = a_s.astype(bf16)
    v_b = vc.astype(bf16)
    intra = jnp.concatenate(
        [_dot(a_b[h * CH:(h + 1) * CH, :], v_b[:, h * DV:(h + 1) * DV]) for h in range(NH)], axis=0)
    o_s = intra + _dot(qf_s, s_all.astype(bf16))
    return dict(bc=bc, bend=bend, eb=eb, enb=enb, ed=ed, qf=qf, qn=qn, kn=kn, kp=kp, kd=kd,
                qf_s=qf_s, qn_s=qn_s, kn_b=kn_b, kp_b=kp_b, a_b=a_b, v_b=v_b, o_s=o_s)


def _col_from_row(row, eye4):
    return jnp.sum(eye4 * row, axis=1, keepdims=True)


def _row_from_col(col, eye4):
    return jnp.sum(eye4 * col, axis=0, keepdims=True)


def gla_fwd(z, la, gn_s, tg, name):
    S = z.shape[0]
    nc = tg // CH

    def body(q_ref, k_ref, v_ref, r_ref, la_ref, gn_ref, yg_ref, sp_ref, st):
        @pl.when(pl.program_id(0) == 0)
        def _():
            st[...] = jnp.zeros_like(st)

        tril, tril4, masks, eye4 = _gla_consts()

        def chunk(c, carry):
            r0 = pl.multiple_of(c * CH, CH)
            s0 = pl.multiple_of(c * DQK, DQK)
            s_all = st[...]
            sp_ref[pl.ds(s0, DQK), :] = s_all
            vc = v_ref[pl.ds(r0, CH), :]
            t = _gla_chunk_fwd(la_ref[pl.ds(r0, CH), :], q_ref[pl.ds(r0, CH), :], k_ref[pl.ds(r0, CH), :], vc,
                               s_all, tril, masks, tril4)
            u_all = _dg(_stack(t["kd"], masks).astype(bf16), _vstack(vc).astype(bf16), TN)
            st[...] = _col_from_row(jnp.exp(t["bend"]), eye4) * s_all + u_all
            o_s = t["o_s"]
            on = o_s * lax.rsqrt(jnp.mean(o_s * o_s, axis=-1, keepdims=True) + EPS) * gn_ref[...]
            rc = r_ref[pl.ds(r0, CH), :]
            yg_ref[pl.ds(r0, CH), :] = (_vunstack(on) * (rc * _sigmoid(rc))).astype(bf16)
            return carry

        lax.fori_loop(0, nc, chunk, 0)

    return pl.pallas_call(
        body, name=name,
        grid=(S // tg,),
        in_specs=[pl.BlockSpec((tg, DQK), lambda i: (i, 4)), pl.BlockSpec((tg, DQK), lambda i: (i, 5)),
                  pl.BlockSpec((tg, DG), lambda i: (i, 3)), pl.BlockSpec((tg, DG), lambda i: (i, 4)),
                  pl.BlockSpec((tg, DQK), lambda i: (i, 0)), _full(gn_s.shape)],
        out_specs=[pl.BlockSpec((tg, DG), lambda i: (i, 0)), pl.BlockSpec((nc * DQK, DV), lambda i: (i, 0))],
        out_shape=[jax.ShapeDtypeStruct((S, DG), bf16), jax.ShapeDtypeStruct((S // CH * DQK, DV), f32)],
        scratch_shapes=[pltpu.VMEM((DQK, DV), f32)],
        compiler_params=_cp(1),
    )(z, z, z, z, la, gn_s)


def gla_bwd(z, la, sprev, dyg, gn_s, tg, name):
    S = z.shape[0]
    nc = tg // CH
    nt = S // tg

    def body(q_ref, k_ref, v_ref, r_ref, la_ref, sp_ref, dy_ref, gn_ref,
             dq_ref, dk_ref, dv_ref, dr_ref, dpre_ref, redg_ref, redb_ref, gs):
        @pl.when(pl.program_id(0) == 0)
        def _():
            gs[...] = jnp.zeros_like(gs)
            redg_ref[...] = jnp.zeros_like(redg_ref)
            redb_ref[...] = jnp.zeros_like(redb_ref)

        tril, tril4, masks, eye4 = _gla_consts()
        umat = (lax.broadcasted_iota(jnp.int32, (CH, CH), 0) <= lax.broadcasted_iota(jnp.int32, (CH, CH), 1)).astype(f32)
        last_row = lax.broadcasted_iota(jnp.int32, (CH, DQK), 0) == CH - 1

        def chunk(tt, carry):
            c = nc - 1 - tt
            r0 = pl.multiple_of(c * CH, CH)
            s0 = pl.multiple_of(c * DQK, DQK)
            s_all = sp_ref[pl.ds(s0, DQK), :]
            lac = la_ref[pl.ds(r0, CH), :]
            vc = v_ref[pl.ds(r0, CH), :]
            rc = r_ref[pl.ds(r0, CH), :]
            t = _gla_chunk_fwd(lac, q_ref[pl.ds(r0, CH), :], k_ref[pl.ds(r0, CH), :], vc, s_all, tril, masks, tril4)
            g_all = gs[...]
            g_b = g_all.astype(bf16)
            s_b = s_all.astype(bf16)
            o_s = t["o_s"]
            rstd = lax.rsqrt(jnp.mean(o_s * o_s, axis=-1, keepdims=True) + EPS)
            oh = o_s * rstd
            gnv = gn_ref[...]
            sr = _sigmoid(rc)
            dyv = dy_ref[pl.ds(r0, CH), :]
            dr_ref[pl.ds(r0, CH), :] = dyv * _vunstack(oh * gnv) * (sr * (1.0 + rc * (1.0 - sr)))
            don = _vstack(dyv * (rc * sr))
            redg_ref[...] += don * oh
            doh = don * gnv
            do_s = rstd * (doh - oh * jnp.mean(doh * oh, axis=-1, keepdims=True))
            do_b = do_s.astype(bf16)
            v_b = t["v_b"]
            vst_b = _vstack(vc).astype(bf16)
            kd_s = _stack(t["kd"], masks).astype(bf16)
            da_s = jnp.concatenate(
                [_dg(do_b[h * CH:(h + 1) * CH, :], v_b[:, h * DV:(h + 1) * DV], NT) for h in range(NH)], axis=0)
            a_b = t["a_b"]
            dv_s = jnp.concatenate(
                [_dg(a_b[h * CH:(h + 1) * CH, :], do_b[h * CH:(h + 1) * CH, :], TN) for h in range(NH)], axis=0)
            dv_s = dv_s + _dot(kd_s, g_b)
            dv_ref[pl.ds(r0, CH), :] = _vunstack(dv_s)
            gend = jnp.exp(t["bend"])
            gcol = _col_from_row(gend, eye4)
            gs[...] = gcol * g_all + _dg(t["qf_s"], do_b, TN)
            dgcol = jnp.sum(g_all * s_all, axis=1, keepdims=True)
            dbend = _row_from_col(dgcol * gcol, eye4)
            dkd = _unstack(_dg(vst_b, g_b, NT), masks)
            daf = jnp.where(tril4, da_s, 0.0).astype(bf16)
            dab = jnp.where(tril4, 0.0, da_s).astype(bf16)
            dqf = _unstack(_dot(daf, t["kn_b"]) + _dg(do_b, s_b, NT), masks)
            dqn = _unstack(_dot(dab, t["kp_b"]), masks)
            dkn = _dg(daf, t["qf_s"], TN)
            dkp = _dg(dab, t["qn_s"], TN)
            dq_ref[pl.ds(r0, CH), :] = (dqf * t["eb"] + dqn * t["enb"]) * (DK ** -0.5)
            dk_ref[pl.ds(r0, CH), :] = dkn * t["enb"] + dkp * t["eb"] + dkd * t["ed"]
            dkd_kd = dkd * t["kd"]
            dbc = dqf * t["qf"] - dqn * t["qn"] - dkn * t["kn"] + dkp * t["kp"] - dkd_kd
            dbc = dbc + jnp.where(last_row, _rowsum(dkd_kd) + dbend, 0.0)
            dla = jnp.dot(umat, dbc, preferred_element_type=f32, precision=HIGHEST)
            dpre = dla * (1.0 / TAU) * (1.0 - jnp.exp(TAU * lac))
            dpre_ref[pl.ds(r0, CH), :] = dpre
            redb_ref[...] += dpre
            return carry

        lax.fori_loop(0, nc, chunk, 0)

    rev = lambda col: (lambda i: (nt - 1 - i, col))
    return pl.pallas_call(
        body, name=name,
        grid=(nt,),
        in_specs=[pl.BlockSpec((tg, DQK), rev(4)), pl.BlockSpec((tg, DQK), rev(5)),
                  pl.BlockSpec((tg, DG), rev(3)), pl.BlockSpec((tg, DG), rev(4)),
                  pl.BlockSpec((tg, DQK), rev(0)), pl.BlockSpec((nc * DQK, DV), rev(0)),
                  pl.BlockSpec((tg, DG), rev(0)), _full(gn_s.shape)],
        out_specs=[pl.BlockSpec((tg, DQK), rev(0)), pl.BlockSpec((tg, DQK), rev(0)),
                   pl.BlockSpec((tg, DG), rev(0)), pl.BlockSpec((tg, DG), rev(0)), pl.BlockSpec((tg, DQK), rev(0)),
                   _full((DQK, DV)), _full((CH, DQK))],
        out_shape=[jax.ShapeDtypeStruct((S, DQK), f32), jax.ShapeDtypeStruct((S, DQK), f32),
                   jax.ShapeDtypeStruct((S, DG), f32), jax.ShapeDtypeStruct((S, DG), f32), jax.ShapeDtypeStruct((S, DQK), f32),
                   jax.ShapeDtypeStruct((DQK, DV), f32), jax.ShapeDtypeStruct((CH, DQK), f32)],
        scratch_shapes=[pltpu.VMEM((DQK, DV), f32)],
        compiler_params=_cp(1),
    )(z, z, z, z, la, sprev, dyg, gn_s)


def mixout_fwd(x1, yc, yg, mod, wout, tm, name):
    S = x1.shape[0]

    def body(x_ref, yc_ref, yg_ref, mod_ref, w_ref, xo_ref):
        mixo = _dot(yc_ref[...], w_ref[0:DC, :]) + _dot(yg_ref[...], w_ref[DC:DC + DG, :])
        xo_ref[...] = x_ref[...] + mod_ref[5:6, :] * mixo

    row = lambda i: (i, 0)
    return pl.pallas_call(
        body, name=name,
        grid=(S // tm,),
        in_specs=[pl.BlockSpec((tm, D), row), pl.BlockSpec((tm, DC), row), pl.BlockSpec((tm, DG), row),
                  _full(mod.shape), _full(wout.shape)],
        out_specs=pl.BlockSpec((tm, D), row),
        out_shape=jax.ShapeDtypeStruct((S, D), f32),
        compiler_params=_cp(1),
    )(x1, yc, yg, mod, wout)


def mixout_bwd(dx2, yc, yg, mod, wout, tm, name):
    S = dx2.shape[0]

    def body(dx_ref, yc_ref, yg_ref, mod_ref, w_ref, dm_ref, dyc_ref, dyg_ref, red_ref):
        @pl.when(pl.program_id(0) == 0)
        def _():
            red_ref[...] = jnp.zeros_like(red_ref)

        dxv = dx_ref[...]
        mixo = _dot(yc_ref[...], w_ref[0:DC, :]) + _dot(yg_ref[...], w_ref[DC:DC + DG, :])
        red_ref[0:1, :] += _rowsum(dxv * mixo)
        dm = (mod_ref[5:6, :] * dxv).astype(bf16)
        dm_ref[...] = dm
        dycat = _dg(dm, w_ref[...], NT)
        dyc_ref[...] = dycat[:, :DC]
        dyg_ref[...] = dycat[:, DC:]

    row = lambda i: (i, 0)
    return pl.pallas_call(
        body, name=name,
        grid=(S // tm,),
        in_specs=[pl.BlockSpec((tm, D), row), pl.BlockSpec((tm, DC), row), pl.BlockSpec((tm, DG), row),
                  _full(mod.shape), _full(wout.shape)],
        out_specs=[pl.BlockSpec((tm, D), row), pl.BlockSpec((tm, DC), row), pl.BlockSpec((tm, DG), row), _full((8, D))],
        out_shape=[jax.ShapeDtypeStruct((S, D), bf16), jax.ShapeDtypeStruct((S, DC), f32),
                   jax.ShapeDtypeStruct((S, DG), f32), jax.ShapeDtypeStruct((8, D), f32)],
        compiler_params=_cp(1),
    )(dx2, yc, yg, mod, wout)


def final_fwd_bwd(x, tgt, fmod, g, tm, name):
    S = x.shape[0]

    def body(x_ref, t_ref, fm_ref, g_ref, dx_ref, red_ref):
        @pl.when(pl.program_id(0) == 0)
        def _():
            red_ref[...] = jnp.zeros_like(red_ref)

        xh, rstd = _rms_parts(x_ref[...])
        gv = g_ref[...]
        n = xh * gv
        sc = 1.0 + fm_ref[1:2, :]
        e = n * sc + fm_ref[0:1, :] - t_ref[...]
        red_ref[0:1, :] += _rowsum(e * e) * (0.5 / D)
        dy = e * (1.0 / D)
        dn = dy * sc
        red_ref[1:2, :] += _rowsum(dy)
        red_ref[2:3, :] += _rowsum(dy * n)
        red_ref[3:4, :] += _rowsum(dn * xh)
        dx_ref[...] = _rms_bwd(dn * gv, xh, rstd)

    row = lambda i: (i, 0)
    return pl.pallas_call(
        body, name=name,
        grid=(S // tm,),
        in_specs=[pl.BlockSpec((tm, D), row), pl.BlockSpec((tm, D), row), _full(fmod.shape), _full(g.shape)],
        out_specs=[pl.BlockSpec((tm, D), row), _full((8, D))],
        out_shape=[jax.ShapeDtypeStruct((S, D), f32), jax.ShapeDtypeStruct((8, D), f32)],
        compiler_params=_cp(1),
    )(x, tgt, fmod, g)


def ada_fwd(c_all, w, b, name):
    n = w.shape[1]

    def body(c_ref, w_ref, b_ref, o_ref):
        cv = c_ref[...]
        o_ref[...] = jnp.dot(cv * _sigmoid(cv), w_ref[...], preferred_element_type=f32, precision=HIGHEST) + b_ref[...]

    return pl.pallas_call(
        body, name=name,
        in_specs=[_full(c_all.shape), _full(w.shape), _full(b.shape)],
        out_specs=_full((N_DEV, n)),
        out_shape=jax.ShapeDtypeStruct((N_DEV, n), f32),
        grid=(1,),
        compiler_params=_cp(1),
    )(c_all, w, b)


def ada_wgrad(c_all_t, dm, name):
    n = dm.shape[1]

    def body(c_ref, d_ref, o_ref):
        cv = c_ref[...]
        o_ref[...] = jnp.dot(cv * _sigmoid(cv), d_ref[...], preferred_element_type=f32, precision=HIGHEST)

    return pl.pallas_call(
        body, name=name,
        in_specs=[_full(c_all_t.shape), _full(dm.shape)],
        out_specs=_full((D, n)),
        out_shape=jax.ShapeDtypeStruct((D, n), f32),
        grid=(1,),
        compiler_params=_cp(1),
    )(c_all_t, dm)


def _adam_math(gv, wv, mv, vv):
    m = ADAM_B1 * mv + (1.0 - ADAM_B1) * gv
    v = ADAM_B2 * vv + (1.0 - ADAM_B2) * (gv * gv)
    m_hat = m / (1.0 - ADAM_B1 ** ADAM_STEP)
    v_hat = v / (1.0 - ADAM_B2 ** ADAM_STEP)
    delta = -ADAM_LR * (m_hat / (jnp.sqrt(v_hat) + ADAM_EPS) + ADAM_WD * wv)
    return delta, m, v


def adam_parts(parts, w, m, v, tr, name):
    L, R, C = w.shape
    nt = R // tr

    def body(*refs):
        p_refs = refs[:L]
        w_ref, m_ref, v_ref, g_ref, d_ref, mo_ref, vo_ref = refs[L:]
        lyr = pl.program_id(0)
        for l in range(L):
            @pl.when(lyr == l)
            def _(p_ref=p_refs[l]):
                gv = p_ref[0].astype(f32)
                for k in range(1, N_DEV):
                    gv = gv + p_ref[k].astype(f32)
                g_ref[...] = gv
                d_ref[...], mo_ref[...], vo_ref[...] = _adam_math(gv, w_ref[...], m_ref[...], v_ref[...])

    def part_spec(l):
        return pl.BlockSpec((N_DEV, tr, C), lambda lyr, i: (0, jnp.where(lyr == l, i, jnp.where(lyr < l, 0, nt - 1)), 0))

    spec = pl.BlockSpec((None, tr, C), lambda lyr, i: (lyr, i, 0))
    shp = jax.ShapeDtypeStruct((L, R, C), f32)
    return pl.pallas_call(
        body, name=name,
        grid=(L, nt),
        in_specs=[part_spec(l) for l in range(L)] + [spec, spec, spec],
        out_specs=[spec, spec, spec, spec],
        out_shape=[shp, shp, shp, shp],
        compiler_params=_cp(2),
    )(*parts, w, m, v)


def adam_plain(gr, w, m, v, tr, name):
    R, C = w.shape

    def body(g_ref, w_ref, m_ref, v_ref, d_ref, mo_ref, vo_ref):
        d_ref[...], mo_ref[...], vo_ref[...] = _adam_math(g_ref[...], w_ref[...], m_ref[...], v_ref[...])

    spec = pl.BlockSpec((tr, C), lambda i: (i, 0))
    shp = jax.ShapeDtypeStruct((R, C), f32)
    return pl.pallas_call(
        body, name=name,
        grid=(R // tr,),
        in_specs=[spec, spec, spec, spec],
        out_specs=[spec, spec, spec],
        out_shape=[shp, shp, shp],
        compiler_params=_cp(1),
    )(gr, w, m, v)


def sum8(parts, name):
    _, R, C = parts.shape

    def body(p_ref, o_ref):
        acc = p_ref[0]
        for k in range(1, N_DEV):
            acc = acc + p_ref[k]
        o_ref[...] = acc

    return pl.pallas_call(
        body, name=name,
        grid=(1,),
        in_specs=[_full(parts.shape)],
        out_specs=_full((R, C)),
        out_shape=jax.ShapeDtypeStruct((R, C), f32),
        compiler_params=_cp(1),
    )(parts)


def _place():
    return lax.axis_index("x"), lax.axis_index("y"), lax.axis_index("c")


def all_gather(arrs, name):
    n = len(arrs)

    def body(*refs):
        ins, outs = refs[:n], refs[n:2 * n]
        send_sems, recv_sems, local_sems = refs[2 * n:]
        x, y, c = _place()
        me, sibling = (x, y, c), (x, y, 1 - c)
        chips = [(1 - x, y), (x, 1 - y), (1 - x, 1 - y)]

        def slot(a, p):
            return outs[a].at[4 * p[0] + 2 * p[1] + p[2]]

        def copy(a, k, block, to, src=None):
            return pltpu.make_async_remote_copy(
                src_ref=slot(a, block) if src is None else src, dst_ref=slot(a, block),
                send_sem=send_sems.at[a * 7 + k], recv_sem=recv_sems.at[a * 7 + k],
                device_id=to, device_id_type=MESH)

        mine = [pltpu.make_async_copy(ins[a], slot(a, me), local_sems.at[a]) for a in range(n)]
        for cp in mine:
            cp.start()
        first = []
        for a in range(n):
            first.append(copy(a, 0, me, sibling, src=ins[a]))
            first += [copy(a, 1 + j, me, (*chip, c), src=ins[a]) for j, chip in enumerate(chips)]
        for cp in first:
            cp.start()
        passed = []
        for j, chip in enumerate(chips):
            for a in range(n):
                copy(a, 1 + j, (*chip, c), me).wait_recv()
                fw = copy(a, 4 + j, (*chip, c), sibling)
                fw.start()
                passed.append(fw)
        for a in range(n):
            copy(a, 0, sibling, me).wait_recv()
            for j, chip in enumerate(chips):
                copy(a, 4 + j, (*chip, 1 - c), me).wait_recv()
        for cp in first + passed:
            cp.wait_send()
        for cp in mine:
            cp.wait()

    any_spec = pl.BlockSpec(memory_space=pl.ANY)
    return pl.pallas_call(
        body, name=name,
        in_specs=[any_spec] * n,
        out_specs=[any_spec] * n,
        out_shape=[jax.ShapeDtypeStruct((N_DEV,) + a.shape, a.dtype) for a in arrs],
        scratch_shapes=[pltpu.SemaphoreType.DMA((7 * n,)), pltpu.SemaphoreType.DMA((7 * n,)),
                        pltpu.SemaphoreType.DMA((n,))],
    )(*arrs)


def all_to_all(arrs, name):
    n = len(arrs)

    def body(*refs):
        ins, outs = refs[:n], refs[n:2 * n]
        send_sems, recv_sems, local_sems = refs[2 * n:]
        x, y, c = _place()
        me_i = 4 * x + 2 * y + c
        mine = [pltpu.make_async_copy(ins[a].at[me_i], outs[a].at[me_i], local_sems.at[a]) for a in range(n)]
        for cp in mine:
            cp.start()
        sends, recvs = [], []
        for k in range(1, N_DEV):
            px = 1 - x if (k >> 2) & 1 else x
            py = 1 - y if (k >> 1) & 1 else y
            pc = 1 - c if k & 1 else c
            p_i = 4 * px + 2 * py + pc
            for a in range(n):
                sem = a * 7 + k - 1
                sends.append(pltpu.make_async_remote_copy(
                    src_ref=ins[a].at[p_i], dst_ref=outs[a].at[me_i],
                    send_sem=send_sems.at[sem], recv_sem=recv_sems.at[sem],
                    device_id=(px, py, pc), device_id_type=MESH))
                recvs.append(pltpu.make_async_remote_copy(
                    src_ref=ins[a].at[p_i], dst_ref=outs[a].at[p_i],
                    send_sem=send_sems.at[sem], recv_sem=recv_sems.at[sem],
                    device_id=(px, py, pc), device_id_type=MESH))
        for cp in sends:
            cp.start()
        for cp in recvs:
            cp.wait_recv()
        for cp in sends:
            cp.wait_send()
        for cp in mine:
            cp.wait()

    any_spec = pl.BlockSpec(memory_space=pl.ANY)
    return pl.pallas_call(
        body, name=name,
        in_specs=[any_spec] * n,
        out_specs=[any_spec] * n,
        out_shape=[jax.ShapeDtypeStruct(a.shape, a.dtype) for a in arrs],
        scratch_shapes=[pltpu.SemaphoreType.DMA((7 * n,)), pltpu.SemaphoreType.DMA((7 * n,)),
                        pltpu.SemaphoreType.DMA((n,))],
    )(*arrs)


def _tiles(S):
    t = min(512, S)
    return dict(ffn=t, row=t, conv=t, gla=t, bk=min(1024, S))


def local_step(x, tgt, mods, fmod, wts):
    S = x.shape[0]
    T = _tiles(S)
    bk = T["bk"]
    saved = []
    xc = x
    for l in range(DEPTH):
        w = wts[f"L{l}"]
        x0 = xc
        x1, zg1, zu1, f1 = ffn_fwd(x0, mods[l], w["g1"], wts["wi1"], w["wo1"], l, (0, 1, 2), T["ffn"], f"ffn1_fwd_{l}")
        z, la = mixin_fwd(x1, mods[l], w["g2"], w["win"], w["wgu"], w["bgate"], T["row"], f"mixin_fwd_{l}")
        y, yc = conv_fwd(z, w["wdw"], w["cpar"], T["conv"], f"conv_fwd_{l}")
        yg, sprev = gla_fwd(z, la, w["gn_s"], T["gla"], f"gla_fwd_{l}")
        x2 = mixout_fwd(x1, yc, yg, mods[l], w["wout"], T["row"], f"mixout_fwd_{l}")
        x3, zg2, zu2, f2 = ffn_fwd(x2, mods[l], w["g3"], wts["wi2"], w["wo2"], l, (6, 7, 8), T["ffn"], f"ffn2_fwd_{l}")
        saved.append(dict(x0=x0, x1=x1, x2=x2, zg1=zg1, zu1=zu1, f1=f1, zg2=zg2, zu2=zu2, f2=f2,
                          z=z, la=la, y=y, yc=yc, yg=yg, sprev=sprev))
        xc = x3

    dx, redf = final_fwd_bwd(xc, tgt, fmod, wts["gf"], T["row"], "loss_head")
    loss_lanes = redf[0]
    dfmod = redf[1:3]
    grads = {"gf": redf[3]}
    dmods = [None] * DEPTH

    for l in reversed(range(DEPTH)):
        w, sv = wts[f"L{l}"], saved[l]
        g = {}
        dx2, h, df, a, dzg, dzu, red3 = ffn_bwd(sv["x2"], dx, sv["zg2"], sv["zu2"], sv["f2"], mods[l], w["g3"],
                                                wts["wi2"], w["wo2"], l, (6, 7, 8), T["ffn"], f"ffn2_bwd_{l}")
        g["wi2"] = dwi_pieces(h, dzg, dzu, bk, f"dwi2_{l}")
        g["wo2"] = dwo_pieces(a, df, bk, f"dwo2_{l}").reshape(N_DEV, F // N_DEV, D)
        dmix, dyc, dyg, red_o = mixout_bwd(dx2, sv["yc"], sv["yg"], mods[l], w["wout"], T["row"], f"mixout_bwd_{l}")
        g["wout"] = jnp.concatenate([matmul_tn(sv["yc"], dmix, DC, D, DC, D, bk, f"dwout_c_{l}", out_dtype=bf16),
                                     matmul_tn(sv["yg"], dmix, DG, D, DG, D, bk, f"dwout_g_{l}", out_dtype=bf16)],
                                    axis=0).reshape(N_DEV, D // N_DEV, D)
        dq, dk, dv, dr, dpre, redg, redb = gla_bwd(sv["z"], sv["la"], sv["sprev"], dyg, w["gn_s"], T["gla"], f"gla_bwd_{l}")
        dzab, redc = conv_bwd(sv["z"], sv["y"], dyc, w["wdw"], w["cpar"], T["conv"], f"conv_bwd_{l}")
        dx1, h2, dz, red2 = mixin_bwd(sv["x1"], dx2, dzab, dq, dk, dv, dr, dpre, mods[l], w["g2"], w["win"], w["wgu"],
                                      T["row"], f"mixin_bwd_{l}")
        dwin = matmul_tn(h2, dz, D, DINP, D, DINP, bk, f"dwin_{l}", out_dtype=bf16)[:, :DIN]
        g["win"] = jnp.transpose(dwin.reshape(D, N_DEV, DIN // N_DEV), (1, 0, 2))
        g["wgu"] = matmul_tn(sv["z"], dpre, 128, DQK, 128, DQK, bk, f"dwgu_{l}", a_col_block=(DINP - 128) // 128)[:GR]
        g["bgate"] = jnp.sum(redb, axis=0)
        g["gn"] = jnp.sum(redg.reshape(NH, CH, DV), axis=1)
        g["wdw"] = redc[:CW]
        g["bdw"], g["gln"], g["bln"] = redc[32], redc[33], redc[34]
        dx0, h, df, a, dzg, dzu, red1 = ffn_bwd(sv["x0"], dx1, sv["zg1"], sv["zu1"], sv["f1"], mods[l], w["g1"],
                                                wts["wi1"], w["wo1"], l, (0, 1, 2), T["ffn"], f"ffn1_bwd_{l}")
        g["wi1"] = dwi_pieces(h, dzg, dzu, bk, f"dwi1_{l}")
        g["wo1"] = dwo_pieces(a, df, bk, f"dwo1_{l}").reshape(N_DEV, F // N_DEV, D)
        g["g1"], g["g2"], g["g3"] = red1[3], red2[2], red3[3]
        dmods[l] = jnp.stack([red1[0], red1[1], red1[2], red2[0], red2[1], red_o[0], red3[0], red3[1], red3[2]], axis=0)
        grads[f"L{l}"] = g
        dx = dx0
    return loss_lanes, dx, grads, dmods, dfmod


def _col_shards_to_full(gathered):
    n, r, c = gathered.shape
    return jnp.transpose(gathered, (1, 0, 2)).reshape(r, n * c)


def _pad_rows(a, rows):
    return jnp.pad(a, ((0, rows - a.shape[0]), (0, 0)))


def kernel(x, c, w_ada, b_ada, g_norm_ffn1, w_ffn1_in, w_ffn1_out, g_norm_mix, w_in, w_dw, b_dw, g_conv_ln, b_conv_ln, w_gate_up, b_gate, g_gla_norm, w_out, g_norm_ffn2, w_ffn2_in, w_ffn2_out, g_norm_final, w_ada_final, b_ada_final, loss_target, m_w_ada, m_b_ada, m_g_norm_ffn1, m_w_ffn1_in, m_w_ffn1_out, m_g_norm_mix, m_w_in, m_w_dw, m_b_dw, m_g_conv_ln, m_b_conv_ln, m_w_gate_up, m_b_gate, m_g_gla_norm, m_w_out, m_g_norm_ffn2, m_w_ffn2_in, m_w_ffn2_out, m_g_norm_final, m_w_ada_final, m_b_ada_final, v_w_ada, v_b_ada, v_g_norm_ffn1, v_w_ffn1_in, v_w_ffn1_out, v_g_norm_mix, v_w_in, v_w_dw, v_b_dw, v_g_conv_ln, v_b_conv_ln, v_w_gate_up, v_b_gate, v_g_gla_norm, v_w_out, v_g_norm_ffn2, v_w_ffn2_in, v_w_ffn2_out, v_g_norm_final, v_w_ada_final, v_b_ada_final):
    me = 4 * lax.axis_index("x") + 2 * lax.axis_index("y") + lax.axis_index("c")
    L = DEPTH
    n_ada = N_MOD * D // N_DEV
    n_fin = 2 * D // N_DEV

    small = jnp.concatenate([c.reshape(-1), w_dw.reshape(-1), w_gate_up.reshape(-1)])
    n_small = small.shape[0]
    small = jnp.pad(small, (0, 8 * D - n_small)).reshape(8, D)
    big = [w_ffn1_in, w_ffn1_out, w_in, w_out, w_ffn2_in, w_ffn2_out]
    gathered = all_gather([a.astype(bf16) for a in big] + [small], "gather_weights")
    wi1_a, wo1_a, win_a, wout_a, wi2_a, wo2_a, small_a = gathered
    small_a = small_a.reshape(N_DEV, 8 * D)
    c_all = small_a[:, :D]
    o1 = D + L * CW * (DC // N_DEV)
    wdw_full = _col_shards_to_full(small_a[:, D:o1].reshape(N_DEV, L * CW, DC // N_DEV)).reshape(L, CW, DC)
    wgu_full = _col_shards_to_full(small_a[:, o1:o1 + L * GR * (DQK // N_DEV)].reshape(N_DEV, L * GR, DQK // N_DEV)).reshape(L, GR, DQK)

    b_ada_mine = lax.dynamic_slice(b_ada, (0, me * n_ada), (L, n_ada))
    b_fin_mine = lax.dynamic_slice(b_ada_final, (me * n_fin,), (n_fin,))
    parts = [ada_fwd(c_all, w_ada[l], b_ada_mine[l:l + 1], f"ada_fwd_{l}") for l in range(L)]
    parts.append(ada_fwd(c_all, w_ada_final, b_fin_mine.reshape(1, n_fin), "ada_fwd_final"))
    modsrc = jnp.concatenate(parts, axis=1)
    n_row = modsrc.shape[1]
    modsrc = jnp.pad(modsrc, ((0, 0), (0, 24 * 128 - n_row))).reshape(N_DEV, 24, 128)
    (modrecv,) = all_to_all([modsrc], "exchange_mod")
    modrecv = modrecv.reshape(N_DEV, 24 * 128)
    mods = []
    for l in range(L):
        mvec = modrecv[:, l * n_ada:(l + 1) * n_ada].reshape(N_MOD, D)
        mods.append(_pad_rows(mvec, 16))
    fmod = _pad_rows(modrecv[:, L * n_ada:L * n_ada + n_fin].reshape(2, D), 8)

    wts = {"gf": g_norm_final.reshape(1, D), "wi1": wi1_a, "wi2": wi2_a}
    for l in range(L):
        win_full = _col_shards_to_full(win_a[:, l])
        wts[f"L{l}"] = dict(
            g1=g_norm_ffn1[l].reshape(1, D), g2=g_norm_mix[l].reshape(1, D), g3=g_norm_ffn2[l].reshape(1, D),
            wo1=wo1_a[:, l].reshape(F, D), wo2=wo2_a[:, l].reshape(F, D),
            win=jnp.pad(win_full, ((0, 0), (0, DINP - DIN))),
            wout=wout_a[:, l].reshape(D, D),
            wgu=_pad_rows(wgu_full[l], 128).astype(bf16),
            bgate=b_gate[l].reshape(1, DQK),
            wdw=_pad_rows(wdw_full[l], 32),
            cpar=_pad_rows(jnp.stack([b_dw[l], g_conv_ln[l], b_conv_ln[l]]), 8),
            gn_s=jnp.repeat(g_gla_norm[l], CH, axis=0),
        )

    loss_lanes, grad_x, gr, dmods, dfmod = local_step(x[0], loss_target[0], mods, fmod, wts)

    big_names = ["wi1", "wo1", "win", "wout", "wi2", "wo2"]
    pieces = [gr[f"L{l}"][n] for n in big_names for l in range(L)]
    recv = all_to_all(pieces, "exchange_grads")
    recv = {n: recv[L * k:L * k + L] for k, n in enumerate(big_names)}

    def adam_big(rv, w, m, v, name):
        R = w.shape[1]
        tr = 256 if R % 256 == 0 else R // 2
        return adam_parts(rv, w, m, v, tr, name)

    res = {}
    res["w_ffn1_in"] = adam_big(recv["wi1"], w_ffn1_in, m_w_ffn1_in, v_w_ffn1_in, "adam_ffn1_in")
    res["w_ffn1_out"] = adam_big(recv["wo1"], w_ffn1_out, m_w_ffn1_out, v_w_ffn1_out, "adam_ffn1_out")
    res["w_in"] = adam_big(recv["win"], w_in, m_w_in, v_w_in, "adam_w_in")
    res["w_out"] = adam_big(recv["wout"], w_out, m_w_out, v_w_out, "adam_w_out")
    res["w_ffn2_in"] = adam_big(recv["wi2"], w_ffn2_in, m_w_ffn2_in, v_w_ffn2_in, "adam_ffn2_in")
    res["w_ffn2_out"] = adam_big(recv["wo2"], w_ffn2_out, m_w_ffn2_out, v_w_ffn2_out, "adam_ffn2_out")

    flat = lambda name: jnp.stack([gr[f"L{l}"][name] for l in range(L)]).reshape(-1)
    sections = [
        ("b_ada", jnp.stack(dmods).reshape(-1)), ("b_ada_final", dfmod.reshape(-1)),
        ("g_norm_ffn1", flat("g1")), ("g_norm_mix", flat("g2")), ("g_norm_ffn2", flat("g3")), ("g_norm_final", gr["gf"]),
        ("b_dw", flat("bdw")), ("g_conv_ln", flat("gln")), ("b_conv_ln", flat("bln")), ("b_gate", flat("bgate")),
        ("g_gla_norm", flat("gn")),
    ]
    n_rep = sum(s[1].shape[0] for s in sections)
    rep_rows = -(-n_rep // D)
    extra = [("loss", loss_lanes), ("w_dw", flat("wdw")), ("w_gate_up", flat("wgu"))]
    pack = jnp.concatenate([s[1] for s in sections] + [jnp.zeros((rep_rows * D - n_rep,), f32)] + [s[1] for s in extra])
    n_pack = pack.shape[0]
    pack_rows = -(-n_pack // (8 * D)) * 8
    pack = jnp.pad(pack, (0, pack_rows * D - n_pack)).reshape(pack_rows, D)
    (pack_all,) = all_gather([pack], "gather_small_grads")
    tot = sum8(pack_all, "sum_small_grads")
    tot_flat = tot.reshape(-1)
    loss = jnp.sum(tot_flat[rep_rows * D:rep_rows * D + D])
    o_dw = rep_rows * D + D
    g_wdw_full = tot_flat[o_dw:o_dw + L * CW * DC].reshape(L, CW, DC)
    o_gu = o_dw + L * CW * DC
    g_wgu_full = tot_flat[o_gu:o_gu + L * GR * DQK].reshape(L, GR, DQK)

    small_params = dict(b_ada=(b_ada, m_b_ada, v_b_ada), b_ada_final=(b_ada_final, m_b_ada_final, v_b_ada_final),
                        g_norm_ffn1=(g_norm_ffn1, m_g_norm_ffn1, v_g_norm_ffn1), g_norm_mix=(g_norm_mix, m_g_norm_mix, v_g_norm_mix),
                        g_norm_ffn2=(g_norm_ffn2, m_g_norm_ffn2, v_g_norm_ffn2), g_norm_final=(g_norm_final, m_g_norm_final, v_g_norm_final),
                        b_dw=(b_dw, m_b_dw, v_b_dw), g_conv_ln=(g_conv_ln, m_g_conv_ln, v_g_conv_ln),
                        b_conv_ln=(b_conv_ln, m_b_conv_ln, v_b_conv_ln), b_gate=(b_gate, m_b_gate, v_b_gate),
                        g_gla_norm=(g_gla_norm, m_g_gla_norm, v_g_gla_norm))

    def rep_pack(idx):
        p = jnp.concatenate([small_params[s[0]][idx].reshape(-1) for s in sections])
        return jnp.pad(p, (0, rep_rows * D - n_rep)).reshape(rep_rows, D)

    g_rep = tot[:rep_rows]
    d_rep, m_rep, v_rep = adam_plain(g_rep, rep_pack(0), rep_pack(1), rep_pack(2), rep_rows, "adam_small")
    off = 0
    for sname, sval in sections:
        shp = small_params[sname][0].shape
        nel = sval.shape[0]
        res[sname] = [a.reshape(-1)[off:off + nel].reshape(shp) for a in (g_rep, d_rep, m_rep, v_rep)]
        off += nel

    def adam_cols(g_full, w, m, v, name):
        shp = w.shape
        g_mine = lax.dynamic_slice(g_full, (0, 0, me * shp[2]), shp)
        R, C = shp[0] * shp[1], shp[2]
        outs = adam_plain(g_mine.reshape(R, C), w.reshape(R, C), m.reshape(R, C), v.reshape(R, C), R, name)
        return [g_mine] + [o.reshape(shp) for o in outs]

    res["w_dw"] = adam_cols(g_wdw_full, w_dw, m_w_dw, v_w_dw, "adam_w_dw")
    res["w_gate_up"] = adam_cols(g_wgu_full, w_gate_up, m_w_gate_up, v_w_gate_up, "adam_w_gate_up")

    c_all_t = c_all.T
    dmod_all = pack_all.reshape(N_DEV, -1)[:, :L * N_MOD * D].reshape(N_DEV, L, N_MOD * D)
    dfm_all = pack_all.reshape(N_DEV, -1)[:, L * N_MOD * D:L * N_MOD * D + 2 * D]
    dm_mine = lax.dynamic_slice(dmod_all, (0, 0, me * n_ada), (N_DEV, L, n_ada))
    dfm_mine = lax.dynamic_slice(dfm_all, (0, me * n_fin), (N_DEV, n_fin))
    g_w_ada = jnp.stack([ada_wgrad(c_all_t, dm_mine[:, l], f"ada_wgrad_{l}") for l in range(L)])
    g_w_fin = ada_wgrad(c_all_t, dfm_mine, "ada_wgrad_final")
    outs = adam_plain(g_w_ada.reshape(L * D, n_ada), w_ada.reshape(L * D, n_ada), m_w_ada.reshape(L * D, n_ada),
                      v_w_ada.reshape(L * D, n_ada), 256, "adam_w_ada")
    res["w_ada"] = [g_w_ada] + [o.reshape(w_ada.shape) for o in outs]
    res["w_ada_final"] = [g_w_fin] + list(adam_plain(g_w_fin, w_ada_final, m_w_ada_final, v_w_ada_final, 256, "adam_w_ada_final"))

    order = ["w_ada", "b_ada", "g_norm_ffn1", "w_ffn1_in", "w_ffn1_out", "g_norm_mix", "w_in", "w_dw", "b_dw", "g_conv_ln",
             "b_conv_ln", "w_gate_up", "b_gate", "g_gla_norm", "w_out", "g_norm_ffn2", "w_ffn2_in", "w_ffn2_out",
             "g_norm_final", "w_ada_final", "b_ada_final"]
    out = [loss, grad_x[None]]
    for k in range(4):
        out += [res[name][k] for name in order]
    return tuple(out)
```

```python
import functools

import jax
import jax.numpy as jnp
from jax import lax
from jax.experimental import pallas as pl
from jax.experimental.pallas import tpu as pltpu

f32 = jnp.float32
bf16 = jnp.bfloat16

N_DEV = 8
DEPTH = 2
D = 1024
F = 2816
DC = 512
NH = 4
DK = 64
DV = 128
DQK = NH * DK
DG = NH * DV
CH = 64
CW = 31
GR = 16
TAU = 16.0
N_MOD = 9
DIN = 2 * DC + 2 * DQK + 2 * DG + GR
DINP = 2688
EPS = 1e-6
HALO = 32
SUBLANES = 8
NFS = 4
FS = F // NFS

ADAM_LR = 0.001
ADAM_B1 = 0.9
ADAM_B2 = 0.999
ADAM_EPS = 1e-08
ADAM_WD = 0.01
ADAM_STEP = 10

V7X_VMEM_LIMIT = 56 * 1024 * 1024
MESH = pl.DeviceIdType.MESH
HIGHEST = lax.Precision.HIGHEST

NT = (((1,), (1,)), ((), ()))
TN = (((0,), (0,)), ((), ()))


def _cp(n_axes):
    return pltpu.CompilerParams(dimension_semantics=("arbitrary",) * n_axes, vmem_limit_bytes=V7X_VMEM_LIMIT)


def _full(shape):
    nd = len(shape)
    return pl.BlockSpec(shape, lambda *_: (0,) * nd)


def _dot(a, b):
    return jnp.dot(a, b, preferred_element_type=f32)


def _dg(a, b, dims):
    return lax.dot_general(a, b, dims, preferred_element_type=f32)


def _sigmoid(x):
    return jax.nn.sigmoid(x)


def _rowsum(x):
    return jnp.sum(x, axis=0, keepdims=True)


def _rms_parts(xv):
    rstd = lax.rsqrt(jnp.mean(xv * xv, axis=-1, keepdims=True) + EPS)
    return xv * rstd, rstd


def _rms_bwd(dxh, xh, rstd):
    return rstd * (dxh - xh * jnp.mean(dxh * xh, axis=-1, keepdims=True))


def ffn_fwd(x, mod, g, wi, wo, rows, tm, name, comm=None):
    S = x.shape[0]
    nj = NFS
    tf = FS
    r_shift, r_scale, r_gate = rows

    def body(x_ref, mod_ref, g_ref, wg_ref, wu_ref, wo_ref, xo_ref, zg_ref, zu_ref, f_ref, h_s, acc_s):
        j = pl.program_id(1)

        @pl.when(j == 0)
        def _():
            xh, _ = _rms_parts(x_ref[...])
            hv = xh * g_ref[...] * (1.0 + mod_ref[r_scale:r_scale + 1, :]) + mod_ref[r_shift:r_shift + 1, :]
            h_s[...] = hv.astype(bf16)
            acc_s[...] = jnp.zeros_like(acc_s)

        h = h_s[...]
        zg = _dot(h, wg_ref[...])
        zu = _dot(h, wu_ref[...])
        zg_ref[...] = zg.astype(bf16)
        zu_ref[...] = zu.astype(bf16)
        a = zg * _sigmoid(zg) * zu
        acc_s[...] += _dot(a.astype(bf16), wo_ref[...])

        @pl.when(j == nj - 1)
        def _():
            fv = acc_s[...]
            f_ref[...] = fv.astype(bf16)
            xo_ref[...] = x_ref[...] + 0.5 * mod_ref[r_gate:r_gate + 1, :] * fv

    return _pcall(
        body, (x, mod, g, wi, wi, wo), name=name, comm=comm,
        grid=(S // tm, nj),
        in_specs=[
            pl.BlockSpec((tm, D), lambda i, j: (i, 0)),
            _full(mod.shape), _full(g.shape),
            pl.BlockSpec((None, D, tf), lambda i, j: (j, 0, 0)),
            pl.BlockSpec((None, D, tf), lambda i, j: (j + NFS, 0, 0)),
            pl.BlockSpec((tf, D), lambda i, j: (j, 0)),
        ],
        out_specs=[
            pl.BlockSpec((tm, D), lambda i, j: (i, 0)),
            pl.BlockSpec((None, tm, tf), lambda i, j: (j, i, 0)),
            pl.BlockSpec((None, tm, tf), lambda i, j: (j, i, 0)),
            pl.BlockSpec((tm, D), lambda i, j: (i, 0)),
        ],
        out_shape=[
            jax.ShapeDtypeStruct((S, D), f32),
            jax.ShapeDtypeStruct((NFS, S, tf), bf16),
            jax.ShapeDtypeStruct((NFS, S, tf), bf16),
            jax.ShapeDtypeStruct((S, D), bf16),
        ],
        scratch_shapes=[pltpu.VMEM((tm, D), bf16), pltpu.VMEM((tm, D), f32)],
    )


def ffn_bwd(x, dy, zg, zu, fo, mod, g, wi, wo, rows, tm, name, comm=None):
    S = x.shape[0]
    nj = NFS
    tf = FS
    r_shift, r_scale, r_gate = rows

    def body(x_ref, dy_ref, zg_ref, zu_ref, f_ref, mod_ref, g_ref, wg_ref, wu_ref, wo_ref,
             dx_ref, h_ref, df_ref, a_ref, dzg_ref, dzu_ref, red_ref, acc_s, df_s):
        i = pl.program_id(0)
        j = pl.program_id(1)

        @pl.when((i == 0) & (j == 0))
        def _():
            red_ref[...] = jnp.zeros_like(red_ref)

        @pl.when(j == 0)
        def _():
            xh, _ = _rms_parts(x_ref[...])
            hv = xh * g_ref[...] * (1.0 + mod_ref[r_scale:r_scale + 1, :]) + mod_ref[r_shift:r_shift + 1, :]
            h_ref[...] = hv.astype(bf16)
            df = (0.5 * mod_ref[r_gate:r_gate + 1, :] * dy_ref[...]).astype(bf16)
            df_s[...] = df
            df_ref[...] = df
            acc_s[...] = jnp.zeros_like(acc_s)

        zgv = zg_ref[...].astype(f32)
        zuv = zu_ref[...].astype(f32)
        s = _sigmoid(zgv)
        sil = zgv * s
        a_ref[...] = (sil * zuv).astype(bf16)
        da = _dg(df_s[...], wo_ref[...], NT)
        dzu = (da * sil).astype(bf16)
        dzg = (da * zuv * (s * (1.0 + zgv * (1.0 - s)))).astype(bf16)
        dzg_ref[...] = dzg
        dzu_ref[...] = dzu
        acc_s[...] += _dg(dzg, wg_ref[...], NT) + _dg(dzu, wu_ref[...], NT)

        @pl.when(j == nj - 1)
        def _():
            dh = acc_s[...]
            dyv = dy_ref[...]
            xh, rstd = _rms_parts(x_ref[...])
            gv = g_ref[...]
            n = xh * gv
            dn = dh * (1.0 + mod_ref[r_scale:r_scale + 1, :])
            red_ref[0:1, :] += _rowsum(dh)
            red_ref[1:2, :] += _rowsum(dh * n)
            red_ref[2:3, :] += _rowsum(0.5 * f_ref[...].astype(f32) * dyv)
            red_ref[3:4, :] += _rowsum(dn * xh)
            dx_ref[...] = dyv + _rms_bwd(dn * gv, xh, rstd)

    row = lambda i, j: (i, 0)
    tile = lambda i, j: (j, i, 0)
    shard = jax.ShapeDtypeStruct((NFS, S, tf), bf16)
    return _pcall(
        body, (x, dy, zg, zu, fo, mod, g, wi, wi, wo), name=name, comm=comm,
        grid=(S // tm, nj),
        in_specs=[
            pl.BlockSpec((tm, D), row), pl.BlockSpec((tm, D), row),
            pl.BlockSpec((None, tm, tf), tile), pl.BlockSpec((None, tm, tf), tile),
            pl.BlockSpec((tm, D), row),
            _full(mod.shape), _full(g.shape),
            pl.BlockSpec((None, D, tf), lambda i, j: (j, 0, 0)),
            pl.BlockSpec((None, D, tf), lambda i, j: (j + NFS, 0, 0)),
            pl.BlockSpec((tf, D), lambda i, j: (j, 0)),
        ],
        out_specs=[
            pl.BlockSpec((tm, D), row), pl.BlockSpec((tm, D), row), pl.BlockSpec((tm, D), row),
            pl.BlockSpec((None, tm, tf), tile), pl.BlockSpec((None, tm, tf), tile), pl.BlockSpec((None, tm, tf), tile),
            _full((8, D)),
        ],
        out_shape=[
            jax.ShapeDtypeStruct((S, D), f32), jax.ShapeDtypeStruct((S, D), bf16), jax.ShapeDtypeStruct((S, D), bf16),
            shard, shard, shard,
            jax.ShapeDtypeStruct((8, D), f32),
        ],
        scratch_shapes=[pltpu.VMEM((tm, D), f32), pltpu.VMEM((tm, D), bf16)],
    )


def matmul_tn(a, b, M, N, bm, bn, bk, name, a_col_block=0, out_dtype=f32):
    S = b.shape[0]
    nk = S // bk

    def body(a_ref, b_ref, o_ref, acc_s):
        k = pl.program_id(2)

        @pl.when(k == 0)
        def _():
            acc_s[...] = jnp.zeros_like(acc_s)

        acc_s[...] += _dg(a_ref[...].astype(bf16), b_ref[...].astype(bf16), TN)

        @pl.when(k == nk - 1)
        def _():
            o_ref[...] = acc_s[...].astype(out_dtype)

    return pl.pallas_call(
        body, name=name,
        grid=(M // bm, N // bn, nk),
        in_specs=[
            pl.BlockSpec((bk, bm), lambda i, j, k: (k, i + a_col_block)),
            pl.BlockSpec((bk, bn), lambda i, j, k: (k, j)),
        ],
        out_specs=pl.BlockSpec((bm, bn), lambda i, j, k: (i, j)),
        out_shape=jax.ShapeDtypeStruct((M, N), out_dtype),
        scratch_shapes=[pltpu.VMEM((bm, bn), f32)],
        compiler_params=_cp(3),
    )(a, b)


def dwi_pieces(h, dzg, dzu, bk, name, comm=None):
    S = h.shape[0]
    nk = S // bk

    def body(h_ref, g_ref, u_ref, o_ref, acc_s):
        j = pl.program_id(0)
        k = pl.program_id(1)

        @pl.when(k == 0)
        def _():
            acc_s[...] = jnp.zeros_like(acc_s)

        @pl.when(j < NFS)
        def _():
            acc_s[...] += _dg(h_ref[...], g_ref[...], TN)

        @pl.when(j >= NFS)
        def _():
            acc_s[...] += _dg(h_ref[...], u_ref[...], TN)

        @pl.when(k == nk - 1)
        def _():
            o_ref[...] = acc_s[...].astype(bf16)

    (out,), comm_outs = _pcall(
        body, (h, dzg, dzu), name=name, comm=comm,
        grid=(2 * NFS, nk),
        in_specs=[
            pl.BlockSpec((bk, D), lambda j, k: (k, 0)),
            pl.BlockSpec((None, bk, FS), lambda j, k: (jnp.minimum(j, NFS - 1), jnp.where(j < NFS, k, nk - 1), 0)),
            pl.BlockSpec((None, bk, FS), lambda j, k: (jnp.maximum(j - NFS, 0), jnp.where(j >= NFS, k, 0), 0)),
        ],
        out_specs=[pl.BlockSpec((None, D, FS), lambda j, k: (j, 0, 0))],
        out_shape=[jax.ShapeDtypeStruct((2 * NFS, D, FS), bf16)],
        scratch_shapes=[pltpu.VMEM((D, FS), f32)],
    )
    return out, comm_outs


def dwo_pieces(a, df, bk, name, comm=None):
    S = df.shape[0]
    nk = S // bk

    def body(a_ref, d_ref, o_ref, acc_s):
        k = pl.program_id(1)

        @pl.when(k == 0)
        def _():
            acc_s[...] = jnp.zeros_like(acc_s)

        acc_s[...] += _dg(a_ref[...], d_ref[...], TN)

        @pl.when(k == nk - 1)
        def _():
            o_ref[...] = acc_s[...].astype(bf16)

    (out,), comm_outs = _pcall(
        body, (a, df), name=name, comm=comm,
        grid=(NFS, nk),
        in_specs=[pl.BlockSpec((None, bk, FS), lambda j, k: (j, k, 0)), pl.BlockSpec((bk, D), lambda j, k: (k, 0))],
        out_specs=[pl.BlockSpec((FS, D), lambda j, k: (j, 0))],
        out_shape=[jax.ShapeDtypeStruct((F, D), bf16)],
        scratch_shapes=[pltpu.VMEM((FS, D), f32)],
    )
    return out, comm_outs


def mixin_fwd(x1, mod, g, win, wgu, bgate, tm, name):
    S = x1.shape[0]

    def body(x_ref, mod_ref, g_ref, win_ref, wgu_ref, bg_ref, z_ref, la_ref):
        xh, _ = _rms_parts(x_ref[...])
        hv = xh * g_ref[...] * (1.0 + mod_ref[4:5, :]) + mod_ref[3:4, :]
        z = _dot(hv.astype(bf16), win_ref[...])
        z_ref[...] = z
        glr = z[:, DINP - 128:]
        pre = _dot(glr.astype(bf16), wgu_ref[...]) + bg_ref[...]
        la_ref[...] = (jnp.minimum(pre, 0.0) - jnp.log(1.0 + jnp.exp(-jnp.abs(pre)))) * (1.0 / TAU)

    return pl.pallas_call(
        body, name=name,
        grid=(S // tm,),
        in_specs=[pl.BlockSpec((tm, D), lambda i: (i, 0)), _full(mod.shape), _full(g.shape),
                  _full(win.shape), _full(wgu.shape), _full(bgate.shape)],
        out_specs=[pl.BlockSpec((tm, DINP), lambda i: (i, 0)), pl.BlockSpec((tm, DQK), lambda i: (i, 0))],
        out_shape=[jax.ShapeDtypeStruct((S, DINP), f32), jax.ShapeDtypeStruct((S, DQK), f32)],
        compiler_params=_cp(1),
    )(x1, mod, g, win, wgu, bgate)


def mixin_bwd(x1, dres, dzab, dq, dk, dv, dr, dpre, mod, g, win, wgu, tm, name, comm=None):
    S = x1.shape[0]

    def body(x_ref, dres_ref, dzab_ref, dq_ref, dk_ref, dv_ref, dr_ref, dpre_ref, mod_ref, g_ref, win_ref, wgu_ref,
             dx_ref, h_ref, dz_ref, red_ref):
        @pl.when(pl.program_id(0) == 0)
        def _():
            red_ref[...] = jnp.zeros_like(red_ref)

        dglr = _dg(dpre_ref[...].astype(bf16), wgu_ref[...], NT)
        dz = jnp.concatenate([dzab_ref[...], dq_ref[...], dk_ref[...], dv_ref[...], dr_ref[...], dglr], axis=1).astype(bf16)
        dz_ref[...] = dz
        dh = _dg(dz, win_ref[...], NT)
        xh, rstd = _rms_parts(x_ref[...])
        gv = g_ref[...]
        n = xh * gv
        sc = 1.0 + mod_ref[4:5, :]
        h_ref[...] = (n * sc + mod_ref[3:4, :]).astype(bf16)
        dn = dh * sc
        red_ref[0:1, :] += _rowsum(dh)
        red_ref[1:2, :] += _rowsum(dh * n)
        red_ref[2:3, :] += _rowsum(dn * xh)
        dx_ref[...] = dres_ref[...] + _rms_bwd(dn * gv, xh, rstd)

    row = lambda i: (i, 0)
    return _pcall(
        body, (x1, dres, dzab, dq, dk, dv, dr, dpre, mod, g, win, wgu), name=name, comm=comm,
        grid=(S // tm,),
        in_specs=[pl.BlockSpec((tm, D), row), pl.BlockSpec((tm, D), row),
                  pl.BlockSpec((tm, 2 * DC), row), pl.BlockSpec((tm, DQK), row), pl.BlockSpec((tm, DQK), row),
                  pl.BlockSpec((tm, DG), row), pl.BlockSpec((tm, DG), row), pl.BlockSpec((tm, DQK), row),
                  _full(mod.shape), _full(g.shape), _full(win.shape), _full(wgu.shape)],
        out_specs=[pl.BlockSpec((tm, D), row), pl.BlockSpec((tm, D), row), pl.BlockSpec((tm, DINP), row), _full((8, D))],
        out_shape=[jax.ShapeDtypeStruct((S, D), f32), jax.ShapeDtypeStruct((S, D), bf16),
                   jax.ShapeDtypeStruct((S, DINP), bf16), jax.ShapeDtypeStruct((8, D), f32)],
    )


def _glu(zab):
    return zab[:, :DC] * _sigmoid(zab[:, DC:])


def _shift_copies(src_s, dst_s, tc):
    n = tc + HALO - SUBLANES
    for b in range(1, SUBLANES):
        dst_s[b, 0:n, :] = src_s[b:b + n, :]


def _shifted(src_s, dst_s, o, tc):
    b = o % SUBLANES
    a = o - b
    return src_s[a:a + tc, :] if b == 0 else dst_s[b, a:a + tc, :]


def conv_fwd(z, wdw, cpar, tc, name):
    S = z.shape[0]
    nb = tc // HALO

    def body(zc_ref, zp_ref, w_ref, cp_ref, y_ref, yc_ref, u_s, us_s):
        i = pl.program_id(0)
        up = _glu(zp_ref[...])
        u_s[0:HALO, :] = jnp.where(i > 0, up, 0.0)
        u_s[HALO:HALO + tc, :] = _glu(zc_ref[...])
        _shift_copies(u_s, us_s, tc)
        acc = jnp.zeros((tc, DC), f32)
        for w in range(CW):
            acc = acc + _shifted(u_s, us_s, HALO - (CW - 1) + w, tc) * w_ref[w:w + 1, :]
        y = acc + cp_ref[0:1, :]
        y_ref[...] = y
        yc = y - jnp.mean(y, axis=-1, keepdims=True)
        yl = yc * lax.rsqrt(jnp.mean(yc * yc, axis=-1, keepdims=True) + EPS) * cp_ref[1:2, :] + cp_ref[2:3, :]
        yc_ref[...] = (yl * _sigmoid(yl)).astype(bf16)

    return pl.pallas_call(
        body, name=name,
        grid=(S // tc,),
        in_specs=[pl.BlockSpec((tc, 2 * DC), lambda i: (i, 0)),
                  pl.BlockSpec((HALO, 2 * DC), lambda i: (jnp.maximum(i * nb - 1, 0), 0)),
                  _full(wdw.shape), _full(cpar.shape)],
        out_specs=[pl.BlockSpec((tc, DC), lambda i: (i, 0)), pl.BlockSpec((tc, DC), lambda i: (i, 0))],
        out_shape=[jax.ShapeDtypeStruct((S, DC), f32), jax.ShapeDtypeStruct((S, DC), bf16)],
        scratch_shapes=[pltpu.VMEM((HALO + tc, DC), f32), pltpu.VMEM((SUBLANES, HALO + tc, DC), f32)],
        compiler_params=_cp(1),
    )(z, z, wdw, cpar)


def conv_bwd(z, y, dyc, wdw, cpar, tc, name, comm=None):
    S = z.shape[0]
    nb = tc // HALO
    nt = S // tc
    last_halo = S // HALO - 1

    def body(zc_ref, zp_ref, y_ref, yn_ref, d_ref, dn_ref, w_ref, cp_ref, dz_ref, red_ref, u_s, dy_s, us_s, dys_s):
        i = pl.program_id(0)

        @pl.when(i == 0)
        def _():
            red_ref[...] = jnp.zeros_like(red_ref)

        gl = cp_ref[1:2, :]
        bl = cp_ref[2:3, :]

        def ln_bwd(yv, dv):
            yc = yv - jnp.mean(yv, axis=-1, keepdims=True)
            rstd = lax.rsqrt(jnp.mean(yc * yc, axis=-1, keepdims=True) + EPS)
            yh = yc * rstd
            yl = yh * gl + bl
            s = _sigmoid(yl)
            dyl = dv * (s * (1.0 + yl * (1.0 - s)))
            dyh = dyl * gl
            dyv = rstd * (dyh - jnp.mean(dyh, axis=-1, keepdims=True) - yh * jnp.mean(dyh * yh, axis=-1, keepdims=True))
            return dyv, dyl, yh

        dy_c, dyl_c, yh_c = ln_bwd(y_ref[...], d_ref[...])
        dy_n, _, _ = ln_bwd(yn_ref[...], dn_ref[...])
        dy_s[0:tc, :] = dy_c
        dy_s[tc:tc + HALO, :] = jnp.where(i < nt - 1, dy_n, 0.0)
        zc = zc_ref[...]
        av = zc[:, :DC]
        sb = _sigmoid(zc[:, DC:])
        u_s[0:HALO, :] = jnp.where(i > 0, _glu(zp_ref[...]), 0.0)
        u_s[HALO:HALO + tc, :] = av * sb
        _shift_copies(u_s, us_s, tc)
        _shift_copies(dy_s, dys_s, tc)
        du = jnp.zeros((tc, DC), f32)
        for w in range(CW):
            red_ref[w:w + 1, :] += _rowsum(_shifted(u_s, us_s, HALO - (CW - 1) + w, tc) * dy_c)
            du = du + _shifted(dy_s, dys_s, CW - 1 - w, tc) * w_ref[w:w + 1, :]
        red_ref[32:33, :] += _rowsum(dy_c)
        red_ref[33:34, :] += _rowsum(dyl_c * yh_c)
        red_ref[34:35, :] += _rowsum(dyl_c)
        dz_ref[...] = jnp.concatenate([du * sb, du * av * sb * (1.0 - sb)], axis=1)

    cur = lambda i: (i, 0)
    nxt = lambda i: (jnp.minimum((i + 1) * nb, last_halo), 0)
    return _pcall(
        body, (z, z, y, y, dyc, dyc, wdw, cpar), name=name, comm=comm,
        grid=(nt,),
        in_specs=[pl.BlockSpec((tc, 2 * DC), cur),
                  pl.BlockSpec((HALO, 2 * DC), lambda i: (jnp.maximum(i * nb - 1, 0), 0)),
                  pl.BlockSpec((tc, DC), cur), pl.BlockSpec((HALO, DC), nxt),
                  pl.BlockSpec((tc, DC), cur), pl.BlockSpec((HALO, DC), nxt),
                  _full(wdw.shape), _full(cpar.shape)],
        out_specs=[pl.BlockSpec((tc, 2 * DC), cur), _full((40, DC))],
        out_shape=[jax.ShapeDtypeStruct((S, 2 * DC), f32), jax.ShapeDtypeStruct((40, DC), f32)],
        scratch_shapes=[pltpu.VMEM((HALO + tc, DC), f32), pltpu.VMEM((tc + HALO, DC), f32),
                        pltpu.VMEM((SUBLANES, HALO + tc, DC), f32), pltpu.VMEM((SUBLANES, HALO + tc, DC), f32)],
    )


def _gla_consts():
    r = lax.broadcasted_iota(jnp.int32, (CH, CH), 0)
    c = lax.broadcasted_iota(jnp.int32, (CH, CH), 1)
    tril = r >= c
    lane = lax.broadcasted_iota(jnp.int32, (CH, DQK), 1)
    masks = [(lane >= h * DK) & (lane < (h + 1) * DK) for h in range(NH)]
    r4 = lax.broadcasted_iota(jnp.int32, (DQK, DQK), 0)
    c4 = lax.broadcasted_iota(jnp.int32, (DQK, DQK), 1)
    eye4 = (r4 == c4).astype(f32)
    rs = lax.broadcasted_iota(jnp.int32, (DQK, CH), 0) & (CH - 1)
    tril4 = rs >= lax.broadcasted_iota(jnp.int32, (DQK, CH), 1)
    return tril, tril4, masks, eye4


def _stack(xv, masks):
    return jnp.concatenate([jnp.where(m, xv, 0.0) for m in masks], axis=0)


def _unstack(rv, masks):
    out = jnp.where(masks[0], rv[0:CH, :], 0.0)
    for h in range(1, NH):
        out = out + jnp.where(masks[h], rv[h * CH:(h + 1) * CH, :], 0.0)
    return out


def _vstack(xv):
    return jnp.concatenate([xv[:, h * DV:(h + 1) * DV] for h in range(NH)], axis=0)


def _vunstack(xv):
    return jnp.concatenate([xv[h * CH:(h + 1) * CH, :] for h in range(NH)], axis=1)


def _gla_chunk_fwd(lac, qc, kc, vc, s_all, tril, masks, tril4):
    lmat = tril.astype(f32)
    bc = jnp.dot(lmat, lac, preferred_element_type=f32, precision=HIGHEST)
    bend = bc[CH - 1:CH, :]
    eb = jnp.exp(bc)
    enb = jnp.exp(-bc)
    ed = jnp.exp(bend - bc)
    qh = qc * (DK ** -0.5)
    qf = qh * eb
    qn = qh * enb
    kn = kc * enb
    kp = kc * eb
    kd = kc * ed
    qf_s = _stack(qf, masks).astype(bf16)
    qn_s = _stack(qn, masks).astype(bf16)
    kn_b = kn.astype(bf16)
    kp_b = kp.astype(bf16)
    attf = _dg(qf_s, kn_b, NT)
    attb = _dg(qn_s, kp_b, NT)
    a_s = jnp.where(tril4, attf, attb)
    a_b = a_s.astype(bf16)
    v_b = vc.astype(bf16)
    intra = jnp.concatenate(
        [_dot(a_b[h * CH:(h + 1) * CH, :], v_b[:, h * DV:(h + 1) * DV]) for h in range(NH)], axis=0)
    o_s = intra + _dot(qf_s, s_all.astype(bf16))
    return dict(bc=bc, bend=bend, eb=eb, enb=enb, ed=ed, qf=qf, qn=qn, kn=kn, kp=kp, kd=kd,
                qf_s=qf_s, qn_s=qn_s, kn_b=kn_b, kp_b=kp_b, a_b=a_b, v_b=v_b, o_s=o_s)


def _col_from_row(row, eye4):
    return jnp.sum(eye4 * row, axis=1, keepdims=True)


def _row_from_col(col, eye4):
    return jnp.sum(eye4 * col, axis=0, keepdims=True)


def gla_fwd(z, la, gn_s, tg, name):
    S = z.shape[0]
    nc = tg // CH

    def body(q_ref, k_ref, v_ref, r_ref, la_ref, gn_ref, yg_ref, sp_ref, st):
        @pl.when(pl.program_id(0) == 0)
        def _():
            st[...] = jnp.zeros_like(st)

        tril, tril4, masks, eye4 = _gla_consts()

        def chunk(c, carry):
            r0 = pl.multiple_of(c * CH, CH)
            s0 = pl.multiple_of(c * DQK, DQK)
            s_all = st[...]
            sp_ref[pl.ds(s0, DQK), :] = s_all
            vc = v_ref[pl.ds(r0, CH), :]
            t = _gla_chunk_fwd(la_ref[pl.ds(r0, CH), :], q_ref[pl.ds(r0, CH), :], k_ref[pl.ds(r0, CH), :], vc,
                               s_all, tril, masks, tril4)
            u_all = _dg(_stack(t["kd"], masks).astype(bf16), _vstack(vc).astype(bf16), TN)
            st[...] = _col_from_row(jnp.exp(t["bend"]), eye4) * s_all + u_all
            o_s = t["o_s"]
            on = o_s * lax.rsqrt(jnp.mean(o_s * o_s, axis=-1, keepdims=True) + EPS) * gn_ref[...]
            rc = r_ref[pl.ds(r0, CH), :]
            yg_ref[pl.ds(r0, CH), :] = (_vunstack(on) * (rc * _sigmoid(rc))).astype(bf16)
            return carry

        lax.fori_loop(0, nc, chunk, 0)

    return pl.pallas_call(
        body, name=name,
        grid=(S // tg,),
        in_specs=[pl.BlockSpec((tg, DQK), lambda i: (i, 4)), pl.BlockSpec((tg, DQK), lambda i: (i, 5)),
                  pl.BlockSpec((tg, DG), lambda i: (i, 3)), pl.BlockSpec((tg, DG), lambda i: (i, 4)),
                  pl.BlockSpec((tg, DQK), lambda i: (i, 0)), _full(gn_s.shape)],
        out_specs=[pl.BlockSpec((tg, DG), lambda i: (i, 0)), pl.BlockSpec((nc * DQK, DV), lambda i: (i, 0))],
        out_shape=[jax.ShapeDtypeStruct((S, DG), bf16), jax.ShapeDtypeStruct((S // CH * DQK, DV), f32)],
        scratch_shapes=[pltpu.VMEM((DQK, DV), f32)],
        compiler_params=_cp(1),
    )(z, z, z, z, la, gn_s)


def gla_bwd(z, la, sprev, dyg, gn_s, tg, name, comm=None):
    S = z.shape[0]
    nc = tg // CH
    nt = S // tg

    def body(q_ref, k_ref, v_ref, r_ref, la_ref, sp_ref, dy_ref, gn_ref,
             dq_ref, dk_ref, dv_ref, dr_ref, dpre_ref, redg_ref, redb_ref, gs):
        @pl.when(pl.program_id(0) == 0)
        def _():
            gs[...] = jnp.zeros_like(gs)
            redg_ref[...] = jnp.zeros_like(redg_ref)
            redb_ref[...] = jnp.zeros_like(redb_ref)

        tril, tril4, masks, eye4 = _gla_consts()
        umat = (lax.broadcasted_iota(jnp.int32, (CH, CH), 0) <= lax.broadcasted_iota(jnp.int32, (CH, CH), 1)).astype(f32)
        last_row = lax.broadcasted_iota(jnp.int32, (CH, DQK), 0) == CH - 1

        def chunk(tt, carry):
            c = nc - 1 - tt
            r0 = pl.multiple_of(c * CH, CH)
            s0 = pl.multiple_of(c * DQK, DQK)
            s_all = sp_ref[pl.ds(s0, DQK), :]
            lac = la_ref[pl.ds(r0, CH), :]
            vc = v_ref[pl.ds(r0, CH), :]
            rc = r_ref[pl.ds(r0, CH), :]
            t = _gla_chunk_fwd(lac, q_ref[pl.ds(r0, CH), :], k_ref[pl.ds(r0, CH), :], vc, s_all, tril, masks, tril4)
            g_all = gs[...]
            g_b = g_all.astype(bf16)
            s_b = s_all.astype(bf16)
            o_s = t["o_s"]
            rstd = lax.rsqrt(jnp.mean(o_s * o_s, axis=-1, keepdims=True) + EPS)
            oh = o_s * rstd
            gnv = gn_ref[...]
            sr = _sigmoid(rc)
            dyv = dy_ref[pl.ds(r0, CH), :]
            dr_ref[pl.ds(r0, CH), :] = dyv * _vunstack(oh * gnv) * (sr * (1.0 + rc * (1.0 - sr)))
            don = _vstack(dyv * (rc * sr))
            redg_ref[...] += don * oh
            doh = don * gnv
            do_s = rstd * (doh - oh * jnp.mean(doh * oh, axis=-1, keepdims=True))
            do_b = do_s.astype(bf16)
            v_b = t["v_b"]
            vst_b = _vstack(vc).astype(bf16)
            kd_s = _stack(t["kd"], masks).astype(bf16)
            da_s = jnp.concatenate(
                [_dg(do_b[h * CH:(h + 1) * CH, :], v_b[:, h * DV:(h + 1) * DV], NT) for h in range(NH)], axis=0)
            a_b = t["a_b"]
            dv_s = jnp.concatenate(
                [_dg(a_b[h * CH:(h + 1) * CH, :], do_b[h * CH:(h + 1) * CH, :], TN) for h in range(NH)], axis=0)
            dv_s = dv_s + _dot(kd_s, g_b)
            dv_ref[pl.ds(r0, CH), :] = _vunstack(dv_s)
            gend = jnp.exp(t["bend"])
            gcol = _col_from_row(gend, eye4)
            gs[...] = gcol * g_all + _dg(t["qf_s"], do_b, TN)
            dgcol = jnp.sum(g_all * s_all, axis=1, keepdims=True)
            dbend = _row_from_col(dgcol * gcol, eye4)
            dkd = _unstack(_dg(vst_b, g_b, NT), masks)
            daf = jnp.where(tril4, da_s, 0.0).astype(bf16)
            dab = jnp.where(tril4, 0.0, da_s).astype(bf16)
            dqf = _unstack(_dot(daf, t["kn_b"]) + _dg(do_b, s_b, NT), masks)
            dqn = _unstack(_dot(dab, t["kp_b"]), masks)
            dkn = _dg(daf, t["qf_s"], TN)
            dkp = _dg(dab, t["qn_s"], TN)
            dq_ref[pl.ds(r0, CH), :] = (dqf * t["eb"] + dqn * t["enb"]) * (DK ** -0.5)
            dk_ref[pl.ds(r0, CH), :] = dkn * t["enb"] + dkp * t["eb"] + dkd * t["ed"]
            dkd_kd = dkd * t["kd"]
            dbc = dqf * t["qf"] - dqn * t["qn"] - dkn * t["kn"] + dkp * t["kp"] - dkd_kd
            dbc = dbc + jnp.where(last_row, _rowsum(dkd_kd) + dbend, 0.0)
            dla = jnp.dot(umat, dbc, preferred_element_type=f32, precision=HIGHEST)
            dpre = dla * (1.0 / TAU) * (1.0 - jnp.exp(TAU * lac))
            dpre_ref[pl.ds(r0, CH), :] = dpre
            redb_ref[...] += dpre
            return carry

        lax.fori_loop(0, nc, chunk, 0)

    rev = lambda col: (lambda i: (nt - 1 - i, col))
    return _pcall(
        body, (z, z, z, z, la, sprev, dyg, gn_s), name=name, comm=comm,
        grid=(nt,),
        in_specs=[pl.BlockSpec((tg, DQK), rev(4)), pl.BlockSpec((tg, DQK), rev(5)),
                  pl.BlockSpec((tg, DG), rev(3)), pl.BlockSpec((tg, DG), rev(4)),
                  pl.BlockSpec((tg, DQK), rev(0)), pl.BlockSpec((nc * DQK, DV), rev(0)),
                  pl.BlockSpec((tg, DG), rev(0)), _full(gn_s.shape)],
        out_specs=[pl.BlockSpec((tg, DQK), rev(0)), pl.BlockSpec((tg, DQK), rev(0)),
                   pl.BlockSpec((tg, DG), rev(0)), pl.BlockSpec((tg, DG), rev(0)), pl.BlockSpec((tg, DQK), rev(0)),
                   _full((DQK, DV)), _full((CH, DQK))],
        out_shape=[jax.ShapeDtypeStruct((S, DQK), f32), jax.ShapeDtypeStruct((S, DQK), f32),
                   jax.ShapeDtypeStruct((S, DG), f32), jax.ShapeDtypeStruct((S, DG), f32), jax.ShapeDtypeStruct((S, DQK), f32),
                   jax.ShapeDtypeStruct((DQK, DV), f32), jax.ShapeDtypeStruct((CH, DQK), f32)],
        scratch_shapes=[pltpu.VMEM((DQK, DV), f32)],
    )


def mixout_fwd(x1, yc, yg, mod, wout, tm, name):
    S = x1.shape[0]

    def body(x_ref, yc_ref, yg_ref, mod_ref, w_ref, xo_ref):
        mixo = _dot(yc_ref[...], w_ref[0:DC, :]) + _dot(yg_ref[...], w_ref[DC:DC + DG, :])
        xo_ref[...] = x_ref[...] + mod_ref[5:6, :] * mixo

    row = lambda i: (i, 0)
    return pl.pallas_call(
        body, name=name,
        grid=(S // tm,),
        in_specs=[pl.BlockSpec((tm, D), row), pl.BlockSpec((tm, DC), row), pl.BlockSpec((tm, DG), row),
                  _full(mod.shape), _full(wout.shape)],
        out_specs=pl.BlockSpec((tm, D), row),
        out_shape=jax.ShapeDtypeStruct((S, D), f32),
        compiler_params=_cp(1),
    )(x1, yc, yg, mod, wout)


def mixout_bwd(dx2, yc, yg, mod, wout, tm, name):
    S = dx2.shape[0]

    def body(dx_ref, yc_ref, yg_ref, mod_ref, w_ref, dm_ref, dyc_ref, dyg_ref, red_ref):
        @pl.when(pl.program_id(0) == 0)
        def _():
            red_ref[...] = jnp.zeros_like(red_ref)

        dxv = dx_ref[...]
        mixo = _dot(yc_ref[...], w_ref[0:DC, :]) + _dot(yg_ref[...], w_ref[DC:DC + DG, :])
        red_ref[0:1, :] += _rowsum(dxv * mixo)
        dm = (mod_ref[5:6, :] * dxv).astype(bf16)
        dm_ref[...] = dm
        dycat = _dg(dm, w_ref[...], NT)
        dyc_ref[...] = dycat[:, :DC]
        dyg_ref[...] = dycat[:, DC:]

    row = lambda i: (i, 0)
    return pl.pallas_call(
        body, name=name,
        grid=(S // tm,),
        in_specs=[pl.BlockSpec((tm, D), row), pl.BlockSpec((tm, DC), row), pl.BlockSpec((tm, DG), row),
                  _full(mod.shape), _full(wout.shape)],
        out_specs=[pl.BlockSpec((tm, D), row), pl.BlockSpec((tm, DC), row), pl.BlockSpec((tm, DG), row), _full((8, D))],
        out_shape=[jax.ShapeDtypeStruct((S, D), bf16), jax.ShapeDtypeStruct((S, DC), f32),
                   jax.ShapeDtypeStruct((S, DG), f32), jax.ShapeDtypeStruct((8, D), f32)],
        compiler_params=_cp(1),
    )(dx2, yc, yg, mod, wout)


def final_fwd_bwd(x, tgt, fmod, g, tm, name):
    S = x.shape[0]

    def body(x_ref, t_ref, fm_ref, g_ref, dx_ref, red_ref):
        @pl.when(pl.program_id(0) == 0)
        def _():
            red_ref[...] = jnp.zeros_like(red_ref)

        xh, rstd = _rms_parts(x_ref[...])
        gv = g_ref[...]
        n = xh * gv
        sc = 1.0 + fm_ref[1:2, :]
        e = n * sc + fm_ref[0:1, :] - t_ref[...]
        red_ref[0:1, :] += _rowsum(e * e) * (0.5 / D)
        dy = e * (1.0 / D)
        dn = dy * sc
        red_ref[1:2, :] += _rowsum(dy)
        red_ref[2:3, :] += _rowsum(dy * n)
        red_ref[3:4, :] += _rowsum(dn * xh)
        dx_ref[...] = _rms_bwd(dn * gv, xh, rstd)

    row = lambda i: (i, 0)
    return pl.pallas_call(
        body, name=name,
        grid=(S // tm,),
        in_specs=[pl.BlockSpec((tm, D), row), pl.BlockSpec((tm, D), row), _full(fmod.shape), _full(g.shape)],
        out_specs=[pl.BlockSpec((tm, D), row), _full((8, D))],
        out_shape=[jax.ShapeDtypeStruct((S, D), f32), jax.ShapeDtypeStruct((8, D), f32)],
        compiler_params=_cp(1),
    )(x, tgt, fmod, g)


def ada_fwd(c_all, w, b, name):
    n = w.shape[1]

    def body(c_ref, w_ref, b_ref, o_ref):
        cv = c_ref[...]
        o_ref[...] = jnp.dot(cv * _sigmoid(cv), w_ref[...], preferred_element_type=f32, precision=HIGHEST) + b_ref[...]

    return pl.pallas_call(
        body, name=name,
        in_specs=[_full(c_all.shape), _full(w.shape), _full(b.shape)],
        out_specs=_full((N_DEV, n)),
        out_shape=jax.ShapeDtypeStruct((N_DEV, n), f32),
        grid=(1,),
        compiler_params=_cp(1),
    )(c_all, w, b)


def ada_wgrad(c_all_t, dm, name):
    n = dm.shape[1]

    def body(c_ref, d_ref, o_ref):
        cv = c_ref[...]
        o_ref[...] = jnp.dot(cv * _sigmoid(cv), d_ref[...], preferred_element_type=f32, precision=HIGHEST)

    return pl.pallas_call(
        body, name=name,
        in_specs=[_full(c_all_t.shape), _full(dm.shape)],
        out_specs=_full((D, n)),
        out_shape=jax.ShapeDtypeStruct((D, n), f32),
        grid=(1,),
        compiler_params=_cp(1),
    )(c_all_t, dm)


def _adam_math(gv, wv, mv, vv):
    m = ADAM_B1 * mv + (1.0 - ADAM_B1) * gv
    v = ADAM_B2 * vv + (1.0 - ADAM_B2) * (gv * gv)
    m_hat = m / (1.0 - ADAM_B1 ** ADAM_STEP)
    v_hat = v / (1.0 - ADAM_B2 ** ADAM_STEP)
    delta = -ADAM_LR * (m_hat / (jnp.sqrt(v_hat) + ADAM_EPS) + ADAM_WD * wv)
    return delta, m, v


def adam_parts(parts, w, m, v, tr, name, comm=None):
    L, R, C = w.shape
    nt = R // tr

    def body(*refs):
        p_refs = refs[:L]
        w_ref, m_ref, v_ref, g_ref, d_ref, mo_ref, vo_ref = refs[L:]
        lyr = pl.program_id(0)
        for l in range(L):
            @pl.when(lyr == l)
            def _(p_ref=p_refs[l]):
                gv = p_ref[0].astype(f32)
                for k in range(1, N_DEV):
                    gv = gv + p_ref[k].astype(f32)
                g_ref[...] = gv
                d_ref[...], mo_ref[...], vo_ref[...] = _adam_math(gv, w_ref[...], m_ref[...], v_ref[...])

    def part_spec(l):
        return pl.BlockSpec((N_DEV, tr, C), lambda lyr, i: (0, jnp.where(lyr == l, i, jnp.where(lyr < l, 0, nt - 1)), 0))

    spec = pl.BlockSpec((None, tr, C), lambda lyr, i: (lyr, i, 0))
    shp = jax.ShapeDtypeStruct((L, R, C), f32)
    return _pcall(
        body, (*parts, w, m, v), name=name, comm=comm,
        grid=(L, nt),
        in_specs=[part_spec(l) for l in range(L)] + [spec, spec, spec],
        out_specs=[spec, spec, spec, spec],
        out_shape=[shp, shp, shp, shp],
    )


def adam_plain(gr, w, m, v, tr, name):
    R, C = w.shape

    def body(g_ref, w_ref, m_ref, v_ref, d_ref, mo_ref, vo_ref):
        d_ref[...], mo_ref[...], vo_ref[...] = _adam_math(g_ref[...], w_ref[...], m_ref[...], v_ref[...])

    spec = pl.BlockSpec((tr, C), lambda i: (i, 0))
    shp = jax.ShapeDtypeStruct((R, C), f32)
    return pl.pallas_call(
        body, name=name,
        grid=(R // tr,),
        in_specs=[spec, spec, spec, spec],
        out_specs=[spec, spec, spec],
        out_shape=[shp, shp, shp],
        compiler_params=_cp(1),
    )(gr, w, m, v)


def sum8(parts, name):
    _, R, C = parts.shape

    def body(p_ref, o_ref):
        acc = p_ref[0]
        for k in range(1, N_DEV):
            acc = acc + p_ref[k]
        o_ref[...] = acc

    return pl.pallas_call(
        body, name=name,
        grid=(1,),
        in_specs=[_full(parts.shape)],
        out_specs=_full((R, C)),
        out_shape=jax.ShapeDtypeStruct((R, C), f32),
        compiler_params=_cp(1),
    )(parts)


def _place():
    return lax.axis_index("x"), lax.axis_index("y"), lax.axis_index("c")


def _gather_steps(ins, outs, send_sems, recv_sems, local_sems, place):
    n = len(ins)
    x, y, c = place
    me, sibling = (x, y, c), (x, y, 1 - c)
    chips = [(1 - x, y), (x, 1 - y), (1 - x, 1 - y)]

    def slot(a, p):
        return outs[a].at[4 * p[0] + 2 * p[1] + p[2]]

    def copy(a, k, block, to, src=None):
        return pltpu.make_async_remote_copy(
            src_ref=slot(a, block) if src is None else src, dst_ref=slot(a, block),
            send_sem=send_sems.at[a * 7 + k], recv_sem=recv_sems.at[a * 7 + k],
            device_id=to, device_id_type=MESH)

    def mine():
        return [pltpu.make_async_copy(ins[a], slot(a, me), local_sems.at[a]) for a in range(n)]

    def first():
        cps = []
        for a in range(n):
            cps.append(copy(a, 0, me, sibling, src=ins[a]))
            cps += [copy(a, 1 + j, me, (*chip, c), src=ins[a]) for j, chip in enumerate(chips)]
        return cps

    def start():
        for cp in mine() + first():
            cp.start()

    def forward():
        for j, chip in enumerate(chips):
            for a in range(n):
                copy(a, 1 + j, (*chip, c), me).wait_recv()
                copy(a, 4 + j, (*chip, c), sibling).start()

    def finish():
        for a in range(n):
            copy(a, 0, sibling, me).wait_recv()
            for j, chip in enumerate(chips):
                copy(a, 4 + j, (*chip, 1 - c), me).wait_recv()
        for cp in first() + [copy(a, 4 + j, (*chip, c), sibling) for j, chip in enumerate(chips) for a in range(n)]:
            cp.wait_send()
        for cp in mine():
            cp.wait()

    return start, forward, finish


def _exchange_steps(ins, outs, send_sems, recv_sems, local_sems, place):
    n = len(ins)
    x, y, c = place
    me_i = 4 * x + 2 * y + c

    def mine():
        return [pltpu.make_async_copy(ins[a].at[me_i], outs[a].at[me_i], local_sems.at[a]) for a in range(n)]

    def copies(receiving):
        cps = []
        for k in range(1, N_DEV):
            px = 1 - x if (k >> 2) & 1 else x
            py = 1 - y if (k >> 1) & 1 else y
            pc = 1 - c if k & 1 else c
            p_i = 4 * px + 2 * py + pc
            for a in range(n):
                sem = a * 7 + k - 1
                cps.append(pltpu.make_async_remote_copy(
                    src_ref=ins[a].at[p_i], dst_ref=outs[a].at[p_i if receiving else me_i],
                    send_sem=send_sems.at[sem], recv_sem=recv_sems.at[sem],
                    device_id=(px, py, pc), device_id_type=MESH))
        return cps

    def start():
        for cp in mine() + copies(False):
            cp.start()

    def finish():
        for cp in copies(True):
            cp.wait_recv()
        for cp in copies(False):
            cp.wait_send()
        for cp in mine():
            cp.wait()

    return start, None, finish


_COMM_STEPS = {"gather": _gather_steps, "exchange": _exchange_steps}


def _comm_out_shapes(kind, arrs):
    if kind == "gather":
        return [jax.ShapeDtypeStruct((N_DEV,) + a.shape, a.dtype) for a in arrs]
    return [jax.ShapeDtypeStruct(a.shape, a.dtype) for a in arrs]


def _comm_sems(n):
    return [pltpu.SemaphoreType.DMA((7 * n,)), pltpu.SemaphoreType.DMA((7 * n,)), pltpu.SemaphoreType.DMA((n,))]


def _pcall(body, args, *, name, grid, in_specs, out_specs, out_shape, scratch_shapes=(), comm=None):
    in_specs, out_specs, out_shape = list(in_specs), list(out_specs), list(out_shape)
    scratch_shapes = list(scratch_shapes)
    cparams = _cp(len(grid))
    if comm is None:
        outs = pl.pallas_call(body, name=name, grid=grid, in_specs=in_specs, out_specs=out_specs, out_shape=out_shape,
                              scratch_shapes=scratch_shapes, compiler_params=cparams)(*args)
        return list(outs), []
    kind, arrs = comm
    nc, n_in, n_out, n_scr = len(arrs), len(in_specs), len(out_specs), len(scratch_shapes)
    total = 1
    for gdim in grid:
        total *= gdim
    forward_step = (total * 5) // 8

    def hosted(*refs):
        core_in, c_in = refs[:n_in], refs[n_in:n_in + nc]
        core_out = refs[n_in + nc:n_in + nc + n_out]
        c_out = refs[n_in + nc + n_out:n_in + 2 * nc + n_out]
        rest = refs[n_in + 2 * nc + n_out:]
        step = pl.program_id(0)
        for ax in range(1, len(grid)):
            step = step * grid[ax] + pl.program_id(ax)
        start, forward, finish = _COMM_STEPS[kind](c_in, c_out, *rest[n_scr:], _place())
        pl.when(step == 0)(start)
        if forward is not None:
            pl.when(step == forward_step)(forward)
        body(*core_in, *core_out, *rest[:n_scr])
        pl.when(step == total - 1)(finish)

    any_spec = pl.BlockSpec(memory_space=pl.ANY)
    outs = pl.pallas_call(
        hosted, name=name, grid=grid,
        in_specs=in_specs + [any_spec] * nc,
        out_specs=out_specs + [any_spec] * nc,
        out_shape=out_shape + _comm_out_shapes(kind, arrs),
        scratch_shapes=scratch_shapes + _comm_sems(nc),
        compiler_params=cparams)(*args, *arrs)
    return list(outs[:n_out]), list(outs[n_out:])


def _comm_call(kind, arrs, name):
    n = len(arrs)

    def body(*refs):
        start, forward, finish = _COMM_STEPS[kind](refs[:n], refs[n:2 * n], *refs[2 * n:], _place())
        start()
        if forward is not None:
            forward()
        finish()

    any_spec = pl.BlockSpec(memory_space=pl.ANY)
    return pl.pallas_call(
        body, name=name,
        in_specs=[any_spec] * n, out_specs=[any_spec] * n,
        out_shape=_comm_out_shapes(kind, arrs), scratch_shapes=_comm_sems(n),
    )(*arrs)


def all_gather(arrs, name):
    return _comm_call("gather", arrs, name)


def all_to_all(arrs, name):
    return _comm_call("exchange", arrs, name)


def _tiles(S):
    t = min(512, S)
    return dict(ffn=t, row=t, conv=t, gla=t, bk=min(1024, S))


BIG = ("wi1", "wo1", "win", "wout", "wi2", "wo2")


def _col_shards_to_full(gathered):
    n, r, c = gathered.shape
    return jnp.transpose(gathered, (1, 0, 2)).reshape(r, n * c)


def _win_full(win_a):
    return jnp.pad(_col_shards_to_full(win_a), ((0, 0), (0, DINP - DIN)))


def train_pass(x, tgt, mods, fmod, sh, ws, wi1_first, wo1_first):
    S = x.shape[0]
    T = _tiles(S)
    bk = T["bk"]
    full = [dict() for _ in range(DEPTH)]
    full[0]["wi1"], full[0]["wo1"] = wi1_first, wo1_first.reshape(F, D)
    saved = []
    xc = x
    for l in range(DEPTH):
        w, fw = ws[f"L{l}"], full[l]
        x0 = xc
        names = ("win", "wout", "wi2", "wo2") if l == 0 else ("wi2", "wo2")
        (x1, zg1, zu1, f1), got = ffn_fwd(x0, mods[l], w["g1"], fw["wi1"], fw["wo1"], (0, 1, 2), T["ffn"], f"ffn1_fwd_{l}",
                                          comm=("gather", [sh[n][l] for n in names]))
        fw.update(zip(names, got))
        if l == 0:
            fw["win"], fw["wout"] = _win_full(fw["win"]), fw["wout"].reshape(D, D)
        fw["wo2"] = fw["wo2"].reshape(F, D)
        z, la = mixin_fwd(x1, mods[l], w["g2"], fw["win"], w["wgu"], w["bgate"], T["row"], f"mixin_fwd_{l}")
        y, yc = conv_fwd(z, w["wdw"], w["cpar"], T["conv"], f"conv_fwd_{l}")
        yg, sprev = gla_fwd(z, la, w["gn_s"], T["gla"], f"gla_fwd_{l}")
        x2 = mixout_fwd(x1, yc, yg, mods[l], fw["wout"], T["row"], f"mixout_fwd_{l}")
        names = ("wi1", "wo1", "win", "wout") if l + 1 < DEPTH else ()
        (x3, zg2, zu2, f2), got = ffn_fwd(x2, mods[l], w["g3"], fw["wi2"], fw["wo2"], (6, 7, 8), T["ffn"], f"ffn2_fwd_{l}",
                                          comm=("gather", [sh[n][l + 1] for n in names]) if names else None)
        if names:
            nx = full[l + 1]
            nx["wi1"], nx["wo1"], nx["win"], nx["wout"] = got[0], got[1].reshape(F, D), _win_full(got[2]), got[3].reshape(D, D)
        saved.append(dict(x0=x0, x1=x1, x2=x2, zg1=zg1, zu1=zu1, f1=f1, zg2=zg2, zu2=zu2, f2=f2,
                          z=z, la=la, y=y, yc=yc, yg=yg, sprev=sprev))
        xc = x3

    dx, redf = final_fwd_bwd(xc, tgt, fmod, ws["gf"], T["row"], "loss_head")
    loss_lanes = redf[0]
    dfmod = redf[1:3]
    grads = {"gf": redf[3]}
    dmods = [None] * DEPTH
    recv = {n: [None] * DEPTH for n in BIG}
    pending = None

    for l in reversed(range(DEPTH)):
        w, fw, sv = ws[f"L{l}"], full[l], saved[l]
        g = {}
        (dx2, h, df, a, dzg, dzu, red3), got = ffn_bwd(sv["x2"], dx, sv["zg2"], sv["zu2"], sv["f2"], mods[l], w["g3"],
                                                       fw["wi2"], fw["wo2"], (6, 7, 8), T["ffn"], f"ffn2_bwd_{l}",
                                                       comm=("exchange", pending) if pending else None)
        if pending:
            recv["wi1"][l + 1], recv["wo1"][l + 1] = got
        p_wi2, _ = dwi_pieces(h, dzg, dzu, bk, f"dwi2_{l}")
        p_wo2, _ = dwo_pieces(a, df, bk, f"dwo2_{l}")
        dmix, dyc, dyg, red_o = mixout_bwd(dx2, sv["yc"], sv["yg"], mods[l], fw["wout"], T["row"], f"mixout_bwd_{l}")
        p_wout = jnp.concatenate([matmul_tn(sv["yc"], dmix, DC, D, DC, D, bk, f"dwout_c_{l}", out_dtype=bf16),
                                  matmul_tn(sv["yg"], dmix, DG, D, DG, D, bk, f"dwout_g_{l}", out_dtype=bf16)], axis=0)
        (dq, dk, dv, dr, dpre, redg, redb), (recv["wi2"][l],) = gla_bwd(
            sv["z"], sv["la"], sv["sprev"], dyg, w["gn_s"], T["gla"], f"gla_bwd_{l}", comm=("exchange", [p_wi2]))
        (dzab, redc), (recv["wo2"][l],) = conv_bwd(
            sv["z"], sv["y"], dyc, w["wdw"], w["cpar"], T["conv"], f"conv_bwd_{l}",
            comm=("exchange", [p_wo2.reshape(N_DEV, F // N_DEV, D)]))
        (dx1, h2, dz, red2), (recv["wout"][l],) = mixin_bwd(
            sv["x1"], dx2, dzab, dq, dk, dv, dr, dpre, mods[l], w["g2"], fw["win"], w["wgu"], T["row"], f"mixin_bwd_{l}",
            comm=("exchange", [p_wout.reshape(N_DEV, D // N_DEV, D)]))
        dwin = matmul_tn(h2, dz, D, DINP, D, DINP, bk, f"dwin_{l}", out_dtype=bf16)[:, :DIN]
        p_win = jnp.transpose(dwin.reshape(D, N_DEV, DIN // N_DEV), (1, 0, 2))
        g["wgu"] = matmul_tn(sv["z"], dpre, 128, DQK, 128, DQK, bk, f"dwgu_{l}", a_col_block=(DINP - 128) // 128)[:GR]
        g["bgate"] = jnp.sum(redb, axis=0)
        g["gn"] = jnp.sum(redg.reshape(NH, CH, DV), axis=1)
        g["wdw"] = redc[:CW]
        g["bdw"], g["gln"], g["bln"] = redc[32], redc[33], redc[34]
        (dx0, h, df, a, dzg, dzu, red1), (recv["win"][l],) = ffn_bwd(
            sv["x0"], dx1, sv["zg1"], sv["zu1"], sv["f1"], mods[l], w["g1"], fw["wi1"], fw["wo1"], (0, 1, 2), T["ffn"],
            f"ffn1_bwd_{l}", comm=("exchange", [p_win]))
        p_wo1, _ = dwo_pieces(a, df, bk, f"dwo1_{l}")
        p_wo1 = p_wo1.reshape(N_DEV, F // N_DEV, D)
        if l > 0:
            p_wi1, _ = dwi_pieces(h, dzg, dzu, bk, f"dwi1_{l}")
            pending = [p_wi1, p_wo1]
        else:
            p_wi1, (recv["wo1"][l],) = dwi_pieces(h, dzg, dzu, bk, f"dwi1_{l}", comm=("exchange", [p_wo1]))
        g["g1"], g["g2"], g["g3"] = red1[3], red2[2], red3[3]
        dmods[l] = jnp.stack([red1[0], red1[1], red1[2], red2[0], red2[1], red_o[0], red3[0], red3[1], red3[2]], axis=0)
        grads[f"L{l}"] = g
        dx = dx0
    return loss_lanes, dx, grads, dmods, dfmod, recv, p_wi1


def _pad_rows(a, rows):
    return jnp.pad(a, ((0, rows - a.shape[0]), (0, 0)))


def kernel(x, c, w_ada, b_ada, g_norm_ffn1, w_ffn1_in, w_ffn1_out, g_norm_mix, w_in, w_dw, b_dw, g_conv_ln, b_conv_ln, w_gate_up, b_gate, g_gla_norm, w_out, g_norm_ffn2, w_ffn2_in, w_ffn2_out, g_norm_final, w_ada_final, b_ada_final, loss_target, m_w_ada, m_b_ada, m_g_norm_ffn1, m_w_ffn1_in, m_w_ffn1_out, m_g_norm_mix, m_w_in, m_w_dw, m_b_dw, m_g_conv_ln, m_b_conv_ln, m_w_gate_up, m_b_gate, m_g_gla_norm, m_w_out, m_g_norm_ffn2, m_w_ffn2_in, m_w_ffn2_out, m_g_norm_final, m_w_ada_final, m_b_ada_final, v_w_ada, v_b_ada, v_g_norm_ffn1, v_w_ffn1_in, v_w_ffn1_out, v_g_norm_mix, v_w_in, v_w_dw, v_b_dw, v_g_conv_ln, v_b_conv_ln, v_w_gate_up, v_b_gate, v_g_gla_norm, v_w_out, v_g_norm_ffn2, v_w_ffn2_in, v_w_ffn2_out, v_g_norm_final, v_w_ada_final, v_b_ada_final):
    me = 4 * lax.axis_index("x") + 2 * lax.axis_index("y") + lax.axis_index("c")
    L = DEPTH
    n_ada = N_MOD * D // N_DEV
    n_fin = 2 * D // N_DEV

    small = jnp.concatenate([c.reshape(-1), w_dw.reshape(-1), w_gate_up.reshape(-1)])
    n_small = small.shape[0]
    small = jnp.pad(small, (0, 8 * D - n_small)).reshape(8, D)
    big = dict(wi1=w_ffn1_in, wo1=w_ffn1_out, win=w_in, wout=w_out, wi2=w_ffn2_in, wo2=w_ffn2_out)
    sh = {n: [a[l].astype(bf16) for l in range(L)] for n, a in big.items()}
    small_a, wi1_first, wo1_first = all_gather([small, sh["wi1"][0], sh["wo1"][0]], "gather_first")
    small_a = small_a.reshape(N_DEV, 8 * D)
    c_all = small_a[:, :D]
    o1 = D + L * CW * (DC // N_DEV)
    wdw_full = _col_shards_to_full(small_a[:, D:o1].reshape(N_DEV, L * CW, DC // N_DEV)).reshape(L, CW, DC)
    wgu_full = _col_shards_to_full(small_a[:, o1:o1 + L * GR * (DQK // N_DEV)].reshape(N_DEV, L * GR, DQK // N_DEV)).reshape(L, GR, DQK)

    b_ada_mine = lax.dynamic_slice(b_ada, (0, me * n_ada), (L, n_ada))
    b_fin_mine = lax.dynamic_slice(b_ada_final, (me * n_fin,), (n_fin,))
    parts = [ada_fwd(c_all, w_ada[l], b_ada_mine[l:l + 1], f"ada_fwd_{l}") for l in range(L)]
    parts.append(ada_fwd(c_all, w_ada_final, b_fin_mine.reshape(1, n_fin), "ada_fwd_final"))
    modsrc = jnp.concatenate(parts, axis=1)
    n_row = modsrc.shape[1]
    modsrc = jnp.pad(modsrc, ((0, 0), (0, 24 * 128 - n_row))).reshape(N_DEV, 24, 128)
    (modrecv,) = all_to_all([modsrc], "exchange_mod")
    modrecv = modrecv.reshape(N_DEV, 24 * 128)
    mods = []
    for l in range(L):
        mvec = modrecv[:, l * n_ada:(l + 1) * n_ada].reshape(N_MOD, D)
        mods.append(_pad_rows(mvec, 16))
    fmod = _pad_rows(modrecv[:, L * n_ada:L * n_ada + n_fin].reshape(2, D), 8)

    ws = {"gf": g_norm_final.reshape(1, D)}
    for l in range(L):
        ws[f"L{l}"] = dict(
            g1=g_norm_ffn1[l].reshape(1, D), g2=g_norm_mix[l].reshape(1, D), g3=g_norm_ffn2[l].reshape(1, D),
            wgu=_pad_rows(wgu_full[l], 128).astype(bf16),
            bgate=b_gate[l].reshape(1, DQK),
            wdw=_pad_rows(wdw_full[l], 32),
            cpar=_pad_rows(jnp.stack([b_dw[l], g_conv_ln[l], b_conv_ln[l]]), 8),
            gn_s=jnp.repeat(g_gla_norm[l], CH, axis=0),
        )

    loss_lanes, grad_x, gr, dmods, dfmod, recv, last_pieces = train_pass(
        x[0], loss_target[0], mods, fmod, sh, ws, wi1_first, wo1_first)

    def adam_big(rv, w, m, v, name, comm=None):
        R = w.shape[1]
        tr = 256 if R % 256 == 0 else R // 2
        return adam_parts(rv, w, m, v, tr, name, comm=comm)

    res = {}
    res["w_ffn2_in"], (recv["wi1"][0],) = adam_big(recv["wi2"], w_ffn2_in, m_w_ffn2_in, v_w_ffn2_in, "adam_ffn2_in",
                                                   comm=("exchange", [last_pieces]))
    res["w_ffn2_out"], _ = adam_big(recv["wo2"], w_ffn2_out, m_w_ffn2_out, v_w_ffn2_out, "adam_ffn2_out")
    res["w_in"], _ = adam_big(recv["win"], w_in, m_w_in, v_w_in, "adam_w_in")
    res["w_out"], _ = adam_big(recv["wout"], w_out, m_w_out, v_w_out, "adam_w_out")
    res["w_ffn1_out"], _ = adam_big(recv["wo1"], w_ffn1_out, m_w_ffn1_out, v_w_ffn1_out, "adam_ffn1_out")
    res["w_ffn1_in"], _ = adam_big(recv["wi1"], w_ffn1_in, m_w_ffn1_in, v_w_ffn1_in, "adam_ffn1_in")

    flat = lambda name: jnp.stack([gr[f"L{l}"][name] for l in range(L)]).reshape(-1)
    sections = [
        ("b_ada", jnp.stack(dmods).reshape(-1)), ("b_ada_final", dfmod.reshape(-1)),
        ("g_norm_ffn1", flat("g1")), ("g_norm_mix", flat("g2")), ("g_norm_ffn2", flat("g3")), ("g_norm_final", gr["gf"]),
        ("b_dw", flat("bdw")), ("g_conv_ln", flat("gln")), ("b_conv_ln", flat("bln")), ("b_gate", flat("bgate")),
        ("g_gla_norm", flat("gn")),
    ]
    n_rep = sum(s[1].shape[0] for s in sections)
    rep_rows = -(-n_rep // D)
    extra = [("loss", loss_lanes), ("w_dw", flat("wdw")), ("w_gate_up", flat("wgu"))]
    pack = jnp.concatenate([s[1] for s in sections] + [jnp.zeros((rep_rows * D - n_rep,), f32)] + [s[1] for s in extra])
    n_pack = pack.shape[0]
    pack_rows = -(-n_pack // (8 * D)) * 8
    pack = jnp.pad(pack, (0, pack_rows * D - n_pack)).reshape(pack_rows, D)
    (pack_all,) = all_gather([pack], "gather_small_grads")
    tot = sum8(pack_all, "sum_small_grads")
    tot_flat = tot.reshape(-1)
    loss = jnp.sum(tot_flat[rep_rows * D:rep_rows * D + D])
    o_dw = rep_rows * D + D
    g_wdw_full = tot_flat[o_dw:o_dw + L * CW * DC].reshape(L, CW, DC)
    o_gu = o_dw + L * CW * DC
    g_wgu_full = tot_flat[o_gu:o_gu + L * GR * DQK].reshape(L, GR, DQK)

    small_params = dict(b_ada=(b_ada, m_b_ada, v_b_ada), b_ada_final=(b_ada_final, m_b_ada_final, v_b_ada_final),
                        g_norm_ffn1=(g_norm_ffn1, m_g_norm_ffn1, v_g_norm_ffn1), g_norm_mix=(g_norm_mix, m_g_norm_mix, v_g_norm_mix),
                        g_norm_ffn2=(g_norm_ffn2, m_g_norm_ffn2, v_g_norm_ffn2), g_norm_final=(g_norm_final, m_g_norm_final, v_g_norm_final),
                        b_dw=(b_dw, m_b_dw, v_b_dw), g_conv_ln=(g_conv_ln, m_g_conv_ln, v_g_conv_ln),
                        b_conv_ln=(b_conv_ln, m_b_conv_ln, v_b_conv_ln), b_gate=(b_gate, m_b_gate, v_b_gate),
                        g_gla_norm=(g_gla_norm, m_g_gla_norm, v_g_gla_norm))

    def rep_pack(idx):
        p = jnp.concatenate([small_params[s[0]][idx].reshape(-1) for s in sections])
        return jnp.pad(p, (0, rep_rows * D - n_rep)).reshape(rep_rows, D)

    g_rep = tot[:rep_rows]
    d_rep, m_rep, v_rep = adam_plain(g_rep, rep_pack(0), rep_pack(1), rep_pack(2), rep_rows, "adam_small")
    off = 0
    for sname, sval in sections:
        shp = small_params[sname][0].shape
        nel = sval.shape[0]
        res[sname] = [a.reshape(-1)[off:off + nel].reshape(shp) for a in (g_rep, d_rep, m_rep, v_rep)]
        off += nel

    def adam_cols(g_full, w, m, v, name):
        shp = w.shape
        g_mine = lax.dynamic_slice(g_full, (0, 0, me * shp[2]), shp)
        R, C = shp[0] * shp[1], shp[2]
        outs = adam_plain(g_mine.reshape(R, C), w.reshape(R, C), m.reshape(R, C), v.reshape(R, C), R, name)
        return [g_mine] + [o.reshape(shp) for o in outs]

    res["w_dw"] = adam_cols(g_wdw_full, w_dw, m_w_dw, v_w_dw, "adam_w_dw")
    res["w_gate_up"] = adam_cols(g_wgu_full, w_gate_up, m_w_gate_up, v_w_gate_up, "adam_w_gate_up")

    c_all_t = c_all.T
    dmod_all = pack_all.reshape(N_DEV, -1)[:, :L * N_MOD * D].reshape(N_DEV, L, N_MOD * D)
    dfm_all = pack_all.reshape(N_DEV, -1)[:, L * N_MOD * D:L * N_MOD * D + 2 * D]
    dm_mine = lax.dynamic_slice(dmod_all, (0, 0, me * n_ada), (N_DEV, L, n_ada))
    dfm_mine = lax.dynamic_slice(dfm_all, (0, me * n_fin), (N_DEV, n_fin))
    g_w_ada = jnp.stack([ada_wgrad(c_all_t, dm_mine[:, l], f"ada_wgrad_{l}") for l in range(L)])
    g_w_fin = ada_wgrad(c_all_t, dfm_mine, "ada_wgrad_final")
    outs = adam_plain(g_w_ada.reshape(L * D, n_ada), w_ada.reshape(L * D, n_ada), m_w_ada.reshape(L * D, n_ada),
                      v_w_ada.reshape(L * D, n_ada), 256, "adam_w_ada")
    res["w_ada"] = [g_w_ada] + [o.reshape(w_ada.shape) for o in outs]
    res["w_ada_final"] = [g_w_fin] + list(adam_plain(g_w_fin, w_ada_final, m_w_ada_final, v_w_ada_final, 256, "adam_w_ada_final"))

    order = ["w_ada", "b_ada", "g_norm_ffn1", "w_ffn1_in", "w_ffn1_out", "g_norm_mix", "w_in", "w_dw", "b_dw", "g_conv_ln",
             "b_conv_ln", "w_gate_up", "b_gate", "g_gla_norm", "w_out", "g_norm_ffn2", "w_ffn2_in", "w_ffn2_out",
             "g_norm_final", "w_ada_final", "b_ada_final"]
    out = [loss, grad_x[None]]
    for k in range(4):
        out += [res[name][k] for name in order]
    return tuple(out)
```

```python
import functools

import jax
import jax.numpy as jnp
from jax import lax
from jax.experimental import pallas as pl
from jax.experimental.pallas import tpu as pltpu

f32 = jnp.float32
bf16 = jnp.bfloat16

N_DEV = 8
DEPTH = 2
D = 1024
F = 2816
DC = 512
NH = 4
DK = 64
DV = 128
DQK = NH * DK
DG = NH * DV
CH = 64
CW = 31
GR = 16
TAU = 16.0
N_MOD = 9
DIN = 2 * DC + 2 * DQK + 2 * DG + GR
DINP = 2688
EPS = 1e-6
HALO = 32
SUBLANES = 8
NFS = 4
FS = F // NFS

ADAM_LR = 0.001
ADAM_B1 = 0.9
ADAM_B2 = 0.999
ADAM_EPS = 1e-08
ADAM_WD = 0.01
ADAM_STEP = 10

V7X_VMEM_LIMIT = 56 * 1024 * 1024
MESH = pl.DeviceIdType.MESH
HIGHEST = lax.Precision.HIGHEST

NT = (((1,), (1,)), ((), ()))
TN = (((0,), (0,)), ((), ()))


def _cp(n_axes):
    return pltpu.CompilerParams(dimension_semantics=("arbitrary",) * n_axes, vmem_limit_bytes=V7X_VMEM_LIMIT)


def _full(shape):
    nd = len(shape)
    return pl.BlockSpec(shape, lambda *_: (0,) * nd)


def _dot(a, b):
    return jnp.dot(a, b, preferred_element_type=f32)


def _dg(a, b, dims):
    return lax.dot_general(a, b, dims, preferred_element_type=f32)


def _sigmoid(x):
    return jax.nn.sigmoid(x)


def _rowsum(x):
    return jnp.sum(x, axis=0, keepdims=True)


def _rms_parts(xv):
    rstd = lax.rsqrt(jnp.mean(xv * xv, axis=-1, keepdims=True) + EPS)
    return xv * rstd, rstd


def _rms_bwd(dxh, xh, rstd):
    return rstd * (dxh - xh * jnp.mean(dxh * xh, axis=-1, keepdims=True))


def ffn_fwd(x, mod, g, wi, wo, rows, tm, name, comm=None):
    S = x.shape[0]
    nj = NFS
    tf = FS
    r_shift, r_scale, r_gate = rows

    def body(x_ref, mod_ref, g_ref, wg_ref, wu_ref, wo_ref, xo_ref, h_ref, zg_ref, zu_ref, f_ref, acc_s):
        j = pl.program_id(1)

        @pl.when(j == 0)
        def _():
            xh, _ = _rms_parts(x_ref[...])
            hv = xh * g_ref[...] * (1.0 + mod_ref[r_scale:r_scale + 1, :]) + mod_ref[r_shift:r_shift + 1, :]
            h_ref[...] = hv.astype(bf16)
            acc_s[...] = jnp.zeros_like(acc_s)

        h = h_ref[...]
        zg = _dot(h, wg_ref[...])
        zu = _dot(h, wu_ref[...])
        zg_ref[...] = zg.astype(bf16)
        zu_ref[...] = zu.astype(bf16)
        a = zg * _sigmoid(zg) * zu
        acc_s[...] += _dot(a.astype(bf16), wo_ref[...])

        @pl.when(j == nj - 1)
        def _():
            fv = acc_s[...]
            f_ref[...] = fv.astype(bf16)
            xo_ref[...] = x_ref[...] + 0.5 * mod_ref[r_gate:r_gate + 1, :] * fv

    return _pcall(
        body, (x, mod, g, wi, wi, wo), name=name, comm=comm,
        grid=(S // tm, nj),
        in_specs=[
            pl.BlockSpec((tm, D), lambda i, j: (i, 0)),
            _full(mod.shape), _full(g.shape),
            pl.BlockSpec((None, D, tf), lambda i, j: (j, 0, 0)),
            pl.BlockSpec((None, D, tf), lambda i, j: (j + NFS, 0, 0)),
            pl.BlockSpec((tf, D), lambda i, j: (j, 0)),
        ],
        out_specs=[
            pl.BlockSpec((tm, D), lambda i, j: (i, 0)),
            pl.BlockSpec((tm, D), lambda i, j: (i, 0)),
            pl.BlockSpec((None, tm, tf), lambda i, j: (j, i, 0)),
            pl.BlockSpec((None, tm, tf), lambda i, j: (j, i, 0)),
            pl.BlockSpec((tm, D), lambda i, j: (i, 0)),
        ],
        out_shape=[
            jax.ShapeDtypeStruct((S, D), f32),
            jax.ShapeDtypeStruct((S, D), bf16),
            jax.ShapeDtypeStruct((NFS, S, tf), bf16),
            jax.ShapeDtypeStruct((NFS, S, tf), bf16),
            jax.ShapeDtypeStruct((S, D), bf16),
        ],
        scratch_shapes=[pltpu.VMEM((tm, D), f32)],
    )


def ffn_bwd_hidden(dy, zg, zu, mod, wo_t, r_gate, tm, name, comm=None):
    S = dy.shape[0]
    tf = FS

    def body(dy_ref, zg_ref, zu_ref, mod_ref, wo_ref, df_ref, a_ref, dzg_ref, dzu_ref):
        @pl.when(pl.program_id(1) == 0)
        def _():
            df_ref[...] = (0.5 * mod_ref[r_gate:r_gate + 1, :] * dy_ref[...]).astype(bf16)

        for r0 in range(0, tm, tm // 4):
            rs = slice(r0, r0 + tm // 4)
            zgv = zg_ref[rs, :].astype(f32)
            zuv = zu_ref[rs, :].astype(f32)
            s = _sigmoid(zgv)
            sil = zgv * s
            a_ref[rs, :] = (sil * zuv).astype(bf16)
            da = _dot(df_ref[rs, :], wo_ref[...])
            dzu_ref[rs, :] = (da * sil).astype(bf16)
            dzg_ref[rs, :] = (da * zuv * (s * (1.0 + zgv * (1.0 - s)))).astype(bf16)

    row = lambda i, j: (i, 0)
    tile = lambda i, j: (j, i, 0)
    shard = jax.ShapeDtypeStruct((NFS, S, tf), bf16)
    return _pcall(
        body, (dy, zg, zu, mod, wo_t), name=name, comm=comm,
        grid=(S // tm, NFS),
        in_specs=[pl.BlockSpec((tm, D), row), pl.BlockSpec((None, tm, tf), tile), pl.BlockSpec((None, tm, tf), tile),
                  _full(mod.shape), pl.BlockSpec((None, D, tf), lambda i, j: (j, 0, 0))],
        out_specs=[pl.BlockSpec((tm, D), row),
                   pl.BlockSpec((None, tm, tf), tile), pl.BlockSpec((None, tm, tf), tile), pl.BlockSpec((None, tm, tf), tile)],
        out_shape=[jax.ShapeDtypeStruct((S, D), bf16), shard, shard, shard],
    )


def ffn_bwd_input(x, dy, dzg, dzu, fo, mod, g, wi_t, rows, tm, name, comm=None):
    S = x.shape[0]
    nj = NFS
    tf = FS
    r_shift, r_scale, r_gate = rows

    def body(x_ref, dy_ref, dzg_ref, dzu_ref, f_ref, mod_ref, g_ref, wg_ref, wu_ref, dx_ref, red_ref, acc_s):
        i = pl.program_id(0)
        j = pl.program_id(1)

        @pl.when((i == 0) & (j == 0))
        def _():
            red_ref[...] = jnp.zeros_like(red_ref)

        part = _dot(dzg_ref[...], wg_ref[...]) + _dot(dzu_ref[...], wu_ref[...])

        @pl.when(j == 0)
        def _():
            acc_s[...] = part

        @pl.when(j > 0)
        def _():
            acc_s[...] += part

        @pl.when(j == nj - 1)
        def _():
            dh = acc_s[...]
            dyv = dy_ref[...]
            xh, rstd = _rms_parts(x_ref[...])
            gv = g_ref[...]
            n = xh * gv
            dn = dh * (1.0 + mod_ref[r_scale:r_scale + 1, :])
            red_ref[0:1, :] += _rowsum(dh)
            red_ref[1:2, :] += _rowsum(dh * n)
            red_ref[2:3, :] += _rowsum(0.5 * f_ref[...].astype(f32) * dyv)
            red_ref[3:4, :] += _rowsum(dn * xh)
            dx_ref[...] = dyv + _rms_bwd(dn * gv, xh, rstd)

    row = lambda i, j: (i, 0)
    tile = lambda i, j: (j, i, 0)
    return _pcall(
        body, (x, dy, dzg, dzu, fo, mod, g, wi_t, wi_t), name=name, comm=comm,
        grid=(S // tm, nj),
        in_specs=[
            pl.BlockSpec((tm, D), row), pl.BlockSpec((tm, D), row),
            pl.BlockSpec((None, tm, tf), tile), pl.BlockSpec((None, tm, tf), tile),
            pl.BlockSpec((tm, D), row),
            _full(mod.shape), _full(g.shape),
            pl.BlockSpec((None, tf, D), lambda i, j: (j, 0, 0)),
            pl.BlockSpec((None, tf, D), lambda i, j: (j + NFS, 0, 0)),
        ],
        out_specs=[pl.BlockSpec((tm, D), row), _full((8, D))],
        out_shape=[jax.ShapeDtypeStruct((S, D), f32), jax.ShapeDtypeStruct((8, D), f32)],
        scratch_shapes=[pltpu.VMEM((tm, D), f32)],
    )


def matmul_tn(a, b, M, N, bm, bn, bk, name, a_col_block=0, out_dtype=f32):
    S = b.shape[0]
    nk = S // bk

    def body(a_ref, b_ref, o_ref, acc_s):
        k = pl.program_id(2)

        @pl.when(k == 0)
        def _():
            acc_s[...] = jnp.zeros_like(acc_s)

        acc_s[...] += _dg(a_ref[...].astype(bf16), b_ref[...].astype(bf16), TN)

        @pl.when(k == nk - 1)
        def _():
            o_ref[...] = acc_s[...].astype(out_dtype)

    return pl.pallas_call(
        body, name=name,
        grid=(M // bm, N // bn, nk),
        in_specs=[
            pl.BlockSpec((bk, bm), lambda i, j, k: (k, i + a_col_block)),
            pl.BlockSpec((bk, bn), lambda i, j, k: (k, j)),
        ],
        out_specs=pl.BlockSpec((bm, bn), lambda i, j, k: (i, j)),
        out_shape=jax.ShapeDtypeStruct((M, N), out_dtype),
        scratch_shapes=[pltpu.VMEM((bm, bn), f32)],
        compiler_params=_cp(3),
    )(a, b)


def dwi_pieces(h, dzg, dzu, bk, name, comm=None):
    S = h.shape[0]
    nk = S // bk

    def body(h_ref, g_ref, u_ref, o_ref, acc_s):
        j = pl.program_id(0)
        k = pl.program_id(1)

        @pl.when(k == 0)
        def _():
            acc_s[...] = jnp.zeros_like(acc_s)

        @pl.when(j < NFS)
        def _():
            acc_s[...] += _dg(h_ref[...], g_ref[...], TN)

        @pl.when(j >= NFS)
        def _():
            acc_s[...] += _dg(h_ref[...], u_ref[...], TN)

        @pl.when(k == nk - 1)
        def _():
            o_ref[...] = acc_s[...].astype(bf16)

    (out,), comm_outs = _pcall(
        body, (h, dzg, dzu), name=name, comm=comm,
        grid=(2 * NFS, nk),
        in_specs=[
            pl.BlockSpec((bk, D), lambda j, k: (k, 0)),
            pl.BlockSpec((None, bk, FS), lambda j, k: (jnp.minimum(j, NFS - 1), jnp.where(j < NFS, k, nk - 1), 0)),
            pl.BlockSpec((None, bk, FS), lambda j, k: (jnp.maximum(j - NFS, 0), jnp.where(j >= NFS, k, 0), 0)),
        ],
        out_specs=[pl.BlockSpec((None, D, FS), lambda j, k: (j, 0, 0))],
        out_shape=[jax.ShapeDtypeStruct((2 * NFS, D, FS), bf16)],
        scratch_shapes=[pltpu.VMEM((D, FS), f32)],
    )
    return out, comm_outs


def dwo_pieces(a, df, bk, name, comm=None):
    S = df.shape[0]
    nk = S // bk

    def body(a_ref, d_ref, o_ref, acc_s):
        k = pl.program_id(1)

        @pl.when(k == 0)
        def _():
            acc_s[...] = jnp.zeros_like(acc_s)

        acc_s[...] += _dg(a_ref[...], d_ref[...], TN)

        @pl.when(k == nk - 1)
        def _():
            o_ref[...] = acc_s[...].astype(bf16)

    (out,), comm_outs = _pcall(
        body, (a, df), name=name, comm=comm,
        grid=(NFS, nk),
        in_specs=[pl.BlockSpec((None, bk, FS), lambda j, k: (j, k, 0)), pl.BlockSpec((bk, D), lambda j, k: (k, 0))],
        out_specs=[pl.BlockSpec((FS, D), lambda j, k: (j, 0))],
        out_shape=[jax.ShapeDtypeStruct((F, D), bf16)],
        scratch_shapes=[pltpu.VMEM((FS, D), f32)],
    )
    return out, comm_outs


def mixin_fwd(x1, mod, g, win, wgu, bgate, tm, name):
    S = x1.shape[0]

    def body(x_ref, mod_ref, g_ref, win_ref, wgu_ref, bg_ref, z_ref, la_ref):
        xh, _ = _rms_parts(x_ref[...])
        hv = xh * g_ref[...] * (1.0 + mod_ref[4:5, :]) + mod_ref[3:4, :]
        z = _dot(hv.astype(bf16), win_ref[...])
        z_ref[...] = z
        glr = z[:, DINP - 128:]
        pre = _dot(glr.astype(bf16), wgu_ref[...]) + bg_ref[...]
        la_ref[...] = (jnp.minimum(pre, 0.0) - jnp.log(1.0 + jnp.exp(-jnp.abs(pre)))) * (1.0 / TAU)

    return pl.pallas_call(
        body, name=name,
        grid=(S // tm,),
        in_specs=[pl.BlockSpec((tm, D), lambda i: (i, 0)), _full(mod.shape), _full(g.shape),
                  _full(win.shape), _full(wgu.shape), _full(bgate.shape)],
        out_specs=[pl.BlockSpec((tm, DINP), lambda i: (i, 0)), pl.BlockSpec((tm, DQK), lambda i: (i, 0))],
        out_shape=[jax.ShapeDtypeStruct((S, DINP), f32), jax.ShapeDtypeStruct((S, DQK), f32)],
        compiler_params=_cp(1),
    )(x1, mod, g, win, wgu, bgate)


def mixin_bwd(x1, dres, dzab, dq, dk, dv, dr, dpre, mod, g, win, wgu, tm, name, comm=None):
    S = x1.shape[0]

    def body(x_ref, dres_ref, dzab_ref, dq_ref, dk_ref, dv_ref, dr_ref, dpre_ref, mod_ref, g_ref, win_ref, wgu_ref,
             dx_ref, h_ref, dz_ref, red_ref):
        @pl.when(pl.program_id(0) == 0)
        def _():
            red_ref[...] = jnp.zeros_like(red_ref)

        dglr = _dg(dpre_ref[...].astype(bf16), wgu_ref[...], NT)
        dz = jnp.concatenate([dzab_ref[...], dq_ref[...], dk_ref[...], dv_ref[...], dr_ref[...], dglr], axis=1).astype(bf16)
        dz_ref[...] = dz
        dh = _dg(dz, win_ref[...], NT)
        xh, rstd = _rms_parts(x_ref[...])
        gv = g_ref[...]
        n = xh * gv
        sc = 1.0 + mod_ref[4:5, :]
        h_ref[...] = (n * sc + mod_ref[3:4, :]).astype(bf16)
        dn = dh * sc
        red_ref[0:1, :] += _rowsum(dh)
        red_ref[1:2, :] += _rowsum(dh * n)
        red_ref[2:3, :] += _rowsum(dn * xh)
        dx_ref[...] = dres_ref[...] + _rms_bwd(dn * gv, xh, rstd)

    row = lambda i: (i, 0)
    return _pcall(
        body, (x1, dres, dzab, dq, dk, dv, dr, dpre, mod, g, win, wgu), name=name, comm=comm,
        grid=(S // tm,),
        in_specs=[pl.BlockSpec((tm, D), row), pl.BlockSpec((tm, D), row),
                  pl.BlockSpec((tm, 2 * DC), row), pl.BlockSpec((tm, DQK), row), pl.BlockSpec((tm, DQK), row),
                  pl.BlockSpec((tm, DG), row), pl.BlockSpec((tm, DG), row), pl.BlockSpec((tm, DQK), row),
                  _full(mod.shape), _full(g.shape), _full(win.shape), _full(wgu.shape)],
        out_specs=[pl.BlockSpec((tm, D), row), pl.BlockSpec((tm, D), row), pl.BlockSpec((tm, DINP), row), _full((8, D))],
        out_shape=[jax.ShapeDtypeStruct((S, D), f32), jax.ShapeDtypeStruct((S, D), bf16),
                   jax.ShapeDtypeStruct((S, DINP), bf16), jax.ShapeDtypeStruct((8, D), f32)],
    )


def _glu(zab):
    return zab[:, :DC] * _sigmoid(zab[:, DC:])


def _shift_copies(src_s, dst_s, tc):
    n = tc + HALO - SUBLANES
    for b in range(1, SUBLANES):
        dst_s[b, 0:n, :] = src_s[b:b + n, :]


def _shifted(src_s, dst_s, o, tc):
    b = o % SUBLANES
    a = o - b
    return src_s[a:a + tc, :] if b == 0 else dst_s[b, a:a + tc, :]


def conv_fwd(z, wdw, cpar, tc, name):
    S = z.shape[0]
    nb = tc // HALO

    def body(zc_ref, zp_ref, w_ref, cp_ref, y_ref, yc_ref, u_s, us_s):
        i = pl.program_id(0)
        up = _glu(zp_ref[...])
        u_s[0:HALO, :] = jnp.where(i > 0, up, 0.0)
        u_s[HALO:HALO + tc, :] = _glu(zc_ref[...])
        _shift_copies(u_s, us_s, tc)
        acc = jnp.zeros((tc, DC), f32)
        for w in range(CW):
            acc = acc + _shifted(u_s, us_s, HALO - (CW - 1) + w, tc) * w_ref[w:w + 1, :]
        y = acc + cp_ref[0:1, :]
        y_ref[...] = y
        yc = y - jnp.mean(y, axis=-1, keepdims=True)
        yl = yc * lax.rsqrt(jnp.mean(yc * yc, axis=-1, keepdims=True) + EPS) * cp_ref[1:2, :] + cp_ref[2:3, :]
        yc_ref[...] = (yl * _sigmoid(yl)).astype(bf16)

    return pl.pallas_call(
        body, name=name,
        grid=(S // tc,),
        in_specs=[pl.BlockSpec((tc, 2 * DC), lambda i: (i, 0)),
                  pl.BlockSpec((HALO, 2 * DC), lambda i: (jnp.maximum(i * nb - 1, 0), 0)),
                  _full(wdw.shape), _full(cpar.shape)],
        out_specs=[pl.BlockSpec((tc, DC), lambda i: (i, 0)), pl.BlockSpec((tc, DC), lambda i: (i, 0))],
        out_shape=[jax.ShapeDtypeStruct((S, DC), f32), jax.ShapeDtypeStruct((S, DC), bf16)],
        scratch_shapes=[pltpu.VMEM((HALO + tc, DC), f32), pltpu.VMEM((SUBLANES, HALO + tc, DC), f32)],
        compiler_params=_cp(1),
    )(z, z, wdw, cpar)


def conv_bwd(z, y, dyc, wdw, cpar, tc, name, comm=None):
    S = z.shape[0]
    nb = tc // HALO
    nt = S // tc
    last_halo = S // HALO - 1

    def body(zc_ref, zp_ref, y_ref, yn_ref, d_ref, dn_ref, w_ref, cp_ref, dz_ref, red_ref, u_s, dy_s, us_s, dys_s):
        i = pl.program_id(0)

        @pl.when(i == 0)
        def _():
            red_ref[...] = jnp.zeros_like(red_ref)

        gl = cp_ref[1:2, :]
        bl = cp_ref[2:3, :]

        def ln_bwd(yv, dv):
            yc = yv - jnp.mean(yv, axis=-1, keepdims=True)
            rstd = lax.rsqrt(jnp.mean(yc * yc, axis=-1, keepdims=True) + EPS)
            yh = yc * rstd
            yl = yh * gl + bl
            s = _sigmoid(yl)
            dyl = dv * (s * (1.0 + yl * (1.0 - s)))
            dyh = dyl * gl
            dyv = rstd * (dyh - jnp.mean(dyh, axis=-1, keepdims=True) - yh * jnp.mean(dyh * yh, axis=-1, keepdims=True))
            return dyv, dyl, yh

        dy_c, dyl_c, yh_c = ln_bwd(y_ref[...], d_ref[...])
        dy_n, _, _ = ln_bwd(yn_ref[...], dn_ref[...])
        dy_s[0:tc, :] = dy_c
        dy_s[tc:tc + HALO, :] = jnp.where(i < nt - 1, dy_n, 0.0)
        zc = zc_ref[...]
        av = zc[:, :DC]
        sb = _sigmoid(zc[:, DC:])
        u_s[0:HALO, :] = jnp.where(i > 0, _glu(zp_ref[...]), 0.0)
        u_s[HALO:HALO + tc, :] = av * sb
        _shift_copies(u_s, us_s, tc)
        _shift_copies(dy_s, dys_s, tc)
        du = jnp.zeros((tc, DC), f32)
        for w in range(CW):
            red_ref[w:w + 1, :] += _rowsum(_shifted(u_s, us_s, HALO - (CW - 1) + w, tc) * dy_c)
            du = du + _shifted(dy_s, dys_s, CW - 1 - w, tc) * w_ref[w:w + 1, :]
        red_ref[32:33, :] += _rowsum(dy_c)
        red_ref[33:34, :] += _rowsum(dyl_c * yh_c)
        red_ref[34:35, :] += _rowsum(dyl_c)
        dz_ref[...] = jnp.concatenate([du * sb, du * av * sb * (1.0 - sb)], axis=1)

    cur = lambda i: (i, 0)
    nxt = lambda i: (jnp.minimum((i + 1) * nb, last_halo), 0)
    return _pcall(
        body, (z, z, y, y, dyc, dyc, wdw, cpar), name=name, comm=comm,
        grid=(nt,),
        in_specs=[pl.BlockSpec((tc, 2 * DC), cur),
                  pl.BlockSpec((HALO, 2 * DC), lambda i: (jnp.maximum(i * nb - 1, 0), 0)),
                  pl.BlockSpec((tc, DC), cur), pl.BlockSpec((HALO, DC), nxt),
                  pl.BlockSpec((tc, DC), cur), pl.BlockSpec((HALO, DC), nxt),
                  _full(wdw.shape), _full(cpar.shape)],
        out_specs=[pl.BlockSpec((tc, 2 * DC), cur), _full((40, DC))],
        out_shape=[jax.ShapeDtypeStruct((S, 2 * DC), f32), jax.ShapeDtypeStruct((40, DC), f32)],
        scratch_shapes=[pltpu.VMEM((HALO + tc, DC), f32), pltpu.VMEM((tc + HALO, DC), f32),
                        pltpu.VMEM((SUBLANES, HALO + tc, DC), f32), pltpu.VMEM((SUBLANES, HALO + tc, DC), f32)],
    )


def _gla_consts():
    r = lax.broadcasted_iota(jnp.int32, (CH, CH), 0)
    c = lax.broadcasted_iota(jnp.int32, (CH, CH), 1)
    tril = r >= c
    lane = lax.broadcasted_iota(jnp.int32, (CH, DQK), 1)
    masks = [(lane >= h * DK) & (lane < (h + 1) * DK) for h in range(NH)]
    r4 = lax.broadcasted_iota(jnp.int32, (DQK, DQK), 0)
    c4 = lax.broadcasted_iota(jnp.int32, (DQK, DQK), 1)
    eye4 = (r4 == c4).astype(f32)
    rs = lax.broadcasted_iota(jnp.int32, (DQK, CH), 0) & (CH - 1)
    tril4 = rs >= lax.broadcasted_iota(jnp.int32, (DQK, CH), 1)
    return tril, tril4, masks, eye4


def _stack(xv, masks):
    return jnp.concatenate([jnp.where(m, xv, 0.0) for m in masks], axis=0)


def _unstack(rv, masks):
    out = jnp.where(masks[0], rv[0:CH, :], 0.0)
    for h in range(1, NH):
        out = out + jnp.where(masks[h], rv[h * CH:(h + 1) * CH, :], 0.0)
    return out


def _vstack(xv):
    return jnp.concatenate([xv[:, h * DV:(h + 1) * DV] for h in range(NH)], axis=0)


def _vunstack(xv):
    return jnp.concatenate([xv[h * CH:(h + 1) * CH, :] for h in range(NH)], axis=1)


def _gla_chunk_fwd(lac, qc, kc, vc, s_all, tril, masks, tril4):
    lmat = tril.astype(f32)
    bc = jnp.dot(lmat, lac, preferred_element_type=f32, precision=HIGHEST)
    bend = bc[CH - 1:CH, :]
    eb = jnp.exp(bc)
    enb = jnp.exp(-bc)
    ed = jnp.exp(bend - bc)
    qh = qc * (DK ** -0.5)
    qf = qh * eb
    qn = qh * enb
    kn = kc * enb
    kp = kc * eb
    kd = kc * ed
    qf_s = _stack(qf, masks).astype(bf16)
    qn_s = _stack(qn, masks).astype(bf16)
    kn_b = kn.astype(bf16)
    kp_b = kp.astype(bf16)
    attf = _dg(qf_s, kn_b, NT)
    attb = _dg(qn_s, kp_b, NT)
    a_s = jnp.where(tril4, attf, attb)
    a_b = a_s.astype(bf16)
    v_b = vc.astype(bf16)
    intra = jnp.concatenate(
        [_dot(a_b[h * CH:(h + 1) * CH, :], v_b[:, h * DV:(h + 1) * DV]) for h in range(NH)], axis=0)
    o_s = intra + _dot(qf_s, s_all.astype(bf16))
    return dict(bc=bc, bend=bend, eb=eb, enb=enb, ed=ed, qf=qf, qn=qn, kn=kn, kp=kp, kd=kd,
                qf_s=qf_s, qn_s=qn_s, kn_b=kn_b, kp_b=kp_b, a_b=a_b, v_b=v_b, o_s=o_s)


def _col_from_row(row, eye4):
    return jnp.sum(eye4 * row, axis=1, keepdims=True)


def _row_from_col(col, eye4):
    return jnp.sum(eye4 * col, axis=0, keepdims=True)


def gla_fwd(z, la, gn_s, tg, name):
    S = z.shape[0]
    nc = tg // CH

    def body(q_ref, k_ref, v_ref, r_ref, la_ref, gn_ref, yg_ref, sp_ref, st):
        @pl.when(pl.program_id(0) == 0)
        def _():
            st[...] = jnp.zeros_like(st)

        tril, tril4, masks, eye4 = _gla_consts()

        def chunk(c, carry):
            r0 = pl.multiple_of(c * CH, CH)
            s0 = pl.multiple_of(c * DQK, DQK)
            s_all = st[...]
            sp_ref[pl.ds(s0, DQK), :] = s_all
            vc = v_ref[pl.ds(r0, CH), :]
            t = _gla_chunk_fwd(la_ref[pl.ds(r0, CH), :], q_ref[pl.ds(r0, CH), :], k_ref[pl.ds(r0, CH), :], vc,
                               s_all, tril, masks, tril4)
            u_all = _dg(_stack(t["kd"], masks).astype(bf16), _vstack(vc).astype(bf16), TN)
            st[...] = _col_from_row(jnp.exp(t["bend"]), eye4) * s_all + u_all
            o_s = t["o_s"]
            on = o_s * lax.rsqrt(jnp.mean(o_s * o_s, axis=-1, keepdims=True) + EPS) * gn_ref[...]
            rc = r_ref[pl.ds(r0, CH), :]
            yg_ref[pl.ds(r0, CH), :] = (_vunstack(on) * (rc * _sigmoid(rc))).astype(bf16)
            return carry

        lax.fori_loop(0, nc, chunk, 0)

    return pl.pallas_call(
        body, name=name,
        grid=(S // tg,),
        in_specs=[pl.BlockSpec((tg, DQK), lambda i: (i, 4)), pl.BlockSpec((tg, DQK), lambda i: (i, 5)),
                  pl.BlockSpec((tg, DG), lambda i: (i, 3)), pl.BlockSpec((tg, DG), lambda i: (i, 4)),
                  pl.BlockSpec((tg, DQK), lambda i: (i, 0)), _full(gn_s.shape)],
        out_specs=[pl.BlockSpec((tg, DG), lambda i: (i, 0)), pl.BlockSpec((nc * DQK, DV), lambda i: (i, 0))],
        out_shape=[jax.ShapeDtypeStruct((S, DG), bf16), jax.ShapeDtypeStruct((S // CH * DQK, DV), f32)],
        scratch_shapes=[pltpu.VMEM((DQK, DV), f32)],
        compiler_params=_cp(1),
    )(z, z, z, z, la, gn_s)


def gla_bwd(z, la, sprev, dyg, gn_s, tg, name, comm=None):
    S = z.shape[0]
    nc = tg // CH
    nt = S // tg

    def body(q_ref, k_ref, v_ref, r_ref, la_ref, sp_ref, dy_ref, gn_ref,
             dq_ref, dk_ref, dv_ref, dr_ref, dpre_ref, redg_ref, redb_ref, gs):
        @pl.when(pl.program_id(0) == 0)
        def _():
            gs[...] = jnp.zeros_like(gs)
            redg_ref[...] = jnp.zeros_like(redg_ref)
            redb_ref[...] = jnp.zeros_like(redb_ref)

        tril, tril4, masks, eye4 = _gla_consts()
        umat = (lax.broadcasted_iota(jnp.int32, (CH, CH), 0) <= lax.broadcasted_iota(jnp.int32, (CH, CH), 1)).astype(f32)
        last_row = lax.broadcasted_iota(jnp.int32, (CH, DQK), 0) == CH - 1

        def chunk(tt, carry):
            c = nc - 1 - tt
            r0 = pl.multiple_of(c * CH, CH)
            s0 = pl.multiple_of(c * DQK, DQK)
            s_all = sp_ref[pl.ds(s0, DQK), :]
            lac = la_ref[pl.ds(r0, CH), :]
            vc = v_ref[pl.ds(r0, CH), :]
            rc = r_ref[pl.ds(r0, CH), :]
            t = _gla_chunk_fwd(lac, q_ref[pl.ds(r0, CH), :], k_ref[pl.ds(r0, CH), :], vc, s_all, tril, masks, tril4)
            g_all = gs[...]
            g_b = g_all.astype(bf16)
            s_b = s_all.astype(bf16)
            o_s = t["o_s"]
            rstd = lax.rsqrt(jnp.mean(o_s * o_s, axis=-1, keepdims=True) + EPS)
            oh = o_s * rstd
            gnv = gn_ref[...]
            sr = _sigmoid(rc)
            dyv = dy_ref[pl.ds(r0, CH), :]
            dr_ref[pl.ds(r0, CH), :] = dyv * _vunstack(oh * gnv) * (sr * (1.0 + rc * (1.0 - sr)))
            don = _vstack(dyv * (rc * sr))
            redg_ref[...] += don * oh
            doh = don * gnv
            do_s = rstd * (doh - oh * jnp.mean(doh * oh, axis=-1, keepdims=True))
            do_b = do_s.astype(bf16)
            v_b = t["v_b"]
            vst_b = _vstack(vc).astype(bf16)
            kd_s = _stack(t["kd"], masks).astype(bf16)
            da_s = jnp.concatenate(
                [_dg(do_b[h * CH:(h + 1) * CH, :], v_b[:, h * DV:(h + 1) * DV], NT) for h in range(NH)], axis=0)
            a_b = t["a_b"]
            dv_s = jnp.concatenate(
                [_dg(a_b[h * CH:(h + 1) * CH, :], do_b[h * CH:(h + 1) * CH, :], TN) for h in range(NH)], axis=0)
            dv_s = dv_s + _dot(kd_s, g_b)
            dv_ref[pl.ds(r0, CH), :] = _vunstack(dv_s)
            gend = jnp.exp(t["bend"])
            gcol = _col_from_row(gend, eye4)
            gs[...] = gcol * g_all + _dg(t["qf_s"], do_b, TN)
            dgcol = jnp.sum(g_all * s_all, axis=1, keepdims=True)
            dbend = _row_from_col(dgcol * gcol, eye4)
            dkd = _unstack(_dg(vst_b, g_b, NT), masks)
            daf = jnp.where(tril4, da_s, 0.0).astype(bf16)
            dab = jnp.where(tril4, 0.0, da_s).astype(bf16)
            dqf = _unstack(_dot(daf, t["kn_b"]) + _dg(do_b, s_b, NT), masks)
            dqn = _unstack(_dot(dab, t["kp_b"]), masks)
            dkn = _dg(daf, t["qf_s"], TN)
            dkp = _dg(dab, t["qn_s"], TN)
            dq_ref[pl.ds(r0, CH), :] = (dqf * t["eb"] + dqn * t["enb"]) * (DK ** -0.5)
            dk_ref[pl.ds(r0, CH), :] = dkn * t["enb"] + dkp * t["eb"] + dkd * t["ed"]
            dkd_kd = dkd * t["kd"]
            dbc = dqf * t["qf"] - dqn * t["qn"] - dkn * t["kn"] + dkp * t["kp"] - dkd_kd
            dbc = dbc + jnp.where(last_row, _rowsum(dkd_kd) + dbend, 0.0)
            dla = jnp.dot(umat, dbc, preferred_element_type=f32, precision=HIGHEST)
            dpre = dla * (1.0 / TAU) * (1.0 - jnp.exp(TAU * lac))
            dpre_ref[pl.ds(r0, CH), :] = dpre
            redb_ref[...] += dpre
            return carry

        lax.fori_loop(0, nc, chunk, 0)

    rev = lambda col: (lambda i: (nt - 1 - i, col))
    return _pcall(
        body, (z, z, z, z, la, sprev, dyg, gn_s), name=name, comm=comm,
        grid=(nt,),
        in_specs=[pl.BlockSpec((tg, DQK), rev(4)), pl.BlockSpec((tg, DQK), rev(5)),
                  pl.BlockSpec((tg, DG), rev(3)), pl.BlockSpec((tg, DG), rev(4)),
                  pl.BlockSpec((tg, DQK), rev(0)), pl.BlockSpec((nc * DQK, DV), rev(0)),
                  pl.BlockSpec((tg, DG), rev(0)), _full(gn_s.shape)],
        out_specs=[pl.BlockSpec((tg, DQK), rev(0)), pl.BlockSpec((tg, DQK), rev(0)),
                   pl.BlockSpec((tg, DG), rev(0)), pl.BlockSpec((tg, DG), rev(0)), pl.BlockSpec((tg, DQK), rev(0)),
                   _full((DQK, DV)), _full((CH, DQK))],
        out_shape=[jax.ShapeDtypeStruct((S, DQK), f32), jax.ShapeDtypeStruct((S, DQK), f32),
                   jax.ShapeDtypeStruct((S, DG), f32), jax.ShapeDtypeStruct((S, DG), f32), jax.ShapeDtypeStruct((S, DQK), f32),
                   jax.ShapeDtypeStruct((DQK, DV), f32), jax.ShapeDtypeStruct((CH, DQK), f32)],
        scratch_shapes=[pltpu.VMEM((DQK, DV), f32)],
    )


def mixout_fwd(x1, yc, yg, mod, wout, tm, name):
    S = x1.shape[0]

    def body(x_ref, yc_ref, yg_ref, mod_ref, w_ref, xo_ref):
        mixo = _dot(yc_ref[...], w_ref[0:DC, :]) + _dot(yg_ref[...], w_ref[DC:DC + DG, :])
        xo_ref[...] = x_ref[...] + mod_ref[5:6, :] * mixo

    row = lambda i: (i, 0)
    return pl.pallas_call(
        body, name=name,
        grid=(S // tm,),
        in_specs=[pl.BlockSpec((tm, D), row), pl.BlockSpec((tm, DC), row), pl.BlockSpec((tm, DG), row),
                  _full(mod.shape), _full(wout.shape)],
        out_specs=pl.BlockSpec((tm, D), row),
        out_shape=jax.ShapeDtypeStruct((S, D), f32),
        compiler_params=_cp(1),
    )(x1, yc, yg, mod, wout)


def mixout_bwd(dx2, yc, yg, mod, wout, tm, name):
    S = dx2.shape[0]

    def body(dx_ref, yc_ref, yg_ref, mod_ref, w_ref, dm_ref, dyc_ref, dyg_ref, red_ref):
        @pl.when(pl.program_id(0) == 0)
        def _():
            red_ref[...] = jnp.zeros_like(red_ref)

        dxv = dx_ref[...]
        mixo = _dot(yc_ref[...], w_ref[0:DC, :]) + _dot(yg_ref[...], w_ref[DC:DC + DG, :])
        red_ref[0:1, :] += _rowsum(dxv * mixo)
        dm = (mod_ref[5:6, :] * dxv).astype(bf16)
        dm_ref[...] = dm
        dycat = _dg(dm, w_ref[...], NT)
        dyc_ref[...] = dycat[:, :DC]
        dyg_ref[...] = dycat[:, DC:]

    row = lambda i: (i, 0)
    return pl.pallas_call(
        body, name=name,
        grid=(S // tm,),
        in_specs=[pl.BlockSpec((tm, D), row), pl.BlockSpec((tm, DC), row), pl.BlockSpec((tm, DG), row),
                  _full(mod.shape), _full(wout.shape)],
        out_specs=[pl.BlockSpec((tm, D), row), pl.BlockSpec((tm, DC), row), pl.BlockSpec((tm, DG), row), _full((8, D))],
        out_shape=[jax.ShapeDtypeStruct((S, D), bf16), jax.ShapeDtypeStruct((S, DC), f32),
                   jax.ShapeDtypeStruct((S, DG), f32), jax.ShapeDtypeStruct((8, D), f32)],
        compiler_params=_cp(1),
    )(dx2, yc, yg, mod, wout)


def final_fwd_bwd(x, tgt, fmod, g, tm, name):
    S = x.shape[0]

    def body(x_ref, t_ref, fm_ref, g_ref, dx_ref, red_ref):
        @pl.when(pl.program_id(0) == 0)
        def _():
            red_ref[...] = jnp.zeros_like(red_ref)

        xh, rstd = _rms_parts(x_ref[...])
        gv = g_ref[...]
        n = xh * gv
        sc = 1.0 + fm_ref[1:2, :]
        e = n * sc + fm_ref[0:1, :] - t_ref[...]
        red_ref[0:1, :] += _rowsum(e * e) * (0.5 / D)
        dy = e * (1.0 / D)
        dn = dy * sc
        red_ref[1:2, :] += _rowsum(dy)
        red_ref[2:3, :] += _rowsum(dy * n)
        red_ref[3:4, :] += _rowsum(dn * xh)
        dx_ref[...] = _rms_bwd(dn * gv, xh, rstd)

    row = lambda i: (i, 0)
    return pl.pallas_call(
        body, name=name,
        grid=(S // tm,),
        in_specs=[pl.BlockSpec((tm, D), row), pl.BlockSpec((tm, D), row), _full(fmod.shape), _full(g.shape)],
        out_specs=[pl.BlockSpec((tm, D), row), _full((8, D))],
        out_shape=[jax.ShapeDtypeStruct((S, D), f32), jax.ShapeDtypeStruct((8, D), f32)],
        compiler_params=_cp(1),
    )(x, tgt, fmod, g)


def ada_fwd(c_all, w, b, name):
    n = w.shape[1]

    def body(c_ref, w_ref, b_ref, o_ref):
        cv = c_ref[...]
        o_ref[...] = jnp.dot(cv * _sigmoid(cv), w_ref[...], preferred_element_type=f32, precision=HIGHEST) + b_ref[...]

    return pl.pallas_call(
        body, name=name,
        in_specs=[_full(c_all.shape), _full(w.shape), _full(b.shape)],
        out_specs=_full((N_DEV, n)),
        out_shape=jax.ShapeDtypeStruct((N_DEV, n), f32),
        grid=(1,),
        compiler_params=_cp(1),
    )(c_all, w, b)


def ada_wgrad(c_all_t, dm, name):
    n = dm.shape[1]

    def body(c_ref, d_ref, o_ref):
        cv = c_ref[...]
        o_ref[...] = jnp.dot(cv * _sigmoid(cv), d_ref[...], preferred_element_type=f32, precision=HIGHEST)

    return pl.pallas_call(
        body, name=name,
        in_specs=[_full(c_all_t.shape), _full(dm.shape)],
        out_specs=_full((D, n)),
        out_shape=jax.ShapeDtypeStruct((D, n), f32),
        grid=(1,),
        compiler_params=_cp(1),
    )(c_all_t, dm)


def _adam_math(gv, wv, mv, vv):
    m = ADAM_B1 * mv + (1.0 - ADAM_B1) * gv
    v = ADAM_B2 * vv + (1.0 - ADAM_B2) * (gv * gv)
    m_hat = m / (1.0 - ADAM_B1 ** ADAM_STEP)
    v_hat = v / (1.0 - ADAM_B2 ** ADAM_STEP)
    delta = -ADAM_LR * (m_hat / (jnp.sqrt(v_hat) + ADAM_EPS) + ADAM_WD * wv)
    return delta, m, v


def adam_parts(parts, w, m, v, tr, name, comm=None):
    L, R, C = w.shape
    nt = R // tr

    def body(*refs):
        p_refs = refs[:L]
        w_ref, m_ref, v_ref, g_ref, d_ref, mo_ref, vo_ref = refs[L:]
        lyr = pl.program_id(0)
        for l in range(L):
            @pl.when(lyr == l)
            def _(p_ref=p_refs[l]):
                gv = p_ref[0].astype(f32)
                for k in range(1, N_DEV):
                    gv = gv + p_ref[k].astype(f32)
                g_ref[...] = gv
                d_ref[...], mo_ref[...], vo_ref[...] = _adam_math(gv, w_ref[...], m_ref[...], v_ref[...])

    def part_spec(l):
        return pl.BlockSpec((N_DEV, tr, C), lambda lyr, i: (0, jnp.where(lyr == l, i, jnp.where(lyr < l, 0, nt - 1)), 0))

    spec = pl.BlockSpec((None, tr, C), lambda lyr, i: (lyr, i, 0))
    shp = jax.ShapeDtypeStruct((L, R, C), f32)
    return _pcall(
        body, (*parts, w, m, v), name=name, comm=comm,
        grid=(L, nt),
        in_specs=[part_spec(l) for l in range(L)] + [spec, spec, spec],
        out_specs=[spec, spec, spec, spec],
        out_shape=[shp, shp, shp, shp],
    )


def adam_plain(gr, w, m, v, tr, name):
    R, C = w.shape

    def body(g_ref, w_ref, m_ref, v_ref, d_ref, mo_ref, vo_ref):
        d_ref[...], mo_ref[...], vo_ref[...] = _adam_math(g_ref[...], w_ref[...], m_ref[...], v_ref[...])

    spec = pl.BlockSpec((tr, C), lambda i: (i, 0))
    shp = jax.ShapeDtypeStruct((R, C), f32)
    return pl.pallas_call(
        body, name=name,
        grid=(R // tr,),
        in_specs=[spec, spec, spec, spec],
        out_specs=[spec, spec, spec],
        out_shape=[shp, shp, shp],
        compiler_params=_cp(1),
    )(gr, w, m, v)


def sum8(parts, name):
    _, R, C = parts.shape

    def body(p_ref, o_ref):
        acc = p_ref[0]
        for k in range(1, N_DEV):
            acc = acc + p_ref[k]
        o_ref[...] = acc

    return pl.pallas_call(
        body, name=name,
        grid=(1,),
        in_specs=[_full(parts.shape)],
        out_specs=_full((R, C)),
        out_shape=jax.ShapeDtypeStruct((R, C), f32),
        compiler_params=_cp(1),
    )(parts)


def _place():
    return lax.axis_index("x"), lax.axis_index("y"), lax.axis_index("c")


def _gather_steps(ins, outs, send_sems, recv_sems, local_sems, place):
    n = len(ins)
    x, y, c = place
    me, sibling = (x, y, c), (x, y, 1 - c)
    chips = [(1 - x, y), (x, 1 - y), (1 - x, 1 - y)]

    def slot(a, p):
        return outs[a].at[4 * p[0] + 2 * p[1] + p[2]]

    def copy(a, k, block, to, src=None):
        return pltpu.make_async_remote_copy(
            src_ref=slot(a, block) if src is None else src, dst_ref=slot(a, block),
            send_sem=send_sems.at[a * 7 + k], recv_sem=recv_sems.at[a * 7 + k],
            device_id=to, device_id_type=MESH)

    def mine():
        return [pltpu.make_async_copy(ins[a], slot(a, me), local_sems.at[a]) for a in range(n)]

    def first():
        cps = []
        for a in range(n):
            cps.append(copy(a, 0, me, sibling, src=ins[a]))
            cps += [copy(a, 1 + j, me, (*chip, c), src=ins[a]) for j, chip in enumerate(chips)]
        return cps

    def start():
        for cp in mine() + first():
            cp.start()

    def forward():
        for j, chip in enumerate(chips):
            for a in range(n):
                copy(a, 1 + j, (*chip, c), me).wait_recv()
                copy(a, 4 + j, (*chip, c), sibling).start()

    def finish():
        for a in range(n):
            copy(a, 0, sibling, me).wait_recv()
            for j, chip in enumerate(chips):
                copy(a, 4 + j, (*chip, 1 - c), me).wait_recv()
        for cp in first() + [copy(a, 4 + j, (*chip, c), sibling) for j, chip in enumerate(chips) for a in range(n)]:
            cp.wait_send()
        for cp in mine():
            cp.wait()

    return start, forward, finish


def _exchange_steps(ins, outs, send_sems, recv_sems, local_sems, place):
    n = len(ins)
    x, y, c = place
    me_i = 4 * x + 2 * y + c

    def mine():
        return [pltpu.make_async_copy(ins[a].at[me_i], outs[a].at[me_i], local_sems.at[a]) for a in range(n)]

    def copies(receiving):
        cps = []
        for k in range(1, N_DEV):
            px = 1 - x if (k >> 2) & 1 else x
            py = 1 - y if (k >> 1) & 1 else y
            pc = 1 - c if k & 1 else c
            p_i = 4 * px + 2 * py + pc
            for a in range(n):
                sem = a * 7 + k - 1
                cps.append(pltpu.make_async_remote_copy(
                    src_ref=ins[a].at[p_i], dst_ref=outs[a].at[p_i if receiving else me_i],
                    send_sem=send_sems.at[sem], recv_sem=recv_sems.at[sem],
                    device_id=(px, py, pc), device_id_type=MESH))
        return cps

    def start():
        for cp in mine() + copies(False):
            cp.start()

    def finish():
        for cp in copies(True):
            cp.wait_recv()
        for cp in copies(False):
            cp.wait_send()
        for cp in mine():
            cp.wait()

    return start, None, finish


_COMM_STEPS = {"gather": _gather_steps, "exchange": _exchange_steps}


def _comm_out_shapes(kind, arrs):
    if kind == "gather":
        return [jax.ShapeDtypeStruct((N_DEV,) + a.shape, a.dtype) for a in arrs]
    return [jax.ShapeDtypeStruct(a.shape, a.dtype) for a in arrs]


def _comm_sems(n):
    return [pltpu.SemaphoreType.DMA((7 * n,)), pltpu.SemaphoreType.DMA((7 * n,)), pltpu.SemaphoreType.DMA((n,))]


def _pcall(body, args, *, name, grid, in_specs, out_specs, out_shape, scratch_shapes=(), comm=None):
    in_specs, out_specs, out_shape = list(in_specs), list(out_specs), list(out_shape)
    scratch_shapes = list(scratch_shapes)
    cparams = _cp(len(grid))
    if comm is None:
        outs = pl.pallas_call(body, name=name, grid=grid, in_specs=in_specs, out_specs=out_specs, out_shape=out_shape,
                              scratch_shapes=scratch_shapes, compiler_params=cparams)(*args)
        return list(outs), []
    kind, arrs = comm
    nc, n_in, n_out, n_scr = len(arrs), len(in_specs), len(out_specs), len(scratch_shapes)
    total = 1
    for gdim in grid:
        total *= gdim
    forward_step = (total * 3) // 4

    def hosted(*refs):
        core_in, c_in = refs[:n_in], refs[n_in:n_in + nc]
        core_out = refs[n_in + nc:n_in + nc + n_out]
        c_out = refs[n_in + nc + n_out:n_in + 2 * nc + n_out]
        rest = refs[n_in + 2 * nc + n_out:]
        step = pl.program_id(0)
        for ax in range(1, len(grid)):
            step = step * grid[ax] + pl.program_id(ax)
        start, forward, finish = _COMM_STEPS[kind](c_in, c_out, *rest[n_scr:], _place())
        pl.when(step == 0)(start)
        if forward is not None:
            pl.when(step == forward_step)(forward)
        body(*core_in, *core_out, *rest[:n_scr])
        pl.when(step == total - 1)(finish)

    any_spec = pl.BlockSpec(memory_space=pl.ANY)
    outs = pl.pallas_call(
        hosted, name=name, grid=grid,
        in_specs=in_specs + [any_spec] * nc,
        out_specs=out_specs + [any_spec] * nc,
        out_shape=out_shape + _comm_out_shapes(kind, arrs),
        scratch_shapes=scratch_shapes + _comm_sems(nc),
        compiler_params=cparams)(*args, *arrs)
    return list(outs[:n_out]), list(outs[n_out:])


def _comm_call(kind, arrs, name):
    n = len(arrs)

    def body(*refs):
        start, forward, finish = _COMM_STEPS[kind](refs[:n], refs[n:2 * n], *refs[2 * n:], _place())
        start()
        if forward is not None:
            forward()
        finish()

    any_spec = pl.BlockSpec(memory_space=pl.ANY)
    return pl.pallas_call(
        body, name=name,
        in_specs=[any_spec] * n, out_specs=[any_spec] * n,
        out_shape=_comm_out_shapes(kind, arrs), scratch_shapes=_comm_sems(n),
    )(*arrs)


def all_gather(arrs, name):
    return _comm_call("gather", arrs, name)


def all_to_all(arrs, name):
    return _comm_call("exchange", arrs, name)


def _tiles(S):
    t = min(512, S)
    return dict(ffn=t, row=t, conv=t, gla=t, bk=min(1024, S))


BIG = ("wi1", "wo1", "win", "wout", "wi2", "wo2")


def _col_shards_to_full(gathered):
    n, r, c = gathered.shape
    return jnp.transpose(gathered, (1, 0, 2)).reshape(r, n * c)


def _ffn_weights_t(wi, wo):
    return jnp.transpose(wi, (0, 2, 1)), jnp.transpose(wo.reshape(NFS, FS, D), (0, 2, 1))


def _win_full(win_a):
    return jnp.pad(_col_shards_to_full(win_a), ((0, 0), (0, DINP - DIN)))


def train_pass(x, tgt, mods, fmod, sh, ws, wi1_first, wo1_first):
    S = x.shape[0]
    T = _tiles(S)
    bk = T["bk"]
    full = [dict() for _ in range(DEPTH)]
    full[0]["wi1"], full[0]["wo1"] = wi1_first, wo1_first.reshape(F, D)
    saved = []
    xc = x
    for l in range(DEPTH):
        w, fw = ws[f"L{l}"], full[l]
        x0 = xc
        names = ("win", "wout", "wi2", "wo2") if l == 0 else ("wi2", "wo2")
        (x1, h1f, zg1, zu1, f1), got = ffn_fwd(x0, mods[l], w["g1"], fw["wi1"], fw["wo1"], (0, 1, 2), T["ffn"], f"ffn1_fwd_{l}",
                                          comm=("gather", [sh[n][l] for n in names]))
        fw.update(zip(names, got))
        if l == 0:
            fw["win"], fw["wout"] = _win_full(fw["win"]), fw["wout"].reshape(D, D)
        fw["wo2"] = fw["wo2"].reshape(F, D)
        z, la = mixin_fwd(x1, mods[l], w["g2"], fw["win"], w["wgu"], w["bgate"], T["row"], f"mixin_fwd_{l}")
        y, yc = conv_fwd(z, w["wdw"], w["cpar"], T["conv"], f"conv_fwd_{l}")
        yg, sprev = gla_fwd(z, la, w["gn_s"], T["gla"], f"gla_fwd_{l}")
        x2 = mixout_fwd(x1, yc, yg, mods[l], fw["wout"], T["row"], f"mixout_fwd_{l}")
        names = ("wi1", "wo1", "win", "wout") if l + 1 < DEPTH else ()
        (x3, h2f, zg2, zu2, f2), got = ffn_fwd(x2, mods[l], w["g3"], fw["wi2"], fw["wo2"], (6, 7, 8), T["ffn"], f"ffn2_fwd_{l}",
                                          comm=("gather", [sh[n][l + 1] for n in names]) if names else None)
        if names:
            nx = full[l + 1]
            nx["wi1"], nx["wo1"], nx["win"], nx["wout"] = got[0], got[1].reshape(F, D), _win_full(got[2]), got[3].reshape(D, D)
        saved.append(dict(x0=x0, x1=x1, x2=x2, h1f=h1f, zg1=zg1, zu1=zu1, f1=f1, h2f=h2f, zg2=zg2, zu2=zu2, f2=f2,
                          z=z, la=la, y=y, yc=yc, yg=yg, sprev=sprev))
        xc = x3

    dx, redf = final_fwd_bwd(xc, tgt, fmod, ws["gf"], T["row"], "loss_head")
    loss_lanes = redf[0]
    dfmod = redf[1:3]
    grads = {"gf": redf[3]}
    dmods = [None] * DEPTH
    recv = {n: [None] * DEPTH for n in BIG}

    def ffn_backward(xin, dy, h, zg, zu, fo, gain, wi, wo, rows, l, tag, ride=None):
        wi_t, wo_t = _ffn_weights_t(wi, wo)
        (df, a, dzg, dzu), got_ride = ffn_bwd_hidden(dy, zg, zu, mods[l], wo_t, rows[2], T["ffn"], f"{tag}_bwd_hidden_{l}",
                                                     comm=("exchange", ride) if ride else None)
        p_wo, _ = dwo_pieces(a, df, bk, f"d{tag}_wo_{l}")
        p_wi, (r_wo,) = dwi_pieces(h, dzg, dzu, bk, f"d{tag}_wi_{l}",
                                   comm=("exchange", [p_wo.reshape(N_DEV, F // N_DEV, D)]))
        (dxin, red), (r_wi,) = ffn_bwd_input(xin, dy, dzg, dzu, fo, mods[l], gain, wi_t, rows, T["ffn"],
                                             f"{tag}_bwd_input_{l}", comm=("exchange", [p_wi]))
        return dxin, red, r_wi, r_wo, got_ride

    for l in reversed(range(DEPTH)):
        w, fw, sv = ws[f"L{l}"], full[l], saved[l]
        g = {}
        dx2, red3, recv["wi2"][l], recv["wo2"][l], _ = ffn_backward(
            sv["x2"], dx, sv["h2f"], sv["zg2"], sv["zu2"], sv["f2"], w["g3"], fw["wi2"], fw["wo2"], (6, 7, 8), l, "ffn2")
        dmix, dyc, dyg, red_o = mixout_bwd(dx2, sv["yc"], sv["yg"], mods[l], fw["wout"], T["row"], f"mixout_bwd_{l}")
        p_wout = jnp.concatenate([matmul_tn(sv["yc"], dmix, DC, D, DC, D, bk, f"dwout_c_{l}", out_dtype=bf16),
                                  matmul_tn(sv["yg"], dmix, DG, D, DG, D, bk, f"dwout_g_{l}", out_dtype=bf16)], axis=0)
        (dq, dk, dv, dr, dpre, redg, redb), (recv["wout"][l],) = gla_bwd(
            sv["z"], sv["la"], sv["sprev"], dyg, w["gn_s"], T["gla"], f"gla_bwd_{l}",
            comm=("exchange", [p_wout.reshape(N_DEV, D // N_DEV, D)]))
        (dzab, redc), _ = conv_bwd(sv["z"], sv["y"], dyc, w["wdw"], w["cpar"], T["conv"], f"conv_bwd_{l}")
        (dx1, h2, dz, red2), _ = mixin_bwd(sv["x1"], dx2, dzab, dq, dk, dv, dr, dpre, mods[l], w["g2"], fw["win"], w["wgu"],
                                           T["row"], f"mixin_bwd_{l}")
        dwin = matmul_tn(h2, dz, D, DINP, D, DINP, bk, f"dwin_{l}", out_dtype=bf16)[:, :DIN]
        p_win = jnp.transpose(dwin.reshape(D, N_DEV, DIN // N_DEV), (1, 0, 2))
        g["wgu"] = matmul_tn(sv["z"], dpre, 128, DQK, 128, DQK, bk, f"dwgu_{l}", a_col_block=(DINP - 128) // 128)[:GR]
        g["bgate"] = jnp.sum(redb, axis=0)
        g["gn"] = jnp.sum(redg.reshape(NH, CH, DV), axis=1)
        g["wdw"] = redc[:CW]
        g["bdw"], g["gln"], g["bln"] = redc[32], redc[33], redc[34]
        dx0, red1, recv["wi1"][l], recv["wo1"][l], (recv["win"][l],) = ffn_backward(
            sv["x0"], dx1, sv["h1f"], sv["zg1"], sv["zu1"], sv["f1"], w["g1"], fw["wi1"], fw["wo1"], (0, 1, 2), l, "ffn1",
            ride=[p_win])
        g["g1"], g["g2"], g["g3"] = red1[3], red2[2], red3[3]
        dmods[l] = jnp.stack([red1[0], red1[1], red1[2], red2[0], red2[1], red_o[0], red3[0], red3[1], red3[2]], axis=0)
        grads[f"L{l}"] = g
        dx = dx0
    return loss_lanes, dx, grads, dmods, dfmod, recv


def _pad_rows(a, rows):
    return jnp.pad(a, ((0, rows - a.shape[0]), (0, 0)))


def kernel(x, c, w_ada, b_ada, g_norm_ffn1, w_ffn1_in, w_ffn1_out, g_norm_mix, w_in, w_dw, b_dw, g_conv_ln, b_conv_ln, w_gate_up, b_gate, g_gla_norm, w_out, g_norm_ffn2, w_ffn2_in, w_ffn2_out, g_norm_final, w_ada_final, b_ada_final, loss_target, m_w_ada, m_b_ada, m_g_norm_ffn1, m_w_ffn1_in, m_w_ffn1_out, m_g_norm_mix, m_w_in, m_w_dw, m_b_dw, m_g_conv_ln, m_b_conv_ln, m_w_gate_up, m_b_gate, m_g_gla_norm, m_w_out, m_g_norm_ffn2, m_w_ffn2_in, m_w_ffn2_out, m_g_norm_final, m_w_ada_final, m_b_ada_final, v_w_ada, v_b_ada, v_g_norm_ffn1, v_w_ffn1_in, v_w_ffn1_out, v_g_norm_mix, v_w_in, v_w_dw, v_b_dw, v_g_conv_ln, v_b_conv_ln, v_w_gate_up, v_b_gate, v_g_gla_norm, v_w_out, v_g_norm_ffn2, v_w_ffn2_in, v_w_ffn2_out, v_g_norm_final, v_w_ada_final, v_b_ada_final):
    me = 4 * lax.axis_index("x") + 2 * lax.axis_index("y") + lax.axis_index("c")
    L = DEPTH
    n_ada = N_MOD * D // N_DEV
    n_fin = 2 * D // N_DEV

    small = jnp.concatenate([c.reshape(-1), w_dw.reshape(-1), w_gate_up.reshape(-1)])
    n_small = small.shape[0]
    small = jnp.pad(small, (0, 8 * D - n_small)).reshape(8, D)
    big = dict(wi1=w_ffn1_in, wo1=w_ffn1_out, win=w_in, wout=w_out, wi2=w_ffn2_in, wo2=w_ffn2_out)
    sh = {n: [a[l].astype(bf16) for l in range(L)] for n, a in big.items()}
    small_a, wi1_first, wo1_first = all_gather([small, sh["wi1"][0], sh["wo1"][0]], "gather_first")
    small_a = small_a.reshape(N_DEV, 8 * D)
    c_all = small_a[:, :D]
    o1 = D + L * CW * (DC // N_DEV)
    wdw_full = _col_shards_to_full(small_a[:, D:o1].reshape(N_DEV, L * CW, DC // N_DEV)).reshape(L, CW, DC)
    wgu_full = _col_shards_to_full(small_a[:, o1:o1 + L * GR * (DQK // N_DEV)].reshape(N_DEV, L * GR, DQK // N_DEV)).reshape(L, GR, DQK)

    b_ada_mine = lax.dynamic_slice(b_ada, (0, me * n_ada), (L, n_ada))
    b_fin_mine = lax.dynamic_slice(b_ada_final, (me * n_fin,), (n_fin,))
    parts = [ada_fwd(c_all, w_ada[l], b_ada_mine[l:l + 1], f"ada_fwd_{l}") for l in range(L)]
    parts.append(ada_fwd(c_all, w_ada_final, b_fin_mine.reshape(1, n_fin), "ada_fwd_final"))
    modsrc = jnp.concatenate(parts, axis=1)
    n_row = modsrc.shape[1]
    modsrc = jnp.pad(modsrc, ((0, 0), (0, 24 * 128 - n_row))).reshape(N_DEV, 24, 128)
    (modrecv,) = all_to_all([modsrc], "exchange_mod")
    modrecv = modrecv.reshape(N_DEV, 24 * 128)
    mods = []
    for l in range(L):
        mvec = modrecv[:, l * n_ada:(l + 1) * n_ada].reshape(N_MOD, D)
        mods.append(_pad_rows(mvec, 16))
    fmod = _pad_rows(modrecv[:, L * n_ada:L * n_ada + n_fin].reshape(2, D), 8)

    ws = {"gf": g_norm_final.reshape(1, D)}
    for l in range(L):
        ws[f"L{l}"] = dict(
            g1=g_norm_ffn1[l].reshape(1, D), g2=g_norm_mix[l].reshape(1, D), g3=g_norm_ffn2[l].reshape(1, D),
            wgu=_pad_rows(wgu_full[l], 128).astype(bf16),
            bgate=b_gate[l].reshape(1, DQK),
            wdw=_pad_rows(wdw_full[l], 32),
            cpar=_pad_rows(jnp.stack([b_dw[l], g_conv_ln[l], b_conv_ln[l]]), 8),
            gn_s=jnp.repeat(g_gla_norm[l], CH, axis=0),
        )

    loss_lanes, grad_x, gr, dmods, dfmod, recv = train_pass(
        x[0], loss_target[0], mods, fmod, sh, ws, wi1_first, wo1_first)

    def adam_big(rv, w, m, v, name):
        R = w.shape[1]
        tr = 256 if R % 256 == 0 else R // 2
        return adam_parts(rv, w, m, v, tr, name)

    res = {}
    res["w_ffn2_in"], _ = adam_big(recv["wi2"], w_ffn2_in, m_w_ffn2_in, v_w_ffn2_in, "adam_ffn2_in")
    res["w_ffn2_out"], _ = adam_big(recv["wo2"], w_ffn2_out, m_w_ffn2_out, v_w_ffn2_out, "adam_ffn2_out")
    res["w_in"], _ = adam_big(recv["win"], w_in, m_w_in, v_w_in, "adam_w_in")
    res["w_out"], _ = adam_big(recv["wout"], w_out, m_w_out, v_w_out, "adam_w_out")
    res["w_ffn1_out"], _ = adam_big(recv["wo1"], w_ffn1_out, m_w_ffn1_out, v_w_ffn1_out, "adam_ffn1_out")
    res["w_ffn1_in"], _ = adam_big(recv["wi1"], w_ffn1_in, m_w_ffn1_in, v_w_ffn1_in, "adam_ffn1_in")

    flat = lambda name: jnp.stack([gr[f"L{l}"][name] for l in range(L)]).reshape(-1)
    sections = [
        ("b_ada", jnp.stack(dmods).reshape(-1)), ("b_ada_final", dfmod.reshape(-1)),
        ("g_norm_ffn1", flat("g1")), ("g_norm_mix", flat("g2")), ("g_norm_ffn2", flat("g3")), ("g_norm_final", gr["gf"]),
        ("b_dw", flat("bdw")), ("g_conv_ln", flat("gln")), ("b_conv_ln", flat("bln")), ("b_gate", flat("bgate")),
        ("g_gla_norm", flat("gn")),
    ]
    n_rep = sum(s[1].shape[0] for s in sections)
    rep_rows = -(-n_rep // D)
    extra = [("loss", loss_lanes), ("w_dw", flat("wdw")), ("w_gate_up", flat("wgu"))]
    pack = jnp.concatenate([s[1] for s in sections] + [jnp.zeros((rep_rows * D - n_rep,), f32)] + [s[1] for s in extra])
    n_pack = pack.shape[0]
    pack_rows = -(-n_pack // (8 * D)) * 8
    pack = jnp.pad(pack, (0, pack_rows * D - n_pack)).reshape(pack_rows, D)
    (pack_all,) = all_gather([pack], "gather_small_grads")
    tot = sum8(pack_all, "sum_small_grads")
    tot_flat = tot.reshape(-1)
    loss = jnp.sum(tot_flat[rep_rows * D:rep_rows * D + D])
    o_dw = rep_rows * D + D
    g_wdw_full = tot_flat[o_dw:o_dw + L * CW * DC].reshape(L, CW, DC)
    o_gu = o_dw + L * CW * DC
    g_wgu_full = tot_flat[o_gu:o_gu + L * GR * DQK].reshape(L, GR, DQK)

    small_params = dict(b_ada=(b_ada, m_b_ada, v_b_ada), b_ada_final=(b_ada_final, m_b_ada_final, v_b_ada_final),
                        g_norm_ffn1=(g_norm_ffn1, m_g_norm_ffn1, v_g_norm_ffn1), g_norm_mix=(g_norm_mix, m_g_norm_mix, v_g_norm_mix),
                        g_norm_ffn2=(g_norm_ffn2, m_g_norm_ffn2, v_g_norm_ffn2), g_norm_final=(g_norm_final, m_g_norm_final, v_g_norm_final),
                        b_dw=(b_dw, m_b_dw, v_b_dw), g_conv_ln=(g_conv_ln, m_g_conv_ln, v_g_conv_ln),
                        b_conv_ln=(b_conv_ln, m_b_conv_ln, v_b_conv_ln), b_gate=(b_gate, m_b_gate, v_b_gate),
                        g_gla_norm=(g_gla_norm, m_g_gla_norm, v_g_gla_norm))

    def rep_pack(idx):
        p = jnp.concatenate([small_params[s[0]][idx].reshape(-1) for s in sections])
        return jnp.pad(p, (0, rep_rows * D - n_rep)).reshape(rep_rows, D)

    g_rep = tot[:rep_rows]
    d_rep, m_rep, v_rep = adam_plain(g_rep, rep_pack(0), rep_pack(1), rep_pack(2), rep_rows, "adam_small")
    off = 0
    for sname, sval in sections:
        shp = small_params[sname][0].shape
        nel = sval.shape[0]
        res[sname] = [a.reshape(-1)[off:off + nel].reshape(shp) for a in (g_rep, d_rep, m_rep, v_rep)]
        off += nel

    def adam_cols(g_full, w, m, v, name):
        shp = w.shape
        g_mine = lax.dynamic_slice(g_full, (0, 0, me * shp[2]), shp)
        R, C = shp[0] * shp[1], shp[2]
        outs = adam_plain(g_mine.reshape(R, C), w.reshape(R, C), m.reshape(R, C), v.reshape(R, C), R, name)
        return [g_mine] + [o.reshape(shp) for o in outs]

    res["w_dw"] = adam_cols(g_wdw_full, w_dw, m_w_dw, v_w_dw, "adam_w_dw")
    res["w_gate_up"] = adam_cols(g_wgu_full, w_gate_up, m_w_gate_up, v_w_gate_up, "adam_w_gate_up")

    c_all_t = c_all.T
    dmod_all = pack_all.reshape(N_DEV, -1)[:, :L * N_MOD * D].reshape(N_DEV, L, N_MOD * D)
    dfm_all = pack_all.reshape(N_DEV, -1)[:, L * N_MOD * D:L * N_MOD * D + 2 * D]
    dm_mine = lax.dynamic_slice(dmod_all, (0, 0, me * n_ada), (N_DEV, L, n_ada))
    dfm_mine = lax.dynamic_slice(dfm_all, (0, me * n_fin), (N_DEV, n_fin))
    g_w_ada = jnp.stack([ada_wgrad(c_all_t, dm_mine[:, l], f"ada_wgrad_{l}") for l in range(L)])
    g_w_fin = ada_wgrad(c_all_t, dfm_mine, "ada_wgrad_final")
    outs = adam_plain(g_w_ada.reshape(L * D, n_ada), w_ada.reshape(L * D, n_ada), m_w_ada.reshape(L * D, n_ada),
                      v_w_ada.reshape(L * D, n_ada), 256, "adam_w_ada")
    res["w_ada"] = [g_w_ada] + [o.reshape(w_ada.shape) for o in outs]
    res["w_ada_final"] = [g_w_fin] + list(adam_plain(g_w_fin, w_ada_final, m_w_ada_final, v_w_ada_final, 256, "adam_w_ada_final"))

    order = ["w_ada", "b_ada", "g_norm_ffn1", "w_ffn1_in", "w_ffn1_out", "g_norm_mix", "w_in", "w_dw", "b_dw", "g_conv_ln",
             "b_conv_ln", "w_gate_up", "b_gate", "g_gla_norm", "w_out", "g_norm_ffn2", "w_ffn2_in", "w_ffn2_out",
             "g_norm_final", "w_ada_final", "b_ada_final"]
    out = [loss, grad_x[None]]
    for k in range(4):
        out += [res[name][k] for name in order]
    return tuple(out)
```

```python
import functools

import jax
import jax.numpy as jnp
from jax import lax
from jax.experimental import pallas as pl
from jax.experimental.pallas import tpu as pltpu

f32 = jnp.float32
bf16 = jnp.bfloat16

N_DEV = 8
DEPTH = 2
D = 1024
F = 2816
DC = 512
NH = 4
DK = 64
DV = 128
DQK = NH * DK
DG = NH * DV
CH = 64
CW = 31
GR = 16
TAU = 16.0
N_MOD = 9
DIN = 2 * DC + 2 * DQK + 2 * DG + GR
DINP = 2688
EPS = 1e-6
HALO = 32
SUBLANES = 8
NFS = 4
FS = F // NFS

ADAM_LR = 0.001
ADAM_B1 = 0.9
ADAM_B2 = 0.999
ADAM_EPS = 1e-08
ADAM_WD = 0.01
ADAM_STEP = 10

V7X_VMEM_LIMIT = 56 * 1024 * 1024
MESH = pl.DeviceIdType.MESH
HIGHEST = lax.Precision.HIGHEST

NT = (((1,), (1,)), ((), ()))
TN = (((0,), (0,)), ((), ()))


def _cp(n_axes):
    return pltpu.CompilerParams(dimension_semantics=("arbitrary",) * n_axes, vmem_limit_bytes=V7X_VMEM_LIMIT)


def _full(shape):
    nd = len(shape)
    return pl.BlockSpec(shape, lambda *_: (0,) * nd)


def _resident(shape):
    nd = len(shape)
    return pl.BlockSpec(shape, lambda *_: (0,) * nd, pipeline_mode=pl.Buffered(1))


def _dot(a, b):
    return jnp.dot(a, b, preferred_element_type=f32)


def _dg(a, b, dims):
    return lax.dot_general(a, b, dims, preferred_element_type=f32)


def _sigmoid(x):
    return jax.nn.sigmoid(x)


def _rowsum(x):
    return jnp.sum(x, axis=0, keepdims=True)


def _rms_parts(xv):
    rstd = lax.rsqrt(jnp.mean(xv * xv, axis=-1, keepdims=True) + EPS)
    return xv * rstd, rstd


def _rms_bwd(dxh, xh, rstd):
    return rstd * (dxh - xh * jnp.mean(dxh * xh, axis=-1, keepdims=True))


def ffn_fwd(x, mod, g, wi, wo, rows, tm, name, comm=None):
    S = x.shape[0]
    r_shift, r_scale, r_gate = rows

    def body(x_ref, mod_ref, g_ref, wi_ref, wo_ref, xo_ref, h_ref, zg_ref, zu_ref, f_ref):
        xv = x_ref[...]
        xh, _ = _rms_parts(xv)
        h = (xh * g_ref[...] * (1.0 + mod_ref[r_scale:r_scale + 1, :]) + mod_ref[r_shift:r_shift + 1, :]).astype(bf16)
        h_ref[...] = h
        fv = None
        for j in range(NFS):
            zg = _dot(h, wi_ref[j])
            zu = _dot(h, wi_ref[j + NFS])
            zg_ref[j] = zg.astype(bf16)
            zu_ref[j] = zu.astype(bf16)
            a = zg * _sigmoid(zg) * zu
            part = _dot(a.astype(bf16), wo_ref[j * FS:(j + 1) * FS, :])
            fv = part if fv is None else fv + part
        f_ref[...] = fv.astype(bf16)
        xo_ref[...] = xv + 0.5 * mod_ref[r_gate:r_gate + 1, :] * fv

    row = lambda i: (i, 0)
    tile = lambda i: (0, i, 0)
    shard = jax.ShapeDtypeStruct((NFS, S, FS), bf16)
    return _pcall(
        body, (x, mod, g, wi, wo), name=name, comm=comm,
        grid=(S // tm,),
        in_specs=[pl.BlockSpec((tm, D), row), _full(mod.shape), _full(g.shape), _resident(wi.shape), _resident(wo.shape)],
        out_specs=[pl.BlockSpec((tm, D), row), pl.BlockSpec((tm, D), row),
                   pl.BlockSpec((NFS, tm, FS), tile), pl.BlockSpec((NFS, tm, FS), tile), pl.BlockSpec((tm, D), row)],
        out_shape=[jax.ShapeDtypeStruct((S, D), f32), jax.ShapeDtypeStruct((S, D), bf16), shard, shard,
                   jax.ShapeDtypeStruct((S, D), bf16)],
    )


def ffn_bwd_hidden(dy, zg, zu, mod, wo_t, r_gate, tm, name, comm=None):
    S = dy.shape[0]

    def body(dy_ref, zg_ref, zu_ref, mod_ref, wo_ref, df_ref, a_ref, dzg_ref, dzu_ref):
        df = (0.5 * mod_ref[r_gate:r_gate + 1, :] * dy_ref[...]).astype(bf16)
        df_ref[...] = df
        for j in range(NFS):
            zgv = zg_ref[j].astype(f32)
            zuv = zu_ref[j].astype(f32)
            s = _sigmoid(zgv)
            sil = zgv * s
            a_ref[j] = (sil * zuv).astype(bf16)
            da = _dot(df, wo_ref[j])
            dzu_ref[j] = (da * sil).astype(bf16)
            dzg_ref[j] = (da * zuv * (s * (1.0 + zgv * (1.0 - s)))).astype(bf16)

    row = lambda i: (i, 0)
    tile = lambda i: (0, i, 0)
    shard = jax.ShapeDtypeStruct((NFS, S, FS), bf16)
    tspec = pl.BlockSpec((NFS, tm, FS), tile)
    return _pcall(
        body, (dy, zg, zu, mod, wo_t), name=name, comm=comm,
        grid=(S // tm,),
        in_specs=[pl.BlockSpec((tm, D), row), tspec, tspec, _full(mod.shape), _resident(wo_t.shape)],
        out_specs=[pl.BlockSpec((tm, D), row), tspec, tspec, tspec],
        out_shape=[jax.ShapeDtypeStruct((S, D), bf16), shard, shard, shard],
    )


def ffn_bwd_input(x, dy, dzg, dzu, fo, mod, g, wi_t, rows, tm, name, comm=None):
    S = x.shape[0]
    r_shift, r_scale, r_gate = rows

    def body(x_ref, dy_ref, dzg_ref, dzu_ref, f_ref, mod_ref, g_ref, wi_ref, dx_ref, red_ref):
        @pl.when(pl.program_id(0) == 0)
        def _():
            red_ref[...] = jnp.zeros_like(red_ref)

        dh = _dot(dzg_ref[0], wi_ref[0]) + _dot(dzu_ref[0], wi_ref[NFS])
        for j in range(1, NFS):
            dh = dh + _dot(dzg_ref[j], wi_ref[j]) + _dot(dzu_ref[j], wi_ref[j + NFS])
        dyv = dy_ref[...]
        xh, rstd = _rms_parts(x_ref[...])
        gv = g_ref[...]
        n = xh * gv
        dn = dh * (1.0 + mod_ref[r_scale:r_scale + 1, :])
        red_ref[0:1, :] += _rowsum(dh)
        red_ref[1:2, :] += _rowsum(dh * n)
        red_ref[2:3, :] += _rowsum(0.5 * f_ref[...].astype(f32) * dyv)
        red_ref[3:4, :] += _rowsum(dn * xh)
        dx_ref[...] = dyv + _rms_bwd(dn * gv, xh, rstd)

    row = lambda i: (i, 0)
    tspec = pl.BlockSpec((NFS, tm, FS), lambda i: (0, i, 0))
    return _pcall(
        body, (x, dy, dzg, dzu, fo, mod, g, wi_t), name=name, comm=comm,
        grid=(S // tm,),
        in_specs=[pl.BlockSpec((tm, D), row), pl.BlockSpec((tm, D), row), tspec, tspec, pl.BlockSpec((tm, D), row),
                  _full(mod.shape), _full(g.shape), _resident(wi_t.shape)],
        out_specs=[pl.BlockSpec((tm, D), row), _full((8, D))],
        out_shape=[jax.ShapeDtypeStruct((S, D), f32), jax.ShapeDtypeStruct((8, D), f32)],
    )


def matmul_tn(a, b, M, N, bm, bn, bk, name, a_col_block=0, out_dtype=f32):
    S = b.shape[0]
    nk = S // bk

    def body(a_ref, b_ref, o_ref, acc_s):
        k = pl.program_id(2)

        @pl.when(k == 0)
        def _():
            acc_s[...] = jnp.zeros_like(acc_s)

        acc_s[...] += _dg(a_ref[...].astype(bf16), b_ref[...].astype(bf16), TN)

        @pl.when(k == nk - 1)
        def _():
            o_ref[...] = acc_s[...].astype(out_dtype)

    return pl.pallas_call(
        body, name=name,
        grid=(M // bm, N // bn, nk),
        in_specs=[
            pl.BlockSpec((bk, bm), lambda i, j, k: (k, i + a_col_block)),
            pl.BlockSpec((bk, bn), lambda i, j, k: (k, j)),
        ],
        out_specs=pl.BlockSpec((bm, bn), lambda i, j, k: (i, j)),
        out_shape=jax.ShapeDtypeStruct((M, N), out_dtype),
        scratch_shapes=[pltpu.VMEM((bm, bn), f32)],
        compiler_params=_cp(3),
    )(a, b)


def dwi_pieces(h, dzg, dzu, bk, name, comm=None):
    S = h.shape[0]
    nk = S // bk

    def body(h_ref, g_ref, u_ref, o_ref, acc_s):
        half = pl.program_id(0)
        k = pl.program_id(1)

        @pl.when(k == 0)
        def _():
            acc_s[...] = jnp.zeros_like(acc_s)

        hv = h_ref[...]

        @pl.when(half == 0)
        def _():
            for j in range(NFS):
                acc_s[j] += _dg(hv, g_ref[j], TN)

        @pl.when(half == 1)
        def _():
            for j in range(NFS):
                acc_s[j] += _dg(hv, u_ref[j], TN)

        @pl.when(k == nk - 1)
        def _():
            o_ref[...] = acc_s[...].astype(bf16)

    (out,), comm_outs = _pcall(
        body, (h, dzg, dzu), name=name, comm=comm,
        grid=(2, nk),
        in_specs=[
            pl.BlockSpec((bk, D), lambda half, k: (k, 0)),
            pl.BlockSpec((NFS, bk, FS), lambda half, k: (0, jnp.where(half == 0, k, nk - 1), 0)),
            pl.BlockSpec((NFS, bk, FS), lambda half, k: (0, jnp.where(half == 1, k, 0), 0)),
        ],
        out_specs=[pl.BlockSpec((NFS, D, FS), lambda half, k: (half, 0, 0))],
        out_shape=[jax.ShapeDtypeStruct((2 * NFS, D, FS), bf16)],
        scratch_shapes=[pltpu.VMEM((NFS, D, FS), f32)],
    )
    return out, comm_outs


def dwo_pieces(a, df, bk, name, comm=None):
    S = df.shape[0]
    nk = S // bk

    def body(a_ref, d_ref, o_ref, acc_s):
        k = pl.program_id(0)

        @pl.when(k == 0)
        def _():
            acc_s[...] = jnp.zeros_like(acc_s)

        dv = d_ref[...]
        for j in range(NFS):
            acc_s[j] += _dg(a_ref[j], dv, TN)

        @pl.when(k == nk - 1)
        def _():
            o_ref[...] = acc_s[...].astype(bf16)

    (out,), comm_outs = _pcall(
        body, (a, df), name=name, comm=comm,
        grid=(nk,),
        in_specs=[pl.BlockSpec((NFS, bk, FS), lambda k: (0, k, 0)), pl.BlockSpec((bk, D), lambda k: (k, 0))],
        out_specs=[_full((NFS, FS, D))],
        out_shape=[jax.ShapeDtypeStruct((NFS, FS, D), bf16)],
        scratch_shapes=[pltpu.VMEM((NFS, FS, D), f32)],
    )
    return out.reshape(F, D), comm_outs


def mixin_fwd(x1, mod, g, win, wgu, bgate, tm, name):
    S = x1.shape[0]

    def body(x_ref, mod_ref, g_ref, win_ref, wgu_ref, bg_ref, z_ref, la_ref):
        xh, _ = _rms_parts(x_ref[...])
        hv = xh * g_ref[...] * (1.0 + mod_ref[4:5, :]) + mod_ref[3:4, :]
        z = _dot(hv.astype(bf16), win_ref[...])
        z_ref[...] = z
        glr = z[:, DINP - 128:]
        pre = _dot(glr.astype(bf16), wgu_ref[...]) + bg_ref[...]
        la_ref[...] = (jnp.minimum(pre, 0.0) - jnp.log(1.0 + jnp.exp(-jnp.abs(pre)))) * (1.0 / TAU)

    return pl.pallas_call(
        body, name=name,
        grid=(S // tm,),
        in_specs=[pl.BlockSpec((tm, D), lambda i: (i, 0)), _full(mod.shape), _full(g.shape),
                  _full(win.shape), _full(wgu.shape), _full(bgate.shape)],
        out_specs=[pl.BlockSpec((tm, DINP), lambda i: (i, 0)), pl.BlockSpec((tm, DQK), lambda i: (i, 0))],
        out_shape=[jax.ShapeDtypeStruct((S, DINP), f32), jax.ShapeDtypeStruct((S, DQK), f32)],
        compiler_params=_cp(1),
    )(x1, mod, g, win, wgu, bgate)


def mixin_bwd(x1, dres, dzab, dq, dk, dv, dr, dpre, mod, g, win, wgu, tm, name, comm=None):
    S = x1.shape[0]

    def body(x_ref, dres_ref, dzab_ref, dq_ref, dk_ref, dv_ref, dr_ref, dpre_ref, mod_ref, g_ref, win_ref, wgu_ref,
             dx_ref, h_ref, dz_ref, red_ref):
        @pl.when(pl.program_id(0) == 0)
        def _():
            red_ref[...] = jnp.zeros_like(red_ref)

        dglr = _dg(dpre_ref[...].astype(bf16), wgu_ref[...], NT)
        dz = jnp.concatenate([dzab_ref[...], dq_ref[...], dk_ref[...], dv_ref[...], dr_ref[...], dglr], axis=1).astype(bf16)
        dz_ref[...] = dz
        dh = _dg(dz, win_ref[...], NT)
        xh, rstd = _rms_parts(x_ref[...])
        gv = g_ref[...]
        n = xh * gv
        sc = 1.0 + mod_ref[4:5, :]
        h_ref[...] = (n * sc + mod_ref[3:4, :]).astype(bf16)
        dn = dh * sc
        red_ref[0:1, :] += _rowsum(dh)
        red_ref[1:2, :] += _rowsum(dh * n)
        red_ref[2:3, :] += _rowsum(dn * xh)
        dx_ref[...] = dres_ref[...] + _rms_bwd(dn * gv, xh, rstd)

    row = lambda i: (i, 0)
    return _pcall(
        body, (x1, dres, dzab, dq, dk, dv, dr, dpre, mod, g, win, wgu), name=name, comm=comm,
        grid=(S // tm,),
        in_specs=[pl.BlockSpec((tm, D), row), pl.BlockSpec((tm, D), row),
                  pl.BlockSpec((tm, 2 * DC), row), pl.BlockSpec((tm, DQK), row), pl.BlockSpec((tm, DQK), row),
                  pl.BlockSpec((tm, DG), row), pl.BlockSpec((tm, DG), row), pl.BlockSpec((tm, DQK), row),
                  _full(mod.shape), _full(g.shape), _full(win.shape), _full(wgu.shape)],
        out_specs=[pl.BlockSpec((tm, D), row), pl.BlockSpec((tm, D), row), pl.BlockSpec((tm, DINP), row), _full((8, D))],
        out_shape=[jax.ShapeDtypeStruct((S, D), f32), jax.ShapeDtypeStruct((S, D), bf16),
                   jax.ShapeDtypeStruct((S, DINP), bf16), jax.ShapeDtypeStruct((8, D), f32)],
    )


def _glu(zab):
    return zab[:, :DC] * _sigmoid(zab[:, DC:])


def _shift_copies(src_s, dst_s, tc):
    n = tc + HALO - SUBLANES
    for b in range(1, SUBLANES):
        dst_s[b, 0:n, :] = src_s[b:b + n, :]


def _shifted(src_s, dst_s, o, tc):
    b = o % SUBLANES
    a = o - b
    return src_s[a:a + tc, :] if b == 0 else dst_s[b, a:a + tc, :]


def conv_fwd(z, wdw, cpar, tc, name):
    S = z.shape[0]
    nb = tc // HALO

    def body(zc_ref, zp_ref, w_ref, cp_ref, y_ref, yc_ref, u_s, us_s):
        i = pl.program_id(0)
        up = _glu(zp_ref[...])
        u_s[0:HALO, :] = jnp.where(i > 0, up, 0.0)
        u_s[HALO:HALO + tc, :] = _glu(zc_ref[...])
        _shift_copies(u_s, us_s, tc)
        acc = jnp.zeros((tc, DC), f32)
        for w in range(CW):
            acc = acc + _shifted(u_s, us_s, HALO - (CW - 1) + w, tc) * w_ref[w:w + 1, :]
        y = acc + cp_ref[0:1, :]
        y_ref[...] = y
        yc = y - jnp.mean(y, axis=-1, keepdims=True)
        yl = yc * lax.rsqrt(jnp.mean(yc * yc, axis=-1, keepdims=True) + EPS) * cp_ref[1:2, :] + cp_ref[2:3, :]
        yc_ref[...] = (yl * _sigmoid(yl)).astype(bf16)

    return pl.pallas_call(
        body, name=name,
        grid=(S // tc,),
        in_specs=[pl.BlockSpec((tc, 2 * DC), lambda i: (i, 0)),
                  pl.BlockSpec((HALO, 2 * DC), lambda i: (jnp.maximum(i * nb - 1, 0), 0)),
                  _full(wdw.shape), _full(cpar.shape)],
        out_specs=[pl.BlockSpec((tc, DC), lambda i: (i, 0)), pl.BlockSpec((tc, DC), lambda i: (i, 0))],
        out_shape=[jax.ShapeDtypeStruct((S, DC), f32), jax.ShapeDtypeStruct((S, DC), bf16)],
        scratch_shapes=[pltpu.VMEM((HALO + tc, DC), f32), pltpu.VMEM((SUBLANES, HALO + tc, DC), f32)],
        compiler_params=_cp(1),
    )(z, z, wdw, cpar)


def conv_bwd(z, y, dyc, wdw, cpar, tc, name, comm=None):
    S = z.shape[0]
    nb = tc // HALO
    nt = S // tc
    last_halo = S // HALO - 1

    def body(zc_ref, zp_ref, y_ref, yn_ref, d_ref, dn_ref, w_ref, cp_ref, dz_ref, red_ref, u_s, dy_s, us_s, dys_s):
        i = pl.program_id(0)

        @pl.when(i == 0)
        def _():
            red_ref[...] = jnp.zeros_like(red_ref)

        gl = cp_ref[1:2, :]
        bl = cp_ref[2:3, :]

        def ln_bwd(yv, dv):
            yc = yv - jnp.mean(yv, axis=-1, keepdims=True)
            rstd = lax.rsqrt(jnp.mean(yc * yc, axis=-1, keepdims=True) + EPS)
            yh = yc * rstd
            yl = yh * gl + bl
            s = _sigmoid(yl)
            dyl = dv * (s * (1.0 + yl * (1.0 - s)))
            dyh = dyl * gl
            dyv = rstd * (dyh - jnp.mean(dyh, axis=-1, keepdims=True) - yh * jnp.mean(dyh * yh, axis=-1, keepdims=True))
            return dyv, dyl, yh

        dy_c, dyl_c, yh_c = ln_bwd(y_ref[...], d_ref[...])
        dy_n, _, _ = ln_bwd(yn_ref[...], dn_ref[...])
        dy_s[0:tc, :] = dy_c
        dy_s[tc:tc + HALO, :] = jnp.where(i < nt - 1, dy_n, 0.0)
        zc = zc_ref[...]
        av = zc[:, :DC]
        sb = _sigmoid(zc[:, DC:])
        u_s[0:HALO, :] = jnp.where(i > 0, _glu(zp_ref[...]), 0.0)
        u_s[HALO:HALO + tc, :] = av * sb
        _shift_copies(u_s, us_s, tc)
        _shift_copies(dy_s, dys_s, tc)
        du = jnp.zeros((tc, DC), f32)
        for w in range(CW):
            red_ref[w:w + 1, :] += _rowsum(_shifted(u_s, us_s, HALO - (CW - 1) + w, tc) * dy_c)
            du = du + _shifted(dy_s, dys_s, CW - 1 - w, tc) * w_ref[w:w + 1, :]
        red_ref[32:33, :] += _rowsum(dy_c)
        red_ref[33:34, :] += _rowsum(dyl_c * yh_c)
        red_ref[34:35, :] += _rowsum(dyl_c)
        dz_ref[...] = jnp.concatenate([du * sb, du * av * sb * (1.0 - sb)], axis=1)

    cur = lambda i: (i, 0)
    nxt = lambda i: (jnp.minimum((i + 1) * nb, last_halo), 0)
    return _pcall(
        body, (z, z, y, y, dyc, dyc, wdw, cpar), name=name, comm=comm,
        grid=(nt,),
        in_specs=[pl.BlockSpec((tc, 2 * DC), cur),
                  pl.BlockSpec((HALO, 2 * DC), lambda i: (jnp.maximum(i * nb - 1, 0), 0)),
                  pl.BlockSpec((tc, DC), cur), pl.BlockSpec((HALO, DC), nxt),
                  pl.BlockSpec((tc, DC), cur), pl.BlockSpec((HALO, DC), nxt),
                  _full(wdw.shape), _full(cpar.shape)],
        out_specs=[pl.BlockSpec((tc, 2 * DC), cur), _full((40, DC))],
        out_shape=[jax.ShapeDtypeStruct((S, 2 * DC), f32), jax.ShapeDtypeStruct((40, DC), f32)],
        scratch_shapes=[pltpu.VMEM((HALO + tc, DC), f32), pltpu.VMEM((tc + HALO, DC), f32),
                        pltpu.VMEM((SUBLANES, HALO + tc, DC), f32), pltpu.VMEM((SUBLANES, HALO + tc, DC), f32)],
    )


def _gla_consts():
    r = lax.broadcasted_iota(jnp.int32, (CH, CH), 0)
    c = lax.broadcasted_iota(jnp.int32, (CH, CH), 1)
    tril = r >= c
    lane = lax.broadcasted_iota(jnp.int32, (CH, DQK), 1)
    masks = [(lane >= h * DK) & (lane < (h + 1) * DK) for h in range(NH)]
    r4 = lax.broadcasted_iota(jnp.int32, (DQK, DQK), 0)
    c4 = lax.broadcasted_iota(jnp.int32, (DQK, DQK), 1)
    eye4 = (r4 == c4).astype(f32)
    rs = lax.broadcasted_iota(jnp.int32, (DQK, CH), 0) & (CH - 1)
    tril4 = rs >= lax.broadcasted_iota(jnp.int32, (DQK, CH), 1)
    return tril, tril4, masks, eye4


def _stack(xv, masks):
    return jnp.concatenate([jnp.where(m, xv, 0.0) for m in masks], axis=0)


def _unstack(rv, masks):
    out = jnp.where(masks[0], rv[0:CH, :], 0.0)
    for h in range(1, NH):
        out = out + jnp.where(masks[h], rv[h * CH:(h + 1) * CH, :], 0.0)
    return out


def _vstack(xv):
    return jnp.concatenate([xv[:, h * DV:(h + 1) * DV] for h in range(NH)], axis=0)


def _vunstack(xv):
    return jnp.concatenate([xv[h * CH:(h + 1) * CH, :] for h in range(NH)], axis=1)


def _gla_chunk_fwd(lac, qc, kc, vc, s_all, tril, masks, tril4):
    lmat = tril.astype(f32)
    bc = jnp.dot(lmat, lac, preferred_element_type=f32, precision=HIGHEST)
    bend = bc[CH - 1:CH, :]
    eb = jnp.exp(bc)
    enb = jnp.exp(-bc)
    ed = jnp.exp(bend - bc)
    qh = qc * (DK ** -0.5)
    qf = qh * eb
    qn = qh * enb
    kn = kc * enb
    kp = kc * eb
    kd = kc * ed
    qf_s = _stack(qf, masks).astype(bf16)
    qn_s = _stack(qn, masks).astype(bf16)
    kn_b = kn.astype(bf16)
    kp_b = kp.astype(bf16)
    attf = _dg(qf_s, kn_b, NT)
    attb = _dg(qn_s, kp_b, NT)
    a_s = jnp.where(tril4, attf, attb)
    a_b = a_s.astype(bf16)
    v_b = vc.astype(bf16)
    intra = jnp.concatenate(
        [_dot(a_b[h * CH:(h + 1) * CH, :], v_b[:, h * DV:(h + 1) * DV]) for h in range(NH)], axis=0)
    o_s = intra + _dot(qf_s, s_all.astype(bf16))
    return dict(bc=bc, bend=bend, eb=eb, enb=enb, ed=ed, qf=qf, qn=qn, kn=kn, kp=kp, kd=kd,
                qf_s=qf_s, qn_s=qn_s, kn_b=kn_b, kp_b=kp_b, a_b=a_b, v_b=v_b, o_s=o_s)


def _col_from_row(row, eye4):
    return jnp.sum(eye4 * row, axis=1, keepdims=True)


def _row_from_col(col, eye4):
    return jnp.sum(eye4 * col, axis=0, keepdims=True)


def gla_fwd(z, la, gn_s, tg, name):
    S = z.shape[0]
    nc = tg // CH

    def body(q_ref, k_ref, v_ref, r_ref, la_ref, gn_ref, yg_ref, sp_ref, st):
        @pl.when(pl.program_id(0) == 0)
        def _():
            st[...] = jnp.zeros_like(st)

        tril, tril4, masks, eye4 = _gla_consts()

        def chunk(c, carry):
            r0 = pl.multiple_of(c * CH, CH)
            s0 = pl.multiple_of(c * DQK, DQK)
            s_all = st[...]
            sp_ref[pl.ds(s0, DQK), :] = s_all
            vc = v_ref[pl.ds(r0, CH), :]
            t = _gla_chunk_fwd(la_ref[pl.ds(r0, CH), :], q_ref[pl.ds(r0, CH), :], k_ref[pl.ds(r0, CH), :], vc,
                               s_all, tril, masks, tril4)
            u_all = _dg(_stack(t["kd"], masks).astype(bf16), _vstack(vc).astype(bf16), TN)
            st[...] = _col_from_row(jnp.exp(t["bend"]), eye4) * s_all + u_all
            o_s = t["o_s"]
            on = o_s * lax.rsqrt(jnp.mean(o_s * o_s, axis=-1, keepdims=True) + EPS) * gn_ref[...]
            rc = r_ref[pl.ds(r0, CH), :]
            yg_ref[pl.ds(r0, CH), :] = (_vunstack(on) * (rc * _sigmoid(rc))).astype(bf16)
            return carry

        lax.fori_loop(0, nc, chunk, 0)

    return pl.pallas_call(
        body, name=name,
        grid=(S // tg,),
        in_specs=[pl.BlockSpec((tg, DQK), lambda i: (i, 4)), pl.BlockSpec((tg, DQK), lambda i: (i, 5)),
                  pl.BlockSpec((tg, DG), lambda i: (i, 3)), pl.BlockSpec((tg, DG), lambda i: (i, 4)),
                  pl.BlockSpec((tg, DQK), lambda i: (i, 0)), _full(gn_s.shape)],
        out_specs=[pl.BlockSpec((tg, DG), lambda i: (i, 0)), pl.BlockSpec((nc * DQK, DV), lambda i: (i, 0))],
        out_shape=[jax.ShapeDtypeStruct((S, DG), bf16), jax.ShapeDtypeStruct((S // CH * DQK, DV), f32)],
        scratch_shapes=[pltpu.VMEM((DQK, DV), f32)],
        compiler_params=_cp(1),
    )(z, z, z, z, la, gn_s)


def gla_bwd(z, la, sprev, dyg, gn_s, tg, name, comm=None):
    S = z.shape[0]
    nc = tg // CH
    nt = S // tg

    def body(q_ref, k_ref, v_ref, r_ref, la_ref, sp_ref, dy_ref, gn_ref,
             dq_ref, dk_ref, dv_ref, dr_ref, dpre_ref, redg_ref, redb_ref, gs):
        @pl.when(pl.program_id(0) == 0)
        def _():
            gs[...] = jnp.zeros_like(gs)
            redg_ref[...] = jnp.zeros_like(redg_ref)
            redb_ref[...] = jnp.zeros_like(redb_ref)

        tril, tril4, masks, eye4 = _gla_consts()
        umat = (lax.broadcasted_iota(jnp.int32, (CH, CH), 0) <= lax.broadcasted_iota(jnp.int32, (CH, CH), 1)).astype(f32)
        last_row = lax.broadcasted_iota(jnp.int32, (CH, DQK), 0) == CH - 1

        def chunk(tt, carry):
            c = nc - 1 - tt
            r0 = pl.multiple_of(c * CH, CH)
            s0 = pl.multiple_of(c * DQK, DQK)
            s_all = sp_ref[pl.ds(s0, DQK), :]
            lac = la_ref[pl.ds(r0, CH), :]
            vc = v_ref[pl.ds(r0, CH), :]
            rc = r_ref[pl.ds(r0, CH), :]
            t = _gla_chunk_fwd(lac, q_ref[pl.ds(r0, CH), :], k_ref[pl.ds(r0, CH), :], vc, s_all, tril, masks, tril4)
            g_all = gs[...]
            g_b = g_all.astype(bf16)
            s_b = s_all.astype(bf16)
            o_s = t["o_s"]
            rstd = lax.rsqrt(jnp.mean(o_s * o_s, axis=-1, keepdims=True) + EPS)
            oh = o_s * rstd
            gnv = gn_ref[...]
            sr = _sigmoid(rc)
            dyv = dy_ref[pl.ds(r0, CH), :]
            dr_ref[pl.ds(r0, CH), :] = dyv * _vunstack(oh * gnv) * (sr * (1.0 + rc * (1.0 - sr)))
            don = _vstack(dyv * (rc * sr))
            redg_ref[...] += don * oh
            doh = don * gnv
            do_s = rstd * (doh - oh * jnp.mean(doh * oh, axis=-1, keepdims=True))
            do_b = do_s.astype(bf16)
            v_b = t["v_b"]
            vst_b = _vstack(vc).astype(bf16)
            kd_s = _stack(t["kd"], masks).astype(bf16)
            da_s = jnp.concatenate(
                [_dg(do_b[h * CH:(h + 1) * CH, :], v_b[:, h * DV:(h + 1) * DV], NT) for h in range(NH)], axis=0)
            a_b = t["a_b"]
            dv_s = jnp.concatenate(
                [_dg(a_b[h * CH:(h + 1) * CH, :], do_b[h * CH:(h + 1) * CH, :], TN) for h in range(NH)], axis=0)
            dv_s = dv_s + _dot(kd_s, g_b)
            dv_ref[pl.ds(r0, CH), :] = _vunstack(dv_s)
            gend = jnp.exp(t["bend"])
            gcol = _col_from_row(gend, eye4)
            gs[...] = gcol * g_all + _dg(t["qf_s"], do_b, TN)
            dgcol = jnp.sum(g_all * s_all, axis=1, keepdims=True)
            dbend = _row_from_col(dgcol * gcol, eye4)
            dkd = _unstack(_dg(vst_b, g_b, NT), masks)
            daf = jnp.where(tril4, da_s, 0.0).astype(bf16)
            dab = jnp.where(tril4, 0.0, da_s).astype(bf16)
            dqf = _unstack(_dot(daf, t["kn_b"]) + _dg(do_b, s_b, NT), masks)
            dqn = _unstack(_dot(dab, t["kp_b"]), masks)
            dkn = _dg(daf, t["qf_s"], TN)
            dkp = _dg(dab, t["qn_s"], TN)
            dq_ref[pl.ds(r0, CH), :] = (dqf * t["eb"] + dqn * t["enb"]) * (DK ** -0.5)
            dk_ref[pl.ds(r0, CH), :] = dkn * t["enb"] + dkp * t["eb"] + dkd * t["ed"]
            dkd_kd = dkd * t["kd"]
            dbc = dqf * t["qf"] - dqn * t["qn"] - dkn * t["kn"] + dkp * t["kp"] - dkd_kd
            dbc = dbc + jnp.where(last_row, _rowsum(dkd_kd) + dbend, 0.0)
            dla = jnp.dot(umat, dbc, preferred_element_type=f32, precision=HIGHEST)
            dpre = dla * (1.0 / TAU) * (1.0 - jnp.exp(TAU * lac))
            dpre_ref[pl.ds(r0, CH), :] = dpre
            redb_ref[...] += dpre
            return carry

        lax.fori_loop(0, nc, chunk, 0)

    rev = lambda col: (lambda i: (nt - 1 - i, col))
    return _pcall(
        body, (z, z, z, z, la, sprev, dyg, gn_s), name=name, comm=comm,
        grid=(nt,),
        in_specs=[pl.BlockSpec((tg, DQK), rev(4)), pl.BlockSpec((tg, DQK), rev(5)),
                  pl.BlockSpec((tg, DG), rev(3)), pl.BlockSpec((tg, DG), rev(4)),
                  pl.BlockSpec((tg, DQK), rev(0)), pl.BlockSpec((nc * DQK, DV), rev(0)),
                  pl.BlockSpec((tg, DG), rev(0)), _full(gn_s.shape)],
        out_specs=[pl.BlockSpec((tg, DQK), rev(0)), pl.BlockSpec((tg, DQK), rev(0)),
                   pl.BlockSpec((tg, DG), rev(0)), pl.BlockSpec((tg, DG), rev(0)), pl.BlockSpec((tg, DQK), rev(0)),
                   _full((DQK, DV)), _full((CH, DQK))],
        out_shape=[jax.ShapeDtypeStruct((S, DQK), f32), jax.ShapeDtypeStruct((S, DQK), f32),
                   jax.ShapeDtypeStruct((S, DG), f32), jax.ShapeDtypeStruct((S, DG), f32), jax.ShapeDtypeStruct((S, DQK), f32),
                   jax.ShapeDtypeStruct((DQK, DV), f32), jax.ShapeDtypeStruct((CH, DQK), f32)],
        scratch_shapes=[pltpu.VMEM((DQK, DV), f32)],
    )


def mixout_fwd(x1, yc, yg, mod, wout, tm, name):
    S = x1.shape[0]

    def body(x_ref, yc_ref, yg_ref, mod_ref, w_ref, xo_ref):
        mixo = _dot(yc_ref[...], w_ref[0:DC, :]) + _dot(yg_ref[...], w_ref[DC:DC + DG, :])
        xo_ref[...] = x_ref[...] + mod_ref[5:6, :] * mixo

    row = lambda i: (i, 0)
    return pl.pallas_call(
        body, name=name,
        grid=(S // tm,),
        in_specs=[pl.BlockSpec((tm, D), row), pl.BlockSpec((tm, DC), row), pl.BlockSpec((tm, DG), row),
                  _full(mod.shape), _full(wout.shape)],
        out_specs=pl.BlockSpec((tm, D), row),
        out_shape=jax.ShapeDtypeStruct((S, D), f32),
        compiler_params=_cp(1),
    )(x1, yc, yg, mod, wout)


def mixout_bwd(dx2, yc, yg, mod, wout, tm, name):
    S = dx2.shape[0]

    def body(dx_ref, yc_ref, yg_ref, mod_ref, w_ref, dm_ref, dyc_ref, dyg_ref, red_ref):
        @pl.when(pl.program_id(0) == 0)
        def _():
            red_ref[...] = jnp.zeros_like(red_ref)

        dxv = dx_ref[...]
        mixo = _dot(yc_ref[...], w_ref[0:DC, :]) + _dot(yg_ref[...], w_ref[DC:DC + DG, :])
        red_ref[0:1, :] += _rowsum(dxv * mixo)
        dm = (mod_ref[5:6, :] * dxv).astype(bf16)
        dm_ref[...] = dm
        dycat = _dg(dm, w_ref[...], NT)
        dyc_ref[...] = dycat[:, :DC]
        dyg_ref[...] = dycat[:, DC:]

    row = lambda i: (i, 0)
    return pl.pallas_call(
        body, name=name,
        grid=(S // tm,),
        in_specs=[pl.BlockSpec((tm, D), row), pl.BlockSpec((tm, DC), row), pl.BlockSpec((tm, DG), row),
                  _full(mod.shape), _full(wout.shape)],
        out_specs=[pl.BlockSpec((tm, D), row), pl.BlockSpec((tm, DC), row), pl.BlockSpec((tm, DG), row), _full((8, D))],
        out_shape=[jax.ShapeDtypeStruct((S, D), bf16), jax.ShapeDtypeStruct((S, DC), f32),
                   jax.ShapeDtypeStruct((S, DG), f32), jax.ShapeDtypeStruct((8, D), f32)],
        compiler_params=_cp(1),
    )(dx2, yc, yg, mod, wout)


def final_fwd_bwd(x, tgt, fmod, g, tm, name):
    S = x.shape[0]

    def body(x_ref, t_ref, fm_ref, g_ref, dx_ref, red_ref):
        @pl.when(pl.program_id(0) == 0)
        def _():
            red_ref[...] = jnp.zeros_like(red_ref)

        xh, rstd = _rms_parts(x_ref[...])
        gv = g_ref[...]
        n = xh * gv
        sc = 1.0 + fm_ref[1:2, :]
        e = n * sc + fm_ref[0:1, :] - t_ref[...]
        red_ref[0:1, :] += _rowsum(e * e) * (0.5 / D)
        dy = e * (1.0 / D)
        dn = dy * sc
        red_ref[1:2, :] += _rowsum(dy)
        red_ref[2:3, :] += _rowsum(dy * n)
        red_ref[3:4, :] += _rowsum(dn * xh)
        dx_ref[...] = _rms_bwd(dn * gv, xh, rstd)

    row = lambda i: (i, 0)
    return pl.pallas_call(
        body, name=name,
        grid=(S // tm,),
        in_specs=[pl.BlockSpec((tm, D), row), pl.BlockSpec((tm, D), row), _full(fmod.shape), _full(g.shape)],
        out_specs=[pl.BlockSpec((tm, D), row), _full((8, D))],
        out_shape=[jax.ShapeDtypeStruct((S, D), f32), jax.ShapeDtypeStruct((8, D), f32)],
        compiler_params=_cp(1),
    )(x, tgt, fmod, g)


def ada_fwd(c_all, w, b, name):
    n = w.shape[1]

    def body(c_ref, w_ref, b_ref, o_ref):
        cv = c_ref[...]
        o_ref[...] = jnp.dot(cv * _sigmoid(cv), w_ref[...], preferred_element_type=f32, precision=HIGHEST) + b_ref[...]

    return pl.pallas_call(
        body, name=name,
        in_specs=[_full(c_all.shape), _full(w.shape), _full(b.shape)],
        out_specs=_full((N_DEV, n)),
        out_shape=jax.ShapeDtypeStruct((N_DEV, n), f32),
        grid=(1,),
        compiler_params=_cp(1),
    )(c_all, w, b)


def ada_wgrad(c_all_t, dm, name):
    n = dm.shape[1]

    def body(c_ref, d_ref, o_ref):
        cv = c_ref[...]
        o_ref[...] = jnp.dot(cv * _sigmoid(cv), d_ref[...], preferred_element_type=f32, precision=HIGHEST)

    return pl.pallas_call(
        body, name=name,
        in_specs=[_full(c_all_t.shape), _full(dm.shape)],
        out_specs=_full((D, n)),
        out_shape=jax.ShapeDtypeStruct((D, n), f32),
        grid=(1,),
        compiler_params=_cp(1),
    )(c_all_t, dm)


def _adam_math(gv, wv, mv, vv):
    m = ADAM_B1 * mv + (1.0 - ADAM_B1) * gv
    v = ADAM_B2 * vv + (1.0 - ADAM_B2) * (gv * gv)
    m_hat = m / (1.0 - ADAM_B1 ** ADAM_STEP)
    v_hat = v / (1.0 - ADAM_B2 ** ADAM_STEP)
    delta = -ADAM_LR * (m_hat / (jnp.sqrt(v_hat) + ADAM_EPS) + ADAM_WD * wv)
    return delta, m, v


def adam_parts(parts, w, m, v, tr, name, comm=None):
    L, R, C = w.shape
    nt = R // tr

    def body(*refs):
        p_refs = refs[:L]
        w_ref, m_ref, v_ref, g_ref, d_ref, mo_ref, vo_ref = refs[L:]
        lyr = pl.program_id(0)
        for l in range(L):
            @pl.when(lyr == l)
            def _(p_ref=p_refs[l]):
                gv = p_ref[0].astype(f32)
                for k in range(1, N_DEV):
                    gv = gv + p_ref[k].astype(f32)
                g_ref[...] = gv
                d_ref[...], mo_ref[...], vo_ref[...] = _adam_math(gv, w_ref[...], m_ref[...], v_ref[...])

    def part_spec(l):
        return pl.BlockSpec((N_DEV, tr, C), lambda lyr, i: (0, jnp.where(lyr == l, i, jnp.where(lyr < l, 0, nt - 1)), 0))

    spec = pl.BlockSpec((None, tr, C), lambda lyr, i: (lyr, i, 0))
    shp = jax.ShapeDtypeStruct((L, R, C), f32)
    return _pcall(
        body, (*parts, w, m, v), name=name, comm=comm,
        grid=(L, nt),
        in_specs=[part_spec(l) for l in range(L)] + [spec, spec, spec],
        out_specs=[spec, spec, spec, spec],
        out_shape=[shp, shp, shp, shp],
    )


def adam_plain(gr, w, m, v, tr, name):
    R, C = w.shape

    def body(g_ref, w_ref, m_ref, v_ref, d_ref, mo_ref, vo_ref):
        d_ref[...], mo_ref[...], vo_ref[...] = _adam_math(g_ref[...], w_ref[...], m_ref[...], v_ref[...])

    spec = pl.BlockSpec((tr, C), lambda i: (i, 0))
    shp = jax.ShapeDtypeStruct((R, C), f32)
    return pl.pallas_call(
        body, name=name,
        grid=(R // tr,),
        in_specs=[spec, spec, spec, spec],
        out_specs=[spec, spec, spec],
        out_shape=[shp, shp, shp],
        compiler_params=_cp(1),
    )(gr, w, m, v)


def sum8(parts, name):
    _, R, C = parts.shape

    def body(p_ref, o_ref):
        acc = p_ref[0]
        for k in range(1, N_DEV):
            acc = acc + p_ref[k]
        o_ref[...] = acc

    return pl.pallas_call(
        body, name=name,
        grid=(1,),
        in_specs=[_full(parts.shape)],
        out_specs=_full((R, C)),
        out_shape=jax.ShapeDtypeStruct((R, C), f32),
        compiler_params=_cp(1),
    )(parts)


def _place():
    return lax.axis_index("x"), lax.axis_index("y"), lax.axis_index("c")


def _gather_steps(ins, outs, send_sems, recv_sems, local_sems, place):
    n = len(ins)
    x, y, c = place
    me, sibling = (x, y, c), (x, y, 1 - c)
    chips = [(1 - x, y), (x, 1 - y), (1 - x, 1 - y)]

    def slot(a, p):
        return outs[a].at[4 * p[0] + 2 * p[1] + p[2]]

    def copy(a, k, block, to, src=None):
        return pltpu.make_async_remote_copy(
            src_ref=slot(a, block) if src is None else src, dst_ref=slot(a, block),
            send_sem=send_sems.at[a * 7 + k], recv_sem=recv_sems.at[a * 7 + k],
            device_id=to, device_id_type=MESH)

    def mine():
        return [pltpu.make_async_copy(ins[a], slot(a, me), local_sems.at[a]) for a in range(n)]

    def first():
        cps = []
        for a in range(n):
            cps.append(copy(a, 0, me, sibling, src=ins[a]))
            cps += [copy(a, 1 + j, me, (*chip, c), src=ins[a]) for j, chip in enumerate(chips)]
        return cps

    def start():
        for cp in mine() + first():
            cp.start()

    def forward():
        for j, chip in enumerate(chips):
            for a in range(n):
                copy(a, 1 + j, (*chip, c), me).wait_recv()
                copy(a, 4 + j, (*chip, c), sibling).start()

    def finish():
        for a in range(n):
            copy(a, 0, sibling, me).wait_recv()
            for j, chip in enumerate(chips):
                copy(a, 4 + j, (*chip, 1 - c), me).wait_recv()
        for cp in first() + [copy(a, 4 + j, (*chip, c), sibling) for j, chip in enumerate(chips) for a in range(n)]:
            cp.wait_send()
        for cp in mine():
            cp.wait()

    return start, forward, finish


def _exchange_steps(ins, outs, send_sems, recv_sems, local_sems, place):
    n = len(ins)
    x, y, c = place
    me_i = 4 * x + 2 * y + c

    def mine():
        return [pltpu.make_async_copy(ins[a].at[me_i], outs[a].at[me_i], local_sems.at[a]) for a in range(n)]

    def copies(receiving):
        cps = []
        for k in range(1, N_DEV):
            px = 1 - x if (k >> 2) & 1 else x
            py = 1 - y if (k >> 1) & 1 else y
            pc = 1 - c if k & 1 else c
            p_i = 4 * px + 2 * py + pc
            for a in range(n):
                sem = a * 7 + k - 1
                cps.append(pltpu.make_async_remote_copy(
                    src_ref=ins[a].at[p_i], dst_ref=outs[a].at[p_i if receiving else me_i],
                    send_sem=send_sems.at[sem], recv_sem=recv_sems.at[sem],
                    device_id=(px, py, pc), device_id_type=MESH))
        return cps

    def start():
        for cp in mine() + copies(False):
            cp.start()

    def finish():
        for cp in copies(True):
            cp.wait_recv()
        for cp in copies(False):
            cp.wait_send()
        for cp in mine():
            cp.wait()

    return start, None, finish


_COMM_STEPS = {"gather": _gather_steps, "exchange": _exchange_steps}


def _comm_out_shapes(kind, arrs):
    if kind == "gather":
        return [jax.ShapeDtypeStruct((N_DEV,) + a.shape, a.dtype) for a in arrs]
    return [jax.ShapeDtypeStruct(a.shape, a.dtype) for a in arrs]


def _comm_sems(n):
    return [pltpu.SemaphoreType.DMA((7 * n,)), pltpu.SemaphoreType.DMA((7 * n,)), pltpu.SemaphoreType.DMA((n,))]


def _pcall(body, args, *, name, grid, in_specs, out_specs, out_shape, scratch_shapes=(), comm=None):
    in_specs, out_specs, out_shape = list(in_specs), list(out_specs), list(out_shape)
    scratch_shapes = list(scratch_shapes)
    cparams = _cp(len(grid))
    if comm is None:
        outs = pl.pallas_call(body, name=name, grid=grid, in_specs=in_specs, out_specs=out_specs, out_shape=out_shape,
                              scratch_shapes=scratch_shapes, compiler_params=cparams)(*args)
        return list(outs), []
    kind, arrs = comm
    nc, n_in, n_out, n_scr = len(arrs), len(in_specs), len(out_specs), len(scratch_shapes)
    total = 1
    for gdim in grid:
        total *= gdim
    forward_step = (total * 3) // 4

    def hosted(*refs):
        core_in, c_in = refs[:n_in], refs[n_in:n_in + nc]
        core_out = refs[n_in + nc:n_in + nc + n_out]
        c_out = refs[n_in + nc + n_out:n_in + 2 * nc + n_out]
        rest = refs[n_in + 2 * nc + n_out:]
        step = pl.program_id(0)
        for ax in range(1, len(grid)):
            step = step * grid[ax] + pl.program_id(ax)
        start, forward, finish = _COMM_STEPS[kind](c_in, c_out, *rest[n_scr:], _place())
        pl.when(step == 0)(start)
        if forward is not None:
            pl.when(step == forward_step)(forward)
        body(*core_in, *core_out, *rest[:n_scr])
        pl.when(step == total - 1)(finish)

    any_spec = pl.BlockSpec(memory_space=pl.ANY)
    outs = pl.pallas_call(
        hosted, name=name, grid=grid,
        in_specs=in_specs + [any_spec] * nc,
        out_specs=out_specs + [any_spec] * nc,
        out_shape=out_shape + _comm_out_shapes(kind, arrs),
        scratch_shapes=scratch_shapes + _comm_sems(nc),
        compiler_params=cparams)(*args, *arrs)
    return list(outs[:n_out]), list(outs[n_out:])


def _comm_call(kind, arrs, name):
    n = len(arrs)

    def body(*refs):
        start, forward, finish = _COMM_STEPS[kind](refs[:n], refs[n:2 * n], *refs[2 * n:], _place())
        start()
        if forward is not None:
            forward()
        finish()

    any_spec = pl.BlockSpec(memory_space=pl.ANY)
    return pl.pallas_call(
        body, name=name,
        in_specs=[any_spec] * n, out_specs=[any_spec] * n,
        out_shape=_comm_out_shapes(kind, arrs), scratch_shapes=_comm_sems(n),
    )(*arrs)


def all_gather(arrs, name):
    return _comm_call("gather", arrs, name)


def all_to_all(arrs, name):
    return _comm_call("exchange", arrs, name)


def _tiles(S):
    t = min(512, S)
    return dict(ffn=min(256, S), row=t, conv=t, gla=t, bk=min(1024, S), bk_ffn=t)


BIG = ("wi1", "wo1", "win", "wout", "wi2", "wo2")


def _col_shards_to_full(gathered):
    n, r, c = gathered.shape
    return jnp.transpose(gathered, (1, 0, 2)).reshape(r, n * c)


def _ffn_weights_t(wi, wo):
    return jnp.transpose(wi, (0, 2, 1)), jnp.transpose(wo.reshape(NFS, FS, D), (0, 2, 1))


def _win_full(win_a):
    return jnp.pad(_col_shards_to_full(win_a), ((0, 0), (0, DINP - DIN)))


def train_pass(x, tgt, mods, fmod, sh, ws, wi1_first, wo1_first):
    S = x.shape[0]
    T = _tiles(S)
    bk = T["bk"]
    full = [dict() for _ in range(DEPTH)]
    full[0]["wi1"], full[0]["wo1"] = wi1_first, wo1_first.reshape(F, D)
    saved = []
    xc = x
    for l in range(DEPTH):
        w, fw = ws[f"L{l}"], full[l]
        x0 = xc
        names = ("win", "wout", "wi2", "wo2") if l == 0 else ("wi2", "wo2")
        (x1, h1f, zg1, zu1, f1), got = ffn_fwd(x0, mods[l], w["g1"], fw["wi1"], fw["wo1"], (0, 1, 2), T["ffn"], f"ffn1_fwd_{l}",
                                          comm=("gather", [sh[n][l] for n in names]))
        fw.update(zip(names, got))
        if l == 0:
            fw["win"], fw["wout"] = _win_full(fw["win"]), fw["wout"].reshape(D, D)
        fw["wo2"] = fw["wo2"].reshape(F, D)
        z, la = mixin_fwd(x1, mods[l], w["g2"], fw["win"], w["wgu"], w["bgate"], T["row"], f"mixin_fwd_{l}")
        y, yc = conv_fwd(z, w["wdw"], w["cpar"], T["conv"], f"conv_fwd_{l}")
        yg, sprev = gla_fwd(z, la, w["gn_s"], T["gla"], f"gla_fwd_{l}")
        x2 = mixout_fwd(x1, yc, yg, mods[l], fw["wout"], T["row"], f"mixout_fwd_{l}")
        names = ("wi1", "wo1", "win", "wout") if l + 1 < DEPTH else ()
        (x3, h2f, zg2, zu2, f2), got = ffn_fwd(x2, mods[l], w["g3"], fw["wi2"], fw["wo2"], (6, 7, 8), T["ffn"], f"ffn2_fwd_{l}",
                                          comm=("gather", [sh[n][l + 1] for n in names]) if names else None)
        if names:
            nx = full[l + 1]
            nx["wi1"], nx["wo1"], nx["win"], nx["wout"] = got[0], got[1].reshape(F, D), _win_full(got[2]), got[3].reshape(D, D)
        saved.append(dict(x0=x0, x1=x1, x2=x2, h1f=h1f, zg1=zg1, zu1=zu1, f1=f1, h2f=h2f, zg2=zg2, zu2=zu2, f2=f2,
                          z=z, la=la, y=y, yc=yc, yg=yg, sprev=sprev))
        xc = x3

    dx, redf = final_fwd_bwd(xc, tgt, fmod, ws["gf"], T["row"], "loss_head")
    loss_lanes = redf[0]
    dfmod = redf[1:3]
    grads = {"gf": redf[3]}
    dmods = [None] * DEPTH
    recv = {n: [None] * DEPTH for n in BIG}

    def ffn_backward(xin, dy, h, zg, zu, fo, gain, wi, wo, rows, l, tag, ride=None):
        wi_t, wo_t = _ffn_weights_t(wi, wo)
        (df, a, dzg, dzu), got_ride = ffn_bwd_hidden(dy, zg, zu, mods[l], wo_t, rows[2], T["ffn"], f"{tag}_bwd_hidden_{l}",
                                                     comm=("exchange", ride) if ride else None)
        p_wo, _ = dwo_pieces(a, df, T["bk_ffn"], f"d{tag}_wo_{l}")
        p_wi, (r_wo,) = dwi_pieces(h, dzg, dzu, T["bk_ffn"], f"d{tag}_wi_{l}",
                                   comm=("exchange", [p_wo.reshape(N_DEV, F // N_DEV, D)]))
        (dxin, red), (r_wi,) = ffn_bwd_input(xin, dy, dzg, dzu, fo, mods[l], gain, wi_t, rows, T["ffn"],
                                             f"{tag}_bwd_input_{l}", comm=("exchange", [p_wi]))
        return dxin, red, r_wi, r_wo, got_ride

    for l in reversed(range(DEPTH)):
        w, fw, sv = ws[f"L{l}"], full[l], saved[l]
        g = {}
        dx2, red3, recv["wi2"][l], recv["wo2"][l], _ = ffn_backward(
            sv["x2"], dx, sv["h2f"], sv["zg2"], sv["zu2"], sv["f2"], w["g3"], fw["wi2"], fw["wo2"], (6, 7, 8), l, "ffn2")
        dmix, dyc, dyg, red_o = mixout_bwd(dx2, sv["yc"], sv["yg"], mods[l], fw["wout"], T["row"], f"mixout_bwd_{l}")
        p_wout = jnp.concatenate([matmul_tn(sv["yc"], dmix, DC, D, DC, D, bk, f"dwout_c_{l}", out_dtype=bf16),
                                  matmul_tn(sv["yg"], dmix, DG, D, DG, D, bk, f"dwout_g_{l}", out_dtype=bf16)], axis=0)
        (dq, dk, dv, dr, dpre, redg, redb), (recv["wout"][l],) = gla_bwd(
            sv["z"], sv["la"], sv["sprev"], dyg, w["gn_s"], T["gla"], f"gla_bwd_{l}",
            comm=("exchange", [p_wout.reshape(N_DEV, D // N_DEV, D)]))
        (dzab, redc), _ = conv_bwd(sv["z"], sv["y"], dyc, w["wdw"], w["cpar"], T["conv"], f"conv_bwd_{l}")
        (dx1, h2, dz, red2), _ = mixin_bwd(sv["x1"], dx2, dzab, dq, dk, dv, dr, dpre, mods[l], w["g2"], fw["win"], w["wgu"],
                                           T["row"], f"mixin_bwd_{l}")
        dwin = matmul_tn(h2, dz, D, DINP, D, DINP, bk, f"dwin_{l}", out_dtype=bf16)[:, :DIN]
        p_win = jnp.transpose(dwin.reshape(D, N_DEV, DIN // N_DEV), (1, 0, 2))
        g["wgu"] = matmul_tn(sv["z"], dpre, 128, DQK, 128, DQK, bk, f"dwgu_{l}", a_col_block=(DINP - 128) // 128)[:GR]
        g["bgate"] = jnp.sum(redb, axis=0)
        g["gn"] = jnp.sum(redg.reshape(NH, CH, DV), axis=1)
        g["wdw"] = redc[:CW]
        g["bdw"], g["gln"], g["bln"] = redc[32], redc[33], redc[34]
        dx0, red1, recv["wi1"][l], recv["wo1"][l], (recv["win"][l],) = ffn_backward(
            sv["x0"], dx1, sv["h1f"], sv["zg1"], sv["zu1"], sv["f1"], w["g1"], fw["wi1"], fw["wo1"], (0, 1, 2), l, "ffn1",
            ride=[p_win])
        g["g1"], g["g2"], g["g3"] = red1[3], red2[2], red3[3]
        dmods[l] = jnp.stack([red1[0], red1[1], red1[2], red2[0], red2[1], red_o[0], red3[0], red3[1], red3[2]], axis=0)
        grads[f"L{l}"] = g
        dx = dx0
    return loss_lanes, dx, grads, dmods, dfmod, recv


def _pad_rows(a, rows):
    return jnp.pad(a, ((0, rows - a.shape[0]), (0, 0)))


def kernel(x, c, w_ada, b_ada, g_norm_ffn1, w_ffn1_in, w_ffn1_out, g_norm_mix, w_in, w_dw, b_dw, g_conv_ln, b_conv_ln, w_gate_up, b_gate, g_gla_norm, w_out, g_norm_ffn2, w_ffn2_in, w_ffn2_out, g_norm_final, w_ada_final, b_ada_final, loss_target, m_w_ada, m_b_ada, m_g_norm_ffn1, m_w_ffn1_in, m_w_ffn1_out, m_g_norm_mix, m_w_in, m_w_dw, m_b_dw, m_g_conv_ln, m_b_conv_ln, m_w_gate_up, m_b_gate, m_g_gla_norm, m_w_out, m_g_norm_ffn2, m_w_ffn2_in, m_w_ffn2_out, m_g_norm_final, m_w_ada_final, m_b_ada_final, v_w_ada, v_b_ada, v_g_norm_ffn1, v_w_ffn1_in, v_w_ffn1_out, v_g_norm_mix, v_w_in, v_w_dw, v_b_dw, v_g_conv_ln, v_b_conv_ln, v_w_gate_up, v_b_gate, v_g_gla_norm, v_w_out, v_g_norm_ffn2, v_w_ffn2_in, v_w_ffn2_out, v_g_norm_final, v_w_ada_final, v_b_ada_final):
    me = 4 * lax.axis_index("x") + 2 * lax.axis_index("y") + lax.axis_index("c")
    L = DEPTH
    n_ada = N_MOD * D // N_DEV
    n_fin = 2 * D // N_DEV

    small = jnp.concatenate([c.reshape(-1), w_dw.reshape(-1), w_gate_up.reshape(-1)])
    n_small = small.shape[0]
    small = jnp.pad(small, (0, 8 * D - n_small)).reshape(8, D)
    big = dict(wi1=w_ffn1_in, wo1=w_ffn1_out, win=w_in, wout=w_out, wi2=w_ffn2_in, wo2=w_ffn2_out)
    sh = {n: [a[l].astype(bf16) for l in range(L)] for n, a in big.items()}
    small_a, wi1_first, wo1_first = all_gather([small, sh["wi1"][0], sh["wo1"][0]], "gather_first")
    small_a = small_a.reshape(N_DEV, 8 * D)
    c_all = small_a[:, :D]
    o1 = D + L * CW * (DC // N_DEV)
    wdw_full = _col_shards_to_full(small_a[:, D:o1].reshape(N_DEV, L * CW, DC // N_DEV)).reshape(L, CW, DC)
    wgu_full = _col_shards_to_full(small_a[:, o1:o1 + L * GR * (DQK // N_DEV)].reshape(N_DEV, L * GR, DQK // N_DEV)).reshape(L, GR, DQK)

    b_ada_mine = lax.dynamic_slice(b_ada, (0, me * n_ada), (L, n_ada))
    b_fin_mine = lax.dynamic_slice(b_ada_final, (me * n_fin,), (n_fin,))
    parts = [ada_fwd(c_all, w_ada[l], b_ada_mine[l:l + 1], f"ada_fwd_{l}") for l in range(L)]
    parts.append(ada_fwd(c_all, w_ada_final, b_fin_mine.reshape(1, n_fin), "ada_fwd_final"))
    modsrc = jnp.concatenate(parts, axis=1)
    n_row = modsrc.shape[1]
    modsrc = jnp.pad(modsrc, ((0, 0), (0, 24 * 128 - n_row))).reshape(N_DEV, 24, 128)
    (modrecv,) = all_to_all([modsrc], "exchange_mod")
    modrecv = modrecv.reshape(N_DEV, 24 * 128)
    mods = []
    for l in range(L):
        mvec = modrecv[:, l * n_ada:(l + 1) * n_ada].reshape(N_MOD, D)
        mods.append(_pad_rows(mvec, 16))
    fmod = _pad_rows(modrecv[:, L * n_ada:L * n_ada + n_fin].reshape(2, D), 8)

    ws = {"gf": g_norm_final.reshape(1, D)}
    for l in range(L):
        ws[f"L{l}"] = dict(
            g1=g_norm_ffn1[l].reshape(1, D), g2=g_norm_mix[l].reshape(1, D), g3=g_norm_ffn2[l].reshape(1, D),
            wgu=_pad_rows(wgu_full[l], 128).astype(bf16),
            bgate=b_gate[l].reshape(1, DQK),
            wdw=_pad_rows(wdw_full[l], 32),
            cpar=_pad_rows(jnp.stack([b_dw[l], g_conv_ln[l], b_conv_ln[l]]), 8),
            gn_s=jnp.repeat(g_gla_norm[l], CH, axis=0),
        )

    loss_lanes, grad_x, gr, dmods, dfmod, recv = train_pass(
        x[0], loss_target[0], mods, fmod, sh, ws, wi1_first, wo1_first)

    def adam_big(rv, w, m, v, name):
        R = w.shape[1]
        tr = 256 if R % 256 == 0 else R // 2
        return adam_parts(rv, w, m, v, tr, name)

    res = {}
    res["w_ffn2_in"], _ = adam_big(recv["wi2"], w_ffn2_in, m_w_ffn2_in, v_w_ffn2_in, "adam_ffn2_in")
    res["w_ffn2_out"], _ = adam_big(recv["wo2"], w_ffn2_out, m_w_ffn2_out, v_w_ffn2_out, "adam_ffn2_out")
    res["w_in"], _ = adam_big(recv["win"], w_in, m_w_in, v_w_in, "adam_w_in")
    res["w_out"], _ = adam_big(recv["wout"], w_out, m_w_out, v_w_out, "adam_w_out")
    res["w_ffn1_out"], _ = adam_big(recv["wo1"], w_ffn1_out, m_w_ffn1_out, v_w_ffn1_out, "adam_ffn1_out")
    res["w_ffn1_in"], _ = adam_big(recv["wi1"], w_ffn1_in, m_w_ffn1_in, v_w_ffn1_in, "adam_ffn1_in")

    flat = lambda name: jnp.stack([gr[f"L{l}"][name] for l in range(L)]).reshape(-1)
    sections = [
        ("b_ada", jnp.stack(dmods).reshape(-1)), ("b_ada_final", dfmod.reshape(-1)),
        ("g_norm_ffn1", flat("g1")), ("g_norm_mix", flat("g2")), ("g_norm_ffn2", flat("g3")), ("g_norm_final", gr["gf"]),
        ("b_dw", flat("bdw")), ("g_conv_ln", flat("gln")), ("b_conv_ln", flat("bln")), ("b_gate", flat("bgate")),
        ("g_gla_norm", flat("gn")),
    ]
    n_rep = sum(s[1].shape[0] for s in sections)
    rep_rows = -(-n_rep // D)
    extra = [("loss", loss_lanes), ("w_dw", flat("wdw")), ("w_gate_up", flat("wgu"))]
    pack = jnp.concatenate([s[1] for s in sections] + [jnp.zeros((rep_rows * D - n_rep,), f32)] + [s[1] for s in extra])
    n_pack = pack.shape[0]
    pack_rows = -(-n_pack // (8 * D)) * 8
    pack = jnp.pad(pack, (0, pack_rows * D - n_pack)).reshape(pack_rows, D)
    (pack_all,) = all_gather([pack], "gather_small_grads")
    tot = sum8(pack_all, "sum_small_grads")
    tot_flat = tot.reshape(-1)
    loss = jnp.sum(tot_flat[rep_rows * D:rep_rows * D + D])
    o_dw = rep_rows * D + D
    g_wdw_full = tot_flat[o_dw:o_dw + L * CW * DC].reshape(L, CW, DC)
    o_gu = o_dw + L * CW * DC
    g_wgu_full = tot_flat[o_gu:o_gu + L * GR * DQK].reshape(L, GR, DQK)

    small_params = dict(b_ada=(b_ada, m_b_ada, v_b_ada), b_ada_final=(b_ada_final, m_b_ada_final, v_b_ada_final),
                        g_norm_ffn1=(g_norm_ffn1, m_g_norm_ffn1, v_g_norm_ffn1), g_norm_mix=(g_norm_mix, m_g_norm_mix, v_g_norm_mix),
                        g_norm_ffn2=(g_norm_ffn2, m_g_norm_ffn2, v_g_norm_ffn2), g_norm_final=(g_norm_final, m_g_norm_final, v_g_norm_final),
                        b_dw=(b_dw, m_b_dw, v_b_dw), g_conv_ln=(g_conv_ln, m_g_conv_ln, v_g_conv_ln),
                        b_conv_ln=(b_conv_ln, m_b_conv_ln, v_b_conv_ln), b_gate=(b_gate, m_b_gate, v_b_gate),
                        g_gla_norm=(g_gla_norm, m_g_gla_norm, v_g_gla_norm))

    def rep_pack(idx):
        p = jnp.concatenate([small_params[s[0]][idx].reshape(-1) for s in sections])
        return jnp.pad(p, (0, rep_rows * D - n_rep)).reshape(rep_rows, D)

    g_rep = tot[:rep_rows]
    d_rep, m_rep, v_rep = adam_plain(g_rep, rep_pack(0), rep_pack(1), rep_pack(2), rep_rows, "adam_small")
    off = 0
    for sname, sval in sections:
        shp = small_params[sname][0].shape
        nel = sval.shape[0]
        res[sname] = [a.reshape(-1)[off:off + nel].reshape(shp) for a in (g_rep, d_rep, m_rep, v_rep)]
        off += nel

    def adam_cols(g_full, w, m, v, name):
        shp = w.shape
        g_mine = lax.dynamic_slice(g_full, (0, 0, me * shp[2]), shp)
        R, C = shp[0] * shp[1], shp[2]
        outs = adam_plain(g_mine.reshape(R, C), w.reshape(R, C), m.reshape(R, C), v.reshape(R, C), R, name)
        return [g_mine] + [o.reshape(shp) for o in outs]

    res["w_dw"] = adam_cols(g_wdw_full, w_dw, m_w_dw, v_w_dw, "adam_w_dw")
    res["w_gate_up"] = adam_cols(g_wgu_full, w_gate_up, m_w_gate_up, v_w_gate_up, "adam_w_gate_up")

    c_all_t = c_all.T
    dmod_all = pack_all.reshape(N_DEV, -1)[:, :L * N_MOD * D].reshape(N_DEV, L, N_MOD * D)
    dfm_all = pack_all.reshape(N_DEV, -1)[:, L * N_MOD * D:L * N_MOD * D + 2 * D]
    dm_mine = lax.dynamic_slice(dmod_all, (0, 0, me * n_ada), (N_DEV, L, n_ada))
    dfm_mine = lax.dynamic_slice(dfm_all, (0, me * n_fin), (N_DEV, n_fin))
    g_w_ada = jnp.stack([ada_wgrad(c_all_t, dm_mine[:, l], f"ada_wgrad_{l}") for l in range(L)])
    g_w_fin = ada_wgrad(c_all_t, dfm_mine, "ada_wgrad_final")
    outs = adam_plain(g_w_ada.reshape(L * D, n_ada), w_ada.reshape(L * D, n_ada), m_w_ada.reshape(L * D, n_ada),
                      v_w_ada.reshape(L * D, n_ada), 256, "adam_w_ada")
    res["w_ada"] = [g_w_ada] + [o.reshape(w_ada.shape) for o in outs]
    res["w_ada_final"] = [g_w_fin] + list(adam_plain(g_w_fin, w_ada_final, m_w_ada_final, v_w_ada_final, 256, "adam_w_ada_final"))

    order = ["w_ada", "b_ada", "g_norm_ffn1", "w_ffn1_in", "w_ffn1_out", "g_norm_mix", "w_in", "w_dw", "b_dw", "g_conv_ln",
             "b_conv_ln", "w_gate_up", "b_gate", "g_gla_norm", "w_out", "g_norm_ffn2", "w_ffn2_in", "w_ffn2_out",
             "g_norm_final", "w_ada_final", "b_ada_final"]
    out = [loss, grad_x[None]]
    for k in range(4):
        out += [res[name][k] for name in order]
    return tuple(out)
```

```python
import functools

import jax
import jax.numpy as jnp
from jax import lax
from jax.experimental import pallas as pl
from jax.experimental.pallas import tpu as pltpu

f32 = jnp.float32
bf16 = jnp.bfloat16

N_DEV = 8
DEPTH = 2
D = 1024
F = 2816
DC = 512
NH = 4
DK = 64
DV = 128
DQK = NH * DK
DG = NH * DV
CH = 64
CW = 31
GR = 16
TAU = 16.0
N_MOD = 9
DIN = 2 * DC + 2 * DQK + 2 * DG + GR
DINP = 2688
EPS = 1e-6
HALO = 32
SUBLANES = 8
CONV_ROWS = 32
NFS = 4
FS = F // NFS

ADAM_LR = 0.001
ADAM_B1 = 0.9
ADAM_B2 = 0.999
ADAM_EPS = 1e-08
ADAM_WD = 0.01
ADAM_STEP = 10

V7X_VMEM_LIMIT = 56 * 1024 * 1024
MESH = pl.DeviceIdType.MESH
HIGHEST = lax.Precision.HIGHEST

NT = (((1,), (1,)), ((), ()))
TN = (((0,), (0,)), ((), ()))


def _cp(n_axes):
    return pltpu.CompilerParams(dimension_semantics=("arbitrary",) * n_axes, vmem_limit_bytes=V7X_VMEM_LIMIT)


def _full(shape):
    nd = len(shape)
    return pl.BlockSpec(shape, lambda *_: (0,) * nd)


def _resident(shape):
    nd = len(shape)
    return pl.BlockSpec(shape, lambda *_: (0,) * nd, pipeline_mode=pl.Buffered(1))


def _dot(a, b):
    return jnp.dot(a, b, preferred_element_type=f32)


def _dg(a, b, dims):
    return lax.dot_general(a, b, dims, preferred_element_type=f32)


def _sigmoid(x):
    return jax.nn.sigmoid(x)


def _rowsum(x):
    return jnp.sum(x, axis=0, keepdims=True)


def _rms_parts(xv):
    rstd = lax.rsqrt(jnp.mean(xv * xv, axis=-1, keepdims=True) + EPS)
    return xv * rstd, rstd


def _rms_bwd(dxh, xh, rstd):
    return rstd * (dxh - xh * jnp.mean(dxh * xh, axis=-1, keepdims=True))


def ffn_fwd(x, mod, g, wi, wo, rows, tm, name, comm=None):
    S = x.shape[0]
    r_shift, r_scale, r_gate = rows

    def body(x_ref, mod_ref, g_ref, wi_ref, wo_ref, xo_ref, h_ref, zg_ref, zu_ref, f_ref):
        xv = x_ref[...]
        xh, _ = _rms_parts(xv)
        h = (xh * g_ref[...] * (1.0 + mod_ref[r_scale:r_scale + 1, :]) + mod_ref[r_shift:r_shift + 1, :]).astype(bf16)
        h_ref[...] = h
        fv = None
        for j in range(NFS):
            zg = _dot(h, wi_ref[j])
            zu = _dot(h, wi_ref[j + NFS])
            zg_ref[j] = zg.astype(bf16)
            zu_ref[j] = zu.astype(bf16)
            a = zg * _sigmoid(zg) * zu
            part = _dot(a.astype(bf16), wo_ref[j * FS:(j + 1) * FS, :])
            fv = part if fv is None else fv + part
        f_ref[...] = fv.astype(bf16)
        xo_ref[...] = xv + 0.5 * mod_ref[r_gate:r_gate + 1, :] * fv

    row = lambda i: (i, 0)
    tile = lambda i: (0, i, 0)
    shard = jax.ShapeDtypeStruct((NFS, S, FS), bf16)
    return _pcall(
        body, (x, mod, g, wi, wo), name=name, comm=comm,
        grid=(S // tm,),
        in_specs=[pl.BlockSpec((tm, D), row), _full(mod.shape), _full(g.shape), _resident(wi.shape), _resident(wo.shape)],
        out_specs=[pl.BlockSpec((tm, D), row), pl.BlockSpec((tm, D), row),
                   pl.BlockSpec((NFS, tm, FS), tile), pl.BlockSpec((NFS, tm, FS), tile), pl.BlockSpec((tm, D), row)],
        out_shape=[jax.ShapeDtypeStruct((S, D), f32), jax.ShapeDtypeStruct((S, D), bf16), shard, shard,
                   jax.ShapeDtypeStruct((S, D), bf16)],
    )


def ffn_bwd_hidden(dy, zg, zu, mod, wo_t, r_gate, tm, name, comm=None):
    S = dy.shape[0]

    def body(dy_ref, zg_ref, zu_ref, mod_ref, wo_ref, df_ref, a_ref, dzg_ref, dzu_ref):
        df = (0.5 * mod_ref[r_gate:r_gate + 1, :] * dy_ref[...]).astype(bf16)
        df_ref[...] = df
        for j in range(NFS):
            zgv = zg_ref[j].astype(f32)
            zuv = zu_ref[j].astype(f32)
            s = _sigmoid(zgv)
            sil = zgv * s
            a_ref[j] = (sil * zuv).astype(bf16)
            da = _dg(df, wo_ref[j * FS:(j + 1) * FS, :], NT)
            dzu_ref[j] = (da * sil).astype(bf16)
            dzg_ref[j] = (da * zuv * (s * (1.0 + zgv * (1.0 - s)))).astype(bf16)

    row = lambda i: (i, 0)
    tile = lambda i: (0, i, 0)
    shard = jax.ShapeDtypeStruct((NFS, S, FS), bf16)
    tspec = pl.BlockSpec((NFS, tm, FS), tile)
    return _pcall(
        body, (dy, zg, zu, mod, wo_t), name=name, comm=comm,
        grid=(S // tm,),
        in_specs=[pl.BlockSpec((tm, D), row), tspec, tspec, _full(mod.shape), _resident(wo_t.shape)],
        out_specs=[pl.BlockSpec((tm, D), row), tspec, tspec, tspec],
        out_shape=[jax.ShapeDtypeStruct((S, D), bf16), shard, shard, shard],
    )


def ffn_bwd_input(x, dy, dzg, dzu, fo, mod, g, wi_t, rows, tm, name, comm=None):
    S = x.shape[0]
    r_shift, r_scale, r_gate = rows

    def body(x_ref, dy_ref, dzg_ref, dzu_ref, f_ref, mod_ref, g_ref, wi_ref, dx_ref, red_ref):
        @pl.when(pl.program_id(0) == 0)
        def _():
            red_ref[...] = jnp.zeros_like(red_ref)

        dh = _dg(dzg_ref[0], wi_ref[0], NT) + _dg(dzu_ref[0], wi_ref[NFS], NT)
        for j in range(1, NFS):
            dh = dh + _dg(dzg_ref[j], wi_ref[j], NT) + _dg(dzu_ref[j], wi_ref[j + NFS], NT)
        dyv = dy_ref[...]
        xh, rstd = _rms_parts(x_ref[...])
        gv = g_ref[...]
        n = xh * gv
        dn = dh * (1.0 + mod_ref[r_scale:r_scale + 1, :])
        red_ref[0:1, :] += _rowsum(dh)
        red_ref[1:2, :] += _rowsum(dh * n)
        red_ref[2:3, :] += _rowsum(0.5 * f_ref[...].astype(f32) * dyv)
        red_ref[3:4, :] += _rowsum(dn * xh)
        dx_ref[...] = dyv + _rms_bwd(dn * gv, xh, rstd)

    row = lambda i: (i, 0)
    tspec = pl.BlockSpec((NFS, tm, FS), lambda i: (0, i, 0))
    return _pcall(
        body, (x, dy, dzg, dzu, fo, mod, g, wi_t), name=name, comm=comm,
        grid=(S // tm,),
        in_specs=[pl.BlockSpec((tm, D), row), pl.BlockSpec((tm, D), row), tspec, tspec, pl.BlockSpec((tm, D), row),
                  _full(mod.shape), _full(g.shape), _resident(wi_t.shape)],
        out_specs=[pl.BlockSpec((tm, D), row), _full((8, D))],
        out_shape=[jax.ShapeDtypeStruct((S, D), f32), jax.ShapeDtypeStruct((8, D), f32)],
    )


def matmul_tn(a, b, M, N, bm, bn, bk, name, a_col_block=0, out_dtype=f32):
    S = b.shape[0]
    nk = S // bk

    def body(a_ref, b_ref, o_ref, acc_s):
        k = pl.program_id(2)

        @pl.when(k == 0)
        def _():
            acc_s[...] = jnp.zeros_like(acc_s)

        acc_s[...] += _dg(a_ref[...].astype(bf16), b_ref[...].astype(bf16), TN)

        @pl.when(k == nk - 1)
        def _():
            o_ref[...] = acc_s[...].astype(out_dtype)

    return pl.pallas_call(
        body, name=name,
        grid=(M // bm, N // bn, nk),
        in_specs=[
            pl.BlockSpec((bk, bm), lambda i, j, k: (k, i + a_col_block)),
            pl.BlockSpec((bk, bn), lambda i, j, k: (k, j)),
        ],
        out_specs=pl.BlockSpec((bm, bn), lambda i, j, k: (i, j)),
        out_shape=jax.ShapeDtypeStruct((M, N), out_dtype),
        scratch_shapes=[pltpu.VMEM((bm, bn), f32)],
        compiler_params=_cp(3),
    )(a, b)


def dwi_pieces(h, dzg, dzu, bk, name, comm=None):
    S = h.shape[0]
    nk = S // bk

    def body(h_ref, g_ref, u_ref, o_ref, acc_s):
        half = pl.program_id(0)
        k = pl.program_id(1)

        @pl.when(k == 0)
        def _():
            acc_s[...] = jnp.zeros_like(acc_s)

        hv = h_ref[...]

        @pl.when(half == 0)
        def _():
            for j in range(NFS):
                acc_s[j] += _dg(hv, g_ref[j], TN)

        @pl.when(half == 1)
        def _():
            for j in range(NFS):
                acc_s[j] += _dg(hv, u_ref[j], TN)

        @pl.when(k == nk - 1)
        def _():
            o_ref[...] = acc_s[...].astype(bf16)

    (out,), comm_outs = _pcall(
        body, (h, dzg, dzu), name=name, comm=comm,
        grid=(2, nk),
        in_specs=[
            pl.BlockSpec((bk, D), lambda half, k: (k, 0)),
            pl.BlockSpec((NFS, bk, FS), lambda half, k: (0, jnp.where(half == 0, k, nk - 1), 0)),
            pl.BlockSpec((NFS, bk, FS), lambda half, k: (0, jnp.where(half == 1, k, 0), 0)),
        ],
        out_specs=[pl.BlockSpec((NFS, D, FS), lambda half, k: (half, 0, 0))],
        out_shape=[jax.ShapeDtypeStruct((2 * NFS, D, FS), bf16)],
        scratch_shapes=[pltpu.VMEM((NFS, D, FS), f32)],
    )
    return out, comm_outs


def dwo_pieces(a, df, bk, name, comm=None):
    S = df.shape[0]
    nk = S // bk

    def body(a_ref, d_ref, o_ref, acc_s):
        k = pl.program_id(0)

        @pl.when(k == 0)
        def _():
            acc_s[...] = jnp.zeros_like(acc_s)

        dv = d_ref[...]
        for j in range(NFS):
            acc_s[j] += _dg(a_ref[j], dv, TN)

        @pl.when(k == nk - 1)
        def _():
            o_ref[...] = acc_s[...].astype(bf16)

    (out,), comm_outs = _pcall(
        body, (a, df), name=name, comm=comm,
        grid=(nk,),
        in_specs=[pl.BlockSpec((NFS, bk, FS), lambda k: (0, k, 0)), pl.BlockSpec((bk, D), lambda k: (k, 0))],
        out_specs=[_full((NFS, FS, D))],
        out_shape=[jax.ShapeDtypeStruct((NFS, FS, D), bf16)],
        scratch_shapes=[pltpu.VMEM((NFS, FS, D), f32)],
    )
    return out.reshape(F, D), comm_outs


def mixin_fwd(x1, mod, g, win, wgu, bgate, tm, name):
    S = x1.shape[0]

    def body(x_ref, mod_ref, g_ref, win_ref, wgu_ref, bg_ref, z_ref, la_ref):
        xh, _ = _rms_parts(x_ref[...])
        hv = xh * g_ref[...] * (1.0 + mod_ref[4:5, :]) + mod_ref[3:4, :]
        z = _dot(hv.astype(bf16), win_ref[...])
        z_ref[...] = z
        glr = z[:, DINP - 128:]
        pre = _dot(glr.astype(bf16), wgu_ref[...]) + bg_ref[...]
        la_ref[...] = (jnp.minimum(pre, 0.0) - jnp.log(1.0 + jnp.exp(-jnp.abs(pre)))) * (1.0 / TAU)

    return pl.pallas_call(
        body, name=name,
        grid=(S // tm,),
        in_specs=[pl.BlockSpec((tm, D), lambda i: (i, 0)), _full(mod.shape), _full(g.shape),
                  _full(win.shape), _full(wgu.shape), _full(bgate.shape)],
        out_specs=[pl.BlockSpec((tm, DINP), lambda i: (i, 0)), pl.BlockSpec((tm, DQK), lambda i: (i, 0))],
        out_shape=[jax.ShapeDtypeStruct((S, DINP), f32), jax.ShapeDtypeStruct((S, DQK), f32)],
        compiler_params=_cp(1),
    )(x1, mod, g, win, wgu, bgate)


def mixin_bwd(x1, dres, dzab, dq, dk, dv, dr, dpre, mod, g, win, wgu, tm, name, comm=None):
    S = x1.shape[0]

    def body(x_ref, dres_ref, dzab_ref, dq_ref, dk_ref, dv_ref, dr_ref, dpre_ref, mod_ref, g_ref, win_ref, wgu_ref,
             dx_ref, h_ref, dz_ref, red_ref):
        @pl.when(pl.program_id(0) == 0)
        def _():
            red_ref[...] = jnp.zeros_like(red_ref)

        dglr = _dg(dpre_ref[...].astype(bf16), wgu_ref[...], NT)
        dz = jnp.concatenate([dzab_ref[...], dq_ref[...], dk_ref[...], dv_ref[...], dr_ref[...], dglr], axis=1).astype(bf16)
        dz_ref[...] = dz
        dh = _dg(dz, win_ref[...], NT)
        xh, rstd = _rms_parts(x_ref[...])
        gv = g_ref[...]
        n = xh * gv
        sc = 1.0 + mod_ref[4:5, :]
        h_ref[...] = (n * sc + mod_ref[3:4, :]).astype(bf16)
        dn = dh * sc
        red_ref[0:1, :] += _rowsum(dh)
        red_ref[1:2, :] += _rowsum(dh * n)
        red_ref[2:3, :] += _rowsum(dn * xh)
        dx_ref[...] = dres_ref[...] + _rms_bwd(dn * gv, xh, rstd)

    row = lambda i: (i, 0)
    return _pcall(
        body, (x1, dres, dzab, dq, dk, dv, dr, dpre, mod, g, win, wgu), name=name, comm=comm,
        grid=(S // tm,),
        in_specs=[pl.BlockSpec((tm, D), row), pl.BlockSpec((tm, D), row),
                  pl.BlockSpec((tm, 2 * DC), row), pl.BlockSpec((tm, DQK), row), pl.BlockSpec((tm, DQK), row),
                  pl.BlockSpec((tm, DG), row), pl.BlockSpec((tm, DG), row), pl.BlockSpec((tm, DQK), row),
                  _full(mod.shape), _full(g.shape), _full(win.shape), _full(wgu.shape)],
        out_specs=[pl.BlockSpec((tm, D), row), pl.BlockSpec((tm, D), row), pl.BlockSpec((tm, DINP), row), _full((8, D))],
        out_shape=[jax.ShapeDtypeStruct((S, D), f32), jax.ShapeDtypeStruct((S, D), bf16),
                   jax.ShapeDtypeStruct((S, DINP), bf16), jax.ShapeDtypeStruct((8, D), f32)],
    )


def _glu(zab):
    return zab[:, :DC] * _sigmoid(zab[:, DC:])


def _shift_copies(src_s, dst_s, tc):
    n = tc + HALO - SUBLANES
    for b in range(1, SUBLANES):
        dst_s[b, 0:n, :] = src_s[b:b + n, :]


def _shifted(src_s, dst_s, o, tc):
    b = o % SUBLANES
    a = o - b
    return src_s[a:a + tc, :] if b == 0 else dst_s[b, a:a + tc, :]


def conv_fwd(z, wdw, cpar, tc, name):
    S = z.shape[0]
    nb = tc // HALO

    def body(zc_ref, zp_ref, w_ref, cp_ref, y_ref, yc_ref, u_s, us_s):
        i = pl.program_id(0)
        up = _glu(zp_ref[...])
        u_s[0:HALO, :] = jnp.where(i > 0, up, 0.0)
        u_s[HALO:HALO + tc, :] = _glu(zc_ref[...])
        _shift_copies(u_s, us_s, tc)
        for r in range(0, tc, CONV_ROWS):
            acc = _shifted(u_s, us_s, HALO - (CW - 1) + r, CONV_ROWS) * w_ref[0:1, :]
            for w in range(1, CW):
                acc = acc + _shifted(u_s, us_s, HALO - (CW - 1) + w + r, CONV_ROWS) * w_ref[w:w + 1, :]
            y = acc + cp_ref[0:1, :]
            y_ref[r:r + CONV_ROWS, :] = y
            yc = y - jnp.mean(y, axis=-1, keepdims=True)
            yl = yc * lax.rsqrt(jnp.mean(yc * yc, axis=-1, keepdims=True) + EPS) * cp_ref[1:2, :] + cp_ref[2:3, :]
            yc_ref[r:r + CONV_ROWS, :] = (yl * _sigmoid(yl)).astype(bf16)

    return pl.pallas_call(
        body, name=name,
        grid=(S // tc,),
        in_specs=[pl.BlockSpec((tc, 2 * DC), lambda i: (i, 0)),
                  pl.BlockSpec((HALO, 2 * DC), lambda i: (jnp.maximum(i * nb - 1, 0), 0)),
                  _full(wdw.shape), _full(cpar.shape)],
        out_specs=[pl.BlockSpec((tc, DC), lambda i: (i, 0)), pl.BlockSpec((tc, DC), lambda i: (i, 0))],
        out_shape=[jax.ShapeDtypeStruct((S, DC), f32), jax.ShapeDtypeStruct((S, DC), bf16)],
        scratch_shapes=[pltpu.VMEM((HALO + tc, DC), f32), pltpu.VMEM((SUBLANES, HALO + tc, DC), f32)],
        compiler_params=_cp(1),
    )(z, z, wdw, cpar)


def conv_bwd(z, y, dyc, wdw, cpar, tc, name, comm=None):
    S = z.shape[0]
    nb = tc // HALO
    nt = S // tc
    last_halo = S // HALO - 1

    def body(zc_ref, zp_ref, y_ref, yn_ref, d_ref, dn_ref, w_ref, cp_ref, dz_ref, red_ref, u_s, dy_s, us_s, dys_s):
        i = pl.program_id(0)

        @pl.when(i == 0)
        def _():
            red_ref[...] = jnp.zeros_like(red_ref)

        gl = cp_ref[1:2, :]
        bl = cp_ref[2:3, :]

        def ln_bwd(yv, dv):
            yc = yv - jnp.mean(yv, axis=-1, keepdims=True)
            rstd = lax.rsqrt(jnp.mean(yc * yc, axis=-1, keepdims=True) + EPS)
            yh = yc * rstd
            yl = yh * gl + bl
            s = _sigmoid(yl)
            dyl = dv * (s * (1.0 + yl * (1.0 - s)))
            dyh = dyl * gl
            dyv = rstd * (dyh - jnp.mean(dyh, axis=-1, keepdims=True) - yh * jnp.mean(dyh * yh, axis=-1, keepdims=True))
            return dyv, dyl, yh

        dy_c, dyl_c, yh_c = ln_bwd(y_ref[...], d_ref[...])
        dy_n, _, _ = ln_bwd(yn_ref[...], dn_ref[...])
        dy_s[0:tc, :] = dy_c
        dy_s[tc:tc + HALO, :] = jnp.where(i < nt - 1, dy_n, 0.0)
        u_s[0:HALO, :] = jnp.where(i > 0, _glu(zp_ref[...]), 0.0)
        u_s[HALO:HALO + tc, :] = _glu(zc_ref[...])
        _shift_copies(u_s, us_s, tc)
        _shift_copies(dy_s, dys_s, tc)
        red_ref[32:33, :] += _rowsum(dy_c)
        red_ref[33:34, :] += _rowsum(dyl_c * yh_c)
        red_ref[34:35, :] += _rowsum(dyl_c)
        for r in range(0, tc, CONV_ROWS):
            du = _shifted(dy_s, dys_s, CW - 1 + r, CONV_ROWS) * w_ref[0:1, :]
            for w in range(1, CW):
                du = du + _shifted(dy_s, dys_s, CW - 1 - w + r, CONV_ROWS) * w_ref[w:w + 1, :]
            zc = zc_ref[r:r + CONV_ROWS, :]
            av = zc[:, :DC]
            sb = _sigmoid(zc[:, DC:])
            dz_ref[r:r + CONV_ROWS, :] = jnp.concatenate([du * sb, du * av * sb * (1.0 - sb)], axis=1)
        for w in range(CW):
            part = None
            for r in range(0, tc, CONV_ROWS):
                prod = _shifted(u_s, us_s, HALO - (CW - 1) + w + r, CONV_ROWS) * dy_s[r:r + CONV_ROWS, :]
                fold = jnp.sum(prod.reshape(CONV_ROWS // SUBLANES, SUBLANES, DC), axis=0)
                part = fold if part is None else part + fold
            red_ref[w:w + 1, :] += _rowsum(part)

    cur = lambda i: (i, 0)
    nxt = lambda i: (jnp.minimum((i + 1) * nb, last_halo), 0)
    return _pcall(
        body, (z, z, y, y, dyc, dyc, wdw, cpar), name=name, comm=comm,
        grid=(nt,),
        in_specs=[pl.BlockSpec((tc, 2 * DC), cur),
                  pl.BlockSpec((HALO, 2 * DC), lambda i: (jnp.maximum(i * nb - 1, 0), 0)),
                  pl.BlockSpec((tc, DC), cur), pl.BlockSpec((HALO, DC), nxt),
                  pl.BlockSpec((tc, DC), cur), pl.BlockSpec((HALO, DC), nxt),
                  _full(wdw.shape), _full(cpar.shape)],
        out_specs=[pl.BlockSpec((tc, 2 * DC), cur), _full((40, DC))],
        out_shape=[jax.ShapeDtypeStruct((S, 2 * DC), f32), jax.ShapeDtypeStruct((40, DC), f32)],
        scratch_shapes=[pltpu.VMEM((HALO + tc, DC), f32), pltpu.VMEM((tc + HALO, DC), f32),
                        pltpu.VMEM((SUBLANES, HALO + tc, DC), f32), pltpu.VMEM((SUBLANES, HALO + tc, DC), f32)],
    )


def _gla_consts():
    r = lax.broadcasted_iota(jnp.int32, (CH, CH), 0)
    c = lax.broadcasted_iota(jnp.int32, (CH, CH), 1)
    tril = r >= c
    lane = lax.broadcasted_iota(jnp.int32, (CH, DQK), 1)
    masks = [(lane >= h * DK) & (lane < (h + 1) * DK) for h in range(NH)]
    r4 = lax.broadcasted_iota(jnp.int32, (DQK, DQK), 0)
    c4 = lax.broadcasted_iota(jnp.int32, (DQK, DQK), 1)
    eye4 = (r4 == c4).astype(f32)
    rs = lax.broadcasted_iota(jnp.int32, (DQK, CH), 0) & (CH - 1)
    tril4 = rs >= lax.broadcasted_iota(jnp.int32, (DQK, CH), 1)
    return tril, tril4, masks, eye4


def _stack(xv, masks):
    return jnp.concatenate([jnp.where(m, xv, 0.0) for m in masks], axis=0)


def _unstack(rv, masks):
    out = jnp.where(masks[0], rv[0:CH, :], 0.0)
    for h in range(1, NH):
        out = out + jnp.where(masks[h], rv[h * CH:(h + 1) * CH, :], 0.0)
    return out


def _vstack(xv):
    return jnp.concatenate([xv[:, h * DV:(h + 1) * DV] for h in range(NH)], axis=0)


def _vunstack(xv):
    return jnp.concatenate([xv[h * CH:(h + 1) * CH, :] for h in range(NH)], axis=1)


def _gla_chunk_fwd(lac, qc, kc, vc, s_all, tril, masks, tril4):
    lmat = tril.astype(f32)
    bc = jnp.dot(lmat, lac, preferred_element_type=f32, precision=HIGHEST)
    bend = bc[CH - 1:CH, :]
    eb = jnp.exp(bc)
    enb = jnp.exp(-bc)
    ed = jnp.exp(bend - bc)
    qh = qc * (DK ** -0.5)
    qf = qh * eb
    qn = qh * enb
    kn = kc * enb
    kp = kc * eb
    kd = kc * ed
    qf_s = _stack(qf, masks).astype(bf16)
    qn_s = _stack(qn, masks).astype(bf16)
    kn_b = kn.astype(bf16)
    kp_b = kp.astype(bf16)
    attf = _dg(qf_s, kn_b, NT)
    attb = _dg(qn_s, kp_b, NT)
    a_s = jnp.where(tril4, attf, attb)
    a_b = a_s.astype(bf16)
    v_b = vc.astype(bf16)
    intra = jnp.concatenate(
        [_dot(a_b[h * CH:(h + 1) * CH, :], v_b[:, h * DV:(h + 1) * DV]) for h in range(NH)], axis=0)
    o_s = intra + _dot(qf_s, s_all.astype(bf16))
    return dict(bc=bc, bend=bend, eb=eb, enb=enb, ed=ed, qf=qf, qn=qn, kn=kn, kp=kp, kd=kd,
                qf_s=qf_s, qn_s=qn_s, kn_b=kn_b, kp_b=kp_b, a_b=a_b, v_b=v_b, o_s=o_s)


def _col_from_row(row, eye4):
    return jnp.sum(eye4 * row, axis=1, keepdims=True)


def _row_from_col(col, eye4):
    return jnp.sum(eye4 * col, axis=0, keepdims=True)


def gla_fwd(z, la, gn_s, tg, name):
    S = z.shape[0]
    nc = tg // CH

    def body(q_ref, k_ref, v_ref, r_ref, la_ref, gn_ref, yg_ref, sp_ref, st):
        @pl.when(pl.program_id(0) == 0)
        def _():
            st[...] = jnp.zeros_like(st)

        tril, tril4, masks, eye4 = _gla_consts()

        def chunk(c, carry):
            r0 = pl.multiple_of(c * CH, CH)
            s0 = pl.multiple_of(c * DQK, DQK)
            s_all = st[...]
            sp_ref[pl.ds(s0, DQK), :] = s_all
            vc = v_ref[pl.ds(r0, CH), :]
            t = _gla_chunk_fwd(la_ref[pl.ds(r0, CH), :], q_ref[pl.ds(r0, CH), :], k_ref[pl.ds(r0, CH), :], vc,
                               s_all, tril, masks, tril4)
            u_all = _dg(_stack(t["kd"], masks).astype(bf16), _vstack(vc).astype(bf16), TN)
            st[...] = _col_from_row(jnp.exp(t["bend"]), eye4) * s_all + u_all
            o_s = t["o_s"]
            on = o_s * lax.rsqrt(jnp.mean(o_s * o_s, axis=-1, keepdims=True) + EPS) * gn_ref[...]
            rc = r_ref[pl.ds(r0, CH), :]
            yg_ref[pl.ds(r0, CH), :] = (_vunstack(on) * (rc * _sigmoid(rc))).astype(bf16)
            return carry

        lax.fori_loop(0, nc, chunk, 0, unroll=True)

    return pl.pallas_call(
        body, name=name,
        grid=(S // tg,),
        in_specs=[pl.BlockSpec((tg, DQK), lambda i: (i, 4)), pl.BlockSpec((tg, DQK), lambda i: (i, 5)),
                  pl.BlockSpec((tg, DG), lambda i: (i, 3)), pl.BlockSpec((tg, DG), lambda i: (i, 4)),
                  pl.BlockSpec((tg, DQK), lambda i: (i, 0)), _full(gn_s.shape)],
        out_specs=[pl.BlockSpec((tg, DG), lambda i: (i, 0)), pl.BlockSpec((nc * DQK, DV), lambda i: (i, 0))],
        out_shape=[jax.ShapeDtypeStruct((S, DG), bf16), jax.ShapeDtypeStruct((S // CH * DQK, DV), f32)],
        scratch_shapes=[pltpu.VMEM((DQK, DV), f32)],
        compiler_params=_cp(1),
    )(z, z, z, z, la, gn_s)


def gla_bwd(z, la, sprev, dyg, gn_s, tg, name, comm=None):
    S = z.shape[0]
    nc = tg // CH
    nt = S // tg

    def body(q_ref, k_ref, v_ref, r_ref, la_ref, sp_ref, dy_ref, gn_ref,
             dq_ref, dk_ref, dv_ref, dr_ref, dpre_ref, redg_ref, redb_ref, gs):
        @pl.when(pl.program_id(0) == 0)
        def _():
            gs[...] = jnp.zeros_like(gs)
            redg_ref[...] = jnp.zeros_like(redg_ref)
            redb_ref[...] = jnp.zeros_like(redb_ref)

        tril, tril4, masks, eye4 = _gla_consts()
        umat = (lax.broadcasted_iota(jnp.int32, (CH, CH), 0) <= lax.broadcasted_iota(jnp.int32, (CH, CH), 1)).astype(f32)
        last_row = lax.broadcasted_iota(jnp.int32, (CH, DQK), 0) == CH - 1

        def chunk(tt, carry):
            c = nc - 1 - tt
            r0 = pl.multiple_of(c * CH, CH)
            s0 = pl.multiple_of(c * DQK, DQK)
            s_all = sp_ref[pl.ds(s0, DQK), :]
            lac = la_ref[pl.ds(r0, CH), :]
            vc = v_ref[pl.ds(r0, CH), :]
            rc = r_ref[pl.ds(r0, CH), :]
            t = _gla_chunk_fwd(lac, q_ref[pl.ds(r0, CH), :], k_ref[pl.ds(r0, CH), :], vc, s_all, tril, masks, tril4)
            g_all = gs[...]
            g_b = g_all.astype(bf16)
            s_b = s_all.astype(bf16)
            o_s = t["o_s"]
            rstd = lax.rsqrt(jnp.mean(o_s * o_s, axis=-1, keepdims=True) + EPS)
            oh = o_s * rstd
            gnv = gn_ref[...]
            sr = _sigmoid(rc)
            dyv = dy_ref[pl.ds(r0, CH), :]
            dr_ref[pl.ds(r0, CH), :] = dyv * _vunstack(oh * gnv) * (sr * (1.0 + rc * (1.0 - sr)))
            don = _vstack(dyv * (rc * sr))
            redg_ref[...] += don * oh
            doh = don * gnv
            do_s = rstd * (doh - oh * jnp.mean(doh * oh, axis=-1, keepdims=True))
            do_b = do_s.astype(bf16)
            v_b = t["v_b"]
            vst_b = _vstack(vc).astype(bf16)
            kd_s = _stack(t["kd"], masks).astype(bf16)
            da_s = jnp.concatenate(
                [_dg(do_b[h * CH:(h + 1) * CH, :], v_b[:, h * DV:(h + 1) * DV], NT) for h in range(NH)], axis=0)
            a_b = t["a_b"]
            dv_s = jnp.concatenate(
                [_dg(a_b[h * CH:(h + 1) * CH, :], do_b[h * CH:(h + 1) * CH, :], TN) for h in range(NH)], axis=0)
            dv_s = dv_s + _dot(kd_s, g_b)
            dv_ref[pl.ds(r0, CH), :] = _vunstack(dv_s)
            gend = jnp.exp(t["bend"])
            gcol = _col_from_row(gend, eye4)
            gs[...] = gcol * g_all + _dg(t["qf_s"], do_b, TN)
            dgcol = jnp.sum(g_all * s_all, axis=1, keepdims=True)
            dbend = _row_from_col(dgcol * gcol, eye4)
            dkd = _unstack(_dg(vst_b, g_b, NT), masks)
            daf = jnp.where(tril4, da_s, 0.0).astype(bf16)
            dab = jnp.where(tril4, 0.0, da_s).astype(bf16)
            dqf = _unstack(_dot(daf, t["kn_b"]) + _dg(do_b, s_b, NT), masks)
            dqn = _unstack(_dot(dab, t["kp_b"]), masks)
            dkn = _dg(daf, t["qf_s"], TN)
            dkp = _dg(dab, t["qn_s"], TN)
            dq_ref[pl.ds(r0, CH), :] = (dqf * t["eb"] + dqn * t["enb"]) * (DK ** -0.5)
            dk_ref[pl.ds(r0, CH), :] = dkn * t["enb"] + dkp * t["eb"] + dkd * t["ed"]
            dkd_kd = dkd * t["kd"]
            dbc = dqf * t["qf"] - dqn * t["qn"] - dkn * t["kn"] + dkp * t["kp"] - dkd_kd
            dbc = dbc + jnp.where(last_row, _rowsum(dkd_kd) + dbend, 0.0)
            dla = jnp.dot(umat, dbc, preferred_element_type=f32, precision=HIGHEST)
            dpre = dla * (1.0 / TAU) * (1.0 - jnp.exp(TAU * lac))
            dpre_ref[pl.ds(r0, CH), :] = dpre
            redb_ref[...] += dpre
            return carry

        lax.fori_loop(0, nc, chunk, 0, unroll=True)

    rev = lambda col: (lambda i: (nt - 1 - i, col))
    return _pcall(
        body, (z, z, z, z, la, sprev, dyg, gn_s), name=name, comm=comm,
        grid=(nt,),
        in_specs=[pl.BlockSpec((tg, DQK), rev(4)), pl.BlockSpec((tg, DQK), rev(5)),
                  pl.BlockSpec((tg, DG), rev(3)), pl.BlockSpec((tg, DG), rev(4)),
                  pl.BlockSpec((tg, DQK), rev(0)), pl.BlockSpec((nc * DQK, DV), rev(0)),
                  pl.BlockSpec((tg, DG), rev(0)), _full(gn_s.shape)],
        out_specs=[pl.BlockSpec((tg, DQK), rev(0)), pl.BlockSpec((tg, DQK), rev(0)),
                   pl.BlockSpec((tg, DG), rev(0)), pl.BlockSpec((tg, DG), rev(0)), pl.BlockSpec((tg, DQK), rev(0)),
                   _full((DQK, DV)), _full((CH, DQK))],
        out_shape=[jax.ShapeDtypeStruct((S, DQK), f32), jax.ShapeDtypeStruct((S, DQK), f32),
                   jax.ShapeDtypeStruct((S, DG), f32), jax.ShapeDtypeStruct((S, DG), f32), jax.ShapeDtypeStruct((S, DQK), f32),
                   jax.ShapeDtypeStruct((DQK, DV), f32), jax.ShapeDtypeStruct((CH, DQK), f32)],
        scratch_shapes=[pltpu.VMEM((DQK, DV), f32)],
    )


def mixout_fwd(x1, yc, yg, mod, wout, tm, name):
    S = x1.shape[0]

    def body(x_ref, yc_ref, yg_ref, mod_ref, w_ref, xo_ref):
        mixo = _dot(yc_ref[...], w_ref[0:DC, :]) + _dot(yg_ref[...], w_ref[DC:DC + DG, :])
        xo_ref[...] = x_ref[...] + mod_ref[5:6, :] * mixo

    row = lambda i: (i, 0)
    return pl.pallas_call(
        body, name=name,
        grid=(S // tm,),
        in_specs=[pl.BlockSpec((tm, D), row), pl.BlockSpec((tm, DC), row), pl.BlockSpec((tm, DG), row),
                  _full(mod.shape), _full(wout.shape)],
        out_specs=pl.BlockSpec((tm, D), row),
        out_shape=jax.ShapeDtypeStruct((S, D), f32),
        compiler_params=_cp(1),
    )(x1, yc, yg, mod, wout)


def mixout_bwd(dx2, yc, yg, mod, wout, tm, name):
    S = dx2.shape[0]

    def body(dx_ref, yc_ref, yg_ref, mod_ref, w_ref, dm_ref, dyc_ref, dyg_ref, red_ref):
        @pl.when(pl.program_id(0) == 0)
        def _():
            red_ref[...] = jnp.zeros_like(red_ref)

        dxv = dx_ref[...]
        mixo = _dot(yc_ref[...], w_ref[0:DC, :]) + _dot(yg_ref[...], w_ref[DC:DC + DG, :])
        red_ref[0:1, :] += _rowsum(dxv * mixo)
        dm = (mod_ref[5:6, :] * dxv).astype(bf16)
        dm_ref[...] = dm
        dycat = _dg(dm, w_ref[...], NT)
        dyc_ref[...] = dycat[:, :DC]
        dyg_ref[...] = dycat[:, DC:]

    row = lambda i: (i, 0)
    return pl.pallas_call(
        body, name=name,
        grid=(S // tm,),
        in_specs=[pl.BlockSpec((tm, D), row), pl.BlockSpec((tm, DC), row), pl.BlockSpec((tm, DG), row),
                  _full(mod.shape), _full(wout.shape)],
        out_specs=[pl.BlockSpec((tm, D), row), pl.BlockSpec((tm, DC), row), pl.BlockSpec((tm, DG), row), _full((8, D))],
        out_shape=[jax.ShapeDtypeStruct((S, D), bf16), jax.ShapeDtypeStruct((S, DC), f32),
                   jax.ShapeDtypeStruct((S, DG), f32), jax.ShapeDtypeStruct((8, D), f32)],
        compiler_params=_cp(1),
    )(dx2, yc, yg, mod, wout)


def final_fwd_bwd(x, tgt, fmod, g, tm, name):
    S = x.shape[0]

    def body(x_ref, t_ref, fm_ref, g_ref, dx_ref, red_ref):
        @pl.when(pl.program_id(0) == 0)
        def _():
            red_ref[...] = jnp.zeros_like(red_ref)

        xh, rstd = _rms_parts(x_ref[...])
        gv = g_ref[...]
        n = xh * gv
        sc = 1.0 + fm_ref[1:2, :]
        e = n * sc + fm_ref[0:1, :] - t_ref[...]
        red_ref[0:1, :] += _rowsum(e * e) * (0.5 / D)
        dy = e * (1.0 / D)
        dn = dy * sc
        red_ref[1:2, :] += _rowsum(dy)
        red_ref[2:3, :] += _rowsum(dy * n)
        red_ref[3:4, :] += _rowsum(dn * xh)
        dx_ref[...] = _rms_bwd(dn * gv, xh, rstd)

    row = lambda i: (i, 0)
    return pl.pallas_call(
        body, name=name,
        grid=(S // tm,),
        in_specs=[pl.BlockSpec((tm, D), row), pl.BlockSpec((tm, D), row), _full(fmod.shape), _full(g.shape)],
        out_specs=[pl.BlockSpec((tm, D), row), _full((8, D))],
        out_shape=[jax.ShapeDtypeStruct((S, D), f32), jax.ShapeDtypeStruct((8, D), f32)],
        compiler_params=_cp(1),
    )(x, tgt, fmod, g)


def ada_fwd(c_all, w, b, name):
    n = w.shape[1]

    def body(c_ref, w_ref, b_ref, o_ref):
        cv = c_ref[...]
        o_ref[...] = jnp.dot(cv * _sigmoid(cv), w_ref[...], preferred_element_type=f32, precision=HIGHEST) + b_ref[...]

    return pl.pallas_call(
        body, name=name,
        in_specs=[_full(c_all.shape), _full(w.shape), _full(b.shape)],
        out_specs=_full((N_DEV, n)),
        out_shape=jax.ShapeDtypeStruct((N_DEV, n), f32),
        grid=(1,),
        compiler_params=_cp(1),
    )(c_all, w, b)


def ada_wgrad(c_all_t, dm, name):
    n = dm.shape[1]

    def body(c_ref, d_ref, o_ref):
        cv = c_ref[...]
        o_ref[...] = jnp.dot(cv * _sigmoid(cv), d_ref[...], preferred_element_type=f32, precision=HIGHEST)

    return pl.pallas_call(
        body, name=name,
        in_specs=[_full(c_all_t.shape), _full(dm.shape)],
        out_specs=_full((D, n)),
        out_shape=jax.ShapeDtypeStruct((D, n), f32),
        grid=(1,),
        compiler_params=_cp(1),
    )(c_all_t, dm)


def _adam_math(gv, wv, mv, vv):
    m = ADAM_B1 * mv + (1.0 - ADAM_B1) * gv
    v = ADAM_B2 * vv + (1.0 - ADAM_B2) * (gv * gv)
    m_hat = m / (1.0 - ADAM_B1 ** ADAM_STEP)
    v_hat = v / (1.0 - ADAM_B2 ** ADAM_STEP)
    delta = -ADAM_LR * (m_hat / (jnp.sqrt(v_hat) + ADAM_EPS) + ADAM_WD * wv)
    return delta, m, v


def adam_parts(parts, w, m, v, tr, name, comm=None):
    L, R, C = w.shape
    nt = R // tr

    def body(*refs):
        p_refs = refs[:L]
        w_ref, m_ref, v_ref, g_ref, d_ref, mo_ref, vo_ref = refs[L:]
        lyr = pl.program_id(0)
        for l in range(L):
            @pl.when(lyr == l)
            def _(p_ref=p_refs[l]):
                gv = p_ref[0].astype(f32)
                for k in range(1, N_DEV):
                    gv = gv + p_ref[k].astype(f32)
                g_ref[...] = gv
                d_ref[...], mo_ref[...], vo_ref[...] = _adam_math(gv, w_ref[...], m_ref[...], v_ref[...])

    def part_spec(l):
        return pl.BlockSpec((N_DEV, tr, C), lambda lyr, i: (0, jnp.where(lyr == l, i, jnp.where(lyr < l, 0, nt - 1)), 0))

    spec = pl.BlockSpec((None, tr, C), lambda lyr, i: (lyr, i, 0))
    shp = jax.ShapeDtypeStruct((L, R, C), f32)
    return _pcall(
        body, (*parts, w, m, v), name=name, comm=comm,
        grid=(L, nt),
        in_specs=[part_spec(l) for l in range(L)] + [spec, spec, spec],
        out_specs=[spec, spec, spec, spec],
        out_shape=[shp, shp, shp, shp],
    )


def adam_plain(gr, w, m, v, tr, name):
    R, C = w.shape

    def body(g_ref, w_ref, m_ref, v_ref, d_ref, mo_ref, vo_ref):
        d_ref[...], mo_ref[...], vo_ref[...] = _adam_math(g_ref[...], w_ref[...], m_ref[...], v_ref[...])

    spec = pl.BlockSpec((tr, C), lambda i: (i, 0))
    shp = jax.ShapeDtypeStruct((R, C), f32)
    return pl.pallas_call(
        body, name=name,
        grid=(R // tr,),
        in_specs=[spec, spec, spec, spec],
        out_specs=[spec, spec, spec],
        out_shape=[shp, shp, shp],
        compiler_params=_cp(1),
    )(gr, w, m, v)


def sum8(parts, name):
    _, R, C = parts.shape

    def body(p_ref, o_ref):
        acc = p_ref[0]
        for k in range(1, N_DEV):
            acc = acc + p_ref[k]
        o_ref[...] = acc

    return pl.pallas_call(
        body, name=name,
        grid=(1,),
        in_specs=[_full(parts.shape)],
        out_specs=_full((R, C)),
        out_shape=jax.ShapeDtypeStruct((R, C), f32),
        compiler_params=_cp(1),
    )(parts)


def _place():
    return lax.axis_index("x"), lax.axis_index("y"), lax.axis_index("c")


def _gather_steps(ins, outs, send_sems, recv_sems, local_sems, place):
    n = len(ins)
    x, y, c = place
    me, sibling = (x, y, c), (x, y, 1 - c)
    chips = [(1 - x, y), (x, 1 - y), (1 - x, 1 - y)]

    def slot(a, p):
        return outs[a].at[4 * p[0] + 2 * p[1] + p[2]]

    def copy(a, k, block, to, src=None):
        return pltpu.make_async_remote_copy(
            src_ref=slot(a, block) if src is None else src, dst_ref=slot(a, block),
            send_sem=send_sems.at[a * 7 + k], recv_sem=recv_sems.at[a * 7 + k],
            device_id=to, device_id_type=MESH)

    def mine():
        return [pltpu.make_async_copy(ins[a], slot(a, me), local_sems.at[a]) for a in range(n)]

    def first():
        cps = []
        for a in range(n):
            cps.append(copy(a, 0, me, sibling, src=ins[a]))
            cps += [copy(a, 1 + j, me, (*chip, c), src=ins[a]) for j, chip in enumerate(chips)]
        return cps

    def start():
        for cp in mine() + first():
            cp.start()

    def forward():
        for j, chip in enumerate(chips):
            for a in range(n):
                copy(a, 1 + j, (*chip, c), me).wait_recv()
                copy(a, 4 + j, (*chip, c), sibling).start()

    def finish():
        for a in range(n):
            copy(a, 0, sibling, me).wait_recv()
            for j, chip in enumerate(chips):
                copy(a, 4 + j, (*chip, 1 - c), me).wait_recv()
        for cp in first() + [copy(a, 4 + j, (*chip, c), sibling) for j, chip in enumerate(chips) for a in range(n)]:
            cp.wait_send()
        for cp in mine():
            cp.wait()

    return start, forward, finish


def _exchange_steps(ins, outs, send_sems, recv_sems, local_sems, place):
    n = len(ins)
    x, y, c = place
    me_i = 4 * x + 2 * y + c

    def mine():
        return [pltpu.make_async_copy(ins[a].at[me_i], outs[a].at[me_i], local_sems.at[a]) for a in range(n)]

    def copies(receiving):
        cps = []
        for k in range(1, N_DEV):
            px = 1 - x if (k >> 2) & 1 else x
            py = 1 - y if (k >> 1) & 1 else y
            pc = 1 - c if k & 1 else c
            p_i = 4 * px + 2 * py + pc
            for a in range(n):
                sem = a * 7 + k - 1
                cps.append(pltpu.make_async_remote_copy(
                    src_ref=ins[a].at[p_i], dst_ref=outs[a].at[p_i if receiving else me_i],
                    send_sem=send_sems.at[sem], recv_sem=recv_sems.at[sem],
                    device_id=(px, py, pc), device_id_type=MESH))
        return cps

    def start():
        for cp in mine() + copies(False):
            cp.start()

    def finish():
        for cp in copies(True):
            cp.wait_recv()
        for cp in copies(False):
            cp.wait_send()
        for cp in mine():
            cp.wait()

    return start, None, finish


_COMM_STEPS = {"gather": _gather_steps, "exchange": _exchange_steps}


def _comm_out_shapes(kind, arrs):
    if kind == "gather":
        return [jax.ShapeDtypeStruct((N_DEV,) + a.shape, a.dtype) for a in arrs]
    return [jax.ShapeDtypeStruct(a.shape, a.dtype) for a in arrs]


def _comm_sems(n):
    return [pltpu.SemaphoreType.DMA((7 * n,)), pltpu.SemaphoreType.DMA((7 * n,)), pltpu.SemaphoreType.DMA((n,))]


def _pcall(body, args, *, name, grid, in_specs, out_specs, out_shape, scratch_shapes=(), comm=None):
    in_specs, out_specs, out_shape = list(in_specs), list(out_specs), list(out_shape)
    scratch_shapes = list(scratch_shapes)
    cparams = _cp(len(grid))
    if comm is None:
        outs = pl.pallas_call(body, name=name, grid=grid, in_specs=in_specs, out_specs=out_specs, out_shape=out_shape,
                              scratch_shapes=scratch_shapes, compiler_params=cparams)(*args)
        return list(outs), []
    kind, arrs = comm
    nc, n_in, n_out, n_scr = len(arrs), len(in_specs), len(out_specs), len(scratch_shapes)
    total = 1
    for gdim in grid:
        total *= gdim
    forward_step = (total * 3) // 4

    def hosted(*refs):
        core_in, c_in = refs[:n_in], refs[n_in:n_in + nc]
        core_out = refs[n_in + nc:n_in + nc + n_out]
        c_out = refs[n_in + nc + n_out:n_in + 2 * nc + n_out]
        rest = refs[n_in + 2 * nc + n_out:]
        step = pl.program_id(0)
        for ax in range(1, len(grid)):
            step = step * grid[ax] + pl.program_id(ax)
        start, forward, finish = _COMM_STEPS[kind](c_in, c_out, *rest[n_scr:], _place())
        pl.when(step == 0)(start)
        if forward is not None:
            pl.when(step == forward_step)(forward)
        body(*core_in, *core_out, *rest[:n_scr])
        pl.when(step == total - 1)(finish)

    any_spec = pl.BlockSpec(memory_space=pl.ANY)
    outs = pl.pallas_call(
        hosted, name=name, grid=grid,
        in_specs=in_specs + [any_spec] * nc,
        out_specs=out_specs + [any_spec] * nc,
        out_shape=out_shape + _comm_out_shapes(kind, arrs),
        scratch_shapes=scratch_shapes + _comm_sems(nc),
        compiler_params=cparams)(*args, *arrs)
    return list(outs[:n_out]), list(outs[n_out:])


def _comm_call(kind, arrs, name):
    n = len(arrs)

    def body(*refs):
        start, forward, finish = _COMM_STEPS[kind](refs[:n], refs[n:2 * n], *refs[2 * n:], _place())
        start()
        if forward is not None:
            forward()
        finish()

    any_spec = pl.BlockSpec(memory_space=pl.ANY)
    return pl.pallas_call(
        body, name=name,
        in_specs=[any_spec] * n, out_specs=[any_spec] * n,
        out_shape=_comm_out_shapes(kind, arrs), scratch_shapes=_comm_sems(n),
    )(*arrs)


def all_gather(arrs, name):
    return _comm_call("gather", arrs, name)


def all_to_all(arrs, name):
    return _comm_call("exchange", arrs, name)


def _tiles(S):
    t = min(512, S)
    return dict(ffn=min(256, S), row=t, conv=t, gla=t, bk=min(1024, S), bk_ffn=t)


BIG = ("wi1", "wo1", "win", "wout", "wi2", "wo2")


def _col_shards_to_full(gathered):
    n, r, c = gathered.shape
    return jnp.transpose(gathered, (1, 0, 2)).reshape(r, n * c)


def _win_full(win_a):
    return jnp.pad(_col_shards_to_full(win_a), ((0, 0), (0, DINP - DIN)))


def train_pass(x, tgt, mods, fmod, sh, ws, wi1_first, wo1_first):
    S = x.shape[0]
    T = _tiles(S)
    bk = T["bk"]
    full = [dict() for _ in range(DEPTH)]
    full[0]["wi1"], full[0]["wo1"] = wi1_first, wo1_first.reshape(F, D)
    saved = []
    xc = x
    for l in range(DEPTH):
        w, fw = ws[f"L{l}"], full[l]
        x0 = xc
        names = ("win", "wout", "wi2", "wo2") if l == 0 else ("wi2", "wo2")
        (x1, h1f, zg1, zu1, f1), got = ffn_fwd(x0, mods[l], w["g1"], fw["wi1"], fw["wo1"], (0, 1, 2), T["ffn"], f"ffn1_fwd_{l}",
                                          comm=("gather", [sh[n][l] for n in names]))
        fw.update(zip(names, got))
        if l == 0:
            fw["win"], fw["wout"] = _win_full(fw["win"]), fw["wout"].reshape(D, D)
        fw["wo2"] = fw["wo2"].reshape(F, D)
        z, la = mixin_fwd(x1, mods[l], w["g2"], fw["win"], w["wgu"], w["bgate"], T["row"], f"mixin_fwd_{l}")
        y, yc = conv_fwd(z, w["wdw"], w["cpar"], T["conv"], f"conv_fwd_{l}")
        yg, sprev = gla_fwd(z, la, w["gn_s"], T["gla"], f"gla_fwd_{l}")
        x2 = mixout_fwd(x1, yc, yg, mods[l], fw["wout"], T["row"], f"mixout_fwd_{l}")
        names = ("wi1", "wo1", "win", "wout") if l + 1 < DEPTH else ()
        (x3, h2f, zg2, zu2, f2), got = ffn_fwd(x2, mods[l], w["g3"], fw["wi2"], fw["wo2"], (6, 7, 8), T["ffn"], f"ffn2_fwd_{l}",
                                          comm=("gather", [sh[n][l + 1] for n in names]) if names else None)
        if names:
            nx = full[l + 1]
            nx["wi1"], nx["wo1"], nx["win"], nx["wout"] = got[0], got[1].reshape(F, D), _win_full(got[2]), got[3].reshape(D, D)
        saved.append(dict(x0=x0, x1=x1, x2=x2, h1f=h1f, zg1=zg1, zu1=zu1, f1=f1, h2f=h2f, zg2=zg2, zu2=zu2, f2=f2,
                          z=z, la=la, y=y, yc=yc, yg=yg, sprev=sprev))
        xc = x3

    dx, redf = final_fwd_bwd(xc, tgt, fmod, ws["gf"], T["row"], "loss_head")
    loss_lanes = redf[0]
    dfmod = redf[1:3]
    grads = {"gf": redf[3]}
    dmods = [None] * DEPTH
    recv = {n: [None] * DEPTH for n in BIG}

    def ffn_backward(xin, dy, h, zg, zu, fo, gain, wi, wo, rows, l, tag, ride=None):
        (df, a, dzg, dzu), got_ride = ffn_bwd_hidden(dy, zg, zu, mods[l], wo, rows[2], T["ffn"], f"{tag}_bwd_hidden_{l}",
                                                     comm=("exchange", ride) if ride else None)
        p_wo, _ = dwo_pieces(a, df, T["bk_ffn"], f"d{tag}_wo_{l}")
        p_wi, (r_wo,) = dwi_pieces(h, dzg, dzu, T["bk_ffn"], f"d{tag}_wi_{l}",
                                   comm=("exchange", [p_wo.reshape(N_DEV, F // N_DEV, D)]))
        (dxin, red), (r_wi,) = ffn_bwd_input(xin, dy, dzg, dzu, fo, mods[l], gain, wi, rows, T["ffn"],
                                             f"{tag}_bwd_input_{l}", comm=("exchange", [p_wi]))
        return dxin, red, r_wi, r_wo, got_ride

    for l in reversed(range(DEPTH)):
        w, fw, sv = ws[f"L{l}"], full[l], saved[l]
        g = {}
        dx2, red3, recv["wi2"][l], recv["wo2"][l], _ = ffn_backward(
            sv["x2"], dx, sv["h2f"], sv["zg2"], sv["zu2"], sv["f2"], w["g3"], fw["wi2"], fw["wo2"], (6, 7, 8), l, "ffn2")
        dmix, dyc, dyg, red_o = mixout_bwd(dx2, sv["yc"], sv["yg"], mods[l], fw["wout"], T["row"], f"mixout_bwd_{l}")
        p_wout = jnp.concatenate([matmul_tn(sv["yc"], dmix, DC, D, DC, D, bk, f"dwout_c_{l}", out_dtype=bf16),
                                  matmul_tn(sv["yg"], dmix, DG, D, DG, D, bk, f"dwout_g_{l}", out_dtype=bf16)], axis=0)
        (dq, dk, dv, dr, dpre, redg, redb), (recv["wout"][l],) = gla_bwd(
            sv["z"], sv["la"], sv["sprev"], dyg, w["gn_s"], T["gla"], f"gla_bwd_{l}",
            comm=("exchange", [p_wout.reshape(N_DEV, D // N_DEV, D)]))
        (dzab, redc), _ = conv_bwd(sv["z"], sv["y"], dyc, w["wdw"], w["cpar"], T["conv"], f"conv_bwd_{l}")
        (dx1, h2, dz, red2), _ = mixin_bwd(sv["x1"], dx2, dzab, dq, dk, dv, dr, dpre, mods[l], w["g2"], fw["win"], w["wgu"],
                                           T["row"], f"mixin_bwd_{l}")
        dwin = matmul_tn(h2, dz, D, DINP, D, DINP, bk, f"dwin_{l}", out_dtype=bf16)[:, :DIN]
        p_win = jnp.transpose(dwin.reshape(D, N_DEV, DIN // N_DEV), (1, 0, 2))
        g["wgu"] = matmul_tn(sv["z"], dpre, 128, DQK, 128, DQK, bk, f"dwgu_{l}", a_col_block=(DINP - 128) // 128)[:GR]
        g["bgate"] = jnp.sum(redb, axis=0)
        g["gn"] = jnp.sum(redg.reshape(NH, CH, DV), axis=1)
        g["wdw"] = redc[:CW]
        g["bdw"], g["gln"], g["bln"] = redc[32], redc[33], redc[34]
        dx0, red1, recv["wi1"][l], recv["wo1"][l], (recv["win"][l],) = ffn_backward(
            sv["x0"], dx1, sv["h1f"], sv["zg1"], sv["zu1"], sv["f1"], w["g1"], fw["wi1"], fw["wo1"], (0, 1, 2), l, "ffn1",
            ride=[p_win])
        g["g1"], g["g2"], g["g3"] = red1[3], red2[2], red3[3]
        dmods[l] = jnp.stack([red1[0], red1[1], red1[2], red2[0], red2[1], red_o[0], red3[0], red3[1], red3[2]], axis=0)
        grads[f"L{l}"] = g
        dx = dx0
    return loss_lanes, dx, grads, dmods, dfmod, recv


def _pad_rows(a, rows):
    return jnp.pad(a, ((0, rows - a.shape[0]), (0, 0)))


def kernel(x, c, w_ada, b_ada, g_norm_ffn1, w_ffn1_in, w_ffn1_out, g_norm_mix, w_in, w_dw, b_dw, g_conv_ln, b_conv_ln, w_gate_up, b_gate, g_gla_norm, w_out, g_norm_ffn2, w_ffn2_in, w_ffn2_out, g_norm_final, w_ada_final, b_ada_final, loss_target, m_w_ada, m_b_ada, m_g_norm_ffn1, m_w_ffn1_in, m_w_ffn1_out, m_g_norm_mix, m_w_in, m_w_dw, m_b_dw, m_g_conv_ln, m_b_conv_ln, m_w_gate_up, m_b_gate, m_g_gla_norm, m_w_out, m_g_norm_ffn2, m_w_ffn2_in, m_w_ffn2_out, m_g_norm_final, m_w_ada_final, m_b_ada_final, v_w_ada, v_b_ada, v_g_norm_ffn1, v_w_ffn1_in, v_w_ffn1_out, v_g_norm_mix, v_w_in, v_w_dw, v_b_dw, v_g_conv_ln, v_b_conv_ln, v_w_gate_up, v_b_gate, v_g_gla_norm, v_w_out, v_g_norm_ffn2, v_w_ffn2_in, v_w_ffn2_out, v_g_norm_final, v_w_ada_final, v_b_ada_final):
    me = 4 * lax.axis_index("x") + 2 * lax.axis_index("y") + lax.axis_index("c")
    L = DEPTH
    n_ada = N_MOD * D // N_DEV
    n_fin = 2 * D // N_DEV

    small = jnp.concatenate([c.reshape(-1), w_dw.reshape(-1), w_gate_up.reshape(-1)])
    n_small = small.shape[0]
    small = jnp.pad(small, (0, 8 * D - n_small)).reshape(8, D)
    big = dict(wi1=w_ffn1_in, wo1=w_ffn1_out, win=w_in, wout=w_out, wi2=w_ffn2_in, wo2=w_ffn2_out)
    sh = {n: [a[l].astype(bf16) for l in range(L)] for n, a in big.items()}
    small_a, wi1_first, wo1_first = all_gather([small, sh["wi1"][0], sh["wo1"][0]], "gather_first")
    small_a = small_a.reshape(N_DEV, 8 * D)
    c_all = small_a[:, :D]
    o1 = D + L * CW * (DC // N_DEV)
    wdw_full = _col_shards_to_full(small_a[:, D:o1].reshape(N_DEV, L * CW, DC // N_DEV)).reshape(L, CW, DC)
    wgu_full = _col_shards_to_full(small_a[:, o1:o1 + L * GR * (DQK // N_DEV)].reshape(N_DEV, L * GR, DQK // N_DEV)).reshape(L, GR, DQK)

    b_ada_mine = lax.dynamic_slice(b_ada, (0, me * n_ada), (L, n_ada))
    b_fin_mine = lax.dynamic_slice(b_ada_final, (me * n_fin,), (n_fin,))
    parts = [ada_fwd(c_all, w_ada[l], b_ada_mine[l:l + 1], f"ada_fwd_{l}") for l in range(L)]
    parts.append(ada_fwd(c_all, w_ada_final, b_fin_mine.reshape(1, n_fin), "ada_fwd_final"))
    modsrc = jnp.concatenate(parts, axis=1)
    n_row = modsrc.shape[1]
    modsrc = jnp.pad(modsrc, ((0, 0), (0, 24 * 128 - n_row))).reshape(N_DEV, 24, 128)
    (modrecv,) = all_to_all([modsrc], "exchange_mod")
    modrecv = modrecv.reshape(N_DEV, 24 * 128)
    mods = []
    for l in range(L):
        mvec = modrecv[:, l * n_ada:(l + 1) * n_ada].reshape(N_MOD, D)
        mods.append(_pad_rows(mvec, 16))
    fmod = _pad_rows(modrecv[:, L * n_ada:L * n_ada + n_fin].reshape(2, D), 8)

    ws = {"gf": g_norm_final.reshape(1, D)}
    for l in range(L):
        ws[f"L{l}"] = dict(
            g1=g_norm_ffn1[l].reshape(1, D), g2=g_norm_mix[l].reshape(1, D), g3=g_norm_ffn2[l].reshape(1, D),
            wgu=_pad_rows(wgu_full[l], 128).astype(bf16),
            bgate=b_gate[l].reshape(1, DQK),
            wdw=_pad_rows(wdw_full[l], 32),
            cpar=_pad_rows(jnp.stack([b_dw[l], g_conv_ln[l], b_conv_ln[l]]), 8),
            gn_s=jnp.repeat(g_gla_norm[l], CH, axis=0),
        )

    loss_lanes, grad_x, gr, dmods, dfmod, recv = train_pass(
        x[0], loss_target[0], mods, fmod, sh, ws, wi1_first, wo1_first)

    def adam_big(rv, w, m, v, name):
        R = w.shape[1]
        tr = 256 if R % 256 == 0 else R // 2
        return adam_parts(rv, w, m, v, tr, name)

    res = {}
    res["w_ffn2_in"], _ = adam_big(recv["wi2"], w_ffn2_in, m_w_ffn2_in, v_w_ffn2_in, "adam_ffn2_in")
    res["w_ffn2_out"], _ = adam_big(recv["wo2"], w_ffn2_out, m_w_ffn2_out, v_w_ffn2_out, "adam_ffn2_out")
    res["w_in"], _ = adam_big(recv["win"], w_in, m_w_in, v_w_in, "adam_w_in")
    res["w_out"], _ = adam_big(recv["wout"], w_out, m_w_out, v_w_out, "adam_w_out")
    res["w_ffn1_out"], _ = adam_big(recv["wo1"], w_ffn1_out, m_w_ffn1_out, v_w_ffn1_out, "adam_ffn1_out")
    res["w_ffn1_in"], _ = adam_big(recv["wi1"], w_ffn1_in, m_w_ffn1_in, v_w_ffn1_in, "adam_ffn1_in")

    flat = lambda name: jnp.stack([gr[f"L{l}"][name] for l in range(L)]).reshape(-1)
    sections = [
        ("b_ada", jnp.stack(dmods).reshape(-1)), ("b_ada_final", dfmod.reshape(-1)),
        ("g_norm_ffn1", flat("g1")), ("g_norm_mix", flat("g2")), ("g_norm_ffn2", flat("g3")), ("g_norm_final", gr["gf"]),
        ("b_dw", flat("bdw")), ("g_conv_ln", flat("gln")), ("b_conv_ln", flat("bln")), ("b_gate", flat("bgate")),
        ("g_gla_norm", flat("gn")),
    ]
    n_rep = sum(s[1].shape[0] for s in sections)
    rep_rows = -(-n_rep // D)
    extra = [("loss", loss_lanes), ("w_dw", flat("wdw")), ("w_gate_up", flat("wgu"))]
    pack = jnp.concatenate([s[1] for s in sections] + [jnp.zeros((rep_rows * D - n_rep,), f32)] + [s[1] for s in extra])
    n_pack = pack.shape[0]
    pack_rows = -(-n_pack // (8 * D)) * 8
    pack = jnp.pad(pack, (0, pack_rows * D - n_pack)).reshape(pack_rows, D)
    (pack_all,) = all_gather([pack], "gather_small_grads")
    tot = sum8(pack_all, "sum_small_grads")
    tot_flat = tot.reshape(-1)
    loss = jnp.sum(tot_flat[rep_rows * D:rep_rows * D + D])
    o_dw = rep_rows * D + D
    g_wdw_full = tot_flat[o_dw:o_dw + L * CW * DC].reshape(L, CW, DC)
    o_gu = o_dw + L * CW * DC
    g_wgu_full = tot_flat[o_gu:o_gu + L * GR * DQK].reshape(L, GR, DQK)

    small_params = dict(b_ada=(b_ada, m_b_ada, v_b_ada), b_ada_final=(b_ada_final, m_b_ada_final, v_b_ada_final),
                        g_norm_ffn1=(g_norm_ffn1, m_g_norm_ffn1, v_g_norm_ffn1), g_norm_mix=(g_norm_mix, m_g_norm_mix, v_g_norm_mix),
                        g_norm_ffn2=(g_norm_ffn2, m_g_norm_ffn2, v_g_norm_ffn2), g_norm_final=(g_norm_final, m_g_norm_final, v_g_norm_final),
                        b_dw=(b_dw, m_b_dw, v_b_dw), g_conv_ln=(g_conv_ln, m_g_conv_ln, v_g_conv_ln),
                        b_conv_ln=(b_conv_ln, m_b_conv_ln, v_b_conv_ln), b_gate=(b_gate, m_b_gate, v_b_gate),
                        g_gla_norm=(g_gla_norm, m_g_gla_norm, v_g_gla_norm))

    def rep_pack(idx):
        p = jnp.concatenate([small_params[s[0]][idx].reshape(-1) for s in sections])
        return jnp.pad(p, (0, rep_rows * D - n_rep)).reshape(rep_rows, D)

    g_rep = tot[:rep_rows]
    d_rep, m_rep, v_rep = adam_plain(g_rep, rep_pack(0), rep_pack(1), rep_pack(2), rep_rows, "adam_small")
    off = 0
    for sname, sval in sections:
        shp = small_params[sname][0].shape
        nel = sval.shape[0]
        res[sname] = [a.reshape(-1)[off:off + nel].reshape(shp) for a in (g_rep, d_rep, m_rep, v_rep)]
        off += nel

    def adam_cols(g_full, w, m, v, name):
        shp = w.shape
        g_mine = lax.dynamic_slice(g_full, (0, 0, me * shp[2]), shp)
        R, C = shp[0] * shp[1], shp[2]
        outs = adam_plain(g_mine.reshape(R, C), w.reshape(R, C), m.reshape(R, C), v.reshape(R, C), R, name)
        return [g_mine] + [o.reshape(shp) for o in outs]

    res["w_dw"] = adam_cols(g_wdw_full, w_dw, m_w_dw, v_w_dw, "adam_w_dw")
    res["w_gate_up"] = adam_cols(g_wgu_full, w_gate_up, m_w_gate_up, v_w_gate_up, "adam_w_gate_up")

    c_all_t = c_all.T
    dmod_all = pack_all.reshape(N_DEV, -1)[:, :L * N_MOD * D].reshape(N_DEV, L, N_MOD * D)
    dfm_all = pack_all.reshape(N_DEV, -1)[:, L * N_MOD * D:L * N_MOD * D + 2 * D]
    dm_mine = lax.dynamic_slice(dmod_all, (0, 0, me * n_ada), (N_DEV, L, n_ada))
    dfm_mine = lax.dynamic_slice(dfm_all, (0, me * n_fin), (N_DEV, n_fin))
    g_w_ada = jnp.stack([ada_wgrad(c_all_t, dm_mine[:, l], f"ada_wgrad_{l}") for l in range(L)])
    g_w_fin = ada_wgrad(c_all_t, dfm_mine, "ada_wgrad_final")
    outs = adam_plain(g_w_ada.reshape(L * D, n_ada), w_ada.reshape(L * D, n_ada), m_w_ada.reshape(L * D, n_ada),
                      v_w_ada.reshape(L * D, n_ada), 256, "adam_w_ada")
    res["w_ada"] = [g_w_ada] + [o.reshape(w_ada.shape) for o in outs]
    res["w_ada_final"] = [g_w_fin] + list(adam_plain(g_w_fin, w_ada_final, m_w_ada_final, v_w_ada_final, 256, "adam_w_ada_final"))

    order = ["w_ada", "b_ada", "g_norm_ffn1", "w_ffn1_in", "w_ffn1_out", "g_norm_mix", "w_in", "w_dw", "b_dw", "g_conv_ln",
             "b_conv_ln", "w_gate_up", "b_gate", "g_gla_norm", "w_out", "g_norm_ffn2", "w_ffn2_in", "w_ffn2_out",
             "g_norm_final", "w_ada_final", "b_ada_final"]
    out = [loss, grad_x[None]]
    for k in range(4):
        out += [res[name][k] for name in order]
    return tuple(out)
```

```python
import functools

import jax
import jax.numpy as jnp
from jax import lax
from jax.experimental import pallas as pl
from jax.experimental.pallas import tpu as pltpu

f32 = jnp.float32
bf16 = jnp.bfloat16

N_DEV = 8
DEPTH = 2
D = 1024
F = 2816
DC = 512
NH = 4
DK = 64
DV = 128
DQK = NH * DK
DG = NH * DV
CH = 64
CW = 31
GR = 16
TAU = 16.0
N_MOD = 9
DIN = 2 * DC + 2 * DQK + 2 * DG + GR
DINP = 2688
EPS = 1e-6
HALO = 32
SUBLANES = 8
CONV_ROWS = 32
NFS = 4
FS = F // NFS

ADAM_LR = 0.001
ADAM_B1 = 0.9
ADAM_B2 = 0.999
ADAM_EPS = 1e-08
ADAM_WD = 0.01
ADAM_STEP = 10

V7X_VMEM_LIMIT = 56 * 1024 * 1024
MESH = pl.DeviceIdType.MESH
HIGHEST = lax.Precision.HIGHEST

NT = (((1,), (1,)), ((), ()))
TN = (((0,), (0,)), ((), ()))


def _cp(n_axes):
    return pltpu.CompilerParams(dimension_semantics=("arbitrary",) * n_axes, vmem_limit_bytes=V7X_VMEM_LIMIT)


def _full(shape):
    nd = len(shape)
    return pl.BlockSpec(shape, lambda *_: (0,) * nd)


def _resident(shape):
    nd = len(shape)
    return pl.BlockSpec(shape, lambda *_: (0,) * nd, pipeline_mode=pl.Buffered(1))


def _dot(a, b):
    return jnp.dot(a, b, preferred_element_type=f32)


def _dg(a, b, dims):
    return lax.dot_general(a, b, dims, preferred_element_type=f32)


def _sigmoid(x):
    return jax.nn.sigmoid(x)


def _rowsum(x):
    return jnp.sum(x, axis=0, keepdims=True)


def _rms_parts(xv):
    rstd = lax.rsqrt(jnp.mean(xv * xv, axis=-1, keepdims=True) + EPS)
    return xv * rstd, rstd


def _rms_bwd(dxh, xh, rstd):
    return rstd * (dxh - xh * jnp.mean(dxh * xh, axis=-1, keepdims=True))


def ffn_fwd(x, mod, g, wi, wo, rows, tm, name, comm=None):
    S = x.shape[0]
    r_shift, r_scale, r_gate = rows

    def body(x_ref, mod_ref, g_ref, wi_ref, wo_ref, xo_ref, h_ref, zg_ref, zu_ref, f_ref):
        xv = x_ref[...]
        xh, _ = _rms_parts(xv)
        h = (xh * g_ref[...] * (1.0 + mod_ref[r_scale:r_scale + 1, :]) + mod_ref[r_shift:r_shift + 1, :]).astype(bf16)
        h_ref[...] = h
        fv = None
        for j in range(NFS):
            zg = _dg(h, wi_ref[j], NT)
            zu = _dg(h, wi_ref[j + NFS], NT)
            zg_ref[j] = zg.astype(bf16)
            zu_ref[j] = zu.astype(bf16)
            a = zg * _sigmoid(zg) * zu
            part = _dot(a.astype(bf16), wo_ref[j * FS:(j + 1) * FS, :])
            fv = part if fv is None else fv + part
        f_ref[...] = fv.astype(bf16)
        xo_ref[...] = xv + 0.5 * mod_ref[r_gate:r_gate + 1, :] * fv

    row = lambda i: (i, 0)
    tile = lambda i: (0, i, 0)
    shard = jax.ShapeDtypeStruct((NFS, S, FS), bf16)
    return _pcall(
        body, (x, mod, g, wi, wo), name=name, comm=comm,
        grid=(S // tm,),
        in_specs=[pl.BlockSpec((tm, D), row), _full(mod.shape), _full(g.shape), _resident(wi.shape), _resident(wo.shape)],
        out_specs=[pl.BlockSpec((tm, D), row), pl.BlockSpec((tm, D), row),
                   pl.BlockSpec((NFS, tm, FS), tile), pl.BlockSpec((NFS, tm, FS), tile), pl.BlockSpec((tm, D), row)],
        out_shape=[jax.ShapeDtypeStruct((S, D), f32), jax.ShapeDtypeStruct((S, D), bf16), shard, shard,
                   jax.ShapeDtypeStruct((S, D), bf16)],
    )


def ffn_bwd_hidden(dy, zg, zu, mod, wo_t, r_gate, tm, name, comm=None):
    S = dy.shape[0]
    nt = S // tm

    def body(dy_ref, zg_ref, zu_ref, mod_ref, wo_ref, dzg_ref, dzu_ref, dwo_ref, acc_s):
        i = pl.program_id(0)

        @pl.when(i == 0)
        def _():
            acc_s[...] = jnp.zeros_like(acc_s)

        df = (0.5 * mod_ref[r_gate:r_gate + 1, :] * dy_ref[...]).astype(bf16)
        for j in range(NFS):
            zgv = zg_ref[j].astype(f32)
            zuv = zu_ref[j].astype(f32)
            s = _sigmoid(zgv)
            sil = zgv * s
            acc_s[j] += _dg((sil * zuv).astype(bf16), df, TN)
            da = _dg(df, wo_ref[j * FS:(j + 1) * FS, :], NT)
            dzu_ref[j] = (da * sil).astype(bf16)
            dzg_ref[j] = (da * zuv * (s * (1.0 + zgv * (1.0 - s)))).astype(bf16)

        @pl.when(i == nt - 1)
        def _():
            dwo_ref[...] = acc_s[...].astype(bf16)

    row = lambda i: (i, 0)
    tile = lambda i: (0, i, 0)
    shard = jax.ShapeDtypeStruct((NFS, S, FS), bf16)
    tspec = pl.BlockSpec((NFS, tm, FS), tile)
    (dzg, dzu, dwo), comm_outs = _pcall(
        body, (dy, zg, zu, mod, wo_t), name=name, comm=comm,
        grid=(nt,),
        in_specs=[pl.BlockSpec((tm, D), row), tspec, tspec, _full(mod.shape), _resident(wo_t.shape)],
        out_specs=[tspec, tspec, _full((NFS, FS, D))],
        out_shape=[shard, shard, jax.ShapeDtypeStruct((NFS, FS, D), bf16)],
        scratch_shapes=[pltpu.VMEM((NFS, FS, D), f32)],
    )
    return (dzg, dzu, dwo.reshape(F, D)), comm_outs


def ffn_bwd_input(x, dy, dzg, dzu, fo, mod, g, wi_t, rows, tm, name, comm=None):
    S = x.shape[0]
    r_shift, r_scale, r_gate = rows

    def body(x_ref, dy_ref, dzg_ref, dzu_ref, f_ref, mod_ref, g_ref, wi_ref, dx_ref, red_ref):
        @pl.when(pl.program_id(0) == 0)
        def _():
            red_ref[...] = jnp.zeros_like(red_ref)

        dh = _dot(dzg_ref[0], wi_ref[0]) + _dot(dzu_ref[0], wi_ref[NFS])
        for j in range(1, NFS):
            dh = dh + _dot(dzg_ref[j], wi_ref[j]) + _dot(dzu_ref[j], wi_ref[j + NFS])
        dyv = dy_ref[...]
        xh, rstd = _rms_parts(x_ref[...])
        gv = g_ref[...]
        n = xh * gv
        dn = dh * (1.0 + mod_ref[r_scale:r_scale + 1, :])
        red_ref[0:1, :] += _rowsum(dh)
        red_ref[1:2, :] += _rowsum(dh * n)
        red_ref[2:3, :] += _rowsum(0.5 * f_ref[...].astype(f32) * dyv)
        red_ref[3:4, :] += _rowsum(dn * xh)
        dx_ref[...] = dyv + _rms_bwd(dn * gv, xh, rstd)

    row = lambda i: (i, 0)
    tspec = pl.BlockSpec((NFS, tm, FS), lambda i: (0, i, 0))
    return _pcall(
        body, (x, dy, dzg, dzu, fo, mod, g, wi_t), name=name, comm=comm,
        grid=(S // tm,),
        in_specs=[pl.BlockSpec((tm, D), row), pl.BlockSpec((tm, D), row), tspec, tspec, pl.BlockSpec((tm, D), row),
                  _full(mod.shape), _full(g.shape), _resident(wi_t.shape)],
        out_specs=[pl.BlockSpec((tm, D), row), _full((8, D))],
        out_shape=[jax.ShapeDtypeStruct((S, D), f32), jax.ShapeDtypeStruct((8, D), f32)],
    )


def matmul_tn(a, b, M, N, bm, bn, bk, name, a_col_block=0, out_dtype=f32):
    S = b.shape[0]
    nk = S // bk

    def body(a_ref, b_ref, o_ref, acc_s):
        k = pl.program_id(2)

        @pl.when(k == 0)
        def _():
            acc_s[...] = jnp.zeros_like(acc_s)

        acc_s[...] += _dg(a_ref[...].astype(bf16), b_ref[...].astype(bf16), TN)

        @pl.when(k == nk - 1)
        def _():
            o_ref[...] = acc_s[...].astype(out_dtype)

    return pl.pallas_call(
        body, name=name,
        grid=(M // bm, N // bn, nk),
        in_specs=[
            pl.BlockSpec((bk, bm), lambda i, j, k: (k, i + a_col_block)),
            pl.BlockSpec((bk, bn), lambda i, j, k: (k, j)),
        ],
        out_specs=pl.BlockSpec((bm, bn), lambda i, j, k: (i, j)),
        out_shape=jax.ShapeDtypeStruct((M, N), out_dtype),
        scratch_shapes=[pltpu.VMEM((bm, bn), f32)],
        compiler_params=_cp(3),
    )(a, b)


def dwi_pieces(h, dzg, dzu, bk, name, comm=None):
    S = h.shape[0]
    nk = S // bk

    def body(h_ref, g_ref, u_ref, o_ref, acc_s):
        half = pl.program_id(0)
        k = pl.program_id(1)

        @pl.when(k == 0)
        def _():
            acc_s[...] = jnp.zeros_like(acc_s)

        hv = h_ref[...]

        @pl.when(half == 0)
        def _():
            for j in range(NFS):
                acc_s[j] += _dg(g_ref[j], hv, TN)

        @pl.when(half == 1)
        def _():
            for j in range(NFS):
                acc_s[j] += _dg(u_ref[j], hv, TN)

        @pl.when(k == nk - 1)
        def _():
            o_ref[...] = acc_s[...].astype(bf16)

    (out,), comm_outs = _pcall(
        body, (h, dzg, dzu), name=name, comm=comm,
        grid=(2, nk),
        in_specs=[
            pl.BlockSpec((bk, D), lambda half, k: (k, 0)),
            pl.BlockSpec((NFS, bk, FS), lambda half, k: (0, jnp.where(half == 0, k, nk - 1), 0)),
            pl.BlockSpec((NFS, bk, FS), lambda half, k: (0, jnp.where(half == 1, k, 0), 0)),
        ],
        out_specs=[pl.BlockSpec((NFS, FS, D), lambda half, k: (half, 0, 0))],
        out_shape=[jax.ShapeDtypeStruct((2 * NFS, FS, D), bf16)],
        scratch_shapes=[pltpu.VMEM((NFS, FS, D), f32)],
    )
    return out, comm_outs


def mixin_fwd(x1, mod, g, win, wgu, bgate, tm, name):
    S = x1.shape[0]

    def body(x_ref, mod_ref, g_ref, win_ref, wgu_ref, bg_ref, z_ref, la_ref):
        xh, _ = _rms_parts(x_ref[...])
        hv = xh * g_ref[...] * (1.0 + mod_ref[4:5, :]) + mod_ref[3:4, :]
        z = _dg(hv.astype(bf16), win_ref[...], NT)
        z_ref[...] = z
        glr = z[:, DINP - 128:]
        pre = _dot(glr.astype(bf16), wgu_ref[...]) + bg_ref[...]
        la_ref[...] = (jnp.minimum(pre, 0.0) - jnp.log(1.0 + jnp.exp(-jnp.abs(pre)))) * (1.0 / TAU)

    return pl.pallas_call(
        body, name=name,
        grid=(S // tm,),
        in_specs=[pl.BlockSpec((tm, D), lambda i: (i, 0)), _full(mod.shape), _full(g.shape),
                  _full(win.shape), _full(wgu.shape), _full(bgate.shape)],
        out_specs=[pl.BlockSpec((tm, DINP), lambda i: (i, 0)), pl.BlockSpec((tm, DQK), lambda i: (i, 0))],
        out_shape=[jax.ShapeDtypeStruct((S, DINP), f32), jax.ShapeDtypeStruct((S, DQK), f32)],
        compiler_params=_cp(1),
    )(x1, mod, g, win, wgu, bgate)


def mixin_bwd(x1, dres, dzab, dq, dk, dv, dr, dpre, mod, g, win, wgu, tm, name, comm=None):
    S = x1.shape[0]

    def body(x_ref, dres_ref, dzab_ref, dq_ref, dk_ref, dv_ref, dr_ref, dpre_ref, mod_ref, g_ref, win_ref, wgu_ref,
             dx_ref, h_ref, dz_ref, red_ref):
        @pl.when(pl.program_id(0) == 0)
        def _():
            red_ref[...] = jnp.zeros_like(red_ref)

        dglr = _dg(dpre_ref[...].astype(bf16), wgu_ref[...], NT)
        dz = jnp.concatenate([dzab_ref[...], dq_ref[...], dk_ref[...], dv_ref[...], dr_ref[...], dglr], axis=1).astype(bf16)
        dz_ref[...] = dz
        dh = _dot(dz, win_ref[...])
        xh, rstd = _rms_parts(x_ref[...])
        gv = g_ref[...]
        n = xh * gv
        sc = 1.0 + mod_ref[4:5, :]
        h_ref[...] = (n * sc + mod_ref[3:4, :]).astype(bf16)
        dn = dh * sc
        red_ref[0:1, :] += _rowsum(dh)
        red_ref[1:2, :] += _rowsum(dh * n)
        red_ref[2:3, :] += _rowsum(dn * xh)
        dx_ref[...] = dres_ref[...] + _rms_bwd(dn * gv, xh, rstd)

    row = lambda i: (i, 0)
    return _pcall(
        body, (x1, dres, dzab, dq, dk, dv, dr, dpre, mod, g, win, wgu), name=name, comm=comm,
        grid=(S // tm,),
        in_specs=[pl.BlockSpec((tm, D), row), pl.BlockSpec((tm, D), row),
                  pl.BlockSpec((tm, 2 * DC), row), pl.BlockSpec((tm, DQK), row), pl.BlockSpec((tm, DQK), row),
                  pl.BlockSpec((tm, DG), row), pl.BlockSpec((tm, DG), row), pl.BlockSpec((tm, DQK), row),
                  _full(mod.shape), _full(g.shape), _full(win.shape), _full(wgu.shape)],
        out_specs=[pl.BlockSpec((tm, D), row), pl.BlockSpec((tm, D), row), pl.BlockSpec((tm, DINP), row), _full((8, D))],
        out_shape=[jax.ShapeDtypeStruct((S, D), f32), jax.ShapeDtypeStruct((S, D), bf16),
                   jax.ShapeDtypeStruct((S, DINP), bf16), jax.ShapeDtypeStruct((8, D), f32)],
    )


def _glu(zab):
    return zab[:, :DC] * _sigmoid(zab[:, DC:])


def _shift_copies(src_s, dst_s, tc):
    n = tc + HALO - SUBLANES
    for b in range(1, SUBLANES):
        dst_s[b, 0:n, :] = src_s[b:b + n, :]


def _shifted(src_s, dst_s, o, tc):
    b = o % SUBLANES
    a = o - b
    return src_s[a:a + tc, :] if b == 0 else dst_s[b, a:a + tc, :]


def conv_fwd(z, wdw, cpar, tc, name):
    S = z.shape[0]
    nb = tc // HALO

    def body(zc_ref, zp_ref, w_ref, cp_ref, y_ref, yc_ref, u_s, us_s):
        i = pl.program_id(0)
        up = _glu(zp_ref[...])
        u_s[0:HALO, :] = jnp.where(i > 0, up, 0.0)
        u_s[HALO:HALO + tc, :] = _glu(zc_ref[...])
        _shift_copies(u_s, us_s, tc)
        for r in range(0, tc, CONV_ROWS):
            acc = _shifted(u_s, us_s, HALO - (CW - 1) + r, CONV_ROWS) * w_ref[0:1, :]
            for w in range(1, CW):
                acc = acc + _shifted(u_s, us_s, HALO - (CW - 1) + w + r, CONV_ROWS) * w_ref[w:w + 1, :]
            y = acc + cp_ref[0:1, :]
            y_ref[r:r + CONV_ROWS, :] = y
            yc = y - jnp.mean(y, axis=-1, keepdims=True)
            yl = yc * lax.rsqrt(jnp.mean(yc * yc, axis=-1, keepdims=True) + EPS) * cp_ref[1:2, :] + cp_ref[2:3, :]
            yc_ref[r:r + CONV_ROWS, :] = (yl * _sigmoid(yl)).astype(bf16)

    return pl.pallas_call(
        body, name=name,
        grid=(S // tc,),
        in_specs=[pl.BlockSpec((tc, 2 * DC), lambda i: (i, 0)),
                  pl.BlockSpec((HALO, 2 * DC), lambda i: (jnp.maximum(i * nb - 1, 0), 0)),
                  _full(wdw.shape), _full(cpar.shape)],
        out_specs=[pl.BlockSpec((tc, DC), lambda i: (i, 0)), pl.BlockSpec((tc, DC), lambda i: (i, 0))],
        out_shape=[jax.ShapeDtypeStruct((S, DC), f32), jax.ShapeDtypeStruct((S, DC), bf16)],
        scratch_shapes=[pltpu.VMEM((HALO + tc, DC), f32), pltpu.VMEM((SUBLANES, HALO + tc, DC), f32)],
        compiler_params=_cp(1),
    )(z, z, wdw, cpar)


def conv_bwd(z, y, dyc, wdw, cpar, tc, name, comm=None):
    S = z.shape[0]
    nb = tc // HALO
    nt = S // tc
    last_halo = S // HALO - 1

    def body(zc_ref, zp_ref, y_ref, yn_ref, d_ref, dn_ref, w_ref, cp_ref, dz_ref, red_ref, u_s, dy_s, us_s, dys_s):
        i = pl.program_id(0)

        @pl.when(i == 0)
        def _():
            red_ref[...] = jnp.zeros_like(red_ref)

        gl = cp_ref[1:2, :]
        bl = cp_ref[2:3, :]

        def ln_bwd(yv, dv):
            yc = yv - jnp.mean(yv, axis=-1, keepdims=True)
            rstd = lax.rsqrt(jnp.mean(yc * yc, axis=-1, keepdims=True) + EPS)
            yh = yc * rstd
            yl = yh * gl + bl
            s = _sigmoid(yl)
            dyl = dv * (s * (1.0 + yl * (1.0 - s)))
            dyh = dyl * gl
            dyv = rstd * (dyh - jnp.mean(dyh, axis=-1, keepdims=True) - yh * jnp.mean(dyh * yh, axis=-1, keepdims=True))
            return dyv, dyl, yh

        dy_c, dyl_c, yh_c = ln_bwd(y_ref[...], d_ref[...])
        dy_n, _, _ = ln_bwd(yn_ref[...], dn_ref[...])
        dy_s[0:tc, :] = dy_c
        dy_s[tc:tc + HALO, :] = jnp.where(i < nt - 1, dy_n, 0.0)
        u_s[0:HALO, :] = jnp.where(i > 0, _glu(zp_ref[...]), 0.0)
        u_s[HALO:HALO + tc, :] = _glu(zc_ref[...])
        _shift_copies(u_s, us_s, tc)
        _shift_copies(dy_s, dys_s, tc)
        red_ref[32:33, :] += _rowsum(dy_c)
        red_ref[33:34, :] += _rowsum(dyl_c * yh_c)
        red_ref[34:35, :] += _rowsum(dyl_c)
        for r in range(0, tc, CONV_ROWS):
            du = _shifted(dy_s, dys_s, CW - 1 + r, CONV_ROWS) * w_ref[0:1, :]
            for w in range(1, CW):
                du = du + _shifted(dy_s, dys_s, CW - 1 - w + r, CONV_ROWS) * w_ref[w:w + 1, :]
            zc = zc_ref[r:r + CONV_ROWS, :]
            av = zc[:, :DC]
            sb = _sigmoid(zc[:, DC:])
            dz_ref[r:r + CONV_ROWS, :] = jnp.concatenate([du * sb, du * av * sb * (1.0 - sb)], axis=1)
        for w in range(CW):
            part = None
            for r in range(0, tc, CONV_ROWS):
                prod = _shifted(u_s, us_s, HALO - (CW - 1) + w + r, CONV_ROWS) * dy_s[r:r + CONV_ROWS, :]
                fold = jnp.sum(prod.reshape(CONV_ROWS // SUBLANES, SUBLANES, DC), axis=0)
                part = fold if part is None else part + fold
            red_ref[w:w + 1, :] += _rowsum(part)

    cur = lambda i: (i, 0)
    nxt = lambda i: (jnp.minimum((i + 1) * nb, last_halo), 0)
    return _pcall(
        body, (z, z, y, y, dyc, dyc, wdw, cpar), name=name, comm=comm,
        grid=(nt,),
        in_specs=[pl.BlockSpec((tc, 2 * DC), cur),
                  pl.BlockSpec((HALO, 2 * DC), lambda i: (jnp.maximum(i * nb - 1, 0), 0)),
                  pl.BlockSpec((tc, DC), cur), pl.BlockSpec((HALO, DC), nxt),
                  pl.BlockSpec((tc, DC), cur), pl.BlockSpec((HALO, DC), nxt),
                  _full(wdw.shape), _full(cpar.shape)],
        out_specs=[pl.BlockSpec((tc, 2 * DC), cur), _full((40, DC))],
        out_shape=[jax.ShapeDtypeStruct((S, 2 * DC), f32), jax.ShapeDtypeStruct((40, DC), f32)],
        scratch_shapes=[pltpu.VMEM((HALO + tc, DC), f32), pltpu.VMEM((tc + HALO, DC), f32),
                        pltpu.VMEM((SUBLANES, HALO + tc, DC), f32), pltpu.VMEM((SUBLANES, HALO + tc, DC), f32)],
    )


def _gla_consts():
    r = lax.broadcasted_iota(jnp.int32, (CH, CH), 0)
    c = lax.broadcasted_iota(jnp.int32, (CH, CH), 1)
    tril = r >= c
    lane = lax.broadcasted_iota(jnp.int32, (CH, DQK), 1)
    masks = [(lane >= h * DK) & (lane < (h + 1) * DK) for h in range(NH)]
    r4 = lax.broadcasted_iota(jnp.int32, (DQK, DQK), 0)
    c4 = lax.broadcasted_iota(jnp.int32, (DQK, DQK), 1)
    eye4 = (r4 == c4).astype(f32)
    rs = lax.broadcasted_iota(jnp.int32, (DQK, CH), 0) & (CH - 1)
    tril4 = rs >= lax.broadcasted_iota(jnp.int32, (DQK, CH), 1)
    return tril, tril4, masks, eye4


def _stack(xv, masks):
    return jnp.concatenate([jnp.where(m, xv, 0.0) for m in masks], axis=0)


def _unstack(rv, masks):
    out = jnp.where(masks[0], rv[0:CH, :], 0.0)
    for h in range(1, NH):
        out = out + jnp.where(masks[h], rv[h * CH:(h + 1) * CH, :], 0.0)
    return out


def _vstack(xv):
    return jnp.concatenate([xv[:, h * DV:(h + 1) * DV] for h in range(NH)], axis=0)


def _vunstack(xv):
    return jnp.concatenate([xv[h * CH:(h + 1) * CH, :] for h in range(NH)], axis=1)


def _gla_chunk_fwd(lac, qc, kc, vc, s_all, tril, masks, tril4):
    lmat = tril.astype(f32)
    bc = jnp.dot(lmat, lac, preferred_element_type=f32, precision=HIGHEST)
    bend = bc[CH - 1:CH, :]
    eb = jnp.exp(bc)
    enb = jnp.exp(-bc)
    ed = jnp.exp(bend - bc)
    qh = qc * (DK ** -0.5)
    qf = qh * eb
    qn = qh * enb
    kn = kc * enb
    kp = kc * eb
    kd = kc * ed
    qf_s = _stack(qf, masks).astype(bf16)
    qn_s = _stack(qn, masks).astype(bf16)
    kn_b = kn.astype(bf16)
    kp_b = kp.astype(bf16)
    attf = _dg(qf_s, kn_b, NT)
    attb = _dg(qn_s, kp_b, NT)
    a_s = jnp.where(tril4, attf, attb)
    a_b = a_s.astype(bf16)
    v_b = vc.astype(bf16)
    intra = jnp.concatenate(
        [_dot(a_b[h * CH:(h + 1) * CH, :], v_b[:, h * DV:(h + 1) * DV]) for h in range(NH)], axis=0)
    o_s = intra + _dot(qf_s, s_all.astype(bf16))
    return dict(bc=bc, bend=bend, eb=eb, enb=enb, ed=ed, qf=qf, qn=qn, kn=kn, kp=kp, kd=kd,
                qf_s=qf_s, qn_s=qn_s, kn_b=kn_b, kp_b=kp_b, a_b=a_b, v_b=v_b, o_s=o_s)


def _col_from_row(row, eye4):
    return jnp.sum(eye4 * row, axis=1, keepdims=True)


def _row_from_col(col, eye4):
    return jnp.sum(eye4 * col, axis=0, keepdims=True)


def gla_fwd(z, la, gn_s, tg, name):
    S = z.shape[0]
    nc = tg // CH

    def body(q_ref, k_ref, v_ref, r_ref, la_ref, gn_ref, yg_ref, sp_ref, st):
        @pl.when(pl.program_id(0) == 0)
        def _():
            st[...] = jnp.zeros_like(st)

        tril, tril4, masks, eye4 = _gla_consts()

        def chunk(c, carry):
            r0 = pl.multiple_of(c * CH, CH)
            s0 = pl.multiple_of(c * DQK, DQK)
            s_all = st[...]
            sp_ref[pl.ds(s0, DQK), :] = s_all
            vc = v_ref[pl.ds(r0, CH), :]
            t = _gla_chunk_fwd(la_ref[pl.ds(r0, CH), :], q_ref[pl.ds(r0, CH), :], k_ref[pl.ds(r0, CH), :], vc,
                               s_all, tril, masks, tril4)
            u_all = _dg(_stack(t["kd"], masks).astype(bf16), _vstack(vc).astype(bf16), TN)
            st[...] = _col_from_row(jnp.exp(t["bend"]), eye4) * s_all + u_all
            o_s = t["o_s"]
            on = o_s * lax.rsqrt(jnp.mean(o_s * o_s, axis=-1, keepdims=True) + EPS) * gn_ref[...]
            rc = r_ref[pl.ds(r0, CH), :]
            yg_ref[pl.ds(r0, CH), :] = (_vunstack(on) * (rc * _sigmoid(rc))).astype(bf16)
            return carry

        lax.fori_loop(0, nc, chunk, 0, unroll=True)

    return pl.pallas_call(
        body, name=name,
        grid=(S // tg,),
        in_specs=[pl.BlockSpec((tg, DQK), lambda i: (i, 4)), pl.BlockSpec((tg, DQK), lambda i: (i, 5)),
                  pl.BlockSpec((tg, DG), lambda i: (i, 3)), pl.BlockSpec((tg, DG), lambda i: (i, 4)),
                  pl.BlockSpec((tg, DQK), lambda i: (i, 0)), _full(gn_s.shape)],
        out_specs=[pl.BlockSpec((tg, DG), lambda i: (i, 0)), pl.BlockSpec((nc * DQK, DV), lambda i: (i, 0))],
        out_shape=[jax.ShapeDtypeStruct((S, DG), bf16), jax.ShapeDtypeStruct((S // CH * DQK, DV), f32)],
        scratch_shapes=[pltpu.VMEM((DQK, DV), f32)],
        compiler_params=_cp(1),
    )(z, z, z, z, la, gn_s)


def gla_bwd(z, la, sprev, dyg, gn_s, tg, name, comm=None):
    S = z.shape[0]
    nc = tg // CH
    nt = S // tg

    def body(q_ref, k_ref, v_ref, r_ref, la_ref, sp_ref, dy_ref, gn_ref,
             dq_ref, dk_ref, dv_ref, dr_ref, dpre_ref, redg_ref, redb_ref, gs):
        @pl.when(pl.program_id(0) == 0)
        def _():
            gs[...] = jnp.zeros_like(gs)
            redg_ref[...] = jnp.zeros_like(redg_ref)
            redb_ref[...] = jnp.zeros_like(redb_ref)

        tril, tril4, masks, eye4 = _gla_consts()
        umat = (lax.broadcasted_iota(jnp.int32, (CH, CH), 0) <= lax.broadcasted_iota(jnp.int32, (CH, CH), 1)).astype(f32)
        last_row = lax.broadcasted_iota(jnp.int32, (CH, DQK), 0) == CH - 1

        def chunk(tt, carry):
            c = nc - 1 - tt
            r0 = pl.multiple_of(c * CH, CH)
            s0 = pl.multiple_of(c * DQK, DQK)
            s_all = sp_ref[pl.ds(s0, DQK), :]
            lac = la_ref[pl.ds(r0, CH), :]
            vc = v_ref[pl.ds(r0, CH), :]
            rc = r_ref[pl.ds(r0, CH), :]
            t = _gla_chunk_fwd(lac, q_ref[pl.ds(r0, CH), :], k_ref[pl.ds(r0, CH), :], vc, s_all, tril, masks, tril4)
            g_all = gs[...]
            g_b = g_all.astype(bf16)
            s_b = s_all.astype(bf16)
            o_s = t["o_s"]
            rstd = lax.rsqrt(jnp.mean(o_s * o_s, axis=-1, keepdims=True) + EPS)
            oh = o_s * rstd
            gnv = gn_ref[...]
            sr = _sigmoid(rc)
            dyv = dy_ref[pl.ds(r0, CH), :]
            dr_ref[pl.ds(r0, CH), :] = dyv * _vunstack(oh * gnv) * (sr * (1.0 + rc * (1.0 - sr)))
            don = _vstack(dyv * (rc * sr))
            redg_ref[...] += don * oh
            doh = don * gnv
            do_s = rstd * (doh - oh * jnp.mean(doh * oh, axis=-1, keepdims=True))
            do_b = do_s.astype(bf16)
            v_b = t["v_b"]
            vst_b = _vstack(vc).astype(bf16)
            kd_s = _stack(t["kd"], masks).astype(bf16)
            da_s = jnp.concatenate(
                [_dg(do_b[h * CH:(h + 1) * CH, :], v_b[:, h * DV:(h + 1) * DV], NT) for h in range(NH)], axis=0)
            a_b = t["a_b"]
            dv_s = jnp.concatenate(
                [_dg(a_b[h * CH:(h + 1) * CH, :], do_b[h * CH:(h + 1) * CH, :], TN) for h in range(NH)], axis=0)
            dv_s = dv_s + _dot(kd_s, g_b)
            dv_ref[pl.ds(r0, CH), :] = _vunstack(dv_s)
            gend = jnp.exp(t["bend"])
            gcol = _col_from_row(gend, eye4)
            gs[...] = gcol * g_all + _dg(t["qf_s"], do_b, TN)
            dgcol = jnp.sum(g_all * s_all, axis=1, keepdims=True)
            dbend = _row_from_col(dgcol * gcol, eye4)
            dkd = _unstack(_dg(vst_b, g_b, NT), masks)
            daf = jnp.where(tril4, da_s, 0.0).astype(bf16)
            dab = jnp.where(tril4, 0.0, da_s).astype(bf16)
            dqf = _unstack(_dot(daf, t["kn_b"]) + _dg(do_b, s_b, NT), masks)
            dqn = _unstack(_dot(dab, t["kp_b"]), masks)
            dkn = _dg(daf, t["qf_s"], TN)
            dkp = _dg(dab, t["qn_s"], TN)
            dq_ref[pl.ds(r0, CH), :] = (dqf * t["eb"] + dqn * t["enb"]) * (DK ** -0.5)
            dk_ref[pl.ds(r0, CH), :] = dkn * t["enb"] + dkp * t["eb"] + dkd * t["ed"]
            dkd_kd = dkd * t["kd"]
            dbc = dqf * t["qf"] - dqn * t["qn"] - dkn * t["kn"] + dkp * t["kp"] - dkd_kd
            dbc = dbc + jnp.where(last_row, _rowsum(dkd_kd) + dbend, 0.0)
            dla = jnp.dot(umat, dbc, preferred_element_type=f32, precision=HIGHEST)
            dpre = dla * (1.0 / TAU) * (1.0 - jnp.exp(TAU * lac))
            dpre_ref[pl.ds(r0, CH), :] = dpre
            redb_ref[...] += dpre
            return carry

        lax.fori_loop(0, nc, chunk, 0, unroll=True)

    rev = lambda col: (lambda i: (nt - 1 - i, col))
    return _pcall(
        body, (z, z, z, z, la, sprev, dyg, gn_s), name=name, comm=comm,
        grid=(nt,),
        in_specs=[pl.BlockSpec((tg, DQK), rev(4)), pl.BlockSpec((tg, DQK), rev(5)),
                  pl.BlockSpec((tg, DG), rev(3)), pl.BlockSpec((tg, DG), rev(4)),
                  pl.BlockSpec((tg, DQK), rev(0)), pl.BlockSpec((nc * DQK, DV), rev(0)),
                  pl.BlockSpec((tg, DG), rev(0)), _full(gn_s.shape)],
        out_specs=[pl.BlockSpec((tg, DQK), rev(0)), pl.BlockSpec((tg, DQK), rev(0)),
                   pl.BlockSpec((tg, DG), rev(0)), pl.BlockSpec((tg, DG), rev(0)), pl.BlockSpec((tg, DQK), rev(0)),
                   _full((DQK, DV)), _full((CH, DQK))],
        out_shape=[jax.ShapeDtypeStruct((S, DQK), f32), jax.ShapeDtypeStruct((S, DQK), f32),
                   jax.ShapeDtypeStruct((S, DG), f32), jax.ShapeDtypeStruct((S, DG), f32), jax.ShapeDtypeStruct((S, DQK), f32),
                   jax.ShapeDtypeStruct((DQK, DV), f32), jax.ShapeDtypeStruct((CH, DQK), f32)],
        scratch_shapes=[pltpu.VMEM((DQK, DV), f32)],
    )


def mixout_fwd(x1, yc, yg, mod, wout, tm, name):
    S = x1.shape[0]

    def body(x_ref, yc_ref, yg_ref, mod_ref, w_ref, xo_ref):
        mixo = _dot(yc_ref[...], w_ref[0:DC, :]) + _dot(yg_ref[...], w_ref[DC:DC + DG, :])
        xo_ref[...] = x_ref[...] + mod_ref[5:6, :] * mixo

    row = lambda i: (i, 0)
    return pl.pallas_call(
        body, name=name,
        grid=(S // tm,),
        in_specs=[pl.BlockSpec((tm, D), row), pl.BlockSpec((tm, DC), row), pl.BlockSpec((tm, DG), row),
                  _full(mod.shape), _full(wout.shape)],
        out_specs=pl.BlockSpec((tm, D), row),
        out_shape=jax.ShapeDtypeStruct((S, D), f32),
        compiler_params=_cp(1),
    )(x1, yc, yg, mod, wout)


def mixout_bwd(dx2, yc, yg, mod, wout, tm, name):
    S = dx2.shape[0]

    def body(dx_ref, yc_ref, yg_ref, mod_ref, w_ref, dm_ref, dyc_ref, dyg_ref, red_ref):
        @pl.when(pl.program_id(0) == 0)
        def _():
            red_ref[...] = jnp.zeros_like(red_ref)

        dxv = dx_ref[...]
        mixo = _dot(yc_ref[...], w_ref[0:DC, :]) + _dot(yg_ref[...], w_ref[DC:DC + DG, :])
        red_ref[0:1, :] += _rowsum(dxv * mixo)
        dm = (mod_ref[5:6, :] * dxv).astype(bf16)
        dm_ref[...] = dm
        dycat = _dg(dm, w_ref[...], NT)
        dyc_ref[...] = dycat[:, :DC]
        dyg_ref[...] = dycat[:, DC:]

    row = lambda i: (i, 0)
    return pl.pallas_call(
        body, name=name,
        grid=(S // tm,),
        in_specs=[pl.BlockSpec((tm, D), row), pl.BlockSpec((tm, DC), row), pl.BlockSpec((tm, DG), row),
                  _full(mod.shape), _full(wout.shape)],
        out_specs=[pl.BlockSpec((tm, D), row), pl.BlockSpec((tm, DC), row), pl.BlockSpec((tm, DG), row), _full((8, D))],
        out_shape=[jax.ShapeDtypeStruct((S, D), bf16), jax.ShapeDtypeStruct((S, DC), f32),
                   jax.ShapeDtypeStruct((S, DG), f32), jax.ShapeDtypeStruct((8, D), f32)],
        compiler_params=_cp(1),
    )(dx2, yc, yg, mod, wout)


def final_fwd_bwd(x, tgt, fmod, g, tm, name):
    S = x.shape[0]

    def body(x_ref, t_ref, fm_ref, g_ref, dx_ref, red_ref):
        @pl.when(pl.program_id(0) == 0)
        def _():
            red_ref[...] = jnp.zeros_like(red_ref)

        xh, rstd = _rms_parts(x_ref[...])
        gv = g_ref[...]
        n = xh * gv
        sc = 1.0 + fm_ref[1:2, :]
        e = n * sc + fm_ref[0:1, :] - t_ref[...]
        red_ref[0:1, :] += _rowsum(e * e) * (0.5 / D)
        dy = e * (1.0 / D)
        dn = dy * sc
        red_ref[1:2, :] += _rowsum(dy)
        red_ref[2:3, :] += _rowsum(dy * n)
        red_ref[3:4, :] += _rowsum(dn * xh)
        dx_ref[...] = _rms_bwd(dn * gv, xh, rstd)

    row = lambda i: (i, 0)
    return pl.pallas_call(
        body, name=name,
        grid=(S // tm,),
        in_specs=[pl.BlockSpec((tm, D), row), pl.BlockSpec((tm, D), row), _full(fmod.shape), _full(g.shape)],
        out_specs=[pl.BlockSpec((tm, D), row), _full((8, D))],
        out_shape=[jax.ShapeDtypeStruct((S, D), f32), jax.ShapeDtypeStruct((8, D), f32)],
        compiler_params=_cp(1),
    )(x, tgt, fmod, g)


def ada_fwd(c_all, w, b, name):
    n = w.shape[1]

    def body(c_ref, w_ref, b_ref, o_ref):
        cv = c_ref[...]
        o_ref[...] = jnp.dot(cv * _sigmoid(cv), w_ref[...], preferred_element_type=f32, precision=HIGHEST) + b_ref[...]

    return pl.pallas_call(
        body, name=name,
        in_specs=[_full(c_all.shape), _full(w.shape), _full(b.shape)],
        out_specs=_full((N_DEV, n)),
        out_shape=jax.ShapeDtypeStruct((N_DEV, n), f32),
        grid=(1,),
        compiler_params=_cp(1),
    )(c_all, w, b)


def ada_wgrad(c_all_t, dm, name):
    n = dm.shape[1]

    def body(c_ref, d_ref, o_ref):
        cv = c_ref[...]
        o_ref[...] = jnp.dot(cv * _sigmoid(cv), d_ref[...], preferred_element_type=f32, precision=HIGHEST)

    return pl.pallas_call(
        body, name=name,
        in_specs=[_full(c_all_t.shape), _full(dm.shape)],
        out_specs=_full((D, n)),
        out_shape=jax.ShapeDtypeStruct((D, n), f32),
        grid=(1,),
        compiler_params=_cp(1),
    )(c_all_t, dm)


def _adam_math(gv, wv, mv, vv):
    m = ADAM_B1 * mv + (1.0 - ADAM_B1) * gv
    v = ADAM_B2 * vv + (1.0 - ADAM_B2) * (gv * gv)
    m_hat = m / (1.0 - ADAM_B1 ** ADAM_STEP)
    v_hat = v / (1.0 - ADAM_B2 ** ADAM_STEP)
    delta = -ADAM_LR * (m_hat / (jnp.sqrt(v_hat) + ADAM_EPS) + ADAM_WD * wv)
    return delta, m, v


def adam_parts(parts, w, m, v, tr, name, comm=None):
    L, R, C = w.shape
    nt = R // tr

    def body(*refs):
        p_refs = refs[:L]
        w_ref, m_ref, v_ref, g_ref, d_ref, mo_ref, vo_ref = refs[L:]
        lyr = pl.program_id(0)
        for l in range(L):
            @pl.when(lyr == l)
            def _(p_ref=p_refs[l]):
                gv = p_ref[0].astype(f32)
                for k in range(1, N_DEV):
                    gv = gv + p_ref[k].astype(f32)
                g_ref[...] = gv
                d_ref[...], mo_ref[...], vo_ref[...] = _adam_math(gv, w_ref[...], m_ref[...], v_ref[...])

    def part_spec(l):
        return pl.BlockSpec((N_DEV, tr, C), lambda lyr, i: (0, jnp.where(lyr == l, i, jnp.where(lyr < l, 0, nt - 1)), 0))

    spec = pl.BlockSpec((None, tr, C), lambda lyr, i: (lyr, i, 0))
    shp = jax.ShapeDtypeStruct((L, R, C), f32)
    return _pcall(
        body, (*parts, w, m, v), name=name, comm=comm,
        grid=(L, nt),
        in_specs=[part_spec(l) for l in range(L)] + [spec, spec, spec],
        out_specs=[spec, spec, spec, spec],
        out_shape=[shp, shp, shp, shp],
    )


def adam_plain(gr, w, m, v, tr, name):
    R, C = w.shape

    def body(g_ref, w_ref, m_ref, v_ref, d_ref, mo_ref, vo_ref):
        d_ref[...], mo_ref[...], vo_ref[...] = _adam_math(g_ref[...], w_ref[...], m_ref[...], v_ref[...])

    spec = pl.BlockSpec((tr, C), lambda i: (i, 0))
    shp = jax.ShapeDtypeStruct((R, C), f32)
    return pl.pallas_call(
        body, name=name,
        grid=(R // tr,),
        in_specs=[spec, spec, spec, spec],
        out_specs=[spec, spec, spec],
        out_shape=[shp, shp, shp],
        compiler_params=_cp(1),
    )(gr, w, m, v)


def sum8(parts, name):
    _, R, C = parts.shape

    def body(p_ref, o_ref):
        acc = p_ref[0]
        for k in range(1, N_DEV):
            acc = acc + p_ref[k]
        o_ref[...] = acc

    return pl.pallas_call(
        body, name=name,
        grid=(1,),
        in_specs=[_full(parts.shape)],
        out_specs=_full((R, C)),
        out_shape=jax.ShapeDtypeStruct((R, C), f32),
        compiler_params=_cp(1),
    )(parts)


def _place():
    return lax.axis_index("x"), lax.axis_index("y"), lax.axis_index("c")


def _gather_steps(ins, outs, send_sems, recv_sems, local_sems, place):
    n = len(ins)
    x, y, c = place
    me, sibling = (x, y, c), (x, y, 1 - c)
    chips = [(1 - x, y), (x, 1 - y), (1 - x, 1 - y)]

    def slot(a, p):
        return outs[a].at[4 * p[0] + 2 * p[1] + p[2]]

    def copy(a, k, block, to, src=None):
        return pltpu.make_async_remote_copy(
            src_ref=slot(a, block) if src is None else src, dst_ref=slot(a, block),
            send_sem=send_sems.at[a * 7 + k], recv_sem=recv_sems.at[a * 7 + k],
            device_id=to, device_id_type=MESH)

    def mine():
        return [pltpu.make_async_copy(ins[a], slot(a, me), local_sems.at[a]) for a in range(n)]

    def first():
        cps = []
        for a in range(n):
            cps.append(copy(a, 0, me, sibling, src=ins[a]))
            cps += [copy(a, 1 + j, me, (*chip, c), src=ins[a]) for j, chip in enumerate(chips)]
        return cps

    def start():
        for cp in mine() + first():
            cp.start()

    def forward():
        for j, chip in enumerate(chips):
            for a in range(n):
                copy(a, 1 + j, (*chip, c), me).wait_recv()
                copy(a, 4 + j, (*chip, c), sibling).start()

    def finish():
        for a in range(n):
            copy(a, 0, sibling, me).wait_recv()
            for j, chip in enumerate(chips):
                copy(a, 4 + j, (*chip, 1 - c), me).wait_recv()
        for cp in first() + [copy(a, 4 + j, (*chip, c), sibling) for j, chip in enumerate(chips) for a in range(n)]:
            cp.wait_send()
        for cp in mine():
            cp.wait()

    return start, forward, finish


def _exchange_steps(ins, outs, send_sems, recv_sems, local_sems, place):
    n = len(ins)
    x, y, c = place
    me_i = 4 * x + 2 * y + c

    def mine():
        return [pltpu.make_async_copy(ins[a].at[me_i], outs[a].at[me_i], local_sems.at[a]) for a in range(n)]

    def copies(receiving):
        cps = []
        for k in range(1, N_DEV):
            px = 1 - x if (k >> 2) & 1 else x
            py = 1 - y if (k >> 1) & 1 else y
            pc = 1 - c if k & 1 else c
            p_i = 4 * px + 2 * py + pc
            for a in range(n):
                sem = a * 7 + k - 1
                cps.append(pltpu.make_async_remote_copy(
                    src_ref=ins[a].at[p_i], dst_ref=outs[a].at[p_i if receiving else me_i],
                    send_sem=send_sems.at[sem], recv_sem=recv_sems.at[sem],
                    device_id=(px, py, pc), device_id_type=MESH))
        return cps

    def start():
        for cp in mine() + copies(False):
            cp.start()

    def finish():
        for cp in copies(True):
            cp.wait_recv()
        for cp in copies(False):
            cp.wait_send()
        for cp in mine():
            cp.wait()

    return start, None, finish


_COMM_STEPS = {"gather": _gather_steps, "exchange": _exchange_steps}


def _comm_out_shapes(kind, arrs):
    if kind == "gather":
        return [jax.ShapeDtypeStruct((N_DEV,) + a.shape, a.dtype) for a in arrs]
    return [jax.ShapeDtypeStruct(a.shape, a.dtype) for a in arrs]


def _comm_sems(n):
    return [pltpu.SemaphoreType.DMA((7 * n,)), pltpu.SemaphoreType.DMA((7 * n,)), pltpu.SemaphoreType.DMA((n,))]


def _pcall(body, args, *, name, grid, in_specs, out_specs, out_shape, scratch_shapes=(), comm=None):
    in_specs, out_specs, out_shape = list(in_specs), list(out_specs), list(out_shape)
    scratch_shapes = list(scratch_shapes)
    cparams = _cp(len(grid))
    if comm is None:
        outs = pl.pallas_call(body, name=name, grid=grid, in_specs=in_specs, out_specs=out_specs, out_shape=out_shape,
                              scratch_shapes=scratch_shapes, compiler_params=cparams)(*args)
        return list(outs), []
    kind, arrs = comm
    nc, n_in, n_out, n_scr = len(arrs), len(in_specs), len(out_specs), len(scratch_shapes)
    total = 1
    for gdim in grid:
        total *= gdim
    forward_step = (total * 3) // 4

    def hosted(*refs):
        core_in, c_in = refs[:n_in], refs[n_in:n_in + nc]
        core_out = refs[n_in + nc:n_in + nc + n_out]
        c_out = refs[n_in + nc + n_out:n_in + 2 * nc + n_out]
        rest = refs[n_in + 2 * nc + n_out:]
        step = pl.program_id(0)
        for ax in range(1, len(grid)):
            step = step * grid[ax] + pl.program_id(ax)
        start, forward, finish = _COMM_STEPS[kind](c_in, c_out, *rest[n_scr:], _place())
        pl.when(step == 0)(start)
        if forward is not None:
            pl.when(step == forward_step)(forward)
        body(*core_in, *core_out, *rest[:n_scr])
        pl.when(step == total - 1)(finish)

    any_spec = pl.BlockSpec(memory_space=pl.ANY)
    outs = pl.pallas_call(
        hosted, name=name, grid=grid,
        in_specs=in_specs + [any_spec] * nc,
        out_specs=out_specs + [any_spec] * nc,
        out_shape=out_shape + _comm_out_shapes(kind, arrs),
        scratch_shapes=scratch_shapes + _comm_sems(nc),
        compiler_params=cparams)(*args, *arrs)
    return list(outs[:n_out]), list(outs[n_out:])


def _comm_call(kind, arrs, name):
    n = len(arrs)

    def body(*refs):
        start, forward, finish = _COMM_STEPS[kind](refs[:n], refs[n:2 * n], *refs[2 * n:], _place())
        start()
        if forward is not None:
            forward()
        finish()

    any_spec = pl.BlockSpec(memory_space=pl.ANY)
    return pl.pallas_call(
        body, name=name,
        in_specs=[any_spec] * n, out_specs=[any_spec] * n,
        out_shape=_comm_out_shapes(kind, arrs), scratch_shapes=_comm_sems(n),
    )(*arrs)


def all_gather(arrs, name):
    return _comm_call("gather", arrs, name)


def all_to_all(arrs, name):
    return _comm_call("exchange", arrs, name)


def _tiles(S):
    t = min(512, S)
    return dict(ffn=min(256, S), row=t, conv=t, gla=t, bk=min(1024, S), bk_ffn=t)


BIG = ("wi1", "wo1", "win", "wout", "wi2", "wo2")


def _col_shards_to_full(gathered):
    n, r, c = gathered.shape
    return jnp.transpose(gathered, (1, 0, 2)).reshape(r, n * c)


def _win_full(win_a):
    return _pad_rows(win_a.reshape(DIN, D), DINP)


def train_pass(x, tgt, mods, fmod, sh, ws, wi1_first, wo1_first):
    S = x.shape[0]
    T = _tiles(S)
    bk = T["bk"]
    full = [dict() for _ in range(DEPTH)]
    full[0]["wi1"], full[0]["wo1"] = wi1_first, wo1_first.reshape(F, D)
    saved = []
    xc = x
    for l in range(DEPTH):
        w, fw = ws[f"L{l}"], full[l]
        x0 = xc
        names = ("win", "wout", "wi2", "wo2") if l == 0 else ("wi2", "wo2")
        (x1, h1f, zg1, zu1, f1), got = ffn_fwd(x0, mods[l], w["g1"], fw["wi1"], fw["wo1"], (0, 1, 2), T["ffn"], f"ffn1_fwd_{l}",
                                          comm=("gather", [sh[n][l] for n in names]))
        fw.update(zip(names, got))
        if l == 0:
            fw["win"], fw["wout"] = _win_full(fw["win"]), fw["wout"].reshape(D, D)
        fw["wo2"] = fw["wo2"].reshape(F, D)
        z, la = mixin_fwd(x1, mods[l], w["g2"], fw["win"], w["wgu"], w["bgate"], T["row"], f"mixin_fwd_{l}")
        y, yc = conv_fwd(z, w["wdw"], w["cpar"], T["conv"], f"conv_fwd_{l}")
        yg, sprev = gla_fwd(z, la, w["gn_s"], T["gla"], f"gla_fwd_{l}")
        x2 = mixout_fwd(x1, yc, yg, mods[l], fw["wout"], T["row"], f"mixout_fwd_{l}")
        names = ("wi1", "wo1", "win", "wout") if l + 1 < DEPTH else ()
        (x3, h2f, zg2, zu2, f2), got = ffn_fwd(x2, mods[l], w["g3"], fw["wi2"], fw["wo2"], (6, 7, 8), T["ffn"], f"ffn2_fwd_{l}",
                                          comm=("gather", [sh[n][l + 1] for n in names]) if names else None)
        if names:
            nx = full[l + 1]
            nx["wi1"], nx["wo1"], nx["win"], nx["wout"] = got[0], got[1].reshape(F, D), _win_full(got[2]), got[3].reshape(D, D)
        saved.append(dict(x0=x0, x1=x1, x2=x2, h1f=h1f, zg1=zg1, zu1=zu1, f1=f1, h2f=h2f, zg2=zg2, zu2=zu2, f2=f2,
                          z=z, la=la, y=y, yc=yc, yg=yg, sprev=sprev))
        xc = x3

    dx, redf = final_fwd_bwd(xc, tgt, fmod, ws["gf"], T["row"], "loss_head")
    loss_lanes = redf[0]
    dfmod = redf[1:3]
    grads = {"gf": redf[3]}
    dmods = [None] * DEPTH
    recv = {n: [None] * DEPTH for n in BIG}

    def ffn_backward(xin, dy, h, zg, zu, fo, gain, wi, wo, rows, l, tag, ride=None):
        (dzg, dzu, p_wo), got_ride = ffn_bwd_hidden(dy, zg, zu, mods[l], wo, rows[2], T["ffn"], f"{tag}_bwd_hidden_{l}",
                                                    comm=("exchange", ride) if ride else None)
        p_wi, (r_wo,) = dwi_pieces(h, dzg, dzu, T["bk_ffn"], f"d{tag}_wi_{l}",
                                   comm=("exchange", [p_wo.reshape(N_DEV, F // N_DEV, D)]))
        (dxin, red), (r_wi,) = ffn_bwd_input(xin, dy, dzg, dzu, fo, mods[l], gain, wi, rows, T["ffn"],
                                             f"{tag}_bwd_input_{l}", comm=("exchange", [p_wi]))
        return dxin, red, r_wi, r_wo, got_ride

    for l in reversed(range(DEPTH)):
        w, fw, sv = ws[f"L{l}"], full[l], saved[l]
        g = {}
        dx2, red3, recv["wi2"][l], recv["wo2"][l], _ = ffn_backward(
            sv["x2"], dx, sv["h2f"], sv["zg2"], sv["zu2"], sv["f2"], w["g3"], fw["wi2"], fw["wo2"], (6, 7, 8), l, "ffn2")
        dmix, dyc, dyg, red_o = mixout_bwd(dx2, sv["yc"], sv["yg"], mods[l], fw["wout"], T["row"], f"mixout_bwd_{l}")
        p_wout = jnp.concatenate([matmul_tn(sv["yc"], dmix, DC, D, DC, D, bk, f"dwout_c_{l}", out_dtype=bf16),
                                  matmul_tn(sv["yg"], dmix, DG, D, DG, D, bk, f"dwout_g_{l}", out_dtype=bf16)], axis=0)
        (dq, dk, dv, dr, dpre, redg, redb), (recv["wout"][l],) = gla_bwd(
            sv["z"], sv["la"], sv["sprev"], dyg, w["gn_s"], T["gla"], f"gla_bwd_{l}",
            comm=("exchange", [p_wout.reshape(N_DEV, D // N_DEV, D)]))
        (dzab, redc), _ = conv_bwd(sv["z"], sv["y"], dyc, w["wdw"], w["cpar"], T["conv"], f"conv_bwd_{l}")
        (dx1, h2, dz, red2), _ = mixin_bwd(sv["x1"], dx2, dzab, dq, dk, dv, dr, dpre, mods[l], w["g2"], fw["win"], w["wgu"],
                                           T["row"], f"mixin_bwd_{l}")
        dwin_t = matmul_tn(dz, h2, DINP, D, DINP, D, bk, f"dwin_{l}", out_dtype=bf16)
        p_win = dwin_t[:DIN].reshape(N_DEV, DIN // N_DEV, D)
        g["wgu"] = matmul_tn(sv["z"], dpre, 128, DQK, 128, DQK, bk, f"dwgu_{l}", a_col_block=(DINP - 128) // 128)[:GR]
        g["bgate"] = jnp.sum(redb, axis=0)
        g["gn"] = jnp.sum(redg.reshape(NH, CH, DV), axis=1)
        g["wdw"] = redc[:CW]
        g["bdw"], g["gln"], g["bln"] = redc[32], redc[33], redc[34]
        dx0, red1, recv["wi1"][l], recv["wo1"][l], (recv["win"][l],) = ffn_backward(
            sv["x0"], dx1, sv["h1f"], sv["zg1"], sv["zu1"], sv["f1"], w["g1"], fw["wi1"], fw["wo1"], (0, 1, 2), l, "ffn1",
            ride=[p_win])
        g["g1"], g["g2"], g["g3"] = red1[3], red2[2], red3[3]
        dmods[l] = jnp.stack([red1[0], red1[1], red1[2], red2[0], red2[1], red_o[0], red3[0], red3[1], red3[2]], axis=0)
        grads[f"L{l}"] = g
        dx = dx0
    return loss_lanes, dx, grads, dmods, dfmod, recv


def _pad_rows(a, rows):
    return jnp.pad(a, ((0, rows - a.shape[0]), (0, 0)))


def kernel(x, c, w_ada, b_ada, g_norm_ffn1, w_ffn1_in, w_ffn1_out, g_norm_mix, w_in, w_dw, b_dw, g_conv_ln, b_conv_ln, w_gate_up, b_gate, g_gla_norm, w_out, g_norm_ffn2, w_ffn2_in, w_ffn2_out, g_norm_final, w_ada_final, b_ada_final, loss_target, m_w_ada, m_b_ada, m_g_norm_ffn1, m_w_ffn1_in, m_w_ffn1_out, m_g_norm_mix, m_w_in, m_w_dw, m_b_dw, m_g_conv_ln, m_b_conv_ln, m_w_gate_up, m_b_gate, m_g_gla_norm, m_w_out, m_g_norm_ffn2, m_w_ffn2_in, m_w_ffn2_out, m_g_norm_final, m_w_ada_final, m_b_ada_final, v_w_ada, v_b_ada, v_g_norm_ffn1, v_w_ffn1_in, v_w_ffn1_out, v_g_norm_mix, v_w_in, v_w_dw, v_b_dw, v_g_conv_ln, v_b_conv_ln, v_w_gate_up, v_b_gate, v_g_gla_norm, v_w_out, v_g_norm_ffn2, v_w_ffn2_in, v_w_ffn2_out, v_g_norm_final, v_w_ada_final, v_b_ada_final):
    me = 4 * lax.axis_index("x") + 2 * lax.axis_index("y") + lax.axis_index("c")
    L = DEPTH
    n_ada = N_MOD * D // N_DEV
    n_fin = 2 * D // N_DEV

    small = jnp.concatenate([c.reshape(-1), w_dw.reshape(-1), w_gate_up.reshape(-1)])
    n_small = small.shape[0]
    small = jnp.pad(small, (0, 8 * D - n_small)).reshape(8, D)
    big = dict(wi1=w_ffn1_in, wo1=w_ffn1_out, win=w_in, wout=w_out, wi2=w_ffn2_in, wo2=w_ffn2_out)
    transposed = ("wi1", "wi2", "win")
    sh = {n: [(a[l].T if n in transposed else a[l]).astype(bf16) for l in range(L)] for n, a in big.items()}
    small_a, wi1_first, wo1_first = all_gather([small, sh["wi1"][0], sh["wo1"][0]], "gather_first")
    small_a = small_a.reshape(N_DEV, 8 * D)
    c_all = small_a[:, :D]
    o1 = D + L * CW * (DC // N_DEV)
    wdw_full = _col_shards_to_full(small_a[:, D:o1].reshape(N_DEV, L * CW, DC // N_DEV)).reshape(L, CW, DC)
    wgu_full = _col_shards_to_full(small_a[:, o1:o1 + L * GR * (DQK // N_DEV)].reshape(N_DEV, L * GR, DQK // N_DEV)).reshape(L, GR, DQK)

    b_ada_mine = lax.dynamic_slice(b_ada, (0, me * n_ada), (L, n_ada))
    b_fin_mine = lax.dynamic_slice(b_ada_final, (me * n_fin,), (n_fin,))
    parts = [ada_fwd(c_all, w_ada[l], b_ada_mine[l:l + 1], f"ada_fwd_{l}") for l in range(L)]
    parts.append(ada_fwd(c_all, w_ada_final, b_fin_mine.reshape(1, n_fin), "ada_fwd_final"))
    modsrc = jnp.concatenate(parts, axis=1)
    n_row = modsrc.shape[1]
    modsrc = jnp.pad(modsrc, ((0, 0), (0, 24 * 128 - n_row))).reshape(N_DEV, 24, 128)
    (modrecv,) = all_to_all([modsrc], "exchange_mod")
    modrecv = modrecv.reshape(N_DEV, 24 * 128)
    mods = []
    for l in range(L):
        mvec = modrecv[:, l * n_ada:(l + 1) * n_ada].reshape(N_MOD, D)
        mods.append(_pad_rows(mvec, 16))
    fmod = _pad_rows(modrecv[:, L * n_ada:L * n_ada + n_fin].reshape(2, D), 8)

    ws = {"gf": g_norm_final.reshape(1, D)}
    for l in range(L):
        ws[f"L{l}"] = dict(
            g1=g_norm_ffn1[l].reshape(1, D), g2=g_norm_mix[l].reshape(1, D), g3=g_norm_ffn2[l].reshape(1, D),
            wgu=_pad_rows(wgu_full[l], 128).astype(bf16),
            bgate=b_gate[l].reshape(1, DQK),
            wdw=_pad_rows(wdw_full[l], 32),
            cpar=_pad_rows(jnp.stack([b_dw[l], g_conv_ln[l], b_conv_ln[l]]), 8),
            gn_s=jnp.repeat(g_gla_norm[l], CH, axis=0),
        )

    loss_lanes, grad_x, gr, dmods, dfmod, recv = train_pass(
        x[0], loss_target[0], mods, fmod, sh, ws, wi1_first, wo1_first)

    def adam_big(rv, w, m, v, name, is_transposed=False):
        if is_transposed:
            w, m, v = (jnp.swapaxes(a, 1, 2) for a in (w, m, v))
        R = w.shape[1]
        tr = 256 if R % 256 == 0 else (R // 2 if (R // 2) % 16 == 0 else R)
        outs, _ = adam_parts(rv, w, m, v, tr, name)
        return [jnp.swapaxes(o, 1, 2) for o in outs] if is_transposed else outs

    res = {}
    res["w_ffn2_in"] = adam_big(recv["wi2"], w_ffn2_in, m_w_ffn2_in, v_w_ffn2_in, "adam_ffn2_in", True)
    res["w_ffn2_out"] = adam_big(recv["wo2"], w_ffn2_out, m_w_ffn2_out, v_w_ffn2_out, "adam_ffn2_out")
    res["w_in"] = adam_big(recv["win"], w_in, m_w_in, v_w_in, "adam_w_in", True)
    res["w_out"] = adam_big(recv["wout"], w_out, m_w_out, v_w_out, "adam_w_out")
    res["w_ffn1_out"] = adam_big(recv["wo1"], w_ffn1_out, m_w_ffn1_out, v_w_ffn1_out, "adam_ffn1_out")
    res["w_ffn1_in"] = adam_big(recv["wi1"], w_ffn1_in, m_w_ffn1_in, v_w_ffn1_in, "adam_ffn1_in", True)

    flat = lambda name: jnp.stack([gr[f"L{l}"][name] for l in range(L)]).reshape(-1)
    sections = [
        ("b_ada", jnp.stack(dmods).reshape(-1)), ("b_ada_final", dfmod.reshape(-1)),
        ("g_norm_ffn1", flat("g1")), ("g_norm_mix", flat("g2")), ("g_norm_ffn2", flat("g3")), ("g_norm_final", gr["gf"]),
        ("b_dw", flat("bdw")), ("g_conv_ln", flat("gln")), ("b_conv_ln", flat("bln")), ("b_gate", flat("bgate")),
        ("g_gla_norm", flat("gn")),
    ]
    n_rep = sum(s[1].shape[0] for s in sections)
    rep_rows = -(-n_rep // D)
    extra = [("loss", loss_lanes), ("w_dw", flat("wdw")), ("w_gate_up", flat("wgu"))]
    pack = jnp.concatenate([s[1] for s in sections] + [jnp.zeros((rep_rows * D - n_rep,), f32)] + [s[1] for s in extra])
    n_pack = pack.shape[0]
    pack_rows = -(-n_pack // (8 * D)) * 8
    pack = jnp.pad(pack, (0, pack_rows * D - n_pack)).reshape(pack_rows, D)
    (pack_all,) = all_gather([pack], "gather_small_grads")
    tot = sum8(pack_all, "sum_small_grads")
    tot_flat = tot.reshape(-1)
    loss = jnp.sum(tot_flat[rep_rows * D:rep_rows * D + D])
    o_dw = rep_rows * D + D
    g_wdw_full = tot_flat[o_dw:o_dw + L * CW * DC].reshape(L, CW, DC)
    o_gu = o_dw + L * CW * DC
    g_wgu_full = tot_flat[o_gu:o_gu + L * GR * DQK].reshape(L, GR, DQK)

    small_params = dict(b_ada=(b_ada, m_b_ada, v_b_ada), b_ada_final=(b_ada_final, m_b_ada_final, v_b_ada_final),
                        g_norm_ffn1=(g_norm_ffn1, m_g_norm_ffn1, v_g_norm_ffn1), g_norm_mix=(g_norm_mix, m_g_norm_mix, v_g_norm_mix),
                        g_norm_ffn2=(g_norm_ffn2, m_g_norm_ffn2, v_g_norm_ffn2), g_norm_final=(g_norm_final, m_g_norm_final, v_g_norm_final),
                        b_dw=(b_dw, m_b_dw, v_b_dw), g_conv_ln=(g_conv_ln, m_g_conv_ln, v_g_conv_ln),
                        b_conv_ln=(b_conv_ln, m_b_conv_ln, v_b_conv_ln), b_gate=(b_gate, m_b_gate, v_b_gate),
                        g_gla_norm=(g_gla_norm, m_g_gla_norm, v_g_gla_norm))

    def rep_pack(idx):
        p = jnp.concatenate([small_params[s[0]][idx].reshape(-1) for s in sections])
        return jnp.pad(p, (0, rep_rows * D - n_rep)).reshape(rep_rows, D)

    g_rep = tot[:rep_rows]
    d_rep, m_rep, v_rep = adam_plain(g_rep, rep_pack(0), rep_pack(1), rep_pack(2), rep_rows, "adam_small")
    off = 0
    for sname, sval in sections:
        shp = small_params[sname][0].shape
        nel = sval.shape[0]
        res[sname] = [a.reshape(-1)[off:off + nel].reshape(shp) for a in (g_rep, d_rep, m_rep, v_rep)]
        off += nel

    def adam_cols(g_full, w, m, v, name):
        shp = w.shape
        g_mine = lax.dynamic_slice(g_full, (0, 0, me * shp[2]), shp)
        R, C = shp[0] * shp[1], shp[2]
        outs = adam_plain(g_mine.reshape(R, C), w.reshape(R, C), m.reshape(R, C), v.reshape(R, C), R, name)
        return [g_mine] + [o.reshape(shp) for o in outs]

    res["w_dw"] = adam_cols(g_wdw_full, w_dw, m_w_dw, v_w_dw, "adam_w_dw")
    res["w_gate_up"] = adam_cols(g_wgu_full, w_gate_up, m_w_gate_up, v_w_gate_up, "adam_w_gate_up")

    c_all_t = c_all.T
    dmod_all = pack_all.reshape(N_DEV, -1)[:, :L * N_MOD * D].reshape(N_DEV, L, N_MOD * D)
    dfm_all = pack_all.reshape(N_DEV, -1)[:, L * N_MOD * D:L * N_MOD * D + 2 * D]
    dm_mine = lax.dynamic_slice(dmod_all, (0, 0, me * n_ada), (N_DEV, L, n_ada))
    dfm_mine = lax.dynamic_slice(dfm_all, (0, me * n_fin), (N_DEV, n_fin))
    g_w_ada = jnp.stack([ada_wgrad(c_all_t, dm_mine[:, l], f"ada_wgrad_{l}") for l in range(L)])
    g_w_fin = ada_wgrad(c_all_t, dfm_mine, "ada_wgrad_final")
    outs = adam_plain(g_w_ada.reshape(L * D, n_ada), w_ada.reshape(L * D, n_ada), m_w_ada.reshape(L * D, n_ada),
                      v_w_ada.reshape(L * D, n_ada), 256, "adam_w_ada")
    res["w_ada"] = [g_w_ada] + [o.reshape(w_ada.shape) for o in outs]
    res["w_ada_final"] = [g_w_fin] + list(adam_plain(g_w_fin, w_ada_final, m_w_ada_final, v_w_ada_final, 256, "adam_w_ada_final"))

    order = ["w_ada", "b_ada", "g_norm_ffn1", "w_ffn1_in", "w_ffn1_out", "g_norm_mix", "w_in", "w_dw", "b_dw", "g_conv_ln",
             "b_conv_ln", "w_gate_up", "b_gate", "g_gla_norm", "w_out", "g_norm_ffn2", "w_ffn2_in", "w_ffn2_out",
             "g_norm_final", "w_ada_final", "b_ada_final"]
    out = [loss, grad_x[None]]
    for k in range(4):
        out += [res[name][k] for name in order]
    return tuple(out)
```

```python
import functools

import jax
import jax.numpy as jnp
from jax import lax
from jax.experimental import pallas as pl
from jax.experimental.pallas import tpu as pltpu

f32 = jnp.float32
bf16 = jnp.bfloat16

N_DEV = 8
DEPTH = 2
D = 1024
F = 2816
DC = 512
NH = 4
DK = 64
DV = 128
DQK = NH * DK
DG = NH * DV
CH = 64
CW = 31
GR = 16
TAU = 16.0
N_MOD = 9
DIN = 2 * DC + 2 * DQK + 2 * DG + GR
DINP = 2688
EPS = 1e-6
HALO = 32
SUBLANES = 8
CONV_ROWS = 32
NFS = 4
FS = F // NFS

ADAM_LR = 0.001
ADAM_B1 = 0.9
ADAM_B2 = 0.999
ADAM_EPS = 1e-08
ADAM_WD = 0.01
ADAM_STEP = 10

V7X_VMEM_LIMIT = 56 * 1024 * 1024
MESH = pl.DeviceIdType.MESH
HIGHEST = lax.Precision.HIGHEST

NT = (((1,), (1,)), ((), ()))
TN = (((0,), (0,)), ((), ()))


def _cp(n_axes):
    return pltpu.CompilerParams(dimension_semantics=("arbitrary",) * n_axes, vmem_limit_bytes=V7X_VMEM_LIMIT)


def _full(shape):
    nd = len(shape)
    return pl.BlockSpec(shape, lambda *_: (0,) * nd)


def _resident(shape):
    nd = len(shape)
    return pl.BlockSpec(shape, lambda *_: (0,) * nd, pipeline_mode=pl.Buffered(1))


def _dot(a, b):
    return jnp.dot(a, b, preferred_element_type=f32)


def _dg(a, b, dims):
    return lax.dot_general(a, b, dims, preferred_element_type=f32)


def _sigmoid(x):
    return jax.nn.sigmoid(x)


def _rowsum(x):
    return jnp.sum(x, axis=0, keepdims=True)


def _rms_parts(xv):
    rstd = lax.rsqrt(jnp.mean(xv * xv, axis=-1, keepdims=True) + EPS)
    return xv * rstd, rstd


def _rms_bwd(dxh, xh, rstd):
    return rstd * (dxh - xh * jnp.mean(dxh * xh, axis=-1, keepdims=True))


def ffn_fwd(x, mod, g, wi, wo, rows, tm, name, comm=None):
    S = x.shape[0]
    r_shift, r_scale, r_gate = rows

    def body(x_ref, mod_ref, g_ref, wi_ref, wo_ref, xo_ref, h_ref, zg_ref, zu_ref, f_ref):
        xv = x_ref[...]
        xh, _ = _rms_parts(xv)
        h = (xh * g_ref[...] * (1.0 + mod_ref[r_scale:r_scale + 1, :]) + mod_ref[r_shift:r_shift + 1, :]).astype(bf16)
        h_ref[...] = h
        fv = None
        for j in range(NFS):
            zg = _dg(h, wi_ref[j], NT)
            zu = _dg(h, wi_ref[j + NFS], NT)
            zg_ref[j] = zg.astype(bf16)
            zu_ref[j] = zu.astype(bf16)
            a = zg * _sigmoid(zg) * zu
            part = _dot(a.astype(bf16), wo_ref[j * FS:(j + 1) * FS, :])
            fv = part if fv is None else fv + part
        f_ref[...] = fv.astype(bf16)
        xo_ref[...] = xv + 0.5 * mod_ref[r_gate:r_gate + 1, :] * fv

    row = lambda i: (i, 0)
    tile = lambda i: (0, i, 0)
    shard = jax.ShapeDtypeStruct((NFS, S, FS), bf16)
    return _pcall(
        body, (x, mod, g, wi, wo), name=name, comm=comm,
        grid=(S // tm,),
        in_specs=[pl.BlockSpec((tm, D), row), _full(mod.shape), _full(g.shape), _resident(wi.shape), _resident(wo.shape)],
        out_specs=[pl.BlockSpec((tm, D), row), pl.BlockSpec((tm, D), row),
                   pl.BlockSpec((NFS, tm, FS), tile), pl.BlockSpec((NFS, tm, FS), tile), pl.BlockSpec((tm, D), row)],
        out_shape=[jax.ShapeDtypeStruct((S, D), f32), jax.ShapeDtypeStruct((S, D), bf16), shard, shard,
                   jax.ShapeDtypeStruct((S, D), bf16)],
    )


def ffn_bwd_hidden(dy, zg, zu, mod, wo_t, r_gate, tm, name, comm=None):
    S = dy.shape[0]
    nt = S // tm

    def body(dy_ref, zg_ref, zu_ref, mod_ref, wo_ref, dzg_ref, dzu_ref, dwo_ref, acc_s):
        i = pl.program_id(0)

        @pl.when(i == 0)
        def _():
            acc_s[...] = jnp.zeros_like(acc_s)

        df = (0.5 * mod_ref[r_gate:r_gate + 1, :] * dy_ref[...]).astype(bf16)
        for j in range(NFS):
            zgv = zg_ref[j].astype(f32)
            zuv = zu_ref[j].astype(f32)
            s = _sigmoid(zgv)
            sil = zgv * s
            acc_s[j] += _dg((sil * zuv).astype(bf16), df, TN)
            da = _dg(df, wo_ref[j * FS:(j + 1) * FS, :], NT)
            dzu_ref[j] = (da * sil).astype(bf16)
            dzg_ref[j] = (da * zuv * (s * (1.0 + zgv * (1.0 - s)))).astype(bf16)

        @pl.when(i == nt - 1)
        def _():
            dwo_ref[...] = acc_s[...].astype(bf16)

    row = lambda i: (i, 0)
    tile = lambda i: (0, i, 0)
    shard = jax.ShapeDtypeStruct((NFS, S, FS), bf16)
    tspec = pl.BlockSpec((NFS, tm, FS), tile)
    (dzg, dzu, dwo), comm_outs = _pcall(
        body, (dy, zg, zu, mod, wo_t), name=name, comm=comm,
        grid=(nt,),
        in_specs=[pl.BlockSpec((tm, D), row), tspec, tspec, _full(mod.shape), _resident(wo_t.shape)],
        out_specs=[tspec, tspec, _full((NFS, FS, D))],
        out_shape=[shard, shard, jax.ShapeDtypeStruct((NFS, FS, D), bf16)],
        scratch_shapes=[pltpu.VMEM((NFS, FS, D), f32)],
    )
    return (dzg, dzu, dwo.reshape(F, D)), comm_outs


def ffn_bwd_input(x, dy, dzg, dzu, fo, mod, g, wi_t, rows, tm, name, comm=None):
    S = x.shape[0]
    r_shift, r_scale, r_gate = rows

    def body(x_ref, dy_ref, dzg_ref, dzu_ref, f_ref, mod_ref, g_ref, wi_ref, dx_ref, red_ref):
        @pl.when(pl.program_id(0) == 0)
        def _():
            red_ref[...] = jnp.zeros_like(red_ref)

        dh = _dot(dzg_ref[0], wi_ref[0]) + _dot(dzu_ref[0], wi_ref[NFS])
        for j in range(1, NFS):
            dh = dh + _dot(dzg_ref[j], wi_ref[j]) + _dot(dzu_ref[j], wi_ref[j + NFS])
        dyv = dy_ref[...]
        xh, rstd = _rms_parts(x_ref[...])
        gv = g_ref[...]
        n = xh * gv
        dn = dh * (1.0 + mod_ref[r_scale:r_scale + 1, :])
        red_ref[0:1, :] += _rowsum(dh)
        red_ref[1:2, :] += _rowsum(dh * n)
        red_ref[2:3, :] += _rowsum(0.5 * f_ref[...].astype(f32) * dyv)
        red_ref[3:4, :] += _rowsum(dn * xh)
        dx_ref[...] = dyv + _rms_bwd(dn * gv, xh, rstd)

    row = lambda i: (i, 0)
    tspec = pl.BlockSpec((NFS, tm, FS), lambda i: (0, i, 0))
    return _pcall(
        body, (x, dy, dzg, dzu, fo, mod, g, wi_t), name=name, comm=comm,
        grid=(S // tm,),
        in_specs=[pl.BlockSpec((tm, D), row), pl.BlockSpec((tm, D), row), tspec, tspec, pl.BlockSpec((tm, D), row),
                  _full(mod.shape), _full(g.shape), _resident(wi_t.shape)],
        out_specs=[pl.BlockSpec((tm, D), row), _full((8, D))],
        out_shape=[jax.ShapeDtypeStruct((S, D), f32), jax.ShapeDtypeStruct((8, D), f32)],
    )


def matmul_tn(a, b, M, N, bm, bn, bk, name, a_col_block=0, out_dtype=f32):
    S = b.shape[0]
    nk = S // bk

    def body(a_ref, b_ref, o_ref, acc_s):
        k = pl.program_id(2)

        @pl.when(k == 0)
        def _():
            acc_s[...] = jnp.zeros_like(acc_s)

        acc_s[...] += _dg(a_ref[...].astype(bf16), b_ref[...].astype(bf16), TN)

        @pl.when(k == nk - 1)
        def _():
            o_ref[...] = acc_s[...].astype(out_dtype)

    return pl.pallas_call(
        body, name=name,
        grid=(M // bm, N // bn, nk),
        in_specs=[
            pl.BlockSpec((bk, bm), lambda i, j, k: (k, i + a_col_block)),
            pl.BlockSpec((bk, bn), lambda i, j, k: (k, j)),
        ],
        out_specs=pl.BlockSpec((bm, bn), lambda i, j, k: (i, j)),
        out_shape=jax.ShapeDtypeStruct((M, N), out_dtype),
        scratch_shapes=[pltpu.VMEM((bm, bn), f32)],
        compiler_params=_cp(3),
    )(a, b)


def dwi_pieces(h, dzg, dzu, bk, name, comm=None):
    S = h.shape[0]
    nk = S // bk

    def body(h_ref, g_ref, u_ref, o_ref, acc_s):
        half = pl.program_id(0)
        k = pl.program_id(1)

        @pl.when(k == 0)
        def _():
            acc_s[...] = jnp.zeros_like(acc_s)

        hv = h_ref[...]

        @pl.when(half == 0)
        def _():
            for j in range(NFS):
                acc_s[j] += _dg(g_ref[j], hv, TN)

        @pl.when(half == 1)
        def _():
            for j in range(NFS):
                acc_s[j] += _dg(u_ref[j], hv, TN)

        @pl.when(k == nk - 1)
        def _():
            o_ref[...] = acc_s[...].astype(bf16)

    (out,), comm_outs = _pcall(
        body, (h, dzg, dzu), name=name, comm=comm,
        grid=(2, nk),
        in_specs=[
            pl.BlockSpec((bk, D), lambda half, k: (k, 0)),
            pl.BlockSpec((NFS, bk, FS), lambda half, k: (0, jnp.where(half == 0, k, nk - 1), 0)),
            pl.BlockSpec((NFS, bk, FS), lambda half, k: (0, jnp.where(half == 1, k, 0), 0)),
        ],
        out_specs=[pl.BlockSpec((NFS, FS, D), lambda half, k: (half, 0, 0))],
        out_shape=[jax.ShapeDtypeStruct((2 * NFS, FS, D), bf16)],
        scratch_shapes=[pltpu.VMEM((NFS, FS, D), f32)],
    )
    return out, comm_outs


def mixin_fwd(x1, mod, g, win, wgu, bgate, tm, name):
    S = x1.shape[0]

    def body(x_ref, mod_ref, g_ref, win_ref, wgu_ref, bg_ref, z_ref, la_ref):
        xh, _ = _rms_parts(x_ref[...])
        hv = xh * g_ref[...] * (1.0 + mod_ref[4:5, :]) + mod_ref[3:4, :]
        z = _dg(hv.astype(bf16), win_ref[...], NT)
        z_ref[...] = z
        glr = z[:, DINP - 128:]
        pre = _dot(glr.astype(bf16), wgu_ref[...]) + bg_ref[...]
        la_ref[...] = (jnp.minimum(pre, 0.0) - jnp.log(1.0 + jnp.exp(-jnp.abs(pre)))) * (1.0 / TAU)

    return pl.pallas_call(
        body, name=name,
        grid=(S // tm,),
        in_specs=[pl.BlockSpec((tm, D), lambda i: (i, 0)), _full(mod.shape), _full(g.shape),
                  _full(win.shape), _full(wgu.shape), _full(bgate.shape)],
        out_specs=[pl.BlockSpec((tm, DINP), lambda i: (i, 0)), pl.BlockSpec((tm, DQK), lambda i: (i, 0))],
        out_shape=[jax.ShapeDtypeStruct((S, DINP), f32), jax.ShapeDtypeStruct((S, DQK), f32)],
        compiler_params=_cp(1),
    )(x1, mod, g, win, wgu, bgate)


def mixin_bwd(x1, dres, dzab, dq, dk, dv, dr, dpre, mod, g, win, wgu, tm, name, comm=None):
    S = x1.shape[0]

    def body(x_ref, dres_ref, dzab_ref, dq_ref, dk_ref, dv_ref, dr_ref, dpre_ref, mod_ref, g_ref, win_ref, wgu_ref,
             dx_ref, h_ref, dz_ref, red_ref):
        @pl.when(pl.program_id(0) == 0)
        def _():
            red_ref[...] = jnp.zeros_like(red_ref)

        dglr = _dg(dpre_ref[...], wgu_ref[...], NT).astype(bf16)
        dz = jnp.concatenate([dzab_ref[...], dq_ref[...], dk_ref[...], dv_ref[...], dr_ref[...], dglr], axis=1)
        dz_ref[...] = dz
        dh = _dot(dz, win_ref[...])
        xh, rstd = _rms_parts(x_ref[...])
        gv = g_ref[...]
        n = xh * gv
        sc = 1.0 + mod_ref[4:5, :]
        h_ref[...] = (n * sc + mod_ref[3:4, :]).astype(bf16)
        dn = dh * sc
        red_ref[0:1, :] += _rowsum(dh)
        red_ref[1:2, :] += _rowsum(dh * n)
        red_ref[2:3, :] += _rowsum(dn * xh)
        dx_ref[...] = dres_ref[...] + _rms_bwd(dn * gv, xh, rstd)

    row = lambda i: (i, 0)
    return _pcall(
        body, (x1, dres, dzab, dq, dk, dv, dr, dpre, mod, g, win, wgu), name=name, comm=comm,
        grid=(S // tm,),
        in_specs=[pl.BlockSpec((tm, D), row), pl.BlockSpec((tm, D), row),
                  pl.BlockSpec((tm, 2 * DC), row), pl.BlockSpec((tm, DQK), row), pl.BlockSpec((tm, DQK), row),
                  pl.BlockSpec((tm, DG), row), pl.BlockSpec((tm, DG), row), pl.BlockSpec((tm, DQK), row),
                  _full(mod.shape), _full(g.shape), _full(win.shape), _full(wgu.shape)],
        out_specs=[pl.BlockSpec((tm, D), row), pl.BlockSpec((tm, D), row), pl.BlockSpec((tm, DINP), row), _full((8, D))],
        out_shape=[jax.ShapeDtypeStruct((S, D), f32), jax.ShapeDtypeStruct((S, D), bf16),
                   jax.ShapeDtypeStruct((S, DINP), bf16), jax.ShapeDtypeStruct((8, D), f32)],
    )


def _glu(zab):
    return zab[:, :DC] * _sigmoid(zab[:, DC:])


def _shift_copies(src_s, dst_s, tc):
    n = tc + HALO - SUBLANES
    for b in range(1, SUBLANES):
        dst_s[b, 0:n, :] = src_s[b:b + n, :]


def _shifted(src_s, dst_s, o, tc):
    b = o % SUBLANES
    a = o - b
    return src_s[a:a + tc, :] if b == 0 else dst_s[b, a:a + tc, :]


def _conv_fwd_prepare(first, zc_ref, zp_ref, u_s, us_s, tc):
    up = _glu(zp_ref[...])
    u_s[0:HALO, :] = jnp.where(first, 0.0, up)
    u_s[HALO:HALO + tc, :] = _glu(zc_ref[...])
    _shift_copies(u_s, us_s, tc)


def _conv_fwd_rows(r0, n, w_ref, cp_ref, y_ref, yc_ref, u_s, us_s):
    for r in range(r0, r0 + n, CONV_ROWS):
        acc = _shifted(u_s, us_s, HALO - (CW - 1) + r, CONV_ROWS) * w_ref[0:1, :]
        for w in range(1, CW):
            acc = acc + _shifted(u_s, us_s, HALO - (CW - 1) + w + r, CONV_ROWS) * w_ref[w:w + 1, :]
        y = acc + cp_ref[0:1, :]
        y_ref[r:r + CONV_ROWS, :] = y
        yc = y - jnp.mean(y, axis=-1, keepdims=True)
        yl = yc * lax.rsqrt(jnp.mean(yc * yc, axis=-1, keepdims=True) + EPS) * cp_ref[1:2, :] + cp_ref[2:3, :]
        yc_ref[r:r + CONV_ROWS, :] = (yl * _sigmoid(yl)).astype(bf16)


def _conv_bwd_prepare(first, last, zc_ref, zp_ref, y_ref, yn_ref, d_ref, dn_ref, cp_ref, red_ref, u_s, dy_s, us_s, dys_s, tc):
    gl = cp_ref[1:2, :]
    bl = cp_ref[2:3, :]

    def ln_bwd(yv, dv):
        yc = yv - jnp.mean(yv, axis=-1, keepdims=True)
        rstd = lax.rsqrt(jnp.mean(yc * yc, axis=-1, keepdims=True) + EPS)
        yh = yc * rstd
        yl = yh * gl + bl
        s = _sigmoid(yl)
        dyl = dv * (s * (1.0 + yl * (1.0 - s)))
        dyh = dyl * gl
        dyv = rstd * (dyh - jnp.mean(dyh, axis=-1, keepdims=True) - yh * jnp.mean(dyh * yh, axis=-1, keepdims=True))
        return dyv, dyl, yh

    dy_c, dyl_c, yh_c = ln_bwd(y_ref[...], d_ref[...])
    dy_n, _, _ = ln_bwd(yn_ref[...], dn_ref[...])
    dy_s[0:tc, :] = dy_c
    dy_s[tc:tc + HALO, :] = jnp.where(last, 0.0, dy_n)
    u_s[0:HALO, :] = jnp.where(first, 0.0, _glu(zp_ref[...]))
    u_s[HALO:HALO + tc, :] = _glu(zc_ref[...])
    _shift_copies(u_s, us_s, tc)
    _shift_copies(dy_s, dys_s, tc)
    red_ref[32:33, :] += _rowsum(dy_c)
    red_ref[33:34, :] += _rowsum(dyl_c * yh_c)
    red_ref[34:35, :] += _rowsum(dyl_c)


def _conv_bwd_input_rows(r0, n, zc_ref, w_ref, dz_ref, dy_s, dys_s):
    for r in range(r0, r0 + n, CONV_ROWS):
        du = _shifted(dy_s, dys_s, CW - 1 + r, CONV_ROWS) * w_ref[0:1, :]
        for w in range(1, CW):
            du = du + _shifted(dy_s, dys_s, CW - 1 - w + r, CONV_ROWS) * w_ref[w:w + 1, :]
        zc = zc_ref[r:r + CONV_ROWS, :]
        av = zc[:, :DC]
        sb = _sigmoid(zc[:, DC:])
        dz_ref[r:r + CONV_ROWS, :] = jnp.concatenate([du * sb, du * av * sb * (1.0 - sb)], axis=1).astype(dz_ref.dtype)


def _conv_bwd_taps(w0, w1, red_ref, u_s, us_s, dy_s, tc):
    for w in range(w0, w1):
        part = None
        for r in range(0, tc, CONV_ROWS):
            prod = _shifted(u_s, us_s, HALO - (CW - 1) + w + r, CONV_ROWS) * dy_s[r:r + CONV_ROWS, :]
            fold = jnp.sum(prod.reshape(CONV_ROWS // SUBLANES, SUBLANES, DC), axis=0)
            part = fold if part is None else part + fold
        red_ref[w:w + 1, :] += _rowsum(part)


def conv_bwd(z, y, dyc, wdw, cpar, tc, name):
    S = z.shape[0]
    nb = tc // HALO
    nt = S // tc
    last_halo = S // HALO - 1

    def body(zc_ref, zp_ref, y_ref, yn_ref, d_ref, dn_ref, w_ref, cp_ref, dz_ref, red_ref, u_s, dy_s, us_s, dys_s):
        i = pl.program_id(0)

        @pl.when(i == 0)
        def _():
            red_ref[...] = jnp.zeros_like(red_ref)

        _conv_bwd_prepare(i == 0, i == nt - 1, zc_ref, zp_ref, y_ref, yn_ref, d_ref, dn_ref, cp_ref, red_ref,
                          u_s, dy_s, us_s, dys_s, tc)
        _conv_bwd_input_rows(0, tc, zc_ref, w_ref, dz_ref, dy_s, dys_s)
        _conv_bwd_taps(0, CW, red_ref, u_s, us_s, dy_s, tc)

    cur = lambda i: (i, 0)
    nxt = lambda i: (jnp.minimum((i + 1) * nb, last_halo), 0)
    return pl.pallas_call(
        body, name=name,
        grid=(nt,),
        in_specs=[pl.BlockSpec((tc, 2 * DC), cur),
                  pl.BlockSpec((HALO, 2 * DC), lambda i: (jnp.maximum(i * nb - 1, 0), 0)),
                  pl.BlockSpec((tc, DC), cur), pl.BlockSpec((HALO, DC), nxt),
                  pl.BlockSpec((tc, DC), cur), pl.BlockSpec((HALO, DC), nxt),
                  _full(wdw.shape), _full(cpar.shape)],
        out_specs=[pl.BlockSpec((tc, 2 * DC), cur), _full((40, DC))],
        out_shape=[jax.ShapeDtypeStruct((S, 2 * DC), bf16), jax.ShapeDtypeStruct((40, DC), f32)],
        scratch_shapes=[pltpu.VMEM((HALO + tc, DC), f32), pltpu.VMEM((tc + HALO, DC), f32),
                        pltpu.VMEM((SUBLANES, HALO + tc, DC), f32), pltpu.VMEM((SUBLANES, HALO + tc, DC), f32)],
        compiler_params=_cp(1),
    )(z, z, y, y, dyc, dyc, wdw, cpar)


def _gla_consts():
    r = lax.broadcasted_iota(jnp.int32, (CH, CH), 0)
    c = lax.broadcasted_iota(jnp.int32, (CH, CH), 1)
    tril = r >= c
    lane = lax.broadcasted_iota(jnp.int32, (CH, DQK), 1)
    masks = [(lane >= h * DK) & (lane < (h + 1) * DK) for h in range(NH)]
    r4 = lax.broadcasted_iota(jnp.int32, (DQK, DQK), 0)
    c4 = lax.broadcasted_iota(jnp.int32, (DQK, DQK), 1)
    eye4 = (r4 == c4).astype(f32)
    rs = lax.broadcasted_iota(jnp.int32, (DQK, CH), 0) & (CH - 1)
    tril4 = rs >= lax.broadcasted_iota(jnp.int32, (DQK, CH), 1)
    return tril, tril4, masks, eye4


def _stack(xv, masks):
    return jnp.concatenate([jnp.where(m, xv, 0.0) for m in masks], axis=0)


def _unstack(rv, masks):
    out = jnp.where(masks[0], rv[0:CH, :], 0.0)
    for h in range(1, NH):
        out = out + jnp.where(masks[h], rv[h * CH:(h + 1) * CH, :], 0.0)
    return out


def _vstack(xv):
    return jnp.concatenate([xv[:, h * DV:(h + 1) * DV] for h in range(NH)], axis=0)


def _vunstack(xv):
    return jnp.concatenate([xv[h * CH:(h + 1) * CH, :] for h in range(NH)], axis=1)


def _gla_chunk_fwd(lac, qc, kc, vc, s_all, tril, masks, tril4):
    lmat = tril.astype(f32)
    bc = jnp.dot(lmat, lac, preferred_element_type=f32, precision=HIGHEST)
    bend = bc[CH - 1:CH, :]
    eb = jnp.exp(bc)
    enb = jnp.exp(-bc)
    ed = jnp.exp(bend - bc)
    qh = qc * (DK ** -0.5)
    qf = qh * eb
    qn = qh * enb
    kn = kc * enb
    kp = kc * eb
    kd = kc * ed
    qf_s = _stack(qf, masks).astype(bf16)
    qn_s = _stack(qn, masks).astype(bf16)
    kn_b = kn.astype(bf16)
    kp_b = kp.astype(bf16)
    attf = _dg(qf_s, kn_b, NT)
    attb = _dg(qn_s, kp_b, NT)
    a_s = jnp.where(tril4, attf, attb)
    a_b = a_s.astype(bf16)
    v_b = vc.astype(bf16)
    intra = jnp.concatenate(
        [_dot(a_b[h * CH:(h + 1) * CH, :], v_b[:, h * DV:(h + 1) * DV]) for h in range(NH)], axis=0)
    o_s = intra + _dot(qf_s, s_all.astype(bf16))
    return dict(bc=bc, bend=bend, eb=eb, enb=enb, ed=ed, qf=qf, qn=qn, kn=kn, kp=kp, kd=kd,
                qf_s=qf_s, qn_s=qn_s, kn_b=kn_b, kp_b=kp_b, a_b=a_b, v_b=v_b, o_s=o_s)


def _col_from_row(row, eye4):
    return jnp.sum(eye4 * row, axis=1, keepdims=True)


def _row_from_col(col, eye4):
    return jnp.sum(eye4 * col, axis=0, keepdims=True)


def _gla_fwd_chunk(c, consts, q_ref, k_ref, v_ref, r_ref, la_ref, gn_ref, yg_ref, sp_ref, st):
    tril, tril4, masks, eye4 = consts
    r0, s0 = c * CH, c * DQK
    s_all = st[...]
    sp_ref[s0:s0 + DQK, :] = s_all
    vc = v_ref[r0:r0 + CH, :]
    t = _gla_chunk_fwd(la_ref[r0:r0 + CH, :], q_ref[r0:r0 + CH, :], k_ref[r0:r0 + CH, :], vc,
                       s_all, tril, masks, tril4)
    u_all = _dg(_stack(t["kd"], masks).astype(bf16), _vstack(vc).astype(bf16), TN)
    st[...] = _col_from_row(jnp.exp(t["bend"]), eye4) * s_all + u_all
    o_s = t["o_s"]
    on = o_s * lax.rsqrt(jnp.mean(o_s * o_s, axis=-1, keepdims=True) + EPS) * gn_ref[...]
    rc = r_ref[r0:r0 + CH, :]
    yg_ref[r0:r0 + CH, :] = (_vunstack(on) * (rc * _sigmoid(rc))).astype(bf16)


def mixer_core_fwd(z, la, gn_s, wdw, cpar, t, name):
    S = z.shape[0]
    nb = t // HALO
    nc = t // CH

    def body(zc_ref, zp_ref, q_ref, k_ref, v_ref, r_ref, la_ref, gn_ref, w_ref, cp_ref,
             y_ref, yc_ref, yg_ref, sp_ref, st, u_s, us_s):
        i = pl.program_id(0)

        @pl.when(i == 0)
        def _():
            st[...] = jnp.zeros_like(st)

        _conv_fwd_prepare(i == 0, zc_ref, zp_ref, u_s, us_s, t)
        consts = _gla_consts()
        for c in range(nc):
            _conv_fwd_rows(c * CH, CH, w_ref, cp_ref, y_ref, yc_ref, u_s, us_s)
            _gla_fwd_chunk(c, consts, q_ref, k_ref, v_ref, r_ref, la_ref, gn_ref, yg_ref, sp_ref, st)

    row = lambda i: (i, 0)
    return pl.pallas_call(
        body, name=name,
        grid=(S // t,),
        in_specs=[pl.BlockSpec((t, 2 * DC), row),
                  pl.BlockSpec((HALO, 2 * DC), lambda i: (jnp.maximum(i * nb - 1, 0), 0)),
                  pl.BlockSpec((t, DQK), lambda i: (i, 4)), pl.BlockSpec((t, DQK), lambda i: (i, 5)),
                  pl.BlockSpec((t, DG), lambda i: (i, 3)), pl.BlockSpec((t, DG), lambda i: (i, 4)),
                  pl.BlockSpec((t, DQK), row), _full(gn_s.shape), _full(wdw.shape), _full(cpar.shape)],
        out_specs=[pl.BlockSpec((t, DC), row), pl.BlockSpec((t, DC), row), pl.BlockSpec((t, DG), row),
                   pl.BlockSpec((nc * DQK, DV), row)],
        out_shape=[jax.ShapeDtypeStruct((S, DC), f32), jax.ShapeDtypeStruct((S, DC), bf16),
                   jax.ShapeDtypeStruct((S, DG), bf16), jax.ShapeDtypeStruct((S // CH * DQK, DV), f32)],
        scratch_shapes=[pltpu.VMEM((DQK, DV), f32), pltpu.VMEM((HALO + t, DC), f32),
                        pltpu.VMEM((SUBLANES, HALO + t, DC), f32)],
        compiler_params=_cp(1),
    )(z, z, z, z, z, z, la, gn_s, wdw, cpar)


def _gla_bwd_chunk(c, consts, umat, last_row, q_ref, k_ref, v_ref, r_ref, la_ref, sp_ref, dy_ref, gn_ref,
                   dq_ref, dk_ref, dv_ref, dr_ref, dpre_ref, redg_ref, redb_ref, gs):
    tril, tril4, masks, eye4 = consts
    r0, s0 = c * CH, c * DQK
    rows = slice(r0, r0 + CH)
    s_all = sp_ref[s0:s0 + DQK, :]
    lac = la_ref[rows, :]
    vc = v_ref[rows, :]
    rc = r_ref[rows, :]
    t = _gla_chunk_fwd(lac, q_ref[rows, :], k_ref[rows, :], vc, s_all, tril, masks, tril4)
    g_all = gs[...]
    g_b = g_all.astype(bf16)
    s_b = s_all.astype(bf16)
    o_s = t["o_s"]
    rstd = lax.rsqrt(jnp.mean(o_s * o_s, axis=-1, keepdims=True) + EPS)
    oh = o_s * rstd
    gnv = gn_ref[...]
    sr = _sigmoid(rc)
    dyv = dy_ref[rows, :]
    dr_ref[rows, :] = (dyv * _vunstack(oh * gnv) * (sr * (1.0 + rc * (1.0 - sr)))).astype(dr_ref.dtype)
    don = _vstack(dyv * (rc * sr))
    redg_ref[...] += don * oh
    doh = don * gnv
    do_s = rstd * (doh - oh * jnp.mean(doh * oh, axis=-1, keepdims=True))
    do_b = do_s.astype(bf16)
    v_b = t["v_b"]
    vst_b = _vstack(vc).astype(bf16)
    kd_s = _stack(t["kd"], masks).astype(bf16)
    da_s = jnp.concatenate(
        [_dg(do_b[h * CH:(h + 1) * CH, :], v_b[:, h * DV:(h + 1) * DV], NT) for h in range(NH)], axis=0)
    a_b = t["a_b"]
    dv_s = jnp.concatenate(
        [_dg(a_b[h * CH:(h + 1) * CH, :], do_b[h * CH:(h + 1) * CH, :], TN) for h in range(NH)], axis=0)
    dv_s = dv_s + _dot(kd_s, g_b)
    dv_ref[rows, :] = _vunstack(dv_s).astype(dv_ref.dtype)
    gend = jnp.exp(t["bend"])
    gcol = _col_from_row(gend, eye4)
    gs[...] = gcol * g_all + _dg(t["qf_s"], do_b, TN)
    dgcol = jnp.sum(g_all * s_all, axis=1, keepdims=True)
    dbend = _row_from_col(dgcol * gcol, eye4)
    dkd = _unstack(_dg(vst_b, g_b, NT), masks)
    daf = jnp.where(tril4, da_s, 0.0).astype(bf16)
    dab = jnp.where(tril4, 0.0, da_s).astype(bf16)
    dqf = _unstack(_dot(daf, t["kn_b"]) + _dg(do_b, s_b, NT), masks)
    dqn = _unstack(_dot(dab, t["kp_b"]), masks)
    dkn = _dg(daf, t["qf_s"], TN)
    dkp = _dg(dab, t["qn_s"], TN)
    dq_ref[rows, :] = ((dqf * t["eb"] + dqn * t["enb"]) * (DK ** -0.5)).astype(dq_ref.dtype)
    dk_ref[rows, :] = (dkn * t["enb"] + dkp * t["eb"] + dkd * t["ed"]).astype(dk_ref.dtype)
    dkd_kd = dkd * t["kd"]
    dbc = dqf * t["qf"] - dqn * t["qn"] - dkn * t["kn"] + dkp * t["kp"] - dkd_kd
    dbc = dbc + jnp.where(last_row, _rowsum(dkd_kd) + dbend, 0.0)
    dla = jnp.dot(umat, dbc, preferred_element_type=f32, precision=HIGHEST)
    dpre = dla * (1.0 / TAU) * (1.0 - jnp.exp(TAU * lac))
    dpre_ref[rows, :] = dpre.astype(dpre_ref.dtype)
    redb_ref[...] += dpre


def gla_bwd(z, la, sprev, dyg, gn_s, t, name, comm=None):
    S = z.shape[0]
    nc = t // CH
    nt = S // t

    def body(q_ref, k_ref, v_ref, r_ref, la_ref, sp_ref, dy_ref, gn_ref,
             dq_ref, dk_ref, dv_ref, dr_ref, dpre_ref, redg_ref, redb_ref, gs):
        @pl.when(pl.program_id(0) == 0)
        def _():
            gs[...] = jnp.zeros_like(gs)
            redg_ref[...] = jnp.zeros_like(redg_ref)
            redb_ref[...] = jnp.zeros_like(redb_ref)

        consts = _gla_consts()
        umat = (lax.broadcasted_iota(jnp.int32, (CH, CH), 0) <= lax.broadcasted_iota(jnp.int32, (CH, CH), 1)).astype(f32)
        last_row = lax.broadcasted_iota(jnp.int32, (CH, DQK), 0) == CH - 1
        for c in reversed(range(nc)):
            _gla_bwd_chunk(c, consts, umat, last_row, q_ref, k_ref, v_ref, r_ref, la_ref, sp_ref, dy_ref, gn_ref,
                           dq_ref, dk_ref, dv_ref, dr_ref, dpre_ref, redg_ref, redb_ref, gs)

    rev = lambda col: (lambda i: (nt - 1 - i, col))
    return _pcall(
        body, (z, z, z, z, la, sprev, dyg, gn_s), name=name, comm=comm,
        grid=(nt,),
        in_specs=[pl.BlockSpec((t, DQK), rev(4)), pl.BlockSpec((t, DQK), rev(5)),
                  pl.BlockSpec((t, DG), rev(3)), pl.BlockSpec((t, DG), rev(4)),
                  pl.BlockSpec((t, DQK), rev(0)), pl.BlockSpec((nc * DQK, DV), rev(0)),
                  pl.BlockSpec((t, DG), rev(0)), _full(gn_s.shape)],
        out_specs=[pl.BlockSpec((t, DQK), rev(0)), pl.BlockSpec((t, DQK), rev(0)),
                   pl.BlockSpec((t, DG), rev(0)), pl.BlockSpec((t, DG), rev(0)), pl.BlockSpec((t, DQK), rev(0)),
                   _full((DQK, DV)), _full((CH, DQK))],
        out_shape=[jax.ShapeDtypeStruct((S, DQK), bf16), jax.ShapeDtypeStruct((S, DQK), bf16),
                   jax.ShapeDtypeStruct((S, DG), bf16), jax.ShapeDtypeStruct((S, DG), bf16), jax.ShapeDtypeStruct((S, DQK), bf16),
                   jax.ShapeDtypeStruct((DQK, DV), f32), jax.ShapeDtypeStruct((CH, DQK), f32)],
        scratch_shapes=[pltpu.VMEM((DQK, DV), f32)],
    )


def mixout_fwd(x1, yc, yg, mod, wout, tm, name):
    S = x1.shape[0]

    def body(x_ref, yc_ref, yg_ref, mod_ref, w_ref, xo_ref):
        mixo = _dot(yc_ref[...], w_ref[0:DC, :]) + _dot(yg_ref[...], w_ref[DC:DC + DG, :])
        xo_ref[...] = x_ref[...] + mod_ref[5:6, :] * mixo

    row = lambda i: (i, 0)
    return pl.pallas_call(
        body, name=name,
        grid=(S // tm,),
        in_specs=[pl.BlockSpec((tm, D), row), pl.BlockSpec((tm, DC), row), pl.BlockSpec((tm, DG), row),
                  _full(mod.shape), _full(wout.shape)],
        out_specs=pl.BlockSpec((tm, D), row),
        out_shape=jax.ShapeDtypeStruct((S, D), f32),
        compiler_params=_cp(1),
    )(x1, yc, yg, mod, wout)


def mixout_bwd(dx2, yc, yg, mod, wout, tm, name):
    S = dx2.shape[0]

    def body(dx_ref, yc_ref, yg_ref, mod_ref, w_ref, dm_ref, dyc_ref, dyg_ref, red_ref):
        @pl.when(pl.program_id(0) == 0)
        def _():
            red_ref[...] = jnp.zeros_like(red_ref)

        dxv = dx_ref[...]
        mixo = _dot(yc_ref[...], w_ref[0:DC, :]) + _dot(yg_ref[...], w_ref[DC:DC + DG, :])
        red_ref[0:1, :] += _rowsum(dxv * mixo)
        dm = (mod_ref[5:6, :] * dxv).astype(bf16)
        dm_ref[...] = dm
        dycat = _dg(dm, w_ref[...], NT)
        dyc_ref[...] = dycat[:, :DC]
        dyg_ref[...] = dycat[:, DC:]

    row = lambda i: (i, 0)
    return pl.pallas_call(
        body, name=name,
        grid=(S // tm,),
        in_specs=[pl.BlockSpec((tm, D), row), pl.BlockSpec((tm, DC), row), pl.BlockSpec((tm, DG), row),
                  _full(mod.shape), _full(wout.shape)],
        out_specs=[pl.BlockSpec((tm, D), row), pl.BlockSpec((tm, DC), row), pl.BlockSpec((tm, DG), row), _full((8, D))],
        out_shape=[jax.ShapeDtypeStruct((S, D), bf16), jax.ShapeDtypeStruct((S, DC), f32),
                   jax.ShapeDtypeStruct((S, DG), f32), jax.ShapeDtypeStruct((8, D), f32)],
        compiler_params=_cp(1),
    )(dx2, yc, yg, mod, wout)


def final_fwd_bwd(x, tgt, fmod, g, tm, name):
    S = x.shape[0]

    def body(x_ref, t_ref, fm_ref, g_ref, dx_ref, red_ref):
        @pl.when(pl.program_id(0) == 0)
        def _():
            red_ref[...] = jnp.zeros_like(red_ref)

        xh, rstd = _rms_parts(x_ref[...])
        gv = g_ref[...]
        n = xh * gv
        sc = 1.0 + fm_ref[1:2, :]
        e = n * sc + fm_ref[0:1, :] - t_ref[...]
        red_ref[0:1, :] += _rowsum(e * e) * (0.5 / D)
        dy = e * (1.0 / D)
        dn = dy * sc
        red_ref[1:2, :] += _rowsum(dy)
        red_ref[2:3, :] += _rowsum(dy * n)
        red_ref[3:4, :] += _rowsum(dn * xh)
        dx_ref[...] = _rms_bwd(dn * gv, xh, rstd)

    row = lambda i: (i, 0)
    return pl.pallas_call(
        body, name=name,
        grid=(S // tm,),
        in_specs=[pl.BlockSpec((tm, D), row), pl.BlockSpec((tm, D), row), _full(fmod.shape), _full(g.shape)],
        out_specs=[pl.BlockSpec((tm, D), row), _full((8, D))],
        out_shape=[jax.ShapeDtypeStruct((S, D), f32), jax.ShapeDtypeStruct((8, D), f32)],
        compiler_params=_cp(1),
    )(x, tgt, fmod, g)


def ada_fwd(c_all, w, b, name):
    n = w.shape[1]

    def body(c_ref, w_ref, b_ref, o_ref):
        cv = c_ref[...]
        o_ref[...] = jnp.dot(cv * _sigmoid(cv), w_ref[...], preferred_element_type=f32, precision=HIGHEST) + b_ref[...]

    return pl.pallas_call(
        body, name=name,
        in_specs=[_full(c_all.shape), _full(w.shape), _full(b.shape)],
        out_specs=_full((N_DEV, n)),
        out_shape=jax.ShapeDtypeStruct((N_DEV, n), f32),
        grid=(1,),
        compiler_params=_cp(1),
    )(c_all, w, b)


def ada_wgrad(c_all_t, dm, name):
    n = dm.shape[1]

    def body(c_ref, d_ref, o_ref):
        cv = c_ref[...]
        o_ref[...] = jnp.dot(cv * _sigmoid(cv), d_ref[...], preferred_element_type=f32, precision=HIGHEST)

    return pl.pallas_call(
        body, name=name,
        in_specs=[_full(c_all_t.shape), _full(dm.shape)],
        out_specs=_full((D, n)),
        out_shape=jax.ShapeDtypeStruct((D, n), f32),
        grid=(1,),
        compiler_params=_cp(1),
    )(c_all_t, dm)


def _adam_math(gv, wv, mv, vv):
    m = ADAM_B1 * mv + (1.0 - ADAM_B1) * gv
    v = ADAM_B2 * vv + (1.0 - ADAM_B2) * (gv * gv)
    m_hat = m / (1.0 - ADAM_B1 ** ADAM_STEP)
    v_hat = v / (1.0 - ADAM_B2 ** ADAM_STEP)
    delta = -ADAM_LR * (m_hat / (jnp.sqrt(v_hat) + ADAM_EPS) + ADAM_WD * wv)
    return delta, m, v


def adam_parts(parts, w, m, v, tr, name, comm=None):
    L, R, C = w.shape
    nt = R // tr

    def body(*refs):
        p_refs = refs[:L]
        w_ref, m_ref, v_ref, g_ref, d_ref, mo_ref, vo_ref = refs[L:]
        lyr = pl.program_id(0)
        for l in range(L):
            @pl.when(lyr == l)
            def _(p_ref=p_refs[l]):
                gv = p_ref[0].astype(f32)
                for k in range(1, N_DEV):
                    gv = gv + p_ref[k].astype(f32)
                g_ref[...] = gv
                d_ref[...], mo_ref[...], vo_ref[...] = _adam_math(gv, w_ref[...], m_ref[...], v_ref[...])

    def part_spec(l):
        return pl.BlockSpec((N_DEV, tr, C), lambda lyr, i: (0, jnp.where(lyr == l, i, jnp.where(lyr < l, 0, nt - 1)), 0))

    spec = pl.BlockSpec((None, tr, C), lambda lyr, i: (lyr, i, 0))
    shp = jax.ShapeDtypeStruct((L, R, C), f32)
    return _pcall(
        body, (*parts, w, m, v), name=name, comm=comm,
        grid=(L, nt),
        in_specs=[part_spec(l) for l in range(L)] + [spec, spec, spec],
        out_specs=[spec, spec, spec, spec],
        out_shape=[shp, shp, shp, shp],
    )


def adam_plain(gr, w, m, v, tr, name):
    R, C = w.shape

    def body(g_ref, w_ref, m_ref, v_ref, d_ref, mo_ref, vo_ref):
        d_ref[...], mo_ref[...], vo_ref[...] = _adam_math(g_ref[...], w_ref[...], m_ref[...], v_ref[...])

    spec = pl.BlockSpec((tr, C), lambda i: (i, 0))
    shp = jax.ShapeDtypeStruct((R, C), f32)
    return pl.pallas_call(
        body, name=name,
        grid=(R // tr,),
        in_specs=[spec, spec, spec, spec],
        out_specs=[spec, spec, spec],
        out_shape=[shp, shp, shp],
        compiler_params=_cp(1),
    )(gr, w, m, v)


def sum8(parts, name):
    _, R, C = parts.shape

    def body(p_ref, o_ref):
        acc = p_ref[0]
        for k in range(1, N_DEV):
            acc = acc + p_ref[k]
        o_ref[...] = acc

    return pl.pallas_call(
        body, name=name,
        grid=(1,),
        in_specs=[_full(parts.shape)],
        out_specs=_full((R, C)),
        out_shape=jax.ShapeDtypeStruct((R, C), f32),
        compiler_params=_cp(1),
    )(parts)


def _place():
    return lax.axis_index("x"), lax.axis_index("y"), lax.axis_index("c")


def _gather_steps(ins, outs, send_sems, recv_sems, local_sems, place):
    n = len(ins)
    x, y, c = place
    me, sibling = (x, y, c), (x, y, 1 - c)
    chips = [(1 - x, y), (x, 1 - y), (1 - x, 1 - y)]

    def slot(a, p):
        return outs[a].at[4 * p[0] + 2 * p[1] + p[2]]

    def copy(a, k, block, to, src=None):
        return pltpu.make_async_remote_copy(
            src_ref=slot(a, block) if src is None else src, dst_ref=slot(a, block),
            send_sem=send_sems.at[a * 7 + k], recv_sem=recv_sems.at[a * 7 + k],
            device_id=to, device_id_type=MESH)

    def mine():
        return [pltpu.make_async_copy(ins[a], slot(a, me), local_sems.at[a]) for a in range(n)]

    def first():
        cps = []
        for a in range(n):
            cps.append(copy(a, 0, me, sibling, src=ins[a]))
            cps += [copy(a, 1 + j, me, (*chip, c), src=ins[a]) for j, chip in enumerate(chips)]
        return cps

    def start():
        for cp in mine() + first():
            cp.start()

    def forward():
        for j, chip in enumerate(chips):
            for a in range(n):
                copy(a, 1 + j, (*chip, c), me).wait_recv()
                copy(a, 4 + j, (*chip, c), sibling).start()

    def finish():
        for a in range(n):
            copy(a, 0, sibling, me).wait_recv()
            for j, chip in enumerate(chips):
                copy(a, 4 + j, (*chip, 1 - c), me).wait_recv()
        for cp in first() + [copy(a, 4 + j, (*chip, c), sibling) for j, chip in enumerate(chips) for a in range(n)]:
            cp.wait_send()
        for cp in mine():
            cp.wait()

    return start, forward, finish


def _exchange_steps(ins, outs, send_sems, recv_sems, local_sems, place):
    n = len(ins)
    x, y, c = place
    me_i = 4 * x + 2 * y + c

    def mine():
        return [pltpu.make_async_copy(ins[a].at[me_i], outs[a].at[me_i], local_sems.at[a]) for a in range(n)]

    def copies(receiving):
        cps = []
        for k in range(1, N_DEV):
            px = 1 - x if (k >> 2) & 1 else x
            py = 1 - y if (k >> 1) & 1 else y
            pc = 1 - c if k & 1 else c
            p_i = 4 * px + 2 * py + pc
            for a in range(n):
                sem = a * 7 + k - 1
                cps.append(pltpu.make_async_remote_copy(
                    src_ref=ins[a].at[p_i], dst_ref=outs[a].at[p_i if receiving else me_i],
                    send_sem=send_sems.at[sem], recv_sem=recv_sems.at[sem],
                    device_id=(px, py, pc), device_id_type=MESH))
        return cps

    def start():
        for cp in mine() + copies(False):
            cp.start()

    def finish():
        for cp in copies(True):
            cp.wait_recv()
        for cp in copies(False):
            cp.wait_send()
        for cp in mine():
            cp.wait()

    return start, None, finish


_COMM_STEPS = {"gather": _gather_steps, "exchange": _exchange_steps}


def _comm_out_shapes(kind, arrs):
    if kind == "gather":
        return [jax.ShapeDtypeStruct((N_DEV,) + a.shape, a.dtype) for a in arrs]
    return [jax.ShapeDtypeStruct(a.shape, a.dtype) for a in arrs]


def _comm_sems(n):
    return [pltpu.SemaphoreType.DMA((7 * n,)), pltpu.SemaphoreType.DMA((7 * n,)), pltpu.SemaphoreType.DMA((n,))]


def _pcall(body, args, *, name, grid, in_specs, out_specs, out_shape, scratch_shapes=(), comm=None):
    in_specs, out_specs, out_shape = list(in_specs), list(out_specs), list(out_shape)
    scratch_shapes = list(scratch_shapes)
    cparams = _cp(len(grid))
    if comm is None:
        outs = pl.pallas_call(body, name=name, grid=grid, in_specs=in_specs, out_specs=out_specs, out_shape=out_shape,
                              scratch_shapes=scratch_shapes, compiler_params=cparams)(*args)
        return list(outs), []
    kind, arrs = comm
    nc, n_in, n_out, n_scr = len(arrs), len(in_specs), len(out_specs), len(scratch_shapes)
    total = 1
    for gdim in grid:
        total *= gdim
    forward_step = (total * 3) // 4

    def hosted(*refs):
        core_in, c_in = refs[:n_in], refs[n_in:n_in + nc]
        core_out = refs[n_in + nc:n_in + nc + n_out]
        c_out = refs[n_in + nc + n_out:n_in + 2 * nc + n_out]
        rest = refs[n_in + 2 * nc + n_out:]
        step = pl.program_id(0)
        for ax in range(1, len(grid)):
            step = step * grid[ax] + pl.program_id(ax)
        start, forward, finish = _COMM_STEPS[kind](c_in, c_out, *rest[n_scr:], _place())
        pl.when(step == 0)(start)
        if forward is not None:
            pl.when(step == forward_step)(forward)
        body(*core_in, *core_out, *rest[:n_scr])
        pl.when(step == total - 1)(finish)

    any_spec = pl.BlockSpec(memory_space=pl.ANY)
    outs = pl.pallas_call(
        hosted, name=name, grid=grid,
        in_specs=in_specs + [any_spec] * nc,
        out_specs=out_specs + [any_spec] * nc,
        out_shape=out_shape + _comm_out_shapes(kind, arrs),
        scratch_shapes=scratch_shapes + _comm_sems(nc),
        compiler_params=cparams)(*args, *arrs)
    return list(outs[:n_out]), list(outs[n_out:])


def _comm_call(kind, arrs, name):
    n = len(arrs)

    def body(*refs):
        start, forward, finish = _COMM_STEPS[kind](refs[:n], refs[n:2 * n], *refs[2 * n:], _place())
        start()
        if forward is not None:
            forward()
        finish()

    any_spec = pl.BlockSpec(memory_space=pl.ANY)
    return pl.pallas_call(
        body, name=name,
        in_specs=[any_spec] * n, out_specs=[any_spec] * n,
        out_shape=_comm_out_shapes(kind, arrs), scratch_shapes=_comm_sems(n),
    )(*arrs)


def all_gather(arrs, name):
    return _comm_call("gather", arrs, name)


def all_to_all(arrs, name):
    return _comm_call("exchange", arrs, name)


def _tiles(S):
    t = min(512, S)
    return dict(ffn=min(256, S), row=t, conv=t, gla=t, bk=min(1024, S), bk_ffn=t)


BIG = ("wi1", "wo1", "win", "wout", "wi2", "wo2")


def _col_shards_to_full(gathered):
    n, r, c = gathered.shape
    return jnp.transpose(gathered, (1, 0, 2)).reshape(r, n * c)


def _win_full(win_a):
    return _pad_rows(win_a.reshape(DIN, D), DINP)


def train_pass(x, tgt, mods, fmod, sh, ws, wi1_first, wo1_first):
    S = x.shape[0]
    T = _tiles(S)
    bk = T["bk"]
    full = [dict() for _ in range(DEPTH)]
    full[0]["wi1"], full[0]["wo1"] = wi1_first, wo1_first.reshape(F, D)
    saved = []
    xc = x
    for l in range(DEPTH):
        w, fw = ws[f"L{l}"], full[l]
        x0 = xc
        names = ("win", "wout", "wi2", "wo2") if l == 0 else ("wi2", "wo2")
        (x1, h1f, zg1, zu1, f1), got = ffn_fwd(x0, mods[l], w["g1"], fw["wi1"], fw["wo1"], (0, 1, 2), T["ffn"], f"ffn1_fwd_{l}",
                                          comm=("gather", [sh[n][l] for n in names]))
        fw.update(zip(names, got))
        if l == 0:
            fw["win"], fw["wout"] = _win_full(fw["win"]), fw["wout"].reshape(D, D)
        fw["wo2"] = fw["wo2"].reshape(F, D)
        z, la = mixin_fwd(x1, mods[l], w["g2"], fw["win"], w["wgu"], w["bgate"], T["row"], f"mixin_fwd_{l}")
        y, yc, yg, sprev = mixer_core_fwd(z, la, w["gn_s"], w["wdw"], w["cpar"], T["gla"], f"mixer_core_fwd_{l}")
        x2 = mixout_fwd(x1, yc, yg, mods[l], fw["wout"], T["row"], f"mixout_fwd_{l}")
        names = ("wi1", "wo1", "win", "wout") if l + 1 < DEPTH else ()
        (x3, h2f, zg2, zu2, f2), got = ffn_fwd(x2, mods[l], w["g3"], fw["wi2"], fw["wo2"], (6, 7, 8), T["ffn"], f"ffn2_fwd_{l}",
                                          comm=("gather", [sh[n][l + 1] for n in names]) if names else None)
        if names:
            nx = full[l + 1]
            nx["wi1"], nx["wo1"], nx["win"], nx["wout"] = got[0], got[1].reshape(F, D), _win_full(got[2]), got[3].reshape(D, D)
        saved.append(dict(x0=x0, x1=x1, x2=x2, h1f=h1f, zg1=zg1, zu1=zu1, f1=f1, h2f=h2f, zg2=zg2, zu2=zu2, f2=f2,
                          z=z, la=la, y=y, yc=yc, yg=yg, sprev=sprev))
        xc = x3

    dx, redf = final_fwd_bwd(xc, tgt, fmod, ws["gf"], T["row"], "loss_head")
    loss_lanes = redf[0]
    dfmod = redf[1:3]
    grads = {"gf": redf[3]}
    dmods = [None] * DEPTH
    recv = {n: [None] * DEPTH for n in BIG}

    def ffn_backward(xin, dy, h, zg, zu, fo, gain, wi, wo, rows, l, tag, ride=None):
        (dzg, dzu, p_wo), got_ride = ffn_bwd_hidden(dy, zg, zu, mods[l], wo, rows[2], T["ffn"], f"{tag}_bwd_hidden_{l}",
                                                    comm=("exchange", ride) if ride else None)
        p_wi, (r_wo,) = dwi_pieces(h, dzg, dzu, T["bk_ffn"], f"d{tag}_wi_{l}",
                                   comm=("exchange", [p_wo.reshape(N_DEV, F // N_DEV, D)]))
        (dxin, red), (r_wi,) = ffn_bwd_input(xin, dy, dzg, dzu, fo, mods[l], gain, wi, rows, T["ffn"],
                                             f"{tag}_bwd_input_{l}", comm=("exchange", [p_wi]))
        return dxin, red, r_wi, r_wo, got_ride

    for l in reversed(range(DEPTH)):
        w, fw, sv = ws[f"L{l}"], full[l], saved[l]
        g = {}
        dx2, red3, recv["wi2"][l], recv["wo2"][l], _ = ffn_backward(
            sv["x2"], dx, sv["h2f"], sv["zg2"], sv["zu2"], sv["f2"], w["g3"], fw["wi2"], fw["wo2"], (6, 7, 8), l, "ffn2")
        dmix, dyc, dyg, red_o = mixout_bwd(dx2, sv["yc"], sv["yg"], mods[l], fw["wout"], T["row"], f"mixout_bwd_{l}")
        p_wout = jnp.concatenate([matmul_tn(sv["yc"], dmix, DC, D, DC, D, bk, f"dwout_c_{l}", out_dtype=bf16),
                                  matmul_tn(sv["yg"], dmix, DG, D, DG, D, bk, f"dwout_g_{l}", out_dtype=bf16)], axis=0)
        (dq, dk, dv, dr, dpre, redg, redb), (recv["wout"][l],) = gla_bwd(
            sv["z"], sv["la"], sv["sprev"], dyg, w["gn_s"], T["gla"], f"gla_bwd_{l}",
            comm=("exchange", [p_wout.reshape(N_DEV, D // N_DEV, D)]))
        dzab, redc = conv_bwd(sv["z"], sv["y"], dyc, w["wdw"], w["cpar"], T["conv"], f"conv_bwd_{l}")
        (dx1, h2, dz, red2), _ = mixin_bwd(sv["x1"], dx2, dzab, dq, dk, dv, dr, dpre, mods[l], w["g2"], fw["win"], w["wgu"],
                                           T["row"], f"mixin_bwd_{l}")
        dwin_t = matmul_tn(dz, h2, DINP, D, DINP, D, bk, f"dwin_{l}", out_dtype=bf16)
        p_win = dwin_t[:DIN].reshape(N_DEV, DIN // N_DEV, D)
        g["wgu"] = matmul_tn(sv["z"], dpre, 128, DQK, 128, DQK, bk, f"dwgu_{l}", a_col_block=(DINP - 128) // 128)[:GR]
        g["bgate"] = jnp.sum(redb, axis=0)
        g["gn"] = jnp.sum(redg.reshape(NH, CH, DV), axis=1)
        g["wdw"] = redc[:CW]
        g["bdw"], g["gln"], g["bln"] = redc[32], redc[33], redc[34]
        dx0, red1, recv["wi1"][l], recv["wo1"][l], (recv["win"][l],) = ffn_backward(
            sv["x0"], dx1, sv["h1f"], sv["zg1"], sv["zu1"], sv["f1"], w["g1"], fw["wi1"], fw["wo1"], (0, 1, 2), l, "ffn1",
            ride=[p_win])
        g["g1"], g["g2"], g["g3"] = red1[3], red2[2], red3[3]
        dmods[l] = jnp.stack([red1[0], red1[1], red1[2], red2[0], red2[1], red_o[0], red3[0], red3[1], red3[2]], axis=0)
        grads[f"L{l}"] = g
        dx = dx0
    return loss_lanes, dx, grads, dmods, dfmod, recv


def _pad_rows(a, rows):
    return jnp.pad(a, ((0, rows - a.shape[0]), (0, 0)))


def kernel(x, c, w_ada, b_ada, g_norm_ffn1, w_ffn1_in, w_ffn1_out, g_norm_mix, w_in, w_dw, b_dw, g_conv_ln, b_conv_ln, w_gate_up, b_gate, g_gla_norm, w_out, g_norm_ffn2, w_ffn2_in, w_ffn2_out, g_norm_final, w_ada_final, b_ada_final, loss_target, m_w_ada, m_b_ada, m_g_norm_ffn1, m_w_ffn1_in, m_w_ffn1_out, m_g_norm_mix, m_w_in, m_w_dw, m_b_dw, m_g_conv_ln, m_b_conv_ln, m_w_gate_up, m_b_gate, m_g_gla_norm, m_w_out, m_g_norm_ffn2, m_w_ffn2_in, m_w_ffn2_out, m_g_norm_final, m_w_ada_final, m_b_ada_final, v_w_ada, v_b_ada, v_g_norm_ffn1, v_w_ffn1_in, v_w_ffn1_out, v_g_norm_mix, v_w_in, v_w_dw, v_b_dw, v_g_conv_ln, v_b_conv_ln, v_w_gate_up, v_b_gate, v_g_gla_norm, v_w_out, v_g_norm_ffn2, v_w_ffn2_in, v_w_ffn2_out, v_g_norm_final, v_w_ada_final, v_b_ada_final):
    me = 4 * lax.axis_index("x") + 2 * lax.axis_index("y") + lax.axis_index("c")
    L = DEPTH
    n_ada = N_MOD * D // N_DEV
    n_fin = 2 * D // N_DEV

    small = jnp.concatenate([c.reshape(-1), w_dw.reshape(-1), w_gate_up.reshape(-1)])
    n_small = small.shape[0]
    small = jnp.pad(small, (0, 8 * D - n_small)).reshape(8, D)
    big = dict(wi1=w_ffn1_in, wo1=w_ffn1_out, win=w_in, wout=w_out, wi2=w_ffn2_in, wo2=w_ffn2_out)
    transposed = ("wi1", "wi2", "win")
    sh = {n: [(a[l].T if n in transposed else a[l]).astype(bf16) for l in range(L)] for n, a in big.items()}
    small_a, wi1_first, wo1_first = all_gather([small, sh["wi1"][0], sh["wo1"][0]], "gather_first")
    small_a = small_a.reshape(N_DEV, 8 * D)
    c_all = small_a[:, :D]
    o1 = D + L * CW * (DC // N_DEV)
    wdw_full = _col_shards_to_full(small_a[:, D:o1].reshape(N_DEV, L * CW, DC // N_DEV)).reshape(L, CW, DC)
    wgu_full = _col_shards_to_full(small_a[:, o1:o1 + L * GR * (DQK // N_DEV)].reshape(N_DEV, L * GR, DQK // N_DEV)).reshape(L, GR, DQK)

    b_ada_mine = lax.dynamic_slice(b_ada, (0, me * n_ada), (L, n_ada))
    b_fin_mine = lax.dynamic_slice(b_ada_final, (me * n_fin,), (n_fin,))
    parts = [ada_fwd(c_all, w_ada[l], b_ada_mine[l:l + 1], f"ada_fwd_{l}") for l in range(L)]
    parts.append(ada_fwd(c_all, w_ada_final, b_fin_mine.reshape(1, n_fin), "ada_fwd_final"))
    modsrc = jnp.concatenate(parts, axis=1)
    n_row = modsrc.shape[1]
    modsrc = jnp.pad(modsrc, ((0, 0), (0, 24 * 128 - n_row))).reshape(N_DEV, 24, 128)
    (modrecv,) = all_to_all([modsrc], "exchange_mod")
    modrecv = modrecv.reshape(N_DEV, 24 * 128)
    mods = []
    for l in range(L):
        mvec = modrecv[:, l * n_ada:(l + 1) * n_ada].reshape(N_MOD, D)
        mods.append(_pad_rows(mvec, 16))
    fmod = _pad_rows(modrecv[:, L * n_ada:L * n_ada + n_fin].reshape(2, D), 8)

    ws = {"gf": g_norm_final.reshape(1, D)}
    for l in range(L):
        ws[f"L{l}"] = dict(
            g1=g_norm_ffn1[l].reshape(1, D), g2=g_norm_mix[l].reshape(1, D), g3=g_norm_ffn2[l].reshape(1, D),
            wgu=_pad_rows(wgu_full[l], 128).astype(bf16),
            bgate=b_gate[l].reshape(1, DQK),
            wdw=_pad_rows(wdw_full[l], 32),
            cpar=_pad_rows(jnp.stack([b_dw[l], g_conv_ln[l], b_conv_ln[l]]), 8),
            gn_s=jnp.repeat(g_gla_norm[l], CH, axis=0),
        )

    loss_lanes, grad_x, gr, dmods, dfmod, recv = train_pass(
        x[0], loss_target[0], mods, fmod, sh, ws, wi1_first, wo1_first)

    def adam_big(rv, w, m, v, name, is_transposed=False):
        if is_transposed:
            w, m, v = (jnp.swapaxes(a, 1, 2) for a in (w, m, v))
        R = w.shape[1]
        tr = 256 if R % 256 == 0 else (R // 2 if (R // 2) % 16 == 0 else R)
        outs, _ = adam_parts(rv, w, m, v, tr, name)
        return [jnp.swapaxes(o, 1, 2) for o in outs] if is_transposed else outs

    res = {}
    res["w_ffn2_in"] = adam_big(recv["wi2"], w_ffn2_in, m_w_ffn2_in, v_w_ffn2_in, "adam_ffn2_in", True)
    res["w_ffn2_out"] = adam_big(recv["wo2"], w_ffn2_out, m_w_ffn2_out, v_w_ffn2_out, "adam_ffn2_out")
    res["w_in"] = adam_big(recv["win"], w_in, m_w_in, v_w_in, "adam_w_in", True)
    res["w_out"] = adam_big(recv["wout"], w_out, m_w_out, v_w_out, "adam_w_out")
    res["w_ffn1_out"] = adam_big(recv["wo1"], w_ffn1_out, m_w_ffn1_out, v_w_ffn1_out, "adam_ffn1_out")
    res["w_ffn1_in"] = adam_big(recv["wi1"], w_ffn1_in, m_w_ffn1_in, v_w_ffn1_in, "adam_ffn1_in", True)

    flat = lambda name: jnp.stack([gr[f"L{l}"][name] for l in range(L)]).reshape(-1)
    sections = [
        ("b_ada", jnp.stack(dmods).reshape(-1)), ("b_ada_final", dfmod.reshape(-1)),
        ("g_norm_ffn1", flat("g1")), ("g_norm_mix", flat("g2")), ("g_norm_ffn2", flat("g3")), ("g_norm_final", gr["gf"]),
        ("b_dw", flat("bdw")), ("g_conv_ln", flat("gln")), ("b_conv_ln", flat("bln")), ("b_gate", flat("bgate")),
        ("g_gla_norm", flat("gn")),
    ]
    n_rep = sum(s[1].shape[0] for s in sections)
    rep_rows = -(-n_rep // D)
    extra = [("loss", loss_lanes), ("w_dw", flat("wdw")), ("w_gate_up", flat("wgu"))]
    pack = jnp.concatenate([s[1] for s in sections] + [jnp.zeros((rep_rows * D - n_rep,), f32)] + [s[1] for s in extra])
    n_pack = pack.shape[0]
    pack_rows = -(-n_pack // (8 * D)) * 8
    pack = jnp.pad(pack, (0, pack_rows * D - n_pack)).reshape(pack_rows, D)
    (pack_all,) = all_gather([pack], "gather_small_grads")
    tot = sum8(pack_all, "sum_small_grads")
    tot_flat = tot.reshape(-1)
    loss = jnp.sum(tot_flat[rep_rows * D:rep_rows * D + D])
    o_dw = rep_rows * D + D
    g_wdw_full = tot_flat[o_dw:o_dw + L * CW * DC].reshape(L, CW, DC)
    o_gu = o_dw + L * CW * DC
    g_wgu_full = tot_flat[o_gu:o_gu + L * GR * DQK].reshape(L, GR, DQK)

    small_params = dict(b_ada=(b_ada, m_b_ada, v_b_ada), b_ada_final=(b_ada_final, m_b_ada_final, v_b_ada_final),
                        g_norm_ffn1=(g_norm_ffn1, m_g_norm_ffn1, v_g_norm_ffn1), g_norm_mix=(g_norm_mix, m_g_norm_mix, v_g_norm_mix),
                        g_norm_ffn2=(g_norm_ffn2, m_g_norm_ffn2, v_g_norm_ffn2), g_norm_final=(g_norm_final, m_g_norm_final, v_g_norm_final),
                        b_dw=(b_dw, m_b_dw, v_b_dw), g_conv_ln=(g_conv_ln, m_g_conv_ln, v_g_conv_ln),
                        b_conv_ln=(b_conv_ln, m_b_conv_ln, v_b_conv_ln), b_gate=(b_gate, m_b_gate, v_b_gate),
                        g_gla_norm=(g_gla_norm, m_g_gla_norm, v_g_gla_norm))

    def rep_pack(idx):
        p = jnp.concatenate([small_params[s[0]][idx].reshape(-1) for s in sections])
        return jnp.pad(p, (0, rep_rows * D - n_rep)).reshape(rep_rows, D)

    g_rep = tot[:rep_rows]
    d_rep, m_rep, v_rep = adam_plain(g_rep, rep_pack(0), rep_pack(1), rep_pack(2), rep_rows, "adam_small")
    off = 0
    for sname, sval in sections:
        shp = small_params[sname][0].shape
        nel = sval.shape[0]
        res[sname] = [a.reshape(-1)[off:off + nel].reshape(shp) for a in (g_rep, d_rep, m_rep, v_rep)]
        off += nel

    def adam_cols(g_full, w, m, v, name):
        shp = w.shape
        g_mine = lax.dynamic_slice(g_full, (0, 0, me * shp[2]), shp)
        R, C = shp[0] * shp[1], shp[2]
        outs = adam_plain(g_mine.reshape(R, C), w.reshape(R, C), m.reshape(R, C), v.reshape(R, C), R, name)
        return [g_mine] + [o.reshape(shp) for o in outs]

    res["w_dw"] = adam_cols(g_wdw_full, w_dw, m_w_dw, v_w_dw, "adam_w_dw")
    res["w_gate_up"] = adam_cols(g_wgu_full, w_gate_up, m_w_gate_up, v_w_gate_up, "adam_w_gate_up")

    c_all_t = c_all.T
    dmod_all = pack_all.reshape(N_DEV, -1)[:, :L * N_MOD * D].reshape(N_DEV, L, N_MOD * D)
    dfm_all = pack_all.reshape(N_DEV, -1)[:, L * N_MOD * D:L * N_MOD * D + 2 * D]
    dm_mine = lax.dynamic_slice(dmod_all, (0, 0, me * n_ada), (N_DEV, L, n_ada))
    dfm_mine = lax.dynamic_slice(dfm_all, (0, me * n_fin), (N_DEV, n_fin))
    g_w_ada = jnp.stack([ada_wgrad(c_all_t, dm_mine[:, l], f"ada_wgrad_{l}") for l in range(L)])
    g_w_fin = ada_wgrad(c_all_t, dfm_mine, "ada_wgrad_final")
    outs = adam_plain(g_w_ada.reshape(L * D, n_ada), w_ada.reshape(L * D, n_ada), m_w_ada.reshape(L * D, n_ada),
                      v_w_ada.reshape(L * D, n_ada), 256, "adam_w_ada")
    res["w_ada"] = [g_w_ada] + [o.reshape(w_ada.shape) for o in outs]
    res["w_ada_final"] = [g_w_fin] + list(adam_plain(g_w_fin, w_ada_final, m_w_ada_final, v_w_ada_final, 256, "adam_w_ada_final"))

    order = ["w_ada", "b_ada", "g_norm_ffn1", "w_ffn1_in", "w_ffn1_out", "g_norm_mix", "w_in", "w_dw", "b_dw", "g_conv_ln",
             "b_conv_ln", "w_gate_up", "b_gate", "g_gla_norm", "w_out", "g_norm_ffn2", "w_ffn2_in", "w_ffn2_out",
             "g_norm_final", "w_ada_final", "b_ada_final"]
    out = [loss, grad_x[None]]
    for k in range(4):
        out += [res[name][k] for name in order]
    return tuple(out)
```

```python
import functools

import jax
import jax.numpy as jnp
from jax import lax
from jax.experimental import pallas as pl
from jax.experimental.pallas import tpu as pltpu

f32 = jnp.float32
bf16 = jnp.bfloat16

N_DEV = 8
DEPTH = 2
D = 1024
F = 2816
DC = 512
NH = 4
DK = 64
DV = 128
DQK = NH * DK
DG = NH * DV
CH = 64
CW = 31
GR = 16
TAU = 16.0
N_MOD = 9
DIN = 2 * DC + 2 * DQK + 2 * DG + GR
DINP = 2688
EPS = 1e-6
HALO = 32
SUBLANES = 8
CONV_ROWS = 32
FS = 2 * F // N_DEV

ADAM_LR = 0.001
ADAM_B1 = 0.9
ADAM_B2 = 0.999
ADAM_EPS = 1e-08
ADAM_WD = 0.01
ADAM_STEP = 10

V7X_VMEM_LIMIT = 56 * 1024 * 1024
MESH = pl.DeviceIdType.MESH
HIGHEST = lax.Precision.HIGHEST

NT = (((1,), (1,)), ((), ()))
TN = (((0,), (0,)), ((), ()))


def _cp(n_axes):
    return pltpu.CompilerParams(dimension_semantics=("arbitrary",) * n_axes, vmem_limit_bytes=V7X_VMEM_LIMIT)


def _full(shape):
    nd = len(shape)
    return pl.BlockSpec(shape, lambda *_: (0,) * nd)


def _resident(shape):
    nd = len(shape)
    return pl.BlockSpec(shape, lambda *_: (0,) * nd, pipeline_mode=pl.Buffered(1))


def _dot(a, b):
    return jnp.dot(a, b, preferred_element_type=f32)


def _dg(a, b, dims):
    return lax.dot_general(a, b, dims, preferred_element_type=f32)


def _sigmoid(x):
    return jax.nn.sigmoid(x)


def _rowsum(x):
    return jnp.sum(x, axis=0, keepdims=True)


def _rms_parts(xv):
    rstd = lax.rsqrt(jnp.mean(xv * xv, axis=-1, keepdims=True) + EPS)
    return xv * rstd, rstd


def _rms_bwd(dxh, xh, rstd):
    return rstd * (dxh - xh * jnp.mean(dxh * xh, axis=-1, keepdims=True))


def ffn_fwd(x, mod, g, wi_t, wo, rows, tm, name, comm=None):
    S = x.shape[0]
    r_shift, r_scale, r_gate = rows

    def body(x_ref, mod_ref, g_ref, wi_ref, wo_ref, xo_ref, h_ref, z_ref, f_ref):
        xv = x_ref[...]
        xh, _ = _rms_parts(xv)
        h = (xh * g_ref[...] * (1.0 + mod_ref[r_scale:r_scale + 1, :]) + mod_ref[r_shift:r_shift + 1, :]).astype(bf16)
        h_ref[...] = h
        zg = _dg(h, wi_ref[0:F, :], NT)
        zu = _dg(h, wi_ref[F:2 * F, :], NT)
        z_ref[:, 0:F] = zg.astype(bf16)
        z_ref[:, F:2 * F] = zu.astype(bf16)
        fv = _dot((zg * _sigmoid(zg) * zu).astype(bf16), wo_ref[...])
        f_ref[...] = fv.astype(bf16)
        xo_ref[...] = xv + 0.5 * mod_ref[r_gate:r_gate + 1, :] * fv

    row = lambda i: (i, 0)
    return _pcall(
        body, (x, mod, g, wi_t, wo), name=name, comm=comm,
        grid=(S // tm,),
        in_specs=[pl.BlockSpec((tm, D), row), _full(mod.shape), _full(g.shape), _resident(wi_t.shape), _resident(wo.shape)],
        out_specs=[pl.BlockSpec((tm, D), row), pl.BlockSpec((tm, D), row), pl.BlockSpec((tm, 2 * F), row),
                   pl.BlockSpec((tm, D), row)],
        out_shape=[jax.ShapeDtypeStruct((S, D), f32), jax.ShapeDtypeStruct((S, D), bf16),
                   jax.ShapeDtypeStruct((S, 2 * F), bf16), jax.ShapeDtypeStruct((S, D), bf16)],
    )


def ffn_bwd_hidden(dy, z, mod, wo_t, r_gate, tm, name, comm=None):
    S = dy.shape[0]
    nt = S // tm
    halves = 2
    fc = F // halves

    def body(dy_ref, z_ref, mod_ref, wo_ref, dz_ref, dwo_ref, acc_s):
        i = pl.program_id(0)

        @pl.when(i == 0)
        def _():
            acc_s[...] = jnp.zeros_like(acc_s)

        df = (0.5 * mod_ref[r_gate:r_gate + 1, :] * dy_ref[...]).astype(bf16)
        for c in range(halves):
            lo, hi = c * fc, (c + 1) * fc
            zgv = z_ref[:, lo:hi].astype(f32)
            zuv = z_ref[:, F + lo:F + hi].astype(f32)
            s = _sigmoid(zgv)
            sil = zgv * s
            acc_s[lo:hi, :] += _dg((sil * zuv).astype(bf16), df, TN)
            da = _dot(df, wo_ref[:, lo:hi])
            dz_ref[:, F + lo:F + hi] = (da * sil).astype(bf16)
            dz_ref[:, lo:hi] = (da * zuv * (s * (1.0 + zgv * (1.0 - s)))).astype(bf16)

        @pl.when(i == nt - 1)
        def _():
            dwo_ref[...] = acc_s[...].astype(bf16)

    row = lambda i: (i, 0)
    return _pcall(
        body, (dy, z, mod, wo_t), name=name, comm=comm,
        grid=(nt,),
        in_specs=[pl.BlockSpec((tm, D), row), pl.BlockSpec((tm, 2 * F), row), _full(mod.shape), _resident(wo_t.shape)],
        out_specs=[pl.BlockSpec((tm, 2 * F), row), _full((F, D))],
        out_shape=[jax.ShapeDtypeStruct((S, 2 * F), bf16), jax.ShapeDtypeStruct((F, D), bf16)],
        scratch_shapes=[pltpu.VMEM((F, D), f32)],
    )


def ffn_bwd_input(x, dy, dz, fo, mod, g, wi_t, rows, tm, name, comm=None):
    S = x.shape[0]
    r_shift, r_scale, r_gate = rows

    def body(x_ref, dy_ref, dz_ref, f_ref, mod_ref, g_ref, wi_ref, dx_ref, red_ref):
        @pl.when(pl.program_id(0) == 0)
        def _():
            red_ref[...] = jnp.zeros_like(red_ref)

        dh = _dot(dz_ref[...], wi_ref[...])
        dyv = dy_ref[...]
        xh, rstd = _rms_parts(x_ref[...])
        gv = g_ref[...]
        n = xh * gv
        dn = dh * (1.0 + mod_ref[r_scale:r_scale + 1, :])
        red_ref[0:1, :] += _rowsum(dh)
        red_ref[1:2, :] += _rowsum(dh * n)
        red_ref[2:3, :] += _rowsum(0.5 * f_ref[...].astype(f32) * dyv)
        red_ref[3:4, :] += _rowsum(dn * xh)
        dx_ref[...] = dyv + _rms_bwd(dn * gv, xh, rstd)

    row = lambda i: (i, 0)
    return _pcall(
        body, (x, dy, dz, fo, mod, g, wi_t), name=name, comm=comm,
        grid=(S // tm,),
        in_specs=[pl.BlockSpec((tm, D), row), pl.BlockSpec((tm, D), row), pl.BlockSpec((tm, 2 * F), row),
                  pl.BlockSpec((tm, D), row), _full(mod.shape), _full(g.shape), _resident(wi_t.shape)],
        out_specs=[pl.BlockSpec((tm, D), row), _full((8, D))],
        out_shape=[jax.ShapeDtypeStruct((S, D), f32), jax.ShapeDtypeStruct((8, D), f32)],
    )


def matmul_tn(a, b, M, N, bm, bn, bk, name, a_col_block=0, out_dtype=f32):
    S = b.shape[0]
    nk = S // bk

    def body(a_ref, b_ref, o_ref, acc_s):
        k = pl.program_id(2)

        @pl.when(k == 0)
        def _():
            acc_s[...] = jnp.zeros_like(acc_s)

        acc_s[...] += _dg(a_ref[...].astype(bf16), b_ref[...].astype(bf16), TN)

        @pl.when(k == nk - 1)
        def _():
            o_ref[...] = acc_s[...].astype(out_dtype)

    return pl.pallas_call(
        body, name=name,
        grid=(M // bm, N // bn, nk),
        in_specs=[
            pl.BlockSpec((bk, bm), lambda i, j, k: (k, i + a_col_block)),
            pl.BlockSpec((bk, bn), lambda i, j, k: (k, j)),
        ],
        out_specs=pl.BlockSpec((bm, bn), lambda i, j, k: (i, j)),
        out_shape=jax.ShapeDtypeStruct((M, N), out_dtype),
        scratch_shapes=[pltpu.VMEM((bm, bn), f32)],
        compiler_params=_cp(3),
    )(a, b)


def dwi_pieces(h, dz, bk, name, comm=None):
    S = h.shape[0]
    nk = S // bk

    def body(h_ref, dz_ref, o_ref, acc_s):
        k = pl.program_id(1)

        @pl.when(k == 0)
        def _():
            acc_s[...] = jnp.zeros_like(acc_s)

        acc_s[...] += _dg(dz_ref[...], h_ref[...], TN)

        @pl.when(k == nk - 1)
        def _():
            o_ref[...] = acc_s[...].astype(bf16)

    (out,), comm_outs = _pcall(
        body, (h, dz), name=name, comm=comm,
        grid=(2, nk),
        in_specs=[pl.BlockSpec((bk, D), lambda half, k: (k, 0)), pl.BlockSpec((bk, F), lambda half, k: (k, half))],
        out_specs=[pl.BlockSpec((F, D), lambda half, k: (half, 0))],
        out_shape=[jax.ShapeDtypeStruct((2 * F, D), bf16)],
        scratch_shapes=[pltpu.VMEM((F, D), f32)],
    )
    return out.reshape(N_DEV, FS, D), comm_outs


def mixin_fwd(x1, mod, g, win, wgu, bgate, tm, name):
    S = x1.shape[0]

    def body(x_ref, mod_ref, g_ref, win_ref, wgu_ref, bg_ref, z_ref, la_ref):
        xh, _ = _rms_parts(x_ref[...])
        hv = xh * g_ref[...] * (1.0 + mod_ref[4:5, :]) + mod_ref[3:4, :]
        z = _dg(hv.astype(bf16), win_ref[...], NT)
        z_ref[...] = z
        glr = z[:, DINP - 128:]
        pre = _dot(glr.astype(bf16), wgu_ref[...]) + bg_ref[...]
        la_ref[...] = (jnp.minimum(pre, 0.0) - jnp.log(1.0 + jnp.exp(-jnp.abs(pre)))) * (1.0 / TAU)

    return pl.pallas_call(
        body, name=name,
        grid=(S // tm,),
        in_specs=[pl.BlockSpec((tm, D), lambda i: (i, 0)), _full(mod.shape), _full(g.shape),
                  _full(win.shape), _full(wgu.shape), _full(bgate.shape)],
        out_specs=[pl.BlockSpec((tm, DINP), lambda i: (i, 0)), pl.BlockSpec((tm, DQK), lambda i: (i, 0))],
        out_shape=[jax.ShapeDtypeStruct((S, DINP), f32), jax.ShapeDtypeStruct((S, DQK), f32)],
        compiler_params=_cp(1),
    )(x1, mod, g, win, wgu, bgate)


def mixin_bwd(x1, dres, dzab, dq, dk, dv, dr, dpre, mod, g, win, wgu, tm, name, comm=None):
    S = x1.shape[0]

    def body(x_ref, dres_ref, dzab_ref, dq_ref, dk_ref, dv_ref, dr_ref, dpre_ref, mod_ref, g_ref, win_ref, wgu_ref,
             dx_ref, h_ref, dz_ref, red_ref):
        @pl.when(pl.program_id(0) == 0)
        def _():
            red_ref[...] = jnp.zeros_like(red_ref)

        dglr = _dg(dpre_ref[...], wgu_ref[...], NT).astype(bf16)
        dz = jnp.concatenate([dzab_ref[...], dq_ref[...], dk_ref[...], dv_ref[...], dr_ref[...], dglr], axis=1)
        dz_ref[...] = dz
        dh = _dot(dz, win_ref[...])
        xh, rstd = _rms_parts(x_ref[...])
        gv = g_ref[...]
        n = xh * gv
        sc = 1.0 + mod_ref[4:5, :]
        h_ref[...] = (n * sc + mod_ref[3:4, :]).astype(bf16)
        dn = dh * sc
        red_ref[0:1, :] += _rowsum(dh)
        red_ref[1:2, :] += _rowsum(dh * n)
        red_ref[2:3, :] += _rowsum(dn * xh)
        dx_ref[...] = dres_ref[...] + _rms_bwd(dn * gv, xh, rstd)

    row = lambda i: (i, 0)
    return _pcall(
        body, (x1, dres, dzab, dq, dk, dv, dr, dpre, mod, g, win, wgu), name=name, comm=comm,
        grid=(S // tm,),
        in_specs=[pl.BlockSpec((tm, D), row), pl.BlockSpec((tm, D), row),
                  pl.BlockSpec((tm, 2 * DC), row), pl.BlockSpec((tm, DQK), row), pl.BlockSpec((tm, DQK), row),
                  pl.BlockSpec((tm, DG), row), pl.BlockSpec((tm, DG), row), pl.BlockSpec((tm, DQK), row),
                  _full(mod.shape), _full(g.shape), _full(win.shape), _full(wgu.shape)],
        out_specs=[pl.BlockSpec((tm, D), row), pl.BlockSpec((tm, D), row), pl.BlockSpec((tm, DINP), row), _full((8, D))],
        out_shape=[jax.ShapeDtypeStruct((S, D), f32), jax.ShapeDtypeStruct((S, D), bf16),
                   jax.ShapeDtypeStruct((S, DINP), bf16), jax.ShapeDtypeStruct((8, D), f32)],
    )


def _glu(zab):
    return zab[:, :DC] * _sigmoid(zab[:, DC:])


def _shift_copies(src_s, dst_s, tc):
    n = tc + HALO - SUBLANES
    for b in range(1, SUBLANES):
        dst_s[b, 0:n, :] = src_s[b:b + n, :]


def _shifted(src_s, dst_s, o, tc):
    b = o % SUBLANES
    a = o - b
    return src_s[a:a + tc, :] if b == 0 else dst_s[b, a:a + tc, :]


def _conv_fwd_prepare(first, zc_ref, zp_ref, u_s, us_s, tc):
    up = _glu(zp_ref[...])
    u_s[0:HALO, :] = jnp.where(first, 0.0, up)
    u_s[HALO:HALO + tc, :] = _glu(zc_ref[...])
    _shift_copies(u_s, us_s, tc)


def _conv_fwd_rows(r0, n, w_ref, cp_ref, y_ref, yc_ref, u_s, us_s):
    for r in range(r0, r0 + n, CONV_ROWS):
        acc = _shifted(u_s, us_s, HALO - (CW - 1) + r, CONV_ROWS) * w_ref[0:1, :]
        for w in range(1, CW):
            acc = acc + _shifted(u_s, us_s, HALO - (CW - 1) + w + r, CONV_ROWS) * w_ref[w:w + 1, :]
        y = acc + cp_ref[0:1, :]
        y_ref[r:r + CONV_ROWS, :] = y
        yc = y - jnp.mean(y, axis=-1, keepdims=True)
        yl = yc * lax.rsqrt(jnp.mean(yc * yc, axis=-1, keepdims=True) + EPS) * cp_ref[1:2, :] + cp_ref[2:3, :]
        yc_ref[r:r + CONV_ROWS, :] = (yl * _sigmoid(yl)).astype(bf16)


def _conv_bwd_prepare(first, last, zc_ref, zp_ref, y_ref, yn_ref, d_ref, dn_ref, cp_ref, red_ref, u_s, dy_s, us_s, dys_s, tc):
    gl = cp_ref[1:2, :]
    bl = cp_ref[2:3, :]

    def ln_bwd(yv, dv):
        yc = yv - jnp.mean(yv, axis=-1, keepdims=True)
        rstd = lax.rsqrt(jnp.mean(yc * yc, axis=-1, keepdims=True) + EPS)
        yh = yc * rstd
        yl = yh * gl + bl
        s = _sigmoid(yl)
        dyl = dv * (s * (1.0 + yl * (1.0 - s)))
        dyh = dyl * gl
        dyv = rstd * (dyh - jnp.mean(dyh, axis=-1, keepdims=True) - yh * jnp.mean(dyh * yh, axis=-1, keepdims=True))
        return dyv, dyl, yh

    dy_c, dyl_c, yh_c = ln_bwd(y_ref[...], d_ref[...])
    dy_n, _, _ = ln_bwd(yn_ref[...], dn_ref[...])
    dy_s[0:tc, :] = dy_c
    dy_s[tc:tc + HALO, :] = jnp.where(last, 0.0, dy_n)
    u_s[0:HALO, :] = jnp.where(first, 0.0, _glu(zp_ref[...]))
    u_s[HALO:HALO + tc, :] = _glu(zc_ref[...])
    _shift_copies(u_s, us_s, tc)
    _shift_copies(dy_s, dys_s, tc)
    red_ref[32:33, :] += _rowsum(dy_c)
    red_ref[33:34, :] += _rowsum(dyl_c * yh_c)
    red_ref[34:35, :] += _rowsum(dyl_c)


def _conv_bwd_input_rows(r0, n, zc_ref, w_ref, dz_ref, dy_s, dys_s):
    for r in range(r0, r0 + n, CONV_ROWS):
        du = _shifted(dy_s, dys_s, CW - 1 + r, CONV_ROWS) * w_ref[0:1, :]
        for w in range(1, CW):
            du = du + _shifted(dy_s, dys_s, CW - 1 - w + r, CONV_ROWS) * w_ref[w:w + 1, :]
        zc = zc_ref[r:r + CONV_ROWS, :]
        av = zc[:, :DC]
        sb = _sigmoid(zc[:, DC:])
        dz_ref[r:r + CONV_ROWS, :] = jnp.concatenate([du * sb, du * av * sb * (1.0 - sb)], axis=1).astype(dz_ref.dtype)


def _conv_bwd_taps(w0, w1, red_ref, u_s, us_s, dy_s, tc):
    for w in range(w0, w1):
        part = None
        for r in range(0, tc, CONV_ROWS):
            prod = _shifted(u_s, us_s, HALO - (CW - 1) + w + r, CONV_ROWS) * dy_s[r:r + CONV_ROWS, :]
            fold = jnp.sum(prod.reshape(CONV_ROWS // SUBLANES, SUBLANES, DC), axis=0)
            part = fold if part is None else part + fold
        red_ref[w:w + 1, :] += _rowsum(part)


def conv_bwd(z, y, dyc, wdw, cpar, tc, name):
    S = z.shape[0]
    nb = tc // HALO
    nt = S // tc
    last_halo = S // HALO - 1

    def body(zc_ref, zp_ref, y_ref, yn_ref, d_ref, dn_ref, w_ref, cp_ref, dz_ref, red_ref, u_s, dy_s, us_s, dys_s):
        i = pl.program_id(0)

        @pl.when(i == 0)
        def _():
            red_ref[...] = jnp.zeros_like(red_ref)

        _conv_bwd_prepare(i == 0, i == nt - 1, zc_ref, zp_ref, y_ref, yn_ref, d_ref, dn_ref, cp_ref, red_ref,
                          u_s, dy_s, us_s, dys_s, tc)
        _conv_bwd_input_rows(0, tc, zc_ref, w_ref, dz_ref, dy_s, dys_s)
        _conv_bwd_taps(0, CW, red_ref, u_s, us_s, dy_s, tc)

    cur = lambda i: (i, 0)
    nxt = lambda i: (jnp.minimum((i + 1) * nb, last_halo), 0)
    return pl.pallas_call(
        body, name=name,
        grid=(nt,),
        in_specs=[pl.BlockSpec((tc, 2 * DC), cur),
                  pl.BlockSpec((HALO, 2 * DC), lambda i: (jnp.maximum(i * nb - 1, 0), 0)),
                  pl.BlockSpec((tc, DC), cur), pl.BlockSpec((HALO, DC), nxt),
                  pl.BlockSpec((tc, DC), cur), pl.BlockSpec((HALO, DC), nxt),
                  _full(wdw.shape), _full(cpar.shape)],
        out_specs=[pl.BlockSpec((tc, 2 * DC), cur), _full((40, DC))],
        out_shape=[jax.ShapeDtypeStruct((S, 2 * DC), bf16), jax.ShapeDtypeStruct((40, DC), f32)],
        scratch_shapes=[pltpu.VMEM((HALO + tc, DC), f32), pltpu.VMEM((tc + HALO, DC), f32),
                        pltpu.VMEM((SUBLANES, HALO + tc, DC), f32), pltpu.VMEM((SUBLANES, HALO + tc, DC), f32)],
        compiler_params=_cp(1),
    )(z, z, y, y, dyc, dyc, wdw, cpar)


def _gla_consts():
    r = lax.broadcasted_iota(jnp.int32, (CH, CH), 0)
    c = lax.broadcasted_iota(jnp.int32, (CH, CH), 1)
    tril = r >= c
    lane = lax.broadcasted_iota(jnp.int32, (CH, DQK), 1)
    masks = [(lane >= h * DK) & (lane < (h + 1) * DK) for h in range(NH)]
    r4 = lax.broadcasted_iota(jnp.int32, (DQK, DQK), 0)
    c4 = lax.broadcasted_iota(jnp.int32, (DQK, DQK), 1)
    eye4 = (r4 == c4).astype(f32)
    rs = lax.broadcasted_iota(jnp.int32, (DQK, CH), 0) & (CH - 1)
    tril4 = rs >= lax.broadcasted_iota(jnp.int32, (DQK, CH), 1)
    return tril, tril4, masks, eye4


def _stack(xv, masks):
    return jnp.concatenate([jnp.where(m, xv, 0.0) for m in masks], axis=0)


def _unstack(rv, masks):
    out = jnp.where(masks[0], rv[0:CH, :], 0.0)
    for h in range(1, NH):
        out = out + jnp.where(masks[h], rv[h * CH:(h + 1) * CH, :], 0.0)
    return out


def _vstack(xv):
    return jnp.concatenate([xv[:, h * DV:(h + 1) * DV] for h in range(NH)], axis=0)


def _vunstack(xv):
    return jnp.concatenate([xv[h * CH:(h + 1) * CH, :] for h in range(NH)], axis=1)


def _gla_chunk_fwd(lac, qc, kc, vc, s_all, tril, masks, tril4):
    lmat = tril.astype(f32)
    bc = jnp.dot(lmat, lac, preferred_element_type=f32, precision=HIGHEST)
    bend = bc[CH - 1:CH, :]
    eb = jnp.exp(bc)
    enb = jnp.exp(-bc)
    ed = jnp.exp(bend - bc)
    qh = qc * (DK ** -0.5)
    qf = qh * eb
    qn = qh * enb
    kn = kc * enb
    kp = kc * eb
    kd = kc * ed
    qf_s = _stack(qf, masks).astype(bf16)
    qn_s = _stack(qn, masks).astype(bf16)
    kn_b = kn.astype(bf16)
    kp_b = kp.astype(bf16)
    attf = _dg(qf_s, kn_b, NT)
    attb = _dg(qn_s, kp_b, NT)
    a_s = jnp.where(tril4, attf, attb)
    a_b = a_s.astype(bf16)
    v_b = vc.astype(bf16)
    intra = jnp.concatenate(
        [_dot(a_b[h * CH:(h + 1) * CH, :], v_b[:, h * DV:(h + 1) * DV]) for h in range(NH)], axis=0)
    o_s = intra + _dot(qf_s, s_all.astype(bf16))
    return dict(bc=bc, bend=bend, eb=eb, enb=enb, ed=ed, qf=qf, qn=qn, kn=kn, kp=kp, kd=kd,
                qf_s=qf_s, qn_s=qn_s, kn_b=kn_b, kp_b=kp_b, a_b=a_b, v_b=v_b, o_s=o_s)


def _col_from_row(row, eye4):
    return jnp.sum(eye4 * row, axis=1, keepdims=True)


def _row_from_col(col, eye4):
    return jnp.sum(eye4 * col, axis=0, keepdims=True)


def _gla_fwd_chunk(c, consts, q_ref, k_ref, v_ref, r_ref, la_ref, gn_ref, yg_ref, sp_ref, st):
    tril, tril4, masks, eye4 = consts
    r0, s0 = c * CH, c * DQK
    s_all = st[...]
    sp_ref[s0:s0 + DQK, :] = s_all
    vc = v_ref[r0:r0 + CH, :]
    t = _gla_chunk_fwd(la_ref[r0:r0 + CH, :], q_ref[r0:r0 + CH, :], k_ref[r0:r0 + CH, :], vc,
                       s_all, tril, masks, tril4)
    u_all = _dg(_stack(t["kd"], masks).astype(bf16), _vstack(vc).astype(bf16), TN)
    st[...] = _col_from_row(jnp.exp(t["bend"]), eye4) * s_all + u_all
    o_s = t["o_s"]
    on = o_s * lax.rsqrt(jnp.mean(o_s * o_s, axis=-1, keepdims=True) + EPS) * gn_ref[...]
    rc = r_ref[r0:r0 + CH, :]
    yg_ref[r0:r0 + CH, :] = (_vunstack(on) * (rc * _sigmoid(rc))).astype(bf16)


def mixer_core_fwd(z, la, gn_s, wdw, cpar, t, name):
    S = z.shape[0]
    nb = t // HALO
    nc = t // CH

    def body(zc_ref, zp_ref, q_ref, k_ref, v_ref, r_ref, la_ref, gn_ref, w_ref, cp_ref,
             y_ref, yc_ref, yg_ref, sp_ref, st, u_s, us_s):
        i = pl.program_id(0)

        @pl.when(i == 0)
        def _():
            st[...] = jnp.zeros_like(st)

        _conv_fwd_prepare(i == 0, zc_ref, zp_ref, u_s, us_s, t)
        consts = _gla_consts()
        for c in range(nc):
            _conv_fwd_rows(c * CH, CH, w_ref, cp_ref, y_ref, yc_ref, u_s, us_s)
            _gla_fwd_chunk(c, consts, q_ref, k_ref, v_ref, r_ref, la_ref, gn_ref, yg_ref, sp_ref, st)

    row = lambda i: (i, 0)
    return pl.pallas_call(
        body, name=name,
        grid=(S // t,),
        in_specs=[pl.BlockSpec((t, 2 * DC), row),
                  pl.BlockSpec((HALO, 2 * DC), lambda i: (jnp.maximum(i * nb - 1, 0), 0)),
                  pl.BlockSpec((t, DQK), lambda i: (i, 4)), pl.BlockSpec((t, DQK), lambda i: (i, 5)),
                  pl.BlockSpec((t, DG), lambda i: (i, 3)), pl.BlockSpec((t, DG), lambda i: (i, 4)),
                  pl.BlockSpec((t, DQK), row), _full(gn_s.shape), _full(wdw.shape), _full(cpar.shape)],
        out_specs=[pl.BlockSpec((t, DC), row), pl.BlockSpec((t, DC), row), pl.BlockSpec((t, DG), row),
                   pl.BlockSpec((nc * DQK, DV), row)],
        out_shape=[jax.ShapeDtypeStruct((S, DC), f32), jax.ShapeDtypeStruct((S, DC), bf16),
                   jax.ShapeDtypeStruct((S, DG), bf16), jax.ShapeDtypeStruct((S // CH * DQK, DV), f32)],
        scratch_shapes=[pltpu.VMEM((DQK, DV), f32), pltpu.VMEM((HALO + t, DC), f32),
                        pltpu.VMEM((SUBLANES, HALO + t, DC), f32)],
        compiler_params=_cp(1),
    )(z, z, z, z, z, z, la, gn_s, wdw, cpar)


def _gla_bwd_chunk(c, consts, umat, last_row, q_ref, k_ref, v_ref, r_ref, la_ref, sp_ref, dy_ref, gn_ref,
                   dq_ref, dk_ref, dv_ref, dr_ref, dpre_ref, redg_ref, redb_ref, gs):
    tril, tril4, masks, eye4 = consts
    r0, s0 = c * CH, c * DQK
    rows = slice(r0, r0 + CH)
    s_all = sp_ref[s0:s0 + DQK, :]
    lac = la_ref[rows, :]
    vc = v_ref[rows, :]
    rc = r_ref[rows, :]
    t = _gla_chunk_fwd(lac, q_ref[rows, :], k_ref[rows, :], vc, s_all, tril, masks, tril4)
    g_all = gs[...]
    g_b = g_all.astype(bf16)
    s_b = s_all.astype(bf16)
    o_s = t["o_s"]
    rstd = lax.rsqrt(jnp.mean(o_s * o_s, axis=-1, keepdims=True) + EPS)
    oh = o_s * rstd
    gnv = gn_ref[...]
    sr = _sigmoid(rc)
    dyv = dy_ref[rows, :]
    dr_ref[rows, :] = (dyv * _vunstack(oh * gnv) * (sr * (1.0 + rc * (1.0 - sr)))).astype(dr_ref.dtype)
    don = _vstack(dyv * (rc * sr))
    redg_ref[...] += don * oh
    doh = don * gnv
    do_s = rstd * (doh - oh * jnp.mean(doh * oh, axis=-1, keepdims=True))
    do_b = do_s.astype(bf16)
    v_b = t["v_b"]
    vst_b = _vstack(vc).astype(bf16)
    kd_s = _stack(t["kd"], masks).astype(bf16)
    da_s = jnp.concatenate(
        [_dg(do_b[h * CH:(h + 1) * CH, :], v_b[:, h * DV:(h + 1) * DV], NT) for h in range(NH)], axis=0)
    a_b = t["a_b"]
    dv_s = jnp.concatenate(
        [_dg(a_b[h * CH:(h + 1) * CH, :], do_b[h * CH:(h + 1) * CH, :], TN) for h in range(NH)], axis=0)
    dv_s = dv_s + _dot(kd_s, g_b)
    dv_ref[rows, :] = _vunstack(dv_s).astype(dv_ref.dtype)
    gend = jnp.exp(t["bend"])
    gcol = _col_from_row(gend, eye4)
    gs[...] = gcol * g_all + _dg(t["qf_s"], do_b, TN)
    dgcol = jnp.sum(g_all * s_all, axis=1, keepdims=True)
    dbend = _row_from_col(dgcol * gcol, eye4)
    dkd = _unstack(_dg(vst_b, g_b, NT), masks)
    daf = jnp.where(tril4, da_s, 0.0).astype(bf16)
    dab = jnp.where(tril4, 0.0, da_s).astype(bf16)
    dqf = _unstack(_dot(daf, t["kn_b"]) + _dg(do_b, s_b, NT), masks)
    dqn = _unstack(_dot(dab, t["kp_b"]), masks)
    dkn = _dg(daf, t["qf_s"], TN)
    dkp = _dg(dab, t["qn_s"], TN)
    dq_ref[rows, :] = ((dqf * t["eb"] + dqn * t["enb"]) * (DK ** -0.5)).astype(dq_ref.dtype)
    dk_ref[rows, :] = (dkn * t["enb"] + dkp * t["eb"] + dkd * t["ed"]).astype(dk_ref.dtype)
    dkd_kd = dkd * t["kd"]
    dbc = dqf * t["qf"] - dqn * t["qn"] - dkn * t["kn"] + dkp * t["kp"] - dkd_kd
    dbc = dbc + jnp.where(last_row, _rowsum(dkd_kd) + dbend, 0.0)
    dla = jnp.dot(umat, dbc, preferred_element_type=f32, precision=HIGHEST)
    dpre = dla * (1.0 / TAU) * (1.0 - jnp.exp(TAU * lac))
    dpre_ref[rows, :] = dpre.astype(dpre_ref.dtype)
    redb_ref[...] += dpre


def gla_bwd(z, la, sprev, dyg, gn_s, t, name, comm=None):
    S = z.shape[0]
    nc = t // CH
    nt = S // t

    def body(q_ref, k_ref, v_ref, r_ref, la_ref, sp_ref, dy_ref, gn_ref,
             dq_ref, dk_ref, dv_ref, dr_ref, dpre_ref, redg_ref, redb_ref, gs):
        @pl.when(pl.program_id(0) == 0)
        def _():
            gs[...] = jnp.zeros_like(gs)
            redg_ref[...] = jnp.zeros_like(redg_ref)
            redb_ref[...] = jnp.zeros_like(redb_ref)

        consts = _gla_consts()
        umat = (lax.broadcasted_iota(jnp.int32, (CH, CH), 0) <= lax.broadcasted_iota(jnp.int32, (CH, CH), 1)).astype(f32)
        last_row = lax.broadcasted_iota(jnp.int32, (CH, DQK), 0) == CH - 1
        for c in reversed(range(nc)):
            _gla_bwd_chunk(c, consts, umat, last_row, q_ref, k_ref, v_ref, r_ref, la_ref, sp_ref, dy_ref, gn_ref,
                           dq_ref, dk_ref, dv_ref, dr_ref, dpre_ref, redg_ref, redb_ref, gs)

    rev = lambda col: (lambda i: (nt - 1 - i, col))
    return _pcall(
        body, (z, z, z, z, la, sprev, dyg, gn_s), name=name, comm=comm,
        grid=(nt,),
        in_specs=[pl.BlockSpec((t, DQK), rev(4)), pl.BlockSpec((t, DQK), rev(5)),
                  pl.BlockSpec((t, DG), rev(3)), pl.BlockSpec((t, DG), rev(4)),
                  pl.BlockSpec((t, DQK), rev(0)), pl.BlockSpec((nc * DQK, DV), rev(0)),
                  pl.BlockSpec((t, DG), rev(0)), _full(gn_s.shape)],
        out_specs=[pl.BlockSpec((t, DQK), rev(0)), pl.BlockSpec((t, DQK), rev(0)),
                   pl.BlockSpec((t, DG), rev(0)), pl.BlockSpec((t, DG), rev(0)), pl.BlockSpec((t, DQK), rev(0)),
                   _full((DQK, DV)), _full((CH, DQK))],
        out_shape=[jax.ShapeDtypeStruct((S, DQK), bf16), jax.ShapeDtypeStruct((S, DQK), bf16),
                   jax.ShapeDtypeStruct((S, DG), bf16), jax.ShapeDtypeStruct((S, DG), bf16), jax.ShapeDtypeStruct((S, DQK), bf16),
                   jax.ShapeDtypeStruct((DQK, DV), f32), jax.ShapeDtypeStruct((CH, DQK), f32)],
        scratch_shapes=[pltpu.VMEM((DQK, DV), f32)],
    )


def mixout_fwd(x1, yc, yg, mod, wout, tm, name):
    S = x1.shape[0]

    def body(x_ref, yc_ref, yg_ref, mod_ref, w_ref, xo_ref):
        mixo = _dot(yc_ref[...], w_ref[0:DC, :]) + _dot(yg_ref[...], w_ref[DC:DC + DG, :])
        xo_ref[...] = x_ref[...] + mod_ref[5:6, :] * mixo

    row = lambda i: (i, 0)
    return pl.pallas_call(
        body, name=name,
        grid=(S // tm,),
        in_specs=[pl.BlockSpec((tm, D), row), pl.BlockSpec((tm, DC), row), pl.BlockSpec((tm, DG), row),
                  _full(mod.shape), _full(wout.shape)],
        out_specs=pl.BlockSpec((tm, D), row),
        out_shape=jax.ShapeDtypeStruct((S, D), f32),
        compiler_params=_cp(1),
    )(x1, yc, yg, mod, wout)


def mixout_bwd(dx2, yc, yg, mod, wout, tm, name):
    S = dx2.shape[0]

    def body(dx_ref, yc_ref, yg_ref, mod_ref, w_ref, dm_ref, dyc_ref, dyg_ref, red_ref):
        @pl.when(pl.program_id(0) == 0)
        def _():
            red_ref[...] = jnp.zeros_like(red_ref)

        dxv = dx_ref[...]
        mixo = _dot(yc_ref[...], w_ref[0:DC, :]) + _dot(yg_ref[...], w_ref[DC:DC + DG, :])
        red_ref[0:1, :] += _rowsum(dxv * mixo)
        dm = (mod_ref[5:6, :] * dxv).astype(bf16)
        dm_ref[...] = dm
        dycat = _dg(dm, w_ref[...], NT)
        dyc_ref[...] = dycat[:, :DC]
        dyg_ref[...] = dycat[:, DC:]

    row = lambda i: (i, 0)
    return pl.pallas_call(
        body, name=name,
        grid=(S // tm,),
        in_specs=[pl.BlockSpec((tm, D), row), pl.BlockSpec((tm, DC), row), pl.BlockSpec((tm, DG), row),
                  _full(mod.shape), _full(wout.shape)],
        out_specs=[pl.BlockSpec((tm, D), row), pl.BlockSpec((tm, DC), row), pl.BlockSpec((tm, DG), row), _full((8, D))],
        out_shape=[jax.ShapeDtypeStruct((S, D), bf16), jax.ShapeDtypeStruct((S, DC), f32),
                   jax.ShapeDtypeStruct((S, DG), f32), jax.ShapeDtypeStruct((8, D), f32)],
        compiler_params=_cp(1),
    )(dx2, yc, yg, mod, wout)


def final_fwd_bwd(x, tgt, fmod, g, tm, name):
    S = x.shape[0]

    def body(x_ref, t_ref, fm_ref, g_ref, dx_ref, red_ref):
        @pl.when(pl.program_id(0) == 0)
        def _():
            red_ref[...] = jnp.zeros_like(red_ref)

        xh, rstd = _rms_parts(x_ref[...])
        gv = g_ref[...]
        n = xh * gv
        sc = 1.0 + fm_ref[1:2, :]
        e = n * sc + fm_ref[0:1, :] - t_ref[...]
        red_ref[0:1, :] += _rowsum(e * e) * (0.5 / D)
        dy = e * (1.0 / D)
        dn = dy * sc
        red_ref[1:2, :] += _rowsum(dy)
        red_ref[2:3, :] += _rowsum(dy * n)
        red_ref[3:4, :] += _rowsum(dn * xh)
        dx_ref[...] = _rms_bwd(dn * gv, xh, rstd)

    row = lambda i: (i, 0)
    return pl.pallas_call(
        body, name=name,
        grid=(S // tm,),
        in_specs=[pl.BlockSpec((tm, D), row), pl.BlockSpec((tm, D), row), _full(fmod.shape), _full(g.shape)],
        out_specs=[pl.BlockSpec((tm, D), row), _full((8, D))],
        out_shape=[jax.ShapeDtypeStruct((S, D), f32), jax.ShapeDtypeStruct((8, D), f32)],
        compiler_params=_cp(1),
    )(x, tgt, fmod, g)


def ada_fwd(c_all, w, b, name):
    n = w.shape[1]

    def body(c_ref, w_ref, b_ref, o_ref):
        cv = c_ref[...]
        o_ref[...] = jnp.dot(cv * _sigmoid(cv), w_ref[...], preferred_element_type=f32, precision=HIGHEST) + b_ref[...]

    return pl.pallas_call(
        body, name=name,
        in_specs=[_full(c_all.shape), _full(w.shape), _full(b.shape)],
        out_specs=_full((N_DEV, n)),
        out_shape=jax.ShapeDtypeStruct((N_DEV, n), f32),
        grid=(1,),
        compiler_params=_cp(1),
    )(c_all, w, b)


def ada_wgrad(c_all_t, dm, name):
    n = dm.shape[1]

    def body(c_ref, d_ref, o_ref):
        cv = c_ref[...]
        o_ref[...] = jnp.dot(cv * _sigmoid(cv), d_ref[...], preferred_element_type=f32, precision=HIGHEST)

    return pl.pallas_call(
        body, name=name,
        in_specs=[_full(c_all_t.shape), _full(dm.shape)],
        out_specs=_full((D, n)),
        out_shape=jax.ShapeDtypeStruct((D, n), f32),
        grid=(1,),
        compiler_params=_cp(1),
    )(c_all_t, dm)


def _adam_math(gv, wv, mv, vv):
    m = ADAM_B1 * mv + (1.0 - ADAM_B1) * gv
    v = ADAM_B2 * vv + (1.0 - ADAM_B2) * (gv * gv)
    m_hat = m / (1.0 - ADAM_B1 ** ADAM_STEP)
    v_hat = v / (1.0 - ADAM_B2 ** ADAM_STEP)
    delta = -ADAM_LR * (m_hat / (jnp.sqrt(v_hat) + ADAM_EPS) + ADAM_WD * wv)
    return delta, m, v


def adam_parts(parts, w, m, v, tr, name, comm=None):
    L, R, C = w.shape
    nt = R // tr

    def body(*refs):
        p_refs = refs[:L]
        w_ref, m_ref, v_ref, g_ref, d_ref, mo_ref, vo_ref = refs[L:]
        lyr = pl.program_id(0)
        for l in range(L):
            @pl.when(lyr == l)
            def _(p_ref=p_refs[l]):
                gv = p_ref[0].astype(f32)
                for k in range(1, N_DEV):
                    gv = gv + p_ref[k].astype(f32)
                g_ref[...] = gv
                d_ref[...], mo_ref[...], vo_ref[...] = _adam_math(gv, w_ref[...], m_ref[...], v_ref[...])

    def part_spec(l):
        return pl.BlockSpec((N_DEV, tr, C), lambda lyr, i: (0, jnp.where(lyr == l, i, jnp.where(lyr < l, 0, nt - 1)), 0))

    spec = pl.BlockSpec((None, tr, C), lambda lyr, i: (lyr, i, 0))
    shp = jax.ShapeDtypeStruct((L, R, C), f32)
    return _pcall(
        body, (*parts, w, m, v), name=name, comm=comm,
        grid=(L, nt),
        in_specs=[part_spec(l) for l in range(L)] + [spec, spec, spec],
        out_specs=[spec, spec, spec, spec],
        out_shape=[shp, shp, shp, shp],
    )


def adam_plain(gr, w, m, v, tr, name):
    R, C = w.shape

    def body(g_ref, w_ref, m_ref, v_ref, d_ref, mo_ref, vo_ref):
        d_ref[...], mo_ref[...], vo_ref[...] = _adam_math(g_ref[...], w_ref[...], m_ref[...], v_ref[...])

    spec = pl.BlockSpec((tr, C), lambda i: (i, 0))
    shp = jax.ShapeDtypeStruct((R, C), f32)
    return pl.pallas_call(
        body, name=name,
        grid=(R // tr,),
        in_specs=[spec, spec, spec, spec],
        out_specs=[spec, spec, spec],
        out_shape=[shp, shp, shp],
        compiler_params=_cp(1),
    )(gr, w, m, v)


def sum8(parts, name):
    _, R, C = parts.shape

    def body(p_ref, o_ref):
        acc = p_ref[0]
        for k in range(1, N_DEV):
            acc = acc + p_ref[k]
        o_ref[...] = acc

    return pl.pallas_call(
        body, name=name,
        grid=(1,),
        in_specs=[_full(parts.shape)],
        out_specs=_full((R, C)),
        out_shape=jax.ShapeDtypeStruct((R, C), f32),
        compiler_params=_cp(1),
    )(parts)


def _place():
    return lax.axis_index("x"), lax.axis_index("y"), lax.axis_index("c")


def _gather_steps(ins, outs, send_sems, recv_sems, local_sems, place):
    n = len(ins)
    x, y, c = place
    me, sibling = (x, y, c), (x, y, 1 - c)
    chips = [(1 - x, y), (x, 1 - y), (1 - x, 1 - y)]

    def slot(a, p):
        return outs[a].at[4 * p[0] + 2 * p[1] + p[2]]

    def copy(a, k, block, to, src=None):
        return pltpu.make_async_remote_copy(
            src_ref=slot(a, block) if src is None else src, dst_ref=slot(a, block),
            send_sem=send_sems.at[a * 7 + k], recv_sem=recv_sems.at[a * 7 + k],
            device_id=to, device_id_type=MESH)

    def mine():
        return [pltpu.make_async_copy(ins[a], slot(a, me), local_sems.at[a]) for a in range(n)]

    def first():
        cps = []
        for a in range(n):
            cps.append(copy(a, 0, me, sibling, src=ins[a]))
            cps += [copy(a, 1 + j, me, (*chip, c), src=ins[a]) for j, chip in enumerate(chips)]
        return cps

    def start():
        for cp in mine() + first():
            cp.start()

    def forward():
        for j, chip in enumerate(chips):
            for a in range(n):
                copy(a, 1 + j, (*chip, c), me).wait_recv()
                copy(a, 4 + j, (*chip, c), sibling).start()

    def finish():
        for a in range(n):
            copy(a, 0, sibling, me).wait_recv()
            for j, chip in enumerate(chips):
                copy(a, 4 + j, (*chip, 1 - c), me).wait_recv()
        for cp in first() + [copy(a, 4 + j, (*chip, c), sibling) for j, chip in enumerate(chips) for a in range(n)]:
            cp.wait_send()
        for cp in mine():
            cp.wait()

    return start, forward, finish


def _exchange_steps(ins, outs, send_sems, recv_sems, local_sems, place):
    n = len(ins)
    x, y, c = place
    me_i = 4 * x + 2 * y + c

    def mine():
        return [pltpu.make_async_copy(ins[a].at[me_i], outs[a].at[me_i], local_sems.at[a]) for a in range(n)]

    def copies(receiving):
        cps = []
        for k in range(1, N_DEV):
            px = 1 - x if (k >> 2) & 1 else x
            py = 1 - y if (k >> 1) & 1 else y
            pc = 1 - c if k & 1 else c
            p_i = 4 * px + 2 * py + pc
            for a in range(n):
                sem = a * 7 + k - 1
                cps.append(pltpu.make_async_remote_copy(
                    src_ref=ins[a].at[p_i], dst_ref=outs[a].at[p_i if receiving else me_i],
                    send_sem=send_sems.at[sem], recv_sem=recv_sems.at[sem],
                    device_id=(px, py, pc), device_id_type=MESH))
        return cps

    def start():
        for cp in mine() + copies(False):
            cp.start()

    def finish():
        for cp in copies(True):
            cp.wait_recv()
        for cp in copies(False):
            cp.wait_send()
        for cp in mine():
            cp.wait()

    return start, None, finish


_COMM_STEPS = {"gather": _gather_steps, "exchange": _exchange_steps}


def _comm_out_shapes(kind, arrs):
    if kind == "gather":
        return [jax.ShapeDtypeStruct((N_DEV,) + a.shape, a.dtype) for a in arrs]
    return [jax.ShapeDtypeStruct(a.shape, a.dtype) for a in arrs]


def _comm_sems(n):
    return [pltpu.SemaphoreType.DMA((7 * n,)), pltpu.SemaphoreType.DMA((7 * n,)), pltpu.SemaphoreType.DMA((n,))]


def _pcall(body, args, *, name, grid, in_specs, out_specs, out_shape, scratch_shapes=(), comm=None):
    in_specs, out_specs, out_shape = list(in_specs), list(out_specs), list(out_shape)
    scratch_shapes = list(scratch_shapes)
    cparams = _cp(len(grid))
    if comm is None:
        outs = pl.pallas_call(body, name=name, grid=grid, in_specs=in_specs, out_specs=out_specs, out_shape=out_shape,
                              scratch_shapes=scratch_shapes, compiler_params=cparams)(*args)
        return list(outs), []
    kind, arrs = comm
    nc, n_in, n_out, n_scr = len(arrs), len(in_specs), len(out_specs), len(scratch_shapes)
    total = 1
    for gdim in grid:
        total *= gdim
    forward_step = (total * 3) // 4

    def hosted(*refs):
        core_in, c_in = refs[:n_in], refs[n_in:n_in + nc]
        core_out = refs[n_in + nc:n_in + nc + n_out]
        c_out = refs[n_in + nc + n_out:n_in + 2 * nc + n_out]
        rest = refs[n_in + 2 * nc + n_out:]
        step = pl.program_id(0)
        for ax in range(1, len(grid)):
            step = step * grid[ax] + pl.program_id(ax)
        start, forward, finish = _COMM_STEPS[kind](c_in, c_out, *rest[n_scr:], _place())
        pl.when(step == 0)(start)
        if forward is not None:
            pl.when(step == forward_step)(forward)
        body(*core_in, *core_out, *rest[:n_scr])
        pl.when(step == total - 1)(finish)

    any_spec = pl.BlockSpec(memory_space=pl.ANY)
    outs = pl.pallas_call(
        hosted, name=name, grid=grid,
        in_specs=in_specs + [any_spec] * nc,
        out_specs=out_specs + [any_spec] * nc,
        out_shape=out_shape + _comm_out_shapes(kind, arrs),
        scratch_shapes=scratch_shapes + _comm_sems(nc),
        compiler_params=cparams)(*args, *arrs)
    return list(outs[:n_out]), list(outs[n_out:])


def _comm_call(kind, arrs, name):
    n = len(arrs)

    def body(*refs):
        start, forward, finish = _COMM_STEPS[kind](refs[:n], refs[n:2 * n], *refs[2 * n:], _place())
        start()
        if forward is not None:
            forward()
        finish()

    any_spec = pl.BlockSpec(memory_space=pl.ANY)
    return pl.pallas_call(
        body, name=name,
        in_specs=[any_spec] * n, out_specs=[any_spec] * n,
        out_shape=_comm_out_shapes(kind, arrs), scratch_shapes=_comm_sems(n),
    )(*arrs)


def all_gather(arrs, name):
    return _comm_call("gather", arrs, name)


def all_to_all(arrs, name):
    return _comm_call("exchange", arrs, name)


def _tiles(S):
    t = min(512, S)
    return dict(ffn=min(256, S), row=t, conv=t, gla=t, bk=min(1024, S))


BIG = ("wi1", "wo1", "win", "wout", "wi2", "wo2")


def _col_shards_to_full(gathered):
    n, r, c = gathered.shape
    return jnp.transpose(gathered, (1, 0, 2)).reshape(r, n * c)


def _win_full(win_a):
    return _pad_rows(win_a.reshape(DIN, D), DINP)


def train_pass(x, tgt, mods, fmod, sh, ws, wi1_first, wo1_first):
    S = x.shape[0]
    T = _tiles(S)
    bk = T["bk"]
    full = [dict() for _ in range(DEPTH)]
    full[0]["wi1"], full[0]["wo1"] = wi1_first.reshape(2 * F, D), wo1_first.reshape(F, D)
    saved = []
    xc = x
    for l in range(DEPTH):
        w, fw = ws[f"L{l}"], full[l]
        x0 = xc
        names = ("win", "wout", "wi2", "wo2") if l == 0 else ("wi2", "wo2")
        (x1, h1f, z1, f1), got = ffn_fwd(x0, mods[l], w["g1"], fw["wi1"], fw["wo1"], (0, 1, 2), T["ffn"], f"ffn1_fwd_{l}",
                                         comm=("gather", [sh[n][l] for n in names]))
        fw.update(zip(names, got))
        if l == 0:
            fw["win"], fw["wout"] = _win_full(fw["win"]), fw["wout"].reshape(D, D)
        fw["wi2"], fw["wo2"] = fw["wi2"].reshape(2 * F, D), fw["wo2"].reshape(F, D)
        z, la = mixin_fwd(x1, mods[l], w["g2"], fw["win"], w["wgu"], w["bgate"], T["row"], f"mixin_fwd_{l}")
        y, yc, yg, sprev = mixer_core_fwd(z, la, w["gn_s"], w["wdw"], w["cpar"], T["gla"], f"mixer_core_fwd_{l}")
        x2 = mixout_fwd(x1, yc, yg, mods[l], fw["wout"], T["row"], f"mixout_fwd_{l}")
        names = ("wi1", "wo1", "win", "wout") if l + 1 < DEPTH else ()
        (x3, h2f, z2, f2), got = ffn_fwd(x2, mods[l], w["g3"], fw["wi2"], fw["wo2"], (6, 7, 8), T["ffn"], f"ffn2_fwd_{l}",
                                         comm=("gather", [sh[n][l + 1] for n in names]) if names else None)
        if names:
            nx = full[l + 1]
            nx["wi1"], nx["wo1"] = got[0].reshape(2 * F, D), got[1].reshape(F, D)
            nx["win"], nx["wout"] = _win_full(got[2]), got[3].reshape(D, D)
        saved.append(dict(x0=x0, x1=x1, x2=x2, h1f=h1f, z1=z1, f1=f1, h2f=h2f, z2=z2, f2=f2,
                          z=z, la=la, y=y, yc=yc, yg=yg, sprev=sprev))
        xc = x3

    dx, redf = final_fwd_bwd(xc, tgt, fmod, ws["gf"], T["row"], "loss_head")
    loss_lanes = redf[0]
    dfmod = redf[1:3]
    grads = {"gf": redf[3]}
    dmods = [None] * DEPTH
    recv = {n: [None] * DEPTH for n in BIG}

    def ffn_backward(xin, dy, h, z, fo, gain, wi_t, wo, rows, l, tag, ride=None):
        (dz, p_wo), got_ride = ffn_bwd_hidden(dy, z, mods[l], wo.T, rows[2], T["ffn"], f"{tag}_bwd_hidden_{l}",
                                              comm=("exchange", ride) if ride else None)
        p_wi, (r_wo,) = dwi_pieces(h, dz, T["bk"], f"d{tag}_wi_{l}",
                                   comm=("exchange", [p_wo.reshape(N_DEV, F // N_DEV, D)]))
        (dxin, red), (r_wi,) = ffn_bwd_input(xin, dy, dz, fo, mods[l], gain, wi_t, rows, T["ffn"],
                                             f"{tag}_bwd_input_{l}", comm=("exchange", [p_wi]))
        return dxin, red, r_wi, r_wo, got_ride

    for l in reversed(range(DEPTH)):
        w, fw, sv = ws[f"L{l}"], full[l], saved[l]
        g = {}
        dx2, red3, recv["wi2"][l], recv["wo2"][l], _ = ffn_backward(
            sv["x2"], dx, sv["h2f"], sv["z2"], sv["f2"], w["g3"], fw["wi2"], fw["wo2"], (6, 7, 8), l, "ffn2")
        dmix, dyc, dyg, red_o = mixout_bwd(dx2, sv["yc"], sv["yg"], mods[l], fw["wout"], T["row"], f"mixout_bwd_{l}")
        p_wout = jnp.concatenate([matmul_tn(sv["yc"], dmix, DC, D, DC, D, bk, f"dwout_c_{l}", out_dtype=bf16),
                                  matmul_tn(sv["yg"], dmix, DG, D, DG, D, bk, f"dwout_g_{l}", out_dtype=bf16)], axis=0)
        (dq, dk, dv, dr, dpre, redg, redb), (recv["wout"][l],) = gla_bwd(
            sv["z"], sv["la"], sv["sprev"], dyg, w["gn_s"], T["gla"], f"gla_bwd_{l}",
            comm=("exchange", [p_wout.reshape(N_DEV, D // N_DEV, D)]))
        dzab, redc = conv_bwd(sv["z"], sv["y"], dyc, w["wdw"], w["cpar"], T["conv"], f"conv_bwd_{l}")
        (dx1, h2, dz, red2), _ = mixin_bwd(sv["x1"], dx2, dzab, dq, dk, dv, dr, dpre, mods[l], w["g2"], fw["win"], w["wgu"],
                                           T["row"], f"mixin_bwd_{l}")
        dwin_t = matmul_tn(dz, h2, DINP, D, DINP, D, bk, f"dwin_{l}", out_dtype=bf16)
        p_win = dwin_t[:DIN].reshape(N_DEV, DIN // N_DEV, D)
        g["wgu"] = matmul_tn(sv["z"], dpre, 128, DQK, 128, DQK, bk, f"dwgu_{l}", a_col_block=(DINP - 128) // 128)[:GR]
        g["bgate"] = jnp.sum(redb, axis=0)
        g["gn"] = jnp.sum(redg.reshape(NH, CH, DV), axis=1)
        g["wdw"] = redc[:CW]
        g["bdw"], g["gln"], g["bln"] = redc[32], redc[33], redc[34]
        dx0, red1, recv["wi1"][l], recv["wo1"][l], (recv["win"][l],) = ffn_backward(
            sv["x0"], dx1, sv["h1f"], sv["z1"], sv["f1"], w["g1"], fw["wi1"], fw["wo1"], (0, 1, 2), l, "ffn1",
            ride=[p_win])
        g["g1"], g["g2"], g["g3"] = red1[3], red2[2], red3[3]
        dmods[l] = jnp.stack([red1[0], red1[1], red1[2], red2[0], red2[1], red_o[0], red3[0], red3[1], red3[2]], axis=0)
        grads[f"L{l}"] = g
        dx = dx0
    return loss_lanes, dx, grads, dmods, dfmod, recv


def _pad_rows(a, rows):
    return jnp.pad(a, ((0, rows - a.shape[0]), (0, 0)))


def kernel(x, c, w_ada, b_ada, g_norm_ffn1, w_ffn1_in, w_ffn1_out, g_norm_mix, w_in, w_dw, b_dw, g_conv_ln, b_conv_ln, w_gate_up, b_gate, g_gla_norm, w_out, g_norm_ffn2, w_ffn2_in, w_ffn2_out, g_norm_final, w_ada_final, b_ada_final, loss_target, m_w_ada, m_b_ada, m_g_norm_ffn1, m_w_ffn1_in, m_w_ffn1_out, m_g_norm_mix, m_w_in, m_w_dw, m_b_dw, m_g_conv_ln, m_b_conv_ln, m_w_gate_up, m_b_gate, m_g_gla_norm, m_w_out, m_g_norm_ffn2, m_w_ffn2_in, m_w_ffn2_out, m_g_norm_final, m_w_ada_final, m_b_ada_final, v_w_ada, v_b_ada, v_g_norm_ffn1, v_w_ffn1_in, v_w_ffn1_out, v_g_norm_mix, v_w_in, v_w_dw, v_b_dw, v_g_conv_ln, v_b_conv_ln, v_w_gate_up, v_b_gate, v_g_gla_norm, v_w_out, v_g_norm_ffn2, v_w_ffn2_in, v_w_ffn2_out, v_g_norm_final, v_w_ada_final, v_b_ada_final):
    me = 4 * lax.axis_index("x") + 2 * lax.axis_index("y") + lax.axis_index("c")
    L = DEPTH
    n_ada = N_MOD * D // N_DEV
    n_fin = 2 * D // N_DEV

    small = jnp.concatenate([c.reshape(-1), w_dw.reshape(-1), w_gate_up.reshape(-1)])
    n_small = small.shape[0]
    small = jnp.pad(small, (0, 8 * D - n_small)).reshape(8, D)
    big = dict(wi1=w_ffn1_in, wo1=w_ffn1_out, win=w_in, wout=w_out, wi2=w_ffn2_in, wo2=w_ffn2_out)
    transposed = ("wi1", "wi2", "win")
    sh = {n: [(a[l].T if n in transposed else a[l]).astype(bf16) for l in range(L)] for n, a in big.items()}
    small_a, wi1_first, wo1_first = all_gather([small, sh["wi1"][0], sh["wo1"][0]], "gather_first")
    small_a = small_a.reshape(N_DEV, 8 * D)
    c_all = small_a[:, :D]
    o1 = D + L * CW * (DC // N_DEV)
    wdw_full = _col_shards_to_full(small_a[:, D:o1].reshape(N_DEV, L * CW, DC // N_DEV)).reshape(L, CW, DC)
    wgu_full = _col_shards_to_full(small_a[:, o1:o1 + L * GR * (DQK // N_DEV)].reshape(N_DEV, L * GR, DQK // N_DEV)).reshape(L, GR, DQK)

    b_ada_mine = lax.dynamic_slice(b_ada, (0, me * n_ada), (L, n_ada))
    b_fin_mine = lax.dynamic_slice(b_ada_final, (me * n_fin,), (n_fin,))
    parts = [ada_fwd(c_all, w_ada[l], b_ada_mine[l:l + 1], f"ada_fwd_{l}") for l in range(L)]
    parts.append(ada_fwd(c_all, w_ada_final, b_fin_mine.reshape(1, n_fin), "ada_fwd_final"))
    modsrc = jnp.concatenate(parts, axis=1)
    n_row = modsrc.shape[1]
    modsrc = jnp.pad(modsrc, ((0, 0), (0, 24 * 128 - n_row))).reshape(N_DEV, 24, 128)
    (modrecv,) = all_to_all([modsrc], "exchange_mod")
    modrecv = modrecv.reshape(N_DEV, 24 * 128)
    mods = []
    for l in range(L):
        mvec = modrecv[:, l * n_ada:(l + 1) * n_ada].reshape(N_MOD, D)
        mods.append(_pad_rows(mvec, 16))
    fmod = _pad_rows(modrecv[:, L * n_ada:L * n_ada + n_fin].reshape(2, D), 8)

    ws = {"gf": g_norm_final.reshape(1, D)}
    for l in range(L):
        ws[f"L{l}"] = dict(
            g1=g_norm_ffn1[l].reshape(1, D), g2=g_norm_mix[l].reshape(1, D), g3=g_norm_ffn2[l].reshape(1, D),
            wgu=_pad_rows(wgu_full[l], 128).astype(bf16),
            bgate=b_gate[l].reshape(1, DQK),
            wdw=_pad_rows(wdw_full[l], 32),
            cpar=_pad_rows(jnp.stack([b_dw[l], g_conv_ln[l], b_conv_ln[l]]), 8),
            gn_s=jnp.repeat(g_gla_norm[l], CH, axis=0),
        )

    loss_lanes, grad_x, gr, dmods, dfmod, recv = train_pass(
        x[0], loss_target[0], mods, fmod, sh, ws, wi1_first, wo1_first)

    def adam_big(rv, w, m, v, name, is_transposed=False):
        if is_transposed:
            w, m, v = (jnp.swapaxes(a, 1, 2) for a in (w, m, v))
        R = w.shape[1]
        tr = 256 if R % 256 == 0 else (R // 2 if (R // 2) % 16 == 0 else R)
        outs, _ = adam_parts(rv, w, m, v, tr, name)
        return [jnp.swapaxes(o, 1, 2) for o in outs] if is_transposed else outs

    res = {}
    res["w_ffn2_in"] = adam_big(recv["wi2"], w_ffn2_in, m_w_ffn2_in, v_w_ffn2_in, "adam_ffn2_in", True)
    res["w_ffn2_out"] = adam_big(recv["wo2"], w_ffn2_out, m_w_ffn2_out, v_w_ffn2_out, "adam_ffn2_out")
    res["w_in"] = adam_big(recv["win"], w_in, m_w_in, v_w_in, "adam_w_in", True)
    res["w_out"] = adam_big(recv["wout"], w_out, m_w_out, v_w_out, "adam_w_out")
    res["w_ffn1_out"] = adam_big(recv["wo1"], w_ffn1_out, m_w_ffn1_out, v_w_ffn1_out, "adam_ffn1_out")
    res["w_ffn1_in"] = adam_big(recv["wi1"], w_ffn1_in, m_w_ffn1_in, v_w_ffn1_in, "adam_ffn1_in", True)

    flat = lambda name: jnp.stack([gr[f"L{l}"][name] for l in range(L)]).reshape(-1)
    sections = [
        ("b_ada", jnp.stack(dmods).reshape(-1)), ("b_ada_final", dfmod.reshape(-1)),
        ("g_norm_ffn1", flat("g1")), ("g_norm_mix", flat("g2")), ("g_norm_ffn2", flat("g3")), ("g_norm_final", gr["gf"]),
        ("b_dw", flat("bdw")), ("g_conv_ln", flat("gln")), ("b_conv_ln", flat("bln")), ("b_gate", flat("bgate")),
        ("g_gla_norm", flat("gn")),
    ]
    n_rep = sum(s[1].shape[0] for s in sections)
    rep_rows = -(-n_rep // D)
    extra = [("loss", loss_lanes), ("w_dw", flat("wdw")), ("w_gate_up", flat("wgu"))]
    pack = jnp.concatenate([s[1] for s in sections] + [jnp.zeros((rep_rows * D - n_rep,), f32)] + [s[1] for s in extra])
    n_pack = pack.shape[0]
    pack_rows = -(-n_pack // (8 * D)) * 8
    pack = jnp.pad(pack, (0, pack_rows * D - n_pack)).reshape(pack_rows, D)
    (pack_all,) = all_gather([pack], "gather_small_grads")
    tot = sum8(pack_all, "sum_small_grads")
    tot_flat = tot.reshape(-1)
    loss = jnp.sum(tot_flat[rep_rows * D:rep_rows * D + D])
    o_dw = rep_rows * D + D
    g_wdw_full = tot_flat[o_dw:o_dw + L * CW * DC].reshape(L, CW, DC)
    o_gu = o_dw + L * CW * DC
    g_wgu_full = tot_flat[o_gu:o_gu + L * GR * DQK].reshape(L, GR, DQK)

    small_params = dict(b_ada=(b_ada, m_b_ada, v_b_ada), b_ada_final=(b_ada_final, m_b_ada_final, v_b_ada_final),
                        g_norm_ffn1=(g_norm_ffn1, m_g_norm_ffn1, v_g_norm_ffn1), g_norm_mix=(g_norm_mix, m_g_norm_mix, v_g_norm_mix),
                        g_norm_ffn2=(g_norm_ffn2, m_g_norm_ffn2, v_g_norm_ffn2), g_norm_final=(g_norm_final, m_g_norm_final, v_g_norm_final),
                        b_dw=(b_dw, m_b_dw, v_b_dw), g_conv_ln=(g_conv_ln, m_g_conv_ln, v_g_conv_ln),
                        b_conv_ln=(b_conv_ln, m_b_conv_ln, v_b_conv_ln), b_gate=(b_gate, m_b_gate, v_b_gate),
                        g_gla_norm=(g_gla_norm, m_g_gla_norm, v_g_gla_norm))

    def rep_pack(idx):
        p = jnp.concatenate([small_params[s[0]][idx].reshape(-1) for s in sections])
        return jnp.pad(p, (0, rep_rows * D - n_rep)).reshape(rep_rows, D)

    g_rep = tot[:rep_rows]
    d_rep, m_rep, v_rep = adam_plain(g_rep, rep_pack(0), rep_pack(1), rep_pack(2), rep_rows, "adam_small")
    off = 0
    for sname, sval in sections:
        shp = small_params[sname][0].shape
        nel = sval.shape[0]
        res[sname] = [a.reshape(-1)[off:off + nel].reshape(shp) for a in (g_rep, d_rep, m_rep, v_rep)]
        off += nel

    def adam_cols(g_full, w, m, v, name):
        shp = w.shape
        g_mine = lax.dynamic_slice(g_full, (0, 0, me * shp[2]), shp)
        R, C = shp[0] * shp[1], shp[2]
        outs = adam_plain(g_mine.reshape(R, C), w.reshape(R, C), m.reshape(R, C), v.reshape(R, C), R, name)
        return [g_mine] + [o.reshape(shp) for o in outs]

    res["w_dw"] = adam_cols(g_wdw_full, w_dw, m_w_dw, v_w_dw, "adam_w_dw")
    res["w_gate_up"] = adam_cols(g_wgu_full, w_gate_up, m_w_gate_up, v_w_gate_up, "adam_w_gate_up")

    c_all_t = c_all.T
    dmod_all = pack_all.reshape(N_DEV, -1)[:, :L * N_MOD * D].reshape(N_DEV, L, N_MOD * D)
    dfm_all = pack_all.reshape(N_DEV, -1)[:, L * N_MOD * D:L * N_MOD * D + 2 * D]
    dm_mine = lax.dynamic_slice(dmod_all, (0, 0, me * n_ada), (N_DEV, L, n_ada))
    dfm_mine = lax.dynamic_slice(dfm_all, (0, me * n_fin), (N_DEV, n_fin))
    g_w_ada = jnp.stack([ada_wgrad(c_all_t, dm_mine[:, l], f"ada_wgrad_{l}") for l in range(L)])
    g_w_fin = ada_wgrad(c_all_t, dfm_mine, "ada_wgrad_final")
    outs = adam_plain(g_w_ada.reshape(L * D, n_ada), w_ada.reshape(L * D, n_ada), m_w_ada.reshape(L * D, n_ada),
                      v_w_ada.reshape(L * D, n_ada), 256, "adam_w_ada")
    res["w_ada"] = [g_w_ada] + [o.reshape(w_ada.shape) for o in outs]
    res["w_ada_final"] = [g_w_fin] + list(adam_plain(g_w_fin, w_ada_final, m_w_ada_final, v_w_ada_final, 256, "adam_w_ada_final"))

    order = ["w_ada", "b_ada", "g_norm_ffn1", "w_ffn1_in", "w_ffn1_out", "g_norm_mix", "w_in", "w_dw", "b_dw", "g_conv_ln",
             "b_conv_ln", "w_gate_up", "b_gate", "g_gla_norm", "w_out", "g_norm_ffn2", "w_ffn2_in", "w_ffn2_out",
             "g_norm_final", "w_ada_final", "b_ada_final"]
    out = [loss, grad_x[None]]
    for k in range(4):
        out += [res[name][k] for name in order]
    return tuple(out)
```

```python
import functools

import jax
import jax.numpy as jnp
from jax import lax
from jax.experimental import pallas as pl
from jax.experimental.pallas import tpu as pltpu

f32 = jnp.float32
bf16 = jnp.bfloat16

N_DEV = 8
DEPTH = 2
D = 1024
F = 2816
DC = 512
NH = 4
DK = 64
DV = 128
DQK = NH * DK
DG = NH * DV
CH = 64
CW = 31
GR = 16
TAU = 16.0
N_MOD = 9
DIN = 2 * DC + 2 * DQK + 2 * DG + GR
DINP = 2688
EPS = 1e-6
HALO = 32
SUBLANES = 8
CONV_ROWS = 32
FS = 2 * F // N_DEV

ADAM_LR = 0.001
ADAM_B1 = 0.9
ADAM_B2 = 0.999
ADAM_EPS = 1e-08
ADAM_WD = 0.01
ADAM_STEP = 10

V7X_VMEM_LIMIT = 56 * 1024 * 1024
MESH = pl.DeviceIdType.MESH
HIGHEST = lax.Precision.HIGHEST

NT = (((1,), (1,)), ((), ()))
TN = (((0,), (0,)), ((), ()))


def _cp(n_axes):
    return pltpu.CompilerParams(dimension_semantics=("arbitrary",) * n_axes, vmem_limit_bytes=V7X_VMEM_LIMIT)


def _full(shape):
    nd = len(shape)
    return pl.BlockSpec(shape, lambda *_: (0,) * nd)


def _resident(shape):
    nd = len(shape)
    return pl.BlockSpec(shape, lambda *_: (0,) * nd, pipeline_mode=pl.Buffered(1))


def _dot(a, b):
    return jnp.dot(a, b, preferred_element_type=f32)


def _dg(a, b, dims):
    return lax.dot_general(a, b, dims, preferred_element_type=f32)


def _sigmoid(x):
    return jax.nn.sigmoid(x)


def _rowsum(x):
    return jnp.sum(x, axis=0, keepdims=True)


def _rms_parts(xv):
    rstd = lax.rsqrt(jnp.mean(xv * xv, axis=-1, keepdims=True) + EPS)
    return xv * rstd, rstd


def _rms_bwd(dxh, xh, rstd):
    return rstd * (dxh - xh * jnp.mean(dxh * xh, axis=-1, keepdims=True))


def ffn_fwd(x, mod, g, wi_t, wo, rows, tm, name, comm=None):
    S = x.shape[0]
    r_shift, r_scale, r_gate = rows

    def body(x_ref, mod_ref, g_ref, wi_ref, wo_ref, xo_ref, h_ref, z_ref, f_ref):
        xv = x_ref[...]
        xh, _ = _rms_parts(xv)
        h = (xh * g_ref[...] * (1.0 + mod_ref[r_scale:r_scale + 1, :]) + mod_ref[r_shift:r_shift + 1, :]).astype(bf16)
        h_ref[...] = h
        zg = _dg(h, wi_ref[0:F, :], NT)
        zu = _dg(h, wi_ref[F:2 * F, :], NT)
        z_ref[:, 0:F] = zg.astype(bf16)
        z_ref[:, F:2 * F] = zu.astype(bf16)
        fv = _dot((zg * _sigmoid(zg) * zu).astype(bf16), wo_ref[...])
        f_ref[...] = fv.astype(bf16)
        xo_ref[...] = xv + 0.5 * mod_ref[r_gate:r_gate + 1, :] * fv

    row = lambda i: (i, 0)
    return _pcall(
        body, (x, mod, g, wi_t, wo), name=name, comm=comm,
        grid=(S // tm,),
        in_specs=[pl.BlockSpec((tm, D), row), _full(mod.shape), _full(g.shape), _resident(wi_t.shape), _resident(wo.shape)],
        out_specs=[pl.BlockSpec((tm, D), row), pl.BlockSpec((tm, D), row), pl.BlockSpec((tm, 2 * F), row),
                   pl.BlockSpec((tm, D), row)],
        out_shape=[jax.ShapeDtypeStruct((S, D), f32), jax.ShapeDtypeStruct((S, D), bf16),
                   jax.ShapeDtypeStruct((S, 2 * F), bf16), jax.ShapeDtypeStruct((S, D), bf16)],
    )


def ffn_bwd_hidden(dy, z, mod, wo_t, r_gate, tm, name, comm=None):
    S = dy.shape[0]
    nt = S // tm
    halves = 2
    fc = F // halves

    def body(dy_ref, z_ref, mod_ref, wo_ref, dz_ref, dwo_ref, acc_s):
        i = pl.program_id(0)

        @pl.when(i == 0)
        def _():
            acc_s[...] = jnp.zeros_like(acc_s)

        df = (0.5 * mod_ref[r_gate:r_gate + 1, :] * dy_ref[...]).astype(bf16)
        for c in range(halves):
            lo, hi = c * fc, (c + 1) * fc
            zgv = z_ref[:, lo:hi].astype(f32)
            zuv = z_ref[:, F + lo:F + hi].astype(f32)
            s = _sigmoid(zgv)
            sil = zgv * s
            acc_s[lo:hi, :] += _dg((sil * zuv).astype(bf16), df, TN)
            da = _dot(df, wo_ref[:, lo:hi])
            dz_ref[:, F + lo:F + hi] = (da * sil).astype(bf16)
            dz_ref[:, lo:hi] = (da * zuv * (s * (1.0 + zgv * (1.0 - s)))).astype(bf16)

        @pl.when(i == nt - 1)
        def _():
            dwo_ref[...] = acc_s[...].astype(bf16)

    row = lambda i: (i, 0)
    return _pcall(
        body, (dy, z, mod, wo_t), name=name, comm=comm,
        grid=(nt,),
        in_specs=[pl.BlockSpec((tm, D), row), pl.BlockSpec((tm, 2 * F), row), _full(mod.shape), _resident(wo_t.shape)],
        out_specs=[pl.BlockSpec((tm, 2 * F), row), _full((F, D))],
        out_shape=[jax.ShapeDtypeStruct((S, 2 * F), bf16), jax.ShapeDtypeStruct((F, D), bf16)],
        scratch_shapes=[pltpu.VMEM((F, D), f32)],
    )


def ffn_bwd_input(x, dy, dz, fo, mod, g, wi_t, rows, tm, name, comm=None):
    S = x.shape[0]
    r_shift, r_scale, r_gate = rows

    def body(x_ref, dy_ref, dz_ref, f_ref, mod_ref, g_ref, wi_ref, dx_ref, red_ref):
        @pl.when(pl.program_id(0) == 0)
        def _():
            red_ref[...] = jnp.zeros_like(red_ref)

        dh = _dot(dz_ref[...], wi_ref[...])
        dyv = dy_ref[...]
        xh, rstd = _rms_parts(x_ref[...])
        gv = g_ref[...]
        n = xh * gv
        dn = dh * (1.0 + mod_ref[r_scale:r_scale + 1, :])
        red_ref[0:1, :] += _rowsum(dh)
        red_ref[1:2, :] += _rowsum(dh * n)
        red_ref[2:3, :] += _rowsum(0.5 * f_ref[...].astype(f32) * dyv)
        red_ref[3:4, :] += _rowsum(dn * xh)
        dx_ref[...] = dyv + _rms_bwd(dn * gv, xh, rstd)

    row = lambda i: (i, 0)
    return _pcall(
        body, (x, dy, dz, fo, mod, g, wi_t), name=name, comm=comm,
        grid=(S // tm,),
        in_specs=[pl.BlockSpec((tm, D), row), pl.BlockSpec((tm, D), row), pl.BlockSpec((tm, 2 * F), row),
                  pl.BlockSpec((tm, D), row), _full(mod.shape), _full(g.shape), _resident(wi_t.shape)],
        out_specs=[pl.BlockSpec((tm, D), row), _full((8, D))],
        out_shape=[jax.ShapeDtypeStruct((S, D), f32), jax.ShapeDtypeStruct((8, D), f32)],
    )


def matmul_tn(a, b, M, N, bm, bn, bk, name, a_col_block=0, out_dtype=f32):
    S = b.shape[0]
    nk = S // bk

    def body(a_ref, b_ref, o_ref, acc_s):
        k = pl.program_id(2)

        @pl.when(k == 0)
        def _():
            acc_s[...] = jnp.zeros_like(acc_s)

        acc_s[...] += _dg(a_ref[...].astype(bf16), b_ref[...].astype(bf16), TN)

        @pl.when(k == nk - 1)
        def _():
            o_ref[...] = acc_s[...].astype(out_dtype)

    return pl.pallas_call(
        body, name=name,
        grid=(M // bm, N // bn, nk),
        in_specs=[
            pl.BlockSpec((bk, bm), lambda i, j, k: (k, i + a_col_block)),
            pl.BlockSpec((bk, bn), lambda i, j, k: (k, j)),
        ],
        out_specs=pl.BlockSpec((bm, bn), lambda i, j, k: (i, j)),
        out_shape=jax.ShapeDtypeStruct((M, N), out_dtype),
        scratch_shapes=[pltpu.VMEM((bm, bn), f32)],
        compiler_params=_cp(3),
    )(a, b)


def dwi_pieces(h, dz, bk, name, comm=None):
    S = h.shape[0]
    nk = S // bk

    def body(h_ref, dz_ref, o_ref, acc_s):
        k = pl.program_id(1)

        @pl.when(k == 0)
        def _():
            acc_s[...] = jnp.zeros_like(acc_s)

        acc_s[...] += _dg(dz_ref[...], h_ref[...], TN)

        @pl.when(k == nk - 1)
        def _():
            o_ref[...] = acc_s[...].astype(bf16)

    (out,), comm_outs = _pcall(
        body, (h, dz), name=name, comm=comm,
        grid=(2, nk),
        in_specs=[pl.BlockSpec((bk, D), lambda half, k: (k, 0)), pl.BlockSpec((bk, F), lambda half, k: (k, half))],
        out_specs=[pl.BlockSpec((F, D), lambda half, k: (half, 0))],
        out_shape=[jax.ShapeDtypeStruct((2 * F, D), bf16)],
        scratch_shapes=[pltpu.VMEM((F, D), f32)],
    )
    return out.reshape(N_DEV, FS, D), comm_outs


def mixin_fwd(x1, mod, g, win, wgu, bgate, tm, name, comm=None):
    S = x1.shape[0]

    def body(x_ref, mod_ref, g_ref, win_ref, wgu_ref, bg_ref, z_ref, la_ref):
        xh, _ = _rms_parts(x_ref[...])
        hv = xh * g_ref[...] * (1.0 + mod_ref[4:5, :]) + mod_ref[3:4, :]
        z = _dg(hv.astype(bf16), win_ref[...], NT).astype(bf16)
        z_ref[...] = z
        pre = _dot(z[:, DINP - 128:], wgu_ref[...]) + bg_ref[...]
        la_ref[...] = (jnp.minimum(pre, 0.0) - jnp.log(1.0 + jnp.exp(-jnp.abs(pre)))) * (1.0 / TAU)

    return _pcall(
        body, (x1, mod, g, win, wgu, bgate), name=name, comm=comm,
        grid=(S // tm,),
        in_specs=[pl.BlockSpec((tm, D), lambda i: (i, 0)), _full(mod.shape), _full(g.shape),
                  _full(win.shape), _full(wgu.shape), _full(bgate.shape)],
        out_specs=[pl.BlockSpec((tm, DINP), lambda i: (i, 0)), pl.BlockSpec((tm, DQK), lambda i: (i, 0))],
        out_shape=[jax.ShapeDtypeStruct((S, DINP), bf16), jax.ShapeDtypeStruct((S, DQK), f32)],
    )


def mixin_bwd(x1, dres, dzab, dq, dk, dv, dr, dpre, mod, g, win, wgu, tm, name, comm=None):
    S = x1.shape[0]

    def body(x_ref, dres_ref, dzab_ref, dq_ref, dk_ref, dv_ref, dr_ref, dpre_ref, mod_ref, g_ref, win_ref, wgu_ref,
             dx_ref, h_ref, dz_ref, red_ref):
        @pl.when(pl.program_id(0) == 0)
        def _():
            red_ref[...] = jnp.zeros_like(red_ref)

        dglr = _dg(dpre_ref[...], wgu_ref[...], NT).astype(bf16)
        dz = jnp.concatenate([dzab_ref[...], dq_ref[...], dk_ref[...], dv_ref[...], dr_ref[...], dglr], axis=1)
        dz_ref[...] = dz
        dh = _dot(dz, win_ref[...])
        xh, rstd = _rms_parts(x_ref[...])
        gv = g_ref[...]
        n = xh * gv
        sc = 1.0 + mod_ref[4:5, :]
        h_ref[...] = (n * sc + mod_ref[3:4, :]).astype(bf16)
        dn = dh * sc
        red_ref[0:1, :] += _rowsum(dh)
        red_ref[1:2, :] += _rowsum(dh * n)
        red_ref[2:3, :] += _rowsum(dn * xh)
        dx_ref[...] = dres_ref[...] + _rms_bwd(dn * gv, xh, rstd)

    row = lambda i: (i, 0)
    return _pcall(
        body, (x1, dres, dzab, dq, dk, dv, dr, dpre, mod, g, win, wgu), name=name, comm=comm,
        grid=(S // tm,),
        in_specs=[pl.BlockSpec((tm, D), row), pl.BlockSpec((tm, D), row),
                  pl.BlockSpec((tm, 2 * DC), row), pl.BlockSpec((tm, DQK), row), pl.BlockSpec((tm, DQK), row),
                  pl.BlockSpec((tm, DG), row), pl.BlockSpec((tm, DG), row), pl.BlockSpec((tm, DQK), row),
                  _full(mod.shape), _full(g.shape), _full(win.shape), _full(wgu.shape)],
        out_specs=[pl.BlockSpec((tm, D), row), pl.BlockSpec((tm, D), row), pl.BlockSpec((tm, DINP), row), _full((8, D))],
        out_shape=[jax.ShapeDtypeStruct((S, D), f32), jax.ShapeDtypeStruct((S, D), bf16),
                   jax.ShapeDtypeStruct((S, DINP), bf16), jax.ShapeDtypeStruct((8, D), f32)],
    )


def _glu(zab):
    zab = zab.astype(f32)
    return zab[:, :DC] * _sigmoid(zab[:, DC:])


def _shift_copies(src_s, dst_s, tc):
    n = tc + HALO - SUBLANES
    for b in range(1, SUBLANES):
        dst_s[b, 0:n, :] = src_s[b:b + n, :]


def _shifted(src_s, dst_s, o, tc):
    b = o % SUBLANES
    a = o - b
    return src_s[a:a + tc, :] if b == 0 else dst_s[b, a:a + tc, :]


def _conv_fwd_prepare(first, zc_ref, zp_ref, u_s, us_s, tc):
    up = _glu(zp_ref[...])
    u_s[0:HALO, :] = jnp.where(first, 0.0, up)
    u_s[HALO:HALO + tc, :] = _glu(zc_ref[...])
    _shift_copies(u_s, us_s, tc)


def _conv_fwd_rows(r0, n, w_ref, cp_ref, y_ref, yc_ref, u_s, us_s):
    for r in range(r0, r0 + n, CONV_ROWS):
        acc = _shifted(u_s, us_s, HALO - (CW - 1) + r, CONV_ROWS) * w_ref[0:1, :]
        for w in range(1, CW):
            acc = acc + _shifted(u_s, us_s, HALO - (CW - 1) + w + r, CONV_ROWS) * w_ref[w:w + 1, :]
        y = acc + cp_ref[0:1, :]
        y_ref[r:r + CONV_ROWS, :] = y
        yc = y - jnp.mean(y, axis=-1, keepdims=True)
        yl = yc * lax.rsqrt(jnp.mean(yc * yc, axis=-1, keepdims=True) + EPS) * cp_ref[1:2, :] + cp_ref[2:3, :]
        yc_ref[r:r + CONV_ROWS, :] = (yl * _sigmoid(yl)).astype(bf16)


def _conv_bwd_prepare(first, last, zc_ref, zp_ref, y_ref, yn_ref, d_ref, dn_ref, cp_ref, red_ref, u_s, dy_s, us_s, dys_s, tc):
    gl = cp_ref[1:2, :]
    bl = cp_ref[2:3, :]

    def ln_bwd(yv, dv):
        yc = yv - jnp.mean(yv, axis=-1, keepdims=True)
        rstd = lax.rsqrt(jnp.mean(yc * yc, axis=-1, keepdims=True) + EPS)
        yh = yc * rstd
        yl = yh * gl + bl
        s = _sigmoid(yl)
        dyl = dv * (s * (1.0 + yl * (1.0 - s)))
        dyh = dyl * gl
        dyv = rstd * (dyh - jnp.mean(dyh, axis=-1, keepdims=True) - yh * jnp.mean(dyh * yh, axis=-1, keepdims=True))
        return dyv, dyl, yh

    dy_c, dyl_c, yh_c = ln_bwd(y_ref[...], d_ref[...])
    dy_n, _, _ = ln_bwd(yn_ref[...], dn_ref[...])
    dy_s[0:tc, :] = dy_c
    dy_s[tc:tc + HALO, :] = jnp.where(last, 0.0, dy_n)
    u_s[0:HALO, :] = jnp.where(first, 0.0, _glu(zp_ref[...]))
    u_s[HALO:HALO + tc, :] = _glu(zc_ref[...])
    _shift_copies(u_s, us_s, tc)
    _shift_copies(dy_s, dys_s, tc)
    red_ref[32:33, :] += _rowsum(dy_c)
    red_ref[33:34, :] += _rowsum(dyl_c * yh_c)
    red_ref[34:35, :] += _rowsum(dyl_c)


def _conv_bwd_input_rows(r0, n, zc_ref, w_ref, dz_ref, dy_s, dys_s):
    for r in range(r0, r0 + n, CONV_ROWS):
        du = _shifted(dy_s, dys_s, CW - 1 + r, CONV_ROWS) * w_ref[0:1, :]
        for w in range(1, CW):
            du = du + _shifted(dy_s, dys_s, CW - 1 - w + r, CONV_ROWS) * w_ref[w:w + 1, :]
        zc = zc_ref[r:r + CONV_ROWS, :].astype(f32)
        av = zc[:, :DC]
        sb = _sigmoid(zc[:, DC:])
        dz_ref[r:r + CONV_ROWS, :] = jnp.concatenate([du * sb, du * av * sb * (1.0 - sb)], axis=1).astype(dz_ref.dtype)


def _conv_bwd_taps(w0, w1, red_ref, u_s, us_s, dy_s, tc):
    for w in range(w0, w1):
        part = None
        for r in range(0, tc, CONV_ROWS):
            prod = _shifted(u_s, us_s, HALO - (CW - 1) + w + r, CONV_ROWS) * dy_s[r:r + CONV_ROWS, :]
            fold = jnp.sum(prod.reshape(CONV_ROWS // SUBLANES, SUBLANES, DC), axis=0)
            part = fold if part is None else part + fold
        red_ref[w:w + 1, :] += _rowsum(part)


def conv_bwd(z, y, dyc, wdw, cpar, tc, name):
    S = z.shape[0]
    nb = tc // HALO
    nt = S // tc
    last_halo = S // HALO - 1

    def body(zc_ref, zp_ref, y_ref, yn_ref, d_ref, dn_ref, w_ref, cp_ref, dz_ref, red_ref, u_s, dy_s, us_s, dys_s):
        i = pl.program_id(0)

        @pl.when(i == 0)
        def _():
            red_ref[...] = jnp.zeros_like(red_ref)

        _conv_bwd_prepare(i == 0, i == nt - 1, zc_ref, zp_ref, y_ref, yn_ref, d_ref, dn_ref, cp_ref, red_ref,
                          u_s, dy_s, us_s, dys_s, tc)
        _conv_bwd_input_rows(0, tc, zc_ref, w_ref, dz_ref, dy_s, dys_s)
        _conv_bwd_taps(0, CW, red_ref, u_s, us_s, dy_s, tc)

    cur = lambda i: (i, 0)
    nxt = lambda i: (jnp.minimum((i + 1) * nb, last_halo), 0)
    return pl.pallas_call(
        body, name=name,
        grid=(nt,),
        in_specs=[pl.BlockSpec((tc, 2 * DC), cur),
                  pl.BlockSpec((HALO, 2 * DC), lambda i: (jnp.maximum(i * nb - 1, 0), 0)),
                  pl.BlockSpec((tc, DC), cur), pl.BlockSpec((HALO, DC), nxt),
                  pl.BlockSpec((tc, DC), cur), pl.BlockSpec((HALO, DC), nxt),
                  _full(wdw.shape), _full(cpar.shape)],
        out_specs=[pl.BlockSpec((tc, 2 * DC), cur), _full((40, DC))],
        out_shape=[jax.ShapeDtypeStruct((S, 2 * DC), bf16), jax.ShapeDtypeStruct((40, DC), f32)],
        scratch_shapes=[pltpu.VMEM((HALO + tc, DC), f32), pltpu.VMEM((tc + HALO, DC), f32),
                        pltpu.VMEM((SUBLANES, HALO + tc, DC), f32), pltpu.VMEM((SUBLANES, HALO + tc, DC), f32)],
        compiler_params=_cp(1),
    )(z, z, y, y, dyc, dyc, wdw, cpar)


def _gla_consts():
    r = lax.broadcasted_iota(jnp.int32, (CH, CH), 0)
    c = lax.broadcasted_iota(jnp.int32, (CH, CH), 1)
    tril = r >= c
    lane = lax.broadcasted_iota(jnp.int32, (CH, DQK), 1)
    masks = [(lane >= h * DK) & (lane < (h + 1) * DK) for h in range(NH)]
    r4 = lax.broadcasted_iota(jnp.int32, (DQK, DQK), 0)
    c4 = lax.broadcasted_iota(jnp.int32, (DQK, DQK), 1)
    eye4 = (r4 == c4).astype(f32)
    rs = lax.broadcasted_iota(jnp.int32, (DQK, CH), 0) & (CH - 1)
    tril4 = rs >= lax.broadcasted_iota(jnp.int32, (DQK, CH), 1)
    return tril, tril4, masks, eye4


def _stack(xv, masks):
    return jnp.concatenate([jnp.where(m, xv, 0.0) for m in masks], axis=0)


def _unstack(rv, masks):
    out = jnp.where(masks[0], rv[0:CH, :], 0.0)
    for h in range(1, NH):
        out = out + jnp.where(masks[h], rv[h * CH:(h + 1) * CH, :], 0.0)
    return out


def _vstack(xv):
    return jnp.concatenate([xv[:, h * DV:(h + 1) * DV] for h in range(NH)], axis=0)


def _vunstack(xv):
    return jnp.concatenate([xv[h * CH:(h + 1) * CH, :] for h in range(NH)], axis=1)


def _gla_chunk_fwd(lac, qc, kc, vc, s_all, tril, masks, tril4):
    qc, kc = qc.astype(f32), kc.astype(f32)
    lmat = tril.astype(f32)
    bc = jnp.dot(lmat, lac, preferred_element_type=f32, precision=HIGHEST)
    bend = bc[CH - 1:CH, :]
    eb = jnp.exp(bc)
    enb = jnp.exp(-bc)
    ed = jnp.exp(bend - bc)
    qh = qc * (DK ** -0.5)
    qf = qh * eb
    qn = qh * enb
    kn = kc * enb
    kp = kc * eb
    kd = kc * ed
    qf_s = _stack(qf, masks).astype(bf16)
    qn_s = _stack(qn, masks).astype(bf16)
    kn_b = kn.astype(bf16)
    kp_b = kp.astype(bf16)
    attf = _dg(qf_s, kn_b, NT)
    attb = _dg(qn_s, kp_b, NT)
    a_s = jnp.where(tril4, attf, attb)
    a_b = a_s.astype(bf16)
    v_b = vc.astype(bf16)
    intra = jnp.concatenate(
        [_dot(a_b[h * CH:(h + 1) * CH, :], v_b[:, h * DV:(h + 1) * DV]) for h in range(NH)], axis=0)
    o_s = intra + _dot(qf_s, s_all.astype(bf16))
    return dict(bc=bc, bend=bend, eb=eb, enb=enb, ed=ed, qf=qf, qn=qn, kn=kn, kp=kp, kd=kd,
                qf_s=qf_s, qn_s=qn_s, kn_b=kn_b, kp_b=kp_b, a_b=a_b, v_b=v_b, o_s=o_s)


def _col_from_row(row, eye4):
    return jnp.sum(eye4 * row, axis=1, keepdims=True)


def _row_from_col(col, eye4):
    return jnp.sum(eye4 * col, axis=0, keepdims=True)


def _gla_fwd_chunk(c, consts, q_ref, k_ref, v_ref, r_ref, la_ref, gn_ref, yg_ref, sp_ref, st):
    tril, tril4, masks, eye4 = consts
    r0, s0 = c * CH, c * DQK
    s_all = st[...]
    sp_ref[s0:s0 + DQK, :] = s_all
    vc = v_ref[r0:r0 + CH, :]
    t = _gla_chunk_fwd(la_ref[r0:r0 + CH, :], q_ref[r0:r0 + CH, :], k_ref[r0:r0 + CH, :], vc,
                       s_all, tril, masks, tril4)
    u_all = _dg(_stack(t["kd"], masks).astype(bf16), _vstack(vc).astype(bf16), TN)
    st[...] = _col_from_row(jnp.exp(t["bend"]), eye4) * s_all + u_all
    o_s = t["o_s"]
    on = o_s * lax.rsqrt(jnp.mean(o_s * o_s, axis=-1, keepdims=True) + EPS) * gn_ref[...]
    rc = r_ref[r0:r0 + CH, :].astype(f32)
    yg_ref[r0:r0 + CH, :] = (_vunstack(on) * (rc * _sigmoid(rc))).astype(bf16)


def mixer_core_fwd(z, la, gn_s, wdw, cpar, t, name, comm=None):
    S = z.shape[0]
    nb = t // HALO
    nc = t // CH

    def body(zc_ref, zp_ref, q_ref, k_ref, v_ref, r_ref, la_ref, gn_ref, w_ref, cp_ref,
             y_ref, yc_ref, yg_ref, sp_ref, st, u_s, us_s):
        i = pl.program_id(0)

        @pl.when(i == 0)
        def _():
            st[...] = jnp.zeros_like(st)

        _conv_fwd_prepare(i == 0, zc_ref, zp_ref, u_s, us_s, t)
        consts = _gla_consts()
        for c in range(nc):
            _conv_fwd_rows(c * CH, CH, w_ref, cp_ref, y_ref, yc_ref, u_s, us_s)
            _gla_fwd_chunk(c, consts, q_ref, k_ref, v_ref, r_ref, la_ref, gn_ref, yg_ref, sp_ref, st)

    row = lambda i: (i, 0)
    return _pcall(
        body, (z, z, z, z, z, z, la, gn_s, wdw, cpar), name=name, comm=comm,
        grid=(S // t,),
        in_specs=[pl.BlockSpec((t, 2 * DC), row),
                  pl.BlockSpec((HALO, 2 * DC), lambda i: (jnp.maximum(i * nb - 1, 0), 0)),
                  pl.BlockSpec((t, DQK), lambda i: (i, 4)), pl.BlockSpec((t, DQK), lambda i: (i, 5)),
                  pl.BlockSpec((t, DG), lambda i: (i, 3)), pl.BlockSpec((t, DG), lambda i: (i, 4)),
                  pl.BlockSpec((t, DQK), row), _full(gn_s.shape), _full(wdw.shape), _full(cpar.shape)],
        out_specs=[pl.BlockSpec((t, DC), row), pl.BlockSpec((t, DC), row), pl.BlockSpec((t, DG), row),
                   pl.BlockSpec((nc * DQK, DV), row)],
        out_shape=[jax.ShapeDtypeStruct((S, DC), f32), jax.ShapeDtypeStruct((S, DC), bf16),
                   jax.ShapeDtypeStruct((S, DG), bf16), jax.ShapeDtypeStruct((S // CH * DQK, DV), f32)],
        scratch_shapes=[pltpu.VMEM((DQK, DV), f32), pltpu.VMEM((HALO + t, DC), f32),
                        pltpu.VMEM((SUBLANES, HALO + t, DC), f32)],
    )


def _gla_bwd_chunk(c, consts, umat, last_row, q_ref, k_ref, v_ref, r_ref, la_ref, sp_ref, dy_ref, gn_ref,
                   dq_ref, dk_ref, dv_ref, dr_ref, dpre_ref, redg_ref, redb_ref, gs):
    tril, tril4, masks, eye4 = consts
    r0, s0 = c * CH, c * DQK
    rows = slice(r0, r0 + CH)
    s_all = sp_ref[s0:s0 + DQK, :]
    lac = la_ref[rows, :]
    vc = v_ref[rows, :]
    rc = r_ref[rows, :].astype(f32)
    t = _gla_chunk_fwd(lac, q_ref[rows, :], k_ref[rows, :], vc, s_all, tril, masks, tril4)
    g_all = gs[...]
    g_b = g_all.astype(bf16)
    s_b = s_all.astype(bf16)
    o_s = t["o_s"]
    rstd = lax.rsqrt(jnp.mean(o_s * o_s, axis=-1, keepdims=True) + EPS)
    oh = o_s * rstd
    gnv = gn_ref[...]
    sr = _sigmoid(rc)
    dyv = dy_ref[rows, :]
    dr_ref[rows, :] = (dyv * _vunstack(oh * gnv) * (sr * (1.0 + rc * (1.0 - sr)))).astype(dr_ref.dtype)
    don = _vstack(dyv * (rc * sr))
    redg_ref[...] += don * oh
    doh = don * gnv
    do_s = rstd * (doh - oh * jnp.mean(doh * oh, axis=-1, keepdims=True))
    do_b = do_s.astype(bf16)
    v_b = t["v_b"]
    vst_b = _vstack(vc).astype(bf16)
    kd_s = _stack(t["kd"], masks).astype(bf16)
    da_s = jnp.concatenate(
        [_dg(do_b[h * CH:(h + 1) * CH, :], v_b[:, h * DV:(h + 1) * DV], NT) for h in range(NH)], axis=0)
    a_b = t["a_b"]
    dv_s = jnp.concatenate(
        [_dg(a_b[h * CH:(h + 1) * CH, :], do_b[h * CH:(h + 1) * CH, :], TN) for h in range(NH)], axis=0)
    dv_s = dv_s + _dot(kd_s, g_b)
    dv_ref[rows, :] = _vunstack(dv_s).astype(dv_ref.dtype)
    gend = jnp.exp(t["bend"])
    gcol = _col_from_row(gend, eye4)
    gs[...] = gcol * g_all + _dg(t["qf_s"], do_b, TN)
    dgcol = jnp.sum(g_all * s_all, axis=1, keepdims=True)
    dbend = _row_from_col(dgcol * gcol, eye4)
    dkd = _unstack(_dg(vst_b, g_b, NT), masks)
    daf = jnp.where(tril4, da_s, 0.0).astype(bf16)
    dab = jnp.where(tril4, 0.0, da_s).astype(bf16)
    dqf = _unstack(_dot(daf, t["kn_b"]) + _dg(do_b, s_b, NT), masks)
    dqn = _unstack(_dot(dab, t["kp_b"]), masks)
    dkn = _dg(daf, t["qf_s"], TN)
    dkp = _dg(dab, t["qn_s"], TN)
    dq_ref[rows, :] = ((dqf * t["eb"] + dqn * t["enb"]) * (DK ** -0.5)).astype(dq_ref.dtype)
    dk_ref[rows, :] = (dkn * t["enb"] + dkp * t["eb"] + dkd * t["ed"]).astype(dk_ref.dtype)
    dkd_kd = dkd * t["kd"]
    dbc = dqf * t["qf"] - dqn * t["qn"] - dkn * t["kn"] + dkp * t["kp"] - dkd_kd
    dbc = dbc + jnp.where(last_row, _rowsum(dkd_kd) + dbend, 0.0)
    dla = jnp.dot(umat, dbc, preferred_element_type=f32, precision=HIGHEST)
    dpre = dla * (1.0 / TAU) * (1.0 - jnp.exp(TAU * lac))
    dpre_ref[rows, :] = dpre.astype(dpre_ref.dtype)
    redb_ref[...] += dpre


def gla_bwd(z, la, sprev, dyg, gn_s, t, name, comm=None):
    S = z.shape[0]
    nc = t // CH
    nt = S // t

    def body(q_ref, k_ref, v_ref, r_ref, la_ref, sp_ref, dy_ref, gn_ref,
             dq_ref, dk_ref, dv_ref, dr_ref, dpre_ref, redg_ref, redb_ref, gs):
        @pl.when(pl.program_id(0) == 0)
        def _():
            gs[...] = jnp.zeros_like(gs)
            redg_ref[...] = jnp.zeros_like(redg_ref)
            redb_ref[...] = jnp.zeros_like(redb_ref)

        consts = _gla_consts()
        umat = (lax.broadcasted_iota(jnp.int32, (CH, CH), 0) <= lax.broadcasted_iota(jnp.int32, (CH, CH), 1)).astype(f32)
        last_row = lax.broadcasted_iota(jnp.int32, (CH, DQK), 0) == CH - 1
        for c in reversed(range(nc)):
            _gla_bwd_chunk(c, consts, umat, last_row, q_ref, k_ref, v_ref, r_ref, la_ref, sp_ref, dy_ref, gn_ref,
                           dq_ref, dk_ref, dv_ref, dr_ref, dpre_ref, redg_ref, redb_ref, gs)

    rev = lambda col: (lambda i: (nt - 1 - i, col))
    return _pcall(
        body, (z, z, z, z, la, sprev, dyg, gn_s), name=name, comm=comm,
        grid=(nt,),
        in_specs=[pl.BlockSpec((t, DQK), rev(4)), pl.BlockSpec((t, DQK), rev(5)),
                  pl.BlockSpec((t, DG), rev(3)), pl.BlockSpec((t, DG), rev(4)),
                  pl.BlockSpec((t, DQK), rev(0)), pl.BlockSpec((nc * DQK, DV), rev(0)),
                  pl.BlockSpec((t, DG), rev(0)), _full(gn_s.shape)],
        out_specs=[pl.BlockSpec((t, DQK), rev(0)), pl.BlockSpec((t, DQK), rev(0)),
                   pl.BlockSpec((t, DG), rev(0)), pl.BlockSpec((t, DG), rev(0)), pl.BlockSpec((t, DQK), rev(0)),
                   _full((DQK, DV)), _full((CH, DQK))],
        out_shape=[jax.ShapeDtypeStruct((S, DQK), bf16), jax.ShapeDtypeStruct((S, DQK), bf16),
                   jax.ShapeDtypeStruct((S, DG), bf16), jax.ShapeDtypeStruct((S, DG), bf16), jax.ShapeDtypeStruct((S, DQK), bf16),
                   jax.ShapeDtypeStruct((DQK, DV), f32), jax.ShapeDtypeStruct((CH, DQK), f32)],
        scratch_shapes=[pltpu.VMEM((DQK, DV), f32)],
    )


def mixout_fwd(x1, yc, yg, mod, wout, tm, name):
    S = x1.shape[0]

    def body(x_ref, yc_ref, yg_ref, mod_ref, w_ref, xo_ref):
        mixo = _dot(yc_ref[...], w_ref[0:DC, :]) + _dot(yg_ref[...], w_ref[DC:DC + DG, :])
        xo_ref[...] = x_ref[...] + mod_ref[5:6, :] * mixo

    row = lambda i: (i, 0)
    return pl.pallas_call(
        body, name=name,
        grid=(S // tm,),
        in_specs=[pl.BlockSpec((tm, D), row), pl.BlockSpec((tm, DC), row), pl.BlockSpec((tm, DG), row),
                  _full(mod.shape), _full(wout.shape)],
        out_specs=pl.BlockSpec((tm, D), row),
        out_shape=jax.ShapeDtypeStruct((S, D), f32),
        compiler_params=_cp(1),
    )(x1, yc, yg, mod, wout)


def mixout_bwd(dx2, yc, yg, mod, wout, tm, name):
    S = dx2.shape[0]

    def body(dx_ref, yc_ref, yg_ref, mod_ref, w_ref, dm_ref, dyc_ref, dyg_ref, red_ref):
        @pl.when(pl.program_id(0) == 0)
        def _():
            red_ref[...] = jnp.zeros_like(red_ref)

        dxv = dx_ref[...]
        mixo = _dot(yc_ref[...], w_ref[0:DC, :]) + _dot(yg_ref[...], w_ref[DC:DC + DG, :])
        red_ref[0:1, :] += _rowsum(dxv * mixo)
        dm = (mod_ref[5:6, :] * dxv).astype(bf16)
        dm_ref[...] = dm
        dycat = _dg(dm, w_ref[...], NT)
        dyc_ref[...] = dycat[:, :DC]
        dyg_ref[...] = dycat[:, DC:]

    row = lambda i: (i, 0)
    return pl.pallas_call(
        body, name=name,
        grid=(S // tm,),
        in_specs=[pl.BlockSpec((tm, D), row), pl.BlockSpec((tm, DC), row), pl.BlockSpec((tm, DG), row),
                  _full(mod.shape), _full(wout.shape)],
        out_specs=[pl.BlockSpec((tm, D), row), pl.BlockSpec((tm, DC), row), pl.BlockSpec((tm, DG), row), _full((8, D))],
        out_shape=[jax.ShapeDtypeStruct((S, D), bf16), jax.ShapeDtypeStruct((S, DC), f32),
                   jax.ShapeDtypeStruct((S, DG), f32), jax.ShapeDtypeStruct((8, D), f32)],
        compiler_params=_cp(1),
    )(dx2, yc, yg, mod, wout)


def final_fwd_bwd(x, tgt, fmod, g, tm, name):
    S = x.shape[0]

    def body(x_ref, t_ref, fm_ref, g_ref, dx_ref, red_ref):
        @pl.when(pl.program_id(0) == 0)
        def _():
            red_ref[...] = jnp.zeros_like(red_ref)

        xh, rstd = _rms_parts(x_ref[...])
        gv = g_ref[...]
        n = xh * gv
        sc = 1.0 + fm_ref[1:2, :]
        e = n * sc + fm_ref[0:1, :] - t_ref[...]
        red_ref[0:1, :] += _rowsum(e * e) * (0.5 / D)
        dy = e * (1.0 / D)
        dn = dy * sc
        red_ref[1:2, :] += _rowsum(dy)
        red_ref[2:3, :] += _rowsum(dy * n)
        red_ref[3:4, :] += _rowsum(dn * xh)
        dx_ref[...] = _rms_bwd(dn * gv, xh, rstd)

    row = lambda i: (i, 0)
    return pl.pallas_call(
        body, name=name,
        grid=(S // tm,),
        in_specs=[pl.BlockSpec((tm, D), row), pl.BlockSpec((tm, D), row), _full(fmod.shape), _full(g.shape)],
        out_specs=[pl.BlockSpec((tm, D), row), _full((8, D))],
        out_shape=[jax.ShapeDtypeStruct((S, D), f32), jax.ShapeDtypeStruct((8, D), f32)],
        compiler_params=_cp(1),
    )(x, tgt, fmod, g)


def ada_fwd(c_all, w, b, name):
    n = w.shape[1]

    def body(c_ref, w_ref, b_ref, o_ref):
        cv = c_ref[...]
        o_ref[...] = jnp.dot(cv * _sigmoid(cv), w_ref[...], preferred_element_type=f32, precision=HIGHEST) + b_ref[...]

    return pl.pallas_call(
        body, name=name,
        in_specs=[_full(c_all.shape), _full(w.shape), _full(b.shape)],
        out_specs=_full((N_DEV, n)),
        out_shape=jax.ShapeDtypeStruct((N_DEV, n), f32),
        grid=(1,),
        compiler_params=_cp(1),
    )(c_all, w, b)


def ada_wgrad(c_all_t, dm, name):
    n = dm.shape[1]

    def body(c_ref, d_ref, o_ref):
        cv = c_ref[...]
        o_ref[...] = jnp.dot(cv * _sigmoid(cv), d_ref[...], preferred_element_type=f32, precision=HIGHEST)

    return pl.pallas_call(
        body, name=name,
        in_specs=[_full(c_all_t.shape), _full(dm.shape)],
        out_specs=_full((D, n)),
        out_shape=jax.ShapeDtypeStruct((D, n), f32),
        grid=(1,),
        compiler_params=_cp(1),
    )(c_all_t, dm)


def _adam_math(gv, wv, mv, vv):
    m = ADAM_B1 * mv + (1.0 - ADAM_B1) * gv
    v = ADAM_B2 * vv + (1.0 - ADAM_B2) * (gv * gv)
    m_hat = m / (1.0 - ADAM_B1 ** ADAM_STEP)
    v_hat = v / (1.0 - ADAM_B2 ** ADAM_STEP)
    delta = -ADAM_LR * (m_hat / (jnp.sqrt(v_hat) + ADAM_EPS) + ADAM_WD * wv)
    return delta, m, v


def adam_parts(parts, w, m, v, tr, name, comm=None):
    L, R, C = w.shape
    nt = R // tr

    def body(*refs):
        p_refs = refs[:L]
        w_ref, m_ref, v_ref, g_ref, d_ref, mo_ref, vo_ref = refs[L:]
        lyr = pl.program_id(0)
        for l in range(L):
            @pl.when(lyr == l)
            def _(p_ref=p_refs[l]):
                gv = p_ref[0].astype(f32)
                for k in range(1, N_DEV):
                    gv = gv + p_ref[k].astype(f32)
                g_ref[...] = gv
                d_ref[...], mo_ref[...], vo_ref[...] = _adam_math(gv, w_ref[...], m_ref[...], v_ref[...])

    def part_spec(l):
        return pl.BlockSpec((N_DEV, tr, C), lambda lyr, i: (0, jnp.where(lyr == l, i, jnp.where(lyr < l, 0, nt - 1)), 0))

    spec = pl.BlockSpec((None, tr, C), lambda lyr, i: (lyr, i, 0))
    shp = jax.ShapeDtypeStruct((L, R, C), f32)
    return _pcall(
        body, (*parts, w, m, v), name=name, comm=comm,
        grid=(L, nt),
        in_specs=[part_spec(l) for l in range(L)] + [spec, spec, spec],
        out_specs=[spec, spec, spec, spec],
        out_shape=[shp, shp, shp, shp],
    )


def adam_plain(gr, w, m, v, tr, name):
    R, C = w.shape

    def body(g_ref, w_ref, m_ref, v_ref, d_ref, mo_ref, vo_ref):
        d_ref[...], mo_ref[...], vo_ref[...] = _adam_math(g_ref[...], w_ref[...], m_ref[...], v_ref[...])

    spec = pl.BlockSpec((tr, C), lambda i: (i, 0))
    shp = jax.ShapeDtypeStruct((R, C), f32)
    return pl.pallas_call(
        body, name=name,
        grid=(R // tr,),
        in_specs=[spec, spec, spec, spec],
        out_specs=[spec, spec, spec],
        out_shape=[shp, shp, shp],
        compiler_params=_cp(1),
    )(gr, w, m, v)


def sum8(parts, name):
    _, R, C = parts.shape

    def body(p_ref, o_ref):
        acc = p_ref[0]
        for k in range(1, N_DEV):
            acc = acc + p_ref[k]
        o_ref[...] = acc

    return pl.pallas_call(
        body, name=name,
        grid=(1,),
        in_specs=[_full(parts.shape)],
        out_specs=_full((R, C)),
        out_shape=jax.ShapeDtypeStruct((R, C), f32),
        compiler_params=_cp(1),
    )(parts)


def _place():
    return lax.axis_index("x"), lax.axis_index("y"), lax.axis_index("c")


def _gather_steps(ins, outs, send_sems, recv_sems, local_sems, place):
    n = len(ins)
    x, y, c = place
    me, sibling = (x, y, c), (x, y, 1 - c)
    chips = [(1 - x, y), (x, 1 - y), (1 - x, 1 - y)]

    def slot(a, p):
        return outs[a].at[4 * p[0] + 2 * p[1] + p[2]]

    def copy(a, k, block, to, src=None):
        return pltpu.make_async_remote_copy(
            src_ref=slot(a, block) if src is None else src, dst_ref=slot(a, block),
            send_sem=send_sems.at[a * 7 + k], recv_sem=recv_sems.at[a * 7 + k],
            device_id=to, device_id_type=MESH)

    def mine():
        return [pltpu.make_async_copy(ins[a], slot(a, me), local_sems.at[a]) for a in range(n)]

    def first():
        cps = []
        for a in range(n):
            cps.append(copy(a, 0, me, sibling, src=ins[a]))
            cps += [copy(a, 1 + j, me, (*chip, c), src=ins[a]) for j, chip in enumerate(chips)]
        return cps

    def start():
        for cp in mine() + first():
            cp.start()

    def forward():
        for j, chip in enumerate(chips):
            for a in range(n):
                copy(a, 1 + j, (*chip, c), me).wait_recv()
                copy(a, 4 + j, (*chip, c), sibling).start()

    def finish():
        for a in range(n):
            copy(a, 0, sibling, me).wait_recv()
            for j, chip in enumerate(chips):
                copy(a, 4 + j, (*chip, 1 - c), me).wait_recv()
        for cp in first() + [copy(a, 4 + j, (*chip, c), sibling) for j, chip in enumerate(chips) for a in range(n)]:
            cp.wait_send()
        for cp in mine():
            cp.wait()

    return start, forward, finish


def _exchange_steps(ins, outs, send_sems, recv_sems, local_sems, place):
    n = len(ins)
    x, y, c = place
    me_i = 4 * x + 2 * y + c

    def mine():
        return [pltpu.make_async_copy(ins[a].at[me_i], outs[a].at[me_i], local_sems.at[a]) for a in range(n)]

    def copies(receiving):
        cps = []
        for k in range(1, N_DEV):
            px = 1 - x if (k >> 2) & 1 else x
            py = 1 - y if (k >> 1) & 1 else y
            pc = 1 - c if k & 1 else c
            p_i = 4 * px + 2 * py + pc
            for a in range(n):
                sem = a * 7 + k - 1
                cps.append(pltpu.make_async_remote_copy(
                    src_ref=ins[a].at[p_i], dst_ref=outs[a].at[p_i if receiving else me_i],
                    send_sem=send_sems.at[sem], recv_sem=recv_sems.at[sem],
                    device_id=(px, py, pc), device_id_type=MESH))
        return cps

    def start():
        for cp in mine() + copies(False):
            cp.start()

    def finish():
        for cp in copies(True):
            cp.wait_recv()
        for cp in copies(False):
            cp.wait_send()
        for cp in mine():
            cp.wait()

    return start, None, finish


_COMM_STEPS = {"gather": _gather_steps, "exchange": _exchange_steps}


def _comm_out_shapes(kind, arrs):
    if kind == "gather":
        return [jax.ShapeDtypeStruct((N_DEV,) + a.shape, a.dtype) for a in arrs]
    return [jax.ShapeDtypeStruct(a.shape, a.dtype) for a in arrs]


def _comm_sems(n):
    return [pltpu.SemaphoreType.DMA((7 * n,)), pltpu.SemaphoreType.DMA((7 * n,)), pltpu.SemaphoreType.DMA((n,))]


def _pcall(body, args, *, name, grid, in_specs, out_specs, out_shape, scratch_shapes=(), comm=None):
    in_specs, out_specs, out_shape = list(in_specs), list(out_specs), list(out_shape)
    scratch_shapes = list(scratch_shapes)
    cparams = _cp(len(grid))
    if comm is None:
        outs = pl.pallas_call(body, name=name, grid=grid, in_specs=in_specs, out_specs=out_specs, out_shape=out_shape,
                              scratch_shapes=scratch_shapes, compiler_params=cparams)(*args)
        return list(outs), []
    kind, arrs = comm
    nc, n_in, n_out, n_scr = len(arrs), len(in_specs), len(out_specs), len(scratch_shapes)
    total = 1
    for gdim in grid:
        total *= gdim
    forward_step = (total * 3) // 4

    def hosted(*refs):
        core_in, c_in = refs[:n_in], refs[n_in:n_in + nc]
        core_out = refs[n_in + nc:n_in + nc + n_out]
        c_out = refs[n_in + nc + n_out:n_in + 2 * nc + n_out]
        rest = refs[n_in + 2 * nc + n_out:]
        step = pl.program_id(0)
        for ax in range(1, len(grid)):
            step = step * grid[ax] + pl.program_id(ax)
        start, forward, finish = _COMM_STEPS[kind](c_in, c_out, *rest[n_scr:], _place())
        pl.when(step == 0)(start)
        if forward is not None:
            pl.when(step == forward_step)(forward)
        body(*core_in, *core_out, *rest[:n_scr])
        pl.when(step == total - 1)(finish)

    any_spec = pl.BlockSpec(memory_space=pl.ANY)
    outs = pl.pallas_call(
        hosted, name=name, grid=grid,
        in_specs=in_specs + [any_spec] * nc,
        out_specs=out_specs + [any_spec] * nc,
        out_shape=out_shape + _comm_out_shapes(kind, arrs),
        scratch_shapes=scratch_shapes + _comm_sems(nc),
        compiler_params=cparams)(*args, *arrs)
    return list(outs[:n_out]), list(outs[n_out:])


def _comm_call(kind, arrs, name):
    n = len(arrs)

    def body(*refs):
        start, forward, finish = _COMM_STEPS[kind](refs[:n], refs[n:2 * n], *refs[2 * n:], _place())
        start()
        if forward is not None:
            forward()
        finish()

    any_spec = pl.BlockSpec(memory_space=pl.ANY)
    return pl.pallas_call(
        body, name=name,
        in_specs=[any_spec] * n, out_specs=[any_spec] * n,
        out_shape=_comm_out_shapes(kind, arrs), scratch_shapes=_comm_sems(n),
    )(*arrs)


def all_gather(arrs, name):
    return _comm_call("gather", arrs, name)


def all_to_all(arrs, name):
    return _comm_call("exchange", arrs, name)


def _tiles(S):
    t = min(512, S)
    return dict(ffn=min(256, S), row=t, conv=t, gla=t, bk=min(1024, S))


BIG = ("wi1", "wo1", "win", "wout", "wi2", "wo2")


def _col_shards_to_full(gathered):
    n, r, c = gathered.shape
    return jnp.transpose(gathered, (1, 0, 2)).reshape(r, n * c)


def _win_full(win_a):
    return _pad_rows(win_a.reshape(DIN, D), DINP)


def train_pass(x, tgt, mods, fmod, sh, ws, wi1_first, wo1_first):
    S = x.shape[0]
    T = _tiles(S)
    bk = T["bk"]
    full = [dict() for _ in range(DEPTH)]
    hosted = {("ffn1", 0): [("win", 0), ("wout", 0), ("wo2", 0)], ("core", 0): [("wi2", 0)]}
    for l in range(1, DEPTH):
        hosted[("mixin", l - 1)] = [("win", l), ("wout", l)]
        hosted[("ffn2", l - 1)] = [("wi1", l), ("wo1", l)]
        hosted[("ffn1", l)] = [("wi2", l), ("wo2", l)]

    def comm_for(key):
        return ("gather", [sh[n][ll] for n, ll in hosted[key]]) if key in hosted else None

    def keep(key, got):
        for (n, ll), gathered in zip(hosted.get(key, []), got):
            if n in ("wi1", "wi2"):
                full[ll][n] = gathered.reshape(2 * F, D)
            elif n in ("wo1", "wo2"):
                full[ll][n] = gathered.reshape(F, D)
            else:
                full[ll][n] = _win_full(gathered) if n == "win" else gathered.reshape(D, D)

    full[0]["wi1"], full[0]["wo1"] = wi1_first.reshape(2 * F, D), wo1_first.reshape(F, D)
    saved = []
    xc = x
    for l in range(DEPTH):
        w, fw = ws[f"L{l}"], full[l]
        x0 = xc
        (x1, h1f, z1, f1), got = ffn_fwd(x0, mods[l], w["g1"], fw["wi1"], fw["wo1"], (0, 1, 2), T["ffn"], f"ffn1_fwd_{l}",
                                         comm=comm_for(("ffn1", l)))
        keep(("ffn1", l), got)
        (z, la), got = mixin_fwd(x1, mods[l], w["g2"], fw["win"], w["wgu"], w["bgate"], T["row"], f"mixin_fwd_{l}",
                                 comm=comm_for(("mixin", l)))
        keep(("mixin", l), got)
        (y, yc, yg, sprev), got = mixer_core_fwd(z, la, w["gn_s"], w["wdw"], w["cpar"], T["gla"], f"mixer_core_fwd_{l}",
                                                 comm=comm_for(("core", l)))
        keep(("core", l), got)
        x2 = mixout_fwd(x1, yc, yg, mods[l], fw["wout"], T["row"], f"mixout_fwd_{l}")
        (x3, h2f, z2, f2), got = ffn_fwd(x2, mods[l], w["g3"], fw["wi2"], fw["wo2"], (6, 7, 8), T["ffn"], f"ffn2_fwd_{l}",
                                         comm=comm_for(("ffn2", l)))
        keep(("ffn2", l), got)
        saved.append(dict(x0=x0, x1=x1, x2=x2, h1f=h1f, z1=z1, f1=f1, h2f=h2f, z2=z2, f2=f2,
                          z=z, la=la, y=y, yc=yc, yg=yg, sprev=sprev))
        xc = x3

    dx, redf = final_fwd_bwd(xc, tgt, fmod, ws["gf"], T["row"], "loss_head")
    loss_lanes = redf[0]
    dfmod = redf[1:3]
    grads = {"gf": redf[3]}
    dmods = [None] * DEPTH
    recv = {n: [None] * DEPTH for n in BIG}

    def ffn_backward(xin, dy, h, z, fo, gain, wi_t, wo, rows, l, tag, ride=None):
        (dz, p_wo), got_ride = ffn_bwd_hidden(dy, z, mods[l], wo.T, rows[2], T["ffn"], f"{tag}_bwd_hidden_{l}",
                                              comm=("exchange", ride) if ride else None)
        p_wi, (r_wo,) = dwi_pieces(h, dz, T["bk"], f"d{tag}_wi_{l}",
                                   comm=("exchange", [p_wo.reshape(N_DEV, F // N_DEV, D)]))
        (dxin, red), (r_wi,) = ffn_bwd_input(xin, dy, dz, fo, mods[l], gain, wi_t, rows, T["ffn"],
                                             f"{tag}_bwd_input_{l}", comm=("exchange", [p_wi]))
        return dxin, red, r_wi, r_wo, got_ride

    for l in reversed(range(DEPTH)):
        w, fw, sv = ws[f"L{l}"], full[l], saved[l]
        g = {}
        dx2, red3, recv["wi2"][l], recv["wo2"][l], _ = ffn_backward(
            sv["x2"], dx, sv["h2f"], sv["z2"], sv["f2"], w["g3"], fw["wi2"], fw["wo2"], (6, 7, 8), l, "ffn2")
        dmix, dyc, dyg, red_o = mixout_bwd(dx2, sv["yc"], sv["yg"], mods[l], fw["wout"], T["row"], f"mixout_bwd_{l}")
        p_wout = jnp.concatenate([matmul_tn(sv["yc"], dmix, DC, D, DC, D, bk, f"dwout_c_{l}", out_dtype=bf16),
                                  matmul_tn(sv["yg"], dmix, DG, D, DG, D, bk, f"dwout_g_{l}", out_dtype=bf16)], axis=0)
        (dq, dk, dv, dr, dpre, redg, redb), (recv["wout"][l],) = gla_bwd(
            sv["z"], sv["la"], sv["sprev"], dyg, w["gn_s"], T["gla"], f"gla_bwd_{l}",
            comm=("exchange", [p_wout.reshape(N_DEV, D // N_DEV, D)]))
        dzab, redc = conv_bwd(sv["z"], sv["y"], dyc, w["wdw"], w["cpar"], T["conv"], f"conv_bwd_{l}")
        (dx1, h2, dz, red2), _ = mixin_bwd(sv["x1"], dx2, dzab, dq, dk, dv, dr, dpre, mods[l], w["g2"], fw["win"], w["wgu"],
                                           T["row"], f"mixin_bwd_{l}")
        dwin_t = matmul_tn(dz, h2, DINP, D, DINP, D, bk, f"dwin_{l}", out_dtype=bf16)
        p_win = dwin_t[:DIN].reshape(N_DEV, DIN // N_DEV, D)
        g["wgu"] = matmul_tn(sv["z"], dpre, 128, DQK, 128, DQK, bk, f"dwgu_{l}", a_col_block=(DINP - 128) // 128)[:GR]
        g["bgate"] = jnp.sum(redb, axis=0)
        g["gn"] = jnp.sum(redg.reshape(NH, CH, DV), axis=1)
        g["wdw"] = redc[:CW]
        g["bdw"], g["gln"], g["bln"] = redc[32], redc[33], redc[34]
        dx0, red1, recv["wi1"][l], recv["wo1"][l], (recv["win"][l],) = ffn_backward(
            sv["x0"], dx1, sv["h1f"], sv["z1"], sv["f1"], w["g1"], fw["wi1"], fw["wo1"], (0, 1, 2), l, "ffn1",
            ride=[p_win])
        g["g1"], g["g2"], g["g3"] = red1[3], red2[2], red3[3]
        dmods[l] = jnp.stack([red1[0], red1[1], red1[2], red2[0], red2[1], red_o[0], red3[0], red3[1], red3[2]], axis=0)
        grads[f"L{l}"] = g
        dx = dx0
    return loss_lanes, dx, grads, dmods, dfmod, recv


def _pad_rows(a, rows):
    return jnp.pad(a, ((0, rows - a.shape[0]), (0, 0)))


def kernel(x, c, w_ada, b_ada, g_norm_ffn1, w_ffn1_in, w_ffn1_out, g_norm_mix, w_in, w_dw, b_dw, g_conv_ln, b_conv_ln, w_gate_up, b_gate, g_gla_norm, w_out, g_norm_ffn2, w_ffn2_in, w_ffn2_out, g_norm_final, w_ada_final, b_ada_final, loss_target, m_w_ada, m_b_ada, m_g_norm_ffn1, m_w_ffn1_in, m_w_ffn1_out, m_g_norm_mix, m_w_in, m_w_dw, m_b_dw, m_g_conv_ln, m_b_conv_ln, m_w_gate_up, m_b_gate, m_g_gla_norm, m_w_out, m_g_norm_ffn2, m_w_ffn2_in, m_w_ffn2_out, m_g_norm_final, m_w_ada_final, m_b_ada_final, v_w_ada, v_b_ada, v_g_norm_ffn1, v_w_ffn1_in, v_w_ffn1_out, v_g_norm_mix, v_w_in, v_w_dw, v_b_dw, v_g_conv_ln, v_b_conv_ln, v_w_gate_up, v_b_gate, v_g_gla_norm, v_w_out, v_g_norm_ffn2, v_w_ffn2_in, v_w_ffn2_out, v_g_norm_final, v_w_ada_final, v_b_ada_final):
    me = 4 * lax.axis_index("x") + 2 * lax.axis_index("y") + lax.axis_index("c")
    L = DEPTH
    n_ada = N_MOD * D // N_DEV
    n_fin = 2 * D // N_DEV

    small = jnp.concatenate([c.reshape(-1), w_dw.reshape(-1), w_gate_up.reshape(-1)])
    n_small = small.shape[0]
    small = jnp.pad(small, (0, 8 * D - n_small)).reshape(8, D)
    big = dict(wi1=w_ffn1_in, wo1=w_ffn1_out, win=w_in, wout=w_out, wi2=w_ffn2_in, wo2=w_ffn2_out)
    transposed = ("wi1", "wi2", "win")
    sh = {n: [(a[l].T if n in transposed else a[l]).astype(bf16) for l in range(L)] for n, a in big.items()}
    small_a, wi1_first, wo1_first = all_gather([small, sh["wi1"][0], sh["wo1"][0]], "gather_first")
    small_a = small_a.reshape(N_DEV, 8 * D)
    c_all = small_a[:, :D]
    o1 = D + L * CW * (DC // N_DEV)
    wdw_full = _col_shards_to_full(small_a[:, D:o1].reshape(N_DEV, L * CW, DC // N_DEV)).reshape(L, CW, DC)
    wgu_full = _col_shards_to_full(small_a[:, o1:o1 + L * GR * (DQK // N_DEV)].reshape(N_DEV, L * GR, DQK // N_DEV)).reshape(L, GR, DQK)

    b_ada_mine = lax.dynamic_slice(b_ada, (0, me * n_ada), (L, n_ada))
    b_fin_mine = lax.dynamic_slice(b_ada_final, (me * n_fin,), (n_fin,))
    parts = [ada_fwd(c_all, w_ada[l], b_ada_mine[l:l + 1], f"ada_fwd_{l}") for l in range(L)]
    parts.append(ada_fwd(c_all, w_ada_final, b_fin_mine.reshape(1, n_fin), "ada_fwd_final"))
    modsrc = jnp.concatenate(parts, axis=1)
    n_row = modsrc.shape[1]
    modsrc = jnp.pad(modsrc, ((0, 0), (0, 24 * 128 - n_row))).reshape(N_DEV, 24, 128)
    (modrecv,) = all_to_all([modsrc], "exchange_mod")
    modrecv = modrecv.reshape(N_DEV, 24 * 128)
    mods = []
    for l in range(L):
        mvec = modrecv[:, l * n_ada:(l + 1) * n_ada].reshape(N_MOD, D)
        mods.append(_pad_rows(mvec, 16))
    fmod = _pad_rows(modrecv[:, L * n_ada:L * n_ada + n_fin].reshape(2, D), 8)

    ws = {"gf": g_norm_final.reshape(1, D)}
    for l in range(L):
        ws[f"L{l}"] = dict(
            g1=g_norm_ffn1[l].reshape(1, D), g2=g_norm_mix[l].reshape(1, D), g3=g_norm_ffn2[l].reshape(1, D),
            wgu=_pad_rows(wgu_full[l], 128).astype(bf16),
            bgate=b_gate[l].reshape(1, DQK),
            wdw=_pad_rows(wdw_full[l], 32),
            cpar=_pad_rows(jnp.stack([b_dw[l], g_conv_ln[l], b_conv_ln[l]]), 8),
            gn_s=jnp.repeat(g_gla_norm[l], CH, axis=0),
        )

    loss_lanes, grad_x, gr, dmods, dfmod, recv = train_pass(
        x[0], loss_target[0], mods, fmod, sh, ws, wi1_first, wo1_first)

    def adam_big(rv, w, m, v, name, is_transposed=False):
        if is_transposed:
            w, m, v = (jnp.swapaxes(a, 1, 2) for a in (w, m, v))
        R = w.shape[1]
        tr = 256 if R % 256 == 0 else (R // 2 if (R // 2) % 16 == 0 else R)
        outs, _ = adam_parts(rv, w, m, v, tr, name)
        return [jnp.swapaxes(o, 1, 2) for o in outs] if is_transposed else outs

    res = {}
    res["w_ffn2_in"] = adam_big(recv["wi2"], w_ffn2_in, m_w_ffn2_in, v_w_ffn2_in, "adam_ffn2_in", True)
    res["w_ffn2_out"] = adam_big(recv["wo2"], w_ffn2_out, m_w_ffn2_out, v_w_ffn2_out, "adam_ffn2_out")
    res["w_in"] = adam_big(recv["win"], w_in, m_w_in, v_w_in, "adam_w_in", True)
    res["w_out"] = adam_big(recv["wout"], w_out, m_w_out, v_w_out, "adam_w_out")
    res["w_ffn1_out"] = adam_big(recv["wo1"], w_ffn1_out, m_w_ffn1_out, v_w_ffn1_out, "adam_ffn1_out")
    res["w_ffn1_in"] = adam_big(recv["wi1"], w_ffn1_in, m_w_ffn1_in, v_w_ffn1_in, "adam_ffn1_in", True)

    flat = lambda name: jnp.stack([gr[f"L{l}"][name] for l in range(L)]).reshape(-1)
    sections = [
        ("b_ada", jnp.stack(dmods).reshape(-1)), ("b_ada_final", dfmod.reshape(-1)),
        ("g_norm_ffn1", flat("g1")), ("g_norm_mix", flat("g2")), ("g_norm_ffn2", flat("g3")), ("g_norm_final", gr["gf"]),
        ("b_dw", flat("bdw")), ("g_conv_ln", flat("gln")), ("b_conv_ln", flat("bln")), ("b_gate", flat("bgate")),
        ("g_gla_norm", flat("gn")),
    ]
    n_rep = sum(s[1].shape[0] for s in sections)
    rep_rows = -(-n_rep // D)
    extra = [("loss", loss_lanes), ("w_dw", flat("wdw")), ("w_gate_up", flat("wgu"))]
    pack = jnp.concatenate([s[1] for s in sections] + [jnp.zeros((rep_rows * D - n_rep,), f32)] + [s[1] for s in extra])
    n_pack = pack.shape[0]
    pack_rows = -(-n_pack // (8 * D)) * 8
    pack = jnp.pad(pack, (0, pack_rows * D - n_pack)).reshape(pack_rows, D)
    (pack_all,) = all_gather([pack], "gather_small_grads")
    tot = sum8(pack_all, "sum_small_grads")
    tot_flat = tot.reshape(-1)
    loss = jnp.sum(tot_flat[rep_rows * D:rep_rows * D + D])
    o_dw = rep_rows * D + D
    g_wdw_full = tot_flat[o_dw:o_dw + L * CW * DC].reshape(L, CW, DC)
    o_gu = o_dw + L * CW * DC
    g_wgu_full = tot_flat[o_gu:o_gu + L * GR * DQK].reshape(L, GR, DQK)

    small_params = dict(b_ada=(b_ada, m_b_ada, v_b_ada), b_ada_final=(b_ada_final, m_b_ada_final, v_b_ada_final),
                        g_norm_ffn1=(g_norm_ffn1, m_g_norm_ffn1, v_g_norm_ffn1), g_norm_mix=(g_norm_mix, m_g_norm_mix, v_g_norm_mix),
                        g_norm_ffn2=(g_norm_ffn2, m_g_norm_ffn2, v_g_norm_ffn2), g_norm_final=(g_norm_final, m_g_norm_final, v_g_norm_final),
                        b_dw=(b_dw, m_b_dw, v_b_dw), g_conv_ln=(g_conv_ln, m_g_conv_ln, v_g_conv_ln),
                        b_conv_ln=(b_conv_ln, m_b_conv_ln, v_b_conv_ln), b_gate=(b_gate, m_b_gate, v_b_gate),
                        g_gla_norm=(g_gla_norm, m_g_gla_norm, v_g_gla_norm))

    def rep_pack(idx):
        p = jnp.concatenate([small_params[s[0]][idx].reshape(-1) for s in sections])
        return jnp.pad(p, (0, rep_rows * D - n_rep)).reshape(rep_rows, D)

    g_rep = tot[:rep_rows]
    d_rep, m_rep, v_rep = adam_plain(g_rep, rep_pack(0), rep_pack(1), rep_pack(2), rep_rows, "adam_small")
    off = 0
    for sname, sval in sections:
        shp = small_params[sname][0].shape
        nel = sval.shape[0]
        res[sname] = [a.reshape(-1)[off:off + nel].reshape(shp) for a in (g_rep, d_rep, m_rep, v_rep)]
        off += nel

    def adam_cols(g_full, w, m, v, name):
        shp = w.shape
        g_mine = lax.dynamic_slice(g_full, (0, 0, me * shp[2]), shp)
        R, C = shp[0] * shp[1], shp[2]
        outs = adam_plain(g_mine.reshape(R, C), w.reshape(R, C), m.reshape(R, C), v.reshape(R, C), R, name)
        return [g_mine] + [o.reshape(shp) for o in outs]

    res["w_dw"] = adam_cols(g_wdw_full, w_dw, m_w_dw, v_w_dw, "adam_w_dw")
    res["w_gate_up"] = adam_cols(g_wgu_full, w_gate_up, m_w_gate_up, v_w_gate_up, "adam_w_gate_up")

    c_all_t = c_all.T
    dmod_all = pack_all.reshape(N_DEV, -1)[:, :L * N_MOD * D].reshape(N_DEV, L, N_MOD * D)
    dfm_all = pack_all.reshape(N_DEV, -1)[:, L * N_MOD * D:L * N_MOD * D + 2 * D]
    dm_mine = lax.dynamic_slice(dmod_all, (0, 0, me * n_ada), (N_DEV, L, n_ada))
    dfm_mine = lax.dynamic_slice(dfm_all, (0, me * n_fin), (N_DEV, n_fin))
    g_w_ada = jnp.stack([ada_wgrad(c_all_t, dm_mine[:, l], f"ada_wgrad_{l}") for l in range(L)])
    g_w_fin = ada_wgrad(c_all_t, dfm_mine, "ada_wgrad_final")
    outs = adam_plain(g_w_ada.reshape(L * D, n_ada), w_ada.reshape(L * D, n_ada), m_w_ada.reshape(L * D, n_ada),
                      v_w_ada.reshape(L * D, n_ada), 256, "adam_w_ada")
    res["w_ada"] = [g_w_ada] + [o.reshape(w_ada.shape) for o in outs]
    res["w_ada_final"] = [g_w_fin] + list(adam_plain(g_w_fin, w_ada_final, m_w_ada_final, v_w_ada_final, 256, "adam_w_ada_final"))

    order = ["w_ada", "b_ada", "g_norm_ffn1", "w_ffn1_in", "w_ffn1_out", "g_norm_mix", "w_in", "w_dw", "b_dw", "g_conv_ln",
             "b_conv_ln", "w_gate_up", "b_gate", "g_gla_norm", "w_out", "g_norm_ffn2", "w_ffn2_in", "w_ffn2_out",
             "g_norm_final", "w_ada_final", "b_ada_final"]
    out = [loss, grad_x[None]]
    for k in range(4):
        out += [res[name][k] for name in order]
    return tuple(out)
```

```python
import functools

import jax
import jax.numpy as jnp
from jax import lax
from jax.experimental import pallas as pl
from jax.experimental.pallas import tpu as pltpu

f32 = jnp.float32
bf16 = jnp.bfloat16

N_DEV = 8
DEPTH = 2
D = 1024
F = 2816
DC = 512
NH = 4
DK = 64
DV = 128
DQK = NH * DK
DG = NH * DV
CH = 64
CW = 31
GR = 16
TAU = 16.0
N_MOD = 9
DIN = 2 * DC + 2 * DQK + 2 * DG + GR
DINP = 2688
EPS = 1e-6
HALO = 32
SUBLANES = 8
CONV_ROWS = 32
FS = 2 * F // N_DEV

ADAM_LR = 0.001
ADAM_B1 = 0.9
ADAM_B2 = 0.999
ADAM_EPS = 1e-08
ADAM_WD = 0.01
ADAM_STEP = 10

V7X_VMEM_LIMIT = 56 * 1024 * 1024
MESH = pl.DeviceIdType.MESH
HIGHEST = lax.Precision.HIGHEST

NT = (((1,), (1,)), ((), ()))
TN = (((0,), (0,)), ((), ()))


def _cp(n_axes):
    return pltpu.CompilerParams(dimension_semantics=("arbitrary",) * n_axes, vmem_limit_bytes=V7X_VMEM_LIMIT)


def _full(shape):
    nd = len(shape)
    return pl.BlockSpec(shape, lambda *_: (0,) * nd)


def _resident(shape):
    nd = len(shape)
    return pl.BlockSpec(shape, lambda *_: (0,) * nd, pipeline_mode=pl.Buffered(1))


def _dot(a, b):
    return jnp.dot(a, b, preferred_element_type=f32)


def _dg(a, b, dims):
    return lax.dot_general(a, b, dims, preferred_element_type=f32)


def _sigmoid(x):
    return jax.nn.sigmoid(x)


def _rowsum(x):
    return jnp.sum(x, axis=0, keepdims=True)


def _rms_parts(xv):
    rstd = lax.rsqrt(jnp.mean(xv * xv, axis=-1, keepdims=True) + EPS)
    return xv * rstd, rstd


def _rms_bwd(dxh, xh, rstd):
    return rstd * (dxh - xh * jnp.mean(dxh * xh, axis=-1, keepdims=True))


def ffn_fwd(x, mod, g, wi_t, wo, rows, tm, name, comm=None):
    S = x.shape[0]
    r_shift, r_scale, r_gate = rows

    def body(x_ref, mod_ref, g_ref, wi_ref, wo_ref, xo_ref, h_ref, z_ref, f_ref):
        xv = x_ref[...]
        xh, _ = _rms_parts(xv)
        h = (xh * g_ref[...] * (1.0 + mod_ref[r_scale:r_scale + 1, :]) + mod_ref[r_shift:r_shift + 1, :]).astype(bf16)
        h_ref[...] = h
        zg = _dg(h, wi_ref[0:F, :], NT)
        zu = _dg(h, wi_ref[F:2 * F, :], NT)
        z_ref[:, 0:F] = zg.astype(bf16)
        z_ref[:, F:2 * F] = zu.astype(bf16)
        fv = _dot((zg * _sigmoid(zg) * zu).astype(bf16), wo_ref[...])
        f_ref[...] = fv.astype(bf16)
        xo_ref[...] = xv + 0.5 * mod_ref[r_gate:r_gate + 1, :] * fv

    row = lambda i: (i, 0)
    return _pcall(
        body, (x, mod, g, wi_t, wo), name=name, comm=comm,
        grid=(S // tm,),
        in_specs=[pl.BlockSpec((tm, D), row), _full(mod.shape), _full(g.shape), _resident(wi_t.shape), _resident(wo.shape)],
        out_specs=[pl.BlockSpec((tm, D), row), pl.BlockSpec((tm, D), row), pl.BlockSpec((tm, 2 * F), row),
                   pl.BlockSpec((tm, D), row)],
        out_shape=[jax.ShapeDtypeStruct((S, D), f32), jax.ShapeDtypeStruct((S, D), bf16),
                   jax.ShapeDtypeStruct((S, 2 * F), bf16), jax.ShapeDtypeStruct((S, D), bf16)],
    )


def ffn_bwd_hidden(dy, z, mod, wo_t, r_gate, tm, name, comm=None):
    S = dy.shape[0]
    nt = S // tm
    halves = 2
    fc = F // halves

    def body(dy_ref, z_ref, mod_ref, wo_ref, dz_ref, dwo_ref, acc_s):
        i = pl.program_id(0)

        @pl.when(i == 0)
        def _():
            acc_s[...] = jnp.zeros_like(acc_s)

        df = (0.5 * mod_ref[r_gate:r_gate + 1, :] * dy_ref[...]).astype(bf16)
        for c in range(halves):
            lo, hi = c * fc, (c + 1) * fc
            zgv = z_ref[:, lo:hi].astype(f32)
            zuv = z_ref[:, F + lo:F + hi].astype(f32)
            s = _sigmoid(zgv)
            sil = zgv * s
            acc_s[lo:hi, :] += _dg((sil * zuv).astype(bf16), df, TN)
            da = _dot(df, wo_ref[:, lo:hi])
            dz_ref[:, F + lo:F + hi] = (da * sil).astype(bf16)
            dz_ref[:, lo:hi] = (da * zuv * (s * (1.0 + zgv * (1.0 - s)))).astype(bf16)

        @pl.when(i == nt - 1)
        def _():
            dwo_ref[...] = acc_s[...].astype(bf16)

    row = lambda i: (i, 0)
    return _pcall(
        body, (dy, z, mod, wo_t), name=name, comm=comm,
        grid=(nt,),
        in_specs=[pl.BlockSpec((tm, D), row), pl.BlockSpec((tm, 2 * F), row), _full(mod.shape), _resident(wo_t.shape)],
        out_specs=[pl.BlockSpec((tm, 2 * F), row), _full((F, D))],
        out_shape=[jax.ShapeDtypeStruct((S, 2 * F), bf16), jax.ShapeDtypeStruct((F, D), bf16)],
        scratch_shapes=[pltpu.VMEM((F, D), f32)],
    )


def ffn_bwd_input(x, dy, dz, fo, mod, g, wi_t, rows, tm, name, comm=None):
    S = x.shape[0]
    r_shift, r_scale, r_gate = rows

    def body(x_ref, dy_ref, dz_ref, f_ref, mod_ref, g_ref, wi_ref, dx_ref, red_ref):
        @pl.when(pl.program_id(0) == 0)
        def _():
            red_ref[...] = jnp.zeros_like(red_ref)

        dh = _dot(dz_ref[...], wi_ref[...])
        dyv = dy_ref[...]
        xh, rstd = _rms_parts(x_ref[...])
        gv = g_ref[...]
        n = xh * gv
        dn = dh * (1.0 + mod_ref[r_scale:r_scale + 1, :])
        red_ref[0:1, :] += _rowsum(dh)
        red_ref[1:2, :] += _rowsum(dh * n)
        red_ref[2:3, :] += _rowsum(0.5 * f_ref[...].astype(f32) * dyv)
        red_ref[3:4, :] += _rowsum(dn * xh)
        dx_ref[...] = dyv + _rms_bwd(dn * gv, xh, rstd)

    row = lambda i: (i, 0)
    return _pcall(
        body, (x, dy, dz, fo, mod, g, wi_t), name=name, comm=comm,
        grid=(S // tm,),
        in_specs=[pl.BlockSpec((tm, D), row), pl.BlockSpec((tm, D), row), pl.BlockSpec((tm, 2 * F), row),
                  pl.BlockSpec((tm, D), row), _full(mod.shape), _full(g.shape), _resident(wi_t.shape)],
        out_specs=[pl.BlockSpec((tm, D), row), _full((8, D))],
        out_shape=[jax.ShapeDtypeStruct((S, D), f32), jax.ShapeDtypeStruct((8, D), f32)],
    )


def matmul_tn(a, b, M, N, bm, bn, bk, name, a_col_block=0, out_dtype=f32):
    S = b.shape[0]
    nk = S // bk

    def body(a_ref, b_ref, o_ref, acc_s):
        k = pl.program_id(2)

        @pl.when(k == 0)
        def _():
            acc_s[...] = jnp.zeros_like(acc_s)

        acc_s[...] += _dg(a_ref[...].astype(bf16), b_ref[...].astype(bf16), TN)

        @pl.when(k == nk - 1)
        def _():
            o_ref[...] = acc_s[...].astype(out_dtype)

    return pl.pallas_call(
        body, name=name,
        grid=(M // bm, N // bn, nk),
        in_specs=[
            pl.BlockSpec((bk, bm), lambda i, j, k: (k, i + a_col_block)),
            pl.BlockSpec((bk, bn), lambda i, j, k: (k, j)),
        ],
        out_specs=pl.BlockSpec((bm, bn), lambda i, j, k: (i, j)),
        out_shape=jax.ShapeDtypeStruct((M, N), out_dtype),
        scratch_shapes=[pltpu.VMEM((bm, bn), f32)],
        compiler_params=_cp(3),
    )(a, b)


def dwi_pieces(h, dz, bk, name, comm=None):
    S = h.shape[0]
    nk = S // bk

    def body(h_ref, dz_ref, o_ref, acc_s):
        k = pl.program_id(1)

        @pl.when(k == 0)
        def _():
            acc_s[...] = jnp.zeros_like(acc_s)

        acc_s[...] += _dg(dz_ref[...], h_ref[...], TN)

        @pl.when(k == nk - 1)
        def _():
            o_ref[...] = acc_s[...].astype(bf16)

    (out,), comm_outs = _pcall(
        body, (h, dz), name=name, comm=comm,
        grid=(2, nk),
        in_specs=[pl.BlockSpec((bk, D), lambda half, k: (k, 0)), pl.BlockSpec((bk, F), lambda half, k: (k, half))],
        out_specs=[pl.BlockSpec((F, D), lambda half, k: (half, 0))],
        out_shape=[jax.ShapeDtypeStruct((2 * F, D), bf16)],
        scratch_shapes=[pltpu.VMEM((F, D), f32)],
    )
    return out.reshape(N_DEV, FS, D), comm_outs


def mixin_fwd(x1, mod, g, win, wgu, bgate, tm, name, comm=None):
    S = x1.shape[0]

    def body(x_ref, mod_ref, g_ref, win_ref, wgu_ref, bg_ref, z_ref, la_ref):
        xh, _ = _rms_parts(x_ref[...])
        hv = xh * g_ref[...] * (1.0 + mod_ref[4:5, :]) + mod_ref[3:4, :]
        z = _dg(hv.astype(bf16), win_ref[...], NT).astype(bf16)
        z_ref[...] = z
        pre = _dot(z[:, DINP - 128:], wgu_ref[...]) + bg_ref[...]
        la_ref[...] = (jnp.minimum(pre, 0.0) - jnp.log(1.0 + jnp.exp(-jnp.abs(pre)))) * (1.0 / TAU)

    return _pcall(
        body, (x1, mod, g, win, wgu, bgate), name=name, comm=comm,
        grid=(S // tm,),
        in_specs=[pl.BlockSpec((tm, D), lambda i: (i, 0)), _full(mod.shape), _full(g.shape),
                  _full(win.shape), _full(wgu.shape), _full(bgate.shape)],
        out_specs=[pl.BlockSpec((tm, DINP), lambda i: (i, 0)), pl.BlockSpec((tm, DQK), lambda i: (i, 0))],
        out_shape=[jax.ShapeDtypeStruct((S, DINP), bf16), jax.ShapeDtypeStruct((S, DQK), f32)],
    )


def mixin_bwd(x1, dres, dzab, dq, dk, dv, dr, dpre, mod, g, win, wgu, tm, name, comm=None):
    S = x1.shape[0]
    nt = S // tm

    def body(x_ref, dres_ref, dzab_ref, dq_ref, dk_ref, dv_ref, dr_ref, dpre_ref, mod_ref, g_ref, win_ref, wgu_ref,
             dx_ref, red_ref, dw_ref, acc_s):
        i = pl.program_id(0)

        @pl.when(i == 0)
        def _():
            red_ref[...] = jnp.zeros_like(red_ref)
            acc_s[...] = jnp.zeros_like(acc_s)

        dglr = _dg(dpre_ref[...], wgu_ref[...], NT).astype(bf16)
        dz = jnp.concatenate([dzab_ref[...], dq_ref[...], dk_ref[...], dv_ref[...], dr_ref[...], dglr], axis=1)
        dh = _dot(dz, win_ref[...])
        xh, rstd = _rms_parts(x_ref[...])
        gv = g_ref[...]
        n = xh * gv
        sc = 1.0 + mod_ref[4:5, :]
        acc_s[...] += _dg(dz, (n * sc + mod_ref[3:4, :]).astype(bf16), TN)
        dn = dh * sc
        red_ref[0:1, :] += _rowsum(dh)
        red_ref[1:2, :] += _rowsum(dh * n)
        red_ref[2:3, :] += _rowsum(dn * xh)
        dx_ref[...] = dres_ref[...] + _rms_bwd(dn * gv, xh, rstd)

        @pl.when(i == nt - 1)
        def _():
            dw_ref[...] = acc_s[...].astype(bf16)

    row = lambda i: (i, 0)
    return _pcall(
        body, (x1, dres, dzab, dq, dk, dv, dr, dpre, mod, g, win, wgu), name=name, comm=comm,
        grid=(nt,),
        in_specs=[pl.BlockSpec((tm, D), row), pl.BlockSpec((tm, D), row),
                  pl.BlockSpec((tm, 2 * DC), row), pl.BlockSpec((tm, DQK), row), pl.BlockSpec((tm, DQK), row),
                  pl.BlockSpec((tm, DG), row), pl.BlockSpec((tm, DG), row), pl.BlockSpec((tm, DQK), row),
                  _full(mod.shape), _full(g.shape), _resident(win.shape), _full(wgu.shape)],
        out_specs=[pl.BlockSpec((tm, D), row), _full((8, D)), _full((DINP, D))],
        out_shape=[jax.ShapeDtypeStruct((S, D), f32), jax.ShapeDtypeStruct((8, D), f32),
                   jax.ShapeDtypeStruct((DINP, D), bf16)],
        scratch_shapes=[pltpu.VMEM((DINP, D), f32)],
    )


def _glu(zab):
    zab = zab.astype(f32)
    return zab[:, :DC] * _sigmoid(zab[:, DC:])


def _shift_copies(src_s, dst_s, tc):
    n = tc + HALO - SUBLANES
    for b in range(1, SUBLANES):
        dst_s[b, 0:n, :] = src_s[b:b + n, :]


def _shifted(src_s, dst_s, o, tc):
    b = o % SUBLANES
    a = o - b
    return src_s[a:a + tc, :] if b == 0 else dst_s[b, a:a + tc, :]


def _conv_fwd_prepare(first, zc_ref, zp_ref, u_s, us_s, tc):
    up = _glu(zp_ref[...])
    u_s[0:HALO, :] = jnp.where(first, 0.0, up)
    u_s[HALO:HALO + tc, :] = _glu(zc_ref[...])
    _shift_copies(u_s, us_s, tc)


def _conv_fwd_rows(r0, n, w_ref, cp_ref, y_ref, yc_ref, u_s, us_s):
    for r in range(r0, r0 + n, CONV_ROWS):
        acc = _shifted(u_s, us_s, HALO - (CW - 1) + r, CONV_ROWS) * w_ref[0:1, :]
        for w in range(1, CW):
            acc = acc + _shifted(u_s, us_s, HALO - (CW - 1) + w + r, CONV_ROWS) * w_ref[w:w + 1, :]
        y = acc + cp_ref[0:1, :]
        y_ref[r:r + CONV_ROWS, :] = y
        yc = y - jnp.mean(y, axis=-1, keepdims=True)
        yl = yc * lax.rsqrt(jnp.mean(yc * yc, axis=-1, keepdims=True) + EPS) * cp_ref[1:2, :] + cp_ref[2:3, :]
        yc_ref[r:r + CONV_ROWS, :] = (yl * _sigmoid(yl)).astype(bf16)


def _conv_bwd_prepare(first, last, zc_ref, zp_ref, y_ref, yn_ref, d_ref, dn_ref, cp_ref, red_ref, u_s, dy_s, us_s, dys_s, tc):
    gl = cp_ref[1:2, :]
    bl = cp_ref[2:3, :]

    def ln_bwd(yv, dv):
        yc = yv - jnp.mean(yv, axis=-1, keepdims=True)
        rstd = lax.rsqrt(jnp.mean(yc * yc, axis=-1, keepdims=True) + EPS)
        yh = yc * rstd
        yl = yh * gl + bl
        s = _sigmoid(yl)
        dyl = dv * (s * (1.0 + yl * (1.0 - s)))
        dyh = dyl * gl
        dyv = rstd * (dyh - jnp.mean(dyh, axis=-1, keepdims=True) - yh * jnp.mean(dyh * yh, axis=-1, keepdims=True))
        return dyv, dyl, yh

    dy_c, dyl_c, yh_c = ln_bwd(y_ref[...], d_ref[...])
    dy_n, _, _ = ln_bwd(yn_ref[...], dn_ref[...])
    dy_s[0:tc, :] = dy_c
    dy_s[tc:tc + HALO, :] = jnp.where(last, 0.0, dy_n)
    u_s[0:HALO, :] = jnp.where(first, 0.0, _glu(zp_ref[...]))
    u_s[HALO:HALO + tc, :] = _glu(zc_ref[...])
    _shift_copies(u_s, us_s, tc)
    _shift_copies(dy_s, dys_s, tc)
    red_ref[32:33, :] += _rowsum(dy_c)
    red_ref[33:34, :] += _rowsum(dyl_c * yh_c)
    red_ref[34:35, :] += _rowsum(dyl_c)


def _conv_bwd_input_rows(r0, n, zc_ref, w_ref, dz_ref, dy_s, dys_s):
    for r in range(r0, r0 + n, CONV_ROWS):
        du = _shifted(dy_s, dys_s, CW - 1 + r, CONV_ROWS) * w_ref[0:1, :]
        for w in range(1, CW):
            du = du + _shifted(dy_s, dys_s, CW - 1 - w + r, CONV_ROWS) * w_ref[w:w + 1, :]
        zc = zc_ref[r:r + CONV_ROWS, :].astype(f32)
        av = zc[:, :DC]
        sb = _sigmoid(zc[:, DC:])
        dz_ref[r:r + CONV_ROWS, :] = jnp.concatenate([du * sb, du * av * sb * (1.0 - sb)], axis=1).astype(dz_ref.dtype)


def _conv_bwd_taps(w0, w1, red_ref, u_s, us_s, dy_s, tc):
    for w in range(w0, w1):
        part = None
        for r in range(0, tc, CONV_ROWS):
            prod = _shifted(u_s, us_s, HALO - (CW - 1) + w + r, CONV_ROWS) * dy_s[r:r + CONV_ROWS, :]
            fold = jnp.sum(prod.reshape(CONV_ROWS // SUBLANES, SUBLANES, DC), axis=0)
            part = fold if part is None else part + fold
        red_ref[w:w + 1, :] += _rowsum(part)


def conv_bwd(z, y, dyc, wdw, cpar, tc, name):
    S = z.shape[0]
    nb = tc // HALO
    nt = S // tc
    last_halo = S // HALO - 1

    def body(zc_ref, zp_ref, y_ref, yn_ref, d_ref, dn_ref, w_ref, cp_ref, dz_ref, red_ref, u_s, dy_s, us_s, dys_s):
        i = pl.program_id(0)

        @pl.when(i == 0)
        def _():
            red_ref[...] = jnp.zeros_like(red_ref)

        _conv_bwd_prepare(i == 0, i == nt - 1, zc_ref, zp_ref, y_ref, yn_ref, d_ref, dn_ref, cp_ref, red_ref,
                          u_s, dy_s, us_s, dys_s, tc)
        _conv_bwd_input_rows(0, tc, zc_ref, w_ref, dz_ref, dy_s, dys_s)
        _conv_bwd_taps(0, CW, red_ref, u_s, us_s, dy_s, tc)

    cur = lambda i: (i, 0)
    nxt = lambda i: (jnp.minimum((i + 1) * nb, last_halo), 0)
    return pl.pallas_call(
        body, name=name,
        grid=(nt,),
        in_specs=[pl.BlockSpec((tc, 2 * DC), cur),
                  pl.BlockSpec((HALO, 2 * DC), lambda i: (jnp.maximum(i * nb - 1, 0), 0)),
                  pl.BlockSpec((tc, DC), cur), pl.BlockSpec((HALO, DC), nxt),
                  pl.BlockSpec((tc, DC), cur), pl.BlockSpec((HALO, DC), nxt),
                  _full(wdw.shape), _full(cpar.shape)],
        out_specs=[pl.BlockSpec((tc, 2 * DC), cur), _full((40, DC))],
        out_shape=[jax.ShapeDtypeStruct((S, 2 * DC), bf16), jax.ShapeDtypeStruct((40, DC), f32)],
        scratch_shapes=[pltpu.VMEM((HALO + tc, DC), f32), pltpu.VMEM((tc + HALO, DC), f32),
                        pltpu.VMEM((SUBLANES, HALO + tc, DC), f32), pltpu.VMEM((SUBLANES, HALO + tc, DC), f32)],
        compiler_params=_cp(1),
    )(z, z, y, y, dyc, dyc, wdw, cpar)


def _gla_consts():
    r = lax.broadcasted_iota(jnp.int32, (CH, CH), 0)
    c = lax.broadcasted_iota(jnp.int32, (CH, CH), 1)
    tril = r >= c
    lane = lax.broadcasted_iota(jnp.int32, (CH, DQK), 1)
    masks = [(lane >= h * DK) & (lane < (h + 1) * DK) for h in range(NH)]
    r4 = lax.broadcasted_iota(jnp.int32, (DQK, DQK), 0)
    c4 = lax.broadcasted_iota(jnp.int32, (DQK, DQK), 1)
    eye4 = (r4 == c4).astype(f32)
    rs = lax.broadcasted_iota(jnp.int32, (DQK, CH), 0) & (CH - 1)
    tril4 = rs >= lax.broadcasted_iota(jnp.int32, (DQK, CH), 1)
    return tril, tril4, masks, eye4


def _stack(xv, masks):
    return jnp.concatenate([jnp.where(m, xv, 0.0) for m in masks], axis=0)


def _unstack(rv, masks):
    out = jnp.where(masks[0], rv[0:CH, :], 0.0)
    for h in range(1, NH):
        out = out + jnp.where(masks[h], rv[h * CH:(h + 1) * CH, :], 0.0)
    return out


def _vstack(xv):
    return jnp.concatenate([xv[:, h * DV:(h + 1) * DV] for h in range(NH)], axis=0)


def _vunstack(xv):
    return jnp.concatenate([xv[h * CH:(h + 1) * CH, :] for h in range(NH)], axis=1)


def _gla_chunk_fwd(lac, qc, kc, vc, s_all, tril, masks, tril4):
    qc, kc = qc.astype(f32), kc.astype(f32)
    lmat = tril.astype(f32)
    bc = jnp.dot(lmat, lac, preferred_element_type=f32, precision=HIGHEST)
    bend = bc[CH - 1:CH, :]
    eb = jnp.exp(bc)
    enb = jnp.exp(-bc)
    ed = jnp.exp(bend - bc)
    qh = qc * (DK ** -0.5)
    qf = qh * eb
    qn = qh * enb
    kn = kc * enb
    kp = kc * eb
    kd = kc * ed
    qf_s = _stack(qf, masks).astype(bf16)
    qn_s = _stack(qn, masks).astype(bf16)
    kn_b = kn.astype(bf16)
    kp_b = kp.astype(bf16)
    attf = _dg(qf_s, kn_b, NT)
    attb = _dg(qn_s, kp_b, NT)
    a_s = jnp.where(tril4, attf, attb)
    a_b = a_s.astype(bf16)
    v_b = vc.astype(bf16)
    intra = jnp.concatenate(
        [_dot(a_b[h * CH:(h + 1) * CH, :], v_b[:, h * DV:(h + 1) * DV]) for h in range(NH)], axis=0)
    o_s = intra + _dot(qf_s, s_all.astype(bf16))
    return dict(bc=bc, bend=bend, eb=eb, enb=enb, ed=ed, qf=qf, qn=qn, kn=kn, kp=kp, kd=kd,
                qf_s=qf_s, qn_s=qn_s, kn_b=kn_b, kp_b=kp_b, a_b=a_b, v_b=v_b, o_s=o_s)


def _col_from_row(row, eye4):
    return jnp.sum(eye4 * row, axis=1, keepdims=True)


def _row_from_col(col, eye4):
    return jnp.sum(eye4 * col, axis=0, keepdims=True)


def _gla_fwd_chunk(c, consts, q_ref, k_ref, v_ref, r_ref, la_ref, gn_ref, yg_ref, sp_ref, st):
    tril, tril4, masks, eye4 = consts
    r0, s0 = c * CH, c * DQK
    s_all = st[...]
    sp_ref[s0:s0 + DQK, :] = s_all
    vc = v_ref[r0:r0 + CH, :]
    t = _gla_chunk_fwd(la_ref[r0:r0 + CH, :], q_ref[r0:r0 + CH, :], k_ref[r0:r0 + CH, :], vc,
                       s_all, tril, masks, tril4)
    u_all = _dg(_stack(t["kd"], masks).astype(bf16), _vstack(vc).astype(bf16), TN)
    st[...] = _col_from_row(jnp.exp(t["bend"]), eye4) * s_all + u_all
    o_s = t["o_s"]
    on = o_s * lax.rsqrt(jnp.mean(o_s * o_s, axis=-1, keepdims=True) + EPS) * gn_ref[...]
    rc = r_ref[r0:r0 + CH, :].astype(f32)
    yg_ref[r0:r0 + CH, :] = (_vunstack(on) * (rc * _sigmoid(rc))).astype(bf16)


def mixer_core_fwd(z, la, gn_s, wdw, cpar, t, name, comm=None):
    S = z.shape[0]
    nb = t // HALO
    nc = t // CH

    def body(zc_ref, zp_ref, q_ref, k_ref, v_ref, r_ref, la_ref, gn_ref, w_ref, cp_ref,
             y_ref, yc_ref, yg_ref, sp_ref, st, u_s, us_s):
        i = pl.program_id(0)

        @pl.when(i == 0)
        def _():
            st[...] = jnp.zeros_like(st)

        _conv_fwd_prepare(i == 0, zc_ref, zp_ref, u_s, us_s, t)
        consts = _gla_consts()
        for c in range(nc):
            _conv_fwd_rows(c * CH, CH, w_ref, cp_ref, y_ref, yc_ref, u_s, us_s)
            _gla_fwd_chunk(c, consts, q_ref, k_ref, v_ref, r_ref, la_ref, gn_ref, yg_ref, sp_ref, st)

    row = lambda i: (i, 0)
    return _pcall(
        body, (z, z, z, z, z, z, la, gn_s, wdw, cpar), name=name, comm=comm,
        grid=(S // t,),
        in_specs=[pl.BlockSpec((t, 2 * DC), row),
                  pl.BlockSpec((HALO, 2 * DC), lambda i: (jnp.maximum(i * nb - 1, 0), 0)),
                  pl.BlockSpec((t, DQK), lambda i: (i, 4)), pl.BlockSpec((t, DQK), lambda i: (i, 5)),
                  pl.BlockSpec((t, DG), lambda i: (i, 3)), pl.BlockSpec((t, DG), lambda i: (i, 4)),
                  pl.BlockSpec((t, DQK), row), _full(gn_s.shape), _full(wdw.shape), _full(cpar.shape)],
        out_specs=[pl.BlockSpec((t, DC), row), pl.BlockSpec((t, DC), row), pl.BlockSpec((t, DG), row),
                   pl.BlockSpec((nc * DQK, DV), row)],
        out_shape=[jax.ShapeDtypeStruct((S, DC), f32), jax.ShapeDtypeStruct((S, DC), bf16),
                   jax.ShapeDtypeStruct((S, DG), bf16), jax.ShapeDtypeStruct((S // CH * DQK, DV), f32)],
        scratch_shapes=[pltpu.VMEM((DQK, DV), f32), pltpu.VMEM((HALO + t, DC), f32),
                        pltpu.VMEM((SUBLANES, HALO + t, DC), f32)],
    )


def _gla_bwd_chunk(c, consts, umat, last_row, q_ref, k_ref, v_ref, r_ref, la_ref, sp_ref, dy_ref, gn_ref,
                   dq_ref, dk_ref, dv_ref, dr_ref, dpre_ref, redg_ref, redb_ref, gs):
    tril, tril4, masks, eye4 = consts
    r0, s0 = c * CH, c * DQK
    rows = slice(r0, r0 + CH)
    s_all = sp_ref[s0:s0 + DQK, :]
    lac = la_ref[rows, :]
    vc = v_ref[rows, :]
    rc = r_ref[rows, :].astype(f32)
    t = _gla_chunk_fwd(lac, q_ref[rows, :], k_ref[rows, :], vc, s_all, tril, masks, tril4)
    g_all = gs[...]
    g_b = g_all.astype(bf16)
    s_b = s_all.astype(bf16)
    o_s = t["o_s"]
    rstd = lax.rsqrt(jnp.mean(o_s * o_s, axis=-1, keepdims=True) + EPS)
    oh = o_s * rstd
    gnv = gn_ref[...]
    sr = _sigmoid(rc)
    dyv = dy_ref[rows, :]
    dr_ref[rows, :] = (dyv * _vunstack(oh * gnv) * (sr * (1.0 + rc * (1.0 - sr)))).astype(dr_ref.dtype)
    don = _vstack(dyv * (rc * sr))
    redg_ref[...] += don * oh
    doh = don * gnv
    do_s = rstd * (doh - oh * jnp.mean(doh * oh, axis=-1, keepdims=True))
    do_b = do_s.astype(bf16)
    v_b = t["v_b"]
    vst_b = _vstack(vc).astype(bf16)
    kd_s = _stack(t["kd"], masks).astype(bf16)
    da_s = jnp.concatenate(
        [_dg(do_b[h * CH:(h + 1) * CH, :], v_b[:, h * DV:(h + 1) * DV], NT) for h in range(NH)], axis=0)
    a_b = t["a_b"]
    dv_s = jnp.concatenate(
        [_dg(a_b[h * CH:(h + 1) * CH, :], do_b[h * CH:(h + 1) * CH, :], TN) for h in range(NH)], axis=0)
    dv_s = dv_s + _dot(kd_s, g_b)
    dv_ref[rows, :] = _vunstack(dv_s).astype(dv_ref.dtype)
    gend = jnp.exp(t["bend"])
    gcol = _col_from_row(gend, eye4)
    gs[...] = gcol * g_all + _dg(t["qf_s"], do_b, TN)
    dgcol = jnp.sum(g_all * s_all, axis=1, keepdims=True)
    dbend = _row_from_col(dgcol * gcol, eye4)
    dkd = _unstack(_dg(vst_b, g_b, NT), masks)
    daf = jnp.where(tril4, da_s, 0.0).astype(bf16)
    dab = jnp.where(tril4, 0.0, da_s).astype(bf16)
    dqf = _unstack(_dot(daf, t["kn_b"]) + _dg(do_b, s_b, NT), masks)
    dqn = _unstack(_dot(dab, t["kp_b"]), masks)
    dkn = _dg(daf, t["qf_s"], TN)
    dkp = _dg(dab, t["qn_s"], TN)
    dq_ref[rows, :] = ((dqf * t["eb"] + dqn * t["enb"]) * (DK ** -0.5)).astype(dq_ref.dtype)
    dk_ref[rows, :] = (dkn * t["enb"] + dkp * t["eb"] + dkd * t["ed"]).astype(dk_ref.dtype)
    dkd_kd = dkd * t["kd"]
    dbc = dqf * t["qf"] - dqn * t["qn"] - dkn * t["kn"] + dkp * t["kp"] - dkd_kd
    dbc = dbc + jnp.where(last_row, _rowsum(dkd_kd) + dbend, 0.0)
    dla = jnp.dot(umat, dbc, preferred_element_type=f32, precision=HIGHEST)
    dpre = dla * (1.0 / TAU) * (1.0 - jnp.exp(TAU * lac))
    dpre_ref[rows, :] = dpre.astype(dpre_ref.dtype)
    redb_ref[...] += dpre


def gla_bwd(z, la, sprev, dyg, gn_s, t, name, comm=None):
    S = z.shape[0]
    nc = t // CH
    nt = S // t

    def body(q_ref, k_ref, v_ref, r_ref, la_ref, sp_ref, dy_ref, gn_ref,
             dq_ref, dk_ref, dv_ref, dr_ref, dpre_ref, redg_ref, redb_ref, gs):
        @pl.when(pl.program_id(0) == 0)
        def _():
            gs[...] = jnp.zeros_like(gs)
            redg_ref[...] = jnp.zeros_like(redg_ref)
            redb_ref[...] = jnp.zeros_like(redb_ref)

        consts = _gla_consts()
        umat = (lax.broadcasted_iota(jnp.int32, (CH, CH), 0) <= lax.broadcasted_iota(jnp.int32, (CH, CH), 1)).astype(f32)
        last_row = lax.broadcasted_iota(jnp.int32, (CH, DQK), 0) == CH - 1
        for c in reversed(range(nc)):
            _gla_bwd_chunk(c, consts, umat, last_row, q_ref, k_ref, v_ref, r_ref, la_ref, sp_ref, dy_ref, gn_ref,
                           dq_ref, dk_ref, dv_ref, dr_ref, dpre_ref, redg_ref, redb_ref, gs)

    rev = lambda col: (lambda i: (nt - 1 - i, col))
    return _pcall(
        body, (z, z, z, z, la, sprev, dyg, gn_s), name=name, comm=comm,
        grid=(nt,),
        in_specs=[pl.BlockSpec((t, DQK), rev(4)), pl.BlockSpec((t, DQK), rev(5)),
                  pl.BlockSpec((t, DG), rev(3)), pl.BlockSpec((t, DG), rev(4)),
                  pl.BlockSpec((t, DQK), rev(0)), pl.BlockSpec((nc * DQK, DV), rev(0)),
                  pl.BlockSpec((t, DG), rev(0)), _full(gn_s.shape)],
        out_specs=[pl.BlockSpec((t, DQK), rev(0)), pl.BlockSpec((t, DQK), rev(0)),
                   pl.BlockSpec((t, DG), rev(0)), pl.BlockSpec((t, DG), rev(0)), pl.BlockSpec((t, DQK), rev(0)),
                   _full((DQK, DV)), _full((CH, DQK))],
        out_shape=[jax.ShapeDtypeStruct((S, DQK), bf16), jax.ShapeDtypeStruct((S, DQK), bf16),
                   jax.ShapeDtypeStruct((S, DG), bf16), jax.ShapeDtypeStruct((S, DG), bf16), jax.ShapeDtypeStruct((S, DQK), bf16),
                   jax.ShapeDtypeStruct((DQK, DV), f32), jax.ShapeDtypeStruct((CH, DQK), f32)],
        scratch_shapes=[pltpu.VMEM((DQK, DV), f32)],
    )


def mixout_fwd(x1, yc, yg, mod, wout, tm, name):
    S = x1.shape[0]

    def body(x_ref, yc_ref, yg_ref, mod_ref, w_ref, xo_ref):
        mixo = _dot(yc_ref[...], w_ref[0:DC, :]) + _dot(yg_ref[...], w_ref[DC:DC + DG, :])
        xo_ref[...] = x_ref[...] + mod_ref[5:6, :] * mixo

    row = lambda i: (i, 0)
    return pl.pallas_call(
        body, name=name,
        grid=(S // tm,),
        in_specs=[pl.BlockSpec((tm, D), row), pl.BlockSpec((tm, DC), row), pl.BlockSpec((tm, DG), row),
                  _full(mod.shape), _full(wout.shape)],
        out_specs=pl.BlockSpec((tm, D), row),
        out_shape=jax.ShapeDtypeStruct((S, D), f32),
        compiler_params=_cp(1),
    )(x1, yc, yg, mod, wout)


def mixout_bwd(dx2, yc, yg, mod, wout, tm, name):
    S = dx2.shape[0]
    nt = S // tm

    def body(dx_ref, yc_ref, yg_ref, mod_ref, w_ref, dyc_ref, dyg_ref, red_ref, dw_ref, acc_s):
        i = pl.program_id(0)

        @pl.when(i == 0)
        def _():
            red_ref[...] = jnp.zeros_like(red_ref)
            acc_s[...] = jnp.zeros_like(acc_s)

        dxv = dx_ref[...]
        ycat = jnp.concatenate([yc_ref[...], yg_ref[...]], axis=1)
        mixo = _dot(ycat, w_ref[...])
        red_ref[0:1, :] += _rowsum(dxv * mixo)
        dm = (mod_ref[5:6, :] * dxv).astype(bf16)
        acc_s[...] += _dg(ycat, dm, TN)
        dycat = _dg(dm, w_ref[...], NT)
        dyc_ref[...] = dycat[:, :DC]
        dyg_ref[...] = dycat[:, DC:]

        @pl.when(i == nt - 1)
        def _():
            dw_ref[...] = acc_s[...].astype(bf16)

    row = lambda i: (i, 0)
    return pl.pallas_call(
        body, name=name,
        grid=(nt,),
        in_specs=[pl.BlockSpec((tm, D), row), pl.BlockSpec((tm, DC), row), pl.BlockSpec((tm, DG), row),
                  _full(mod.shape), _full(wout.shape)],
        out_specs=[pl.BlockSpec((tm, DC), row), pl.BlockSpec((tm, DG), row), _full((8, D)), _full((D, D))],
        out_shape=[jax.ShapeDtypeStruct((S, DC), f32), jax.ShapeDtypeStruct((S, DG), f32),
                   jax.ShapeDtypeStruct((8, D), f32), jax.ShapeDtypeStruct((D, D), bf16)],
        scratch_shapes=[pltpu.VMEM((D, D), f32)],
        compiler_params=_cp(1),
    )(dx2, yc, yg, mod, wout)


def final_fwd_bwd(x, tgt, fmod, g, tm, name):
    S = x.shape[0]

    def body(x_ref, t_ref, fm_ref, g_ref, dx_ref, red_ref):
        @pl.when(pl.program_id(0) == 0)
        def _():
            red_ref[...] = jnp.zeros_like(red_ref)

        xh, rstd = _rms_parts(x_ref[...])
        gv = g_ref[...]
        n = xh * gv
        sc = 1.0 + fm_ref[1:2, :]
        e = n * sc + fm_ref[0:1, :] - t_ref[...]
        red_ref[0:1, :] += _rowsum(e * e) * (0.5 / D)
        dy = e * (1.0 / D)
        dn = dy * sc
        red_ref[1:2, :] += _rowsum(dy)
        red_ref[2:3, :] += _rowsum(dy * n)
        red_ref[3:4, :] += _rowsum(dn * xh)
        dx_ref[...] = _rms_bwd(dn * gv, xh, rstd)

    row = lambda i: (i, 0)
    return pl.pallas_call(
        body, name=name,
        grid=(S // tm,),
        in_specs=[pl.BlockSpec((tm, D), row), pl.BlockSpec((tm, D), row), _full(fmod.shape), _full(g.shape)],
        out_specs=[pl.BlockSpec((tm, D), row), _full((8, D))],
        out_shape=[jax.ShapeDtypeStruct((S, D), f32), jax.ShapeDtypeStruct((8, D), f32)],
        compiler_params=_cp(1),
    )(x, tgt, fmod, g)


def ada_fwd(c_all, w, b, name):
    n = w.shape[1]

    def body(c_ref, w_ref, b_ref, o_ref):
        cv = c_ref[...]
        o_ref[...] = jnp.dot(cv * _sigmoid(cv), w_ref[...], preferred_element_type=f32, precision=HIGHEST) + b_ref[...]

    return pl.pallas_call(
        body, name=name,
        in_specs=[_full(c_all.shape), _full(w.shape), _full(b.shape)],
        out_specs=_full((N_DEV, n)),
        out_shape=jax.ShapeDtypeStruct((N_DEV, n), f32),
        grid=(1,),
        compiler_params=_cp(1),
    )(c_all, w, b)


def ada_wgrad(c_all_t, dm, name):
    n = dm.shape[1]

    def body(c_ref, d_ref, o_ref):
        cv = c_ref[...]
        o_ref[...] = jnp.dot(cv * _sigmoid(cv), d_ref[...], preferred_element_type=f32, precision=HIGHEST)

    return pl.pallas_call(
        body, name=name,
        in_specs=[_full(c_all_t.shape), _full(dm.shape)],
        out_specs=_full((D, n)),
        out_shape=jax.ShapeDtypeStruct((D, n), f32),
        grid=(1,),
        compiler_params=_cp(1),
    )(c_all_t, dm)


def _adam_math(gv, wv, mv, vv):
    m = ADAM_B1 * mv + (1.0 - ADAM_B1) * gv
    v = ADAM_B2 * vv + (1.0 - ADAM_B2) * (gv * gv)
    m_hat = m / (1.0 - ADAM_B1 ** ADAM_STEP)
    v_hat = v / (1.0 - ADAM_B2 ** ADAM_STEP)
    delta = -ADAM_LR * (m_hat / (jnp.sqrt(v_hat) + ADAM_EPS) + ADAM_WD * wv)
    return delta, m, v


def adam_parts(parts, w, m, v, tr, name, comm=None):
    L, R, C = w.shape
    nt = R // tr

    def body(*refs):
        p_refs = refs[:L]
        w_ref, m_ref, v_ref, g_ref, d_ref, mo_ref, vo_ref = refs[L:]
        lyr = pl.program_id(0)
        for l in range(L):
            @pl.when(lyr == l)
            def _(p_ref=p_refs[l]):
                gv = p_ref[0].astype(f32)
                for k in range(1, N_DEV):
                    gv = gv + p_ref[k].astype(f32)
                g_ref[...] = gv
                d_ref[...], mo_ref[...], vo_ref[...] = _adam_math(gv, w_ref[...], m_ref[...], v_ref[...])

    def part_spec(l):
        return pl.BlockSpec((N_DEV, tr, C), lambda lyr, i: (0, jnp.where(lyr == l, i, jnp.where(lyr < l, 0, nt - 1)), 0))

    spec = pl.BlockSpec((None, tr, C), lambda lyr, i: (lyr, i, 0))
    shp = jax.ShapeDtypeStruct((L, R, C), f32)
    return _pcall(
        body, (*parts, w, m, v), name=name, comm=comm,
        grid=(L, nt),
        in_specs=[part_spec(l) for l in range(L)] + [spec, spec, spec],
        out_specs=[spec, spec, spec, spec],
        out_shape=[shp, shp, shp, shp],
    )


def adam_plain(gr, w, m, v, tr, name):
    R, C = w.shape

    def body(g_ref, w_ref, m_ref, v_ref, d_ref, mo_ref, vo_ref):
        d_ref[...], mo_ref[...], vo_ref[...] = _adam_math(g_ref[...], w_ref[...], m_ref[...], v_ref[...])

    spec = pl.BlockSpec((tr, C), lambda i: (i, 0))
    shp = jax.ShapeDtypeStruct((R, C), f32)
    return pl.pallas_call(
        body, name=name,
        grid=(R // tr,),
        in_specs=[spec, spec, spec, spec],
        out_specs=[spec, spec, spec],
        out_shape=[shp, shp, shp],
        compiler_params=_cp(1),
    )(gr, w, m, v)


def sum8(parts, name):
    _, R, C = parts.shape

    def body(p_ref, o_ref):
        acc = p_ref[0]
        for k in range(1, N_DEV):
            acc = acc + p_ref[k]
        o_ref[...] = acc

    return pl.pallas_call(
        body, name=name,
        grid=(1,),
        in_specs=[_full(parts.shape)],
        out_specs=_full((R, C)),
        out_shape=jax.ShapeDtypeStruct((R, C), f32),
        compiler_params=_cp(1),
    )(parts)


def _place():
    return lax.axis_index("x"), lax.axis_index("y"), lax.axis_index("c")


def _gather_steps(ins, outs, send_sems, recv_sems, local_sems, place):
    n = len(ins)
    x, y, c = place
    me, sibling = (x, y, c), (x, y, 1 - c)
    chips = [(1 - x, y), (x, 1 - y), (1 - x, 1 - y)]

    def slot(a, p):
        return outs[a].at[4 * p[0] + 2 * p[1] + p[2]]

    def copy(a, k, block, to, src=None):
        return pltpu.make_async_remote_copy(
            src_ref=slot(a, block) if src is None else src, dst_ref=slot(a, block),
            send_sem=send_sems.at[a * 7 + k], recv_sem=recv_sems.at[a * 7 + k],
            device_id=to, device_id_type=MESH)

    def mine():
        return [pltpu.make_async_copy(ins[a], slot(a, me), local_sems.at[a]) for a in range(n)]

    def first():
        cps = []
        for a in range(n):
            cps.append(copy(a, 0, me, sibling, src=ins[a]))
            cps += [copy(a, 1 + j, me, (*chip, c), src=ins[a]) for j, chip in enumerate(chips)]
        return cps

    def start():
        for cp in mine() + first():
            cp.start()

    def forward():
        for j, chip in enumerate(chips):
            for a in range(n):
                copy(a, 1 + j, (*chip, c), me).wait_recv()
                copy(a, 4 + j, (*chip, c), sibling).start()

    def finish():
        for a in range(n):
            copy(a, 0, sibling, me).wait_recv()
            for j, chip in enumerate(chips):
                copy(a, 4 + j, (*chip, 1 - c), me).wait_recv()
        for cp in first() + [copy(a, 4 + j, (*chip, c), sibling) for j, chip in enumerate(chips) for a in range(n)]:
            cp.wait_send()
        for cp in mine():
            cp.wait()

    return start, forward, finish


def _exchange_steps(ins, outs, send_sems, recv_sems, local_sems, place):
    n = len(ins)
    x, y, c = place
    me_i = 4 * x + 2 * y + c

    def mine():
        return [pltpu.make_async_copy(ins[a].at[me_i], outs[a].at[me_i], local_sems.at[a]) for a in range(n)]

    def copies(receiving):
        cps = []
        for k in range(1, N_DEV):
            px = 1 - x if (k >> 2) & 1 else x
            py = 1 - y if (k >> 1) & 1 else y
            pc = 1 - c if k & 1 else c
            p_i = 4 * px + 2 * py + pc
            for a in range(n):
                sem = a * 7 + k - 1
                cps.append(pltpu.make_async_remote_copy(
                    src_ref=ins[a].at[p_i], dst_ref=outs[a].at[p_i if receiving else me_i],
                    send_sem=send_sems.at[sem], recv_sem=recv_sems.at[sem],
                    device_id=(px, py, pc), device_id_type=MESH))
        return cps

    def start():
        for cp in mine() + copies(False):
            cp.start()

    def finish():
        for cp in copies(True):
            cp.wait_recv()
        for cp in copies(False):
            cp.wait_send()
        for cp in mine():
            cp.wait()

    return start, None, finish


_COMM_STEPS = {"gather": _gather_steps, "exchange": _exchange_steps}


def _comm_out_shapes(kind, arrs):
    if kind == "gather":
        return [jax.ShapeDtypeStruct((N_DEV,) + a.shape, a.dtype) for a in arrs]
    return [jax.ShapeDtypeStruct(a.shape, a.dtype) for a in arrs]


def _comm_sems(n):
    return [pltpu.SemaphoreType.DMA((7 * n,)), pltpu.SemaphoreType.DMA((7 * n,)), pltpu.SemaphoreType.DMA((n,))]


def _pcall(body, args, *, name, grid, in_specs, out_specs, out_shape, scratch_shapes=(), comm=None):
    in_specs, out_specs, out_shape = list(in_specs), list(out_specs), list(out_shape)
    scratch_shapes = list(scratch_shapes)
    cparams = _cp(len(grid))
    if comm is None:
        outs = pl.pallas_call(body, name=name, grid=grid, in_specs=in_specs, out_specs=out_specs, out_shape=out_shape,
                              scratch_shapes=scratch_shapes, compiler_params=cparams)(*args)
        return list(outs), []
    kind, arrs = comm
    nc, n_in, n_out, n_scr = len(arrs), len(in_specs), len(out_specs), len(scratch_shapes)
    total = 1
    for gdim in grid:
        total *= gdim
    forward_step = (total * 3) // 4

    def hosted(*refs):
        core_in, c_in = refs[:n_in], refs[n_in:n_in + nc]
        core_out = refs[n_in + nc:n_in + nc + n_out]
        c_out = refs[n_in + nc + n_out:n_in + 2 * nc + n_out]
        rest = refs[n_in + 2 * nc + n_out:]
        step = pl.program_id(0)
        for ax in range(1, len(grid)):
            step = step * grid[ax] + pl.program_id(ax)
        start, forward, finish = _COMM_STEPS[kind](c_in, c_out, *rest[n_scr:], _place())
        pl.when(step == 0)(start)
        if forward is not None:
            pl.when(step == forward_step)(forward)
        body(*core_in, *core_out, *rest[:n_scr])
        pl.when(step == total - 1)(finish)

    any_spec = pl.BlockSpec(memory_space=pl.ANY)
    outs = pl.pallas_call(
        hosted, name=name, grid=grid,
        in_specs=in_specs + [any_spec] * nc,
        out_specs=out_specs + [any_spec] * nc,
        out_shape=out_shape + _comm_out_shapes(kind, arrs),
        scratch_shapes=scratch_shapes + _comm_sems(nc),
        compiler_params=cparams)(*args, *arrs)
    return list(outs[:n_out]), list(outs[n_out:])


def _comm_call(kind, arrs, name):
    n = len(arrs)

    def body(*refs):
        start, forward, finish = _COMM_STEPS[kind](refs[:n], refs[n:2 * n], *refs[2 * n:], _place())
        start()
        if forward is not None:
            forward()
        finish()

    any_spec = pl.BlockSpec(memory_space=pl.ANY)
    return pl.pallas_call(
        body, name=name,
        in_specs=[any_spec] * n, out_specs=[any_spec] * n,
        out_shape=_comm_out_shapes(kind, arrs), scratch_shapes=_comm_sems(n),
    )(*arrs)


def all_gather(arrs, name):
    return _comm_call("gather", arrs, name)


def all_to_all(arrs, name):
    return _comm_call("exchange", arrs, name)


def _tiles(S):
    t = min(512, S)
    return dict(ffn=min(256, S), row=t, conv=t, gla=t, bk=min(1024, S))


BIG = ("wi1", "wo1", "win", "wout", "wi2", "wo2")


def _col_shards_to_full(gathered):
    n, r, c = gathered.shape
    return jnp.transpose(gathered, (1, 0, 2)).reshape(r, n * c)


def _win_full(win_a):
    return _pad_rows(win_a.reshape(DIN, D), DINP)


def train_pass(x, tgt, mods, fmod, sh, ws, wi1_first, wo1_first):
    S = x.shape[0]
    T = _tiles(S)
    bk = T["bk"]
    full = [dict() for _ in range(DEPTH)]
    hosted = {("ffn1", 0): [("win", 0), ("wout", 0), ("wo2", 0)], ("core", 0): [("wi2", 0)]}
    for l in range(1, DEPTH):
        hosted[("mixin", l - 1)] = [("win", l), ("wout", l)]
        hosted[("ffn2", l - 1)] = [("wi1", l), ("wo1", l)]
        hosted[("ffn1", l)] = [("wi2", l), ("wo2", l)]

    def comm_for(key):
        return ("gather", [sh[n][ll] for n, ll in hosted[key]]) if key in hosted else None

    def keep(key, got):
        for (n, ll), gathered in zip(hosted.get(key, []), got):
            if n in ("wi1", "wi2"):
                full[ll][n] = gathered.reshape(2 * F, D)
            elif n in ("wo1", "wo2"):
                full[ll][n] = gathered.reshape(F, D)
            else:
                full[ll][n] = _win_full(gathered) if n == "win" else gathered.reshape(D, D)

    full[0]["wi1"], full[0]["wo1"] = wi1_first.reshape(2 * F, D), wo1_first.reshape(F, D)
    saved = []
    xc = x
    for l in range(DEPTH):
        w, fw = ws[f"L{l}"], full[l]
        x0 = xc
        (x1, h1f, z1, f1), got = ffn_fwd(x0, mods[l], w["g1"], fw["wi1"], fw["wo1"], (0, 1, 2), T["ffn"], f"ffn1_fwd_{l}",
                                         comm=comm_for(("ffn1", l)))
        keep(("ffn1", l), got)
        (z, la), got = mixin_fwd(x1, mods[l], w["g2"], fw["win"], w["wgu"], w["bgate"], T["row"], f"mixin_fwd_{l}",
                                 comm=comm_for(("mixin", l)))
        keep(("mixin", l), got)
        (y, yc, yg, sprev), got = mixer_core_fwd(z, la, w["gn_s"], w["wdw"], w["cpar"], T["gla"], f"mixer_core_fwd_{l}",
                                                 comm=comm_for(("core", l)))
        keep(("core", l), got)
        x2 = mixout_fwd(x1, yc, yg, mods[l], fw["wout"], T["row"], f"mixout_fwd_{l}")
        (x3, h2f, z2, f2), got = ffn_fwd(x2, mods[l], w["g3"], fw["wi2"], fw["wo2"], (6, 7, 8), T["ffn"], f"ffn2_fwd_{l}",
                                         comm=comm_for(("ffn2", l)))
        keep(("ffn2", l), got)
        saved.append(dict(x0=x0, x1=x1, x2=x2, h1f=h1f, z1=z1, f1=f1, h2f=h2f, z2=z2, f2=f2,
                          z=z, la=la, y=y, yc=yc, yg=yg, sprev=sprev))
        xc = x3

    dx, redf = final_fwd_bwd(xc, tgt, fmod, ws["gf"], T["row"], "loss_head")
    loss_lanes = redf[0]
    dfmod = redf[1:3]
    grads = {"gf": redf[3]}
    dmods = [None] * DEPTH
    recv = {n: [None] * DEPTH for n in BIG}

    def ffn_backward(xin, dy, h, z, fo, gain, wi_t, wo, rows, l, tag, ride=None):
        (dz, p_wo), got_ride = ffn_bwd_hidden(dy, z, mods[l], wo.T, rows[2], T["ffn"], f"{tag}_bwd_hidden_{l}",
                                              comm=("exchange", ride) if ride else None)
        p_wi, (r_wo,) = dwi_pieces(h, dz, T["bk"], f"d{tag}_wi_{l}",
                                   comm=("exchange", [p_wo.reshape(N_DEV, F // N_DEV, D)]))
        (dxin, red), (r_wi,) = ffn_bwd_input(xin, dy, dz, fo, mods[l], gain, wi_t, rows, T["ffn"],
                                             f"{tag}_bwd_input_{l}", comm=("exchange", [p_wi]))
        return dxin, red, r_wi, r_wo, got_ride

    for l in reversed(range(DEPTH)):
        w, fw, sv = ws[f"L{l}"], full[l], saved[l]
        g = {}
        dx2, red3, recv["wi2"][l], recv["wo2"][l], _ = ffn_backward(
            sv["x2"], dx, sv["h2f"], sv["z2"], sv["f2"], w["g3"], fw["wi2"], fw["wo2"], (6, 7, 8), l, "ffn2")
        dyc, dyg, red_o, p_wout = mixout_bwd(dx2, sv["yc"], sv["yg"], mods[l], fw["wout"], T["row"], f"mixout_bwd_{l}")
        (dq, dk, dv, dr, dpre, redg, redb), (recv["wout"][l],) = gla_bwd(
            sv["z"], sv["la"], sv["sprev"], dyg, w["gn_s"], T["gla"], f"gla_bwd_{l}",
            comm=("exchange", [p_wout.reshape(N_DEV, D // N_DEV, D)]))
        dzab, redc = conv_bwd(sv["z"], sv["y"], dyc, w["wdw"], w["cpar"], T["conv"], f"conv_bwd_{l}")
        (dx1, red2, dwin_t), _ = mixin_bwd(sv["x1"], dx2, dzab, dq, dk, dv, dr, dpre, mods[l], w["g2"], fw["win"], w["wgu"],
                                           T["ffn"], f"mixin_bwd_{l}")
        p_win = dwin_t[:DIN].reshape(N_DEV, DIN // N_DEV, D)
        g["wgu"] = matmul_tn(sv["z"], dpre, 128, DQK, 128, DQK, bk, f"dwgu_{l}", a_col_block=(DINP - 128) // 128)[:GR]
        g["bgate"] = jnp.sum(redb, axis=0)
        g["gn"] = jnp.sum(redg.reshape(NH, CH, DV), axis=1)
        g["wdw"] = redc[:CW]
        g["bdw"], g["gln"], g["bln"] = redc[32], redc[33], redc[34]
        dx0, red1, recv["wi1"][l], recv["wo1"][l], (recv["win"][l],) = ffn_backward(
            sv["x0"], dx1, sv["h1f"], sv["z1"], sv["f1"], w["g1"], fw["wi1"], fw["wo1"], (0, 1, 2), l, "ffn1",
            ride=[p_win])
        g["g1"], g["g2"], g["g3"] = red1[3], red2[2], red3[3]
        dmods[l] = jnp.stack([red1[0], red1[1], red1[2], red2[0], red2[1], red_o[0], red3[0], red3[1], red3[2]], axis=0)
        grads[f"L{l}"] = g
        dx = dx0
    return loss_lanes, dx, grads, dmods, dfmod, recv


def _pad_rows(a, rows):
    return jnp.pad(a, ((0, rows - a.shape[0]), (0, 0)))


def kernel(x, c, w_ada, b_ada, g_norm_ffn1, w_ffn1_in, w_ffn1_out, g_norm_mix, w_in, w_dw, b_dw, g_conv_ln, b_conv_ln, w_gate_up, b_gate, g_gla_norm, w_out, g_norm_ffn2, w_ffn2_in, w_ffn2_out, g_norm_final, w_ada_final, b_ada_final, loss_target, m_w_ada, m_b_ada, m_g_norm_ffn1, m_w_ffn1_in, m_w_ffn1_out, m_g_norm_mix, m_w_in, m_w_dw, m_b_dw, m_g_conv_ln, m_b_conv_ln, m_w_gate_up, m_b_gate, m_g_gla_norm, m_w_out, m_g_norm_ffn2, m_w_ffn2_in, m_w_ffn2_out, m_g_norm_final, m_w_ada_final, m_b_ada_final, v_w_ada, v_b_ada, v_g_norm_ffn1, v_w_ffn1_in, v_w_ffn1_out, v_g_norm_mix, v_w_in, v_w_dw, v_b_dw, v_g_conv_ln, v_b_conv_ln, v_w_gate_up, v_b_gate, v_g_gla_norm, v_w_out, v_g_norm_ffn2, v_w_ffn2_in, v_w_ffn2_out, v_g_norm_final, v_w_ada_final, v_b_ada_final):
    me = 4 * lax.axis_index("x") + 2 * lax.axis_index("y") + lax.axis_index("c")
    L = DEPTH
    n_ada = N_MOD * D // N_DEV
    n_fin = 2 * D // N_DEV

    small = jnp.concatenate([c.reshape(-1), w_dw.reshape(-1), w_gate_up.reshape(-1)])
    n_small = small.shape[0]
    small = jnp.pad(small, (0, 8 * D - n_small)).reshape(8, D)
    big = dict(wi1=w_ffn1_in, wo1=w_ffn1_out, win=w_in, wout=w_out, wi2=w_ffn2_in, wo2=w_ffn2_out)
    transposed = ("wi1", "wi2", "win")
    sh = {n: [(a[l].T if n in transposed else a[l]).astype(bf16) for l in range(L)] for n, a in big.items()}
    small_a, wi1_first, wo1_first = all_gather([small, sh["wi1"][0], sh["wo1"][0]], "gather_first")
    small_a = small_a.reshape(N_DEV, 8 * D)
    c_all = small_a[:, :D]
    o1 = D + L * CW * (DC // N_DEV)
    wdw_full = _col_shards_to_full(small_a[:, D:o1].reshape(N_DEV, L * CW, DC // N_DEV)).reshape(L, CW, DC)
    wgu_full = _col_shards_to_full(small_a[:, o1:o1 + L * GR * (DQK // N_DEV)].reshape(N_DEV, L * GR, DQK // N_DEV)).reshape(L, GR, DQK)

    b_ada_mine = lax.dynamic_slice(b_ada, (0, me * n_ada), (L, n_ada))
    b_fin_mine = lax.dynamic_slice(b_ada_final, (me * n_fin,), (n_fin,))
    parts = [ada_fwd(c_all, w_ada[l], b_ada_mine[l:l + 1], f"ada_fwd_{l}") for l in range(L)]
    parts.append(ada_fwd(c_all, w_ada_final, b_fin_mine.reshape(1, n_fin), "ada_fwd_final"))
    modsrc = jnp.concatenate(parts, axis=1)
    n_row = modsrc.shape[1]
    modsrc = jnp.pad(modsrc, ((0, 0), (0, 24 * 128 - n_row))).reshape(N_DEV, 24, 128)
    (modrecv,) = all_to_all([modsrc], "exchange_mod")
    modrecv = modrecv.reshape(N_DEV, 24 * 128)
    mods = []
    for l in range(L):
        mvec = modrecv[:, l * n_ada:(l + 1) * n_ada].reshape(N_MOD, D)
        mods.append(_pad_rows(mvec, 16))
    fmod = _pad_rows(modrecv[:, L * n_ada:L * n_ada + n_fin].reshape(2, D), 8)

    ws = {"gf": g_norm_final.reshape(1, D)}
    for l in range(L):
        ws[f"L{l}"] = dict(
            g1=g_norm_ffn1[l].reshape(1, D), g2=g_norm_mix[l].reshape(1, D), g3=g_norm_ffn2[l].reshape(1, D),
            wgu=_pad_rows(wgu_full[l], 128).astype(bf16),
            bgate=b_gate[l].reshape(1, DQK),
            wdw=_pad_rows(wdw_full[l], 32),
            cpar=_pad_rows(jnp.stack([b_dw[l], g_conv_ln[l], b_conv_ln[l]]), 8),
            gn_s=jnp.repeat(g_gla_norm[l], CH, axis=0),
        )

    loss_lanes, grad_x, gr, dmods, dfmod, recv = train_pass(
        x[0], loss_target[0], mods, fmod, sh, ws, wi1_first, wo1_first)

    def adam_big(rv, w, m, v, name, is_transposed=False):
        if is_transposed:
            w, m, v = (jnp.swapaxes(a, 1, 2) for a in (w, m, v))
        R = w.shape[1]
        tr = 256 if R % 256 == 0 else (R // 2 if (R // 2) % 16 == 0 else R)
        outs, _ = adam_parts(rv, w, m, v, tr, name)
        return [jnp.swapaxes(o, 1, 2) for o in outs] if is_transposed else outs

    res = {}
    res["w_ffn2_in"] = adam_big(recv["wi2"], w_ffn2_in, m_w_ffn2_in, v_w_ffn2_in, "adam_ffn2_in", True)
    res["w_ffn2_out"] = adam_big(recv["wo2"], w_ffn2_out, m_w_ffn2_out, v_w_ffn2_out, "adam_ffn2_out")
    res["w_in"] = adam_big(recv["win"], w_in, m_w_in, v_w_in, "adam_w_in", True)
    res["w_out"] = adam_big(recv["wout"], w_out, m_w_out, v_w_out, "adam_w_out")
    res["w_ffn1_out"] = adam_big(recv["wo1"], w_ffn1_out, m_w_ffn1_out, v_w_ffn1_out, "adam_ffn1_out")
    res["w_ffn1_in"] = adam_big(recv["wi1"], w_ffn1_in, m_w_ffn1_in, v_w_ffn1_in, "adam_ffn1_in", True)

    flat = lambda name: jnp.stack([gr[f"L{l}"][name] for l in range(L)]).reshape(-1)
    sections = [
        ("b_ada", jnp.stack(dmods).reshape(-1)), ("b_ada_final", dfmod.reshape(-1)),
        ("g_norm_ffn1", flat("g1")), ("g_norm_mix", flat("g2")), ("g_norm_ffn2", flat("g3")), ("g_norm_final", gr["gf"]),
        ("b_dw", flat("bdw")), ("g_conv_ln", flat("gln")), ("b_conv_ln", flat("bln")), ("b_gate", flat("bgate")),
        ("g_gla_norm", flat("gn")),
    ]
    n_rep = sum(s[1].shape[0] for s in sections)
    rep_rows = -(-n_rep // D)
    extra = [("loss", loss_lanes), ("w_dw", flat("wdw")), ("w_gate_up", flat("wgu"))]
    pack = jnp.concatenate([s[1] for s in sections] + [jnp.zeros((rep_rows * D - n_rep,), f32)] + [s[1] for s in extra])
    n_pack = pack.shape[0]
    pack_rows = -(-n_pack // (8 * D)) * 8
    pack = jnp.pad(pack, (0, pack_rows * D - n_pack)).reshape(pack_rows, D)
    (pack_all,) = all_gather([pack], "gather_small_grads")
    tot = sum8(pack_all, "sum_small_grads")
    tot_flat = tot.reshape(-1)
    loss = jnp.sum(tot_flat[rep_rows * D:rep_rows * D + D])
    o_dw = rep_rows * D + D
    g_wdw_full = tot_flat[o_dw:o_dw + L * CW * DC].reshape(L, CW, DC)
    o_gu = o_dw + L * CW * DC
    g_wgu_full = tot_flat[o_gu:o_gu + L * GR * DQK].reshape(L, GR, DQK)

    small_params = dict(b_ada=(b_ada, m_b_ada, v_b_ada), b_ada_final=(b_ada_final, m_b_ada_final, v_b_ada_final),
                        g_norm_ffn1=(g_norm_ffn1, m_g_norm_ffn1, v_g_norm_ffn1), g_norm_mix=(g_norm_mix, m_g_norm_mix, v_g_norm_mix),
                        g_norm_ffn2=(g_norm_ffn2, m_g_norm_ffn2, v_g_norm_ffn2), g_norm_final=(g_norm_final, m_g_norm_final, v_g_norm_final),
                        b_dw=(b_dw, m_b_dw, v_b_dw), g_conv_ln=(g_conv_ln, m_g_conv_ln, v_g_conv_ln),
                        b_conv_ln=(b_conv_ln, m_b_conv_ln, v_b_conv_ln), b_gate=(b_gate, m_b_gate, v_b_gate),
                        g_gla_norm=(g_gla_norm, m_g_gla_norm, v_g_gla_norm))

    def rep_pack(idx):
        p = jnp.concatenate([small_params[s[0]][idx].reshape(-1) for s in sections])
        return jnp.pad(p, (0, rep_rows * D - n_rep)).reshape(rep_rows, D)

    g_rep = tot[:rep_rows]
    d_rep, m_rep, v_rep = adam_plain(g_rep, rep_pack(0), rep_pack(1), rep_pack(2), rep_rows, "adam_small")
    off = 0
    for sname, sval in sections:
        shp = small_params[sname][0].shape
        nel = sval.shape[0]
        res[sname] = [a.reshape(-1)[off:off + nel].reshape(shp) for a in (g_rep, d_rep, m_rep, v_rep)]
        off += nel

    def adam_cols(g_full, w, m, v, name):
        shp = w.shape
        g_mine = lax.dynamic_slice(g_full, (0, 0, me * shp[2]), shp)
        R, C = shp[0] * shp[1], shp[2]
        outs = adam_plain(g_mine.reshape(R, C), w.reshape(R, C), m.reshape(R, C), v.reshape(R, C), R, name)
        return [g_mine] + [o.reshape(shp) for o in outs]

    res["w_dw"] = adam_cols(g_wdw_full, w_dw, m_w_dw, v_w_dw, "adam_w_dw")
    res["w_gate_up"] = adam_cols(g_wgu_full, w_gate_up, m_w_gate_up, v_w_gate_up, "adam_w_gate_up")

    c_all_t = c_all.T
    dmod_all = pack_all.reshape(N_DEV, -1)[:, :L * N_MOD * D].reshape(N_DEV, L, N_MOD * D)
    dfm_all = pack_all.reshape(N_DEV, -1)[:, L * N_MOD * D:L * N_MOD * D + 2 * D]
    dm_mine = lax.dynamic_slice(dmod_all, (0, 0, me * n_ada), (N_DEV, L, n_ada))
    dfm_mine = lax.dynamic_slice(dfm_all, (0, me * n_fin), (N_DEV, n_fin))
    g_w_ada = jnp.stack([ada_wgrad(c_all_t, dm_mine[:, l], f"ada_wgrad_{l}") for l in range(L)])
    g_w_fin = ada_wgrad(c_all_t, dfm_mine, "ada_wgrad_final")
    outs = adam_plain(g_w_ada.reshape(L * D, n_ada), w_ada.reshape(L * D, n_ada), m_w_ada.reshape(L * D, n_ada),
                      v_w_ada.reshape(L * D, n_ada), 256, "adam_w_ada")
    res["w_ada"] = [g_w_ada] + [o.reshape(w_ada.shape) for o in outs]
    res["w_ada_final"] = [g_w_fin] + list(adam_plain(g_w_fin, w_ada_final, m_w_ada_final, v_w_ada_final, 256, "adam_w_ada_final"))

    order = ["w_ada", "b_ada", "g_norm_ffn1", "w_ffn1_in", "w_ffn1_out", "g_norm_mix", "w_in", "w_dw", "b_dw", "g_conv_ln",
             "b_conv_ln", "w_gate_up", "b_gate", "g_gla_norm", "w_out", "g_norm_ffn2", "w_ffn2_in", "w_ffn2_out",
             "g_norm_final", "w_ada_final", "b_ada_final"]
    out = [loss, grad_x[None]]
    for k in range(4):
        out += [res[name][k] for name in order]
    return tuple(out)
```

```python
import functools

import jax
import jax.numpy as jnp
from jax import lax
from jax.experimental import pallas as pl
from jax.experimental.pallas import tpu as pltpu

f32 = jnp.float32
bf16 = jnp.bfloat16

N_DEV = 8
DEPTH = 2
D = 1024
F = 2816
DC = 512
NH = 4
DK = 64
DV = 128
DQK = NH * DK
DG = NH * DV
CH = 64
CW = 31
GR = 16
TAU = 16.0
N_MOD = 9
DIN = 2 * DC + 2 * DQK + 2 * DG + GR
DINP = 2688
EPS = 1e-6
HALO = 32
SUBLANES = 8
CONV_ROWS = 32
FS = 2 * F // N_DEV

ADAM_LR = 0.001
ADAM_B1 = 0.9
ADAM_B2 = 0.999
ADAM_EPS = 1e-08
ADAM_WD = 0.01
ADAM_STEP = 10

V7X_VMEM_LIMIT = 56 * 1024 * 1024
MESH = pl.DeviceIdType.MESH
HIGHEST = lax.Precision.HIGHEST

NT = (((1,), (1,)), ((), ()))
TN = (((0,), (0,)), ((), ()))


def _cp(n_axes):
    return pltpu.CompilerParams(dimension_semantics=("arbitrary",) * n_axes, vmem_limit_bytes=V7X_VMEM_LIMIT)


def _full(shape):
    nd = len(shape)
    return pl.BlockSpec(shape, lambda *_: (0,) * nd)


def _resident(shape):
    nd = len(shape)
    return pl.BlockSpec(shape, lambda *_: (0,) * nd, pipeline_mode=pl.Buffered(1))


def _dot(a, b):
    return jnp.dot(a, b, preferred_element_type=f32)


def _dg(a, b, dims):
    return lax.dot_general(a, b, dims, preferred_element_type=f32)


def _sigmoid(x):
    return jax.nn.sigmoid(x)


def _rowsum(x):
    return jnp.sum(x, axis=0, keepdims=True)


def _rms_parts(xv):
    rstd = lax.rsqrt(jnp.mean(xv * xv, axis=-1, keepdims=True) + EPS)
    return xv * rstd, rstd


def _rms_bwd(dxh, xh, rstd):
    return rstd * (dxh - xh * jnp.mean(dxh * xh, axis=-1, keepdims=True))


def ffn_fwd(x, mod, g, wi_t, wo, rows, tm, name, comm=None):
    S = x.shape[0]
    r_shift, r_scale, r_gate = rows

    def body(x_ref, mod_ref, g_ref, wi_ref, wo_ref, xo_ref, h_ref, z_ref, f_ref):
        xv = x_ref[...]
        xh, _ = _rms_parts(xv)
        h = (xh * g_ref[...] * (1.0 + mod_ref[r_scale:r_scale + 1, :]) + mod_ref[r_shift:r_shift + 1, :]).astype(bf16)
        h_ref[...] = h
        zg = _dg(h, wi_ref[0:F, :], NT)
        zu = _dg(h, wi_ref[F:2 * F, :], NT)
        z_ref[:, 0:F] = zg.astype(bf16)
        z_ref[:, F:2 * F] = zu.astype(bf16)
        fv = _dot((zg * _sigmoid(zg) * zu).astype(bf16), wo_ref[...])
        f_ref[...] = fv.astype(bf16)
        xo_ref[...] = xv + 0.5 * mod_ref[r_gate:r_gate + 1, :] * fv

    row = lambda i: (i, 0)
    return _pcall(
        body, (x, mod, g, wi_t, wo), name=name, comm=comm,
        grid=(S // tm,),
        in_specs=[pl.BlockSpec((tm, D), row), _full(mod.shape), _full(g.shape), _resident(wi_t.shape), _resident(wo.shape)],
        out_specs=[pl.BlockSpec((tm, D), row), pl.BlockSpec((tm, D), row), pl.BlockSpec((tm, 2 * F), row),
                   pl.BlockSpec((tm, D), row)],
        out_shape=[jax.ShapeDtypeStruct((S, D), f32), jax.ShapeDtypeStruct((S, D), bf16),
                   jax.ShapeDtypeStruct((S, 2 * F), bf16), jax.ShapeDtypeStruct((S, D), bf16)],
    )


def ffn_bwd_hidden(dy, z, mod, wo_t, r_gate, tm, name, comm=None):
    S = dy.shape[0]
    nt = S // tm
    halves = 2
    fc = F // halves

    def body(dy_ref, z_ref, mod_ref, wo_ref, dz_ref, dwo_ref, acc_s):
        i = pl.program_id(0)

        @pl.when(i == 0)
        def _():
            acc_s[...] = jnp.zeros_like(acc_s)

        df = (0.5 * mod_ref[r_gate:r_gate + 1, :] * dy_ref[...]).astype(bf16)
        for c in range(halves):
            lo, hi = c * fc, (c + 1) * fc
            zgv = z_ref[:, lo:hi].astype(f32)
            zuv = z_ref[:, F + lo:F + hi].astype(f32)
            s = _sigmoid(zgv)
            sil = zgv * s
            acc_s[lo:hi, :] += _dg((sil * zuv).astype(bf16), df, TN)
            da = _dot(df, wo_ref[:, lo:hi])
            dz_ref[:, F + lo:F + hi] = (da * sil).astype(bf16)
            dz_ref[:, lo:hi] = (da * zuv * (s * (1.0 + zgv * (1.0 - s)))).astype(bf16)

        @pl.when(i == nt - 1)
        def _():
            dwo_ref[...] = acc_s[...].astype(bf16)

    row = lambda i: (i, 0)
    return _pcall(
        body, (dy, z, mod, wo_t), name=name, comm=comm,
        grid=(nt,),
        in_specs=[pl.BlockSpec((tm, D), row), pl.BlockSpec((tm, 2 * F), row), _full(mod.shape), _resident(wo_t.shape)],
        out_specs=[pl.BlockSpec((tm, 2 * F), row), _full((F, D))],
        out_shape=[jax.ShapeDtypeStruct((S, 2 * F), bf16), jax.ShapeDtypeStruct((F, D), bf16)],
        scratch_shapes=[pltpu.VMEM((F, D), f32)],
    )


def ffn_bwd_input(x, dy, dz, fo, mod, g, wi_t, rows, tm, name, comm=None):
    S = x.shape[0]
    r_shift, r_scale, r_gate = rows

    def body(x_ref, dy_ref, dz_ref, f_ref, mod_ref, g_ref, wi_ref, dx_ref, red_ref):
        @pl.when(pl.program_id(0) == 0)
        def _():
            red_ref[...] = jnp.zeros_like(red_ref)

        dh = _dot(dz_ref[...], wi_ref[...])
        dyv = dy_ref[...]
        xh, rstd = _rms_parts(x_ref[...])
        gv = g_ref[...]
        n = xh * gv
        dn = dh * (1.0 + mod_ref[r_scale:r_scale + 1, :])
        red_ref[0:1, :] += _rowsum(dh)
        red_ref[1:2, :] += _rowsum(dh * n)
        red_ref[2:3, :] += _rowsum(0.5 * f_ref[...].astype(f32) * dyv)
        red_ref[3:4, :] += _rowsum(dn * xh)
        dx_ref[...] = dyv + _rms_bwd(dn * gv, xh, rstd)

    row = lambda i: (i, 0)
    return _pcall(
        body, (x, dy, dz, fo, mod, g, wi_t), name=name, comm=comm,
        grid=(S // tm,),
        in_specs=[pl.BlockSpec((tm, D), row), pl.BlockSpec((tm, D), row), pl.BlockSpec((tm, 2 * F), row),
                  pl.BlockSpec((tm, D), row), _full(mod.shape), _full(g.shape), _resident(wi_t.shape)],
        out_specs=[pl.BlockSpec((tm, D), row), _full((8, D))],
        out_shape=[jax.ShapeDtypeStruct((S, D), f32), jax.ShapeDtypeStruct((8, D), f32)],
    )


def matmul_tn(a, b, M, N, bm, bn, bk, name, a_col_block=0, out_dtype=f32):
    S = b.shape[0]
    nk = S // bk

    def body(a_ref, b_ref, o_ref, acc_s):
        k = pl.program_id(2)

        @pl.when(k == 0)
        def _():
            acc_s[...] = jnp.zeros_like(acc_s)

        acc_s[...] += _dg(a_ref[...].astype(bf16), b_ref[...].astype(bf16), TN)

        @pl.when(k == nk - 1)
        def _():
            o_ref[...] = acc_s[...].astype(out_dtype)

    return pl.pallas_call(
        body, name=name,
        grid=(M // bm, N // bn, nk),
        in_specs=[
            pl.BlockSpec((bk, bm), lambda i, j, k: (k, i + a_col_block)),
            pl.BlockSpec((bk, bn), lambda i, j, k: (k, j)),
        ],
        out_specs=pl.BlockSpec((bm, bn), lambda i, j, k: (i, j)),
        out_shape=jax.ShapeDtypeStruct((M, N), out_dtype),
        scratch_shapes=[pltpu.VMEM((bm, bn), f32)],
        compiler_params=_cp(3),
    )(a, b)


def dwi_pieces(h, dz, bk, name, comm=None):
    S = h.shape[0]
    nk = S // bk

    def body(h_ref, dz_ref, o_ref, acc_s):
        k = pl.program_id(1)

        @pl.when(k == 0)
        def _():
            acc_s[...] = jnp.zeros_like(acc_s)

        acc_s[...] += _dg(dz_ref[...], h_ref[...], TN)

        @pl.when(k == nk - 1)
        def _():
            o_ref[...] = acc_s[...].astype(bf16)

    (out,), comm_outs = _pcall(
        body, (h, dz), name=name, comm=comm,
        grid=(2, nk),
        in_specs=[pl.BlockSpec((bk, D), lambda half, k: (k, 0)), pl.BlockSpec((bk, F), lambda half, k: (k, half))],
        out_specs=[pl.BlockSpec((F, D), lambda half, k: (half, 0))],
        out_shape=[jax.ShapeDtypeStruct((2 * F, D), bf16)],
        scratch_shapes=[pltpu.VMEM((F, D), f32)],
    )
    return out.reshape(N_DEV, FS, D), comm_outs


def mixin_fwd(x1, mod, g, win, wgu, bgate, tm, name, comm=None):
    S = x1.shape[0]

    def body(x_ref, mod_ref, g_ref, win_ref, wgu_ref, bg_ref, z_ref, la_ref):
        xh, _ = _rms_parts(x_ref[...])
        hv = xh * g_ref[...] * (1.0 + mod_ref[4:5, :]) + mod_ref[3:4, :]
        z = _dg(hv.astype(bf16), win_ref[...], NT).astype(bf16)
        z_ref[...] = z
        pre = _dot(z[:, DINP - 128:], wgu_ref[...]) + bg_ref[...]
        la_ref[...] = (jnp.minimum(pre, 0.0) - jnp.log(1.0 + jnp.exp(-jnp.abs(pre)))) * (1.0 / TAU)

    return _pcall(
        body, (x1, mod, g, win, wgu, bgate), name=name, comm=comm,
        grid=(S // tm,),
        in_specs=[pl.BlockSpec((tm, D), lambda i: (i, 0)), _full(mod.shape), _full(g.shape),
                  _full(win.shape), _full(wgu.shape), _full(bgate.shape)],
        out_specs=[pl.BlockSpec((tm, DINP), lambda i: (i, 0)), pl.BlockSpec((tm, DQK), lambda i: (i, 0))],
        out_shape=[jax.ShapeDtypeStruct((S, DINP), bf16), jax.ShapeDtypeStruct((S, DQK), f32)],
    )


def mixin_bwd(x1, dres, dzab, dq, dk, dv, dr, dpre, mod, g, win, wgu, tm, name, comm=None):
    S = x1.shape[0]
    nt = S // tm

    def body(x_ref, dres_ref, dzab_ref, dq_ref, dk_ref, dv_ref, dr_ref, dpre_ref, mod_ref, g_ref, win_ref, wgu_ref,
             dx_ref, red_ref, dw_ref, acc_s):
        i = pl.program_id(0)

        @pl.when(i == 0)
        def _():
            red_ref[...] = jnp.zeros_like(red_ref)
            acc_s[...] = jnp.zeros_like(acc_s)

        dglr = _dg(dpre_ref[...], wgu_ref[...], NT).astype(bf16)
        dz = jnp.concatenate([dzab_ref[...], dq_ref[...], dk_ref[...], dv_ref[...], dr_ref[...], dglr], axis=1)
        dh = _dot(dz, win_ref[...])
        xh, rstd = _rms_parts(x_ref[...])
        gv = g_ref[...]
        n = xh * gv
        sc = 1.0 + mod_ref[4:5, :]
        acc_s[...] += _dg(dz, (n * sc + mod_ref[3:4, :]).astype(bf16), TN)
        dn = dh * sc
        red_ref[0:1, :] += _rowsum(dh)
        red_ref[1:2, :] += _rowsum(dh * n)
        red_ref[2:3, :] += _rowsum(dn * xh)
        dx_ref[...] = dres_ref[...] + _rms_bwd(dn * gv, xh, rstd)

        @pl.when(i == nt - 1)
        def _():
            dw_ref[...] = acc_s[...].astype(bf16)

    row = lambda i: (i, 0)
    return _pcall(
        body, (x1, dres, dzab, dq, dk, dv, dr, dpre, mod, g, win, wgu), name=name, comm=comm,
        grid=(nt,),
        in_specs=[pl.BlockSpec((tm, D), row), pl.BlockSpec((tm, D), row),
                  pl.BlockSpec((tm, 2 * DC), row), pl.BlockSpec((tm, DQK), row), pl.BlockSpec((tm, DQK), row),
                  pl.BlockSpec((tm, DG), row), pl.BlockSpec((tm, DG), row), pl.BlockSpec((tm, DQK), row),
                  _full(mod.shape), _full(g.shape), _resident(win.shape), _full(wgu.shape)],
        out_specs=[pl.BlockSpec((tm, D), row), _full((8, D)), _full((DINP, D))],
        out_shape=[jax.ShapeDtypeStruct((S, D), f32), jax.ShapeDtypeStruct((8, D), f32),
                   jax.ShapeDtypeStruct((DINP, D), bf16)],
        scratch_shapes=[pltpu.VMEM((DINP, D), f32)],
    )


def _glu(zab):
    zab = zab.astype(f32)
    return zab[:, :DC] * _sigmoid(zab[:, DC:])


def _shift_copies(src_s, dst_s, tc):
    n = tc + HALO - SUBLANES
    for b in range(1, SUBLANES):
        dst_s[b, 0:n, :] = src_s[b:b + n, :]


def _shifted(src_s, dst_s, o, tc):
    b = o % SUBLANES
    a = o - b
    return src_s[a:a + tc, :] if b == 0 else dst_s[b, a:a + tc, :]


def _conv_fwd_prepare(first, zc_ref, zp_ref, u_s, us_s, tc):
    up = _glu(zp_ref[...])
    u_s[0:HALO, :] = jnp.where(first, 0.0, up)
    u_s[HALO:HALO + tc, :] = _glu(zc_ref[...])
    _shift_copies(u_s, us_s, tc)


def _conv_fwd_rows(r0, n, w_ref, cp_ref, y_ref, yc_ref, u_s, us_s):
    for r in range(r0, r0 + n, CONV_ROWS):
        acc = _shifted(u_s, us_s, HALO - (CW - 1) + r, CONV_ROWS) * w_ref[0:1, :]
        for w in range(1, CW):
            acc = acc + _shifted(u_s, us_s, HALO - (CW - 1) + w + r, CONV_ROWS) * w_ref[w:w + 1, :]
        y = acc + cp_ref[0:1, :]
        y_ref[r:r + CONV_ROWS, :] = y
        yc = y - jnp.mean(y, axis=-1, keepdims=True)
        yl = yc * lax.rsqrt(jnp.mean(yc * yc, axis=-1, keepdims=True) + EPS) * cp_ref[1:2, :] + cp_ref[2:3, :]
        yc_ref[r:r + CONV_ROWS, :] = (yl * _sigmoid(yl)).astype(bf16)


def _conv_bwd_prepare(first, last, zc_ref, zp_ref, y_ref, yn_ref, d_ref, dn_ref, cp_ref, red_ref, u_s, dy_s, us_s, dys_s, tc):
    gl = cp_ref[1:2, :]
    bl = cp_ref[2:3, :]

    def ln_bwd(yv, dv):
        yc = yv - jnp.mean(yv, axis=-1, keepdims=True)
        rstd = lax.rsqrt(jnp.mean(yc * yc, axis=-1, keepdims=True) + EPS)
        yh = yc * rstd
        yl = yh * gl + bl
        s = _sigmoid(yl)
        dyl = dv * (s * (1.0 + yl * (1.0 - s)))
        dyh = dyl * gl
        dyv = rstd * (dyh - jnp.mean(dyh, axis=-1, keepdims=True) - yh * jnp.mean(dyh * yh, axis=-1, keepdims=True))
        return dyv, dyl, yh

    dy_c, dyl_c, yh_c = ln_bwd(y_ref[...], d_ref[...])
    dy_n, _, _ = ln_bwd(yn_ref[...], dn_ref[...])
    dy_s[0:tc, :] = dy_c
    dy_s[tc:tc + HALO, :] = jnp.where(last, 0.0, dy_n)
    u_s[0:HALO, :] = jnp.where(first, 0.0, _glu(zp_ref[...]))
    u_s[HALO:HALO + tc, :] = _glu(zc_ref[...])
    _shift_copies(u_s, us_s, tc)
    _shift_copies(dy_s, dys_s, tc)
    red_ref[32:33, :] += _rowsum(dy_c)
    red_ref[33:34, :] += _rowsum(dyl_c * yh_c)
    red_ref[34:35, :] += _rowsum(dyl_c)


def _conv_bwd_input_rows(r0, n, zc_ref, w_ref, dz_ref, dy_s, dys_s):
    for r in range(r0, r0 + n, CONV_ROWS):
        du = _shifted(dy_s, dys_s, CW - 1 + r, CONV_ROWS) * w_ref[0:1, :]
        for w in range(1, CW):
            du = du + _shifted(dy_s, dys_s, CW - 1 - w + r, CONV_ROWS) * w_ref[w:w + 1, :]
        zc = zc_ref[r:r + CONV_ROWS, :].astype(f32)
        av = zc[:, :DC]
        sb = _sigmoid(zc[:, DC:])
        dz_ref[r:r + CONV_ROWS, :] = jnp.concatenate([du * sb, du * av * sb * (1.0 - sb)], axis=1).astype(dz_ref.dtype)


def _conv_bwd_taps(w0, w1, red_ref, u_s, us_s, dy_s, tc):
    for w in range(w0, w1):
        part = None
        for r in range(0, tc, CONV_ROWS):
            prod = _shifted(u_s, us_s, HALO - (CW - 1) + w + r, CONV_ROWS) * dy_s[r:r + CONV_ROWS, :]
            fold = jnp.sum(prod.reshape(CONV_ROWS // SUBLANES, SUBLANES, DC), axis=0)
            part = fold if part is None else part + fold
        red_ref[w:w + 1, :] += _rowsum(part)


def conv_bwd(z, y, dyc, wdw, cpar, tc, name):
    S = z.shape[0]
    nb = tc // HALO
    nt = S // tc
    last_halo = S // HALO - 1

    def body(zc_ref, zp_ref, y_ref, yn_ref, d_ref, dn_ref, w_ref, cp_ref, dz_ref, red_ref, u_s, dy_s, us_s, dys_s):
        i = pl.program_id(0)

        @pl.when(i == 0)
        def _():
            red_ref[...] = jnp.zeros_like(red_ref)

        _conv_bwd_prepare(i == 0, i == nt - 1, zc_ref, zp_ref, y_ref, yn_ref, d_ref, dn_ref, cp_ref, red_ref,
                          u_s, dy_s, us_s, dys_s, tc)
        _conv_bwd_input_rows(0, tc, zc_ref, w_ref, dz_ref, dy_s, dys_s)
        _conv_bwd_taps(0, CW, red_ref, u_s, us_s, dy_s, tc)

    cur = lambda i: (i, 0)
    nxt = lambda i: (jnp.minimum((i + 1) * nb, last_halo), 0)
    return pl.pallas_call(
        body, name=name,
        grid=(nt,),
        in_specs=[pl.BlockSpec((tc, 2 * DC), cur),
                  pl.BlockSpec((HALO, 2 * DC), lambda i: (jnp.maximum(i * nb - 1, 0), 0)),
                  pl.BlockSpec((tc, DC), cur), pl.BlockSpec((HALO, DC), nxt),
                  pl.BlockSpec((tc, DC), cur), pl.BlockSpec((HALO, DC), nxt),
                  _full(wdw.shape), _full(cpar.shape)],
        out_specs=[pl.BlockSpec((tc, 2 * DC), cur), _full((40, DC))],
        out_shape=[jax.ShapeDtypeStruct((S, 2 * DC), bf16), jax.ShapeDtypeStruct((40, DC), f32)],
        scratch_shapes=[pltpu.VMEM((HALO + tc, DC), f32), pltpu.VMEM((tc + HALO, DC), f32),
                        pltpu.VMEM((SUBLANES, HALO + tc, DC), f32), pltpu.VMEM((SUBLANES, HALO + tc, DC), f32)],
        compiler_params=_cp(1),
    )(z, z, y, y, dyc, dyc, wdw, cpar)


def _gla_consts():
    r = lax.broadcasted_iota(jnp.int32, (CH, CH), 0)
    c = lax.broadcasted_iota(jnp.int32, (CH, CH), 1)
    tril = r >= c
    lane = lax.broadcasted_iota(jnp.int32, (CH, DQK), 1)
    masks = [(lane >= h * DK) & (lane < (h + 1) * DK) for h in range(NH)]
    r4 = lax.broadcasted_iota(jnp.int32, (DQK, DQK), 0)
    c4 = lax.broadcasted_iota(jnp.int32, (DQK, DQK), 1)
    eye4 = (r4 == c4).astype(f32)
    rs = lax.broadcasted_iota(jnp.int32, (DQK, CH), 0) & (CH - 1)
    tril4 = rs >= lax.broadcasted_iota(jnp.int32, (DQK, CH), 1)
    return tril, tril4, masks, eye4


def _stack(xv, masks):
    return jnp.concatenate([jnp.where(m, xv, 0.0) for m in masks], axis=0)


def _unstack(rv, masks):
    out = jnp.where(masks[0], rv[0:CH, :], 0.0)
    for h in range(1, NH):
        out = out + jnp.where(masks[h], rv[h * CH:(h + 1) * CH, :], 0.0)
    return out


def _vstack(xv):
    return jnp.concatenate([xv[:, h * DV:(h + 1) * DV] for h in range(NH)], axis=0)


def _vunstack(xv):
    return jnp.concatenate([xv[h * CH:(h + 1) * CH, :] for h in range(NH)], axis=1)


def _gla_chunk_decay(bc, qc, kc, masks):
    qc, kc = qc.astype(f32), kc.astype(f32)
    bend = bc[CH - 1:CH, :]
    eb = jnp.exp(bc)
    enb = jnp.exp(-bc)
    ed = jnp.exp(bend - bc)
    qh = qc * (DK ** -0.5)
    qf = qh * eb
    qn = qh * enb
    kn = kc * enb
    kp = kc * eb
    kd = kc * ed
    return dict(bc=bc, bend=bend, eb=eb, enb=enb, ed=ed, qf=qf, qn=qn, kn=kn, kp=kp, kd=kd,
                qf_s=_stack(qf, masks).astype(bf16), qn_s=_stack(qn, masks).astype(bf16),
                kn_b=kn.astype(bf16), kp_b=kp.astype(bf16))


def _gla_chunk_fwd(lac, qc, kc, vc, s_all, tril, masks, tril4):
    qc, kc = qc.astype(f32), kc.astype(f32)
    lmat = tril.astype(f32)
    bc = jnp.dot(lmat, lac, preferred_element_type=f32, precision=HIGHEST)
    bend = bc[CH - 1:CH, :]
    eb = jnp.exp(bc)
    enb = jnp.exp(-bc)
    ed = jnp.exp(bend - bc)
    qh = qc * (DK ** -0.5)
    qf = qh * eb
    qn = qh * enb
    kn = kc * enb
    kp = kc * eb
    kd = kc * ed
    qf_s = _stack(qf, masks).astype(bf16)
    qn_s = _stack(qn, masks).astype(bf16)
    kn_b = kn.astype(bf16)
    kp_b = kp.astype(bf16)
    attf = _dg(qf_s, kn_b, NT)
    attb = _dg(qn_s, kp_b, NT)
    a_s = jnp.where(tril4, attf, attb)
    a_b = a_s.astype(bf16)
    v_b = vc.astype(bf16)
    intra = jnp.concatenate(
        [_dot(a_b[h * CH:(h + 1) * CH, :], v_b[:, h * DV:(h + 1) * DV]) for h in range(NH)], axis=0)
    o_s = intra + _dot(qf_s, s_all.astype(bf16))
    return dict(bc=bc, bend=bend, eb=eb, enb=enb, ed=ed, qf=qf, qn=qn, kn=kn, kp=kp, kd=kd,
                qf_s=qf_s, qn_s=qn_s, kn_b=kn_b, kp_b=kp_b, a_b=a_b, v_b=v_b, o_s=o_s)


def _col_from_row(row, eye4):
    return jnp.sum(eye4 * row, axis=1, keepdims=True)


def _row_from_col(col, eye4):
    return jnp.sum(eye4 * col, axis=0, keepdims=True)


def _gla_fwd_chunk(c, consts, q_ref, k_ref, v_ref, r_ref, la_ref, gn_ref, yg_ref, sp_ref, bc_ref, as_ref, os_ref, st):
    tril, tril4, masks, eye4 = consts
    r0, s0 = c * CH, c * DQK
    s_all = st[...]
    sp_ref[s0:s0 + DQK, :] = s_all
    vc = v_ref[r0:r0 + CH, :]
    t = _gla_chunk_fwd(la_ref[r0:r0 + CH, :], q_ref[r0:r0 + CH, :], k_ref[r0:r0 + CH, :], vc,
                       s_all, tril, masks, tril4)
    u_all = _dg(_stack(t["kd"], masks).astype(bf16), _vstack(vc).astype(bf16), TN)
    st[...] = _col_from_row(jnp.exp(t["bend"]), eye4) * s_all + u_all
    o_s = t["o_s"]
    bc_ref[r0:r0 + CH, :] = t["bc"]
    as_ref[s0:s0 + DQK, :] = t["a_b"]
    os_ref[s0:s0 + DQK, :] = o_s
    on = o_s * lax.rsqrt(jnp.mean(o_s * o_s, axis=-1, keepdims=True) + EPS) * gn_ref[...]
    rc = r_ref[r0:r0 + CH, :].astype(f32)
    yg_ref[r0:r0 + CH, :] = (_vunstack(on) * (rc * _sigmoid(rc))).astype(bf16)


def mixer_core_fwd(z, la, gn_s, wdw, cpar, t, name, comm=None):
    S = z.shape[0]
    nb = t // HALO
    nc = t // CH

    def body(zc_ref, zp_ref, q_ref, k_ref, v_ref, r_ref, la_ref, gn_ref, w_ref, cp_ref,
             y_ref, yc_ref, yg_ref, sp_ref, bc_ref, as_ref, os_ref, st, u_s, us_s):
        i = pl.program_id(0)

        @pl.when(i == 0)
        def _():
            st[...] = jnp.zeros_like(st)

        _conv_fwd_prepare(i == 0, zc_ref, zp_ref, u_s, us_s, t)
        consts = _gla_consts()
        for c in range(nc):
            _conv_fwd_rows(c * CH, CH, w_ref, cp_ref, y_ref, yc_ref, u_s, us_s)
            _gla_fwd_chunk(c, consts, q_ref, k_ref, v_ref, r_ref, la_ref, gn_ref, yg_ref, sp_ref, bc_ref, as_ref, os_ref, st)

    row = lambda i: (i, 0)
    return _pcall(
        body, (z, z, z, z, z, z, la, gn_s, wdw, cpar), name=name, comm=comm,
        grid=(S // t,),
        in_specs=[pl.BlockSpec((t, 2 * DC), row),
                  pl.BlockSpec((HALO, 2 * DC), lambda i: (jnp.maximum(i * nb - 1, 0), 0)),
                  pl.BlockSpec((t, DQK), lambda i: (i, 4)), pl.BlockSpec((t, DQK), lambda i: (i, 5)),
                  pl.BlockSpec((t, DG), lambda i: (i, 3)), pl.BlockSpec((t, DG), lambda i: (i, 4)),
                  pl.BlockSpec((t, DQK), row), _full(gn_s.shape), _full(wdw.shape), _full(cpar.shape)],
        out_specs=[pl.BlockSpec((t, DC), row), pl.BlockSpec((t, DC), row), pl.BlockSpec((t, DG), row),
                   pl.BlockSpec((nc * DQK, DV), row), pl.BlockSpec((t, DQK), row),
                   pl.BlockSpec((nc * DQK, CH), row), pl.BlockSpec((nc * DQK, DV), row)],
        out_shape=[jax.ShapeDtypeStruct((S, DC), f32), jax.ShapeDtypeStruct((S, DC), bf16),
                   jax.ShapeDtypeStruct((S, DG), bf16), jax.ShapeDtypeStruct((S // CH * DQK, DV), f32),
                   jax.ShapeDtypeStruct((S, DQK), f32), jax.ShapeDtypeStruct((S // CH * DQK, CH), bf16),
                   jax.ShapeDtypeStruct((S // CH * DQK, DV), f32)],
        scratch_shapes=[pltpu.VMEM((DQK, DV), f32), pltpu.VMEM((HALO + t, DC), f32),
                        pltpu.VMEM((SUBLANES, HALO + t, DC), f32)],
    )


def _gla_bwd_chunk(c, consts, umat, last_row, q_ref, k_ref, v_ref, r_ref, la_ref, sp_ref, bc_ref, as_ref, os_ref,
                   dy_ref, gn_ref, dq_ref, dk_ref, dv_ref, dr_ref, dpre_ref, redg_ref, redb_ref, gs):
    tril, tril4, masks, eye4 = consts
    r0, s0 = c * CH, c * DQK
    rows = slice(r0, r0 + CH)
    s_all = sp_ref[s0:s0 + DQK, :]
    lac = la_ref[rows, :]
    vc = v_ref[rows, :]
    rc = r_ref[rows, :].astype(f32)
    t = _gla_chunk_decay(bc_ref[rows, :], q_ref[rows, :], k_ref[rows, :], masks)
    g_all = gs[...]
    g_b = g_all.astype(bf16)
    s_b = s_all.astype(bf16)
    o_s = os_ref[s0:s0 + DQK, :]
    rstd = lax.rsqrt(jnp.mean(o_s * o_s, axis=-1, keepdims=True) + EPS)
    oh = o_s * rstd
    gnv = gn_ref[...]
    sr = _sigmoid(rc)
    dyv = dy_ref[rows, :]
    dr_ref[rows, :] = (dyv * _vunstack(oh * gnv) * (sr * (1.0 + rc * (1.0 - sr)))).astype(dr_ref.dtype)
    don = _vstack(dyv * (rc * sr))
    redg_ref[...] += don * oh
    doh = don * gnv
    do_s = rstd * (doh - oh * jnp.mean(doh * oh, axis=-1, keepdims=True))
    do_b = do_s.astype(bf16)
    v_b = vc.astype(bf16)
    vst_b = _vstack(vc).astype(bf16)
    kd_s = _stack(t["kd"], masks).astype(bf16)
    da_s = jnp.concatenate(
        [_dg(do_b[h * CH:(h + 1) * CH, :], v_b[:, h * DV:(h + 1) * DV], NT) for h in range(NH)], axis=0)
    a_b = as_ref[s0:s0 + DQK, :]
    dv_s = jnp.concatenate(
        [_dg(a_b[h * CH:(h + 1) * CH, :], do_b[h * CH:(h + 1) * CH, :], TN) for h in range(NH)], axis=0)
    dv_s = dv_s + _dot(kd_s, g_b)
    dv_ref[rows, :] = _vunstack(dv_s).astype(dv_ref.dtype)
    gend = jnp.exp(t["bend"])
    gcol = _col_from_row(gend, eye4)
    gs[...] = gcol * g_all + _dg(t["qf_s"], do_b, TN)
    dgcol = jnp.sum(g_all * s_all, axis=1, keepdims=True)
    dbend = _row_from_col(dgcol * gcol, eye4)
    dkd = _unstack(_dg(vst_b, g_b, NT), masks)
    daf = jnp.where(tril4, da_s, 0.0).astype(bf16)
    dab = jnp.where(tril4, 0.0, da_s).astype(bf16)
    dqf = _unstack(_dot(daf, t["kn_b"]) + _dg(do_b, s_b, NT), masks)
    dqn = _unstack(_dot(dab, t["kp_b"]), masks)
    dkn = _dg(daf, t["qf_s"], TN)
    dkp = _dg(dab, t["qn_s"], TN)
    dq_ref[rows, :] = ((dqf * t["eb"] + dqn * t["enb"]) * (DK ** -0.5)).astype(dq_ref.dtype)
    dk_ref[rows, :] = (dkn * t["enb"] + dkp * t["eb"] + dkd * t["ed"]).astype(dk_ref.dtype)
    dkd_kd = dkd * t["kd"]
    dbc = dqf * t["qf"] - dqn * t["qn"] - dkn * t["kn"] + dkp * t["kp"] - dkd_kd
    dbc = dbc + jnp.where(last_row, _rowsum(dkd_kd) + dbend, 0.0)
    dla = jnp.dot(umat, dbc, preferred_element_type=f32, precision=HIGHEST)
    dpre = dla * (1.0 / TAU) * (1.0 - jnp.exp(TAU * lac))
    dpre_ref[rows, :] = dpre.astype(dpre_ref.dtype)
    redb_ref[...] += dpre


def gla_bwd(z, la, sprev, bc, att, o, dyg, gn_s, t, name, comm=None):
    S = z.shape[0]
    nc = t // CH
    nt = S // t

    def body(q_ref, k_ref, v_ref, r_ref, la_ref, sp_ref, bc_ref, as_ref, os_ref, dy_ref, gn_ref,
             dq_ref, dk_ref, dv_ref, dr_ref, dpre_ref, redg_ref, redb_ref, gs):
        @pl.when(pl.program_id(0) == 0)
        def _():
            gs[...] = jnp.zeros_like(gs)
            redg_ref[...] = jnp.zeros_like(redg_ref)
            redb_ref[...] = jnp.zeros_like(redb_ref)

        consts = _gla_consts()
        umat = (lax.broadcasted_iota(jnp.int32, (CH, CH), 0) <= lax.broadcasted_iota(jnp.int32, (CH, CH), 1)).astype(f32)
        last_row = lax.broadcasted_iota(jnp.int32, (CH, DQK), 0) == CH - 1
        for c in reversed(range(nc)):
            _gla_bwd_chunk(c, consts, umat, last_row, q_ref, k_ref, v_ref, r_ref, la_ref, sp_ref, bc_ref, as_ref, os_ref,
                           dy_ref, gn_ref, dq_ref, dk_ref, dv_ref, dr_ref, dpre_ref, redg_ref, redb_ref, gs)

    rev = lambda col: (lambda i: (nt - 1 - i, col))
    return _pcall(
        body, (z, z, z, z, la, sprev, bc, att, o, dyg, gn_s), name=name, comm=comm,
        grid=(nt,),
        in_specs=[pl.BlockSpec((t, DQK), rev(4)), pl.BlockSpec((t, DQK), rev(5)),
                  pl.BlockSpec((t, DG), rev(3)), pl.BlockSpec((t, DG), rev(4)),
                  pl.BlockSpec((t, DQK), rev(0)), pl.BlockSpec((nc * DQK, DV), rev(0)),
                  pl.BlockSpec((t, DQK), rev(0)), pl.BlockSpec((nc * DQK, CH), rev(0)),
                  pl.BlockSpec((nc * DQK, DV), rev(0)),
                  pl.BlockSpec((t, DG), rev(0)), _full(gn_s.shape)],
        out_specs=[pl.BlockSpec((t, DQK), rev(0)), pl.BlockSpec((t, DQK), rev(0)),
                   pl.BlockSpec((t, DG), rev(0)), pl.BlockSpec((t, DG), rev(0)), pl.BlockSpec((t, DQK), rev(0)),
                   _full((DQK, DV)), _full((CH, DQK))],
        out_shape=[jax.ShapeDtypeStruct((S, DQK), bf16), jax.ShapeDtypeStruct((S, DQK), bf16),
                   jax.ShapeDtypeStruct((S, DG), bf16), jax.ShapeDtypeStruct((S, DG), bf16), jax.ShapeDtypeStruct((S, DQK), bf16),
                   jax.ShapeDtypeStruct((DQK, DV), f32), jax.ShapeDtypeStruct((CH, DQK), f32)],
        scratch_shapes=[pltpu.VMEM((DQK, DV), f32)],
    )


def mixout_fwd(x1, yc, yg, mod, wout, tm, name):
    S = x1.shape[0]

    def body(x_ref, yc_ref, yg_ref, mod_ref, w_ref, xo_ref):
        mixo = _dot(yc_ref[...], w_ref[0:DC, :]) + _dot(yg_ref[...], w_ref[DC:DC + DG, :])
        xo_ref[...] = x_ref[...] + mod_ref[5:6, :] * mixo

    row = lambda i: (i, 0)
    return pl.pallas_call(
        body, name=name,
        grid=(S // tm,),
        in_specs=[pl.BlockSpec((tm, D), row), pl.BlockSpec((tm, DC), row), pl.BlockSpec((tm, DG), row),
                  _full(mod.shape), _full(wout.shape)],
        out_specs=pl.BlockSpec((tm, D), row),
        out_shape=jax.ShapeDtypeStruct((S, D), f32),
        compiler_params=_cp(1),
    )(x1, yc, yg, mod, wout)


def mixout_bwd(dx2, yc, yg, mod, wout, tm, name):
    S = dx2.shape[0]
    nt = S // tm

    def body(dx_ref, yc_ref, yg_ref, mod_ref, w_ref, dyc_ref, dyg_ref, red_ref, dw_ref, acc_s):
        i = pl.program_id(0)

        @pl.when(i == 0)
        def _():
            red_ref[...] = jnp.zeros_like(red_ref)
            acc_s[...] = jnp.zeros_like(acc_s)

        dxv = dx_ref[...]
        ycat = jnp.concatenate([yc_ref[...], yg_ref[...]], axis=1)
        mixo = _dot(ycat, w_ref[...])
        red_ref[0:1, :] += _rowsum(dxv * mixo)
        dm = (mod_ref[5:6, :] * dxv).astype(bf16)
        acc_s[...] += _dg(ycat, dm, TN)
        dycat = _dg(dm, w_ref[...], NT)
        dyc_ref[...] = dycat[:, :DC]
        dyg_ref[...] = dycat[:, DC:]

        @pl.when(i == nt - 1)
        def _():
            dw_ref[...] = acc_s[...].astype(bf16)

    row = lambda i: (i, 0)
    return pl.pallas_call(
        body, name=name,
        grid=(nt,),
        in_specs=[pl.BlockSpec((tm, D), row), pl.BlockSpec((tm, DC), row), pl.BlockSpec((tm, DG), row),
                  _full(mod.shape), _full(wout.shape)],
        out_specs=[pl.BlockSpec((tm, DC), row), pl.BlockSpec((tm, DG), row), _full((8, D)), _full((D, D))],
        out_shape=[jax.ShapeDtypeStruct((S, DC), f32), jax.ShapeDtypeStruct((S, DG), f32),
                   jax.ShapeDtypeStruct((8, D), f32), jax.ShapeDtypeStruct((D, D), bf16)],
        scratch_shapes=[pltpu.VMEM((D, D), f32)],
        compiler_params=_cp(1),
    )(dx2, yc, yg, mod, wout)


def final_fwd_bwd(x, tgt, fmod, g, tm, name):
    S = x.shape[0]

    def body(x_ref, t_ref, fm_ref, g_ref, dx_ref, red_ref):
        @pl.when(pl.program_id(0) == 0)
        def _():
            red_ref[...] = jnp.zeros_like(red_ref)

        xh, rstd = _rms_parts(x_ref[...])
        gv = g_ref[...]
        n = xh * gv
        sc = 1.0 + fm_ref[1:2, :]
        e = n * sc + fm_ref[0:1, :] - t_ref[...]
        red_ref[0:1, :] += _rowsum(e * e) * (0.5 / D)
        dy = e * (1.0 / D)
        dn = dy * sc
        red_ref[1:2, :] += _rowsum(dy)
        red_ref[2:3, :] += _rowsum(dy * n)
        red_ref[3:4, :] += _rowsum(dn * xh)
        dx_ref[...] = _rms_bwd(dn * gv, xh, rstd)

    row = lambda i: (i, 0)
    return pl.pallas_call(
        body, name=name,
        grid=(S // tm,),
        in_specs=[pl.BlockSpec((tm, D), row), pl.BlockSpec((tm, D), row), _full(fmod.shape), _full(g.shape)],
        out_specs=[pl.BlockSpec((tm, D), row), _full((8, D))],
        out_shape=[jax.ShapeDtypeStruct((S, D), f32), jax.ShapeDtypeStruct((8, D), f32)],
        compiler_params=_cp(1),
    )(x, tgt, fmod, g)


def ada_fwd(c_all, w, b, name):
    n = w.shape[1]

    def body(c_ref, w_ref, b_ref, o_ref):
        cv = c_ref[...]
        o_ref[...] = jnp.dot(cv * _sigmoid(cv), w_ref[...], preferred_element_type=f32, precision=HIGHEST) + b_ref[...]

    return pl.pallas_call(
        body, name=name,
        in_specs=[_full(c_all.shape), _full(w.shape), _full(b.shape)],
        out_specs=_full((N_DEV, n)),
        out_shape=jax.ShapeDtypeStruct((N_DEV, n), f32),
        grid=(1,),
        compiler_params=_cp(1),
    )(c_all, w, b)


def ada_wgrad(c_all_t, dm, name):
    n = dm.shape[1]

    def body(c_ref, d_ref, o_ref):
        cv = c_ref[...]
        o_ref[...] = jnp.dot(cv * _sigmoid(cv), d_ref[...], preferred_element_type=f32, precision=HIGHEST)

    return pl.pallas_call(
        body, name=name,
        in_specs=[_full(c_all_t.shape), _full(dm.shape)],
        out_specs=_full((D, n)),
        out_shape=jax.ShapeDtypeStruct((D, n), f32),
        grid=(1,),
        compiler_params=_cp(1),
    )(c_all_t, dm)


def _adam_math(gv, wv, mv, vv):
    m = ADAM_B1 * mv + (1.0 - ADAM_B1) * gv
    v = ADAM_B2 * vv + (1.0 - ADAM_B2) * (gv * gv)
    m_hat = m / (1.0 - ADAM_B1 ** ADAM_STEP)
    v_hat = v / (1.0 - ADAM_B2 ** ADAM_STEP)
    delta = -ADAM_LR * (m_hat / (jnp.sqrt(v_hat) + ADAM_EPS) + ADAM_WD * wv)
    return delta, m, v


def adam_parts(parts, w, m, v, tr, name, comm=None):
    L, R, C = w.shape
    nt = R // tr

    def body(*refs):
        p_refs = refs[:L]
        w_ref, m_ref, v_ref, g_ref, d_ref, mo_ref, vo_ref = refs[L:]
        lyr = pl.program_id(0)
        for l in range(L):
            @pl.when(lyr == l)
            def _(p_ref=p_refs[l]):
                gv = p_ref[0].astype(f32)
                for k in range(1, N_DEV):
                    gv = gv + p_ref[k].astype(f32)
                g_ref[...] = gv
                d_ref[...], mo_ref[...], vo_ref[...] = _adam_math(gv, w_ref[...], m_ref[...], v_ref[...])

    def part_spec(l):
        return pl.BlockSpec((N_DEV, tr, C), lambda lyr, i: (0, jnp.where(lyr == l, i, jnp.where(lyr < l, 0, nt - 1)), 0))

    spec = pl.BlockSpec((None, tr, C), lambda lyr, i: (lyr, i, 0))
    shp = jax.ShapeDtypeStruct((L, R, C), f32)
    return _pcall(
        body, (*parts, w, m, v), name=name, comm=comm,
        grid=(L, nt),
        in_specs=[part_spec(l) for l in range(L)] + [spec, spec, spec],
        out_specs=[spec, spec, spec, spec],
        out_shape=[shp, shp, shp, shp],
    )


def adam_plain(gr, w, m, v, tr, name):
    R, C = w.shape

    def body(g_ref, w_ref, m_ref, v_ref, d_ref, mo_ref, vo_ref):
        d_ref[...], mo_ref[...], vo_ref[...] = _adam_math(g_ref[...], w_ref[...], m_ref[...], v_ref[...])

    spec = pl.BlockSpec((tr, C), lambda i: (i, 0))
    shp = jax.ShapeDtypeStruct((R, C), f32)
    return pl.pallas_call(
        body, name=name,
        grid=(R // tr,),
        in_specs=[spec, spec, spec, spec],
        out_specs=[spec, spec, spec],
        out_shape=[shp, shp, shp],
        compiler_params=_cp(1),
    )(gr, w, m, v)


def sum8(parts, name):
    _, R, C = parts.shape

    def body(p_ref, o_ref):
        acc = p_ref[0]
        for k in range(1, N_DEV):
            acc = acc + p_ref[k]
        o_ref[...] = acc

    return pl.pallas_call(
        body, name=name,
        grid=(1,),
        in_specs=[_full(parts.shape)],
        out_specs=_full((R, C)),
        out_shape=jax.ShapeDtypeStruct((R, C), f32),
        compiler_params=_cp(1),
    )(parts)


def _place():
    return lax.axis_index("x"), lax.axis_index("y"), lax.axis_index("c")


def _gather_steps(ins, outs, send_sems, recv_sems, local_sems, place):
    n = len(ins)
    x, y, c = place
    me, sibling = (x, y, c), (x, y, 1 - c)
    chips = [(1 - x, y), (x, 1 - y), (1 - x, 1 - y)]

    def slot(a, p):
        return outs[a].at[4 * p[0] + 2 * p[1] + p[2]]

    def copy(a, k, block, to, src=None):
        return pltpu.make_async_remote_copy(
            src_ref=slot(a, block) if src is None else src, dst_ref=slot(a, block),
            send_sem=send_sems.at[a * 7 + k], recv_sem=recv_sems.at[a * 7 + k],
            device_id=to, device_id_type=MESH)

    def mine():
        return [pltpu.make_async_copy(ins[a], slot(a, me), local_sems.at[a]) for a in range(n)]

    def first():
        cps = []
        for a in range(n):
            cps.append(copy(a, 0, me, sibling, src=ins[a]))
            cps += [copy(a, 1 + j, me, (*chip, c), src=ins[a]) for j, chip in enumerate(chips)]
        return cps

    def start():
        for cp in mine() + first():
            cp.start()

    def forward():
        for j, chip in enumerate(chips):
            for a in range(n):
                copy(a, 1 + j, (*chip, c), me).wait_recv()
                copy(a, 4 + j, (*chip, c), sibling).start()

    def finish():
        for a in range(n):
            copy(a, 0, sibling, me).wait_recv()
            for j, chip in enumerate(chips):
                copy(a, 4 + j, (*chip, 1 - c), me).wait_recv()
        for cp in first() + [copy(a, 4 + j, (*chip, c), sibling) for j, chip in enumerate(chips) for a in range(n)]:
            cp.wait_send()
        for cp in mine():
            cp.wait()

    return start, forward, finish


def _exchange_steps(ins, outs, send_sems, recv_sems, local_sems, place):
    n = len(ins)
    x, y, c = place
    me_i = 4 * x + 2 * y + c

    def mine():
        return [pltpu.make_async_copy(ins[a].at[me_i], outs[a].at[me_i], local_sems.at[a]) for a in range(n)]

    def copies(receiving):
        cps = []
        for k in range(1, N_DEV):
            px = 1 - x if (k >> 2) & 1 else x
            py = 1 - y if (k >> 1) & 1 else y
            pc = 1 - c if k & 1 else c
            p_i = 4 * px + 2 * py + pc
            for a in range(n):
                sem = a * 7 + k - 1
                cps.append(pltpu.make_async_remote_copy(
                    src_ref=ins[a].at[p_i], dst_ref=outs[a].at[p_i if receiving else me_i],
                    send_sem=send_sems.at[sem], recv_sem=recv_sems.at[sem],
                    device_id=(px, py, pc), device_id_type=MESH))
        return cps

    def start():
        for cp in mine() + copies(False):
            cp.start()

    def finish():
        for cp in copies(True):
            cp.wait_recv()
        for cp in copies(False):
            cp.wait_send()
        for cp in mine():
            cp.wait()

    return start, None, finish


_COMM_STEPS = {"gather": _gather_steps, "exchange": _exchange_steps}


def _comm_out_shapes(kind, arrs):
    if kind == "gather":
        return [jax.ShapeDtypeStruct((N_DEV,) + a.shape, a.dtype) for a in arrs]
    return [jax.ShapeDtypeStruct(a.shape, a.dtype) for a in arrs]


def _comm_sems(n):
    return [pltpu.SemaphoreType.DMA((7 * n,)), pltpu.SemaphoreType.DMA((7 * n,)), pltpu.SemaphoreType.DMA((n,))]


def _pcall(body, args, *, name, grid, in_specs, out_specs, out_shape, scratch_shapes=(), comm=None):
    in_specs, out_specs, out_shape = list(in_specs), list(out_specs), list(out_shape)
    scratch_shapes = list(scratch_shapes)
    cparams = _cp(len(grid))
    if comm is None:
        outs = pl.pallas_call(body, name=name, grid=grid, in_specs=in_specs, out_specs=out_specs, out_shape=out_shape,
                              scratch_shapes=scratch_shapes, compiler_params=cparams)(*args)
        return list(outs), []
    kind, arrs = comm
    nc, n_in, n_out, n_scr = len(arrs), len(in_specs), len(out_specs), len(scratch_shapes)
    total = 1
    for gdim in grid:
        total *= gdim
    forward_step = (total * 3) // 4

    def hosted(*refs):
        core_in, c_in = refs[:n_in], refs[n_in:n_in + nc]
        core_out = refs[n_in + nc:n_in + nc + n_out]
        c_out = refs[n_in + nc + n_out:n_in + 2 * nc + n_out]
        rest = refs[n_in + 2 * nc + n_out:]
        step = pl.program_id(0)
        for ax in range(1, len(grid)):
            step = step * grid[ax] + pl.program_id(ax)
        start, forward, finish = _COMM_STEPS[kind](c_in, c_out, *rest[n_scr:], _place())
        pl.when(step == 0)(start)
        if forward is not None:
            pl.when(step == forward_step)(forward)
        body(*core_in, *core_out, *rest[:n_scr])
        pl.when(step == total - 1)(finish)

    any_spec = pl.BlockSpec(memory_space=pl.ANY)
    outs = pl.pallas_call(
        hosted, name=name, grid=grid,
        in_specs=in_specs + [any_spec] * nc,
        out_specs=out_specs + [any_spec] * nc,
        out_shape=out_shape + _comm_out_shapes(kind, arrs),
        scratch_shapes=scratch_shapes + _comm_sems(nc),
        compiler_params=cparams)(*args, *arrs)
    return list(outs[:n_out]), list(outs[n_out:])


def _comm_call(kind, arrs, name):
    n = len(arrs)

    def body(*refs):
        start, forward, finish = _COMM_STEPS[kind](refs[:n], refs[n:2 * n], *refs[2 * n:], _place())
        start()
        if forward is not None:
            forward()
        finish()

    any_spec = pl.BlockSpec(memory_space=pl.ANY)
    return pl.pallas_call(
        body, name=name,
        in_specs=[any_spec] * n, out_specs=[any_spec] * n,
        out_shape=_comm_out_shapes(kind, arrs), scratch_shapes=_comm_sems(n),
    )(*arrs)


def all_gather(arrs, name):
    return _comm_call("gather", arrs, name)


def all_to_all(arrs, name):
    return _comm_call("exchange", arrs, name)


def _tiles(S):
    t = min(512, S)
    return dict(ffn=min(256, S), row=t, conv=t, gla=t, bk=min(1024, S))


BIG = ("wi1", "wo1", "win", "wout", "wi2", "wo2")


def _col_shards_to_full(gathered):
    n, r, c = gathered.shape
    return jnp.transpose(gathered, (1, 0, 2)).reshape(r, n * c)


def _win_full(win_a):
    return _pad_rows(win_a.reshape(DIN, D), DINP)


def train_pass(x, tgt, mods, fmod, sh, ws, wi1_first, wo1_first):
    S = x.shape[0]
    T = _tiles(S)
    bk = T["bk"]
    full = [dict() for _ in range(DEPTH)]
    hosted = {("ffn1", 0): [("win", 0), ("wout", 0), ("wo2", 0)], ("core", 0): [("wi2", 0)]}
    for l in range(1, DEPTH):
        hosted[("mixin", l - 1)] = [("win", l), ("wout", l)]
        hosted[("ffn2", l - 1)] = [("wi1", l), ("wo1", l)]
        hosted[("ffn1", l)] = [("wi2", l), ("wo2", l)]

    def comm_for(key):
        return ("gather", [sh[n][ll] for n, ll in hosted[key]]) if key in hosted else None

    def keep(key, got):
        for (n, ll), gathered in zip(hosted.get(key, []), got):
            if n in ("wi1", "wi2"):
                full[ll][n] = gathered.reshape(2 * F, D)
            elif n in ("wo1", "wo2"):
                full[ll][n] = gathered.reshape(F, D)
            else:
                full[ll][n] = _win_full(gathered) if n == "win" else gathered.reshape(D, D)

    full[0]["wi1"], full[0]["wo1"] = wi1_first.reshape(2 * F, D), wo1_first.reshape(F, D)
    saved = []
    xc = x
    for l in range(DEPTH):
        w, fw = ws[f"L{l}"], full[l]
        x0 = xc
        (x1, h1f, z1, f1), got = ffn_fwd(x0, mods[l], w["g1"], fw["wi1"], fw["wo1"], (0, 1, 2), T["ffn"], f"ffn1_fwd_{l}",
                                         comm=comm_for(("ffn1", l)))
        keep(("ffn1", l), got)
        (z, la), got = mixin_fwd(x1, mods[l], w["g2"], fw["win"], w["wgu"], w["bgate"], T["row"], f"mixin_fwd_{l}",
                                 comm=comm_for(("mixin", l)))
        keep(("mixin", l), got)
        (y, yc, yg, sprev, bc, att, o), got = mixer_core_fwd(z, la, w["gn_s"], w["wdw"], w["cpar"], T["gla"],
                                                             f"mixer_core_fwd_{l}", comm=comm_for(("core", l)))
        keep(("core", l), got)
        x2 = mixout_fwd(x1, yc, yg, mods[l], fw["wout"], T["row"], f"mixout_fwd_{l}")
        (x3, h2f, z2, f2), got = ffn_fwd(x2, mods[l], w["g3"], fw["wi2"], fw["wo2"], (6, 7, 8), T["ffn"], f"ffn2_fwd_{l}",
                                         comm=comm_for(("ffn2", l)))
        keep(("ffn2", l), got)
        saved.append(dict(x0=x0, x1=x1, x2=x2, h1f=h1f, z1=z1, f1=f1, h2f=h2f, z2=z2, f2=f2,
                          z=z, la=la, y=y, yc=yc, yg=yg, sprev=sprev, bc=bc, att=att, o=o))
        xc = x3

    dx, redf = final_fwd_bwd(xc, tgt, fmod, ws["gf"], T["row"], "loss_head")
    loss_lanes = redf[0]
    dfmod = redf[1:3]
    grads = {"gf": redf[3]}
    dmods = [None] * DEPTH
    recv = {n: [None] * DEPTH for n in BIG}

    def ffn_backward(xin, dy, h, z, fo, gain, wi_t, wo, rows, l, tag, ride=None):
        (dz, p_wo), got_ride = ffn_bwd_hidden(dy, z, mods[l], wo.T, rows[2], T["ffn"], f"{tag}_bwd_hidden_{l}",
                                              comm=("exchange", ride) if ride else None)
        p_wi, (r_wo,) = dwi_pieces(h, dz, T["bk"], f"d{tag}_wi_{l}",
                                   comm=("exchange", [p_wo.reshape(N_DEV, F // N_DEV, D)]))
        (dxin, red), (r_wi,) = ffn_bwd_input(xin, dy, dz, fo, mods[l], gain, wi_t, rows, T["ffn"],
                                             f"{tag}_bwd_input_{l}", comm=("exchange", [p_wi]))
        return dxin, red, r_wi, r_wo, got_ride

    for l in reversed(range(DEPTH)):
        w, fw, sv = ws[f"L{l}"], full[l], saved[l]
        g = {}
        dx2, red3, recv["wi2"][l], recv["wo2"][l], _ = ffn_backward(
            sv["x2"], dx, sv["h2f"], sv["z2"], sv["f2"], w["g3"], fw["wi2"], fw["wo2"], (6, 7, 8), l, "ffn2")
        dyc, dyg, red_o, p_wout = mixout_bwd(dx2, sv["yc"], sv["yg"], mods[l], fw["wout"], T["row"], f"mixout_bwd_{l}")
        (dq, dk, dv, dr, dpre, redg, redb), (recv["wout"][l],) = gla_bwd(
            sv["z"], sv["la"], sv["sprev"], sv["bc"], sv["att"], sv["o"], dyg, w["gn_s"], T["gla"], f"gla_bwd_{l}",
            comm=("exchange", [p_wout.reshape(N_DEV, D // N_DEV, D)]))
        dzab, redc = conv_bwd(sv["z"], sv["y"], dyc, w["wdw"], w["cpar"], T["conv"], f"conv_bwd_{l}")
        (dx1, red2, dwin_t), _ = mixin_bwd(sv["x1"], dx2, dzab, dq, dk, dv, dr, dpre, mods[l], w["g2"], fw["win"], w["wgu"],
                                           T["ffn"], f"mixin_bwd_{l}")
        p_win = dwin_t[:DIN].reshape(N_DEV, DIN // N_DEV, D)
        g["wgu"] = matmul_tn(sv["z"], dpre, 128, DQK, 128, DQK, bk, f"dwgu_{l}", a_col_block=(DINP - 128) // 128)[:GR]
        g["bgate"] = jnp.sum(redb, axis=0)
        g["gn"] = jnp.sum(redg.reshape(NH, CH, DV), axis=1)
        g["wdw"] = redc[:CW]
        g["bdw"], g["gln"], g["bln"] = redc[32], redc[33], redc[34]
        dx0, red1, recv["wi1"][l], recv["wo1"][l], (recv["win"][l],) = ffn_backward(
            sv["x0"], dx1, sv["h1f"], sv["z1"], sv["f1"], w["g1"], fw["wi1"], fw["wo1"], (0, 1, 2), l, "ffn1",
            ride=[p_win])
        g["g1"], g["g2"], g["g3"] = red1[3], red2[2], red3[3]
        dmods[l] = jnp.stack([red1[0], red1[1], red1[2], red2[0], red2[1], red_o[0], red3[0], red3[1], red3[2]], axis=0)
        grads[f"L{l}"] = g
        dx = dx0
    return loss_lanes, dx, grads, dmods, dfmod, recv


def _pad_rows(a, rows):
    return jnp.pad(a, ((0, rows - a.shape[0]), (0, 0)))


def kernel(x, c, w_ada, b_ada, g_norm_ffn1, w_ffn1_in, w_ffn1_out, g_norm_mix, w_in, w_dw, b_dw, g_conv_ln, b_conv_ln, w_gate_up, b_gate, g_gla_norm, w_out, g_norm_ffn2, w_ffn2_in, w_ffn2_out, g_norm_final, w_ada_final, b_ada_final, loss_target, m_w_ada, m_b_ada, m_g_norm_ffn1, m_w_ffn1_in, m_w_ffn1_out, m_g_norm_mix, m_w_in, m_w_dw, m_b_dw, m_g_conv_ln, m_b_conv_ln, m_w_gate_up, m_b_gate, m_g_gla_norm, m_w_out, m_g_norm_ffn2, m_w_ffn2_in, m_w_ffn2_out, m_g_norm_final, m_w_ada_final, m_b_ada_final, v_w_ada, v_b_ada, v_g_norm_ffn1, v_w_ffn1_in, v_w_ffn1_out, v_g_norm_mix, v_w_in, v_w_dw, v_b_dw, v_g_conv_ln, v_b_conv_ln, v_w_gate_up, v_b_gate, v_g_gla_norm, v_w_out, v_g_norm_ffn2, v_w_ffn2_in, v_w_ffn2_out, v_g_norm_final, v_w_ada_final, v_b_ada_final):
    me = 4 * lax.axis_index("x") + 2 * lax.axis_index("y") + lax.axis_index("c")
    L = DEPTH
    n_ada = N_MOD * D // N_DEV
    n_fin = 2 * D // N_DEV

    small = jnp.concatenate([c.reshape(-1), w_dw.reshape(-1), w_gate_up.reshape(-1)])
    n_small = small.shape[0]
    small = jnp.pad(small, (0, 8 * D - n_small)).reshape(8, D)
    big = dict(wi1=w_ffn1_in, wo1=w_ffn1_out, win=w_in, wout=w_out, wi2=w_ffn2_in, wo2=w_ffn2_out)
    transposed = ("wi1", "wi2", "win")
    sh = {n: [(a[l].T if n in transposed else a[l]).astype(bf16) for l in range(L)] for n, a in big.items()}
    small_a, wi1_first, wo1_first = all_gather([small, sh["wi1"][0], sh["wo1"][0]], "gather_first")
    small_a = small_a.reshape(N_DEV, 8 * D)
    c_all = small_a[:, :D]
    o1 = D + L * CW * (DC // N_DEV)
    wdw_full = _col_shards_to_full(small_a[:, D:o1].reshape(N_DEV, L * CW, DC // N_DEV)).reshape(L, CW, DC)
    wgu_full = _col_shards_to_full(small_a[:, o1:o1 + L * GR * (DQK // N_DEV)].reshape(N_DEV, L * GR, DQK // N_DEV)).reshape(L, GR, DQK)

    b_ada_mine = lax.dynamic_slice(b_ada, (0, me * n_ada), (L, n_ada))
    b_fin_mine = lax.dynamic_slice(b_ada_final, (me * n_fin,), (n_fin,))
    parts = [ada_fwd(c_all, w_ada[l], b_ada_mine[l:l + 1], f"ada_fwd_{l}") for l in range(L)]
    parts.append(ada_fwd(c_all, w_ada_final, b_fin_mine.reshape(1, n_fin), "ada_fwd_final"))
    modsrc = jnp.concatenate(parts, axis=1)
    n_row = modsrc.shape[1]
    modsrc = jnp.pad(modsrc, ((0, 0), (0, 24 * 128 - n_row))).reshape(N_DEV, 24, 128)
    (modrecv,) = all_to_all([modsrc], "exchange_mod")
    modrecv = modrecv.reshape(N_DEV, 24 * 128)
    mods = []
    for l in range(L):
        mvec = modrecv[:, l * n_ada:(l + 1) * n_ada].reshape(N_MOD, D)
        mods.append(_pad_rows(mvec, 16))
    fmod = _pad_rows(modrecv[:, L * n_ada:L * n_ada + n_fin].reshape(2, D), 8)

    ws = {"gf": g_norm_final.reshape(1, D)}
    for l in range(L):
        ws[f"L{l}"] = dict(
            g1=g_norm_ffn1[l].reshape(1, D), g2=g_norm_mix[l].reshape(1, D), g3=g_norm_ffn2[l].reshape(1, D),
            wgu=_pad_rows(wgu_full[l], 128).astype(bf16),
            bgate=b_gate[l].reshape(1, DQK),
            wdw=_pad_rows(wdw_full[l], 32),
            cpar=_pad_rows(jnp.stack([b_dw[l], g_conv_ln[l], b_conv_ln[l]]), 8),
            gn_s=jnp.repeat(g_gla_norm[l], CH, axis=0),
        )

    loss_lanes, grad_x, gr, dmods, dfmod, recv = train_pass(
        x[0], loss_target[0], mods, fmod, sh, ws, wi1_first, wo1_first)

    def adam_big(rv, w, m, v, name, is_transposed=False):
        if is_transposed:
            w, m, v = (jnp.swapaxes(a, 1, 2) for a in (w, m, v))
        R = w.shape[1]
        tr = 256 if R % 256 == 0 else (R // 2 if (R // 2) % 16 == 0 else R)
        outs, _ = adam_parts(rv, w, m, v, tr, name)
        return [jnp.swapaxes(o, 1, 2) for o in outs] if is_transposed else outs

    res = {}
    res["w_ffn2_in"] = adam_big(recv["wi2"], w_ffn2_in, m_w_ffn2_in, v_w_ffn2_in, "adam_ffn2_in", True)
    res["w_ffn2_out"] = adam_big(recv["wo2"], w_ffn2_out, m_w_ffn2_out, v_w_ffn2_out, "adam_ffn2_out")
    res["w_in"] = adam_big(recv["win"], w_in, m_w_in, v_w_in, "adam_w_in", True)
    res["w_out"] = adam_big(recv["wout"], w_out, m_w_out, v_w_out, "adam_w_out")
    res["w_ffn1_out"] = adam_big(recv["wo1"], w_ffn1_out, m_w_ffn1_out, v_w_ffn1_out, "adam_ffn1_out")
    res["w_ffn1_in"] = adam_big(recv["wi1"], w_ffn1_in, m_w_ffn1_in, v_w_ffn1_in, "adam_ffn1_in", True)

    flat = lambda name: jnp.stack([gr[f"L{l}"][name] for l in range(L)]).reshape(-1)
    sections = [
        ("b_ada", jnp.stack(dmods).reshape(-1)), ("b_ada_final", dfmod.reshape(-1)),
        ("g_norm_ffn1", flat("g1")), ("g_norm_mix", flat("g2")), ("g_norm_ffn2", flat("g3")), ("g_norm_final", gr["gf"]),
        ("b_dw", flat("bdw")), ("g_conv_ln", flat("gln")), ("b_conv_ln", flat("bln")), ("b_gate", flat("bgate")),
        ("g_gla_norm", flat("gn")),
    ]
    n_rep = sum(s[1].shape[0] for s in sections)
    rep_rows = -(-n_rep // D)
    extra = [("loss", loss_lanes), ("w_dw", flat("wdw")), ("w_gate_up", flat("wgu"))]
    pack = jnp.concatenate([s[1] for s in sections] + [jnp.zeros((rep_rows * D - n_rep,), f32)] + [s[1] for s in extra])
    n_pack = pack.shape[0]
    pack_rows = -(-n_pack // (8 * D)) * 8
    pack = jnp.pad(pack, (0, pack_rows * D - n_pack)).reshape(pack_rows, D)
    (pack_all,) = all_gather([pack], "gather_small_grads")
    tot = sum8(pack_all, "sum_small_grads")
    tot_flat = tot.reshape(-1)
    loss = jnp.sum(tot_flat[rep_rows * D:rep_rows * D + D])
    o_dw = rep_rows * D + D
    g_wdw_full = tot_flat[o_dw:o_dw + L * CW * DC].reshape(L, CW, DC)
    o_gu = o_dw + L * CW * DC
    g_wgu_full = tot_flat[o_gu:o_gu + L * GR * DQK].reshape(L, GR, DQK)

    small_params = dict(b_ada=(b_ada, m_b_ada, v_b_ada), b_ada_final=(b_ada_final, m_b_ada_final, v_b_ada_final),
                        g_norm_ffn1=(g_norm_ffn1, m_g_norm_ffn1, v_g_norm_ffn1), g_norm_mix=(g_norm_mix, m_g_norm_mix, v_g_norm_mix),
                        g_norm_ffn2=(g_norm_ffn2, m_g_norm_ffn2, v_g_norm_ffn2), g_norm_final=(g_norm_final, m_g_norm_final, v_g_norm_final),
                        b_dw=(b_dw, m_b_dw, v_b_dw), g_conv_ln=(g_conv_ln, m_g_conv_ln, v_g_conv_ln),
                        b_conv_ln=(b_conv_ln, m_b_conv_ln, v_b_conv_ln), b_gate=(b_gate, m_b_gate, v_b_gate),
                        g_gla_norm=(g_gla_norm, m_g_gla_norm, v_g_gla_norm))

    def rep_pack(idx):
        p = jnp.concatenate([small_params[s[0]][idx].reshape(-1) for s in sections])
        return jnp.pad(p, (0, rep_rows * D - n_rep)).reshape(rep_rows, D)

    g_rep = tot[:rep_rows]
    d_rep, m_rep, v_rep = adam_plain(g_rep, rep_pack(0), rep_pack(1), rep_pack(2), rep_rows, "adam_small")
    off = 0
    for sname, sval in sections:
        shp = small_params[sname][0].shape
        nel = sval.shape[0]
        res[sname] = [a.reshape(-1)[off:off + nel].reshape(shp) for a in (g_rep, d_rep, m_rep, v_rep)]
        off += nel

    def adam_cols(g_full, w, m, v, name):
        shp = w.shape
        g_mine = lax.dynamic_slice(g_full, (0, 0, me * shp[2]), shp)
        R, C = shp[0] * shp[1], shp[2]
        outs = adam_plain(g_mine.reshape(R, C), w.reshape(R, C), m.reshape(R, C), v.reshape(R, C), R, name)
        return [g_mine] + [o.reshape(shp) for o in outs]

    res["w_dw"] = adam_cols(g_wdw_full, w_dw, m_w_dw, v_w_dw, "adam_w_dw")
    res["w_gate_up"] = adam_cols(g_wgu_full, w_gate_up, m_w_gate_up, v_w_gate_up, "adam_w_gate_up")

    c_all_t = c_all.T
    dmod_all = pack_all.reshape(N_DEV, -1)[:, :L * N_MOD * D].reshape(N_DEV, L, N_MOD * D)
    dfm_all = pack_all.reshape(N_DEV, -1)[:, L * N_MOD * D:L * N_MOD * D + 2 * D]
    dm_mine = lax.dynamic_slice(dmod_all, (0, 0, me * n_ada), (N_DEV, L, n_ada))
    dfm_mine = lax.dynamic_slice(dfm_all, (0, me * n_fin), (N_DEV, n_fin))
    g_w_ada = jnp.stack([ada_wgrad(c_all_t, dm_mine[:, l], f"ada_wgrad_{l}") for l in range(L)])
    g_w_fin = ada_wgrad(c_all_t, dfm_mine, "ada_wgrad_final")
    outs = adam_plain(g_w_ada.reshape(L * D, n_ada), w_ada.reshape(L * D, n_ada), m_w_ada.reshape(L * D, n_ada),
                      v_w_ada.reshape(L * D, n_ada), 256, "adam_w_ada")
    res["w_ada"] = [g_w_ada] + [o.reshape(w_ada.shape) for o in outs]
    res["w_ada_final"] = [g_w_fin] + list(adam_plain(g_w_fin, w_ada_final, m_w_ada_final, v_w_ada_final, 256, "adam_w_ada_final"))

    order = ["w_ada", "b_ada", "g_norm_ffn1", "w_ffn1_in", "w_ffn1_out", "g_norm_mix", "w_in", "w_dw", "b_dw", "g_conv_ln",
             "b_conv_ln", "w_gate_up", "b_gate", "g_gla_norm", "w_out", "g_norm_ffn2", "w_ffn2_in", "w_ffn2_out",
             "g_norm_final", "w_ada_final", "b_ada_final"]
    out = [loss, grad_x[None]]
    for k in range(4):
        out += [res[name][k] for name in order]
    return tuple(out)
```

```python
import functools

import jax
import jax.numpy as jnp
from jax import lax
from jax.experimental import pallas as pl
from jax.experimental.pallas import tpu as pltpu

f32 = jnp.float32
bf16 = jnp.bfloat16

N_DEV = 8
DEPTH = 2
D = 1024
F = 2816
DC = 512
NH = 4
DK = 64
DV = 128
DQK = NH * DK
DG = NH * DV
CH = 64
CW = 31
GR = 16
TAU = 16.0
N_MOD = 9
DIN = 2 * DC + 2 * DQK + 2 * DG + GR
DINP = 2688
EPS = 1e-6
HALO = 32
SUBLANES = 8
CONV_ROWS = 32
FS = 2 * F // N_DEV

ADAM_LR = 0.001
ADAM_B1 = 0.9
ADAM_B2 = 0.999
ADAM_EPS = 1e-08
ADAM_WD = 0.01
ADAM_STEP = 10

V7X_VMEM_LIMIT = 56 * 1024 * 1024
MESH = pl.DeviceIdType.MESH
HIGHEST = lax.Precision.HIGHEST

NT = (((1,), (1,)), ((), ()))
TN = (((0,), (0,)), ((), ()))


def _cp(n_axes):
    return pltpu.CompilerParams(dimension_semantics=("arbitrary",) * n_axes, vmem_limit_bytes=V7X_VMEM_LIMIT)


def _full(shape):
    nd = len(shape)
    return pl.BlockSpec(shape, lambda *_: (0,) * nd)


def _resident(shape):
    nd = len(shape)
    return pl.BlockSpec(shape, lambda *_: (0,) * nd, pipeline_mode=pl.Buffered(1))


def _dot(a, b):
    return jnp.dot(a, b, preferred_element_type=f32)


def _dg(a, b, dims):
    return lax.dot_general(a, b, dims, preferred_element_type=f32)


def _sigmoid(x):
    return jax.nn.sigmoid(x)


def _rowsum(x):
    return jnp.sum(x, axis=0, keepdims=True)


def _rms_parts(xv):
    rstd = lax.rsqrt(jnp.mean(xv * xv, axis=-1, keepdims=True) + EPS)
    return xv * rstd, rstd


def _rms_bwd(dxh, xh, rstd):
    return rstd * (dxh - xh * jnp.mean(dxh * xh, axis=-1, keepdims=True))


def ffn_fwd(x, mod, g, wi_t, wo, rows, tm, name, comm=None):
    S = x.shape[0]
    r_shift, r_scale, r_gate = rows

    def body(x_ref, mod_ref, g_ref, wi_ref, wo_ref, xo_ref, h_ref, z_ref, f_ref):
        xv = x_ref[...]
        xh, _ = _rms_parts(xv)
        h = (xh * g_ref[...] * (1.0 + mod_ref[r_scale:r_scale + 1, :]) + mod_ref[r_shift:r_shift + 1, :]).astype(bf16)
        h_ref[...] = h
        zg = _dg(h, wi_ref[0:F, :], NT)
        zu = _dg(h, wi_ref[F:2 * F, :], NT)
        z_ref[:, 0:F] = zg.astype(bf16)
        z_ref[:, F:2 * F] = zu.astype(bf16)
        fv = _dot((zg * _sigmoid(zg) * zu).astype(bf16), wo_ref[...])
        f_ref[...] = fv.astype(bf16)
        xo_ref[...] = xv + 0.5 * mod_ref[r_gate:r_gate + 1, :] * fv

    row = lambda i: (i, 0)
    return _pcall(
        body, (x, mod, g, wi_t, wo), name=name, comm=comm,
        grid=(S // tm,),
        in_specs=[pl.BlockSpec((tm, D), row), _full(mod.shape), _full(g.shape), _resident(wi_t.shape), _resident(wo.shape)],
        out_specs=[pl.BlockSpec((tm, D), row), pl.BlockSpec((tm, D), row), pl.BlockSpec((tm, 2 * F), row),
                   pl.BlockSpec((tm, D), row)],
        out_shape=[jax.ShapeDtypeStruct((S, D), f32), jax.ShapeDtypeStruct((S, D), bf16),
                   jax.ShapeDtypeStruct((S, 2 * F), bf16), jax.ShapeDtypeStruct((S, D), bf16)],
    )


def ffn_bwd_hidden(dy, z, mod, wo_t, r_gate, tm, name, comm=None):
    S = dy.shape[0]
    nt = S // tm
    halves = 11
    fc = F // halves

    def body(dy_ref, z_ref, mod_ref, wo_ref, dz_ref, dwo_ref, acc_s):
        i = pl.program_id(0)

        @pl.when(i == 0)
        def _():
            acc_s[...] = jnp.zeros_like(acc_s)

        df = (0.5 * mod_ref[r_gate:r_gate + 1, :] * dy_ref[...]).astype(bf16)
        for c in range(halves):
            lo, hi = c * fc, (c + 1) * fc
            zgv = z_ref[:, lo:hi].astype(f32)
            zuv = z_ref[:, F + lo:F + hi].astype(f32)
            s = _sigmoid(zgv)
            sil = zgv * s
            acc_s[lo:hi, :] += _dg((sil * zuv).astype(bf16), df, TN)
            da = _dot(df, wo_ref[:, lo:hi])
            dz_ref[:, F + lo:F + hi] = (da * sil).astype(bf16)
            dz_ref[:, lo:hi] = (da * zuv * (s * (1.0 + zgv * (1.0 - s)))).astype(bf16)

        @pl.when(i == nt - 1)
        def _():
            dwo_ref[...] = acc_s[...].astype(bf16)

    row = lambda i: (i, 0)
    return _pcall(
        body, (dy, z, mod, wo_t), name=name, comm=comm,
        grid=(nt,),
        in_specs=[pl.BlockSpec((tm, D), row), pl.BlockSpec((tm, 2 * F), row), _full(mod.shape), _resident(wo_t.shape)],
        out_specs=[pl.BlockSpec((tm, 2 * F), row), _full((F, D))],
        out_shape=[jax.ShapeDtypeStruct((S, 2 * F), bf16), jax.ShapeDtypeStruct((F, D), bf16)],
        scratch_shapes=[pltpu.VMEM((F, D), f32)],
    )


def ffn_bwd_input(x, dy, dz, fo, mod, g, wi_t, rows, tm, name, comm=None):
    S = x.shape[0]
    r_shift, r_scale, r_gate = rows

    def body(x_ref, dy_ref, dz_ref, f_ref, mod_ref, g_ref, wi_ref, dx_ref, red_ref):
        @pl.when(pl.program_id(0) == 0)
        def _():
            red_ref[...] = jnp.zeros_like(red_ref)

        dh = _dot(dz_ref[...], wi_ref[...])
        dyv = dy_ref[...]
        xh, rstd = _rms_parts(x_ref[...])
        gv = g_ref[...]
        n = xh * gv
        dn = dh * (1.0 + mod_ref[r_scale:r_scale + 1, :])
        red_ref[0:1, :] += _rowsum(dh)
        red_ref[1:2, :] += _rowsum(dh * n)
        red_ref[2:3, :] += _rowsum(0.5 * f_ref[...].astype(f32) * dyv)
        red_ref[3:4, :] += _rowsum(dn * xh)
        dx_ref[...] = dyv + _rms_bwd(dn * gv, xh, rstd)

    row = lambda i: (i, 0)
    return _pcall(
        body, (x, dy, dz, fo, mod, g, wi_t), name=name, comm=comm,
        grid=(S // tm,),
        in_specs=[pl.BlockSpec((tm, D), row), pl.BlockSpec((tm, D), row), pl.BlockSpec((tm, 2 * F), row),
                  pl.BlockSpec((tm, D), row), _full(mod.shape), _full(g.shape), _resident(wi_t.shape)],
        out_specs=[pl.BlockSpec((tm, D), row), _full((8, D))],
        out_shape=[jax.ShapeDtypeStruct((S, D), f32), jax.ShapeDtypeStruct((8, D), f32)],
    )


def matmul_tn(a, b, M, N, bm, bn, bk, name, a_col_block=0, out_dtype=f32):
    S = b.shape[0]
    nk = S // bk

    def body(a_ref, b_ref, o_ref, acc_s):
        k = pl.program_id(2)

        @pl.when(k == 0)
        def _():
            acc_s[...] = jnp.zeros_like(acc_s)

        acc_s[...] += _dg(a_ref[...].astype(bf16), b_ref[...].astype(bf16), TN)

        @pl.when(k == nk - 1)
        def _():
            o_ref[...] = acc_s[...].astype(out_dtype)

    return pl.pallas_call(
        body, name=name,
        grid=(M // bm, N // bn, nk),
        in_specs=[
            pl.BlockSpec((bk, bm), lambda i, j, k: (k, i + a_col_block)),
            pl.BlockSpec((bk, bn), lambda i, j, k: (k, j)),
        ],
        out_specs=pl.BlockSpec((bm, bn), lambda i, j, k: (i, j)),
        out_shape=jax.ShapeDtypeStruct((M, N), out_dtype),
        scratch_shapes=[pltpu.VMEM((bm, bn), f32)],
        compiler_params=_cp(3),
    )(a, b)


def dwi_pieces(h, dz, bk, name, comm=None):
    S = h.shape[0]
    nk = S // bk

    def body(h_ref, dz_ref, o_ref, acc_s):
        k = pl.program_id(1)

        @pl.when(k == 0)
        def _():
            acc_s[...] = jnp.zeros_like(acc_s)

        acc_s[...] += _dg(dz_ref[...], h_ref[...], TN)

        @pl.when(k == nk - 1)
        def _():
            o_ref[...] = acc_s[...].astype(bf16)

    (out,), comm_outs = _pcall(
        body, (h, dz), name=name, comm=comm,
        grid=(2, nk),
        in_specs=[pl.BlockSpec((bk, D), lambda half, k: (k, 0)), pl.BlockSpec((bk, F), lambda half, k: (k, half))],
        out_specs=[pl.BlockSpec((F, D), lambda half, k: (half, 0))],
        out_shape=[jax.ShapeDtypeStruct((2 * F, D), bf16)],
        scratch_shapes=[pltpu.VMEM((F, D), f32)],
    )
    return out.reshape(N_DEV, FS, D), comm_outs


def mixin_fwd(x1, mod, g, win, wgu, bgate, tm, name, comm=None):
    S = x1.shape[0]

    def body(x_ref, mod_ref, g_ref, win_ref, wgu_ref, bg_ref, z_ref, la_ref):
        xh, _ = _rms_parts(x_ref[...])
        hv = xh * g_ref[...] * (1.0 + mod_ref[4:5, :]) + mod_ref[3:4, :]
        z = _dg(hv.astype(bf16), win_ref[...], NT).astype(bf16)
        z_ref[...] = z
        pre = _dot(z[:, DINP - 128:], wgu_ref[...]) + bg_ref[...]
        la_ref[...] = (jnp.minimum(pre, 0.0) - jnp.log(1.0 + jnp.exp(-jnp.abs(pre)))) * (1.0 / TAU)

    return _pcall(
        body, (x1, mod, g, win, wgu, bgate), name=name, comm=comm,
        grid=(S // tm,),
        in_specs=[pl.BlockSpec((tm, D), lambda i: (i, 0)), _full(mod.shape), _full(g.shape),
                  _full(win.shape), _full(wgu.shape), _full(bgate.shape)],
        out_specs=[pl.BlockSpec((tm, DINP), lambda i: (i, 0)), pl.BlockSpec((tm, DQK), lambda i: (i, 0))],
        out_shape=[jax.ShapeDtypeStruct((S, DINP), bf16), jax.ShapeDtypeStruct((S, DQK), f32)],
    )


def mixin_bwd(x1, dres, dzab, dq, dk, dv, dr, dpre, mod, g, win, wgu, tm, name, comm=None):
    S = x1.shape[0]
    nt = S // tm

    def body(x_ref, dres_ref, dzab_ref, dq_ref, dk_ref, dv_ref, dr_ref, dpre_ref, mod_ref, g_ref, win_ref, wgu_ref,
             dx_ref, red_ref, dw_ref, acc_s):
        i = pl.program_id(0)

        @pl.when(i == 0)
        def _():
            red_ref[...] = jnp.zeros_like(red_ref)
            acc_s[...] = jnp.zeros_like(acc_s)

        dglr = _dg(dpre_ref[...], wgu_ref[...], NT).astype(bf16)
        dz = jnp.concatenate([dzab_ref[...], dq_ref[...], dk_ref[...], dv_ref[...], dr_ref[...], dglr], axis=1)
        dh = _dot(dz, win_ref[...])
        xh, rstd = _rms_parts(x_ref[...])
        gv = g_ref[...]
        n = xh * gv
        sc = 1.0 + mod_ref[4:5, :]
        acc_s[...] += _dg(dz, (n * sc + mod_ref[3:4, :]).astype(bf16), TN)
        dn = dh * sc
        red_ref[0:1, :] += _rowsum(dh)
        red_ref[1:2, :] += _rowsum(dh * n)
        red_ref[2:3, :] += _rowsum(dn * xh)
        dx_ref[...] = dres_ref[...] + _rms_bwd(dn * gv, xh, rstd)

        @pl.when(i == nt - 1)
        def _():
            dw_ref[...] = acc_s[...].astype(bf16)

    row = lambda i: (i, 0)
    return _pcall(
        body, (x1, dres, dzab, dq, dk, dv, dr, dpre, mod, g, win, wgu), name=name, comm=comm,
        grid=(nt,),
        in_specs=[pl.BlockSpec((tm, D), row), pl.BlockSpec((tm, D), row),
                  pl.BlockSpec((tm, 2 * DC), row), pl.BlockSpec((tm, DQK), row), pl.BlockSpec((tm, DQK), row),
                  pl.BlockSpec((tm, DG), row), pl.BlockSpec((tm, DG), row), pl.BlockSpec((tm, DQK), row),
                  _full(mod.shape), _full(g.shape), _resident(win.shape), _full(wgu.shape)],
        out_specs=[pl.BlockSpec((tm, D), row), _full((8, D)), _full((DINP, D))],
        out_shape=[jax.ShapeDtypeStruct((S, D), f32), jax.ShapeDtypeStruct((8, D), f32),
                   jax.ShapeDtypeStruct((DINP, D), bf16)],
        scratch_shapes=[pltpu.VMEM((DINP, D), f32)],
    )


def _glu(zab):
    zab = zab.astype(f32)
    return zab[:, :DC] * _sigmoid(zab[:, DC:])


def _shift_copies(src_s, dst_s, tc):
    n = tc + HALO - SUBLANES
    for b in range(1, SUBLANES):
        dst_s[b, 0:n, :] = src_s[b:b + n, :]


def _shifted(src_s, dst_s, o, tc):
    b = o % SUBLANES
    a = o - b
    return src_s[a:a + tc, :] if b == 0 else dst_s[b, a:a + tc, :]


def _conv_fwd_prepare(first, zc_ref, zp_ref, u_s, us_s, tc):
    up = _glu(zp_ref[...])
    u_s[0:HALO, :] = jnp.where(first, 0.0, up)
    u_s[HALO:HALO + tc, :] = _glu(zc_ref[...])
    _shift_copies(u_s, us_s, tc)


def _conv_fwd_rows(r0, n, w_ref, cp_ref, y_ref, yc_ref, u_s, us_s):
    for r in range(r0, r0 + n, CONV_ROWS):
        acc = _shifted(u_s, us_s, HALO - (CW - 1) + r, CONV_ROWS) * w_ref[0:1, :]
        for w in range(1, CW):
            acc = acc + _shifted(u_s, us_s, HALO - (CW - 1) + w + r, CONV_ROWS) * w_ref[w:w + 1, :]
        y = acc + cp_ref[0:1, :]
        y_ref[r:r + CONV_ROWS, :] = y
        yc = y - jnp.mean(y, axis=-1, keepdims=True)
        yl = yc * lax.rsqrt(jnp.mean(yc * yc, axis=-1, keepdims=True) + EPS) * cp_ref[1:2, :] + cp_ref[2:3, :]
        yc_ref[r:r + CONV_ROWS, :] = (yl * _sigmoid(yl)).astype(bf16)


def _conv_bwd_prepare(first, last, zc_ref, zp_ref, y_ref, yn_ref, d_ref, dn_ref, cp_ref, red_ref, u_s, dy_s, us_s, dys_s, tc):
    gl = cp_ref[1:2, :]
    bl = cp_ref[2:3, :]

    def ln_bwd(yv, dv):
        yc = yv - jnp.mean(yv, axis=-1, keepdims=True)
        rstd = lax.rsqrt(jnp.mean(yc * yc, axis=-1, keepdims=True) + EPS)
        yh = yc * rstd
        yl = yh * gl + bl
        s = _sigmoid(yl)
        dyl = dv * (s * (1.0 + yl * (1.0 - s)))
        dyh = dyl * gl
        dyv = rstd * (dyh - jnp.mean(dyh, axis=-1, keepdims=True) - yh * jnp.mean(dyh * yh, axis=-1, keepdims=True))
        return dyv, dyl, yh

    dy_c, dyl_c, yh_c = ln_bwd(y_ref[...], d_ref[...])
    dy_n, _, _ = ln_bwd(yn_ref[...], dn_ref[...])
    dy_s[0:tc, :] = dy_c
    dy_s[tc:tc + HALO, :] = jnp.where(last, 0.0, dy_n)
    u_s[0:HALO, :] = jnp.where(first, 0.0, _glu(zp_ref[...]))
    u_s[HALO:HALO + tc, :] = _glu(zc_ref[...])
    _shift_copies(u_s, us_s, tc)
    _shift_copies(dy_s, dys_s, tc)
    red_ref[32:33, :] += _rowsum(dy_c)
    red_ref[33:34, :] += _rowsum(dyl_c * yh_c)
    red_ref[34:35, :] += _rowsum(dyl_c)


def _conv_bwd_input_rows(r0, n, zc_ref, w_ref, dz_ref, dy_s, dys_s):
    for r in range(r0, r0 + n, CONV_ROWS):
        du = _shifted(dy_s, dys_s, CW - 1 + r, CONV_ROWS) * w_ref[0:1, :]
        for w in range(1, CW):
            du = du + _shifted(dy_s, dys_s, CW - 1 - w + r, CONV_ROWS) * w_ref[w:w + 1, :]
        zc = zc_ref[r:r + CONV_ROWS, :].astype(f32)
        av = zc[:, :DC]
        sb = _sigmoid(zc[:, DC:])
        dz_ref[r:r + CONV_ROWS, :] = jnp.concatenate([du * sb, du * av * sb * (1.0 - sb)], axis=1).astype(dz_ref.dtype)


def _conv_bwd_taps(w0, w1, red_ref, u_s, us_s, dy_s, tc):
    for w in range(w0, w1):
        part = None
        for r in range(0, tc, CONV_ROWS):
            prod = _shifted(u_s, us_s, HALO - (CW - 1) + w + r, CONV_ROWS) * dy_s[r:r + CONV_ROWS, :]
            fold = jnp.sum(prod.reshape(CONV_ROWS // SUBLANES, SUBLANES, DC), axis=0)
            part = fold if part is None else part + fold
        red_ref[w:w + 1, :] += _rowsum(part)


def conv_bwd(z, y, dyc, wdw, cpar, tc, name):
    S = z.shape[0]
    nb = tc // HALO
    nt = S // tc
    last_halo = S // HALO - 1

    def body(zc_ref, zp_ref, y_ref, yn_ref, d_ref, dn_ref, w_ref, cp_ref, dz_ref, red_ref, u_s, dy_s, us_s, dys_s):
        i = pl.program_id(0)

        @pl.when(i == 0)
        def _():
            red_ref[...] = jnp.zeros_like(red_ref)

        _conv_bwd_prepare(i == 0, i == nt - 1, zc_ref, zp_ref, y_ref, yn_ref, d_ref, dn_ref, cp_ref, red_ref,
                          u_s, dy_s, us_s, dys_s, tc)
        _conv_bwd_input_rows(0, tc, zc_ref, w_ref, dz_ref, dy_s, dys_s)
        _conv_bwd_taps(0, CW, red_ref, u_s, us_s, dy_s, tc)

    cur = lambda i: (i, 0)
    nxt = lambda i: (jnp.minimum((i + 1) * nb, last_halo), 0)
    return pl.pallas_call(
        body, name=name,
        grid=(nt,),
        in_specs=[pl.BlockSpec((tc, 2 * DC), cur),
                  pl.BlockSpec((HALO, 2 * DC), lambda i: (jnp.maximum(i * nb - 1, 0), 0)),
                  pl.BlockSpec((tc, DC), cur), pl.BlockSpec((HALO, DC), nxt),
                  pl.BlockSpec((tc, DC), cur), pl.BlockSpec((HALO, DC), nxt),
                  _full(wdw.shape), _full(cpar.shape)],
        out_specs=[pl.BlockSpec((tc, 2 * DC), cur), _full((40, DC))],
        out_shape=[jax.ShapeDtypeStruct((S, 2 * DC), bf16), jax.ShapeDtypeStruct((40, DC), f32)],
        scratch_shapes=[pltpu.VMEM((HALO + tc, DC), f32), pltpu.VMEM((tc + HALO, DC), f32),
                        pltpu.VMEM((SUBLANES, HALO + tc, DC), f32), pltpu.VMEM((SUBLANES, HALO + tc, DC), f32)],
        compiler_params=_cp(1),
    )(z, z, y, y, dyc, dyc, wdw, cpar)


def _gla_consts():
    r = lax.broadcasted_iota(jnp.int32, (CH, CH), 0)
    c = lax.broadcasted_iota(jnp.int32, (CH, CH), 1)
    tril = r >= c
    lane = lax.broadcasted_iota(jnp.int32, (CH, DQK), 1)
    masks = [(lane >= h * DK) & (lane < (h + 1) * DK) for h in range(NH)]
    r4 = lax.broadcasted_iota(jnp.int32, (DQK, DQK), 0)
    c4 = lax.broadcasted_iota(jnp.int32, (DQK, DQK), 1)
    eye4 = (r4 == c4).astype(f32)
    rs = lax.broadcasted_iota(jnp.int32, (DQK, CH), 0) & (CH - 1)
    tril4 = rs >= lax.broadcasted_iota(jnp.int32, (DQK, CH), 1)
    return tril, tril4, masks, eye4


def _stack(xv, masks):
    return jnp.concatenate([jnp.where(m, xv, 0.0) for m in masks], axis=0)


def _unstack(rv, masks):
    out = jnp.where(masks[0], rv[0:CH, :], 0.0)
    for h in range(1, NH):
        out = out + jnp.where(masks[h], rv[h * CH:(h + 1) * CH, :], 0.0)
    return out


def _vstack(xv):
    return jnp.concatenate([xv[:, h * DV:(h + 1) * DV] for h in range(NH)], axis=0)


def _vunstack(xv):
    return jnp.concatenate([xv[h * CH:(h + 1) * CH, :] for h in range(NH)], axis=1)


def _gla_chunk_decay(bc, qc, kc, masks):
    qc, kc = qc.astype(f32), kc.astype(f32)
    bend = bc[CH - 1:CH, :]
    eb = jnp.exp(bc)
    enb = jnp.exp(-bc)
    ed = jnp.exp(bend - bc)
    qh = qc * (DK ** -0.5)
    qf = qh * eb
    qn = qh * enb
    kn = kc * enb
    kp = kc * eb
    kd = kc * ed
    return dict(bc=bc, bend=bend, eb=eb, enb=enb, ed=ed, qf=qf, qn=qn, kn=kn, kp=kp, kd=kd,
                qf_s=_stack(qf, masks).astype(bf16), qn_s=_stack(qn, masks).astype(bf16),
                kn_b=kn.astype(bf16), kp_b=kp.astype(bf16))


def _gla_chunk_fwd(lac, qc, kc, vc, s_all, tril, masks, tril4):
    qc, kc = qc.astype(f32), kc.astype(f32)
    lmat = tril.astype(f32)
    bc = jnp.dot(lmat, lac, preferred_element_type=f32, precision=HIGHEST)
    bend = bc[CH - 1:CH, :]
    eb = jnp.exp(bc)
    enb = jnp.exp(-bc)
    ed = jnp.exp(bend - bc)
    qh = qc * (DK ** -0.5)
    qf = qh * eb
    qn = qh * enb
    kn = kc * enb
    kp = kc * eb
    kd = kc * ed
    qf_s = _stack(qf, masks).astype(bf16)
    qn_s = _stack(qn, masks).astype(bf16)
    kn_b = kn.astype(bf16)
    kp_b = kp.astype(bf16)
    attf = _dg(qf_s, kn_b, NT)
    attb = _dg(qn_s, kp_b, NT)
    a_s = jnp.where(tril4, attf, attb)
    a_b = a_s.astype(bf16)
    v_b = vc.astype(bf16)
    intra = jnp.concatenate(
        [_dot(a_b[h * CH:(h + 1) * CH, :], v_b[:, h * DV:(h + 1) * DV]) for h in range(NH)], axis=0)
    o_s = intra + _dot(qf_s, s_all.astype(bf16))
    return dict(bc=bc, bend=bend, eb=eb, enb=enb, ed=ed, qf=qf, qn=qn, kn=kn, kp=kp, kd=kd,
                qf_s=qf_s, qn_s=qn_s, kn_b=kn_b, kp_b=kp_b, a_b=a_b, v_b=v_b, o_s=o_s)


def _col_from_row(row, eye4):
    return jnp.sum(eye4 * row, axis=1, keepdims=True)


def _row_from_col(col, eye4):
    return jnp.sum(eye4 * col, axis=0, keepdims=True)


def _gla_fwd_chunk(c, consts, q_ref, k_ref, v_ref, r_ref, la_ref, gn_ref, yg_ref, sp_ref, bc_ref, as_ref, os_ref, st):
    tril, tril4, masks, eye4 = consts
    r0, s0 = c * CH, c * DQK
    s_all = st[...]
    sp_ref[s0:s0 + DQK, :] = s_all
    vc = v_ref[r0:r0 + CH, :]
    t = _gla_chunk_fwd(la_ref[r0:r0 + CH, :], q_ref[r0:r0 + CH, :], k_ref[r0:r0 + CH, :], vc,
                       s_all, tril, masks, tril4)
    u_all = _dg(_stack(t["kd"], masks).astype(bf16), _vstack(vc).astype(bf16), TN)
    st[...] = _col_from_row(jnp.exp(t["bend"]), eye4) * s_all + u_all
    o_s = t["o_s"]
    bc_ref[r0:r0 + CH, :] = t["bc"]
    as_ref[s0:s0 + DQK, :] = t["a_b"]
    os_ref[s0:s0 + DQK, :] = o_s
    on = o_s * lax.rsqrt(jnp.mean(o_s * o_s, axis=-1, keepdims=True) + EPS) * gn_ref[...]
    rc = r_ref[r0:r0 + CH, :].astype(f32)
    yg_ref[r0:r0 + CH, :] = (_vunstack(on) * (rc * _sigmoid(rc))).astype(bf16)


def mixer_core_fwd(z, la, gn_s, wdw, cpar, t, name, comm=None):
    S = z.shape[0]
    nb = t // HALO
    nc = t // CH

    def body(zc_ref, zp_ref, q_ref, k_ref, v_ref, r_ref, la_ref, gn_ref, w_ref, cp_ref,
             y_ref, yc_ref, yg_ref, sp_ref, bc_ref, as_ref, os_ref, st, u_s, us_s):
        i = pl.program_id(0)

        @pl.when(i == 0)
        def _():
            st[...] = jnp.zeros_like(st)

        _conv_fwd_prepare(i == 0, zc_ref, zp_ref, u_s, us_s, t)
        consts = _gla_consts()
        for c in range(nc):
            _conv_fwd_rows(c * CH, CH, w_ref, cp_ref, y_ref, yc_ref, u_s, us_s)
            _gla_fwd_chunk(c, consts, q_ref, k_ref, v_ref, r_ref, la_ref, gn_ref, yg_ref, sp_ref, bc_ref, as_ref, os_ref, st)

    row = lambda i: (i, 0)
    return _pcall(
        body, (z, z, z, z, z, z, la, gn_s, wdw, cpar), name=name, comm=comm,
        grid=(S // t,),
        in_specs=[pl.BlockSpec((t, 2 * DC), row),
                  pl.BlockSpec((HALO, 2 * DC), lambda i: (jnp.maximum(i * nb - 1, 0), 0)),
                  pl.BlockSpec((t, DQK), lambda i: (i, 4)), pl.BlockSpec((t, DQK), lambda i: (i, 5)),
                  pl.BlockSpec((t, DG), lambda i: (i, 3)), pl.BlockSpec((t, DG), lambda i: (i, 4)),
                  pl.BlockSpec((t, DQK), row), _full(gn_s.shape), _full(wdw.shape), _full(cpar.shape)],
        out_specs=[pl.BlockSpec((t, DC), row), pl.BlockSpec((t, DC), row), pl.BlockSpec((t, DG), row),
                   pl.BlockSpec((nc * DQK, DV), row), pl.BlockSpec((t, DQK), row),
                   pl.BlockSpec((nc * DQK, CH), row), pl.BlockSpec((nc * DQK, DV), row)],
        out_shape=[jax.ShapeDtypeStruct((S, DC), f32), jax.ShapeDtypeStruct((S, DC), bf16),
                   jax.ShapeDtypeStruct((S, DG), bf16), jax.ShapeDtypeStruct((S // CH * DQK, DV), f32),
                   jax.ShapeDtypeStruct((S, DQK), f32), jax.ShapeDtypeStruct((S // CH * DQK, CH), bf16),
                   jax.ShapeDtypeStruct((S // CH * DQK, DV), f32)],
        scratch_shapes=[pltpu.VMEM((DQK, DV), f32), pltpu.VMEM((HALO + t, DC), f32),
                        pltpu.VMEM((SUBLANES, HALO + t, DC), f32)],
    )


def _gla_bwd_chunk(c, consts, umat, last_row, q_ref, k_ref, v_ref, r_ref, la_ref, sp_ref, bc_ref, as_ref, os_ref,
                   dy_ref, gn_ref, dq_ref, dk_ref, dv_ref, dr_ref, dpre_ref, redg_ref, redb_ref, gs):
    tril, tril4, masks, eye4 = consts
    r0, s0 = c * CH, c * DQK
    rows = slice(r0, r0 + CH)
    s_all = sp_ref[s0:s0 + DQK, :]
    lac = la_ref[rows, :]
    vc = v_ref[rows, :]
    rc = r_ref[rows, :].astype(f32)
    t = _gla_chunk_decay(bc_ref[rows, :], q_ref[rows, :], k_ref[rows, :], masks)
    g_all = gs[...]
    g_b = g_all.astype(bf16)
    s_b = s_all.astype(bf16)
    o_s = os_ref[s0:s0 + DQK, :]
    rstd = lax.rsqrt(jnp.mean(o_s * o_s, axis=-1, keepdims=True) + EPS)
    oh = o_s * rstd
    gnv = gn_ref[...]
    sr = _sigmoid(rc)
    dyv = dy_ref[rows, :]
    dr_ref[rows, :] = (dyv * _vunstack(oh * gnv) * (sr * (1.0 + rc * (1.0 - sr)))).astype(dr_ref.dtype)
    don = _vstack(dyv * (rc * sr))
    redg_ref[...] += don * oh
    doh = don * gnv
    do_s = rstd * (doh - oh * jnp.mean(doh * oh, axis=-1, keepdims=True))
    do_b = do_s.astype(bf16)
    v_b = vc.astype(bf16)
    vst_b = _vstack(vc).astype(bf16)
    kd_s = _stack(t["kd"], masks).astype(bf16)
    da_s = jnp.concatenate(
        [_dg(do_b[h * CH:(h + 1) * CH, :], v_b[:, h * DV:(h + 1) * DV], NT) for h in range(NH)], axis=0)
    a_b = as_ref[s0:s0 + DQK, :]
    dv_s = jnp.concatenate(
        [_dg(a_b[h * CH:(h + 1) * CH, :], do_b[h * CH:(h + 1) * CH, :], TN) for h in range(NH)], axis=0)
    dv_s = dv_s + _dot(kd_s, g_b)
    dv_ref[rows, :] = _vunstack(dv_s).astype(dv_ref.dtype)
    gend = jnp.exp(t["bend"])
    gcol = _col_from_row(gend, eye4)
    gs[...] = gcol * g_all + _dg(t["qf_s"], do_b, TN)
    dgcol = jnp.sum(g_all * s_all, axis=1, keepdims=True)
    dbend = _row_from_col(dgcol * gcol, eye4)
    dkd = _unstack(_dg(vst_b, g_b, NT), masks)
    daf = jnp.where(tril4, da_s, 0.0).astype(bf16)
    dab = jnp.where(tril4, 0.0, da_s).astype(bf16)
    dqf = _unstack(_dot(daf, t["kn_b"]) + _dg(do_b, s_b, NT), masks)
    dqn = _unstack(_dot(dab, t["kp_b"]), masks)
    dkn = _dg(daf, t["qf_s"], TN)
    dkp = _dg(dab, t["qn_s"], TN)
    dq_ref[rows, :] = ((dqf * t["eb"] + dqn * t["enb"]) * (DK ** -0.5)).astype(dq_ref.dtype)
    dk_ref[rows, :] = (dkn * t["enb"] + dkp * t["eb"] + dkd * t["ed"]).astype(dk_ref.dtype)
    dkd_kd = dkd * t["kd"]
    dbc = dqf * t["qf"] - dqn * t["qn"] - dkn * t["kn"] + dkp * t["kp"] - dkd_kd
    dbc = dbc + jnp.where(last_row, _rowsum(dkd_kd) + dbend, 0.0)
    dla = jnp.dot(umat, dbc, preferred_element_type=f32, precision=HIGHEST)
    dpre = dla * (1.0 / TAU) * (1.0 - jnp.exp(TAU * lac))
    dpre_ref[rows, :] = dpre.astype(dpre_ref.dtype)
    redb_ref[...] += dpre


def gla_bwd(z, la, sprev, bc, att, o, dyg, gn_s, t, name, comm=None):
    S = z.shape[0]
    nc = t // CH
    nt = S // t

    def body(q_ref, k_ref, v_ref, r_ref, la_ref, sp_ref, bc_ref, as_ref, os_ref, dy_ref, gn_ref,
             dq_ref, dk_ref, dv_ref, dr_ref, dpre_ref, redg_ref, redb_ref, gs):
        @pl.when(pl.program_id(0) == 0)
        def _():
            gs[...] = jnp.zeros_like(gs)
            redg_ref[...] = jnp.zeros_like(redg_ref)
            redb_ref[...] = jnp.zeros_like(redb_ref)

        consts = _gla_consts()
        umat = (lax.broadcasted_iota(jnp.int32, (CH, CH), 0) <= lax.broadcasted_iota(jnp.int32, (CH, CH), 1)).astype(f32)
        last_row = lax.broadcasted_iota(jnp.int32, (CH, DQK), 0) == CH - 1
        for c in reversed(range(nc)):
            _gla_bwd_chunk(c, consts, umat, last_row, q_ref, k_ref, v_ref, r_ref, la_ref, sp_ref, bc_ref, as_ref, os_ref,
                           dy_ref, gn_ref, dq_ref, dk_ref, dv_ref, dr_ref, dpre_ref, redg_ref, redb_ref, gs)

    rev = lambda col: (lambda i: (nt - 1 - i, col))
    return _pcall(
        body, (z, z, z, z, la, sprev, bc, att, o, dyg, gn_s), name=name, comm=comm,
        grid=(nt,),
        in_specs=[pl.BlockSpec((t, DQK), rev(4)), pl.BlockSpec((t, DQK), rev(5)),
                  pl.BlockSpec((t, DG), rev(3)), pl.BlockSpec((t, DG), rev(4)),
                  pl.BlockSpec((t, DQK), rev(0)), pl.BlockSpec((nc * DQK, DV), rev(0)),
                  pl.BlockSpec((t, DQK), rev(0)), pl.BlockSpec((nc * DQK, CH), rev(0)),
                  pl.BlockSpec((nc * DQK, DV), rev(0)),
                  pl.BlockSpec((t, DG), rev(0)), _full(gn_s.shape)],
        out_specs=[pl.BlockSpec((t, DQK), rev(0)), pl.BlockSpec((t, DQK), rev(0)),
                   pl.BlockSpec((t, DG), rev(0)), pl.BlockSpec((t, DG), rev(0)), pl.BlockSpec((t, DQK), rev(0)),
                   _full((DQK, DV)), _full((CH, DQK))],
        out_shape=[jax.ShapeDtypeStruct((S, DQK), bf16), jax.ShapeDtypeStruct((S, DQK), bf16),
                   jax.ShapeDtypeStruct((S, DG), bf16), jax.ShapeDtypeStruct((S, DG), bf16), jax.ShapeDtypeStruct((S, DQK), bf16),
                   jax.ShapeDtypeStruct((DQK, DV), f32), jax.ShapeDtypeStruct((CH, DQK), f32)],
        scratch_shapes=[pltpu.VMEM((DQK, DV), f32)],
    )


def mixout_fwd(x1, yc, yg, mod, wout, tm, name):
    S = x1.shape[0]

    def body(x_ref, yc_ref, yg_ref, mod_ref, w_ref, xo_ref):
        mixo = _dot(yc_ref[...], w_ref[0:DC, :]) + _dot(yg_ref[...], w_ref[DC:DC + DG, :])
        xo_ref[...] = x_ref[...] + mod_ref[5:6, :] * mixo

    row = lambda i: (i, 0)
    return pl.pallas_call(
        body, name=name,
        grid=(S // tm,),
        in_specs=[pl.BlockSpec((tm, D), row), pl.BlockSpec((tm, DC), row), pl.BlockSpec((tm, DG), row),
                  _full(mod.shape), _full(wout.shape)],
        out_specs=pl.BlockSpec((tm, D), row),
        out_shape=jax.ShapeDtypeStruct((S, D), f32),
        compiler_params=_cp(1),
    )(x1, yc, yg, mod, wout)


def mixout_bwd(dx2, yc, yg, mod, wout, tm, name):
    S = dx2.shape[0]
    nt = S // tm

    def body(dx_ref, yc_ref, yg_ref, mod_ref, w_ref, dyc_ref, dyg_ref, red_ref, dw_ref, acc_s):
        i = pl.program_id(0)

        @pl.when(i == 0)
        def _():
            red_ref[...] = jnp.zeros_like(red_ref)
            acc_s[...] = jnp.zeros_like(acc_s)

        dxv = dx_ref[...]
        ycat = jnp.concatenate([yc_ref[...], yg_ref[...]], axis=1)
        mixo = _dot(ycat, w_ref[...])
        red_ref[0:1, :] += _rowsum(dxv * mixo)
        dm = (mod_ref[5:6, :] * dxv).astype(bf16)
        acc_s[...] += _dg(ycat, dm, TN)
        dycat = _dg(dm, w_ref[...], NT)
        dyc_ref[...] = dycat[:, :DC]
        dyg_ref[...] = dycat[:, DC:]

        @pl.when(i == nt - 1)
        def _():
            dw_ref[...] = acc_s[...].astype(bf16)

    row = lambda i: (i, 0)
    return pl.pallas_call(
        body, name=name,
        grid=(nt,),
        in_specs=[pl.BlockSpec((tm, D), row), pl.BlockSpec((tm, DC), row), pl.BlockSpec((tm, DG), row),
                  _full(mod.shape), _full(wout.shape)],
        out_specs=[pl.BlockSpec((tm, DC), row), pl.BlockSpec((tm, DG), row), _full((8, D)), _full((D, D))],
        out_shape=[jax.ShapeDtypeStruct((S, DC), f32), jax.ShapeDtypeStruct((S, DG), f32),
                   jax.ShapeDtypeStruct((8, D), f32), jax.ShapeDtypeStruct((D, D), bf16)],
        scratch_shapes=[pltpu.VMEM((D, D), f32)],
        compiler_params=_cp(1),
    )(dx2, yc, yg, mod, wout)


def final_fwd_bwd(x, tgt, fmod, g, tm, name):
    S = x.shape[0]

    def body(x_ref, t_ref, fm_ref, g_ref, dx_ref, red_ref):
        @pl.when(pl.program_id(0) == 0)
        def _():
            red_ref[...] = jnp.zeros_like(red_ref)

        xh, rstd = _rms_parts(x_ref[...])
        gv = g_ref[...]
        n = xh * gv
        sc = 1.0 + fm_ref[1:2, :]
        e = n * sc + fm_ref[0:1, :] - t_ref[...]
        red_ref[0:1, :] += _rowsum(e * e) * (0.5 / D)
        dy = e * (1.0 / D)
        dn = dy * sc
        red_ref[1:2, :] += _rowsum(dy)
        red_ref[2:3, :] += _rowsum(dy * n)
        red_ref[3:4, :] += _rowsum(dn * xh)
        dx_ref[...] = _rms_bwd(dn * gv, xh, rstd)

    row = lambda i: (i, 0)
    return pl.pallas_call(
        body, name=name,
        grid=(S // tm,),
        in_specs=[pl.BlockSpec((tm, D), row), pl.BlockSpec((tm, D), row), _full(fmod.shape), _full(g.shape)],
        out_specs=[pl.BlockSpec((tm, D), row), _full((8, D))],
        out_shape=[jax.ShapeDtypeStruct((S, D), f32), jax.ShapeDtypeStruct((8, D), f32)],
        compiler_params=_cp(1),
    )(x, tgt, fmod, g)


def ada_fwd(c_all, w, b, name):
    n = w.shape[1]

    def body(c_ref, w_ref, b_ref, o_ref):
        cv = c_ref[...]
        o_ref[...] = jnp.dot(cv * _sigmoid(cv), w_ref[...], preferred_element_type=f32, precision=HIGHEST) + b_ref[...]

    return pl.pallas_call(
        body, name=name,
        in_specs=[_full(c_all.shape), _full(w.shape), _full(b.shape)],
        out_specs=_full((N_DEV, n)),
        out_shape=jax.ShapeDtypeStruct((N_DEV, n), f32),
        grid=(1,),
        compiler_params=_cp(1),
    )(c_all, w, b)


def ada_wgrad(c_all_t, dm, name):
    n = dm.shape[1]

    def body(c_ref, d_ref, o_ref):
        cv = c_ref[...]
        o_ref[...] = jnp.dot(cv * _sigmoid(cv), d_ref[...], preferred_element_type=f32, precision=HIGHEST)

    return pl.pallas_call(
        body, name=name,
        in_specs=[_full(c_all_t.shape), _full(dm.shape)],
        out_specs=_full((D, n)),
        out_shape=jax.ShapeDtypeStruct((D, n), f32),
        grid=(1,),
        compiler_params=_cp(1),
    )(c_all_t, dm)


def _adam_math(gv, wv, mv, vv):
    m = ADAM_B1 * mv + (1.0 - ADAM_B1) * gv
    v = ADAM_B2 * vv + (1.0 - ADAM_B2) * (gv * gv)
    m_hat = m / (1.0 - ADAM_B1 ** ADAM_STEP)
    v_hat = v / (1.0 - ADAM_B2 ** ADAM_STEP)
    delta = -ADAM_LR * (m_hat / (jnp.sqrt(v_hat) + ADAM_EPS) + ADAM_WD * wv)
    return delta, m, v


def adam_parts(parts, w, m, v, tr, name, comm=None):
    L, R, C = w.shape
    nt = R // tr

    def body(*refs):
        p_refs = refs[:L]
        w_ref, m_ref, v_ref, g_ref, d_ref, mo_ref, vo_ref = refs[L:]
        lyr = pl.program_id(0)
        for l in range(L):
            @pl.when(lyr == l)
            def _(p_ref=p_refs[l]):
                gv = p_ref[0].astype(f32)
                for k in range(1, N_DEV):
                    gv = gv + p_ref[k].astype(f32)
                g_ref[...] = gv
                d_ref[...], mo_ref[...], vo_ref[...] = _adam_math(gv, w_ref[...], m_ref[...], v_ref[...])

    def part_spec(l):
        return pl.BlockSpec((N_DEV, tr, C), lambda lyr, i: (0, jnp.where(lyr == l, i, jnp.where(lyr < l, 0, nt - 1)), 0))

    spec = pl.BlockSpec((None, tr, C), lambda lyr, i: (lyr, i, 0))
    shp = jax.ShapeDtypeStruct((L, R, C), f32)
    return _pcall(
        body, (*parts, w, m, v), name=name, comm=comm,
        grid=(L, nt),
        in_specs=[part_spec(l) for l in range(L)] + [spec, spec, spec],
        out_specs=[spec, spec, spec, spec],
        out_shape=[shp, shp, shp, shp],
    )


def adam_plain(gr, w, m, v, tr, name):
    R, C = w.shape

    def body(g_ref, w_ref, m_ref, v_ref, d_ref, mo_ref, vo_ref):
        d_ref[...], mo_ref[...], vo_ref[...] = _adam_math(g_ref[...], w_ref[...], m_ref[...], v_ref[...])

    spec = pl.BlockSpec((tr, C), lambda i: (i, 0))
    shp = jax.ShapeDtypeStruct((R, C), f32)
    return pl.pallas_call(
        body, name=name,
        grid=(R // tr,),
        in_specs=[spec, spec, spec, spec],
        out_specs=[spec, spec, spec],
        out_shape=[shp, shp, shp],
        compiler_params=_cp(1),
    )(gr, w, m, v)


def sum8(parts, name):
    _, R, C = parts.shape

    def body(p_ref, o_ref):
        acc = p_ref[0]
        for k in range(1, N_DEV):
            acc = acc + p_ref[k]
        o_ref[...] = acc

    return pl.pallas_call(
        body, name=name,
        grid=(1,),
        in_specs=[_full(parts.shape)],
        out_specs=_full((R, C)),
        out_shape=jax.ShapeDtypeStruct((R, C), f32),
        compiler_params=_cp(1),
    )(parts)


def _place():
    return lax.axis_index("x"), lax.axis_index("y"), lax.axis_index("c")


def _gather_steps(ins, outs, send_sems, recv_sems, local_sems, place):
    n = len(ins)
    x, y, c = place
    me, sibling = (x, y, c), (x, y, 1 - c)
    chips = [(1 - x, y), (x, 1 - y), (1 - x, 1 - y)]

    def slot(a, p):
        return outs[a].at[4 * p[0] + 2 * p[1] + p[2]]

    def copy(a, k, block, to, src=None):
        return pltpu.make_async_remote_copy(
            src_ref=slot(a, block) if src is None else src, dst_ref=slot(a, block),
            send_sem=send_sems.at[a * 7 + k], recv_sem=recv_sems.at[a * 7 + k],
            device_id=to, device_id_type=MESH)

    def mine():
        return [pltpu.make_async_copy(ins[a], slot(a, me), local_sems.at[a]) for a in range(n)]

    def first():
        cps = []
        for a in range(n):
            cps.append(copy(a, 0, me, sibling, src=ins[a]))
            cps += [copy(a, 1 + j, me, (*chip, c), src=ins[a]) for j, chip in enumerate(chips)]
        return cps

    def start():
        for cp in mine() + first():
            cp.start()

    def forward():
        for j, chip in enumerate(chips):
            for a in range(n):
                copy(a, 1 + j, (*chip, c), me).wait_recv()
                copy(a, 4 + j, (*chip, c), sibling).start()

    def finish():
        for a in range(n):
            copy(a, 0, sibling, me).wait_recv()
            for j, chip in enumerate(chips):
                copy(a, 4 + j, (*chip, 1 - c), me).wait_recv()
        for cp in first() + [copy(a, 4 + j, (*chip, c), sibling) for j, chip in enumerate(chips) for a in range(n)]:
            cp.wait_send()
        for cp in mine():
            cp.wait()

    return start, forward, finish


def _exchange_steps(ins, outs, send_sems, recv_sems, local_sems, place):
    n = len(ins)
    x, y, c = place
    me_i = 4 * x + 2 * y + c

    def mine():
        return [pltpu.make_async_copy(ins[a].at[me_i], outs[a].at[me_i], local_sems.at[a]) for a in range(n)]

    def copies(receiving):
        cps = []
        for k in range(1, N_DEV):
            px = 1 - x if (k >> 2) & 1 else x
            py = 1 - y if (k >> 1) & 1 else y
            pc = 1 - c if k & 1 else c
            p_i = 4 * px + 2 * py + pc
            for a in range(n):
                sem = a * 7 + k - 1
                cps.append(pltpu.make_async_remote_copy(
                    src_ref=ins[a].at[p_i], dst_ref=outs[a].at[p_i if receiving else me_i],
                    send_sem=send_sems.at[sem], recv_sem=recv_sems.at[sem],
                    device_id=(px, py, pc), device_id_type=MESH))
        return cps

    def start():
        for cp in mine() + copies(False):
            cp.start()

    def finish():
        for cp in copies(True):
            cp.wait_recv()
        for cp in copies(False):
            cp.wait_send()
        for cp in mine():
            cp.wait()

    return start, None, finish


_COMM_STEPS = {"gather": _gather_steps, "exchange": _exchange_steps}


def _comm_out_shapes(kind, arrs):
    if kind == "gather":
        return [jax.ShapeDtypeStruct((N_DEV,) + a.shape, a.dtype) for a in arrs]
    return [jax.ShapeDtypeStruct(a.shape, a.dtype) for a in arrs]


def _comm_sems(n):
    return [pltpu.SemaphoreType.DMA((7 * n,)), pltpu.SemaphoreType.DMA((7 * n,)), pltpu.SemaphoreType.DMA((n,))]


def _pcall(body, args, *, name, grid, in_specs, out_specs, out_shape, scratch_shapes=(), comm=None):
    in_specs, out_specs, out_shape = list(in_specs), list(out_specs), list(out_shape)
    scratch_shapes = list(scratch_shapes)
    cparams = _cp(len(grid))
    if comm is None:
        outs = pl.pallas_call(body, name=name, grid=grid, in_specs=in_specs, out_specs=out_specs, out_shape=out_shape,
                              scratch_shapes=scratch_shapes, compiler_params=cparams)(*args)
        return list(outs), []
    kind, arrs = comm
    nc, n_in, n_out, n_scr = len(arrs), len(in_specs), len(out_specs), len(scratch_shapes)
    total = 1
    for gdim in grid:
        total *= gdim
    forward_step = (total * 3) // 4

    def hosted(*refs):
        core_in, c_in = refs[:n_in], refs[n_in:n_in + nc]
        core_out = refs[n_in + nc:n_in + nc + n_out]
        c_out = refs[n_in + nc + n_out:n_in + 2 * nc + n_out]
        rest = refs[n_in + 2 * nc + n_out:]
        step = pl.program_id(0)
        for ax in range(1, len(grid)):
            step = step * grid[ax] + pl.program_id(ax)
        start, forward, finish = _COMM_STEPS[kind](c_in, c_out, *rest[n_scr:], _place())
        pl.when(step == 0)(start)
        if forward is not None:
            pl.when(step == forward_step)(forward)
        body(*core_in, *core_out, *rest[:n_scr])
        pl.when(step == total - 1)(finish)

    any_spec = pl.BlockSpec(memory_space=pl.ANY)
    outs = pl.pallas_call(
        hosted, name=name, grid=grid,
        in_specs=in_specs + [any_spec] * nc,
        out_specs=out_specs + [any_spec] * nc,
        out_shape=out_shape + _comm_out_shapes(kind, arrs),
        scratch_shapes=scratch_shapes + _comm_sems(nc),
        compiler_params=cparams)(*args, *arrs)
    return list(outs[:n_out]), list(outs[n_out:])


def _comm_call(kind, arrs, name):
    n = len(arrs)

    def body(*refs):
        start, forward, finish = _COMM_STEPS[kind](refs[:n], refs[n:2 * n], *refs[2 * n:], _place())
        start()
        if forward is not None:
            forward()
        finish()

    any_spec = pl.BlockSpec(memory_space=pl.ANY)
    return pl.pallas_call(
        body, name=name,
        in_specs=[any_spec] * n, out_specs=[any_spec] * n,
        out_shape=_comm_out_shapes(kind, arrs), scratch_shapes=_comm_sems(n),
    )(*arrs)


def all_gather(arrs, name):
    return _comm_call("gather", arrs, name)


def all_to_all(arrs, name):
    return _comm_call("exchange", arrs, name)


def _tiles(S):
    t = min(512, S)
    return dict(ffn=min(256, S), row=t, conv=t, gla=t, bk=min(1024, S))


BIG = ("wi1", "wo1", "win", "wout", "wi2", "wo2")


def _col_shards_to_full(gathered):
    n, r, c = gathered.shape
    return jnp.transpose(gathered, (1, 0, 2)).reshape(r, n * c)


def _win_full(win_a):
    return _pad_rows(win_a.reshape(DIN, D), DINP)


def train_pass(x, tgt, mods, fmod, sh, ws, wi1_first, wo1_first):
    S = x.shape[0]
    T = _tiles(S)
    bk = T["bk"]
    full = [dict() for _ in range(DEPTH)]
    hosted = {("ffn1", 0): [("win", 0), ("wout", 0), ("wo2", 0)], ("core", 0): [("wi2", 0)]}
    for l in range(1, DEPTH):
        hosted[("mixin", l - 1)] = [("win", l), ("wout", l)]
        hosted[("ffn2", l - 1)] = [("wi1", l), ("wo1", l)]
        hosted[("ffn1", l)] = [("wi2", l), ("wo2", l)]

    def comm_for(key):
        return ("gather", [sh[n][ll] for n, ll in hosted[key]]) if key in hosted else None

    def keep(key, got):
        for (n, ll), gathered in zip(hosted.get(key, []), got):
            if n in ("wi1", "wi2"):
                full[ll][n] = gathered.reshape(2 * F, D)
            elif n in ("wo1", "wo2"):
                full[ll][n] = gathered.reshape(F, D)
            else:
                full[ll][n] = _win_full(gathered) if n == "win" else gathered.reshape(D, D)

    full[0]["wi1"], full[0]["wo1"] = wi1_first.reshape(2 * F, D), wo1_first.reshape(F, D)
    saved = []
    xc = x
    for l in range(DEPTH):
        w, fw = ws[f"L{l}"], full[l]
        x0 = xc
        (x1, h1f, z1, f1), got = ffn_fwd(x0, mods[l], w["g1"], fw["wi1"], fw["wo1"], (0, 1, 2), T["ffn"], f"ffn1_fwd_{l}",
                                         comm=comm_for(("ffn1", l)))
        keep(("ffn1", l), got)
        (z, la), got = mixin_fwd(x1, mods[l], w["g2"], fw["win"], w["wgu"], w["bgate"], T["row"], f"mixin_fwd_{l}",
                                 comm=comm_for(("mixin", l)))
        keep(("mixin", l), got)
        (y, yc, yg, sprev, bc, att, o), got = mixer_core_fwd(z, la, w["gn_s"], w["wdw"], w["cpar"], T["gla"],
                                                             f"mixer_core_fwd_{l}", comm=comm_for(("core", l)))
        keep(("core", l), got)
        x2 = mixout_fwd(x1, yc, yg, mods[l], fw["wout"], T["row"], f"mixout_fwd_{l}")
        (x3, h2f, z2, f2), got = ffn_fwd(x2, mods[l], w["g3"], fw["wi2"], fw["wo2"], (6, 7, 8), T["ffn"], f"ffn2_fwd_{l}",
                                         comm=comm_for(("ffn2", l)))
        keep(("ffn2", l), got)
        saved.append(dict(x0=x0, x1=x1, x2=x2, h1f=h1f, z1=z1, f1=f1, h2f=h2f, z2=z2, f2=f2,
                          z=z, la=la, y=y, yc=yc, yg=yg, sprev=sprev, bc=bc, att=att, o=o))
        xc = x3

    dx, redf = final_fwd_bwd(xc, tgt, fmod, ws["gf"], T["row"], "loss_head")
    loss_lanes = redf[0]
    dfmod = redf[1:3]
    grads = {"gf": redf[3]}
    dmods = [None] * DEPTH
    recv = {n: [None] * DEPTH for n in BIG}

    def ffn_backward(xin, dy, h, z, fo, gain, wi_t, wo, rows, l, tag, ride=None):
        (dz, p_wo), got_ride = ffn_bwd_hidden(dy, z, mods[l], wo.T, rows[2], T["ffn"], f"{tag}_bwd_hidden_{l}",
                                              comm=("exchange", ride) if ride else None)
        p_wi, (r_wo,) = dwi_pieces(h, dz, T["bk"], f"d{tag}_wi_{l}",
                                   comm=("exchange", [p_wo.reshape(N_DEV, F // N_DEV, D)]))
        (dxin, red), (r_wi,) = ffn_bwd_input(xin, dy, dz, fo, mods[l], gain, wi_t, rows, T["row"],
                                             f"{tag}_bwd_input_{l}", comm=("exchange", [p_wi]))
        return dxin, red, r_wi, r_wo, got_ride

    for l in reversed(range(DEPTH)):
        w, fw, sv = ws[f"L{l}"], full[l], saved[l]
        g = {}
        dx2, red3, recv["wi2"][l], recv["wo2"][l], _ = ffn_backward(
            sv["x2"], dx, sv["h2f"], sv["z2"], sv["f2"], w["g3"], fw["wi2"], fw["wo2"], (6, 7, 8), l, "ffn2")
        dyc, dyg, red_o, p_wout = mixout_bwd(dx2, sv["yc"], sv["yg"], mods[l], fw["wout"], T["row"], f"mixout_bwd_{l}")
        (dq, dk, dv, dr, dpre, redg, redb), (recv["wout"][l],) = gla_bwd(
            sv["z"], sv["la"], sv["sprev"], sv["bc"], sv["att"], sv["o"], dyg, w["gn_s"], T["gla"], f"gla_bwd_{l}",
            comm=("exchange", [p_wout.reshape(N_DEV, D // N_DEV, D)]))
        dzab, redc = conv_bwd(sv["z"], sv["y"], dyc, w["wdw"], w["cpar"], T["conv"], f"conv_bwd_{l}")
        (dx1, red2, dwin_t), _ = mixin_bwd(sv["x1"], dx2, dzab, dq, dk, dv, dr, dpre, mods[l], w["g2"], fw["win"], w["wgu"],
                                           T["ffn"], f"mixin_bwd_{l}")
        p_win = dwin_t[:DIN].reshape(N_DEV, DIN // N_DEV, D)
        g["wgu"] = matmul_tn(sv["z"], dpre, 128, DQK, 128, DQK, bk, f"dwgu_{l}", a_col_block=(DINP - 128) // 128)[:GR]
        g["bgate"] = jnp.sum(redb, axis=0)
        g["gn"] = jnp.sum(redg.reshape(NH, CH, DV), axis=1)
        g["wdw"] = redc[:CW]
        g["bdw"], g["gln"], g["bln"] = redc[32], redc[33], redc[34]
        dx0, red1, recv["wi1"][l], recv["wo1"][l], (recv["win"][l],) = ffn_backward(
            sv["x0"], dx1, sv["h1f"], sv["z1"], sv["f1"], w["g1"], fw["wi1"], fw["wo1"], (0, 1, 2), l, "ffn1",
            ride=[p_win])
        g["g1"], g["g2"], g["g3"] = red1[3], red2[2], red3[3]
        dmods[l] = jnp.stack([red1[0], red1[1], red1[2], red2[0], red2[1], red_o[0], red3[0], red3[1], red3[2]], axis=0)
        grads[f"L{l}"] = g
        dx = dx0
    return loss_lanes, dx, grads, dmods, dfmod, recv


def _pad_rows(a, rows):
    return jnp.pad(a, ((0, rows - a.shape[0]), (0, 0)))


def kernel(x, c, w_ada, b_ada, g_norm_ffn1, w_ffn1_in, w_ffn1_out, g_norm_mix, w_in, w_dw, b_dw, g_conv_ln, b_conv_ln, w_gate_up, b_gate, g_gla_norm, w_out, g_norm_ffn2, w_ffn2_in, w_ffn2_out, g_norm_final, w_ada_final, b_ada_final, loss_target, m_w_ada, m_b_ada, m_g_norm_ffn1, m_w_ffn1_in, m_w_ffn1_out, m_g_norm_mix, m_w_in, m_w_dw, m_b_dw, m_g_conv_ln, m_b_conv_ln, m_w_gate_up, m_b_gate, m_g_gla_norm, m_w_out, m_g_norm_ffn2, m_w_ffn2_in, m_w_ffn2_out, m_g_norm_final, m_w_ada_final, m_b_ada_final, v_w_ada, v_b_ada, v_g_norm_ffn1, v_w_ffn1_in, v_w_ffn1_out, v_g_norm_mix, v_w_in, v_w_dw, v_b_dw, v_g_conv_ln, v_b_conv_ln, v_w_gate_up, v_b_gate, v_g_gla_norm, v_w_out, v_g_norm_ffn2, v_w_ffn2_in, v_w_ffn2_out, v_g_norm_final, v_w_ada_final, v_b_ada_final):
    me = 4 * lax.axis_index("x") + 2 * lax.axis_index("y") + lax.axis_index("c")
    L = DEPTH
    n_ada = N_MOD * D // N_DEV
    n_fin = 2 * D // N_DEV

    small = jnp.concatenate([c.reshape(-1), w_dw.reshape(-1), w_gate_up.reshape(-1)])
    n_small = small.shape[0]
    small = jnp.pad(small, (0, 8 * D - n_small)).reshape(8, D)
    big = dict(wi1=w_ffn1_in, wo1=w_ffn1_out, win=w_in, wout=w_out, wi2=w_ffn2_in, wo2=w_ffn2_out)
    transposed = ("wi1", "wi2", "win")
    sh = {n: [(a[l].T if n in transposed else a[l]).astype(bf16) for l in range(L)] for n, a in big.items()}
    small_a, wi1_first, wo1_first = all_gather([small, sh["wi1"][0], sh["wo1"][0]], "gather_first")
    small_a = small_a.reshape(N_DEV, 8 * D)
    c_all = small_a[:, :D]
    o1 = D + L * CW * (DC // N_DEV)
    wdw_full = _col_shards_to_full(small_a[:, D:o1].reshape(N_DEV, L * CW, DC // N_DEV)).reshape(L, CW, DC)
    wgu_full = _col_shards_to_full(small_a[:, o1:o1 + L * GR * (DQK // N_DEV)].reshape(N_DEV, L * GR, DQK // N_DEV)).reshape(L, GR, DQK)

    b_ada_mine = lax.dynamic_slice(b_ada, (0, me * n_ada), (L, n_ada))
    b_fin_mine = lax.dynamic_slice(b_ada_final, (me * n_fin,), (n_fin,))
    parts = [ada_fwd(c_all, w_ada[l], b_ada_mine[l:l + 1], f"ada_fwd_{l}") for l in range(L)]
    parts.append(ada_fwd(c_all, w_ada_final, b_fin_mine.reshape(1, n_fin), "ada_fwd_final"))
    modsrc = jnp.concatenate(parts, axis=1)
    n_row = modsrc.shape[1]
    modsrc = jnp.pad(modsrc, ((0, 0), (0, 24 * 128 - n_row))).reshape(N_DEV, 24, 128)
    (modrecv,) = all_to_all([modsrc], "exchange_mod")
    modrecv = modrecv.reshape(N_DEV, 24 * 128)
    mods = []
    for l in range(L):
        mvec = modrecv[:, l * n_ada:(l + 1) * n_ada].reshape(N_MOD, D)
        mods.append(_pad_rows(mvec, 16))
    fmod = _pad_rows(modrecv[:, L * n_ada:L * n_ada + n_fin].reshape(2, D), 8)

    ws = {"gf": g_norm_final.reshape(1, D)}
    for l in range(L):
        ws[f"L{l}"] = dict(
            g1=g_norm_ffn1[l].reshape(1, D), g2=g_norm_mix[l].reshape(1, D), g3=g_norm_ffn2[l].reshape(1, D),
            wgu=_pad_rows(wgu_full[l], 128).astype(bf16),
            bgate=b_gate[l].reshape(1, DQK),
            wdw=_pad_rows(wdw_full[l], 32),
            cpar=_pad_rows(jnp.stack([b_dw[l], g_conv_ln[l], b_conv_ln[l]]), 8),
            gn_s=jnp.repeat(g_gla_norm[l], CH, axis=0),
        )

    loss_lanes, grad_x, gr, dmods, dfmod, recv = train_pass(
        x[0], loss_target[0], mods, fmod, sh, ws, wi1_first, wo1_first)

    def adam_big(rv, w, m, v, name, is_transposed=False):
        if is_transposed:
            w, m, v = (jnp.swapaxes(a, 1, 2) for a in (w, m, v))
        R = w.shape[1]
        tr = 256 if R % 256 == 0 else (R // 2 if (R // 2) % 16 == 0 else R)
        outs, _ = adam_parts(rv, w, m, v, tr, name)
        return [jnp.swapaxes(o, 1, 2) for o in outs] if is_transposed else outs

    res = {}
    res["w_ffn2_in"] = adam_big(recv["wi2"], w_ffn2_in, m_w_ffn2_in, v_w_ffn2_in, "adam_ffn2_in", True)
    res["w_ffn2_out"] = adam_big(recv["wo2"], w_ffn2_out, m_w_ffn2_out, v_w_ffn2_out, "adam_ffn2_out")
    res["w_in"] = adam_big(recv["win"], w_in, m_w_in, v_w_in, "adam_w_in", True)
    res["w_out"] = adam_big(recv["wout"], w_out, m_w_out, v_w_out, "adam_w_out")
    res["w_ffn1_out"] = adam_big(recv["wo1"], w_ffn1_out, m_w_ffn1_out, v_w_ffn1_out, "adam_ffn1_out")
    res["w_ffn1_in"] = adam_big(recv["wi1"], w_ffn1_in, m_w_ffn1_in, v_w_ffn1_in, "adam_ffn1_in", True)

    flat = lambda name: jnp.stack([gr[f"L{l}"][name] for l in range(L)]).reshape(-1)
    sections = [
        ("b_ada", jnp.stack(dmods).reshape(-1)), ("b_ada_final", dfmod.reshape(-1)),
        ("g_norm_ffn1", flat("g1")), ("g_norm_mix", flat("g2")), ("g_norm_ffn2", flat("g3")), ("g_norm_final", gr["gf"]),
        ("b_dw", flat("bdw")), ("g_conv_ln", flat("gln")), ("b_conv_ln", flat("bln")), ("b_gate", flat("bgate")),
        ("g_gla_norm", flat("gn")),
    ]
    n_rep = sum(s[1].shape[0] for s in sections)
    rep_rows = -(-n_rep // D)
    extra = [("loss", loss_lanes), ("w_dw", flat("wdw")), ("w_gate_up", flat("wgu"))]
    pack = jnp.concatenate([s[1] for s in sections] + [jnp.zeros((rep_rows * D - n_rep,), f32)] + [s[1] for s in extra])
    n_pack = pack.shape[0]
    pack_rows = -(-n_pack // (8 * D)) * 8
    pack = jnp.pad(pack, (0, pack_rows * D - n_pack)).reshape(pack_rows, D)
    (pack_all,) = all_gather([pack], "gather_small_grads")
    tot = sum8(pack_all, "sum_small_grads")
    tot_flat = tot.reshape(-1)
    loss = jnp.sum(tot_flat[rep_rows * D:rep_rows * D + D])
    o_dw = rep_rows * D + D
    g_wdw_full = tot_flat[o_dw:o_dw + L * CW * DC].reshape(L, CW, DC)
    o_gu = o_dw + L * CW * DC
    g_wgu_full = tot_flat[o_gu:o_gu + L * GR * DQK].reshape(L, GR, DQK)

    small_params = dict(b_ada=(b_ada, m_b_ada, v_b_ada), b_ada_final=(b_ada_final, m_b_ada_final, v_b_ada_final),
                        g_norm_ffn1=(g_norm_ffn1, m_g_norm_ffn1, v_g_norm_ffn1), g_norm_mix=(g_norm_mix, m_g_norm_mix, v_g_norm_mix),
                        g_norm_ffn2=(g_norm_ffn2, m_g_norm_ffn2, v_g_norm_ffn2), g_norm_final=(g_norm_final, m_g_norm_final, v_g_norm_final),
                        b_dw=(b_dw, m_b_dw, v_b_dw), g_conv_ln=(g_conv_ln, m_g_conv_ln, v_g_conv_ln),
                        b_conv_ln=(b_conv_ln, m_b_conv_ln, v_b_conv_ln), b_gate=(b_gate, m_b_gate, v_b_gate),
                        g_gla_norm=(g_gla_norm, m_g_gla_norm, v_g_gla_norm))

    def rep_pack(idx):
        p = jnp.concatenate([small_params[s[0]][idx].reshape(-1) for s in sections])
        return jnp.pad(p, (0, rep_rows * D - n_rep)).reshape(rep_rows, D)

    g_rep = tot[:rep_rows]
    d_rep, m_rep, v_rep = adam_plain(g_rep, rep_pack(0), rep_pack(1), rep_pack(2), rep_rows, "adam_small")
    off = 0
    for sname, sval in sections:
        shp = small_params[sname][0].shape
        nel = sval.shape[0]
        res[sname] = [a.reshape(-1)[off:off + nel].reshape(shp) for a in (g_rep, d_rep, m_rep, v_rep)]
        off += nel

    def adam_cols(g_full, w, m, v, name):
        shp = w.shape
        g_mine = lax.dynamic_slice(g_full, (0, 0, me * shp[2]), shp)
        R, C = shp[0] * shp[1], shp[2]
        outs = adam_plain(g_mine.reshape(R, C), w.reshape(R, C), m.reshape(R, C), v.reshape(R, C), R, name)
        return [g_mine] + [o.reshape(shp) for o in outs]

    res["w_dw"] = adam_cols(g_wdw_full, w_dw, m_w_dw, v_w_dw, "adam_w_dw")
    res["w_gate_up"] = adam_cols(g_wgu_full, w_gate_up, m_w_gate_up, v_w_gate_up, "adam_w_gate_up")

    c_all_t = c_all.T
    dmod_all = pack_all.reshape(N_DEV, -1)[:, :L * N_MOD * D].reshape(N_DEV, L, N_MOD * D)
    dfm_all = pack_all.reshape(N_DEV, -1)[:, L * N_MOD * D:L * N_MOD * D + 2 * D]
    dm_mine = lax.dynamic_slice(dmod_all, (0, 0, me * n_ada), (N_DEV, L, n_ada))
    dfm_mine = lax.dynamic_slice(dfm_all, (0, me * n_fin), (N_DEV, n_fin))
    g_w_ada = jnp.stack([ada_wgrad(c_all_t, dm_mine[:, l], f"ada_wgrad_{l}") for l in range(L)])
    g_w_fin = ada_wgrad(c_all_t, dfm_mine, "ada_wgrad_final")
    outs = adam_plain(g_w_ada.reshape(L * D, n_ada), w_ada.reshape(L * D, n_ada), m_w_ada.reshape(L * D, n_ada),
                      v_w_ada.reshape(L * D, n_ada), 256, "adam_w_ada")
    res["w_ada"] = [g_w_ada] + [o.reshape(w_ada.shape) for o in outs]
    res["w_ada_final"] = [g_w_fin] + list(adam_plain(g_w_fin, w_ada_final, m_w_ada_final, v_w_ada_final, 256, "adam_w_ada_final"))

    order = ["w_ada", "b_ada", "g_norm_ffn1", "w_ffn1_in", "w_ffn1_out", "g_norm_mix", "w_in", "w_dw", "b_dw", "g_conv_ln",
             "b_conv_ln", "w_gate_up", "b_gate", "g_gla_norm", "w_out", "g_norm_ffn2", "w_ffn2_in", "w_ffn2_out",
             "g_norm_final", "w_ada_final", "b_ada_final"]
    out = [loss, grad_x[None]]
    for k in range(4):
        out += [res[name][k] for name in order]
    return tuple(out)
```

```python
import functools

import jax
import jax.numpy as jnp
from jax import lax
from jax.experimental import pallas as pl
from jax.experimental.pallas import tpu as pltpu

f32 = jnp.float32
bf16 = jnp.bfloat16

N_DEV = 8
DEPTH = 2
D = 1024
F = 2816
DC = 512
NH = 4
DK = 64
DV = 128
DQK = NH * DK
DG = NH * DV
CH = 64
CW = 31
GR = 16
TAU = 16.0
N_MOD = 9
DIN = 2 * DC + 2 * DQK + 2 * DG + GR
DINP = 2688
EPS = 1e-6
HALO = 32
SUBLANES = 8
CONV_ROWS = 32
FS = 2 * F // N_DEV

ADAM_LR = 0.001
ADAM_B1 = 0.9
ADAM_B2 = 0.999
ADAM_EPS = 1e-08
ADAM_WD = 0.01
ADAM_STEP = 10

V7X_VMEM_LIMIT = 56 * 1024 * 1024
MESH = pl.DeviceIdType.MESH
HIGHEST = lax.Precision.HIGHEST

NT = (((1,), (1,)), ((), ()))
TN = (((0,), (0,)), ((), ()))


def _cp(n_axes):
    return pltpu.CompilerParams(dimension_semantics=("arbitrary",) * n_axes, vmem_limit_bytes=V7X_VMEM_LIMIT)


def _full(shape):
    nd = len(shape)
    return pl.BlockSpec(shape, lambda *_: (0,) * nd)


def _resident(shape):
    nd = len(shape)
    return pl.BlockSpec(shape, lambda *_: (0,) * nd, pipeline_mode=pl.Buffered(1))


def _dot(a, b):
    return jnp.dot(a, b, preferred_element_type=f32)


def _dg(a, b, dims):
    return lax.dot_general(a, b, dims, preferred_element_type=f32)


def _sigmoid(x):
    return jax.nn.sigmoid(x)


def _rowsum(x):
    return jnp.sum(x, axis=0, keepdims=True)


def _rms_parts(xv):
    rstd = lax.rsqrt(jnp.mean(xv * xv, axis=-1, keepdims=True) + EPS)
    return xv * rstd, rstd


def _rms_bwd(dxh, xh, rstd):
    return rstd * (dxh - xh * jnp.mean(dxh * xh, axis=-1, keepdims=True))


def ffn_fwd(x, mod, g, wi_t, wo, rows, tm, name, comm=None, mix=None):
    S = x.shape[0]
    r_shift, r_scale, r_gate = rows
    n_mix = 0 if mix is None else len(mix)

    def body(x_ref, mod_ref, g_ref, wi_ref, wo_ref, *rest):
        xo_ref, h_ref, z_ref, f_ref = rest[n_mix:n_mix + 4]
        xv = x_ref[...]
        if mix is not None:
            yc_ref, yg_ref, wout_ref = rest[:n_mix]
            ycat = jnp.concatenate([yc_ref[...], yg_ref[...]], axis=1)
            xv = xv + mod_ref[5:6, :] * _dot(ycat, wout_ref[...])
            rest[n_mix + 4][...] = xv
        xh, _ = _rms_parts(xv)
        h = (xh * g_ref[...] * (1.0 + mod_ref[r_scale:r_scale + 1, :]) + mod_ref[r_shift:r_shift + 1, :]).astype(bf16)
        h_ref[...] = h
        zg = _dg(h, wi_ref[0:F, :], NT)
        zu = _dg(h, wi_ref[F:2 * F, :], NT)
        z_ref[:, 0:F] = zg.astype(bf16)
        z_ref[:, F:2 * F] = zu.astype(bf16)
        fv = _dot((zg * _sigmoid(zg) * zu).astype(bf16), wo_ref[...])
        f_ref[...] = fv.astype(bf16)
        xo_ref[...] = xv + 0.5 * mod_ref[r_gate:r_gate + 1, :] * fv

    row = lambda i: (i, 0)
    in_specs = [pl.BlockSpec((tm, D), row), _full(mod.shape), _full(g.shape), _resident(wi_t.shape), _resident(wo.shape)]
    out_specs = [pl.BlockSpec((tm, D), row), pl.BlockSpec((tm, D), row), pl.BlockSpec((tm, 2 * F), row),
                 pl.BlockSpec((tm, D), row)]
    out_shape = [jax.ShapeDtypeStruct((S, D), f32), jax.ShapeDtypeStruct((S, D), bf16),
                 jax.ShapeDtypeStruct((S, 2 * F), bf16), jax.ShapeDtypeStruct((S, D), bf16)]
    args = (x, mod, g, wi_t, wo)
    if mix is not None:
        args += tuple(mix)
        in_specs += [pl.BlockSpec((tm, DC), row), pl.BlockSpec((tm, DG), row), _resident(mix[2].shape)]
        out_specs.append(pl.BlockSpec((tm, D), row))
        out_shape.append(jax.ShapeDtypeStruct((S, D), f32))
    return _pcall(body, args, name=name, comm=comm, grid=(S // tm,),
                  in_specs=in_specs, out_specs=out_specs, out_shape=out_shape)


def ffn_bwd_hidden(dy, z, mod, wo_t, r_gate, tm, name, comm=None):
    S = dy.shape[0]
    nt = S // tm
    halves = 11
    fc = F // halves

    def body(dy_ref, z_ref, mod_ref, wo_ref, dz_ref, dwo_ref, acc_s):
        i = pl.program_id(0)

        @pl.when(i == 0)
        def _():
            acc_s[...] = jnp.zeros_like(acc_s)

        df = (0.5 * mod_ref[r_gate:r_gate + 1, :] * dy_ref[...]).astype(bf16)
        for c in range(halves):
            lo, hi = c * fc, (c + 1) * fc
            zgv = z_ref[:, lo:hi].astype(f32)
            zuv = z_ref[:, F + lo:F + hi].astype(f32)
            s = _sigmoid(zgv)
            sil = zgv * s
            acc_s[lo:hi, :] += _dg((sil * zuv).astype(bf16), df, TN)
            da = _dot(df, wo_ref[:, lo:hi])
            dz_ref[:, F + lo:F + hi] = (da * sil).astype(bf16)
            dz_ref[:, lo:hi] = (da * zuv * (s * (1.0 + zgv * (1.0 - s)))).astype(bf16)

        @pl.when(i == nt - 1)
        def _():
            dwo_ref[...] = acc_s[...].astype(bf16)

    row = lambda i: (i, 0)
    return _pcall(
        body, (dy, z, mod, wo_t), name=name, comm=comm,
        grid=(nt,),
        in_specs=[pl.BlockSpec((tm, D), row), pl.BlockSpec((tm, 2 * F), row), _full(mod.shape), _resident(wo_t.shape)],
        out_specs=[pl.BlockSpec((tm, 2 * F), row), _full((F, D))],
        out_shape=[jax.ShapeDtypeStruct((S, 2 * F), bf16), jax.ShapeDtypeStruct((F, D), bf16)],
        scratch_shapes=[pltpu.VMEM((F, D), f32)],
    )


def ffn_bwd_input(x, dy, dz, fo, mod, g, wi_t, rows, tm, name, comm=None):
    S = x.shape[0]
    r_shift, r_scale, r_gate = rows

    def body(x_ref, dy_ref, dz_ref, f_ref, mod_ref, g_ref, wi_ref, dx_ref, red_ref):
        @pl.when(pl.program_id(0) == 0)
        def _():
            red_ref[...] = jnp.zeros_like(red_ref)

        dh = _dot(dz_ref[...], wi_ref[...])
        dyv = dy_ref[...]
        xh, rstd = _rms_parts(x_ref[...])
        gv = g_ref[...]
        n = xh * gv
        dn = dh * (1.0 + mod_ref[r_scale:r_scale + 1, :])
        red_ref[0:1, :] += _rowsum(dh)
        red_ref[1:2, :] += _rowsum(dh * n)
        red_ref[2:3, :] += _rowsum(0.5 * f_ref[...].astype(f32) * dyv)
        red_ref[3:4, :] += _rowsum(dn * xh)
        dx_ref[...] = dyv + _rms_bwd(dn * gv, xh, rstd)

    row = lambda i: (i, 0)
    return _pcall(
        body, (x, dy, dz, fo, mod, g, wi_t), name=name, comm=comm,
        grid=(S // tm,),
        in_specs=[pl.BlockSpec((tm, D), row), pl.BlockSpec((tm, D), row), pl.BlockSpec((tm, 2 * F), row),
                  pl.BlockSpec((tm, D), row), _full(mod.shape), _full(g.shape), _resident(wi_t.shape)],
        out_specs=[pl.BlockSpec((tm, D), row), _full((8, D))],
        out_shape=[jax.ShapeDtypeStruct((S, D), f32), jax.ShapeDtypeStruct((8, D), f32)],
    )


def matmul_tn(a, b, M, N, bm, bn, bk, name, a_col_block=0, out_dtype=f32):
    S = b.shape[0]
    nk = S // bk

    def body(a_ref, b_ref, o_ref, acc_s):
        k = pl.program_id(2)

        @pl.when(k == 0)
        def _():
            acc_s[...] = jnp.zeros_like(acc_s)

        acc_s[...] += _dg(a_ref[...].astype(bf16), b_ref[...].astype(bf16), TN)

        @pl.when(k == nk - 1)
        def _():
            o_ref[...] = acc_s[...].astype(out_dtype)

    return pl.pallas_call(
        body, name=name,
        grid=(M // bm, N // bn, nk),
        in_specs=[
            pl.BlockSpec((bk, bm), lambda i, j, k: (k, i + a_col_block)),
            pl.BlockSpec((bk, bn), lambda i, j, k: (k, j)),
        ],
        out_specs=pl.BlockSpec((bm, bn), lambda i, j, k: (i, j)),
        out_shape=jax.ShapeDtypeStruct((M, N), out_dtype),
        scratch_shapes=[pltpu.VMEM((bm, bn), f32)],
        compiler_params=_cp(3),
    )(a, b)


def dwi_pieces(h, dz, bk, name, comm=None):
    S = h.shape[0]
    nk = S // bk

    def body(h_ref, dz_ref, o_ref, acc_s):
        k = pl.program_id(1)

        @pl.when(k == 0)
        def _():
            acc_s[...] = jnp.zeros_like(acc_s)

        acc_s[...] += _dg(dz_ref[...], h_ref[...], TN)

        @pl.when(k == nk - 1)
        def _():
            o_ref[...] = acc_s[...].astype(bf16)

    (out,), comm_outs = _pcall(
        body, (h, dz), name=name, comm=comm,
        grid=(2, nk),
        in_specs=[pl.BlockSpec((bk, D), lambda half, k: (k, 0)), pl.BlockSpec((bk, F), lambda half, k: (k, half))],
        out_specs=[pl.BlockSpec((F, D), lambda half, k: (half, 0))],
        out_shape=[jax.ShapeDtypeStruct((2 * F, D), bf16)],
        scratch_shapes=[pltpu.VMEM((F, D), f32)],
    )
    return out.reshape(N_DEV, FS, D), comm_outs


def mixin_fwd(x1, mod, g, win, wgu, bgate, tm, name, comm=None):
    S = x1.shape[0]

    def body(x_ref, mod_ref, g_ref, win_ref, wgu_ref, bg_ref, z_ref, la_ref):
        xh, _ = _rms_parts(x_ref[...])
        hv = xh * g_ref[...] * (1.0 + mod_ref[4:5, :]) + mod_ref[3:4, :]
        z = _dg(hv.astype(bf16), win_ref[...], NT).astype(bf16)
        z_ref[...] = z
        pre = _dot(z[:, DINP - 128:], wgu_ref[...]) + bg_ref[...]
        la_ref[...] = (jnp.minimum(pre, 0.0) - jnp.log(1.0 + jnp.exp(-jnp.abs(pre)))) * (1.0 / TAU)

    return _pcall(
        body, (x1, mod, g, win, wgu, bgate), name=name, comm=comm,
        grid=(S // tm,),
        in_specs=[pl.BlockSpec((tm, D), lambda i: (i, 0)), _full(mod.shape), _full(g.shape),
                  _full(win.shape), _full(wgu.shape), _full(bgate.shape)],
        out_specs=[pl.BlockSpec((tm, DINP), lambda i: (i, 0)), pl.BlockSpec((tm, DQK), lambda i: (i, 0))],
        out_shape=[jax.ShapeDtypeStruct((S, DINP), bf16), jax.ShapeDtypeStruct((S, DQK), f32)],
    )


def mixin_bwd(x1, dres, dzab, dq, dk, dv, dr, dpre, mod, g, win, wgu, tm, name, comm=None):
    S = x1.shape[0]
    nt = S // tm

    def body(x_ref, dres_ref, dzab_ref, dq_ref, dk_ref, dv_ref, dr_ref, dpre_ref, mod_ref, g_ref, win_ref, wgu_ref,
             dx_ref, red_ref, dw_ref, acc_s):
        i = pl.program_id(0)

        @pl.when(i == 0)
        def _():
            red_ref[...] = jnp.zeros_like(red_ref)
            acc_s[...] = jnp.zeros_like(acc_s)

        dglr = _dg(dpre_ref[...], wgu_ref[...], NT).astype(bf16)
        dz = jnp.concatenate([dzab_ref[...], dq_ref[...], dk_ref[...], dv_ref[...], dr_ref[...], dglr], axis=1)
        dh = _dot(dz, win_ref[...])
        xh, rstd = _rms_parts(x_ref[...])
        gv = g_ref[...]
        n = xh * gv
        sc = 1.0 + mod_ref[4:5, :]
        acc_s[...] += _dg(dz, (n * sc + mod_ref[3:4, :]).astype(bf16), TN)
        dn = dh * sc
        red_ref[0:1, :] += _rowsum(dh)
        red_ref[1:2, :] += _rowsum(dh * n)
        red_ref[2:3, :] += _rowsum(dn * xh)
        dx_ref[...] = dres_ref[...] + _rms_bwd(dn * gv, xh, rstd)

        @pl.when(i == nt - 1)
        def _():
            dw_ref[...] = acc_s[...].astype(bf16)

    row = lambda i: (i, 0)
    return _pcall(
        body, (x1, dres, dzab, dq, dk, dv, dr, dpre, mod, g, win, wgu), name=name, comm=comm,
        grid=(nt,),
        in_specs=[pl.BlockSpec((tm, D), row), pl.BlockSpec((tm, D), row),
                  pl.BlockSpec((tm, 2 * DC), row), pl.BlockSpec((tm, DQK), row), pl.BlockSpec((tm, DQK), row),
                  pl.BlockSpec((tm, DG), row), pl.BlockSpec((tm, DG), row), pl.BlockSpec((tm, DQK), row),
                  _full(mod.shape), _full(g.shape), _resident(win.shape), _full(wgu.shape)],
        out_specs=[pl.BlockSpec((tm, D), row), _full((8, D)), _full((DINP, D))],
        out_shape=[jax.ShapeDtypeStruct((S, D), f32), jax.ShapeDtypeStruct((8, D), f32),
                   jax.ShapeDtypeStruct((DINP, D), bf16)],
        scratch_shapes=[pltpu.VMEM((DINP, D), f32)],
    )


def _glu(zab):
    zab = zab.astype(f32)
    return zab[:, :DC] * _sigmoid(zab[:, DC:])


def _shift_copies(src_s, dst_s, tc):
    n = tc + HALO - SUBLANES
    for b in range(1, SUBLANES):
        dst_s[b, 0:n, :] = src_s[b:b + n, :]


def _shifted(src_s, dst_s, o, tc):
    b = o % SUBLANES
    a = o - b
    return src_s[a:a + tc, :] if b == 0 else dst_s[b, a:a + tc, :]


def _conv_fwd_prepare(first, zc_ref, zp_ref, u_s, us_s, tc):
    up = _glu(zp_ref[...])
    u_s[0:HALO, :] = jnp.where(first, 0.0, up)
    u_s[HALO:HALO + tc, :] = _glu(zc_ref[...])
    _shift_copies(u_s, us_s, tc)


def _conv_fwd_rows(r0, n, w_ref, cp_ref, y_ref, yc_ref, u_s, us_s):
    for r in range(r0, r0 + n, CONV_ROWS):
        acc = _shifted(u_s, us_s, HALO - (CW - 1) + r, CONV_ROWS) * w_ref[0:1, :]
        for w in range(1, CW):
            acc = acc + _shifted(u_s, us_s, HALO - (CW - 1) + w + r, CONV_ROWS) * w_ref[w:w + 1, :]
        y = acc + cp_ref[0:1, :]
        y_ref[r:r + CONV_ROWS, :] = y
        yc = y - jnp.mean(y, axis=-1, keepdims=True)
        yl = yc * lax.rsqrt(jnp.mean(yc * yc, axis=-1, keepdims=True) + EPS) * cp_ref[1:2, :] + cp_ref[2:3, :]
        yc_ref[r:r + CONV_ROWS, :] = (yl * _sigmoid(yl)).astype(bf16)


def _conv_bwd_prepare(first, last, zc_ref, zp_ref, y_ref, yn_ref, d_ref, dn_ref, cp_ref, red_ref, u_s, dy_s, us_s, dys_s, tc):
    gl = cp_ref[1:2, :]
    bl = cp_ref[2:3, :]

    def ln_bwd(yv, dv):
        yc = yv - jnp.mean(yv, axis=-1, keepdims=True)
        rstd = lax.rsqrt(jnp.mean(yc * yc, axis=-1, keepdims=True) + EPS)
        yh = yc * rstd
        yl = yh * gl + bl
        s = _sigmoid(yl)
        dyl = dv * (s * (1.0 + yl * (1.0 - s)))
        dyh = dyl * gl
        dyv = rstd * (dyh - jnp.mean(dyh, axis=-1, keepdims=True) - yh * jnp.mean(dyh * yh, axis=-1, keepdims=True))
        return dyv, dyl, yh

    dy_c, dyl_c, yh_c = ln_bwd(y_ref[...], d_ref[...])
    dy_n, _, _ = ln_bwd(yn_ref[...], dn_ref[...])
    dy_s[0:tc, :] = dy_c
    dy_s[tc:tc + HALO, :] = jnp.where(last, 0.0, dy_n)
    u_s[0:HALO, :] = jnp.where(first, 0.0, _glu(zp_ref[...]))
    u_s[HALO:HALO + tc, :] = _glu(zc_ref[...])
    _shift_copies(u_s, us_s, tc)
    _shift_copies(dy_s, dys_s, tc)
    red_ref[32:33, :] += _rowsum(dy_c)
    red_ref[33:34, :] += _rowsum(dyl_c * yh_c)
    red_ref[34:35, :] += _rowsum(dyl_c)


def _conv_bwd_input_rows(r0, n, zc_ref, w_ref, dz_ref, dy_s, dys_s):
    for r in range(r0, r0 + n, CONV_ROWS):
        du = _shifted(dy_s, dys_s, CW - 1 + r, CONV_ROWS) * w_ref[0:1, :]
        for w in range(1, CW):
            du = du + _shifted(dy_s, dys_s, CW - 1 - w + r, CONV_ROWS) * w_ref[w:w + 1, :]
        zc = zc_ref[r:r + CONV_ROWS, :].astype(f32)
        av = zc[:, :DC]
        sb = _sigmoid(zc[:, DC:])
        dz_ref[r:r + CONV_ROWS, :] = jnp.concatenate([du * sb, du * av * sb * (1.0 - sb)], axis=1).astype(dz_ref.dtype)


def _conv_bwd_taps(w0, w1, red_ref, u_s, us_s, dy_s, tc):
    for w in range(w0, w1):
        part = None
        for r in range(0, tc, CONV_ROWS):
            prod = _shifted(u_s, us_s, HALO - (CW - 1) + w + r, CONV_ROWS) * dy_s[r:r + CONV_ROWS, :]
            fold = jnp.sum(prod.reshape(CONV_ROWS // SUBLANES, SUBLANES, DC), axis=0)
            part = fold if part is None else part + fold
        red_ref[w:w + 1, :] += _rowsum(part)


def conv_bwd(z, y, dyc, wdw, cpar, tc, name):
    S = z.shape[0]
    nb = tc // HALO
    nt = S // tc
    last_halo = S // HALO - 1

    def body(zc_ref, zp_ref, y_ref, yn_ref, d_ref, dn_ref, w_ref, cp_ref, dz_ref, red_ref, u_s, dy_s, us_s, dys_s):
        i = pl.program_id(0)

        @pl.when(i == 0)
        def _():
            red_ref[...] = jnp.zeros_like(red_ref)

        _conv_bwd_prepare(i == 0, i == nt - 1, zc_ref, zp_ref, y_ref, yn_ref, d_ref, dn_ref, cp_ref, red_ref,
                          u_s, dy_s, us_s, dys_s, tc)
        _conv_bwd_input_rows(0, tc, zc_ref, w_ref, dz_ref, dy_s, dys_s)
        _conv_bwd_taps(0, CW, red_ref, u_s, us_s, dy_s, tc)

    cur = lambda i: (i, 0)
    nxt = lambda i: (jnp.minimum((i + 1) * nb, last_halo), 0)
    return pl.pallas_call(
        body, name=name,
        grid=(nt,),
        in_specs=[pl.BlockSpec((tc, 2 * DC), cur),
                  pl.BlockSpec((HALO, 2 * DC), lambda i: (jnp.maximum(i * nb - 1, 0), 0)),
                  pl.BlockSpec((tc, DC), cur), pl.BlockSpec((HALO, DC), nxt),
                  pl.BlockSpec((tc, DC), cur), pl.BlockSpec((HALO, DC), nxt),
                  _full(wdw.shape), _full(cpar.shape)],
        out_specs=[pl.BlockSpec((tc, 2 * DC), cur), _full((40, DC))],
        out_shape=[jax.ShapeDtypeStruct((S, 2 * DC), bf16), jax.ShapeDtypeStruct((40, DC), f32)],
        scratch_shapes=[pltpu.VMEM((HALO + tc, DC), f32), pltpu.VMEM((tc + HALO, DC), f32),
                        pltpu.VMEM((SUBLANES, HALO + tc, DC), f32), pltpu.VMEM((SUBLANES, HALO + tc, DC), f32)],
        compiler_params=_cp(1),
    )(z, z, y, y, dyc, dyc, wdw, cpar)


def _gla_consts():
    r = lax.broadcasted_iota(jnp.int32, (CH, CH), 0)
    c = lax.broadcasted_iota(jnp.int32, (CH, CH), 1)
    tril = r >= c
    lane = lax.broadcasted_iota(jnp.int32, (CH, DQK), 1)
    masks = [(lane >= h * DK) & (lane < (h + 1) * DK) for h in range(NH)]
    r4 = lax.broadcasted_iota(jnp.int32, (DQK, DQK), 0)
    c4 = lax.broadcasted_iota(jnp.int32, (DQK, DQK), 1)
    eye4 = (r4 == c4).astype(f32)
    rs = lax.broadcasted_iota(jnp.int32, (DQK, CH), 0) & (CH - 1)
    tril4 = rs >= lax.broadcasted_iota(jnp.int32, (DQK, CH), 1)
    return tril, tril4, masks, eye4


def _stack(xv, masks):
    return jnp.concatenate([jnp.where(m, xv, 0.0) for m in masks], axis=0)


def _unstack(rv, masks):
    out = jnp.where(masks[0], rv[0:CH, :], 0.0)
    for h in range(1, NH):
        out = out + jnp.where(masks[h], rv[h * CH:(h + 1) * CH, :], 0.0)
    return out


def _vstack(xv):
    return jnp.concatenate([xv[:, h * DV:(h + 1) * DV] for h in range(NH)], axis=0)


def _vunstack(xv):
    return jnp.concatenate([xv[h * CH:(h + 1) * CH, :] for h in range(NH)], axis=1)


def _gla_chunk_decay(bc, qc, kc, masks):
    qc, kc = qc.astype(f32), kc.astype(f32)
    bend = bc[CH - 1:CH, :]
    eb = jnp.exp(bc)
    enb = jnp.exp(-bc)
    ed = jnp.exp(bend - bc)
    qh = qc * (DK ** -0.5)
    qf = qh * eb
    qn = qh * enb
    kn = kc * enb
    kp = kc * eb
    kd = kc * ed
    return dict(bc=bc, bend=bend, eb=eb, enb=enb, ed=ed, qf=qf, qn=qn, kn=kn, kp=kp, kd=kd,
                qf_s=_stack(qf, masks).astype(bf16), qn_s=_stack(qn, masks).astype(bf16),
                kn_b=kn.astype(bf16), kp_b=kp.astype(bf16))


def _gla_chunk_fwd(lac, qc, kc, vc, s_all, tril, masks, tril4):
    qc, kc = qc.astype(f32), kc.astype(f32)
    lmat = tril.astype(f32)
    bc = jnp.dot(lmat, lac, preferred_element_type=f32, precision=HIGHEST)
    bend = bc[CH - 1:CH, :]
    eb = jnp.exp(bc)
    enb = jnp.exp(-bc)
    ed = jnp.exp(bend - bc)
    qh = qc * (DK ** -0.5)
    qf = qh * eb
    qn = qh * enb
    kn = kc * enb
    kp = kc * eb
    kd = kc * ed
    qf_s = _stack(qf, masks).astype(bf16)
    qn_s = _stack(qn, masks).astype(bf16)
    kn_b = kn.astype(bf16)
    kp_b = kp.astype(bf16)
    attf = _dg(qf_s, kn_b, NT)
    attb = _dg(qn_s, kp_b, NT)
    a_s = jnp.where(tril4, attf, attb)
    a_b = a_s.astype(bf16)
    v_b = vc.astype(bf16)
    intra = jnp.concatenate(
        [_dot(a_b[h * CH:(h + 1) * CH, :], v_b[:, h * DV:(h + 1) * DV]) for h in range(NH)], axis=0)
    o_s = intra + _dot(qf_s, s_all.astype(bf16))
    return dict(bc=bc, bend=bend, eb=eb, enb=enb, ed=ed, qf=qf, qn=qn, kn=kn, kp=kp, kd=kd,
                qf_s=qf_s, qn_s=qn_s, kn_b=kn_b, kp_b=kp_b, a_b=a_b, v_b=v_b, o_s=o_s)


def _col_from_row(row, eye4):
    return jnp.sum(eye4 * row, axis=1, keepdims=True)


def _row_from_col(col, eye4):
    return jnp.sum(eye4 * col, axis=0, keepdims=True)


def _gla_fwd_chunk(c, consts, q_ref, k_ref, v_ref, r_ref, la_ref, gn_ref, yg_ref, sp_ref, bc_ref, as_ref, os_ref, st):
    tril, tril4, masks, eye4 = consts
    r0, s0 = c * CH, c * DQK
    s_all = st[...]
    sp_ref[s0:s0 + DQK, :] = s_all
    vc = v_ref[r0:r0 + CH, :]
    t = _gla_chunk_fwd(la_ref[r0:r0 + CH, :], q_ref[r0:r0 + CH, :], k_ref[r0:r0 + CH, :], vc,
                       s_all, tril, masks, tril4)
    u_all = _dg(_stack(t["kd"], masks).astype(bf16), _vstack(vc).astype(bf16), TN)
    st[...] = _col_from_row(jnp.exp(t["bend"]), eye4) * s_all + u_all
    o_s = t["o_s"]
    bc_ref[r0:r0 + CH, :] = t["bc"]
    as_ref[s0:s0 + DQK, :] = t["a_b"]
    os_ref[s0:s0 + DQK, :] = o_s
    on = o_s * lax.rsqrt(jnp.mean(o_s * o_s, axis=-1, keepdims=True) + EPS) * gn_ref[...]
    rc = r_ref[r0:r0 + CH, :].astype(f32)
    yg_ref[r0:r0 + CH, :] = (_vunstack(on) * (rc * _sigmoid(rc))).astype(bf16)


def mixer_core_fwd(z, la, gn_s, wdw, cpar, t, name, comm=None):
    S = z.shape[0]
    nb = t // HALO
    nc = t // CH

    def body(zc_ref, zp_ref, q_ref, k_ref, v_ref, r_ref, la_ref, gn_ref, w_ref, cp_ref,
             y_ref, yc_ref, yg_ref, sp_ref, bc_ref, as_ref, os_ref, st, u_s, us_s):
        i = pl.program_id(0)

        @pl.when(i == 0)
        def _():
            st[...] = jnp.zeros_like(st)

        _conv_fwd_prepare(i == 0, zc_ref, zp_ref, u_s, us_s, t)
        consts = _gla_consts()
        for c in range(nc):
            _conv_fwd_rows(c * CH, CH, w_ref, cp_ref, y_ref, yc_ref, u_s, us_s)
            _gla_fwd_chunk(c, consts, q_ref, k_ref, v_ref, r_ref, la_ref, gn_ref, yg_ref, sp_ref, bc_ref, as_ref, os_ref, st)

    row = lambda i: (i, 0)
    return _pcall(
        body, (z, z, z, z, z, z, la, gn_s, wdw, cpar), name=name, comm=comm,
        grid=(S // t,),
        in_specs=[pl.BlockSpec((t, 2 * DC), row),
                  pl.BlockSpec((HALO, 2 * DC), lambda i: (jnp.maximum(i * nb - 1, 0), 0)),
                  pl.BlockSpec((t, DQK), lambda i: (i, 4)), pl.BlockSpec((t, DQK), lambda i: (i, 5)),
                  pl.BlockSpec((t, DG), lambda i: (i, 3)), pl.BlockSpec((t, DG), lambda i: (i, 4)),
                  pl.BlockSpec((t, DQK), row), _full(gn_s.shape), _full(wdw.shape), _full(cpar.shape)],
        out_specs=[pl.BlockSpec((t, DC), row), pl.BlockSpec((t, DC), row), pl.BlockSpec((t, DG), row),
                   pl.BlockSpec((nc * DQK, DV), row), pl.BlockSpec((t, DQK), row),
                   pl.BlockSpec((nc * DQK, CH), row), pl.BlockSpec((nc * DQK, DV), row)],
        out_shape=[jax.ShapeDtypeStruct((S, DC), f32), jax.ShapeDtypeStruct((S, DC), bf16),
                   jax.ShapeDtypeStruct((S, DG), bf16), jax.ShapeDtypeStruct((S // CH * DQK, DV), f32),
                   jax.ShapeDtypeStruct((S, DQK), f32), jax.ShapeDtypeStruct((S // CH * DQK, CH), bf16),
                   jax.ShapeDtypeStruct((S // CH * DQK, DV), f32)],
        scratch_shapes=[pltpu.VMEM((DQK, DV), f32), pltpu.VMEM((HALO + t, DC), f32),
                        pltpu.VMEM((SUBLANES, HALO + t, DC), f32)],
    )


def _gla_bwd_chunk(c, consts, umat, last_row, q_ref, k_ref, v_ref, r_ref, la_ref, sp_ref, bc_ref, as_ref, os_ref,
                   dy_ref, gn_ref, dq_ref, dk_ref, dv_ref, dr_ref, dpre_ref, redg_ref, redb_ref, gs):
    tril, tril4, masks, eye4 = consts
    r0, s0 = c * CH, c * DQK
    rows = slice(r0, r0 + CH)
    s_all = sp_ref[s0:s0 + DQK, :]
    lac = la_ref[rows, :]
    vc = v_ref[rows, :]
    rc = r_ref[rows, :].astype(f32)
    t = _gla_chunk_decay(bc_ref[rows, :], q_ref[rows, :], k_ref[rows, :], masks)
    g_all = gs[...]
    g_b = g_all.astype(bf16)
    s_b = s_all.astype(bf16)
    o_s = os_ref[s0:s0 + DQK, :]
    rstd = lax.rsqrt(jnp.mean(o_s * o_s, axis=-1, keepdims=True) + EPS)
    oh = o_s * rstd
    gnv = gn_ref[...]
    sr = _sigmoid(rc)
    dyv = dy_ref[rows, :]
    dr_ref[rows, :] = (dyv * _vunstack(oh * gnv) * (sr * (1.0 + rc * (1.0 - sr)))).astype(dr_ref.dtype)
    don = _vstack(dyv * (rc * sr))
    redg_ref[...] += don * oh
    doh = don * gnv
    do_s = rstd * (doh - oh * jnp.mean(doh * oh, axis=-1, keepdims=True))
    do_b = do_s.astype(bf16)
    v_b = vc.astype(bf16)
    vst_b = _vstack(vc).astype(bf16)
    kd_s = _stack(t["kd"], masks).astype(bf16)
    da_s = jnp.concatenate(
        [_dg(do_b[h * CH:(h + 1) * CH, :], v_b[:, h * DV:(h + 1) * DV], NT) for h in range(NH)], axis=0)
    a_b = as_ref[s0:s0 + DQK, :]
    dv_s = jnp.concatenate(
        [_dg(a_b[h * CH:(h + 1) * CH, :], do_b[h * CH:(h + 1) * CH, :], TN) for h in range(NH)], axis=0)
    dv_s = dv_s + _dot(kd_s, g_b)
    dv_ref[rows, :] = _vunstack(dv_s).astype(dv_ref.dtype)
    gend = jnp.exp(t["bend"])
    gcol = _col_from_row(gend, eye4)
    gs[...] = gcol * g_all + _dg(t["qf_s"], do_b, TN)
    dgcol = jnp.sum(g_all * s_all, axis=1, keepdims=True)
    dbend = _row_from_col(dgcol * gcol, eye4)
    dkd = _unstack(_dg(vst_b, g_b, NT), masks)
    daf = jnp.where(tril4, da_s, 0.0).astype(bf16)
    dab = jnp.where(tril4, 0.0, da_s).astype(bf16)
    dqf = _unstack(_dot(daf, t["kn_b"]) + _dg(do_b, s_b, NT), masks)
    dqn = _unstack(_dot(dab, t["kp_b"]), masks)
    dkn = _dg(daf, t["qf_s"], TN)
    dkp = _dg(dab, t["qn_s"], TN)
    dq_ref[rows, :] = ((dqf * t["eb"] + dqn * t["enb"]) * (DK ** -0.5)).astype(dq_ref.dtype)
    dk_ref[rows, :] = (dkn * t["enb"] + dkp * t["eb"] + dkd * t["ed"]).astype(dk_ref.dtype)
    dkd_kd = dkd * t["kd"]
    dbc = dqf * t["qf"] - dqn * t["qn"] - dkn * t["kn"] + dkp * t["kp"] - dkd_kd
    dbc = dbc + jnp.where(last_row, _rowsum(dkd_kd) + dbend, 0.0)
    dla = jnp.dot(umat, dbc, preferred_element_type=f32, precision=HIGHEST)
    dpre = dla * (1.0 / TAU) * (1.0 - jnp.exp(TAU * lac))
    dpre_ref[rows, :] = dpre.astype(dpre_ref.dtype)
    redb_ref[...] += dpre


def gla_bwd(z, la, sprev, bc, att, o, dyg, gn_s, t, name, comm=None):
    S = z.shape[0]
    nc = t // CH
    nt = S // t

    def body(q_ref, k_ref, v_ref, r_ref, la_ref, sp_ref, bc_ref, as_ref, os_ref, dy_ref, gn_ref,
             dq_ref, dk_ref, dv_ref, dr_ref, dpre_ref, redg_ref, redb_ref, gs):
        @pl.when(pl.program_id(0) == 0)
        def _():
            gs[...] = jnp.zeros_like(gs)
            redg_ref[...] = jnp.zeros_like(redg_ref)
            redb_ref[...] = jnp.zeros_like(redb_ref)

        consts = _gla_consts()
        umat = (lax.broadcasted_iota(jnp.int32, (CH, CH), 0) <= lax.broadcasted_iota(jnp.int32, (CH, CH), 1)).astype(f32)
        last_row = lax.broadcasted_iota(jnp.int32, (CH, DQK), 0) == CH - 1
        for c in reversed(range(nc)):
            _gla_bwd_chunk(c, consts, umat, last_row, q_ref, k_ref, v_ref, r_ref, la_ref, sp_ref, bc_ref, as_ref, os_ref,
                           dy_ref, gn_ref, dq_ref, dk_ref, dv_ref, dr_ref, dpre_ref, redg_ref, redb_ref, gs)

    rev = lambda col: (lambda i: (nt - 1 - i, col))
    return _pcall(
        body, (z, z, z, z, la, sprev, bc, att, o, dyg, gn_s), name=name, comm=comm,
        grid=(nt,),
        in_specs=[pl.BlockSpec((t, DQK), rev(4)), pl.BlockSpec((t, DQK), rev(5)),
                  pl.BlockSpec((t, DG), rev(3)), pl.BlockSpec((t, DG), rev(4)),
                  pl.BlockSpec((t, DQK), rev(0)), pl.BlockSpec((nc * DQK, DV), rev(0)),
                  pl.BlockSpec((t, DQK), rev(0)), pl.BlockSpec((nc * DQK, CH), rev(0)),
                  pl.BlockSpec((nc * DQK, DV), rev(0)),
                  pl.BlockSpec((t, DG), rev(0)), _full(gn_s.shape)],
        out_specs=[pl.BlockSpec((t, DQK), rev(0)), pl.BlockSpec((t, DQK), rev(0)),
                   pl.BlockSpec((t, DG), rev(0)), pl.BlockSpec((t, DG), rev(0)), pl.BlockSpec((t, DQK), rev(0)),
                   _full((DQK, DV)), _full((CH, DQK))],
        out_shape=[jax.ShapeDtypeStruct((S, DQK), bf16), jax.ShapeDtypeStruct((S, DQK), bf16),
                   jax.ShapeDtypeStruct((S, DG), bf16), jax.ShapeDtypeStruct((S, DG), bf16), jax.ShapeDtypeStruct((S, DQK), bf16),
                   jax.ShapeDtypeStruct((DQK, DV), f32), jax.ShapeDtypeStruct((CH, DQK), f32)],
        scratch_shapes=[pltpu.VMEM((DQK, DV), f32)],
    )


def mixout_fwd(x1, yc, yg, mod, wout, tm, name):
    S = x1.shape[0]

    def body(x_ref, yc_ref, yg_ref, mod_ref, w_ref, xo_ref):
        mixo = _dot(yc_ref[...], w_ref[0:DC, :]) + _dot(yg_ref[...], w_ref[DC:DC + DG, :])
        xo_ref[...] = x_ref[...] + mod_ref[5:6, :] * mixo

    row = lambda i: (i, 0)
    return pl.pallas_call(
        body, name=name,
        grid=(S // tm,),
        in_specs=[pl.BlockSpec((tm, D), row), pl.BlockSpec((tm, DC), row), pl.BlockSpec((tm, DG), row),
                  _full(mod.shape), _full(wout.shape)],
        out_specs=pl.BlockSpec((tm, D), row),
        out_shape=jax.ShapeDtypeStruct((S, D), f32),
        compiler_params=_cp(1),
    )(x1, yc, yg, mod, wout)


def mixout_bwd(dx2, yc, yg, mod, wout, tm, name):
    S = dx2.shape[0]
    nt = S // tm

    def body(dx_ref, yc_ref, yg_ref, mod_ref, w_ref, dyc_ref, dyg_ref, red_ref, dw_ref, acc_s):
        i = pl.program_id(0)

        @pl.when(i == 0)
        def _():
            red_ref[...] = jnp.zeros_like(red_ref)
            acc_s[...] = jnp.zeros_like(acc_s)

        dxv = dx_ref[...]
        ycat = jnp.concatenate([yc_ref[...], yg_ref[...]], axis=1)
        mixo = _dot(ycat, w_ref[...])
        red_ref[0:1, :] += _rowsum(dxv * mixo)
        dm = (mod_ref[5:6, :] * dxv).astype(bf16)
        acc_s[...] += _dg(ycat, dm, TN)
        dycat = _dg(dm, w_ref[...], NT)
        dyc_ref[...] = dycat[:, :DC]
        dyg_ref[...] = dycat[:, DC:]

        @pl.when(i == nt - 1)
        def _():
            dw_ref[...] = acc_s[...].astype(bf16)

    row = lambda i: (i, 0)
    return pl.pallas_call(
        body, name=name,
        grid=(nt,),
        in_specs=[pl.BlockSpec((tm, D), row), pl.BlockSpec((tm, DC), row), pl.BlockSpec((tm, DG), row),
                  _full(mod.shape), _full(wout.shape)],
        out_specs=[pl.BlockSpec((tm, DC), row), pl.BlockSpec((tm, DG), row), _full((8, D)), _full((D, D))],
        out_shape=[jax.ShapeDtypeStruct((S, DC), f32), jax.ShapeDtypeStruct((S, DG), f32),
                   jax.ShapeDtypeStruct((8, D), f32), jax.ShapeDtypeStruct((D, D), bf16)],
        scratch_shapes=[pltpu.VMEM((D, D), f32)],
        compiler_params=_cp(1),
    )(dx2, yc, yg, mod, wout)


def final_fwd_bwd(x, tgt, fmod, g, tm, name):
    S = x.shape[0]

    def body(x_ref, t_ref, fm_ref, g_ref, dx_ref, red_ref):
        @pl.when(pl.program_id(0) == 0)
        def _():
            red_ref[...] = jnp.zeros_like(red_ref)

        xh, rstd = _rms_parts(x_ref[...])
        gv = g_ref[...]
        n = xh * gv
        sc = 1.0 + fm_ref[1:2, :]
        e = n * sc + fm_ref[0:1, :] - t_ref[...]
        red_ref[0:1, :] += _rowsum(e * e) * (0.5 / D)
        dy = e * (1.0 / D)
        dn = dy * sc
        red_ref[1:2, :] += _rowsum(dy)
        red_ref[2:3, :] += _rowsum(dy * n)
        red_ref[3:4, :] += _rowsum(dn * xh)
        dx_ref[...] = _rms_bwd(dn * gv, xh, rstd)

    row = lambda i: (i, 0)
    return pl.pallas_call(
        body, name=name,
        grid=(S // tm,),
        in_specs=[pl.BlockSpec((tm, D), row), pl.BlockSpec((tm, D), row), _full(fmod.shape), _full(g.shape)],
        out_specs=[pl.BlockSpec((tm, D), row), _full((8, D))],
        out_shape=[jax.ShapeDtypeStruct((S, D), f32), jax.ShapeDtypeStruct((8, D), f32)],
        compiler_params=_cp(1),
    )(x, tgt, fmod, g)


def ada_fwd(c_all, w, b, name):
    n = w.shape[1]

    def body(c_ref, w_ref, b_ref, o_ref):
        cv = c_ref[...]
        o_ref[...] = jnp.dot(cv * _sigmoid(cv), w_ref[...], preferred_element_type=f32, precision=HIGHEST) + b_ref[...]

    return pl.pallas_call(
        body, name=name,
        in_specs=[_full(c_all.shape), _full(w.shape), _full(b.shape)],
        out_specs=_full((N_DEV, n)),
        out_shape=jax.ShapeDtypeStruct((N_DEV, n), f32),
        grid=(1,),
        compiler_params=_cp(1),
    )(c_all, w, b)


def ada_wgrad(c_all_t, dm, name):
    n = dm.shape[1]

    def body(c_ref, d_ref, o_ref):
        cv = c_ref[...]
        o_ref[...] = jnp.dot(cv * _sigmoid(cv), d_ref[...], preferred_element_type=f32, precision=HIGHEST)

    return pl.pallas_call(
        body, name=name,
        in_specs=[_full(c_all_t.shape), _full(dm.shape)],
        out_specs=_full((D, n)),
        out_shape=jax.ShapeDtypeStruct((D, n), f32),
        grid=(1,),
        compiler_params=_cp(1),
    )(c_all_t, dm)


def _adam_math(gv, wv, mv, vv):
    m = ADAM_B1 * mv + (1.0 - ADAM_B1) * gv
    v = ADAM_B2 * vv + (1.0 - ADAM_B2) * (gv * gv)
    m_hat = m / (1.0 - ADAM_B1 ** ADAM_STEP)
    v_hat = v / (1.0 - ADAM_B2 ** ADAM_STEP)
    delta = -ADAM_LR * (m_hat / (jnp.sqrt(v_hat) + ADAM_EPS) + ADAM_WD * wv)
    return delta, m, v


def adam_parts(parts, w, m, v, tr, name, comm=None):
    L, R, C = w.shape
    nt = R // tr

    def body(*refs):
        p_refs = refs[:L]
        w_ref, m_ref, v_ref, g_ref, d_ref, mo_ref, vo_ref = refs[L:]
        lyr = pl.program_id(0)
        for l in range(L):
            @pl.when(lyr == l)
            def _(p_ref=p_refs[l]):
                gv = p_ref[0].astype(f32)
                for k in range(1, N_DEV):
                    gv = gv + p_ref[k].astype(f32)
                g_ref[...] = gv
                d_ref[...], mo_ref[...], vo_ref[...] = _adam_math(gv, w_ref[...], m_ref[...], v_ref[...])

    def part_spec(l):
        return pl.BlockSpec((N_DEV, tr, C), lambda lyr, i: (0, jnp.where(lyr == l, i, jnp.where(lyr < l, 0, nt - 1)), 0))

    spec = pl.BlockSpec((None, tr, C), lambda lyr, i: (lyr, i, 0))
    shp = jax.ShapeDtypeStruct((L, R, C), f32)
    return _pcall(
        body, (*parts, w, m, v), name=name, comm=comm,
        grid=(L, nt),
        in_specs=[part_spec(l) for l in range(L)] + [spec, spec, spec],
        out_specs=[spec, spec, spec, spec],
        out_shape=[shp, shp, shp, shp],
    )


def adam_plain(gr, w, m, v, tr, name):
    R, C = w.shape

    def body(g_ref, w_ref, m_ref, v_ref, d_ref, mo_ref, vo_ref):
        d_ref[...], mo_ref[...], vo_ref[...] = _adam_math(g_ref[...], w_ref[...], m_ref[...], v_ref[...])

    spec = pl.BlockSpec((tr, C), lambda i: (i, 0))
    shp = jax.ShapeDtypeStruct((R, C), f32)
    return pl.pallas_call(
        body, name=name,
        grid=(R // tr,),
        in_specs=[spec, spec, spec, spec],
        out_specs=[spec, spec, spec],
        out_shape=[shp, shp, shp],
        compiler_params=_cp(1),
    )(gr, w, m, v)


def sum8(parts, name):
    _, R, C = parts.shape

    def body(p_ref, o_ref):
        acc = p_ref[0]
        for k in range(1, N_DEV):
            acc = acc + p_ref[k]
        o_ref[...] = acc

    return pl.pallas_call(
        body, name=name,
        grid=(1,),
        in_specs=[_full(parts.shape)],
        out_specs=_full((R, C)),
        out_shape=jax.ShapeDtypeStruct((R, C), f32),
        compiler_params=_cp(1),
    )(parts)


def _place():
    return lax.axis_index("x"), lax.axis_index("y"), lax.axis_index("c")


def _gather_steps(ins, outs, send_sems, recv_sems, local_sems, place):
    n = len(ins)
    x, y, c = place
    me, sibling = (x, y, c), (x, y, 1 - c)
    chips = [(1 - x, y), (x, 1 - y), (1 - x, 1 - y)]

    def slot(a, p):
        return outs[a].at[4 * p[0] + 2 * p[1] + p[2]]

    def copy(a, k, block, to, src=None):
        return pltpu.make_async_remote_copy(
            src_ref=slot(a, block) if src is None else src, dst_ref=slot(a, block),
            send_sem=send_sems.at[a * 7 + k], recv_sem=recv_sems.at[a * 7 + k],
            device_id=to, device_id_type=MESH)

    def mine():
        return [pltpu.make_async_copy(ins[a], slot(a, me), local_sems.at[a]) for a in range(n)]

    def first():
        cps = []
        for a in range(n):
            cps.append(copy(a, 0, me, sibling, src=ins[a]))
            cps += [copy(a, 1 + j, me, (*chip, c), src=ins[a]) for j, chip in enumerate(chips)]
        return cps

    def start():
        for cp in mine() + first():
            cp.start()

    def forward():
        for j, chip in enumerate(chips):
            for a in range(n):
                copy(a, 1 + j, (*chip, c), me).wait_recv()
                copy(a, 4 + j, (*chip, c), sibling).start()

    def finish():
        for a in range(n):
            copy(a, 0, sibling, me).wait_recv()
            for j, chip in enumerate(chips):
                copy(a, 4 + j, (*chip, 1 - c), me).wait_recv()
        for cp in first() + [copy(a, 4 + j, (*chip, c), sibling) for j, chip in enumerate(chips) for a in range(n)]:
            cp.wait_send()
        for cp in mine():
            cp.wait()

    return start, forward, finish


def _exchange_steps(ins, outs, send_sems, recv_sems, local_sems, place):
    n = len(ins)
    x, y, c = place
    me_i = 4 * x + 2 * y + c

    def mine():
        return [pltpu.make_async_copy(ins[a].at[me_i], outs[a].at[me_i], local_sems.at[a]) for a in range(n)]

    def copies(receiving):
        cps = []
        for k in range(1, N_DEV):
            px = 1 - x if (k >> 2) & 1 else x
            py = 1 - y if (k >> 1) & 1 else y
            pc = 1 - c if k & 1 else c
            p_i = 4 * px + 2 * py + pc
            for a in range(n):
                sem = a * 7 + k - 1
                cps.append(pltpu.make_async_remote_copy(
                    src_ref=ins[a].at[p_i], dst_ref=outs[a].at[p_i if receiving else me_i],
                    send_sem=send_sems.at[sem], recv_sem=recv_sems.at[sem],
                    device_id=(px, py, pc), device_id_type=MESH))
        return cps

    def start():
        for cp in mine() + copies(False):
            cp.start()

    def finish():
        for cp in copies(True):
            cp.wait_recv()
        for cp in copies(False):
            cp.wait_send()
        for cp in mine():
            cp.wait()

    return start, None, finish


_COMM_STEPS = {"gather": _gather_steps, "exchange": _exchange_steps}


def _comm_out_shapes(kind, arrs):
    if kind == "gather":
        return [jax.ShapeDtypeStruct((N_DEV,) + a.shape, a.dtype) for a in arrs]
    return [jax.ShapeDtypeStruct(a.shape, a.dtype) for a in arrs]


def _comm_sems(n):
    return [pltpu.SemaphoreType.DMA((7 * n,)), pltpu.SemaphoreType.DMA((7 * n,)), pltpu.SemaphoreType.DMA((n,))]


def _pcall(body, args, *, name, grid, in_specs, out_specs, out_shape, scratch_shapes=(), comm=None):
    in_specs, out_specs, out_shape = list(in_specs), list(out_specs), list(out_shape)
    scratch_shapes = list(scratch_shapes)
    cparams = _cp(len(grid))
    if comm is None:
        outs = pl.pallas_call(body, name=name, grid=grid, in_specs=in_specs, out_specs=out_specs, out_shape=out_shape,
                              scratch_shapes=scratch_shapes, compiler_params=cparams)(*args)
        return list(outs), []
    kind, arrs = comm
    nc, n_in, n_out, n_scr = len(arrs), len(in_specs), len(out_specs), len(scratch_shapes)
    total = 1
    for gdim in grid:
        total *= gdim
    forward_step = (total * 3) // 4

    def hosted(*refs):
        core_in, c_in = refs[:n_in], refs[n_in:n_in + nc]
        core_out = refs[n_in + nc:n_in + nc + n_out]
        c_out = refs[n_in + nc + n_out:n_in + 2 * nc + n_out]
        rest = refs[n_in + 2 * nc + n_out:]
        step = pl.program_id(0)
        for ax in range(1, len(grid)):
            step = step * grid[ax] + pl.program_id(ax)
        start, forward, finish = _COMM_STEPS[kind](c_in, c_out, *rest[n_scr:], _place())
        pl.when(step == 0)(start)
        if forward is not None:
            pl.when(step == forward_step)(forward)
        body(*core_in, *core_out, *rest[:n_scr])
        pl.when(step == total - 1)(finish)

    any_spec = pl.BlockSpec(memory_space=pl.ANY)
    outs = pl.pallas_call(
        hosted, name=name, grid=grid,
        in_specs=in_specs + [any_spec] * nc,
        out_specs=out_specs + [any_spec] * nc,
        out_shape=out_shape + _comm_out_shapes(kind, arrs),
        scratch_shapes=scratch_shapes + _comm_sems(nc),
        compiler_params=cparams)(*args, *arrs)
    return list(outs[:n_out]), list(outs[n_out:])


def _comm_call(kind, arrs, name):
    n = len(arrs)

    def body(*refs):
        start, forward, finish = _COMM_STEPS[kind](refs[:n], refs[n:2 * n], *refs[2 * n:], _place())
        start()
        if forward is not None:
            forward()
        finish()

    any_spec = pl.BlockSpec(memory_space=pl.ANY)
    return pl.pallas_call(
        body, name=name,
        in_specs=[any_spec] * n, out_specs=[any_spec] * n,
        out_shape=_comm_out_shapes(kind, arrs), scratch_shapes=_comm_sems(n),
    )(*arrs)


def all_gather(arrs, name):
    return _comm_call("gather", arrs, name)


def all_to_all(arrs, name):
    return _comm_call("exchange", arrs, name)


def _tiles(S):
    t = min(512, S)
    return dict(ffn=min(256, S), row=t, conv=t, gla=t, bk=min(1024, S))


BIG = ("wi1", "wo1", "win", "wout", "wi2", "wo2")


def _col_shards_to_full(gathered):
    n, r, c = gathered.shape
    return jnp.transpose(gathered, (1, 0, 2)).reshape(r, n * c)


def _win_full(win_a):
    return _pad_rows(win_a.reshape(DIN, D), DINP)


def train_pass(x, tgt, mods, fmod, sh, ws, wi1_first, wo1_first):
    S = x.shape[0]
    T = _tiles(S)
    bk = T["bk"]
    full = [dict() for _ in range(DEPTH)]
    hosted = {("ffn1", 0): [("win", 0), ("wout", 0), ("wo2", 0)], ("core", 0): [("wi2", 0)]}
    for l in range(1, DEPTH):
        hosted[("mixin", l - 1)] = [("win", l), ("wout", l)]
        hosted[("ffn2", l - 1)] = [("wi1", l), ("wo1", l)]
        hosted[("ffn1", l)] = [("wi2", l), ("wo2", l)]

    def comm_for(key):
        return ("gather", [sh[n][ll] for n, ll in hosted[key]]) if key in hosted else None

    def keep(key, got):
        for (n, ll), gathered in zip(hosted.get(key, []), got):
            if n in ("wi1", "wi2"):
                full[ll][n] = gathered.reshape(2 * F, D)
            elif n in ("wo1", "wo2"):
                full[ll][n] = gathered.reshape(F, D)
            else:
                full[ll][n] = _win_full(gathered) if n == "win" else gathered.reshape(D, D)

    full[0]["wi1"], full[0]["wo1"] = wi1_first.reshape(2 * F, D), wo1_first.reshape(F, D)
    saved = []
    xc = x
    for l in range(DEPTH):
        w, fw = ws[f"L{l}"], full[l]
        x0 = xc
        (x1, h1f, z1, f1), got = ffn_fwd(x0, mods[l], w["g1"], fw["wi1"], fw["wo1"], (0, 1, 2), T["ffn"], f"ffn1_fwd_{l}",
                                         comm=comm_for(("ffn1", l)))
        keep(("ffn1", l), got)
        (z, la), got = mixin_fwd(x1, mods[l], w["g2"], fw["win"], w["wgu"], w["bgate"], T["row"], f"mixin_fwd_{l}",
                                 comm=comm_for(("mixin", l)))
        keep(("mixin", l), got)
        (y, yc, yg, sprev, bc, att, o), got = mixer_core_fwd(z, la, w["gn_s"], w["wdw"], w["cpar"], T["gla"],
                                                             f"mixer_core_fwd_{l}", comm=comm_for(("core", l)))
        keep(("core", l), got)
        (x3, h2f, z2, f2, x2), got = ffn_fwd(x1, mods[l], w["g3"], fw["wi2"], fw["wo2"], (6, 7, 8), T["ffn"],
                                             f"ffn2_fwd_{l}", comm=comm_for(("ffn2", l)), mix=(yc, yg, fw["wout"]))
        keep(("ffn2", l), got)
        saved.append(dict(x0=x0, x1=x1, x2=x2, h1f=h1f, z1=z1, f1=f1, h2f=h2f, z2=z2, f2=f2,
                          z=z, la=la, y=y, yc=yc, yg=yg, sprev=sprev, bc=bc, att=att, o=o))
        xc = x3

    dx, redf = final_fwd_bwd(xc, tgt, fmod, ws["gf"], T["row"], "loss_head")
    loss_lanes = redf[0]
    dfmod = redf[1:3]
    grads = {"gf": redf[3]}
    dmods = [None] * DEPTH
    recv = {n: [None] * DEPTH for n in BIG}

    def ffn_backward(xin, dy, h, z, fo, gain, wi_t, wo, rows, l, tag, ride=None):
        (dz, p_wo), got_ride = ffn_bwd_hidden(dy, z, mods[l], wo.T, rows[2], T["ffn"], f"{tag}_bwd_hidden_{l}",
                                              comm=("exchange", ride) if ride else None)
        p_wi, (r_wo,) = dwi_pieces(h, dz, T["bk"], f"d{tag}_wi_{l}",
                                   comm=("exchange", [p_wo.reshape(N_DEV, F // N_DEV, D)]))
        (dxin, red), (r_wi,) = ffn_bwd_input(xin, dy, dz, fo, mods[l], gain, wi_t, rows, T["row"],
                                             f"{tag}_bwd_input_{l}", comm=("exchange", [p_wi]))
        return dxin, red, r_wi, r_wo, got_ride

    for l in reversed(range(DEPTH)):
        w, fw, sv = ws[f"L{l}"], full[l], saved[l]
        g = {}
        dx2, red3, recv["wi2"][l], recv["wo2"][l], _ = ffn_backward(
            sv["x2"], dx, sv["h2f"], sv["z2"], sv["f2"], w["g3"], fw["wi2"], fw["wo2"], (6, 7, 8), l, "ffn2")
        dyc, dyg, red_o, p_wout = mixout_bwd(dx2, sv["yc"], sv["yg"], mods[l], fw["wout"], T["row"], f"mixout_bwd_{l}")
        (dq, dk, dv, dr, dpre, redg, redb), (recv["wout"][l],) = gla_bwd(
            sv["z"], sv["la"], sv["sprev"], sv["bc"], sv["att"], sv["o"], dyg, w["gn_s"], T["gla"], f"gla_bwd_{l}",
            comm=("exchange", [p_wout.reshape(N_DEV, D // N_DEV, D)]))
        dzab, redc = conv_bwd(sv["z"], sv["y"], dyc, w["wdw"], w["cpar"], T["conv"], f"conv_bwd_{l}")
        (dx1, red2, dwin_t), _ = mixin_bwd(sv["x1"], dx2, dzab, dq, dk, dv, dr, dpre, mods[l], w["g2"], fw["win"], w["wgu"],
                                           T["ffn"], f"mixin_bwd_{l}")
        p_win = dwin_t[:DIN].reshape(N_DEV, DIN // N_DEV, D)
        g["wgu"] = matmul_tn(sv["z"], dpre, 128, DQK, 128, DQK, bk, f"dwgu_{l}", a_col_block=(DINP - 128) // 128)[:GR]
        g["bgate"] = jnp.sum(redb, axis=0)
        g["gn"] = jnp.sum(redg.reshape(NH, CH, DV), axis=1)
        g["wdw"] = redc[:CW]
        g["bdw"], g["gln"], g["bln"] = redc[32], redc[33], redc[34]
        dx0, red1, recv["wi1"][l], recv["wo1"][l], (recv["win"][l],) = ffn_backward(
            sv["x0"], dx1, sv["h1f"], sv["z1"], sv["f1"], w["g1"], fw["wi1"], fw["wo1"], (0, 1, 2), l, "ffn1",
            ride=[p_win])
        g["g1"], g["g2"], g["g3"] = red1[3], red2[2], red3[3]
        dmods[l] = jnp.stack([red1[0], red1[1], red1[2], red2[0], red2[1], red_o[0], red3[0], red3[1], red3[2]], axis=0)
        grads[f"L{l}"] = g
        dx = dx0
    return loss_lanes, dx, grads, dmods, dfmod, recv


def _pad_rows(a, rows):
    return jnp.pad(a, ((0, rows - a.shape[0]), (0, 0)))


def kernel(x, c, w_ada, b_ada, g_norm_ffn1, w_ffn1_in, w_ffn1_out, g_norm_mix, w_in, w_dw, b_dw, g_conv_ln, b_conv_ln, w_gate_up, b_gate, g_gla_norm, w_out, g_norm_ffn2, w_ffn2_in, w_ffn2_out, g_norm_final, w_ada_final, b_ada_final, loss_target, m_w_ada, m_b_ada, m_g_norm_ffn1, m_w_ffn1_in, m_w_ffn1_out, m_g_norm_mix, m_w_in, m_w_dw, m_b_dw, m_g_conv_ln, m_b_conv_ln, m_w_gate_up, m_b_gate, m_g_gla_norm, m_w_out, m_g_norm_ffn2, m_w_ffn2_in, m_w_ffn2_out, m_g_norm_final, m_w_ada_final, m_b_ada_final, v_w_ada, v_b_ada, v_g_norm_ffn1, v_w_ffn1_in, v_w_ffn1_out, v_g_norm_mix, v_w_in, v_w_dw, v_b_dw, v_g_conv_ln, v_b_conv_ln, v_w_gate_up, v_b_gate, v_g_gla_norm, v_w_out, v_g_norm_ffn2, v_w_ffn2_in, v_w_ffn2_out, v_g_norm_final, v_w_ada_final, v_b_ada_final):
    me = 4 * lax.axis_index("x") + 2 * lax.axis_index("y") + lax.axis_index("c")
    L = DEPTH
    n_ada = N_MOD * D // N_DEV
    n_fin = 2 * D // N_DEV

    small = jnp.concatenate([c.reshape(-1), w_dw.reshape(-1), w_gate_up.reshape(-1)])
    n_small = small.shape[0]
    small = jnp.pad(small, (0, 8 * D - n_small)).reshape(8, D)
    big = dict(wi1=w_ffn1_in, wo1=w_ffn1_out, win=w_in, wout=w_out, wi2=w_ffn2_in, wo2=w_ffn2_out)
    transposed = ("wi1", "wi2", "win")
    sh = {n: [(a[l].T if n in transposed else a[l]).astype(bf16) for l in range(L)] for n, a in big.items()}
    small_a, wi1_first, wo1_first = all_gather([small, sh["wi1"][0], sh["wo1"][0]], "gather_first")
    small_a = small_a.reshape(N_DEV, 8 * D)
    c_all = small_a[:, :D]
    o1 = D + L * CW * (DC // N_DEV)
    wdw_full = _col_shards_to_full(small_a[:, D:o1].reshape(N_DEV, L * CW, DC // N_DEV)).reshape(L, CW, DC)
    wgu_full = _col_shards_to_full(small_a[:, o1:o1 + L * GR * (DQK // N_DEV)].reshape(N_DEV, L * GR, DQK // N_DEV)).reshape(L, GR, DQK)

    b_ada_mine = lax.dynamic_slice(b_ada, (0, me * n_ada), (L, n_ada))
    b_fin_mine = lax.dynamic_slice(b_ada_final, (me * n_fin,), (n_fin,))
    parts = [ada_fwd(c_all, w_ada[l], b_ada_mine[l:l + 1], f"ada_fwd_{l}") for l in range(L)]
    parts.append(ada_fwd(c_all, w_ada_final, b_fin_mine.reshape(1, n_fin), "ada_fwd_final"))
    modsrc = jnp.concatenate(parts, axis=1)
    n_row = modsrc.shape[1]
    modsrc = jnp.pad(modsrc, ((0, 0), (0, 24 * 128 - n_row))).reshape(N_DEV, 24, 128)
    (modrecv,) = all_to_all([modsrc], "exchange_mod")
    modrecv = modrecv.reshape(N_DEV, 24 * 128)
    mods = []
    for l in range(L):
        mvec = modrecv[:, l * n_ada:(l + 1) * n_ada].reshape(N_MOD, D)
        mods.append(_pad_rows(mvec, 16))
    fmod = _pad_rows(modrecv[:, L * n_ada:L * n_ada + n_fin].reshape(2, D), 8)

    ws = {"gf": g_norm_final.reshape(1, D)}
    for l in range(L):
        ws[f"L{l}"] = dict(
            g1=g_norm_ffn1[l].reshape(1, D), g2=g_norm_mix[l].reshape(1, D), g3=g_norm_ffn2[l].reshape(1, D),
            wgu=_pad_rows(wgu_full[l], 128).astype(bf16),
            bgate=b_gate[l].reshape(1, DQK),
            wdw=_pad_rows(wdw_full[l], 32),
            cpar=_pad_rows(jnp.stack([b_dw[l], g_conv_ln[l], b_conv_ln[l]]), 8),
            gn_s=jnp.repeat(g_gla_norm[l], CH, axis=0),
        )

    loss_lanes, grad_x, gr, dmods, dfmod, recv = train_pass(
        x[0], loss_target[0], mods, fmod, sh, ws, wi1_first, wo1_first)

    def adam_big(rv, w, m, v, name, is_transposed=False):
        if is_transposed:
            w, m, v = (jnp.swapaxes(a, 1, 2) for a in (w, m, v))
        R = w.shape[1]
        tr = 256 if R % 256 == 0 else (R // 2 if (R // 2) % 16 == 0 else R)
        outs, _ = adam_parts(rv, w, m, v, tr, name)
        return [jnp.swapaxes(o, 1, 2) for o in outs] if is_transposed else outs

    res = {}
    res["w_ffn2_in"] = adam_big(recv["wi2"], w_ffn2_in, m_w_ffn2_in, v_w_ffn2_in, "adam_ffn2_in", True)
    res["w_ffn2_out"] = adam_big(recv["wo2"], w_ffn2_out, m_w_ffn2_out, v_w_ffn2_out, "adam_ffn2_out")
    res["w_in"] = adam_big(recv["win"], w_in, m_w_in, v_w_in, "adam_w_in", True)
    res["w_out"] = adam_big(recv["wout"], w_out, m_w_out, v_w_out, "adam_w_out")
    res["w_ffn1_out"] = adam_big(recv["wo1"], w_ffn1_out, m_w_ffn1_out, v_w_ffn1_out, "adam_ffn1_out")
    res["w_ffn1_in"] = adam_big(recv["wi1"], w_ffn1_in, m_w_ffn1_in, v_w_ffn1_in, "adam_ffn1_in", True)

    flat = lambda name: jnp.stack([gr[f"L{l}"][name] for l in range(L)]).reshape(-1)
    sections = [
        ("b_ada", jnp.stack(dmods).reshape(-1)), ("b_ada_final", dfmod.reshape(-1)),
        ("g_norm_ffn1", flat("g1")), ("g_norm_mix", flat("g2")), ("g_norm_ffn2", flat("g3")), ("g_norm_final", gr["gf"]),
        ("b_dw", flat("bdw")), ("g_conv_ln", flat("gln")), ("b_conv_ln", flat("bln")), ("b_gate", flat("bgate")),
        ("g_gla_norm", flat("gn")),
    ]
    n_rep = sum(s[1].shape[0] for s in sections)
    rep_rows = -(-n_rep // D)
    extra = [("loss", loss_lanes), ("w_dw", flat("wdw")), ("w_gate_up", flat("wgu"))]
    pack = jnp.concatenate([s[1] for s in sections] + [jnp.zeros((rep_rows * D - n_rep,), f32)] + [s[1] for s in extra])
    n_pack = pack.shape[0]
    pack_rows = -(-n_pack // (8 * D)) * 8
    pack = jnp.pad(pack, (0, pack_rows * D - n_pack)).reshape(pack_rows, D)
    (pack_all,) = all_gather([pack], "gather_small_grads")
    tot = sum8(pack_all, "sum_small_grads")
    tot_flat = tot.reshape(-1)
    loss = jnp.sum(tot_flat[rep_rows * D:rep_rows * D + D])
    o_dw = rep_rows * D + D
    g_wdw_full = tot_flat[o_dw:o_dw + L * CW * DC].reshape(L, CW, DC)
    o_gu = o_dw + L * CW * DC
    g_wgu_full = tot_flat[o_gu:o_gu + L * GR * DQK].reshape(L, GR, DQK)

    small_params = dict(b_ada=(b_ada, m_b_ada, v_b_ada), b_ada_final=(b_ada_final, m_b_ada_final, v_b_ada_final),
                        g_norm_ffn1=(g_norm_ffn1, m_g_norm_ffn1, v_g_norm_ffn1), g_norm_mix=(g_norm_mix, m_g_norm_mix, v_g_norm_mix),
                        g_norm_ffn2=(g_norm_ffn2, m_g_norm_ffn2, v_g_norm_ffn2), g_norm_final=(g_norm_final, m_g_norm_final, v_g_norm_final),
                        b_dw=(b_dw, m_b_dw, v_b_dw), g_conv_ln=(g_conv_ln, m_g_conv_ln, v_g_conv_ln),
                        b_conv_ln=(b_conv_ln, m_b_conv_ln, v_b_conv_ln), b_gate=(b_gate, m_b_gate, v_b_gate),
                        g_gla_norm=(g_gla_norm, m_g_gla_norm, v_g_gla_norm))

    def rep_pack(idx):
        p = jnp.concatenate([small_params[s[0]][idx].reshape(-1) for s in sections])
        return jnp.pad(p, (0, rep_rows * D - n_rep)).reshape(rep_rows, D)

    g_rep = tot[:rep_rows]
    d_rep, m_rep, v_rep = adam_plain(g_rep, rep_pack(0), rep_pack(1), rep_pack(2), rep_rows, "adam_small")
    off = 0
    for sname, sval in sections:
        shp = small_params[sname][0].shape
        nel = sval.shape[0]
        res[sname] = [a.reshape(-1)[off:off + nel].reshape(shp) for a in (g_rep, d_rep, m_rep, v_rep)]
        off += nel

    def adam_cols(g_full, w, m, v, name):
        shp = w.shape
        g_mine = lax.dynamic_slice(g_full, (0, 0, me * shp[2]), shp)
        R, C = shp[0] * shp[1], shp[2]
        outs = adam_plain(g_mine.reshape(R, C), w.reshape(R, C), m.reshape(R, C), v.reshape(R, C), R, name)
        return [g_mine] + [o.reshape(shp) for o in outs]

    res["w_dw"] = adam_cols(g_wdw_full, w_dw, m_w_dw, v_w_dw, "adam_w_dw")
    res["w_gate_up"] = adam_cols(g_wgu_full, w_gate_up, m_w_gate_up, v_w_gate_up, "adam_w_gate_up")

    c_all_t = c_all.T
    dmod_all = pack_all.reshape(N_DEV, -1)[:, :L * N_MOD * D].reshape(N_DEV, L, N_MOD * D)
    dfm_all = pack_all.reshape(N_DEV, -1)[:, L * N_MOD * D:L * N_MOD * D + 2 * D]
    dm_mine = lax.dynamic_slice(dmod_all, (0, 0, me * n_ada), (N_DEV, L, n_ada))
    dfm_mine = lax.dynamic_slice(dfm_all, (0, me * n_fin), (N_DEV, n_fin))
    g_w_ada = jnp.stack([ada_wgrad(c_all_t, dm_mine[:, l], f"ada_wgrad_{l}") for l in range(L)])
    g_w_fin = ada_wgrad(c_all_t, dfm_mine, "ada_wgrad_final")
    outs = adam_plain(g_w_ada.reshape(L * D, n_ada), w_ada.reshape(L * D, n_ada), m_w_ada.reshape(L * D, n_ada),
                      v_w_ada.reshape(L * D, n_ada), 256, "adam_w_ada")
    res["w_ada"] = [g_w_ada] + [o.reshape(w_ada.shape) for o in outs]
    res["w_ada_final"] = [g_w_fin] + list(adam_plain(g_w_fin, w_ada_final, m_w_ada_final, v_w_ada_final, 256, "adam_w_ada_final"))

    order = ["w_ada", "b_ada", "g_norm_ffn1", "w_ffn1_in", "w_ffn1_out", "g_norm_mix", "w_in", "w_dw", "b_dw", "g_conv_ln",
             "b_conv_ln", "w_gate_up", "b_gate", "g_gla_norm", "w_out", "g_norm_ffn2", "w_ffn2_in", "w_ffn2_out",
             "g_norm_final", "w_ada_final", "b_ada_final"]
    out = [loss, grad_x[None]]
    for k in range(4):
        out += [res[name][k] for name in order]
    return tuple(out)
```
